```python
import math, functools
import jax, jax.numpy as jnp
from jax import lax
import numpy as np

D_MODEL = 1024
BATCH = 8
SEQ = 8192
DEPTH = 4

N_MIXERS = 3
N_LAYERS_CONV = len(range(0, DEPTH, N_MIXERS))
N_LAYERS_FOX = len(range(1, DEPTH, N_MIXERS))
N_LAYERS_SSD = len(range(2, DEPTH, N_MIXERS))

RMS_EPS = 1e-6

D_FF = -(-8 * D_MODEL // (3 * 256)) * 256

CONV_WIDTH = 3

ATTN_HEAD_DIM = 64
ATTN_HEADS = D_MODEL // ATTN_HEAD_DIM
ATTN_WIDTH = ATTN_HEADS * ATTN_HEAD_DIM
Q_BLOCK = 128
FOX_IN = 3 * ATTN_WIDTH + ATTN_HEADS

SSM_EXPAND = 2
SSM_D_INNER = SSM_EXPAND * D_MODEL
SSM_HEAD_DIM = 64
SSM_HEADS = SSM_D_INNER // SSM_HEAD_DIM
SSM_GROUPS = 8
SSM_HEADS_PER_GROUP = SSM_HEADS // SSM_GROUPS
SSM_STATE = 128
SSM_CONV = 4
SSM_CHUNK = 128
SSM_CONV_DIM = SSM_D_INNER + 2 * SSM_GROUPS * SSM_STATE
SSM_IN = SSM_D_INNER + SSM_CONV_DIM + SSM_HEADS

kernel_name = "hybrid_conv_fox_ssd_trunk"


def rms_norm(x, w, eps=RMS_EPS):
    xf = x.astype(jnp.float32)
    y = xf * lax.rsqrt(jnp.mean(xf * xf, axis=-1, keepdims=True) + eps)
    return (y * w.astype(jnp.float32)).astype(x.dtype)


def causal_depthwise_conv(u, w):
    k_w = w.shape[0]
    seq = u.shape[1]
    up = jnp.pad(u, ((0, 0), (k_w - 1, 0), (0, 0)))
    out = up[:, 0:seq] * w[0]
    for k in range(1, k_w):
        out = out + up[:, k:k + seq] * w[k]
    return out


def swiglu_ffn(h, w_gu, w_down):
    g, u = jnp.split(h @ w_gu, 2, axis=-1)
    return (jax.nn.silu(g) * u) @ w_down


def short_conv_mixer(h, w_in, conv_w, w_out):
    b_gate, c_gate, v = jnp.split(h @ w_in, 3, axis=-1)
    u = causal_depthwise_conv(c_gate * v, conv_w)
    return (b_gate * u) @ w_out


def forgetting_attention(h, w_in, b_f, q_gain, k_gain, w_out):
    bsz, seq, _ = h.shape
    proj = h @ w_in
    q, k, v, f_logit = jnp.split(proj, [ATTN_WIDTH, 2 * ATTN_WIDTH, 3 * ATTN_WIDTH], axis=-1)
    q = rms_norm(q.reshape(bsz, seq, ATTN_HEADS, ATTN_HEAD_DIM), q_gain).astype(jnp.float32)
    k = rms_norm(k.reshape(bsz, seq, ATTN_HEADS, ATTN_HEAD_DIM), k_gain).astype(jnp.float32)
    v = v.reshape(bsz, seq, ATTN_HEADS, ATTN_HEAD_DIM).astype(jnp.float32)
    log_f = jax.nn.log_sigmoid((f_logit + b_f).astype(jnp.float32))
    cum = jnp.cumsum(log_f, axis=1)
    cum_k = jnp.transpose(cum, (0, 2, 1))[:, :, None, :]
    n_blk = seq // Q_BLOCK
    q_blk = jnp.moveaxis(q.reshape(bsz, n_blk, Q_BLOCK, ATTN_HEADS, ATTN_HEAD_DIM), 1, 0)
    cum_q = jnp.moveaxis(cum.reshape(bsz, n_blk, Q_BLOCK, ATTN_HEADS), 1, 0)
    key_pos = jnp.arange(seq)
    scale = ATTN_HEAD_DIM ** -0.5

    def attend(args):
        qb, cq, bi = args
        logits = jnp.einsum('bqhd,bkhd->bhqk', qb, k) * scale
        logits = logits + jnp.transpose(cq, (0, 2, 1))[..., None] - cum_k
        q_pos = bi * Q_BLOCK + jnp.arange(Q_BLOCK)
        causal = q_pos[:, None] >= key_pos[None, :]
        logits = jnp.where(causal, logits, -jnp.inf)
        p = jax.nn.softmax(logits, axis=-1)
        return jnp.einsum('bhqk,bkhd->bqhd', p, v)

    out = lax.map(attend, (q_blk, cum_q, jnp.arange(n_blk)))
    out = jnp.moveaxis(out, 0, 1).reshape(bsz, seq, ATTN_WIDTH).astype(h.dtype)
    return out @ w_out


def ssd_chunked(xs, dt, a, b_m, c_m):
    bsz, seq = xs.shape[:2]
    nc = seq // SSM_CHUNK
    L, G, K, P, N = SSM_CHUNK, SSM_GROUPS, SSM_HEADS_PER_GROUP, SSM_HEAD_DIM, SSM_STATE
    x = xs.reshape(bsz, nc, L, G, K, P)
    dtc = dt.reshape(bsz, nc, L, G, K)
    bc = b_m.reshape(bsz, nc, L, G, N)
    cc = c_m.reshape(bsz, nc, L, G, N)
    acum = jnp.cumsum(dtc * a.reshape(G, K), axis=2)
    seg = acum[:, :, :, None] - acum[:, :, None]
    mask = jnp.tril(jnp.ones((L, L), dtype=bool))[:, :, None, None]
    decay = jnp.exp(jnp.where(mask, seg, -jnp.inf))
    cb = jnp.einsum('bclgn,bcsgn->bclsg', cc, bc)
    w = cb[..., None] * decay * dtc[:, :, None]
    y_diag = jnp.einsum('bclsgk,bcsgkp->bclgkp', w, x)
    decay_states = jnp.exp(acum[:, :, -1:] - acum)
    states = jnp.einsum('bclgn,bclgk,bclgkp->bcgkpn', bc, decay_states * dtc, x)
    chunk_decay = jnp.exp(acum[:, :, -1])

    def step(hst, inp):
        st, dec = inp
        return dec[..., None, None] * hst + st, hst

    h0 = jnp.zeros((bsz, G, K, P, N), jnp.float32)
    _, prev = lax.scan(step, h0, (jnp.moveaxis(states, 1, 0), jnp.moveaxis(chunk_decay, 1, 0)))
    prev = jnp.moveaxis(prev, 0, 1)
    y_off = jnp.einsum('bclgn,bcgkpn,bclgk->bclgkp', cc, prev, jnp.exp(acum))
    return (y_diag + y_off).reshape(bsz, seq, SSM_HEADS, P)


def mamba2_mixer(h, w_in, conv_w, conv_b, dt_bias, a_log, d_skip, norm_w, w_out):
    bsz, seq, _ = h.shape
    proj = h @ w_in
    z, xbc, dt = jnp.split(proj, [SSM_D_INNER, SSM_D_INNER + SSM_CONV_DIM], axis=-1)
    xbc = jax.nn.silu(causal_depthwise_conv(xbc, conv_w) + conv_b)
    xs, b_m, c_m = jnp.split(xbc, [SSM_D_INNER, SSM_D_INNER + SSM_GROUPS * SSM_STATE], axis=-1)
    xs = xs.reshape(bsz, seq, SSM_HEADS, SSM_HEAD_DIM).astype(jnp.float32)
    b_m = b_m.reshape(bsz, seq, SSM_GROUPS, SSM_STATE).astype(jnp.float32)
    c_m = c_m.reshape(bsz, seq, SSM_GROUPS, SSM_STATE).astype(jnp.float32)
    dt = jax.nn.softplus((dt + dt_bias).astype(jnp.float32))
    a = -jnp.exp(a_log.astype(jnp.float32))
    y = ssd_chunked(xs, dt, a, b_m, c_m) + d_skip.astype(jnp.float32)[:, None] * xs
    y = y.reshape(bsz, seq, SSM_D_INNER) * jax.nn.silu(z.astype(jnp.float32))
    yg = y.reshape(bsz, seq, SSM_GROUPS, SSM_D_INNER // SSM_GROUPS)
    yg = yg * lax.rsqrt(jnp.mean(yg * yg, axis=-1, keepdims=True) + RMS_EPS)
    y = (yg.reshape(bsz, seq, SSM_D_INNER) * norm_w.astype(jnp.float32)).astype(h.dtype)
    return y @ w_out


def _fwd_setup_inputs(seed: int = 0) -> dict:
    key = jax.random.key(seed)
    ks = iter(jax.random.split(key, 32))
    f32 = jnp.float32

    def normal(shape, scale):
        return jax.random.normal(next(ks), shape, f32) * scale

    def gains(shape):
        return 1.0 + normal(shape, 0.02)

    out_scale = (2.0 * DEPTH) ** -0.5
    nC, nF, nS = N_LAYERS_CONV, N_LAYERS_FOX, N_LAYERS_SSD
    dt0 = jnp.exp(jax.random.uniform(next(ks), (nS, SSM_HEADS), f32, math.log(1e-3), math.log(1e-1)))
    return {
        "x": normal((BATCH, SEQ, D_MODEL), 1.0),
        "mix_norm": gains((DEPTH, D_MODEL)),
        "ffn_norm": gains((DEPTH, D_MODEL)),
        "ffn_w_gu": normal((DEPTH, D_MODEL, 2 * D_FF), D_MODEL ** -0.5),
        "ffn_w_down": normal((DEPTH, D_FF, D_MODEL), D_FF ** -0.5 * out_scale),
        "conv_w_in": normal((nC, D_MODEL, 3 * D_MODEL), D_MODEL ** -0.5),
        "conv_w_dw": normal((nC, CONV_WIDTH, D_MODEL), CONV_WIDTH ** -0.5),
        "conv_w_out": normal((nC, D_MODEL, D_MODEL), D_MODEL ** -0.5 * out_scale),
        "fox_w_in": normal((nF, D_MODEL, FOX_IN), D_MODEL ** -0.5),
        "fox_b_f": 2.0 + normal((nF, ATTN_HEADS), 0.5),
        "fox_q_gain": gains((nF, ATTN_HEAD_DIM)),
        "fox_k_gain": gains((nF, ATTN_HEAD_DIM)),
        "fox_w_out": normal((nF, ATTN_WIDTH, D_MODEL), ATTN_WIDTH ** -0.5 * out_scale),
        "ssd_w_in": normal((nS, D_MODEL, SSM_IN), D_MODEL ** -0.5),
        "ssd_conv_w": normal((nS, SSM_CONV, SSM_CONV_DIM), SSM_CONV ** -0.5),
        "ssd_conv_b": normal((nS, SSM_CONV_DIM), 0.02),
        "ssd_dt_bias": dt0 + jnp.log(-jnp.expm1(-dt0)),
        "ssd_a_log": jnp.log(jax.random.uniform(next(ks), (nS, SSM_HEADS), f32, 1.0, 16.0)),
        "ssd_d": gains((nS, SSM_HEADS)),
        "ssd_norm_w": gains((nS, SSM_D_INNER)),
        "ssd_w_out": normal((nS, SSM_D_INNER, D_MODEL), SSM_D_INNER ** -0.5 * out_scale),
    }


def _fwd_reference(x, mix_norm, ffn_norm, ffn_w_gu, ffn_w_down,
              conv_w_in, conv_w_dw, conv_w_out,
              fox_w_in, fox_b_f, fox_q_gain, fox_k_gain, fox_w_out,
              ssd_w_in, ssd_conv_w, ssd_conv_b, ssd_dt_bias, ssd_a_log, ssd_d, ssd_norm_w, ssd_w_out):
    for i in range(DEPTH):
        kind, j = i % N_MIXERS, i // N_MIXERS
        h = rms_norm(x, mix_norm[i])
        if kind == 0:
            m = short_conv_mixer(h, conv_w_in[j], conv_w_dw[j], conv_w_out[j])
        elif kind == 1:
            m = forgetting_attention(h, fox_w_in[j], fox_b_f[j], fox_q_gain[j], fox_k_gain[j], fox_w_out[j])
        else:
            m = mamba2_mixer(h, ssd_w_in[j], ssd_conv_w[j], ssd_conv_b[j], ssd_dt_bias[j],
                             ssd_a_log[j], ssd_d[j], ssd_norm_w[j], ssd_w_out[j])
        x = x + m
        x = x + swiglu_ffn(rms_norm(x, ffn_norm[i]), ffn_w_gu[i], ffn_w_down[i])
    return x


import jax as _jax
import jax.numpy as _jnp

TWIN_FORMAT = 'train_step'
FWD_PARAMS = ['x', 'mix_norm', 'ffn_norm', 'ffn_w_gu', 'ffn_w_down', 'conv_w_in', 'conv_w_dw', 'conv_w_out', 'fox_w_in', 'fox_b_f', 'fox_q_gain', 'fox_k_gain', 'fox_w_out', 'ssd_w_in', 'ssd_conv_w', 'ssd_conv_b', 'ssd_dt_bias', 'ssd_a_log', 'ssd_d', 'ssd_norm_w', 'ssd_w_out']
TWIN_WEIGHTS = ['mix_norm', 'ffn_norm', 'ffn_w_gu', 'ffn_w_down', 'conv_w_in', 'conv_w_dw', 'conv_w_out', 'fox_w_in', 'fox_b_f', 'fox_q_gain', 'fox_k_gain', 'fox_w_out', 'ssd_w_in', 'ssd_conv_w', 'ssd_conv_b', 'ssd_dt_bias', 'ssd_a_log', 'ssd_d', 'ssd_norm_w', 'ssd_w_out']
TWIN_DIFF_INPUT = 'x'
TWIN_INPUTS = ['x', 'mix_norm', 'ffn_norm', 'ffn_w_gu', 'ffn_w_down', 'conv_w_in', 'conv_w_dw', 'conv_w_out', 'fox_w_in', 'fox_b_f', 'fox_q_gain', 'fox_k_gain', 'fox_w_out', 'ssd_w_in', 'ssd_conv_w', 'ssd_conv_b', 'ssd_dt_bias', 'ssd_a_log', 'ssd_d', 'ssd_norm_w', 'ssd_w_out', 'loss_target', 'm_mix_norm', 'm_ffn_norm', 'm_ffn_w_gu', 'm_ffn_w_down', 'm_conv_w_in', 'm_conv_w_dw', 'm_conv_w_out', 'm_fox_w_in', 'm_fox_b_f', 'm_fox_q_gain', 'm_fox_k_gain', 'm_fox_w_out', 'm_ssd_w_in', 'm_ssd_conv_w', 'm_ssd_conv_b', 'm_ssd_dt_bias', 'm_ssd_a_log', 'm_ssd_d', 'm_ssd_norm_w', 'm_ssd_w_out', 'v_mix_norm', 'v_ffn_norm', 'v_ffn_w_gu', 'v_ffn_w_down', 'v_conv_w_in', 'v_conv_w_dw', 'v_conv_w_out', 'v_fox_w_in', 'v_fox_b_f', 'v_fox_q_gain', 'v_fox_k_gain', 'v_fox_w_out', 'v_ssd_w_in', 'v_ssd_conv_w', 'v_ssd_conv_b', 'v_ssd_dt_bias', 'v_ssd_a_log', 'v_ssd_d', 'v_ssd_norm_w', 'v_ssd_w_out']
TWIN_OUTPUTS = ['loss', 'grad_x', 'grad_mix_norm', 'grad_ffn_norm', 'grad_ffn_w_gu', 'grad_ffn_w_down', 'grad_conv_w_in', 'grad_conv_w_dw', 'grad_conv_w_out', 'grad_fox_w_in', 'grad_fox_b_f', 'grad_fox_q_gain', 'grad_fox_k_gain', 'grad_fox_w_out', 'grad_ssd_w_in', 'grad_ssd_conv_w', 'grad_ssd_conv_b', 'grad_ssd_dt_bias', 'grad_ssd_a_log', 'grad_ssd_d', 'grad_ssd_norm_w', 'grad_ssd_w_out', 'delta_mix_norm', 'delta_ffn_norm', 'delta_ffn_w_gu', 'delta_ffn_w_down', 'delta_conv_w_in', 'delta_conv_w_dw', 'delta_conv_w_out', 'delta_fox_w_in', 'delta_fox_b_f', 'delta_fox_q_gain', 'delta_fox_k_gain', 'delta_fox_w_out', 'delta_ssd_w_in', 'delta_ssd_conv_w', 'delta_ssd_conv_b', 'delta_ssd_dt_bias', 'delta_ssd_a_log', 'delta_ssd_d', 'delta_ssd_norm_w', 'delta_ssd_w_out', 'new_m_mix_norm', 'new_m_ffn_norm', 'new_m_ffn_w_gu', 'new_m_ffn_w_down', 'new_m_conv_w_in', 'new_m_conv_w_dw', 'new_m_conv_w_out', 'new_m_fox_w_in', 'new_m_fox_b_f', 'new_m_fox_q_gain', 'new_m_fox_k_gain', 'new_m_fox_w_out', 'new_m_ssd_w_in', 'new_m_ssd_conv_w', 'new_m_ssd_conv_b', 'new_m_ssd_dt_bias', 'new_m_ssd_a_log', 'new_m_ssd_d', 'new_m_ssd_norm_w', 'new_m_ssd_w_out', 'new_v_mix_norm', 'new_v_ffn_norm', 'new_v_ffn_w_gu', 'new_v_ffn_w_down', 'new_v_conv_w_in', 'new_v_conv_w_dw', 'new_v_conv_w_out', 'new_v_fox_w_in', 'new_v_fox_b_f', 'new_v_fox_q_gain', 'new_v_fox_k_gain', 'new_v_fox_w_out', 'new_v_ssd_w_in', 'new_v_ssd_conv_w', 'new_v_ssd_conv_b', 'new_v_ssd_dt_bias', 'new_v_ssd_a_log', 'new_v_ssd_d', 'new_v_ssd_norm_w', 'new_v_ssd_w_out']
TWIN_LEAF_KINDS = {'loss': 'loss', 'grad_x': 'grad_x', 'grad_mix_norm': 'grad_w', 'grad_ffn_norm': 'grad_w', 'grad_ffn_w_gu': 'grad_w', 'grad_ffn_w_down': 'grad_w', 'grad_conv_w_in': 'grad_w', 'grad_conv_w_dw': 'grad_w', 'grad_conv_w_out': 'grad_w', 'grad_fox_w_in': 'grad_w', 'grad_fox_b_f': 'grad_w', 'grad_fox_q_gain': 'grad_w', 'grad_fox_k_gain': 'grad_w', 'grad_fox_w_out': 'grad_w', 'grad_ssd_w_in': 'grad_w', 'grad_ssd_conv_w': 'grad_w', 'grad_ssd_conv_b': 'grad_w', 'grad_ssd_dt_bias': 'grad_w', 'grad_ssd_a_log': 'grad_w', 'grad_ssd_d': 'grad_w', 'grad_ssd_norm_w': 'grad_w', 'grad_ssd_w_out': 'grad_w', 'delta_mix_norm': 'delta_w', 'delta_ffn_norm': 'delta_w', 'delta_ffn_w_gu': 'delta_w', 'delta_ffn_w_down': 'delta_w', 'delta_conv_w_in': 'delta_w', 'delta_conv_w_dw': 'delta_w', 'delta_conv_w_out': 'delta_w', 'delta_fox_w_in': 'delta_w', 'delta_fox_b_f': 'delta_w', 'delta_fox_q_gain': 'delta_w', 'delta_fox_k_gain': 'delta_w', 'delta_fox_w_out': 'delta_w', 'delta_ssd_w_in': 'delta_w', 'delta_ssd_conv_w': 'delta_w', 'delta_ssd_conv_b': 'delta_w', 'delta_ssd_dt_bias': 'delta_w', 'delta_ssd_a_log': 'delta_w', 'delta_ssd_d': 'delta_w', 'delta_ssd_norm_w': 'delta_w', 'delta_ssd_w_out': 'delta_w', 'new_m_mix_norm': 'new_m', 'new_m_ffn_norm': 'new_m', 'new_m_ffn_w_gu': 'new_m', 'new_m_ffn_w_down': 'new_m', 'new_m_conv_w_in': 'new_m', 'new_m_conv_w_dw': 'new_m', 'new_m_conv_w_out': 'new_m', 'new_m_fox_w_in': 'new_m', 'new_m_fox_b_f': 'new_m', 'new_m_fox_q_gain': 'new_m', 'new_m_fox_k_gain': 'new_m', 'new_m_fox_w_out': 'new_m', 'new_m_ssd_w_in': 'new_m', 'new_m_ssd_conv_w': 'new_m', 'new_m_ssd_conv_b': 'new_m', 'new_m_ssd_dt_bias': 'new_m', 'new_m_ssd_a_log': 'new_m', 'new_m_ssd_d': 'new_m', 'new_m_ssd_norm_w': 'new_m', 'new_m_ssd_w_out': 'new_m', 'new_v_mix_norm': 'new_v', 'new_v_ffn_norm': 'new_v', 'new_v_ffn_w_gu': 'new_v', 'new_v_ffn_w_down': 'new_v', 'new_v_conv_w_in': 'new_v', 'new_v_conv_w_dw': 'new_v', 'new_v_conv_w_out': 'new_v', 'new_v_fox_w_in': 'new_v', 'new_v_fox_b_f': 'new_v', 'new_v_fox_q_gain': 'new_v', 'new_v_fox_k_gain': 'new_v', 'new_v_fox_w_out': 'new_v', 'new_v_ssd_w_in': 'new_v', 'new_v_ssd_conv_w': 'new_v', 'new_v_ssd_conv_b': 'new_v', 'new_v_ssd_dt_bias': 'new_v', 'new_v_ssd_a_log': 'new_v', 'new_v_ssd_d': 'new_v', 'new_v_ssd_norm_w': 'new_v', 'new_v_ssd_w_out': 'new_v'}


def _forward(args):
    return _fwd_reference(*[args[k] for k in FWD_PARAMS])


def _output_shape():
    def fwd():
        inp = _fwd_setup_inputs(0)
        return _fwd_reference(*[inp[k] for k in FWD_PARAMS])
    out = _jax.eval_shape(fwd)
    return out.shape, out.dtype

N_MICROBATCH = 1
ADAM_LR = 0.001
ADAM_B1 = 0.9
ADAM_B2 = 0.999
ADAM_EPS = 1e-08
ADAM_WD = 0.01
ADAM_STEP = 10
PER_EXAMPLE_BATCH_AXIS = {'x': 0, 'loss_target': 0}
SHARED_INPUTS = []
_WEIGHT_DTYPES = {'mix_norm': _jnp.float32, 'ffn_norm': _jnp.float32, 'ffn_w_gu': _jnp.float32, 'ffn_w_down': _jnp.float32, 'conv_w_in': _jnp.float32, 'conv_w_dw': _jnp.float32, 'conv_w_out': _jnp.float32, 'fox_w_in': _jnp.float32, 'fox_b_f': _jnp.float32, 'fox_q_gain': _jnp.float32, 'fox_k_gain': _jnp.float32, 'fox_w_out': _jnp.float32, 'ssd_w_in': _jnp.float32, 'ssd_conv_w': _jnp.float32, 'ssd_conv_b': _jnp.float32, 'ssd_dt_bias': _jnp.float32, 'ssd_a_log': _jnp.float32, 'ssd_d': _jnp.float32, 'ssd_norm_w': _jnp.float32, 'ssd_w_out': _jnp.float32}
MOMENT_SCALE = {'mix_norm': 1.674715e+01, 'ffn_norm': 6.116272e+00, 'ffn_w_gu': 8.380890e-02, 'ffn_w_down': 4.311204e-01, 'conv_w_in': 2.818998e-01, 'conv_w_dw': 4.499741e+00, 'conv_w_out': 7.479367e-01, 'fox_w_in': 8.987557e-02, 'fox_b_f': 2.462489e+01, 'fox_q_gain': 7.997877e+00, 'fox_k_gain': 7.981682e+00, 'fox_w_out': 3.185168e-01, 'ssd_w_in': 1.399199e-01, 'ssd_conv_w': 1.440802e-01, 'ssd_conv_b': 4.231958e-01, 'ssd_dt_bias': 3.577281e-01, 'ssd_a_log': 1.296874e+00, 'ssd_d': 8.801212e-01, 'ssd_norm_w': 5.387427e+00, 'ssd_w_out': 1.114843e+00}


def _to_microbatches(a, axis):
    t = _jnp.moveaxis(a, axis, 0)
    t = t.reshape((N_MICROBATCH, t.shape[0] // N_MICROBATCH) + t.shape[1:])
    return _jnp.moveaxis(t, 1, axis + 1)


def setup_inputs(seed: int = 0) -> dict:
    inp = _fwd_setup_inputs(seed)
    key = _jax.random.fold_in(_jax.random.key(seed), 7919)
    shape, _ = _output_shape()
    out = dict(inp)
    out["loss_target"] = _jax.random.normal(_jax.random.fold_in(key, 0), shape, _jnp.float32)
    for i, name in enumerate(TWIN_WEIGHTS):
        w = inp[name].astype(_jnp.float32)
        if MOMENT_SCALE is None:
            s = _jnp.sqrt(_jnp.mean(_jnp.square(w)) + 1e-30)
        else:
            s = MOMENT_SCALE[name]
        km, kv = _jax.random.split(_jax.random.fold_in(key, i + 1))
        out[name] = w
        out["m_" + name] = s * _jax.random.normal(km, w.shape, _jnp.float32)
        out["v_" + name] = (s * s) * _jax.random.uniform(kv, w.shape, _jnp.float32, 0.5, 1.5)
    if N_MICROBATCH > 1:
        for name, axis in PER_EXAMPLE_BATCH_AXIS.items():
            out[name] = _to_microbatches(out[name], axis)
    return {'x': out['x'], 'mix_norm': out['mix_norm'], 'ffn_norm': out['ffn_norm'], 'ffn_w_gu': out['ffn_w_gu'], 'ffn_w_down': out['ffn_w_down'], 'conv_w_in': out['conv_w_in'], 'conv_w_dw': out['conv_w_dw'], 'conv_w_out': out['conv_w_out'], 'fox_w_in': out['fox_w_in'], 'fox_b_f': out['fox_b_f'], 'fox_q_gain': out['fox_q_gain'], 'fox_k_gain': out['fox_k_gain'], 'fox_w_out': out['fox_w_out'], 'ssd_w_in': out['ssd_w_in'], 'ssd_conv_w': out['ssd_conv_w'], 'ssd_conv_b': out['ssd_conv_b'], 'ssd_dt_bias': out['ssd_dt_bias'], 'ssd_a_log': out['ssd_a_log'], 'ssd_d': out['ssd_d'], 'ssd_norm_w': out['ssd_norm_w'], 'ssd_w_out': out['ssd_w_out'], 'loss_target': out['loss_target'], 'm_mix_norm': out['m_mix_norm'], 'm_ffn_norm': out['m_ffn_norm'], 'm_ffn_w_gu': out['m_ffn_w_gu'], 'm_ffn_w_down': out['m_ffn_w_down'], 'm_conv_w_in': out['m_conv_w_in'], 'm_conv_w_dw': out['m_conv_w_dw'], 'm_conv_w_out': out['m_conv_w_out'], 'm_fox_w_in': out['m_fox_w_in'], 'm_fox_b_f': out['m_fox_b_f'], 'm_fox_q_gain': out['m_fox_q_gain'], 'm_fox_k_gain': out['m_fox_k_gain'], 'm_fox_w_out': out['m_fox_w_out'], 'm_ssd_w_in': out['m_ssd_w_in'], 'm_ssd_conv_w': out['m_ssd_conv_w'], 'm_ssd_conv_b': out['m_ssd_conv_b'], 'm_ssd_dt_bias': out['m_ssd_dt_bias'], 'm_ssd_a_log': out['m_ssd_a_log'], 'm_ssd_d': out['m_ssd_d'], 'm_ssd_norm_w': out['m_ssd_norm_w'], 'm_ssd_w_out': out['m_ssd_w_out'], 'v_mix_norm': out['v_mix_norm'], 'v_ffn_norm': out['v_ffn_norm'], 'v_ffn_w_gu': out['v_ffn_w_gu'], 'v_ffn_w_down': out['v_ffn_w_down'], 'v_conv_w_in': out['v_conv_w_in'], 'v_conv_w_dw': out['v_conv_w_dw'], 'v_conv_w_out': out['v_conv_w_out'], 'v_fox_w_in': out['v_fox_w_in'], 'v_fox_b_f': out['v_fox_b_f'], 'v_fox_q_gain': out['v_fox_q_gain'], 'v_fox_k_gain': out['v_fox_k_gain'], 'v_fox_w_out': out['v_fox_w_out'], 'v_ssd_w_in': out['v_ssd_w_in'], 'v_ssd_conv_w': out['v_ssd_conv_w'], 'v_ssd_conv_b': out['v_ssd_conv_b'], 'v_ssd_dt_bias': out['v_ssd_dt_bias'], 'v_ssd_a_log': out['v_ssd_a_log'], 'v_ssd_d': out['v_ssd_d'], 'v_ssd_norm_w': out['v_ssd_norm_w'], 'v_ssd_w_out': out['v_ssd_w_out']}


def _loss(weights, diff, rest, loss_target):
    with _jax.named_scope("forward"):
        args = {**rest, TWIN_DIFF_INPUT: diff, **{k: w.astype(_WEIGHT_DTYPES[k]) for k, w in weights.items()}}
        y = _forward(args)
    with _jax.named_scope("loss_head"):
        err = _jnp.square(y.astype(_jnp.float32) - loss_target)
        return 0.5 * _jnp.sum(_jnp.mean(err, axis=-1)) if err.ndim else 0.5 * err


def _adamw(w, g, m, v):
    m = ADAM_B1 * m + (1.0 - ADAM_B1) * g
    v = ADAM_B2 * v + (1.0 - ADAM_B2) * _jnp.square(g)
    m_hat = m / (1.0 - ADAM_B1 ** ADAM_STEP)
    v_hat = v / (1.0 - ADAM_B2 ** ADAM_STEP)
    delta = -ADAM_LR * (m_hat / (_jnp.sqrt(v_hat) + ADAM_EPS) + ADAM_WD * w)
    return delta, m, v


def reference(x, mix_norm, ffn_norm, ffn_w_gu, ffn_w_down, conv_w_in, conv_w_dw, conv_w_out, fox_w_in, fox_b_f, fox_q_gain, fox_k_gain, fox_w_out, ssd_w_in, ssd_conv_w, ssd_conv_b, ssd_dt_bias, ssd_a_log, ssd_d, ssd_norm_w, ssd_w_out, loss_target, m_mix_norm, m_ffn_norm, m_ffn_w_gu, m_ffn_w_down, m_conv_w_in, m_conv_w_dw, m_conv_w_out, m_fox_w_in, m_fox_b_f, m_fox_q_gain, m_fox_k_gain, m_fox_w_out, m_ssd_w_in, m_ssd_conv_w, m_ssd_conv_b, m_ssd_dt_bias, m_ssd_a_log, m_ssd_d, m_ssd_norm_w, m_ssd_w_out, v_mix_norm, v_ffn_norm, v_ffn_w_gu, v_ffn_w_down, v_conv_w_in, v_conv_w_dw, v_conv_w_out, v_fox_w_in, v_fox_b_f, v_fox_q_gain, v_fox_k_gain, v_fox_w_out, v_ssd_w_in, v_ssd_conv_w, v_ssd_conv_b, v_ssd_dt_bias, v_ssd_a_log, v_ssd_d, v_ssd_norm_w, v_ssd_w_out):
    given = dict(x=x, mix_norm=mix_norm, ffn_norm=ffn_norm, ffn_w_gu=ffn_w_gu, ffn_w_down=ffn_w_down, conv_w_in=conv_w_in, conv_w_dw=conv_w_dw, conv_w_out=conv_w_out, fox_w_in=fox_w_in, fox_b_f=fox_b_f, fox_q_gain=fox_q_gain, fox_k_gain=fox_k_gain, fox_w_out=fox_w_out, ssd_w_in=ssd_w_in, ssd_conv_w=ssd_conv_w, ssd_conv_b=ssd_conv_b, ssd_dt_bias=ssd_dt_bias, ssd_a_log=ssd_a_log, ssd_d=ssd_d, ssd_norm_w=ssd_norm_w, ssd_w_out=ssd_w_out, loss_target=loss_target, m_mix_norm=m_mix_norm, m_ffn_norm=m_ffn_norm, m_ffn_w_gu=m_ffn_w_gu, m_ffn_w_down=m_ffn_w_down, m_conv_w_in=m_conv_w_in, m_conv_w_dw=m_conv_w_dw, m_conv_w_out=m_conv_w_out, m_fox_w_in=m_fox_w_in, m_fox_b_f=m_fox_b_f, m_fox_q_gain=m_fox_q_gain, m_fox_k_gain=m_fox_k_gain, m_fox_w_out=m_fox_w_out, m_ssd_w_in=m_ssd_w_in, m_ssd_conv_w=m_ssd_conv_w, m_ssd_conv_b=m_ssd_conv_b, m_ssd_dt_bias=m_ssd_dt_bias, m_ssd_a_log=m_ssd_a_log, m_ssd_d=m_ssd_d, m_ssd_norm_w=m_ssd_norm_w, m_ssd_w_out=m_ssd_w_out, v_mix_norm=v_mix_norm, v_ffn_norm=v_ffn_norm, v_ffn_w_gu=v_ffn_w_gu, v_ffn_w_down=v_ffn_w_down, v_conv_w_in=v_conv_w_in, v_conv_w_dw=v_conv_w_dw, v_conv_w_out=v_conv_w_out, v_fox_w_in=v_fox_w_in, v_fox_b_f=v_fox_b_f, v_fox_q_gain=v_fox_q_gain, v_fox_k_gain=v_fox_k_gain, v_fox_w_out=v_fox_w_out, v_ssd_w_in=v_ssd_w_in, v_ssd_conv_w=v_ssd_conv_w, v_ssd_conv_b=v_ssd_conv_b, v_ssd_dt_bias=v_ssd_dt_bias, v_ssd_a_log=v_ssd_a_log, v_ssd_d=v_ssd_d, v_ssd_norm_w=v_ssd_norm_w, v_ssd_w_out=v_ssd_w_out)
    weights = {n: given[n] for n in TWIN_WEIGHTS}
    shared = {n: given[n] for n in SHARED_INPUTS}
    per_example = {n: given[n] for n in ['x']}
    grad_fn = _jax.value_and_grad(_loss, argnums=(0, 1))

    def one_microbatch(ex, loss_target):
        ex = dict(ex)
        diff = ex.pop(TWIN_DIFF_INPUT)
        return grad_fn(weights, diff, {**shared, **ex}, loss_target)

    if N_MICROBATCH == 1:
        loss, (grad_w, grad_x) = one_microbatch(per_example, given["loss_target"])
    else:
        def body(carry, xs):
            loss_sum, grad_sum = carry
            l_k, (gw_k, gx_k) = one_microbatch(xs[0], xs[1])
            with _jax.named_scope("update"):
                return (loss_sum + l_k, _jax.tree.map(_jnp.add, grad_sum, gw_k)), gx_k

        init = (_jnp.zeros((), _jnp.float32), _jax.tree.map(_jnp.zeros_like, weights))
        (loss, grad_w), grad_x = _jax.lax.scan(body, init, (per_example, given["loss_target"]))
    with _jax.named_scope("update"):
        delta_w, new_m, new_v = {}, {}, {}
        for n in TWIN_WEIGHTS:
            delta_w[n], new_m[n], new_v[n] = _adamw(weights[n], grad_w[n], given["m_" + n], given["v_" + n])
    return (loss, grad_x, *[grad_w[n] for n in TWIN_WEIGHTS], *[delta_w[n] for n in TWIN_WEIGHTS],
            *[new_m[n] for n in TWIN_WEIGHTS], *[new_v[n] for n in TWIN_WEIGHTS])
```

```python
import jax
import jax.numpy as jnp
from jax import lax
from jax.experimental import pallas as pl
from jax.experimental.pallas import tpu as pltpu

F32 = jnp.float32
BF16 = jnp.bfloat16
HI = lax.Precision.HIGHEST

NDEV = 8
D_MODEL = 1024
DEPTH = 4
D_FF = 2816
RMS_EPS = 1e-6
HEAD_DIM = 64
ATTN_HEADS = 16
FOX_IN = 3 * D_MODEL + ATTN_HEADS
FOX_IN_PAD = 3200
SSM_INNER = 2048
SSM_HEADS = 32
SSM_GROUPS = 8
SSM_STATE = 128
SSM_CHUNK = 128
SSM_CONV_DIM = 4096
SSM_IN = SSM_INNER + SSM_CONV_DIM + SSM_HEADS
SSM_IN_PAD = 6272
LANES = 128
V7X_VMEM_BYTES = 64 * 1024 * 1024
VMEM_LIMIT_BYTES = (V7X_VMEM_BYTES * 3) // 4

ADAM_LR = 0.001
ADAM_B1 = 0.9
ADAM_B2 = 0.999
ADAM_EPS = 1e-08
ADAM_WD = 0.01
ADAM_STEP = 10

_TILE_CANDIDATES = (1024, 1408, 896, 768, 640, 512, 384, 256, 128)


def _pick_tile(n):
    for c in _TILE_CANDIDATES:
        if n % c == 0:
            return c
    raise ValueError(f"no tile for {n}")


def _params(ngrid):
    return pltpu.CompilerParams(dimension_semantics=("arbitrary",) * ngrid, vmem_limit_bytes=VMEM_LIMIT_BYTES)


def _pc(body, name, grid, in_specs, out_specs, out_shape, scratch=()):
    return pl.pallas_call(
        body, name=name, grid=grid, in_specs=in_specs, out_specs=out_specs, out_shape=out_shape,
        scratch_shapes=list(scratch), compiler_params=_params(len(grid)))


def _dot(a, b, ca, cb, prec=None):
    return lax.dot_general(a, b, (((ca,), (cb,)), ((), ())), preferred_element_type=F32, precision=prec)


def _sds(shape, dtype=F32):
    return jax.ShapeDtypeStruct(shape, dtype)


def _row_tile(s, want=256):
    return want if s % want == 0 else s


def _sigmoid(x):
    return 1.0 / (1.0 + jnp.exp(-x))


def _softplus(x):
    return jnp.maximum(x, 0.0) + jnp.log(1.0 + jnp.exp(-jnp.abs(x)))


def _mm(a, b, mode, name, out_dtype=F32, res=None):
    if mode == "tn":
        r, m = a.shape
        n = b.shape[1]
        tm, tn, tk = _pick_tile(m), _pick_tile(n), _pick_tile(r)
        grid = (m // tm, n // tn, r // tk)
        a_spec = pl.BlockSpec((tk, tm), lambda i, j, k: (k, i))
        b_spec = pl.BlockSpec((tk, tn), lambda i, j, k: (k, j))
        ca, cb = 0, 0
    else:
        m, kd = a.shape
        n = b.shape[1] if mode == "nn" else b.shape[0]
        tm, tn, tk = _pick_tile(m), _pick_tile(n), _pick_tile(kd)
        grid = (m // tm, n // tn, kd // tk)
        a_spec = pl.BlockSpec((tm, tk), lambda i, j, k: (i, k))
        if mode == "nn":
            b_spec = pl.BlockSpec((tk, tn), lambda i, j, k: (k, j))
            ca, cb = 1, 0
        else:
            b_spec = pl.BlockSpec((tn, tk), lambda i, j, k: (j, k))
            ca, cb = 1, 1
    nk = grid[2]
    o_spec = pl.BlockSpec((tm, tn), lambda i, j, k: (i, j))
    has_res = res is not None

    def body(*refs):
        if has_res:
            a_ref, b_ref, r_ref, o_ref, acc_ref = refs
        else:
            a_ref, b_ref, o_ref, acc_ref = refs
        k = pl.program_id(2)

        @pl.when(k == 0)
        def _():
            acc_ref[...] = jnp.zeros_like(acc_ref)

        acc_ref[...] += _dot(a_ref[...].astype(BF16), b_ref[...].astype(BF16), ca, cb)

        @pl.when(k == nk - 1)
        def _():
            out = acc_ref[...]
            if has_res:
                out = out + r_ref[...]
            o_ref[...] = out.astype(out_dtype)

    in_specs = [a_spec, b_spec] + ([o_spec] if has_res else [])
    args = (a, b) + ((res,) if has_res else ())
    return _pc(body, name, grid, in_specs, o_spec, _sds((m, n), out_dtype), [pltpu.VMEM((tm, tn), F32)])(*args)


def _rms_fwd(x, w, name):
    s, d = x.shape
    ts = _row_tile(s)

    def body(x_ref, w_ref, o_ref):
        xv = x_ref[...]
        r = lax.rsqrt(jnp.mean(xv * xv, axis=-1, keepdims=True) + RMS_EPS)
        o_ref[...] = ((xv * r) * w_ref[...]).astype(BF16)

    row = pl.BlockSpec((ts, d), lambda i: (i, 0))
    return _pc(body, name, (s // ts,), [row, pl.BlockSpec((1, d), lambda i: (0, 0))], row, _sds((s, d), BF16))(x, w)


def _rms_bwd(x, w, dh, dres, name):
    s, d = x.shape
    ts = _row_tile(s)

    def body(x_ref, w_ref, dh_ref, dr_ref, dx_ref, dw_ref):
        i = pl.program_id(0)
        xv = x_ref[...]
        r = lax.rsqrt(jnp.mean(xv * xv, axis=-1, keepdims=True) + RMS_EPS)
        xhat = xv * r
        dhv = dh_ref[...]
        g = dhv * w_ref[...]
        dx_ref[...] = dr_ref[...] + r * (g - xhat * jnp.mean(g * xhat, axis=-1, keepdims=True))

        @pl.when(i == 0)
        def _():
            dw_ref[...] = jnp.zeros_like(dw_ref)

        dw_ref[...] += jnp.sum(dhv * xhat, axis=0, keepdims=True)

    row = pl.BlockSpec((ts, d), lambda i: (i, 0))
    vec = pl.BlockSpec((1, d), lambda i: (0, 0))
    return _pc(body, name, (s // ts,), [row, vec, row, row], [row, vec], [_sds((s, d)), _sds((1, d))])(x, w, dh, dres)


def _swiglu_fwd(gu, name):
    s = gu.shape[0]
    ts = _row_tile(s)

    def body(gu_ref, o_ref):
        g = gu_ref[:, :D_FF]
        u = gu_ref[:, D_FF:]
        o_ref[...] = (g * _sigmoid(g) * u).astype(BF16)

    return _pc(body, name, (s // ts,), [pl.BlockSpec((ts, 2 * D_FF), lambda i: (i, 0))],
               pl.BlockSpec((ts, D_FF), lambda i: (i, 0)), _sds((s, D_FF), BF16))(gu)


def _swiglu_bwd(gu, da, name):
    s = gu.shape[0]
    ts = _row_tile(s)

    def body(gu_ref, da_ref, o_ref):
        g = gu_ref[:, :D_FF]
        u = gu_ref[:, D_FF:]
        dav = da_ref[...]
        sg = _sigmoid(g)
        o_ref[:, :D_FF] = (dav * u * (sg * (1.0 + g * (1.0 - sg)))).astype(BF16)
        o_ref[:, D_FF:] = (dav * (g * sg)).astype(BF16)

    wide = pl.BlockSpec((ts, 2 * D_FF), lambda i: (i, 0))
    return _pc(body, name, (s // ts,), [wide, pl.BlockSpec((ts, D_FF), lambda i: (i, 0))], wide,
               _sds((s, 2 * D_FF), BF16))(gu, da)


def _ffn_fwd(x, norm_w, w_gu, w_down, tag):
    h = _rms_fwd(x, norm_w, f"ffn_norm_{tag}")
    gu = _mm(h, w_gu, "nn", f"ffn_gu_{tag}")
    a = _swiglu_fwd(gu, f"ffn_act_{tag}")
    y = _mm(a, w_down, "nn", f"ffn_down_{tag}", res=x)
    return y, (x, h, gu, a)


def _ffn_bwd(dy, saved, norm_w, w_gu, w_down, tag):
    x, h, gu, a = saved
    da = _mm(dy, w_down, "nt", f"ffn_dact_{tag}")
    g_down = _mm(a, dy, "tn", f"ffn_gdown_{tag}")
    dgu = _swiglu_bwd(gu, da, f"ffn_dgu_{tag}")
    g_gu = _mm(h, dgu, "tn", f"ffn_ggu_{tag}")
    dh = _mm(dgu, w_gu, "nt", f"ffn_dh_{tag}")
    dx, g_norm = _rms_bwd(x, norm_w, dh, dy, f"ffn_dnorm_{tag}")
    return dx, g_norm, g_gu, g_down


def _prev_rows(cur, halo, j, first):
    rid = lax.broadcasted_iota(jnp.int32, cur.shape, 0)
    hid = lax.broadcasted_iota(jnp.int32, halo.shape, 0)
    out = pltpu.roll(cur, j, 0)
    for t in range(j):
        row = jnp.sum(jnp.where(hid == 8 - j + t, halo, 0.0), axis=0, keepdims=True)
        row = jnp.where(first, 0.0, row)
        out = jnp.where(rid == t, row, out)
    return out


def _next_rows(cur, halo, j, last):
    ts = cur.shape[0]
    rid = lax.broadcasted_iota(jnp.int32, cur.shape, 0)
    hid = lax.broadcasted_iota(jnp.int32, halo.shape, 0)
    out = pltpu.roll(cur, ts - j, 0)
    for t in range(j):
        row = jnp.sum(jnp.where(hid == t, halo, 0.0), axis=0, keepdims=True)
        row = jnp.where(last, 0.0, row)
        out = jnp.where(rid == ts - j + t, row, out)
    return out


def _halo_specs(ts, s, width, col):
    per = ts // 8
    nblk = s // 8
    prev = pl.BlockSpec((8, width), lambda i: (jnp.maximum(i * per - 1, 0), col))
    nxt = pl.BlockSpec((8, width), lambda i: (jnp.minimum((i + 1) * per, nblk - 1), col))
    return prev, nxt


def _cgate_fwd(p, w_dw, name):
    s = p.shape[0]
    d = D_MODEL
    ts = _row_tile(s)
    prev, _ = _halo_specs(ts, s, 3 * d, 0)

    def body(p_ref, h_ref, w_ref, z_ref):
        first = pl.program_id(0) == 0
        b = p_ref[:, :d]
        cv = p_ref[:, d:2 * d] * p_ref[:, 2 * d:]
        hcv = h_ref[:, d:2 * d] * h_ref[:, 2 * d:]
        u = w_ref[2:3, :] * cv + w_ref[1:2, :] * _prev_rows(cv, hcv, 1, first) + w_ref[0:1, :] * _prev_rows(cv, hcv, 2, first)
        z_ref[...] = (b * u).astype(BF16)

    return _pc(body, name, (s // ts,),
               [pl.BlockSpec((ts, 3 * d), lambda i: (i, 0)), prev, pl.BlockSpec((3, d), lambda i: (0, 0))],
               pl.BlockSpec((ts, d), lambda i: (i, 0)), _sds((s, d), BF16))(p, p, w_dw)


def _cgate_bwd(p, dz, w_dw, name):
    s = p.shape[0]
    d = D_MODEL
    ts = _row_tile(s)
    nt = s // ts
    p_prev, p_next = _halo_specs(ts, s, 3 * d, 0)
    _, dz_next = _halo_specs(ts, s, d, 0)

    def body(p_ref, hp_ref, hn_ref, dz_ref, dzn_ref, w_ref, dp_ref, dw_ref):
        i = pl.program_id(0)
        first = i == 0
        last = i == nt - 1
        b = p_ref[:, :d]
        c = p_ref[:, d:2 * d]
        v = p_ref[:, 2 * d:]
        cv = c * v
        hcv = hp_ref[:, d:2 * d] * hp_ref[:, 2 * d:]
        cv1 = _prev_rows(cv, hcv, 1, first)
        cv2 = _prev_rows(cv, hcv, 2, first)
        w0, w1, w2 = w_ref[0:1, :], w_ref[1:2, :], w_ref[2:3, :]
        u = w2 * cv + w1 * cv1 + w0 * cv2
        dzv = dz_ref[...]
        du = dzv * b
        dun = dzn_ref[...] * hn_ref[:, :d]
        dcv = w2 * du + w1 * _next_rows(du, dun, 1, last) + w0 * _next_rows(du, dun, 2, last)
        dp_ref[:, :d] = (dzv * u).astype(BF16)
        dp_ref[:, d:2 * d] = (dcv * v).astype(BF16)
        dp_ref[:, 2 * d:] = (dcv * c).astype(BF16)

        @pl.when(first)
        def _():
            dw_ref[...] = jnp.zeros_like(dw_ref)

        dw_ref[0:1, :] += jnp.sum(du * cv2, axis=0, keepdims=True)
        dw_ref[1:2, :] += jnp.sum(du * cv1, axis=0, keepdims=True)
        dw_ref[2:3, :] += jnp.sum(du * cv, axis=0, keepdims=True)

    wide = pl.BlockSpec((ts, 3 * d), lambda i: (i, 0))
    wspec = pl.BlockSpec((3, d), lambda i: (0, 0))
    return _pc(body, name, (nt,),
               [wide, p_prev, p_next, pl.BlockSpec((ts, d), lambda i: (i, 0)), dz_next, wspec],
               [wide, wspec], [_sds((s, 3 * d), BF16), _sds((3, d))])(p, p, p, dz, dz, w_dw)


def _conv_fwd(x, norm_w, w_in, w_dw, w_out, tag):
    h = _rms_fwd(x, norm_w, f"conv_norm_{tag}")
    p = _mm(h, w_in, "nn", f"conv_in_{tag}")
    z = _cgate_fwd(p, w_dw, f"conv_gate_{tag}")
    y = _mm(z, w_out, "nn", f"conv_out_{tag}", res=x)
    return y, (x, h, p, z)


def _conv_bwd(dy, saved, norm_w, w_in, w_dw, w_out, tag):
    x, h, p, z = saved
    dz = _mm(dy, w_out, "nt", f"conv_dz_{tag}")
    g_out = _mm(z, dy, "tn", f"conv_gout_{tag}")
    dp, g_dw = _cgate_bwd(p, dz, w_dw, f"conv_dgate_{tag}")
    g_in = _mm(h, dp, "tn", f"conv_gin_{tag}")
    dh = _mm(dp, w_in, "nt", f"conv_dh_{tag}")
    dx, g_norm = _rms_bwd(x, norm_w, dh, dy, f"conv_dnorm_{tag}")
    return dx, g_norm, g_in, g_dw, g_out


def _tri(lower):
    r = lax.broadcasted_iota(jnp.int32, (LANES, LANES), 0)
    c = lax.broadcasted_iota(jnp.int32, (LANES, LANES), 1)
    return jnp.where((r >= c) if lower else (r <= c), 1.0, 0.0).astype(F32)


def _cumsum_rows(v, reverse, name):
    s = v.shape[0]
    n = s // LANES
    idx = (lambda i: (n - 1 - i, 0)) if reverse else (lambda i: (i, 0))

    def body(v_ref, o_ref, carry_ref):
        @pl.when(pl.program_id(0) == 0)
        def _():
            carry_ref[...] = jnp.zeros_like(carry_ref)

        blk = v_ref[...]
        o_ref[...] = _dot(_tri(not reverse), blk, 1, 0, HI) + carry_ref[0:1, :]
        carry_ref[...] += jnp.sum(blk, axis=0, keepdims=True)

    spec = pl.BlockSpec((LANES, LANES), idx)
    return _pc(body, name, (n,), [spec], spec, _sds((s, LANES)), [pltpu.VMEM((8, LANES), F32)])(v)


def _lo_mask(shape):
    return lax.broadcasted_iota(jnp.int32, shape, len(shape) - 1) < HEAD_DIM


def _half_sums(v, lo):
    sa = jnp.sum(jnp.where(lo, v, 0.0), axis=-1, keepdims=True)
    sb = jnp.sum(jnp.where(lo, 0.0, v), axis=-1, keepdims=True)
    return jnp.where(lo, sa, sb)


def _fox_prep_fwd(proj, gq, gk, name):
    s = proj.shape[0]
    ts = _row_tile(s, 512)
    scale = HEAD_DIM ** -0.5

    def body(q_ref, k_ref, v_ref, gq_ref, gk_ref, qo_ref, ko_ref, vo_ref):
        lo = _lo_mask((ts, LANES))

        def hnorm(xv, g):
            ms = _half_sums(xv * xv, lo) * (1.0 / HEAD_DIM)
            return (xv * lax.rsqrt(ms + RMS_EPS)) * g

        qo_ref[...] = (hnorm(q_ref[...], gq_ref[...]) * scale).astype(BF16)
        ko_ref[...] = hnorm(k_ref[...], gk_ref[...]).astype(BF16)
        vo_ref[...] = v_ref[...].astype(BF16)

    def col(off):
        return pl.BlockSpec((ts, LANES), lambda i, p: (i, off + p))

    gspec = pl.BlockSpec((1, LANES), lambda i, p: (0, 0))
    out = _sds((s, D_MODEL), BF16)
    return _pc(body, name, (s // ts, 8), [col(0), col(8), col(16), gspec, gspec], [col(0)] * 3, [out] * 3)(
        proj, proj, proj, gq, gk)


def _fox_logf(proj, bf, name):
    s = proj.shape[0]
    ts = _row_tile(s, 512)

    def body(f_ref, b_ref, o_ref):
        z = f_ref[...] + b_ref[...]
        lf = jnp.minimum(z, 0.0) - jnp.log(1.0 + jnp.exp(-jnp.abs(z)))
        real = lax.broadcasted_iota(jnp.int32, (ts, LANES), 1) < ATTN_HEADS
        o_ref[...] = jnp.where(real, lf, 0.0)

    return _pc(body, name, (s // ts,), [pl.BlockSpec((ts, LANES), lambda i: (i, 24)), pl.BlockSpec((1, LANES), lambda i: (0, 0))],
               pl.BlockSpec((ts, LANES), lambda i: (i, 0)), _sds((s, LANES)))(proj, bf)


def _fox_dlogf(proj, bf, dlf, name):
    s = proj.shape[0]
    ts = _row_tile(s, 512)

    def body(f_ref, b_ref, d_ref, o_ref, db_ref):
        z = f_ref[...] + b_ref[...]
        real = lax.broadcasted_iota(jnp.int32, (ts, LANES), 1) < ATTN_HEADS
        g = jnp.where(real, d_ref[...] * _sigmoid(-z), 0.0)
        o_ref[...] = g.astype(BF16)

        @pl.when(pl.program_id(0) == 0)
        def _():
            db_ref[...] = jnp.zeros_like(db_ref)

        db_ref[...] += jnp.sum(g, axis=0, keepdims=True)

    vec = pl.BlockSpec((1, LANES), lambda i: (0, 0))
    row = pl.BlockSpec((ts, LANES), lambda i: (i, 0))
    return _pc(body, name, (s // ts,), [pl.BlockSpec((ts, LANES), lambda i: (i, 24)), vec, row], [row, vec],
               [_sds((s, LANES), BF16), _sds((1, LANES))])(proj, bf, dlf)


def _attn_tiles(s):
    t = 512 if s % 512 == 0 else s
    return t, s // t


def _head_logits(qx, k2, cq_col, c_row, mask):
    sc = _dot(qx, k2, 1, 1) + (cq_col - c_row)
    return jnp.where(mask, sc, -jnp.inf)


def _flash_fwd(qs, kn, vb, cqf, crow, name):
    s = qs.shape[0]
    t, n = _attn_tiles(s)

    def body(q_ref, k_ref, v_ref, cq_ref, cr_ref, o_ref, lse_ref, m_ref, l_ref, acc_ref):
        qi = pl.program_id(1)
        ki = pl.program_id(2)

        @pl.when(ki == 0)
        def _():
            m_ref[...] = jnp.full_like(m_ref, -jnp.inf)
            l_ref[...] = jnp.zeros_like(l_ref)
            acc_ref[...] = jnp.zeros_like(acc_ref)

        @pl.when(ki <= qi)
        def _():
            lo = _lo_mask((t, LANES))
            q2 = q_ref[...]
            k2 = k_ref[...]
            v2 = v_ref[...]
            zero = jnp.zeros_like(q2)
            rid = qi * t + lax.broadcasted_iota(jnp.int32, (t, t), 0)
            cid = ki * t + lax.broadcasted_iota(jnp.int32, (t, t), 1)
            mask = rid >= cid
            cq = cq_ref[...]
            cr = cr_ref[0]
            outs = []
            for hd in range(2):
                qx = jnp.where(lo, q2, zero) if hd == 0 else jnp.where(lo, zero, q2)
                sc = _head_logits(qx, k2, cq[:, hd * HEAD_DIM:hd * HEAD_DIM + 1], cr[hd:hd + 1, :], mask)
                m_prev = m_ref[:, hd * HEAD_DIM:hd * HEAD_DIM + 1]
                l_prev = l_ref[:, hd * HEAD_DIM:hd * HEAD_DIM + 1]
                m_new = jnp.maximum(m_prev, jnp.max(sc, axis=-1, keepdims=True))
                alpha = jnp.exp(m_prev - m_new)
                pm = jnp.exp(sc - m_new)
                l_new = alpha * l_prev + jnp.sum(pm, axis=-1, keepdims=True)
                outs.append((m_new, l_new, alpha, _dot(pm.astype(BF16), v2, 1, 0)))
            (ma, la, aa, pa), (mb, lb, ab, pb) = outs
            m_ref[...] = jnp.where(lo, ma, mb)
            l_ref[...] = jnp.where(lo, la, lb)
            acc_ref[...] = jnp.where(lo, aa, ab) * acc_ref[...] + jnp.where(lo, pa, pb)

        @pl.when(ki == qi)
        def _():
            o_ref[...] = acc_ref[...] / l_ref[...]
            lse_ref[...] = m_ref[...] + jnp.log(l_ref[...])

    qspec = pl.BlockSpec((t, LANES), lambda p, qi, ki: (qi, p))
    kspec = pl.BlockSpec((t, LANES), lambda p, qi, ki: (jnp.minimum(ki, qi), p))
    rspec = pl.BlockSpec((1, 2, t), lambda p, qi, ki: (p, 0, jnp.minimum(ki, qi)))
    out = _sds((s, D_MODEL))
    return _pc(body, name, (8, n, n), [qspec, kspec, kspec, qspec, rspec], [qspec, qspec], [out, out],
               [pltpu.VMEM((t, LANES), F32)] * 3)(qs, kn, vb, cqf, crow)


def _fox_delta(do, o, name):
    s = do.shape[0]
    ts = _row_tile(s, 512)

    def body(do_ref, o_ref, d_ref):
        d_ref[...] = _half_sums(do_ref[...] * o_ref[...], _lo_mask((ts, LANES)))

    spec = pl.BlockSpec((ts, LANES), lambda i, p: (i, p))
    return _pc(body, name, (s // ts, 8), [spec, spec], spec, _sds((s, D_MODEL)))(do, o)


def _flash_bwd_dq(qs, kn, vb, do, lse, delta, cqf, crow, name):
    s = qs.shape[0]
    t, n = _attn_tiles(s)

    def body(q_ref, k_ref, v_ref, do_ref, lse_ref, dl_ref, cq_ref, cr_ref, dq_ref, dcq_ref, acc_ref, racc_ref):
        qi = pl.program_id(1)
        ki = pl.program_id(2)

        @pl.when(ki == 0)
        def _():
            acc_ref[...] = jnp.zeros_like(acc_ref)
            racc_ref[...] = jnp.zeros_like(racc_ref)

        @pl.when(ki <= qi)
        def _():
            lo = _lo_mask((t, LANES))
            q2 = q_ref[...]
            k2 = k_ref[...]
            v2 = v_ref[...]
            do2 = do_ref[...].astype(BF16)
            zero = jnp.zeros_like(q2)
            rid = qi * t + lax.broadcasted_iota(jnp.int32, (t, t), 0)
            cid = ki * t + lax.broadcasted_iota(jnp.int32, (t, t), 1)
            mask = rid >= cid
            cq = cq_ref[...]
            cr = cr_ref[0]
            lse = lse_ref[...]
            dl = dl_ref[...]
            parts, rsum = [], []
            for hd in range(2):
                c0 = hd * HEAD_DIM
                sel = lo if hd == 0 else jnp.logical_not(lo)
                qx = jnp.where(sel, q2, zero)
                dox = jnp.where(sel, do2, zero)
                sc = _head_logits(qx, k2, cq[:, c0:c0 + 1], cr[hd:hd + 1, :], mask)
                pm = jnp.exp(sc - lse[:, c0:c0 + 1])
                dp = _dot(dox, v2, 1, 1)
                ds = pm * (dp - dl[:, c0:c0 + 1])
                parts.append(_dot(ds.astype(BF16), k2, 1, 0))
                rsum.append(jnp.sum(ds, axis=-1, keepdims=True))
            acc_ref[...] += jnp.where(lo, parts[0], parts[1])
            racc_ref[...] += jnp.where(lo, rsum[0], rsum[1])

        @pl.when(ki == qi)
        def _():
            dq_ref[...] = acc_ref[...]
            dcq_ref[...] = racc_ref[...]

    qspec = pl.BlockSpec((t, LANES), lambda p, qi, ki: (qi, p))
    kspec = pl.BlockSpec((t, LANES), lambda p, qi, ki: (jnp.minimum(ki, qi), p))
    rspec = pl.BlockSpec((1, 2, t), lambda p, qi, ki: (p, 0, jnp.minimum(ki, qi)))
    out = _sds((s, D_MODEL))
    return _pc(body, name, (8, n, n), [qspec, kspec, kspec, qspec, qspec, qspec, qspec, rspec], [qspec, qspec],
               [out, out], [pltpu.VMEM((t, LANES), F32)] * 2)(qs, kn, vb, do, lse, delta, cqf, crow)


def _flash_bwd_dkv(qs, kn, vb, do, lse, delta, cqf, crow, name):
    s = qs.shape[0]
    t, n = _attn_tiles(s)

    def body(q_ref, k_ref, v_ref, do_ref, lse_ref, dl_ref, cq_ref, cr_ref, dk_ref, dv_ref, dc_ref, dka_ref, dva_ref, dca_ref):
        ki = pl.program_id(1)
        qi = pl.program_id(2)

        @pl.when(qi == 0)
        def _():
            dka_ref[...] = jnp.zeros_like(dka_ref)
            dva_ref[...] = jnp.zeros_like(dva_ref)
            dca_ref[...] = jnp.zeros_like(dca_ref)

        @pl.when(qi >= ki)
        def _():
            lo = _lo_mask((t, LANES))
            q2 = q_ref[...]
            k2 = k_ref[...]
            v2 = v_ref[...]
            do2 = do_ref[...].astype(BF16)
            zero = jnp.zeros_like(q2)
            rid = qi * t + lax.broadcasted_iota(jnp.int32, (t, t), 0)
            cid = ki * t + lax.broadcasted_iota(jnp.int32, (t, t), 1)
            mask = rid >= cid
            cq = cq_ref[...]
            cr = cr_ref[0]
            lse = lse_ref[...]
            dl = dl_ref[...]
            dks, dvs = [], []
            for hd in range(2):
                c0 = hd * HEAD_DIM
                sel = lo if hd == 0 else jnp.logical_not(lo)
                qx = jnp.where(sel, q2, zero)
                dox = jnp.where(sel, do2, zero)
                sc = _head_logits(qx, k2, cq[:, c0:c0 + 1], cr[hd:hd + 1, :], mask)
                pm = jnp.exp(sc - lse[:, c0:c0 + 1])
                dp = _dot(dox, v2, 1, 1)
                ds = pm * (dp - dl[:, c0:c0 + 1])
                dvs.append(_dot(pm.astype(BF16), do2, 0, 0))
                dks.append(_dot(ds.astype(BF16), q2, 0, 0))
                dca_ref[hd:hd + 1, :] -= jnp.sum(ds, axis=0, keepdims=True)
            dka_ref[...] += jnp.where(lo, dks[0], dks[1])
            dva_ref[...] += jnp.where(lo, dvs[0], dvs[1])

        @pl.when(qi == n - 1)
        def _():
            dk_ref[...] = dka_ref[...]
            dv_ref[...] = dva_ref[...]
            dc_ref[0] = dca_ref[0:2, :]

    qspec = pl.BlockSpec((t, LANES), lambda p, ki, qi: (jnp.maximum(qi, ki), p))
    kspec = pl.BlockSpec((t, LANES), lambda p, ki, qi: (ki, p))
    rspec = pl.BlockSpec((1, 2, t), lambda p, ki, qi: (p, 0, ki))
    out = _sds((s, D_MODEL))
    return _pc(body, name, (8, n, n), [qspec, kspec, kspec, qspec, qspec, qspec, qspec, rspec], [kspec, kspec, rspec],
               [out, out, _sds((8, 2, s))],
               [pltpu.VMEM((t, LANES), F32), pltpu.VMEM((t, LANES), F32), pltpu.VMEM((8, t), F32)])(
                   qs, kn, vb, do, lse, delta, cqf, crow)


def _fox_prep_bwd(proj, dqs, dk, dv, gq, gk, name):
    s = proj.shape[0]
    ts = _row_tile(s, 512)
    scale = HEAD_DIM ** -0.5

    def body(q_ref, k_ref, dq_ref, dk_ref, dv_ref, gq_ref, gk_ref, oq_ref, ok_ref, ov_ref, dgq_ref, dgk_ref):
        lo = _lo_mask((ts, LANES))

        @pl.when(jnp.logical_and(pl.program_id(0) == 0, pl.program_id(1) == 0))
        def _():
            dgq_ref[...] = jnp.zeros_like(dgq_ref)
            dgk_ref[...] = jnp.zeros_like(dgk_ref)

        def back(xv, dout, g):
            r = lax.rsqrt(_half_sums(xv * xv, lo) * (1.0 / HEAD_DIM) + RMS_EPS)
            y = xv * r
            dy = dout * g
            dx = r * (dy - y * (_half_sums(dy * y, lo) * (1.0 / HEAD_DIM)))
            return dx, jnp.sum(dout * y, axis=0, keepdims=True)

        dxq, dgq = back(q_ref[...], dq_ref[...] * scale, gq_ref[...])
        dxk, dgk = back(k_ref[...], dk_ref[...], gk_ref[...])
        oq_ref[...] = dxq.astype(BF16)
        ok_ref[...] = dxk.astype(BF16)
        ov_ref[...] = dv_ref[...].astype(BF16)
        dgq_ref[...] += dgq
        dgk_ref[...] += dgk

    def col(off):
        return pl.BlockSpec((ts, LANES), lambda i, p: (i, off + p))

    gspec = pl.BlockSpec((1, LANES), lambda i, p: (0, 0))
    out = _sds((s, D_MODEL), BF16)
    return _pc(body, name, (s // ts, 8), [col(0), col(8), col(0), col(0), col(0), gspec, gspec],
               [col(0)] * 3 + [gspec] * 2, [out] * 3 + [_sds((1, LANES))] * 2)(proj, proj, dqs, dk, dv, gq, gk)


def _fox_fwd(x, norm_w, w_in, b_f, q_gain, k_gain, w_out):
    s = x.shape[0]
    h = _rms_fwd(x, norm_w, "fox_norm")
    proj = _mm(h, w_in, "nn", "fox_in")
    gq = jnp.tile(q_gain, (1, 2))
    gk = jnp.tile(k_gain, (1, 2))
    bf = jnp.pad(b_f, ((0, 0), (0, LANES - ATTN_HEADS)))
    qs, kn, vb = _fox_prep_fwd(proj, gq, gk, "fox_prep")
    cum = _cumsum_rows(_fox_logf(proj, bf, "fox_logf"), False, "fox_cum")[:, :ATTN_HEADS]
    cqf = jnp.repeat(cum, HEAD_DIM, axis=1)
    crow = cum.T.reshape(8, 2, s)
    o, lse = _flash_fwd(qs, kn, vb, cqf, crow, "fox_attn")
    y = _mm(o, w_out, "nn", "fox_out", res=x)
    return y, (x, h, proj, gq, gk, bf, qs, kn, vb, cqf, crow, o, lse)


def _fox_bwd(dy, saved, norm_w, w_in, w_out):
    x, h, proj, gq, gk, bf, qs, kn, vb, cqf, crow, o, lse = saved
    s = x.shape[0]
    do = _mm(dy, w_out, "nt", "fox_do")
    g_out = _mm(o, dy, "tn", "fox_gout")
    delta = _fox_delta(do, o, "fox_delta")
    dqs, dcq = _flash_bwd_dq(qs, kn, vb, do, lse, delta, cqf, crow, "fox_dq")
    dk, dv, dcrow = _flash_bwd_dkv(qs, kn, vb, do, lse, delta, cqf, crow, "fox_dkv")
    dcum = jnp.pad(dcq[:, ::HEAD_DIM] + dcrow.reshape(ATTN_HEADS, s).T, ((0, 0), (0, LANES - ATTN_HEADS)))
    dlf = _cumsum_rows(dcum, True, "fox_dcum")
    dfl, g_bf = _fox_dlogf(proj, bf, dlf, "fox_dlogf")
    dq_o, dk_o, dv_o, g_gq, g_gk = _fox_prep_bwd(proj, dqs, dk, dv, gq, gk, "fox_dprep")
    dproj = jnp.concatenate([dq_o, dk_o, dv_o, dfl], axis=1)
    g_in = _mm(h, dproj, "tn", "fox_gin")
    dh = _mm(dproj, w_in, "nt", "fox_dh")
    dx, g_norm = _rms_bwd(x, norm_w, dh, dy, "fox_dnorm")
    g_q = g_gq[:, :HEAD_DIM] + g_gq[:, HEAD_DIM:]
    g_k = g_gk[:, :HEAD_DIM] + g_gk[:, HEAD_DIM:]
    return dx, g_norm, g_in[:, :FOX_IN], g_bf[:, :ATTN_HEADS], g_q, g_k, g_out


def _ssd_conv_fwd(proj, cw, cb, name):
    s = proj.shape[0]
    ts = _row_tile(s)
    w = 1024
    per = ts // 8

    def body(p_ref, h_ref, w_ref, b_ref, o_ref):
        first = pl.program_id(0) == 0
        cur = p_ref[...]
        halo = h_ref[...]
        u = w_ref[3:4, :] * cur + b_ref[...]
        for j in range(1, 4):
            u = u + w_ref[3 - j:4 - j, :] * _prev_rows(cur, halo, j, first)
        o_ref[...] = u * _sigmoid(u)

    return _pc(body, name, (s // ts, 4),
               [pl.BlockSpec((ts, w), lambda i, j: (i, 2 + j)),
                pl.BlockSpec((8, w), lambda i, j: (jnp.maximum(i * per - 1, 0), 2 + j)),
                pl.BlockSpec((4, w), lambda i, j: (0, j)), pl.BlockSpec((1, w), lambda i, j: (0, j))],
               pl.BlockSpec((ts, w), lambda i, j: (i, j)), _sds((s, SSM_CONV_DIM)))(proj, proj, cw, cb)


def _ssd_conv_bwd_act(proj, dxbc, cw, cb, name):
    s = proj.shape[0]
    ts = _row_tile(s)
    w = 1024
    per = ts // 8

    def body(p_ref, h_ref, d_ref, w_ref, b_ref, g_ref, db_ref):
        first = pl.program_id(1) == 0
        cur = p_ref[...]
        halo = h_ref[...]
        u = w_ref[3:4, :] * cur + b_ref[...]
        for j in range(1, 4):
            u = u + w_ref[3 - j:4 - j, :] * _prev_rows(cur, halo, j, first)
        sg = _sigmoid(u)
        g = d_ref[...] * (sg * (1.0 + u * (1.0 - sg)))
        g_ref[...] = g

        @pl.when(first)
        def _():
            db_ref[...] = jnp.zeros_like(db_ref)

        db_ref[...] += jnp.sum(g, axis=0, keepdims=True)

    vec = pl.BlockSpec((1, w), lambda j, i: (0, j))
    tile = pl.BlockSpec((ts, w), lambda j, i: (i, j))
    return _pc(body, name, (4, s // ts),
               [pl.BlockSpec((ts, w), lambda j, i: (i, 2 + j)),
                pl.BlockSpec((8, w), lambda j, i: (jnp.maximum(i * per - 1, 0), 2 + j)),
                tile, pl.BlockSpec((4, w), lambda j, i: (0, j)), vec],
               [tile, vec], [_sds((s, SSM_CONV_DIM)), _sds((1, SSM_CONV_DIM))])(proj, proj, dxbc, cw, cb)


def _ssd_conv_bwd_in(proj, g, cw, name):
    s = proj.shape[0]
    ts = _row_tile(s)
    nt = s // ts
    w = 1024
    per = ts // 8
    nblk = s // 8

    def body(p_ref, h_ref, g_ref, gn_ref, w_ref, o_ref, dw_ref):
        i = pl.program_id(1)
        first = i == 0
        last = i == nt - 1
        cur = p_ref[...]
        halo = h_ref[...]
        gv = g_ref[...]
        gn = gn_ref[...]

        @pl.when(first)
        def _():
            dw_ref[...] = jnp.zeros_like(dw_ref)

        dpre = w_ref[3:4, :] * gv
        dw_ref[3:4, :] += jnp.sum(gv * cur, axis=0, keepdims=True)
        for j in range(1, 4):
            dpre = dpre + w_ref[3 - j:4 - j, :] * _next_rows(gv, gn, j, last)
            dw_ref[3 - j:4 - j, :] += jnp.sum(gv * _prev_rows(cur, halo, j, first), axis=0, keepdims=True)
        o_ref[...] = dpre.astype(BF16)

    tile = pl.BlockSpec((ts, w), lambda j, i: (i, j))
    wspec = pl.BlockSpec((4, w), lambda j, i: (0, j))
    return _pc(body, name, (4, nt),
               [pl.BlockSpec((ts, w), lambda j, i: (i, 2 + j)),
                pl.BlockSpec((8, w), lambda j, i: (jnp.maximum(i * per - 1, 0), 2 + j)),
                tile, pl.BlockSpec((8, w), lambda j, i: (jnp.minimum((i + 1) * per, nblk - 1), j)), wspec],
               [tile, wspec], [_sds((s, SSM_CONV_DIM), BF16), _sds((4, SSM_CONV_DIM))])(proj, proj, g, g, cw)


def _ssd_dt_fwd(proj, bias, a_neg, name):
    s = proj.shape[0]
    n = s // SSM_CHUNK

    def body(r_ref, b_ref, a_ref, dt_ref, ac_ref):
        real = lax.broadcasted_iota(jnp.int32, (SSM_CHUNK, LANES), 1) < SSM_HEADS
        dt = jnp.where(real, _softplus(r_ref[...] + b_ref[...]), 0.0)
        dt_ref[...] = dt
        ac_ref[...] = _dot(_tri(True), dt * a_ref[...], 1, 0, HI)

    vec = pl.BlockSpec((1, LANES), lambda c: (0, 0))
    row = pl.BlockSpec((SSM_CHUNK, LANES), lambda c: (c, 0))
    return _pc(body, name, (n,), [pl.BlockSpec((SSM_CHUNK, LANES), lambda c: (c, 48)), vec, vec], [row, row],
               [_sds((s, LANES)), _sds((s, LANES))])(proj, bias, a_neg)


def _ssd_dt_bwd(proj, bias, ddt, name):
    s = proj.shape[0]
    ts = _row_tile(s, 512)

    def body(r_ref, b_ref, d_ref, o_ref, db_ref):
        real = lax.broadcasted_iota(jnp.int32, (ts, LANES), 1) < SSM_HEADS
        g = jnp.where(real, d_ref[...] * _sigmoid(r_ref[...] + b_ref[...]), 0.0)
        o_ref[...] = g.astype(BF16)

        @pl.when(pl.program_id(0) == 0)
        def _():
            db_ref[...] = jnp.zeros_like(db_ref)

        db_ref[...] += jnp.sum(g, axis=0, keepdims=True)

    vec = pl.BlockSpec((1, LANES), lambda i: (0, 0))
    row = pl.BlockSpec((ts, LANES), lambda i: (i, 0))
    return _pc(body, name, (s // ts,), [pl.BlockSpec((ts, LANES), lambda i: (i, 48)), vec, row], [row, vec],
               [_sds((s, LANES), BF16), _sds((1, LANES))])(proj, bias, ddt)


def _pair_cols(cols, k0, lo):
    return jnp.where(lo, cols[:, k0:k0 + 1], cols[:, k0 + 1:k0 + 2])


def _last_lane(row):
    lane = lax.broadcasted_iota(jnp.int32, row.shape, 1)
    return jnp.sum(jnp.where(lane == SSM_CHUNK - 1, row, 0.0), axis=-1, keepdims=True)


def _ssd_specs(nc, rev):
    cc = (lambda c: nc - 1 - c) if rev else (lambda c: c)
    return dict(
        x=pl.BlockSpec((SSM_CHUNK, 256), lambda g, c: (cc(c), g)),
        b=pl.BlockSpec((SSM_CHUNK, LANES), lambda g, c: (cc(c), 16 + g)),
        c=pl.BlockSpec((SSM_CHUNK, LANES), lambda g, c: (cc(c), 24 + g)),
        col=pl.BlockSpec((1, SSM_CHUNK, 4), lambda g, c: (g, cc(c), 0)),
        row=pl.BlockSpec((1, 4, SSM_CHUNK), lambda g, c: (g, 0, cc(c))),
        grp=pl.BlockSpec((1, 1, 256), lambda g, c: (g, 0, 0)),
        grow=pl.BlockSpec((1, 4, LANES), lambda g, c: (g, 0, 0)),
        hs=pl.BlockSpec((1, 1, 256, SSM_STATE), lambda g, c: (cc(c), g, 0, 0)),
        bc=pl.BlockSpec((SSM_CHUNK, LANES), lambda g, c: (cc(c), g)),
    )


def _ssd_scan_fwd(xbc, dtc, acol, drow, arow, dskip, name):
    s = xbc.shape[0]
    nc = s // SSM_CHUNK
    sp = _ssd_specs(nc, False)
    L = SSM_CHUNK

    def body(x_ref, b_ref, c_ref, dtc_ref, ac_ref, dr_ref, ar_ref, dk_ref, y_ref, hs_ref, h_ref):
        @pl.when(pl.program_id(1) == 0)
        def _():
            h_ref[...] = jnp.zeros_like(h_ref)

        bb = b_ref[...].astype(BF16)
        cb = c_ref[...].astype(BF16)
        gm = _dot(cb, bb, 1, 1)
        dtc = dtc_ref[0]
        ac = ac_ref[0]
        dr = dr_ref[0]
        ar = ar_ref[0]
        dsk = dk_ref[0]
        hs_ref[0, 0] = h_ref[...]
        tril = lax.broadcasted_iota(jnp.int32, (L, L), 0) >= lax.broadcasted_iota(jnp.int32, (L, L), 1)
        lo = _lo_mask((L, LANES))
        rowlo = lax.broadcasted_iota(jnp.int32, (L, LANES), 0) < HEAD_DIM
        for pr in range(2):
            k0 = 2 * pr
            xp = x_ref[:, pr * LANES:(pr + 1) * LANES]
            xpb = xp.astype(BF16)
            hp = h_ref[pr * LANES:(pr + 1) * LANES, :]
            yd, al = [], []
            for k in (k0, k0 + 1):
                seg = ac[:, k:k + 1] - ar[k:k + 1, :]
                wk = gm * jnp.exp(jnp.where(tril, seg, -jnp.inf)) * dr[k:k + 1, :]
                yd.append(_dot(wk.astype(BF16), xpb, 1, 0))
                al.append(_last_lane(ar[k:k + 1, :]))
            e = jnp.exp(_pair_cols(ac, k0, lo))
            yo = _dot(cb, hp.astype(BF16), 1, 1) * e
            y_ref[:, pr * LANES:(pr + 1) * LANES] = jnp.where(lo, yd[0], yd[1]) + yo + dsk[:, pr * LANES:(pr + 1) * LANES] * xp
            wp = jnp.where(lo, jnp.exp(al[0] - ac[:, k0:k0 + 1]) * dtc[:, k0:k0 + 1],
                           jnp.exp(al[1] - ac[:, k0 + 1:k0 + 2]) * dtc[:, k0 + 1:k0 + 2])
            st = _dot((xp * wp).astype(BF16), bb, 0, 0)
            dec = jnp.where(rowlo, jnp.exp(al[0]), jnp.exp(al[1]))
            h_ref[pr * LANES:(pr + 1) * LANES, :] = dec * hp + st

    return _pc(body, name, (SSM_GROUPS, nc),
               [sp["x"], sp["b"], sp["c"], sp["col"], sp["col"], sp["row"], sp["row"], sp["grp"]],
               [sp["x"], sp["hs"]], [_sds((s, SSM_INNER)), _sds((nc, SSM_GROUPS, 256, SSM_STATE))],
               [pltpu.VMEM((256, SSM_STATE), F32)])(xbc, xbc, xbc, dtc, acol, drow, arow, dskip)


def _ssd_scan_bwd(xbc, dtc, acol, drow, arow, dskip, agrp, hs, dy, name):
    s = xbc.shape[0]
    nc = s // SSM_CHUNK
    sp = _ssd_specs(nc, True)
    L = SSM_CHUNK

    def body(x_ref, b_ref, c_ref, dtc_ref, ac_ref, dr_ref, ar_ref, dk_ref, ag_ref, hs_ref, dy_ref,
             dx_ref, db_ref, dc_ref, ddt_ref, da_ref, dd_ref, dh_ref):
        @pl.when(pl.program_id(1) == 0)
        def _():
            dh_ref[...] = jnp.zeros_like(dh_ref)
            da_ref[...] = jnp.zeros_like(da_ref)
            dd_ref[...] = jnp.zeros_like(dd_ref)

        bb = b_ref[...].astype(BF16)
        cb = c_ref[...].astype(BF16)
        gm = _dot(cb, bb, 1, 1)
        dtc = dtc_ref[0]
        ac = ac_ref[0]
        dr = dr_ref[0]
        ar = ar_ref[0]
        dsk = dk_ref[0]
        ag = ag_ref[0]
        tril = lax.broadcasted_iota(jnp.int32, (L, L), 0) >= lax.broadcasted_iota(jnp.int32, (L, L), 1)
        lo = _lo_mask((L, LANES))
        nlo = jnp.logical_not(lo)
        rowlo = lax.broadcasted_iota(jnp.int32, (L, LANES), 0) < HEAD_DIM
        lane = lax.broadcasted_iota(jnp.int32, (L, LANES), 1)
        lane_row = lax.broadcasted_iota(jnp.int32, (1, LANES), 1)
        dgm = jnp.zeros((L, L), F32)
        dcm = jnp.zeros((L, SSM_STATE), F32)
        dbm = jnp.zeros((L, SSM_STATE), F32)
        cols = jnp.zeros((L, LANES), F32)
        rows_ddt, rows_q, al_all, dcd_all = [], [], [], []
        for pr in range(2):
            k0 = 2 * pr
            xp = x_ref[:, pr * LANES:(pr + 1) * LANES]
            xpb = xp.astype(BF16)
            dyp = dy_ref[:, pr * LANES:(pr + 1) * LANES]
            dypb = dyp.astype(BF16)
            zero = jnp.zeros_like(dypb)
            hp = hs_ref[0, 0, pr * LANES:(pr + 1) * LANES, :]
            hpb = hp.astype(BF16)
            dst = dh_ref[pr * LANES:(pr + 1) * LANES, :]
            dstb = dst.astype(BF16)
            dxd, al = [], []
            for k in (k0, k0 + 1):
                sel = lo if k == k0 else nlo
                seg = ac[:, k:k + 1] - ar[k:k + 1, :]
                lam = jnp.exp(jnp.where(tril, seg, -jnp.inf))
                wk = gm * lam * dr[k:k + 1, :]
                dwk = _dot(jnp.where(sel, dypb, zero), xpb, 1, 1)
                mk = dwk * gm * lam
                qk = mk * dr[k:k + 1, :]
                dgm = dgm + dwk * lam * dr[k:k + 1, :]
                rows_ddt.append(jnp.sum(mk, axis=0, keepdims=True))
                rows_q.append(jnp.sum(qk, axis=0, keepdims=True))
                cols = jnp.where(lane == k, jnp.sum(qk, axis=-1, keepdims=True), cols)
                dxd.append(_dot(wk.astype(BF16), dypb, 0, 0))
                al.append(_last_lane(ar[k:k + 1, :]))
            al_all += al
            dxp = jnp.where(lo, dxd[0], dxd[1])
            e = jnp.exp(_pair_cols(ac, k0, lo))
            dye = dyp * e
            dyeb = dye.astype(BF16)
            dcm = dcm + _dot(dyeb, hpb, 1, 0)
            dh_yoff = _dot(dyeb, cb, 0, 0)
            tq = dye * _dot(cb, hpb, 1, 1)
            cols = jnp.where(lane == 4 + k0, jnp.sum(jnp.where(lo, tq, 0.0), axis=-1, keepdims=True), cols)
            cols = jnp.where(lane == 5 + k0, jnp.sum(jnp.where(lo, 0.0, tq), axis=-1, keepdims=True), cols)
            wp = jnp.where(lo, jnp.exp(al[0] - ac[:, k0:k0 + 1]) * dtc[:, k0:k0 + 1],
                           jnp.exp(al[1] - ac[:, k0 + 1:k0 + 2]) * dtc[:, k0 + 1:k0 + 2])
            dxw = _dot(bb, dstb, 1, 1)
            dxp = dxp + dxw * wp
            tw = xp * dxw
            cols = jnp.where(lane == 8 + k0, jnp.sum(jnp.where(lo, tw, 0.0), axis=-1, keepdims=True), cols)
            cols = jnp.where(lane == 9 + k0, jnp.sum(jnp.where(lo, 0.0, tw), axis=-1, keepdims=True), cols)
            dbm = dbm + _dot((xp * wp).astype(BF16), dstb, 1, 0)
            dsl = dsk[:, pr * LANES:(pr + 1) * LANES]
            dx_ref[:, pr * LANES:(pr + 1) * LANES] = dxp + dsl * dyp
            dd_ref[0, :, pr * LANES:(pr + 1) * LANES] += jnp.sum(dyp * xp, axis=0, keepdims=True)
            prod = dst * hp
            dcd_all.append(jnp.sum(jnp.sum(jnp.where(rowlo, prod, 0.0), axis=-1, keepdims=True), axis=0, keepdims=True))
            dcd_all.append(jnp.sum(jnp.sum(jnp.where(rowlo, 0.0, prod), axis=-1, keepdims=True), axis=0, keepdims=True))
            dec = jnp.where(rowlo, jnp.exp(al[0]), jnp.exp(al[1]))
            dh_ref[pr * LANES:(pr + 1) * LANES, :] = dec * dst + dh_yoff
        dgb = dgm.astype(BF16)
        dc_ref[...] = dcm + _dot(dgb, bb, 1, 0)
        db_ref[...] = dbm + _dot(dgb, cb, 0, 0)
        colt = cols.T
        sub8 = lax.broadcasted_iota(jnp.int32, (8, LANES), 0)
        da_rows = jnp.zeros((8, LANES), F32)
        ddt_part = []
        for k in range(4):
            rs = colt[k:k + 1, :]
            uo = colt[4 + k:5 + k, :]
            dwl = colt[8 + k:9 + k, :]
            es = jnp.exp(al_all[k] - ar[k:k + 1, :])
            wrow = es * dr[k:k + 1, :]
            dwl_w = dwl * wrow
            da_k = rs - rows_q[k] + uo - dwl_w
            tail = jnp.sum(dwl_w, axis=-1, keepdims=True) + jnp.exp(al_all[k]) * dcd_all[k]
            da_k = da_k + jnp.where(lane_row == L - 1, tail, 0.0)
            da_rows = jnp.where(sub8 == k, da_k, da_rows)
            ddt_part.append(rows_ddt[k] + dwl * es)
        dda = _dot(da_rows, _tri(True), 1, 0, HI)
        for k in range(4):
            dda_k = dda[k:k + 1, :]
            ddt_ref[0, k:k + 1, :] = ddt_part[k] + dda_k * ag[k:k + 1, :]
            da_ref[0, k:k + 1, :] += dda_k * dr[k:k + 1, :] * ag[k:k + 1, :]

    return _pc(body, name, (SSM_GROUPS, nc),
               [sp["x"], sp["b"], sp["c"], sp["col"], sp["col"], sp["row"], sp["row"], sp["grp"], sp["grow"], sp["hs"], sp["x"]],
               [sp["x"], sp["bc"], sp["bc"], sp["row"], sp["grow"], sp["grp"]],
               [_sds((s, SSM_INNER)), _sds((s, 1024)), _sds((s, 1024)), _sds((SSM_GROUPS, 4, s)),
                _sds((SSM_GROUPS, 4, LANES)), _sds((SSM_GROUPS, 1, 256))],
               [pltpu.VMEM((256, SSM_STATE), F32)])(xbc, xbc, xbc, dtc, acol, drow, arow, dskip, agrp, hs, dy)


def _gnorm_fwd(y, proj, nw, name):
    s = y.shape[0]
    ts = _row_tile(s)
    gw = SSM_INNER // SSM_GROUPS

    def body(y_ref, z_ref, w_ref, o_ref):
        for g in range(SSM_GROUPS):
            sl = slice(g * gw, (g + 1) * gw)
            z = z_ref[:, sl]
            y2 = y_ref[:, sl] * (z * _sigmoid(z))
            r = lax.rsqrt(jnp.mean(y2 * y2, axis=-1, keepdims=True) + RMS_EPS)
            o_ref[:, sl] = ((y2 * r) * w_ref[:, sl]).astype(BF16)

    row = pl.BlockSpec((ts, SSM_INNER), lambda i: (i, 0))
    return _pc(body, name, (s // ts,), [row, row, pl.BlockSpec((1, SSM_INNER), lambda i: (0, 0))], row,
               _sds((s, SSM_INNER), BF16))(y, proj, nw)


def _gnorm_bwd(y, proj, nw, dyn, name):
    s = y.shape[0]
    ts = _row_tile(s)
    gw = SSM_INNER // SSM_GROUPS

    def body(y_ref, z_ref, w_ref, d_ref, dy_ref, dz_ref, dw_ref):
        @pl.when(pl.program_id(0) == 0)
        def _():
            dw_ref[...] = jnp.zeros_like(dw_ref)

        for g in range(SSM_GROUPS):
            sl = slice(g * gw, (g + 1) * gw)
            z = z_ref[:, sl]
            yv = y_ref[:, sl]
            sg = _sigmoid(z)
            sz = z * sg
            y2 = yv * sz
            r = lax.rsqrt(jnp.mean(y2 * y2, axis=-1, keepdims=True) + RMS_EPS)
            yn = y2 * r
            dout = d_ref[:, sl]
            dyg = dout * w_ref[:, sl]
            dy2 = r * (dyg - yn * jnp.mean(dyg * yn, axis=-1, keepdims=True))
            dy_ref[:, sl] = dy2 * sz
            dz_ref[:, sl] = (dy2 * yv * (sg * (1.0 + z * (1.0 - sg)))).astype(BF16)
            dw_ref[:, sl] += jnp.sum(dout * yn, axis=0, keepdims=True)

    row = pl.BlockSpec((ts, SSM_INNER), lambda i: (i, 0))
    vec = pl.BlockSpec((1, SSM_INNER), lambda i: (0, 0))
    return _pc(body, name, (s // ts,), [row, row, vec, row], [row, row, vec],
               [_sds((s, SSM_INNER)), _sds((s, SSM_INNER), BF16), _sds((1, SSM_INNER))])(y, proj, nw, dyn)


def _head_layouts(v, s):
    return v.reshape(s, SSM_GROUPS, 4).transpose(1, 0, 2), v.T.reshape(SSM_GROUPS, 4, s)


def _ssd_fwd(x, norm_w, w_in, conv_w, conv_b, dt_bias, a_log, d_skip, gnorm_w, w_out):
    s = x.shape[0]
    h = _rms_fwd(x, norm_w, "ssd_norm")
    proj = _mm(h, w_in, "nn", "ssd_in")
    xbc = _ssd_conv_fwd(proj, conv_w, conv_b, "ssd_conv")
    pad = ((0, 0), (0, LANES - SSM_HEADS))
    a_neg = -jnp.exp(a_log)
    bias = jnp.pad(dt_bias, pad)
    dt, acum = _ssd_dt_fwd(proj, bias, jnp.pad(a_neg, pad), "ssd_dt")
    dtc, drow = _head_layouts(dt[:, :SSM_HEADS], s)
    acol, arow = _head_layouts(acum[:, :SSM_HEADS], s)
    dskip = jnp.repeat(d_skip.reshape(SSM_GROUPS, 1, 4), HEAD_DIM, axis=2)
    y, hs = _ssd_scan_fwd(xbc, dtc, acol, drow, arow, dskip, "ssd_scan")
    yn = _gnorm_fwd(y, proj, gnorm_w, "ssd_gnorm")
    out = _mm(yn, w_out, "nn", "ssd_out", res=x)
    return out, (x, h, proj, xbc, bias, a_neg, dtc, acol, drow, arow, dskip, y, hs, yn)


def _ssd_bwd(dout, saved, norm_w, w_in, conv_w, conv_b, gnorm_w, w_out):
    x, h, proj, xbc, bias, a_neg, dtc, acol, drow, arow, dskip, y, hs, yn = saved
    s = x.shape[0]
    dyn = _mm(dout, w_out, "nt", "ssd_dyn")
    g_out = _mm(yn, dout, "tn", "ssd_gout")
    dy, dz, g_gnorm = _gnorm_bwd(y, proj, gnorm_w, dyn, "ssd_dgnorm")
    agrp = jnp.broadcast_to(a_neg.reshape(SSM_GROUPS, 4, 1), (SSM_GROUPS, 4, LANES))
    dxs, db, dc, ddt_row, da_acc, dd_acc = _ssd_scan_bwd(xbc, dtc, acol, drow, arow, dskip, agrp, hs, dy, "ssd_dscan")
    dxbc = jnp.concatenate([dxs, db, dc], axis=1)
    gact, g_cb = _ssd_conv_bwd_act(proj, dxbc, conv_w, conv_b, "ssd_dconv_act")
    dpre, g_cw = _ssd_conv_bwd_in(proj, gact, conv_w, "ssd_dconv_in")
    ddt = jnp.pad(ddt_row.reshape(SSM_HEADS, s).T, ((0, 0), (0, LANES - SSM_HEADS)))
    ddtraw, g_dtb = _ssd_dt_bwd(proj, bias, ddt, "ssd_ddt")
    dproj = jnp.concatenate([dz, dpre, ddtraw], axis=1)
    g_in = _mm(h, dproj, "tn", "ssd_gin")
    dh = _mm(dproj, w_in, "nt", "ssd_dh")
    dx, g_norm = _rms_bwd(x, norm_w, dh, dout, "ssd_dnorm")
    g_alog = jnp.sum(da_acc, axis=-1).reshape(1, SSM_HEADS)
    g_d = jnp.sum(dd_acc.reshape(SSM_GROUPS, 4, HEAD_DIM), axis=-1).reshape(1, SSM_HEADS)
    return dx, g_norm, g_in[:, :SSM_IN], g_cw, g_cb, g_dtb[:, :SSM_HEADS], g_alog, g_d, g_gnorm, g_out


def _loss_head(y, target, name):
    s, d = y.shape
    ts = _row_tile(s)

    def body(y_ref, t_ref, dy_ref, l_ref):
        @pl.when(pl.program_id(0) == 0)
        def _():
            l_ref[...] = jnp.zeros_like(l_ref)

        e = y_ref[...] - t_ref[...]
        dy_ref[...] = e * (1.0 / d)
        part = jnp.sum(jnp.sum(e * e, axis=-1, keepdims=True), axis=0, keepdims=True) * (0.5 / d)
        l_ref[...] += jnp.broadcast_to(part, l_ref.shape)

    row = pl.BlockSpec((ts, d), lambda i: (i, 0))
    dy, lacc = _pc(body, name, (s // ts,), [row, row], [row, pl.BlockSpec((8, LANES), lambda i: (0, 0))],
                   [_sds((s, d)), _sds((8, LANES))])(y, target)
    return lacc[0, 0], dy


def _local_step(x, target, w):
    saved = []
    for i in range(DEPTH):
        kind, j = i % 3, i // 3
        mn = w["mix_norm"][i:i + 1]
        if kind == 0:
            x, sv = _conv_fwd(x, mn, w["conv_w_in"][j], w["conv_w_dw"][j], w["conv_w_out"][j], str(i))
        elif kind == 1:
            x, sv = _fox_fwd(x, mn, w["fox_w_in"][j], w["fox_b_f"], w["fox_q_gain"], w["fox_k_gain"], w["fox_w_out"][j])
        else:
            x, sv = _ssd_fwd(x, mn, w["ssd_w_in"][j], w["ssd_conv_w"][j], w["ssd_conv_b"], w["ssd_dt_bias"],
                             w["ssd_a_log"], w["ssd_d"], w["ssd_norm_w"], w["ssd_w_out"][j])
        x, sf = _ffn_fwd(x, w["ffn_norm"][i:i + 1], w["ffn_w_gu"][i], w["ffn_w_down"][i], str(i))
        saved.append((sv, sf))
    loss, dx = _loss_head(x, target, "loss_head")
    g = {k: [None] * n for k, n in (("mix_norm", DEPTH), ("ffn_norm", DEPTH), ("ffn_w_gu", DEPTH), ("ffn_w_down", DEPTH),
                                    ("conv_w_in", 2), ("conv_w_dw", 2), ("conv_w_out", 2))}
    for i in reversed(range(DEPTH)):
        kind, j = i % 3, i // 3
        sv, sf = saved[i]
        dx, g["ffn_norm"][i], g["ffn_w_gu"][i], g["ffn_w_down"][i] = _ffn_bwd(
            dx, sf, w["ffn_norm"][i:i + 1], w["ffn_w_gu"][i], w["ffn_w_down"][i], str(i))
        mn = w["mix_norm"][i:i + 1]
        if kind == 0:
            dx, g["mix_norm"][i], g["conv_w_in"][j], g["conv_w_dw"][j], g["conv_w_out"][j] = _conv_bwd(
                dx, sv, mn, w["conv_w_in"][j], w["conv_w_dw"][j], w["conv_w_out"][j], str(i))
        elif kind == 1:
            (dx, g["mix_norm"][i], g["fox_w_in"], g["fox_b_f"], g["fox_q_gain"], g["fox_k_gain"],
             g["fox_w_out"]) = _fox_bwd(dx, sv, mn, w["fox_w_in"][j], w["fox_w_out"][j])
        else:
            (dx, g["mix_norm"][i], g["ssd_w_in"], g["ssd_conv_w"], g["ssd_conv_b"], g["ssd_dt_bias"], g["ssd_a_log"],
             g["ssd_d"], g["ssd_norm_w"], g["ssd_w_out"]) = _ssd_bwd(
                 dx, sv, mn, w["ssd_w_in"][j], w["ssd_conv_w"][j], w["ssd_conv_b"], w["ssd_norm_w"], w["ssd_w_out"][j])
    out = {}
    for k, v in g.items():
        if isinstance(v, list):
            v = jnp.concatenate(v, axis=0) if k in ("mix_norm", "ffn_norm") else jnp.stack(v, axis=0)
        out[k] = v
    for k in ("fox_w_in", "fox_w_out", "ssd_w_in", "ssd_conv_w", "ssd_w_out"):
        out[k] = out[k][None]
    return loss, dx, out


def _mesh_position():
    return lax.axis_index("x") * 4 + lax.axis_index("y") * 2 + lax.axis_index("c")


def _device_of(t):
    return (lax.shift_right_logical(t, 2), lax.bitwise_and(lax.shift_right_logical(t, 1), 1), lax.bitwise_and(t, 1))


def _exchange_copies(src_of, out_ref, send_sems, recv_sems, me):
    copies = []
    for j in range(1, NDEV):
        t = lax.rem(me + j, NDEV)
        copies.append(pltpu.make_async_remote_copy(
            src_ref=src_of(t), dst_ref=out_ref.at[me], send_sem=send_sems.at[j - 1], recv_sem=recv_sems.at[j - 1],
            device_id=_device_of(t), device_id_type=pl.DeviceIdType.MESH))
    return copies


def _exchange_run(src_of, out_ref, send_sems, recv_sems, local_sem):
    me = _mesh_position()
    local = pltpu.make_async_copy(src_of(me), out_ref.at[me], local_sem)
    local.start()
    copies = _exchange_copies(src_of, out_ref, send_sems, recv_sems, me)
    for cp in copies:
        cp.start()
    for cp in copies:
        cp.wait_send()
    for j in range(1, NDEV):
        frm = lax.rem(me + NDEV - j, NDEV)
        pltpu.make_async_remote_copy(
            src_ref=src_of(me), dst_ref=out_ref.at[frm], send_sem=send_sems.at[j - 1], recv_sem=recv_sems.at[j - 1],
            device_id=_device_of(frm), device_id_type=pl.DeviceIdType.MESH).wait_recv()
    local.wait()


def _all_gather(pack, name):
    def body(src_ref, out_ref, send_sems, recv_sems, local_sem):
        _exchange_run(lambda t: src_ref, out_ref, send_sems, recv_sems, local_sem)

    hbm = pl.BlockSpec(memory_space=pl.ANY)
    return pl.pallas_call(
        body, name=name, in_specs=[hbm], out_specs=hbm, out_shape=_sds((NDEV,) + pack.shape, pack.dtype),
        scratch_shapes=[pltpu.SemaphoreType.DMA((NDEV - 1,)), pltpu.SemaphoreType.DMA((NDEV - 1,)), pltpu.SemaphoreType.DMA(())])(pack)


def _all_to_all(slabs, name):
    def body(src_ref, out_ref, send_sems, recv_sems, local_sem):
        _exchange_run(lambda t: src_ref.at[t], out_ref, send_sems, recv_sems, local_sem)

    hbm = pl.BlockSpec(memory_space=pl.ANY)
    return pl.pallas_call(
        body, name=name, in_specs=[hbm], out_specs=hbm, out_shape=_sds(slabs.shape, slabs.dtype),
        scratch_shapes=[pltpu.SemaphoreType.DMA((NDEV - 1,)), pltpu.SemaphoreType.DMA((NDEV - 1,)), pltpu.SemaphoreType.DMA(())])(slabs)


def _all_sum_small(pack, name):
    def body(src_ref, out_ref, buf_ref, send_sems, recv_sems, local_sem):
        _exchange_run(lambda t: src_ref, buf_ref, send_sems, recv_sems, local_sem)
        acc = buf_ref[0]
        for d in range(1, NDEV):
            acc = acc + buf_ref[d]
        out_ref[...] = acc

    vmem = pl.BlockSpec(memory_space=pltpu.VMEM)
    return pl.pallas_call(
        body, name=name, in_specs=[vmem], out_specs=vmem, out_shape=_sds(pack.shape, pack.dtype),
        scratch_shapes=[pltpu.VMEM((NDEV,) + pack.shape, pack.dtype), pltpu.SemaphoreType.DMA((NDEV - 1,)),
                        pltpu.SemaphoreType.DMA((NDEV - 1,)), pltpu.SemaphoreType.DMA(())])(pack)


def _sum_slabs(slabs, name):
    _, r, c = slabs.shape
    tr = 400 if r % 400 == 0 else r

    def body(s_ref, o_ref):
        acc = s_ref[0].astype(F32)
        for d in range(1, NDEV):
            acc = acc + s_ref[d].astype(F32)
        o_ref[...] = acc

    return _pc(body, name, (r // tr,), [pl.BlockSpec((NDEV, tr, c), lambda i: (0, i, 0))],
               pl.BlockSpec((tr, c), lambda i: (i, 0)), _sds((r, c)))(slabs)


def _adamw(wt, g, m, v, name):
    shape = wt.shape
    w2, g2, m2, v2 = (a.reshape(-1, shape[-1]) for a in (wt, g, m, v))
    r, c = w2.shape
    tr = r
    for cand in (512, 352, 256):
        if r % cand == 0:
            tr = cand
            break
    c1 = 1.0 - ADAM_B1 ** ADAM_STEP
    c2 = 1.0 - ADAM_B2 ** ADAM_STEP

    def body(w_ref, g_ref, m_ref, v_ref, d_ref, mo_ref, vo_ref):
        gv = g_ref[...]
        mn = ADAM_B1 * m_ref[...] + (1.0 - ADAM_B1) * gv
        vn = ADAM_B2 * v_ref[...] + (1.0 - ADAM_B2) * (gv * gv)
        mo_ref[...] = mn
        vo_ref[...] = vn
        d_ref[...] = -ADAM_LR * ((mn / c1) / (jnp.sqrt(vn / c2) + ADAM_EPS) + ADAM_WD * w_ref[...])

    spec = pl.BlockSpec((tr, c), lambda i: (i, 0))
    outs = _pc(body, name, (r // tr,), [spec] * 4, [spec] * 3, [_sds((r, c))] * 3)(w2, g2, m2, v2)
    return tuple(o.reshape(shape) for o in outs)


_NAMES = ["mix_norm", "ffn_norm", "ffn_w_gu", "ffn_w_down", "conv_w_in", "conv_w_dw", "conv_w_out", "fox_w_in", "fox_b_f",
          "fox_q_gain", "fox_k_gain", "fox_w_out", "ssd_w_in", "ssd_conv_w", "ssd_conv_b", "ssd_dt_bias", "ssd_a_log",
          "ssd_d", "ssd_norm_w", "ssd_w_out"]
_MATRICES = {"ffn_w_gu": 2, "ffn_w_down": 1, "conv_w_in": 2, "conv_w_out": 1, "fox_w_in": 2, "fox_w_out": 1,
             "ssd_w_in": 2, "ssd_w_out": 1}
_VECTORS = {"conv_w_dw": 2, "ssd_conv_w": 2, "ssd_conv_b": 1, "ssd_norm_w": 1}
_REPLICATED = ["mix_norm", "ffn_norm", "fox_b_f", "fox_q_gain", "fox_k_gain", "ssd_dt_bias", "ssd_a_log", "ssd_d"]
PACK_ROW_ALIGN = 16


def _to_rows(flat, align):
    n = flat.shape[0]
    rows = -(-n // D_MODEL)
    rows = -(-rows // align) * align
    return jnp.pad(flat, (0, rows * D_MODEL - n)).reshape(rows, D_MODEL)


def _full_from_gathered(g, axis):
    g = jnp.moveaxis(g, 0, axis)
    shp = list(g.shape)
    return g.reshape(shp[:axis] + [shp[axis] * shp[axis + 1]] + shp[axis + 2:])


def _slabs_from_full(full, axis):
    shp = list(full.shape)
    full = full.reshape(shp[:axis] + [NDEV, shp[axis] // NDEV] + shp[axis + 1:])
    return jnp.moveaxis(full, axis, 0)


def _f32_as_bf16(v):
    return lax.bitcast_convert_type(v, BF16).reshape(-1)


def _bf16_as_f32(v):
    return lax.bitcast_convert_type(v.reshape(-1, 2), F32)


def kernel(x, mix_norm, ffn_norm, ffn_w_gu, ffn_w_down, conv_w_in, conv_w_dw, conv_w_out, fox_w_in, fox_b_f, fox_q_gain, fox_k_gain, fox_w_out, ssd_w_in, ssd_conv_w, ssd_conv_b, ssd_dt_bias, ssd_a_log, ssd_d, ssd_norm_w, ssd_w_out, loss_target, m_mix_norm, m_ffn_norm, m_ffn_w_gu, m_ffn_w_down, m_conv_w_in, m_conv_w_dw, m_conv_w_out, m_fox_w_in, m_fox_b_f, m_fox_q_gain, m_fox_k_gain, m_fox_w_out, m_ssd_w_in, m_ssd_conv_w, m_ssd_conv_b, m_ssd_dt_bias, m_ssd_a_log, m_ssd_d, m_ssd_norm_w, m_ssd_w_out, v_mix_norm, v_ffn_norm, v_ffn_w_gu, v_ffn_w_down, v_conv_w_in, v_conv_w_dw, v_conv_w_out, v_fox_w_in, v_fox_b_f, v_fox_q_gain, v_fox_k_gain, v_fox_w_out, v_ssd_w_in, v_ssd_conv_w, v_ssd_conv_b, v_ssd_dt_bias, v_ssd_a_log, v_ssd_d, v_ssd_norm_w, v_ssd_w_out):
    local = dict(mix_norm=mix_norm, ffn_norm=ffn_norm, ffn_w_gu=ffn_w_gu, ffn_w_down=ffn_w_down, conv_w_in=conv_w_in,
                 conv_w_dw=conv_w_dw, conv_w_out=conv_w_out, fox_w_in=fox_w_in, fox_b_f=fox_b_f, fox_q_gain=fox_q_gain,
                 fox_k_gain=fox_k_gain, fox_w_out=fox_w_out, ssd_w_in=ssd_w_in, ssd_conv_w=ssd_conv_w, ssd_conv_b=ssd_conv_b,
                 ssd_dt_bias=ssd_dt_bias, ssd_a_log=ssd_a_log, ssd_d=ssd_d, ssd_norm_w=ssd_norm_w, ssd_w_out=ssd_w_out)
    mom = dict(zip(_NAMES, [m_mix_norm, m_ffn_norm, m_ffn_w_gu, m_ffn_w_down, m_conv_w_in, m_conv_w_dw, m_conv_w_out, m_fox_w_in,
                            m_fox_b_f, m_fox_q_gain, m_fox_k_gain, m_fox_w_out, m_ssd_w_in, m_ssd_conv_w, m_ssd_conv_b,
                            m_ssd_dt_bias, m_ssd_a_log, m_ssd_d, m_ssd_norm_w, m_ssd_w_out]))
    var = dict(zip(_NAMES, [v_mix_norm, v_ffn_norm, v_ffn_w_gu, v_ffn_w_down, v_conv_w_in, v_conv_w_dw, v_conv_w_out, v_fox_w_in,
                            v_fox_b_f, v_fox_q_gain, v_fox_k_gain, v_fox_w_out, v_ssd_w_in, v_ssd_conv_w, v_ssd_conv_b,
                            v_ssd_dt_bias, v_ssd_a_log, v_ssd_d, v_ssd_norm_w, v_ssd_w_out]))

    pieces = [local[k].astype(BF16).reshape(-1) for k in _MATRICES] + [_f32_as_bf16(local[k]) for k in _VECTORS]
    sizes = [p.shape[0] for p in pieces]
    gathered = _all_gather(_to_rows(jnp.concatenate(pieces), PACK_ROW_ALIGN), "gather_weights").reshape(NDEV, -1)
    full = {k: local[k] for k in _REPLICATED}
    off = 0
    for (k, axis), n in zip(list(_MATRICES.items()) + list(_VECTORS.items()), sizes):
        blk = gathered[:, off:off + n]
        off += n
        if k in _VECTORS:
            blk = _bf16_as_f32(blk)
        full[k] = _full_from_gathered(blk.reshape((NDEV,) + local[k].shape), axis)
    full["fox_w_in"] = jnp.pad(full["fox_w_in"], ((0, 0), (0, 0), (0, FOX_IN_PAD - FOX_IN)))
    full["ssd_w_in"] = jnp.pad(full["ssd_w_in"], ((0, 0), (0, 0), (0, SSM_IN_PAD - SSM_IN)))
    full["ssd_conv_b"] = full["ssd_conv_b"].reshape(1, SSM_CONV_DIM)
    full["ssd_norm_w"] = full["ssd_norm_w"].reshape(1, SSM_INNER)

    loss_part, dx, grads = _local_step(x[0], loss_target[0], full)

    slabs = [_slabs_from_full(grads[k], axis).astype(BF16).reshape(NDEV, -1) for k, axis in _MATRICES.items()]
    msizes = [sl.shape[1] for sl in slabs]
    slab_pack = jnp.concatenate(slabs, axis=1)
    rows = -(-slab_pack.shape[1] // (D_MODEL * 400)) * 400
    slab_pack = jnp.pad(slab_pack, ((0, 0), (0, rows * D_MODEL - slab_pack.shape[1]))).reshape(NDEV, rows, D_MODEL)
    summed = _sum_slabs(_all_to_all(slab_pack, "scatter_grads"), "sum_grads").reshape(-1)
    shard_grad = {}
    off = 0
    for k, n in zip(_MATRICES, msizes):
        shard_grad[k] = summed[off:off + n].reshape(local[k].shape)
        off += n

    small_names = _REPLICATED + list(_VECTORS)
    small = [jnp.reshape(loss_part, (1,))] + [grads[k].reshape(-1) for k in small_names]
    ssizes = [p.shape[0] for p in small]
    total = _all_sum_small(_to_rows(jnp.concatenate(small), 8), "sum_small").reshape(-1)
    loss = total[0]
    off = 1
    me = _mesh_position()
    for k, n in zip(small_names, ssizes[1:]):
        gk = total[off:off + n]
        off += n
        if k in _VECTORS:
            gfull = gk.reshape(_full_shape(local[k].shape, _VECTORS[k]))
            shard_grad[k] = lax.dynamic_index_in_dim(_slabs_from_full(gfull, _VECTORS[k]), me, 0, keepdims=False)
        else:
            shard_grad[k] = gk.reshape(local[k].shape)

    deltas, new_m, new_v = {}, {}, {}
    for k in _NAMES:
        deltas[k], new_m[k], new_v[k] = _adamw(local[k], shard_grad[k], mom[k], var[k], f"adamw_{k}")
    return (loss, dx[None], *[shard_grad[k] for k in _NAMES], *[deltas[k] for k in _NAMES],
            *[new_m[k] for k in _NAMES], *[new_v[k] for k in _NAMES])


def _full_shape(local_shape, axis):
    shp = list(local_shape)
    shp[axis] *= NDEV
    return tuple(shp)
```

```python
import numpy as np

import jax
import jax.numpy as jnp
from jax import lax
from jax.experimental import pallas as pl
from jax.experimental.pallas import tpu as pltpu

F32 = jnp.float32
BF16 = jnp.bfloat16
HI = lax.Precision.HIGHEST

NDEV = 8
D_MODEL = 1024
DEPTH = 4
D_FF = 2816
FF_BLOCK = 2 * D_FF // NDEV
CONV_BLOCK = 3 * D_MODEL // NDEV
RMS_EPS = 1e-6
HEAD_DIM = 64
ATTN_HEADS = 16
FOX_IN = 3 * D_MODEL + ATTN_HEADS
FOX_IN_PAD = 3200
SSM_INNER = 2048
SSM_HEADS = 32
SSM_GROUPS = 8
SSM_STATE = 128
SSM_CHUNK = 128
SSM_CONV_DIM = 4096
SSM_IN = SSM_INNER + SSM_CONV_DIM + SSM_HEADS
SSM_IN_PAD = 6272
LANES = 128
V7X_VMEM_BYTES = 64 * 1024 * 1024
VMEM_LIMIT_BYTES = (V7X_VMEM_BYTES * 3) // 4
LOG2E = 1.4426950408889634
LN2 = 0.6931471805599453
ATTN_ROWS = 64

ADAM_LR = 0.001
ADAM_B1 = 0.9
ADAM_B2 = 0.999
ADAM_EPS = 1e-08
ADAM_WD = 0.01
ADAM_STEP = 10

_TILE_CANDIDATES = (1024, 1408, 896, 768, 640, 512, 384, 256, 128)


def _pick_tile(n):
    for c in _TILE_CANDIDATES:
        if n % c == 0:
            return c
    raise ValueError(f"no tile for {n}")


def _params(ngrid):
    return pltpu.CompilerParams(dimension_semantics=("arbitrary",) * ngrid, vmem_limit_bytes=VMEM_LIMIT_BYTES)


def _pc(body, name, grid, in_specs, out_specs, out_shape, scratch=()):
    return pl.pallas_call(
        body, name=name, grid=grid, in_specs=in_specs, out_specs=out_specs, out_shape=out_shape,
        scratch_shapes=list(scratch), compiler_params=_params(len(grid)))


def _dot(a, b, ca, cb, prec=None):
    return lax.dot_general(a, b, (((ca,), (cb,)), ((), ())), preferred_element_type=F32, precision=prec)


def _sds(shape, dtype=F32):
    return jax.ShapeDtypeStruct(shape, dtype)


def _row_tile(s, want=256):
    return want if s % want == 0 else s


def _sigmoid(x):
    return 1.0 / (1.0 + jnp.exp(-x))


def _softplus(x):
    return jnp.maximum(x, 0.0) + jnp.log(1.0 + jnp.exp(-jnp.abs(x)))


def _mm_spec(a, b, name, grid, a_spec, b_spec, o_spec, out, ca, cb, acc_shape, drop=(0, 0, 0), res=None, r_spec=None):
    nk = grid[2]
    da, db, do_ = drop
    has_res = res is not None

    def body(*refs):
        if has_res:
            a_ref, b_ref, r_ref, o_ref, acc_ref = refs
        else:
            a_ref, b_ref, o_ref, acc_ref = refs
        k = pl.program_id(2)

        @pl.when(k == 0)
        def _():
            acc_ref[...] = jnp.zeros_like(acc_ref)

        av = a_ref[(0,) * da] if da else a_ref[...]
        bv = b_ref[(0,) * db] if db else b_ref[...]
        acc_ref[...] += _dot(av.astype(BF16), bv.astype(BF16), ca, cb)

        @pl.when(k == nk - 1)
        def _():
            val = acc_ref[...]
            if has_res:
                val = val + r_ref[...]
            if do_:
                o_ref[(0,) * do_] = val.astype(out.dtype)
            else:
                o_ref[...] = val.astype(out.dtype)

    in_specs = [a_spec, b_spec] + ([r_spec] if has_res else [])
    args = (a, b) + ((res,) if has_res else ())
    return _pc(body, name, grid, in_specs, o_spec, out, [pltpu.VMEM(acc_shape, F32)])(*args)


def _mm(a, b, mode, name, out_dtype=F32, res=None):
    if mode == "tn":
        r, m = a.shape
        n = b.shape[1]
        tm, tn, tk = _pick_tile(m), _pick_tile(n), _pick_tile(r)
        grid = (m // tm, n // tn, r // tk)
        a_spec = pl.BlockSpec((tk, tm), lambda i, j, k: (k, i))
        b_spec = pl.BlockSpec((tk, tn), lambda i, j, k: (k, j))
        ca, cb = 0, 0
    else:
        m, kd = a.shape
        n = b.shape[1] if mode == "nn" else b.shape[0]
        tm, tn, tk = _pick_tile(m), _pick_tile(n), _pick_tile(kd)
        grid = (m // tm, n // tn, kd // tk)
        a_spec = pl.BlockSpec((tm, tk), lambda i, j, k: (i, k))
        if mode == "nn":
            b_spec = pl.BlockSpec((tk, tn), lambda i, j, k: (k, j))
            ca, cb = 1, 0
        else:
            b_spec = pl.BlockSpec((tn, tk), lambda i, j, k: (j, k))
            ca, cb = 1, 1
    o_spec = pl.BlockSpec((tm, tn), lambda i, j, k: (i, j))
    return _mm_spec(a, b, name, grid, a_spec, b_spec, o_spec, _sds((m, n), out_dtype), ca, cb, (tm, tn), res=res, r_spec=o_spec)


def _rms_fwd(x, w, name):
    s, d = x.shape
    ts = _row_tile(s)

    def body(x_ref, w_ref, o_ref):
        xv = x_ref[...]
        r = lax.rsqrt(jnp.mean(xv * xv, axis=-1, keepdims=True) + RMS_EPS)
        o_ref[...] = ((xv * r) * w_ref[...]).astype(BF16)

    row = pl.BlockSpec((ts, d), lambda i: (i, 0))
    return _pc(body, name, (s // ts,), [row, pl.BlockSpec((1, d), lambda i: (0, 0))], row, _sds((s, d), BF16))(x, w)


def _rms_bwd(x, w, dh, dres, name):
    s, d = x.shape
    ts = _row_tile(s)

    def body(x_ref, w_ref, dh_ref, dr_ref, dx_ref, dw_ref):
        i = pl.program_id(0)
        xv = x_ref[...]
        r = lax.rsqrt(jnp.mean(xv * xv, axis=-1, keepdims=True) + RMS_EPS)
        xhat = xv * r
        dhv = dh_ref[...]
        g = dhv * w_ref[...]
        dx_ref[...] = dr_ref[...] + r * (g - xhat * jnp.mean(g * xhat, axis=-1, keepdims=True))

        @pl.when(i == 0)
        def _():
            dw_ref[...] = jnp.zeros_like(dw_ref)

        dw_ref[...] += jnp.sum(dhv * xhat, axis=0, keepdims=True)

    row = pl.BlockSpec((ts, d), lambda i: (i, 0))
    vec = pl.BlockSpec((1, d), lambda i: (0, 0))
    return _pc(body, name, (s // ts,), [row, vec, row, row], [row, vec], [_sds((s, d)), _sds((1, d))])(x, w, dh, dres)


def _swiglu_fwd(gu, name):
    s = gu.shape[2]
    ts = _row_tile(s)

    def body(gu_ref, o_ref):
        g = gu_ref[0, 0]
        u = gu_ref[0, 1]
        o_ref[0] = (g * _sigmoid(g) * u).astype(BF16)

    return _pc(body, name, (s // ts, 4), [pl.BlockSpec((1, 2, ts, FF_BLOCK), lambda i, k: (k, 0, i, 0))],
               pl.BlockSpec((1, ts, FF_BLOCK), lambda i, k: (k, i, 0)), _sds((4, s, FF_BLOCK), BF16))(gu)


def _swiglu_bwd(gu, da, name):
    s = gu.shape[2]
    ts = _row_tile(s)

    def body(gu_ref, da_ref, o_ref):
        g = gu_ref[0, 0]
        u = gu_ref[0, 1]
        dav = da_ref[0]
        sg = _sigmoid(g)
        o_ref[0, 0] = (dav * u * (sg * (1.0 + g * (1.0 - sg)))).astype(BF16)
        o_ref[0, 1] = (dav * (g * sg)).astype(BF16)

    pair = pl.BlockSpec((1, 2, ts, FF_BLOCK), lambda i, k: (k, 0, i, 0))
    return _pc(body, name, (s // ts, 4), [pair, pl.BlockSpec((1, ts, FF_BLOCK), lambda i, k: (k, i, 0))], pair,
               _sds((4, 2, s, FF_BLOCK), BF16))(gu, da)


def _ffn_fwd(x, norm_w, w_gu, w_down, tag):
    s = x.shape[0]
    tm = _pick_tile(s)
    h = _rms_fwd(x, norm_w, f"ffn_norm_{tag}")
    gu = _mm_spec(h, w_gu, f"ffn_gu_{tag}", (s // tm, NDEV, 1),
                  pl.BlockSpec((tm, D_MODEL), lambda i, j, k: (i, 0)),
                  pl.BlockSpec((1, D_MODEL, FF_BLOCK), lambda i, j, k: (j, 0, 0)),
                  pl.BlockSpec((1, 1, tm, FF_BLOCK), lambda i, j, k: (j % 4, j // 4, i, 0)),
                  _sds((4, 2, s, FF_BLOCK)), 1, 0, (tm, FF_BLOCK), drop=(0, 1, 2))
    a = _swiglu_fwd(gu, f"ffn_act_{tag}")
    xspec = pl.BlockSpec((tm, D_MODEL), lambda i, j, k: (i, 0))
    y = _mm_spec(a, w_down, f"ffn_down_{tag}", (s // tm, 1, 4),
                 pl.BlockSpec((1, tm, FF_BLOCK), lambda i, j, k: (k, i, 0)),
                 pl.BlockSpec((1, FF_BLOCK, D_MODEL), lambda i, j, k: (k, 0, 0)),
                 xspec, _sds((s, D_MODEL)), 1, 0, (tm, D_MODEL), drop=(1, 1, 0), res=x, r_spec=xspec)
    return y, (x, h, gu, a)


def _ffn_bwd(dy, saved, norm_w, w_gu, w_down, tag):
    x, h, gu, a = saved
    s = x.shape[0]
    tm = _pick_tile(s)
    row = pl.BlockSpec((tm, D_MODEL), lambda i, j, k: (i, 0))
    da = _mm_spec(dy, w_down, f"ffn_dact_{tag}", (s // tm, 4, 1), row,
                  pl.BlockSpec((1, FF_BLOCK, D_MODEL), lambda i, j, k: (j, 0, 0)),
                  pl.BlockSpec((1, tm, FF_BLOCK), lambda i, j, k: (j, i, 0)),
                  _sds((4, s, FF_BLOCK)), 1, 1, (tm, FF_BLOCK), drop=(0, 1, 1))
    g_down = _mm_spec(a, dy, f"ffn_gdown_{tag}", (4, 1, s // tm),
                      pl.BlockSpec((1, tm, FF_BLOCK), lambda i, j, k: (i, k, 0)),
                      pl.BlockSpec((tm, D_MODEL), lambda i, j, k: (k, 0)),
                      pl.BlockSpec((1, FF_BLOCK, D_MODEL), lambda i, j, k: (i, 0, 0)),
                      _sds((4, FF_BLOCK, D_MODEL), BF16), 0, 0, (FF_BLOCK, D_MODEL), drop=(1, 0, 1))
    dgu = _swiglu_bwd(gu, da, f"ffn_dgu_{tag}")
    g_gu = _mm_spec(h, dgu, f"ffn_ggu_{tag}", (NDEV, 1, s // tm),
                    pl.BlockSpec((tm, D_MODEL), lambda i, j, k: (k, 0)),
                    pl.BlockSpec((1, 1, tm, FF_BLOCK), lambda i, j, k: (i % 4, i // 4, k, 0)),
                    pl.BlockSpec((1, D_MODEL, FF_BLOCK), lambda i, j, k: (i, 0, 0)),
                    _sds((NDEV, D_MODEL, FF_BLOCK), BF16), 0, 0, (D_MODEL, FF_BLOCK), drop=(0, 2, 1))
    dh = _mm_spec(dgu, w_gu, f"ffn_dh_{tag}", (s // tm, 1, NDEV),
                  pl.BlockSpec((1, 1, tm, FF_BLOCK), lambda i, j, k: (k % 4, k // 4, i, 0)),
                  pl.BlockSpec((1, D_MODEL, FF_BLOCK), lambda i, j, k: (k, 0, 0)),
                  row, _sds((s, D_MODEL)), 1, 1, (tm, D_MODEL), drop=(2, 1, 0))
    dx, g_norm = _rms_bwd(x, norm_w, dh, dy, f"ffn_dnorm_{tag}")
    return dx, g_norm, g_gu, g_down


def _prev_rows(cur, halo, j, first):
    rid = lax.broadcasted_iota(jnp.int32, cur.shape, 0)
    hid = lax.broadcasted_iota(jnp.int32, halo.shape, 0)
    out = pltpu.roll(cur, j, 0)
    for t in range(j):
        row = jnp.sum(jnp.where(hid == 8 - j + t, halo, 0.0), axis=0, keepdims=True)
        row = jnp.where(first, 0.0, row)
        out = jnp.where(rid == t, row, out)
    return out


def _next_rows(cur, halo, j, last):
    ts = cur.shape[0]
    rid = lax.broadcasted_iota(jnp.int32, cur.shape, 0)
    hid = lax.broadcasted_iota(jnp.int32, halo.shape, 0)
    out = pltpu.roll(cur, ts - j, 0)
    for t in range(j):
        row = jnp.sum(jnp.where(hid == t, halo, 0.0), axis=0, keepdims=True)
        row = jnp.where(last, 0.0, row)
        out = jnp.where(rid == ts - j + t, row, out)
    return out


def _halo_specs(ts, s, width, col):
    per = ts // 8
    nblk = s // 8
    prev = pl.BlockSpec((8, width), lambda i: (jnp.maximum(i * per - 1, 0), col))
    nxt = pl.BlockSpec((8, width), lambda i: (jnp.minimum((i + 1) * per, nblk - 1), col))
    return prev, nxt


def _cgate_fwd(p, w_dw, name):
    s = p.shape[0]
    d = D_MODEL
    ts = _row_tile(s)
    prev, _ = _halo_specs(ts, s, 3 * d, 0)

    def body(p_ref, h_ref, w_ref, z_ref):
        first = pl.program_id(0) == 0
        b = p_ref[:, :d]
        cv = p_ref[:, d:2 * d] * p_ref[:, 2 * d:]
        hcv = h_ref[:, d:2 * d] * h_ref[:, 2 * d:]
        u = w_ref[2:3, :] * cv + w_ref[1:2, :] * _prev_rows(cv, hcv, 1, first) + w_ref[0:1, :] * _prev_rows(cv, hcv, 2, first)
        z_ref[...] = (b * u).astype(BF16)

    return _pc(body, name, (s // ts,),
               [pl.BlockSpec((ts, 3 * d), lambda i: (i, 0)), prev, pl.BlockSpec((3, d), lambda i: (0, 0))],
               pl.BlockSpec((ts, d), lambda i: (i, 0)), _sds((s, d), BF16))(p, p, w_dw)


def _cgate_bwd(p, dz, w_dw, name):
    s = p.shape[0]
    d = D_MODEL
    ts = _row_tile(s)
    nt = s // ts
    p_prev, p_next = _halo_specs(ts, s, 3 * d, 0)
    _, dz_next = _halo_specs(ts, s, d, 0)

    def body(p_ref, hp_ref, hn_ref, dz_ref, dzn_ref, w_ref, dp_ref, dw_ref):
        i = pl.program_id(0)
        first = i == 0
        last = i == nt - 1
        b = p_ref[:, :d]
        c = p_ref[:, d:2 * d]
        v = p_ref[:, 2 * d:]
        cv = c * v
        hcv = hp_ref[:, d:2 * d] * hp_ref[:, 2 * d:]
        cv1 = _prev_rows(cv, hcv, 1, first)
        cv2 = _prev_rows(cv, hcv, 2, first)
        w0, w1, w2 = w_ref[0:1, :], w_ref[1:2, :], w_ref[2:3, :]
        u = w2 * cv + w1 * cv1 + w0 * cv2
        dzv = dz_ref[...]
        du = dzv * b
        dun = dzn_ref[...] * hn_ref[:, :d]
        dcv = w2 * du + w1 * _next_rows(du, dun, 1, last) + w0 * _next_rows(du, dun, 2, last)
        dp_ref[:, :d] = (dzv * u).astype(BF16)
        dp_ref[:, d:2 * d] = (dcv * v).astype(BF16)
        dp_ref[:, 2 * d:] = (dcv * c).astype(BF16)

        @pl.when(first)
        def _():
            dw_ref[...] = jnp.zeros_like(dw_ref)

        dw_ref[0:1, :] += jnp.sum(du * cv2, axis=0, keepdims=True)
        dw_ref[1:2, :] += jnp.sum(du * cv1, axis=0, keepdims=True)
        dw_ref[2:3, :] += jnp.sum(du * cv, axis=0, keepdims=True)

    wide = pl.BlockSpec((ts, 3 * d), lambda i: (i, 0))
    wspec = pl.BlockSpec((3, d), lambda i: (0, 0))
    return _pc(body, name, (nt,),
               [wide, p_prev, p_next, pl.BlockSpec((ts, d), lambda i: (i, 0)), dz_next, wspec],
               [wide, wspec], [_sds((s, 3 * d), BF16), _sds((3, d))])(p, p, p, dz, dz, w_dw)


def _conv_fwd(x, norm_w, w_in, w_dw, w_out, tag):
    s = x.shape[0]
    tm = _pick_tile(s)
    h = _rms_fwd(x, norm_w, f"conv_norm_{tag}")
    p = _mm_spec(h, w_in, f"conv_in_{tag}", (s // tm, NDEV, 1),
                 pl.BlockSpec((tm, D_MODEL), lambda i, j, k: (i, 0)),
                 pl.BlockSpec((1, D_MODEL, CONV_BLOCK), lambda i, j, k: (j, 0, 0)),
                 pl.BlockSpec((tm, CONV_BLOCK), lambda i, j, k: (i, j)),
                 _sds((s, 3 * D_MODEL)), 1, 0, (tm, CONV_BLOCK), drop=(0, 1, 0))
    z = _cgate_fwd(p, w_dw, f"conv_gate_{tag}")
    y = _mm(z, w_out, "nn", f"conv_out_{tag}", res=x)
    return y, (x, h, p, z)


def _conv_bwd(dy, saved, norm_w, w_in, w_dw, w_out, tag):
    x, h, p, z = saved
    s = x.shape[0]
    tm = _pick_tile(s)
    dz = _mm(dy, w_out, "nt", f"conv_dz_{tag}")
    g_out = _mm(z, dy, "tn", f"conv_gout_{tag}", out_dtype=BF16)
    dp, g_dw = _cgate_bwd(p, dz, w_dw, f"conv_dgate_{tag}")
    g_in = _mm_spec(h, dp, f"conv_gin_{tag}", (NDEV, 1, s // tm),
                    pl.BlockSpec((tm, D_MODEL), lambda i, j, k: (k, 0)),
                    pl.BlockSpec((tm, CONV_BLOCK), lambda i, j, k: (k, i)),
                    pl.BlockSpec((1, D_MODEL, CONV_BLOCK), lambda i, j, k: (i, 0, 0)),
                    _sds((NDEV, D_MODEL, CONV_BLOCK), BF16), 0, 0, (D_MODEL, CONV_BLOCK), drop=(0, 0, 1))
    dh = _mm_spec(dp, w_in, f"conv_dh_{tag}", (s // tm, 1, NDEV),
                  pl.BlockSpec((tm, CONV_BLOCK), lambda i, j, k: (i, k)),
                  pl.BlockSpec((1, D_MODEL, CONV_BLOCK), lambda i, j, k: (k, 0, 0)),
                  pl.BlockSpec((tm, D_MODEL), lambda i, j, k: (i, 0)),
                  _sds((s, D_MODEL)), 1, 1, (tm, D_MODEL), drop=(0, 1, 0))
    dx, g_norm = _rms_bwd(x, norm_w, dh, dy, f"conv_dnorm_{tag}")
    return dx, g_norm, g_in, g_dw, g_out


def _tri(lower):
    r = lax.broadcasted_iota(jnp.int32, (LANES, LANES), 0)
    c = lax.broadcasted_iota(jnp.int32, (LANES, LANES), 1)
    return jnp.where((r >= c) if lower else (r <= c), 1.0, 0.0).astype(F32)


def _cumsum_rows(v, reverse, name):
    s = v.shape[0]
    n = s // LANES
    idx = (lambda i: (n - 1 - i, 0)) if reverse else (lambda i: (i, 0))

    def body(v_ref, o_ref, carry_ref):
        @pl.when(pl.program_id(0) == 0)
        def _():
            carry_ref[...] = jnp.zeros_like(carry_ref)

        blk = v_ref[...]
        o_ref[...] = _dot(_tri(not reverse), blk, 1, 0, HI) + carry_ref[0:1, :]
        carry_ref[...] += jnp.sum(blk, axis=0, keepdims=True)

    spec = pl.BlockSpec((LANES, LANES), idx)
    return _pc(body, name, (n,), [spec], spec, _sds((s, LANES)), [pltpu.VMEM((8, LANES), F32)])(v)


def _lo_mask(shape):
    return lax.broadcasted_iota(jnp.int32, shape, len(shape) - 1) < HEAD_DIM


def _half_sums(v, lo):
    sa = jnp.sum(jnp.where(lo, v, 0.0), axis=-1, keepdims=True)
    sb = jnp.sum(jnp.where(lo, 0.0, v), axis=-1, keepdims=True)
    return jnp.where(lo, sa, sb)


def _fox_prep_fwd(proj, gq, gk, name):
    s = proj.shape[0]
    ts = _row_tile(s, 512)
    qscale = HEAD_DIM ** -0.5 * LOG2E

    def body(q_ref, k_ref, v_ref, gq_ref, gk_ref, qo_ref, ko_ref, vo_ref):
        lo = _lo_mask((ts, LANES))

        def hnorm(xv, g):
            ms = _half_sums(xv * xv, lo) * (1.0 / HEAD_DIM)
            return (xv * lax.rsqrt(ms + RMS_EPS)) * g

        qo_ref[...] = (hnorm(q_ref[...], gq_ref[...]) * qscale).astype(BF16)
        ko_ref[...] = hnorm(k_ref[...], gk_ref[...]).astype(BF16)
        vo_ref[...] = v_ref[...].astype(BF16)

    def col(off):
        return pl.BlockSpec((ts, LANES), lambda i, p: (i, off + p))

    gspec = pl.BlockSpec((1, LANES), lambda i, p: (0, 0))
    out = _sds((s, D_MODEL), BF16)
    return _pc(body, name, (s // ts, 8), [col(0), col(8), col(16), gspec, gspec], [col(0)] * 3, [out] * 3)(
        proj, proj, proj, gq, gk)


def _fox_logf(proj, bf, name):
    s = proj.shape[0]
    ts = _row_tile(s, 512)

    def body(f_ref, b_ref, o_ref):
        z = f_ref[...] + b_ref[...]
        lf = jnp.minimum(z, 0.0) - jnp.log(1.0 + jnp.exp(-jnp.abs(z)))
        real = lax.broadcasted_iota(jnp.int32, (ts, LANES), 1) < ATTN_HEADS
        o_ref[...] = jnp.where(real, lf, 0.0)

    return _pc(body, name, (s // ts,), [pl.BlockSpec((ts, LANES), lambda i: (i, 24)), pl.BlockSpec((1, LANES), lambda i: (0, 0))],
               pl.BlockSpec((ts, LANES), lambda i: (i, 0)), _sds((s, LANES)))(proj, bf)


def _fox_dlogf(proj, bf, dlf, name):
    s = proj.shape[0]
    ts = _row_tile(s, 512)

    def body(f_ref, b_ref, d_ref, o_ref, db_ref):
        z = f_ref[...] + b_ref[...]
        real = lax.broadcasted_iota(jnp.int32, (ts, LANES), 1) < ATTN_HEADS
        g = jnp.where(real, d_ref[...] * _sigmoid(-z), 0.0)
        o_ref[...] = g.astype(BF16)

        @pl.when(pl.program_id(0) == 0)
        def _():
            db_ref[...] = jnp.zeros_like(db_ref)

        db_ref[...] += jnp.sum(g, axis=0, keepdims=True)

    vec = pl.BlockSpec((1, LANES), lambda i: (0, 0))
    row = pl.BlockSpec((ts, LANES), lambda i: (i, 0))
    return _pc(body, name, (s // ts,), [pl.BlockSpec((ts, LANES), lambda i: (i, 24)), vec, row], [row, vec],
               [_sds((s, LANES), BF16), _sds((1, LANES))])(proj, bf, dlf)


def _decay_terms(cum):
    s = cum.shape[0]
    c2 = cum * LOG2E
    hi = lax.reduce_precision(c2, 8, 7)
    mid = lax.reduce_precision(c2 - hi, 8, 7)
    low = lax.reduce_precision(c2 - hi - mid, 8, 7)
    one = jnp.ones_like(hi)

    def place(terms):
        tt = jnp.stack(terms, axis=-1).astype(BF16).reshape(s, 8, 2, 6)
        z = jnp.zeros((s, 8, HEAD_DIM - 6), BF16)
        return jnp.concatenate([tt[:, :, 1], z, tt[:, :, 0], z], axis=-1).reshape(s, D_MODEL)

    return place([hi, mid, low, one, one, one]), place([one, one, one, -hi, -mid, -low])


def _attn_tiles(s):
    t = 512 if s % 512 == 0 else s
    return t, s // t


def _tri_steps(n, by_key):
    if by_key:
        pairs = [(q, k) for k in range(n) for q in range(k, n)]
    else:
        pairs = [(q, k) for q in range(n) for k in range(q + 1)]
    arr = np.asarray(pairs, np.int32)
    return jnp.asarray(arr[:, 0]), jnp.asarray(arr[:, 1])


def _attn_call(body, name, s, by_key, inputs, in_kinds, out_kinds, out_shapes, scratch):
    t, n = _attn_tiles(s)
    qi_arr, ki_arr = _tri_steps(n, by_key)
    specs = {
        "q": pl.BlockSpec((t, LANES), lambda p, i, qi, ki: (qi[i], p)),
        "k": pl.BlockSpec((t, LANES), lambda p, i, qi, ki: (ki[i], p)),
        "r": pl.BlockSpec((1, 2, t), lambda p, i, qi, ki: (p, 0, ki[i])),
    }
    grid_spec = pltpu.PrefetchScalarGridSpec(
        num_scalar_prefetch=2, grid=(8, int(qi_arr.shape[0])), in_specs=[specs[c] for c in in_kinds],
        out_specs=[specs[c] for c in out_kinds], scratch_shapes=list(scratch))
    return pl.pallas_call(body, name=name, grid_spec=grid_spec, out_shape=out_shapes, compiler_params=_params(2))(
        qi_arr, ki_arr, *inputs)


def _biased_qk(q2, k2, aq, ak, lo):
    sa = _dot(jnp.where(lo, q2, aq), jnp.where(lo, k2, ak), 1, 1)
    sb = _dot(jnp.where(lo, aq, q2), jnp.where(lo, ak, k2), 1, 1)
    return sa, sb


def _causal_rows(r, t):
    rid = r * ATTN_ROWS + lax.broadcasted_iota(jnp.int32, (ATTN_ROWS, t), 0)
    return rid >= lax.broadcasted_iota(jnp.int32, (ATTN_ROWS, t), 1)


def _flash_fwd(qs, kn, vb, augq, augk, name):
    s = qs.shape[0]
    t, n = _attn_tiles(s)
    nrep = t // LANES

    def body(qi_ref, ki_ref, q_ref, k_ref, v_ref, aq_ref, ak_ref, o_ref, lse_ref, s_ref, p_ref, m_ref, l_ref, al_ref, acc_ref):
        i = pl.program_id(1)
        qi = qi_ref[i]
        ki = ki_ref[i]

        @pl.when(ki == 0)
        def _():
            m_ref[...] = jnp.full_like(m_ref, -jnp.inf)
            l_ref[...] = jnp.zeros_like(l_ref)
            acc_ref[...] = jnp.zeros_like(acc_ref)

        lo = _lo_mask((t, LANES))
        v2 = v_ref[...]
        sa, sb = _biased_qk(q_ref[...], k_ref[...], aq_ref[...], ak_ref[...], lo)
        s_ref[0] = sa
        s_ref[1] = sb

        def softmax(masked):
            for hd in range(2):
                for r in range(t // ATTN_ROWS):
                    rows = slice(r * ATTN_ROWS, (r + 1) * ATTN_ROWS)
                    sc = s_ref[hd, rows, :]
                    if masked:
                        sc = jnp.where(_causal_rows(r, t), sc, -jnp.inf)
                    m_prev = m_ref[hd, rows, :]
                    m_new = jnp.maximum(m_prev, jnp.max(sc, axis=-1, keepdims=True))
                    alpha = jnp.exp2(m_prev - m_new)
                    pm = jnp.exp2(sc - jnp.tile(m_new, (1, nrep)))
                    l_ref[hd, rows, :] = alpha * l_ref[hd, rows, :] + jnp.sum(pm, axis=-1, keepdims=True)
                    m_ref[hd, rows, :] = m_new
                    al_ref[hd, rows, :] = alpha
                    p_ref[hd, rows, :] = pm.astype(BF16)

        @pl.when(ki < qi)
        def _():
            softmax(False)

        @pl.when(ki == qi)
        def _():
            softmax(True)

        pv = jnp.where(lo, _dot(p_ref[0], v2, 1, 0), _dot(p_ref[1], v2, 1, 0))
        acc_ref[...] = jnp.where(lo, al_ref[0], al_ref[1]) * acc_ref[...] + pv

        @pl.when(ki == qi)
        def _():
            l2 = jnp.where(lo, l_ref[0], l_ref[1])
            o_ref[...] = acc_ref[...] / l2
            lse_ref[...] = jnp.where(lo, m_ref[0], m_ref[1]) + jnp.log2(l2)

    out = _sds((s, D_MODEL))
    rep = pltpu.VMEM((2, t, LANES), F32)
    return _attn_call(body, name, s, False, (qs, kn, vb, augq, augk), "qkkqk", "qq", [out, out],
                      [pltpu.VMEM((2, t, t), F32), pltpu.VMEM((2, t, t), BF16), rep, rep, rep, pltpu.VMEM((t, LANES), F32)])


def _fox_delta(do, o, name):
    s = do.shape[0]
    ts = _row_tile(s, 512)

    def body(do_ref, o_ref, d_ref):
        d_ref[...] = _half_sums(do_ref[...] * o_ref[...], _lo_mask((ts, LANES)))

    spec = pl.BlockSpec((ts, LANES), lambda i, p: (i, p))
    return _pc(body, name, (s // ts, 8), [spec, spec], spec, _sds((s, D_MODEL)))(do, o)


def _bwd_tile(q_ref, k_ref, v_ref, aq_ref, ak_ref, do_ref, s_ref, dp_ref, lo):
    do2 = do_ref[...].astype(BF16)
    zero = jnp.zeros_like(do2)
    v2 = v_ref[...]
    sa, sb = _biased_qk(q_ref[...], k_ref[...], aq_ref[...], ak_ref[...], lo)
    s_ref[0] = sa
    s_ref[1] = sb
    dp_ref[0] = _dot(jnp.where(lo, do2, zero), v2, 1, 1)
    dp_ref[1] = _dot(jnp.where(lo, zero, do2), v2, 1, 1)
    return do2


def _bwd_chunk(s_ref, dp_ref, lse_ref, dl_ref, hd, r, t, masked):
    rows = slice(r * ATTN_ROWS, (r + 1) * ATTN_ROWS)
    c0 = hd * HEAD_DIM
    sc = s_ref[hd, rows, :]
    if masked:
        sc = jnp.where(_causal_rows(r, t), sc, -jnp.inf)
    pm = jnp.exp2(sc - lse_ref[rows, c0:c0 + 1])
    ds = pm * (dp_ref[hd, rows, :] - dl_ref[rows, c0:c0 + 1])
    return rows, pm, ds


def _flash_bwd_dq(qs, kn, vb, augq, augk, do, lse, delta, name):
    s = qs.shape[0]
    t, n = _attn_tiles(s)

    def body(qi_ref, ki_ref, q_ref, k_ref, v_ref, aq_ref, ak_ref, do_ref, lse_ref, dl_ref, dq_ref, dcq_ref,
             s_ref, dp_ref, ds_ref, acc_ref, racc_ref):
        i = pl.program_id(1)
        qi = qi_ref[i]
        ki = ki_ref[i]

        @pl.when(ki == 0)
        def _():
            acc_ref[...] = jnp.zeros_like(acc_ref)
            racc_ref[...] = jnp.zeros_like(racc_ref)

        lo = _lo_mask((t, LANES))
        _bwd_tile(q_ref, k_ref, v_ref, aq_ref, ak_ref, do_ref, s_ref, dp_ref, lo)

        def chunks(masked):
            for hd in range(2):
                for r in range(t // ATTN_ROWS):
                    rows, _, ds = _bwd_chunk(s_ref, dp_ref, lse_ref, dl_ref, hd, r, t, masked)
                    racc_ref[hd, rows, :] += jnp.sum(ds, axis=-1, keepdims=True)
                    ds_ref[hd, rows, :] = ds.astype(BF16)

        @pl.when(ki < qi)
        def _():
            chunks(False)

        @pl.when(ki == qi)
        def _():
            chunks(True)

        k2 = k_ref[...]
        acc_ref[...] += jnp.where(lo, _dot(ds_ref[0], k2, 1, 0), _dot(ds_ref[1], k2, 1, 0))

        @pl.when(ki == qi)
        def _():
            dq_ref[...] = acc_ref[...]
            dcq_ref[...] = jnp.where(lo, racc_ref[0], racc_ref[1])

    out = _sds((s, D_MODEL))
    return _attn_call(body, name, s, False, (qs, kn, vb, augq, augk, do, lse, delta), "qkkqkqqq", "qq", [out, out],
                      [pltpu.VMEM((2, t, t), F32), pltpu.VMEM((2, t, t), F32), pltpu.VMEM((2, t, t), BF16),
                       pltpu.VMEM((t, LANES), F32), pltpu.VMEM((2, t, LANES), F32)])


def _flash_bwd_dkv(qs, kn, vb, augq, augk, do, lse, delta, name):
    s = qs.shape[0]
    t, n = _attn_tiles(s)

    def body(qi_ref, ki_ref, q_ref, k_ref, v_ref, aq_ref, ak_ref, do_ref, lse_ref, dl_ref, dk_ref, dv_ref, dc_ref,
             s_ref, dp_ref, p_ref, ds_ref, dka_ref, dva_ref, dca_ref):
        i = pl.program_id(1)
        qi = qi_ref[i]
        ki = ki_ref[i]

        @pl.when(qi == ki)
        def _():
            dka_ref[...] = jnp.zeros_like(dka_ref)
            dva_ref[...] = jnp.zeros_like(dva_ref)
            dca_ref[...] = jnp.zeros_like(dca_ref)

        lo = _lo_mask((t, LANES))
        do2 = _bwd_tile(q_ref, k_ref, v_ref, aq_ref, ak_ref, do_ref, s_ref, dp_ref, lo)

        def chunks(masked):
            for hd in range(2):
                col = jnp.zeros((1, t), F32)
                for r in range(t // ATTN_ROWS):
                    rows, pm, ds = _bwd_chunk(s_ref, dp_ref, lse_ref, dl_ref, hd, r, t, masked)
                    col = col + jnp.sum(ds, axis=0, keepdims=True)
                    p_ref[hd, rows, :] = pm.astype(BF16)
                    ds_ref[hd, rows, :] = ds.astype(BF16)
                dca_ref[hd:hd + 1, :] -= col

        @pl.when(ki < qi)
        def _():
            chunks(False)

        @pl.when(ki == qi)
        def _():
            chunks(True)

        q2 = q_ref[...]
        dva_ref[...] += jnp.where(lo, _dot(p_ref[0], do2, 0, 0), _dot(p_ref[1], do2, 0, 0))
        dka_ref[...] += jnp.where(lo, _dot(ds_ref[0], q2, 0, 0), _dot(ds_ref[1], q2, 0, 0))

        @pl.when(qi == n - 1)
        def _():
            dk_ref[...] = dka_ref[...] * LN2
            dv_ref[...] = dva_ref[...]
            dc_ref[0] = dca_ref[0:2, :]

    out = _sds((s, D_MODEL))
    return _attn_call(body, name, s, True, (qs, kn, vb, augq, augk, do, lse, delta), "qkkqkqqq", "kkr",
                      [out, out, _sds((8, 2, s))],
                      [pltpu.VMEM((2, t, t), F32), pltpu.VMEM((2, t, t), F32), pltpu.VMEM((2, t, t), BF16),
                       pltpu.VMEM((2, t, t), BF16), pltpu.VMEM((t, LANES), F32), pltpu.VMEM((t, LANES), F32),
                       pltpu.VMEM((8, t), F32)])


def _fox_prep_bwd(proj, dqs, dk, dv, gq, gk, name):
    s = proj.shape[0]
    ts = _row_tile(s, 512)
    scale = HEAD_DIM ** -0.5

    def body(q_ref, k_ref, dq_ref, dk_ref, dv_ref, gq_ref, gk_ref, oq_ref, ok_ref, ov_ref, dgq_ref, dgk_ref):
        lo = _lo_mask((ts, LANES))

        @pl.when(jnp.logical_and(pl.program_id(0) == 0, pl.program_id(1) == 0))
        def _():
            dgq_ref[...] = jnp.zeros_like(dgq_ref)
            dgk_ref[...] = jnp.zeros_like(dgk_ref)

        def back(xv, dout, g):
            r = lax.rsqrt(_half_sums(xv * xv, lo) * (1.0 / HEAD_DIM) + RMS_EPS)
            y = xv * r
            dy = dout * g
            dx = r * (dy - y * (_half_sums(dy * y, lo) * (1.0 / HEAD_DIM)))
            return dx, jnp.sum(dout * y, axis=0, keepdims=True)

        dxq, dgq = back(q_ref[...], dq_ref[...] * scale, gq_ref[...])
        dxk, dgk = back(k_ref[...], dk_ref[...], gk_ref[...])
        oq_ref[...] = dxq.astype(BF16)
        ok_ref[...] = dxk.astype(BF16)
        ov_ref[...] = dv_ref[...].astype(BF16)
        dgq_ref[...] += dgq
        dgk_ref[...] += dgk

    def col(off):
        return pl.BlockSpec((ts, LANES), lambda i, p: (i, off + p))

    gspec = pl.BlockSpec((1, LANES), lambda i, p: (0, 0))
    out = _sds((s, D_MODEL), BF16)
    return _pc(body, name, (s // ts, 8), [col(0), col(8), col(0), col(0), col(0), gspec, gspec],
               [col(0)] * 3 + [gspec] * 2, [out] * 3 + [_sds((1, LANES))] * 2)(proj, proj, dqs, dk, dv, gq, gk)


def _fox_fwd(x, norm_w, w_in, b_f, q_gain, k_gain, w_out):
    h = _rms_fwd(x, norm_w, "fox_norm")
    proj = _mm(h, w_in, "nn", "fox_in")
    gq = jnp.tile(q_gain, (1, 2))
    gk = jnp.tile(k_gain, (1, 2))
    bf = jnp.pad(b_f, ((0, 0), (0, LANES - ATTN_HEADS)))
    qs, kn, vb = _fox_prep_fwd(proj, gq, gk, "fox_prep")
    cum = _cumsum_rows(_fox_logf(proj, bf, "fox_logf"), False, "fox_cum")[:, :ATTN_HEADS]
    augq, augk = _decay_terms(cum)
    o, lse = _flash_fwd(qs, kn, vb, augq, augk, "fox_attn")
    y = _mm(o, w_out, "nn", "fox_out", res=x)
    return y, (x, h, proj, gq, gk, bf, qs, kn, vb, augq, augk, o, lse)


def _fox_bwd(dy, saved, norm_w, w_in, w_out):
    x, h, proj, gq, gk, bf, qs, kn, vb, augq, augk, o, lse = saved
    s = x.shape[0]
    do = _mm(dy, w_out, "nt", "fox_do")
    g_out = _mm(o, dy, "tn", "fox_gout", out_dtype=BF16)
    delta = _fox_delta(do, o, "fox_delta")
    dqs, dcq = _flash_bwd_dq(qs, kn, vb, augq, augk, do, lse, delta, "fox_dq")
    dk, dv, dcrow = _flash_bwd_dkv(qs, kn, vb, augq, augk, do, lse, delta, "fox_dkv")
    dcum = jnp.pad(dcq[:, ::HEAD_DIM] + dcrow.reshape(ATTN_HEADS, s).T, ((0, 0), (0, LANES - ATTN_HEADS)))
    dlf = _cumsum_rows(dcum, True, "fox_dcum")
    dfl, g_bf = _fox_dlogf(proj, bf, dlf, "fox_dlogf")
    dq_o, dk_o, dv_o, g_gq, g_gk = _fox_prep_bwd(proj, dqs, dk, dv, gq, gk, "fox_dprep")
    dproj = jnp.concatenate([dq_o, dk_o, dv_o, dfl], axis=1)
    g_in = _mm(h, dproj, "tn", "fox_gin", out_dtype=BF16)
    dh = _mm(dproj, w_in, "nt", "fox_dh")
    dx, g_norm = _rms_bwd(x, norm_w, dh, dy, "fox_dnorm")
    g_q = g_gq[:, :HEAD_DIM] + g_gq[:, HEAD_DIM:]
    g_k = g_gk[:, :HEAD_DIM] + g_gk[:, HEAD_DIM:]
    return dx, g_norm, g_in[:, :FOX_IN], g_bf[:, :ATTN_HEADS], g_q, g_k, g_out


def _ssd_conv_fwd(proj, cw, cb, name):
    s = proj.shape[0]
    ts = _row_tile(s)
    w = 1024
    per = ts // 8

    def body(p_ref, h_ref, w_ref, b_ref, o_ref):
        first = pl.program_id(0) == 0
        cur = p_ref[...]
        halo = h_ref[...]
        u = w_ref[3:4, :] * cur + b_ref[...]
        for j in range(1, 4):
            u = u + w_ref[3 - j:4 - j, :] * _prev_rows(cur, halo, j, first)
        o_ref[...] = u * _sigmoid(u)

    return _pc(body, name, (s // ts, 4),
               [pl.BlockSpec((ts, w), lambda i, j: (i, 2 + j)),
                pl.BlockSpec((8, w), lambda i, j: (jnp.maximum(i * per - 1, 0), 2 + j)),
                pl.BlockSpec((4, w), lambda i, j: (0, j)), pl.BlockSpec((1, w), lambda i, j: (0, j))],
               pl.BlockSpec((ts, w), lambda i, j: (i, j)), _sds((s, SSM_CONV_DIM)))(proj, proj, cw, cb)


def _ssd_conv_bwd_act(proj, dxbc, cw, cb, name):
    s = proj.shape[0]
    ts = _row_tile(s)
    w = 1024
    per = ts // 8

    def body(p_ref, h_ref, d_ref, w_ref, b_ref, g_ref, db_ref):
        first = pl.program_id(1) == 0
        cur = p_ref[...]
        halo = h_ref[...]
        u = w_ref[3:4, :] * cur + b_ref[...]
        for j in range(1, 4):
            u = u + w_ref[3 - j:4 - j, :] * _prev_rows(cur, halo, j, first)
        sg = _sigmoid(u)
        g = d_ref[...] * (sg * (1.0 + u * (1.0 - sg)))
        g_ref[...] = g

        @pl.when(first)
        def _():
            db_ref[...] = jnp.zeros_like(db_ref)

        db_ref[...] += jnp.sum(g, axis=0, keepdims=True)

    vec = pl.BlockSpec((1, w), lambda j, i: (0, j))
    tile = pl.BlockSpec((ts, w), lambda j, i: (i, j))
    return _pc(body, name, (4, s // ts),
               [pl.BlockSpec((ts, w), lambda j, i: (i, 2 + j)),
                pl.BlockSpec((8, w), lambda j, i: (jnp.maximum(i * per - 1, 0), 2 + j)),
                tile, pl.BlockSpec((4, w), lambda j, i: (0, j)), vec],
               [tile, vec], [_sds((s, SSM_CONV_DIM)), _sds((1, SSM_CONV_DIM))])(proj, proj, dxbc, cw, cb)


def _ssd_conv_bwd_in(proj, g, cw, name):
    s = proj.shape[0]
    ts = _row_tile(s)
    nt = s // ts
    w = 1024
    per = ts // 8
    nblk = s // 8

    def body(p_ref, h_ref, g_ref, gn_ref, w_ref, o_ref, dw_ref):
        i = pl.program_id(1)
        first = i == 0
        last = i == nt - 1
        cur = p_ref[...]
        halo = h_ref[...]
        gv = g_ref[...]
        gn = gn_ref[...]

        @pl.when(first)
        def _():
            dw_ref[...] = jnp.zeros_like(dw_ref)

        dpre = w_ref[3:4, :] * gv
        dw_ref[3:4, :] += jnp.sum(gv * cur, axis=0, keepdims=True)
        for j in range(1, 4):
            dpre = dpre + w_ref[3 - j:4 - j, :] * _next_rows(gv, gn, j, last)
            dw_ref[3 - j:4 - j, :] += jnp.sum(gv * _prev_rows(cur, halo, j, first), axis=0, keepdims=True)
        o_ref[...] = dpre.astype(BF16)

    tile = pl.BlockSpec((ts, w), lambda j, i: (i, j))
    wspec = pl.BlockSpec((4, w), lambda j, i: (0, j))
    return _pc(body, name, (4, nt),
               [pl.BlockSpec((ts, w), lambda j, i: (i, 2 + j)),
                pl.BlockSpec((8, w), lambda j, i: (jnp.maximum(i * per - 1, 0), 2 + j)),
                tile, pl.BlockSpec((8, w), lambda j, i: (jnp.minimum((i + 1) * per, nblk - 1), j)), wspec],
               [tile, wspec], [_sds((s, SSM_CONV_DIM), BF16), _sds((4, SSM_CONV_DIM))])(proj, proj, g, g, cw)


def _ssd_dt_fwd(proj, bias, a_neg, name):
    s = proj.shape[0]
    n = s // SSM_CHUNK

    def body(r_ref, b_ref, a_ref, dt_ref, ac_ref):
        real = lax.broadcasted_iota(jnp.int32, (SSM_CHUNK, LANES), 1) < SSM_HEADS
        dt = jnp.where(real, _softplus(r_ref[...] + b_ref[...]), 0.0)
        dt_ref[...] = dt
        ac_ref[...] = _dot(_tri(True), dt * a_ref[...], 1, 0, HI)

    vec = pl.BlockSpec((1, LANES), lambda c: (0, 0))
    row = pl.BlockSpec((SSM_CHUNK, LANES), lambda c: (c, 0))
    return _pc(body, name, (n,), [pl.BlockSpec((SSM_CHUNK, LANES), lambda c: (c, 48)), vec, vec], [row, row],
               [_sds((s, LANES)), _sds((s, LANES))])(proj, bias, a_neg)


def _ssd_dt_bwd(proj, bias, ddt, name):
    s = proj.shape[0]
    ts = _row_tile(s, 512)

    def body(r_ref, b_ref, d_ref, o_ref, db_ref):
        real = lax.broadcasted_iota(jnp.int32, (ts, LANES), 1) < SSM_HEADS
        g = jnp.where(real, d_ref[...] * _sigmoid(r_ref[...] + b_ref[...]), 0.0)
        o_ref[...] = g.astype(BF16)

        @pl.when(pl.program_id(0) == 0)
        def _():
            db_ref[...] = jnp.zeros_like(db_ref)

        db_ref[...] += jnp.sum(g, axis=0, keepdims=True)

    vec = pl.BlockSpec((1, LANES), lambda i: (0, 0))
    row = pl.BlockSpec((ts, LANES), lambda i: (i, 0))
    return _pc(body, name, (s // ts,), [pl.BlockSpec((ts, LANES), lambda i: (i, 48)), vec, row], [row, vec],
               [_sds((s, LANES), BF16), _sds((1, LANES))])(proj, bias, ddt)


def _pair_cols(cols, k0, lo):
    return jnp.where(lo, cols[:, k0:k0 + 1], cols[:, k0 + 1:k0 + 2])


def _last_lane(row):
    lane = lax.broadcasted_iota(jnp.int32, row.shape, 1)
    return jnp.sum(jnp.where(lane == SSM_CHUNK - 1, row, 0.0), axis=-1, keepdims=True)


def _ssd_specs(nc, rev):
    cc = (lambda c: nc - 1 - c) if rev else (lambda c: c)
    return dict(
        x=pl.BlockSpec((SSM_CHUNK, 256), lambda g, c: (cc(c), g)),
        b=pl.BlockSpec((SSM_CHUNK, LANES), lambda g, c: (cc(c), 16 + g)),
        c=pl.BlockSpec((SSM_CHUNK, LANES), lambda g, c: (cc(c), 24 + g)),
        col=pl.BlockSpec((1, SSM_CHUNK, 4), lambda g, c: (g, cc(c), 0)),
        row=pl.BlockSpec((1, 4, SSM_CHUNK), lambda g, c: (g, 0, cc(c))),
        grp=pl.BlockSpec((1, 1, 256), lambda g, c: (g, 0, 0)),
        grow=pl.BlockSpec((1, 4, LANES), lambda g, c: (g, 0, 0)),
        hs=pl.BlockSpec((1, 1, 256, SSM_STATE), lambda g, c: (cc(c), g, 0, 0)),
        bc=pl.BlockSpec((SSM_CHUNK, LANES), lambda g, c: (cc(c), g)),
    )


def _ssd_scan_fwd(xbc, dtc, acol, drow, arow, dskip, name):
    s = xbc.shape[0]
    nc = s // SSM_CHUNK
    sp = _ssd_specs(nc, False)
    L = SSM_CHUNK

    def body(x_ref, b_ref, c_ref, dtc_ref, ac_ref, dr_ref, ar_ref, dk_ref, y_ref, hs_ref, h_ref):
        @pl.when(pl.program_id(1) == 0)
        def _():
            h_ref[...] = jnp.zeros_like(h_ref)

        bb = b_ref[...].astype(BF16)
        cb = c_ref[...].astype(BF16)
        gm = _dot(cb, bb, 1, 1)
        dtc = dtc_ref[0]
        ac = ac_ref[0]
        dr = dr_ref[0]
        ar = ar_ref[0]
        dsk = dk_ref[0]
        hs_ref[0, 0] = h_ref[...]
        tril = lax.broadcasted_iota(jnp.int32, (L, L), 0) >= lax.broadcasted_iota(jnp.int32, (L, L), 1)
        lo = _lo_mask((L, LANES))
        rowlo = lax.broadcasted_iota(jnp.int32, (L, LANES), 0) < HEAD_DIM
        for pr in range(2):
            k0 = 2 * pr
            xp = x_ref[:, pr * LANES:(pr + 1) * LANES]
            xpb = xp.astype(BF16)
            hp = h_ref[pr * LANES:(pr + 1) * LANES, :]
            yd, al = [], []
            for k in (k0, k0 + 1):
                seg = ac[:, k:k + 1] - ar[k:k + 1, :]
                wk = gm * jnp.exp(jnp.where(tril, seg, -jnp.inf)) * dr[k:k + 1, :]
                yd.append(_dot(wk.astype(BF16), xpb, 1, 0))
                al.append(_last_lane(ar[k:k + 1, :]))
            e = jnp.exp(_pair_cols(ac, k0, lo))
            yo = _dot(cb, hp.astype(BF16), 1, 1) * e
            y_ref[:, pr * LANES:(pr + 1) * LANES] = jnp.where(lo, yd[0], yd[1]) + yo + dsk[:, pr * LANES:(pr + 1) * LANES] * xp
            wp = jnp.where(lo, jnp.exp(al[0] - ac[:, k0:k0 + 1]) * dtc[:, k0:k0 + 1],
                           jnp.exp(al[1] - ac[:, k0 + 1:k0 + 2]) * dtc[:, k0 + 1:k0 + 2])
            st = _dot((xp * wp).astype(BF16), bb, 0, 0)
            dec = jnp.where(rowlo, jnp.exp(al[0]), jnp.exp(al[1]))
            h_ref[pr * LANES:(pr + 1) * LANES, :] = dec * hp + st

    return _pc(body, name, (SSM_GROUPS, nc),
               [sp["x"], sp["b"], sp["c"], sp["col"], sp["col"], sp["row"], sp["row"], sp["grp"]],
               [sp["x"], sp["hs"]], [_sds((s, SSM_INNER)), _sds((nc, SSM_GROUPS, 256, SSM_STATE))],
               [pltpu.VMEM((256, SSM_STATE), F32)])(xbc, xbc, xbc, dtc, acol, drow, arow, dskip)


def _ssd_scan_bwd(xbc, dtc, acol, drow, arow, dskip, agrp, hs, dy, name):
    s = xbc.shape[0]
    nc = s // SSM_CHUNK
    sp = _ssd_specs(nc, True)
    L = SSM_CHUNK

    def body(x_ref, b_ref, c_ref, dtc_ref, ac_ref, dr_ref, ar_ref, dk_ref, ag_ref, hs_ref, dy_ref,
             dx_ref, db_ref, dc_ref, ddt_ref, da_ref, dd_ref, dh_ref):
        @pl.when(pl.program_id(1) == 0)
        def _():
            dh_ref[...] = jnp.zeros_like(dh_ref)
            da_ref[...] = jnp.zeros_like(da_ref)
            dd_ref[...] = jnp.zeros_like(dd_ref)

        bb = b_ref[...].astype(BF16)
        cb = c_ref[...].astype(BF16)
        gm = _dot(cb, bb, 1, 1)
        dtc = dtc_ref[0]
        ac = ac_ref[0]
        dr = dr_ref[0]
        ar = ar_ref[0]
        dsk = dk_ref[0]
        ag = ag_ref[0]
        tril = lax.broadcasted_iota(jnp.int32, (L, L), 0) >= lax.broadcasted_iota(jnp.int32, (L, L), 1)
        lo = _lo_mask((L, LANES))
        nlo = jnp.logical_not(lo)
        rowlo = lax.broadcasted_iota(jnp.int32, (L, LANES), 0) < HEAD_DIM
        lane = lax.broadcasted_iota(jnp.int32, (L, LANES), 1)
        lane_row = lax.broadcasted_iota(jnp.int32, (1, LANES), 1)
        dgm = jnp.zeros((L, L), F32)
        dcm = jnp.zeros((L, SSM_STATE), F32)
        dbm = jnp.zeros((L, SSM_STATE), F32)
        cols = jnp.zeros((L, LANES), F32)
        rows_ddt, rows_q, al_all, dcd_all = [], [], [], []
        for pr in range(2):
            k0 = 2 * pr
            xp = x_ref[:, pr * LANES:(pr + 1) * LANES]
            xpb = xp.astype(BF16)
            dyp = dy_ref[:, pr * LANES:(pr + 1) * LANES]
            dypb = dyp.astype(BF16)
            zero = jnp.zeros_like(dypb)
            hp = hs_ref[0, 0, pr * LANES:(pr + 1) * LANES, :]
            hpb = hp.astype(BF16)
            dst = dh_ref[pr * LANES:(pr + 1) * LANES, :]
            dstb = dst.astype(BF16)
            dxd, al = [], []
            for k in (k0, k0 + 1):
                sel = lo if k == k0 else nlo
                seg = ac[:, k:k + 1] - ar[k:k + 1, :]
                lam = jnp.exp(jnp.where(tril, seg, -jnp.inf))
                wk = gm * lam * dr[k:k + 1, :]
                dwk = _dot(jnp.where(sel, dypb, zero), xpb, 1, 1)
                mk = dwk * gm * lam
                qk = mk * dr[k:k + 1, :]
                dgm = dgm + dwk * lam * dr[k:k + 1, :]
                rows_ddt.append(jnp.sum(mk, axis=0, keepdims=True))
                rows_q.append(jnp.sum(qk, axis=0, keepdims=True))
                cols = jnp.where(lane == k, jnp.sum(qk, axis=-1, keepdims=True), cols)
                dxd.append(_dot(wk.astype(BF16), dypb, 0, 0))
                al.append(_last_lane(ar[k:k + 1, :]))
            al_all += al
            dxp = jnp.where(lo, dxd[0], dxd[1])
            e = jnp.exp(_pair_cols(ac, k0, lo))
            dye = dyp * e
            dyeb = dye.astype(BF16)
            dcm = dcm + _dot(dyeb, hpb, 1, 0)
            dh_yoff = _dot(dyeb, cb, 0, 0)
            tq = dye * _dot(cb, hpb, 1, 1)
            cols = jnp.where(lane == 4 + k0, jnp.sum(jnp.where(lo, tq, 0.0), axis=-1, keepdims=True), cols)
            cols = jnp.where(lane == 5 + k0, jnp.sum(jnp.where(lo, 0.0, tq), axis=-1, keepdims=True), cols)
            wp = jnp.where(lo, jnp.exp(al[0] - ac[:, k0:k0 + 1]) * dtc[:, k0:k0 + 1],
                           jnp.exp(al[1] - ac[:, k0 + 1:k0 + 2]) * dtc[:, k0 + 1:k0 + 2])
            dxw = _dot(bb, dstb, 1, 1)
            dxp = dxp + dxw * wp
            tw = xp * dxw
            cols = jnp.where(lane == 8 + k0, jnp.sum(jnp.where(lo, tw, 0.0), axis=-1, keepdims=True), cols)
            cols = jnp.where(lane == 9 + k0, jnp.sum(jnp.where(lo, 0.0, tw), axis=-1, keepdims=True), cols)
            dbm = dbm + _dot((xp * wp).astype(BF16), dstb, 1, 0)
            dsl = dsk[:, pr * LANES:(pr + 1) * LANES]
            dx_ref[:, pr * LANES:(pr + 1) * LANES] = dxp + dsl * dyp
            dd_ref[0, :, pr * LANES:(pr + 1) * LANES] += jnp.sum(dyp * xp, axis=0, keepdims=True)
            prod = dst * hp
            dcd_all.append(jnp.sum(jnp.sum(jnp.where(rowlo, prod, 0.0), axis=-1, keepdims=True), axis=0, keepdims=True))
            dcd_all.append(jnp.sum(jnp.sum(jnp.where(rowlo, 0.0, prod), axis=-1, keepdims=True), axis=0, keepdims=True))
            dec = jnp.where(rowlo, jnp.exp(al[0]), jnp.exp(al[1]))
            dh_ref[pr * LANES:(pr + 1) * LANES, :] = dec * dst + dh_yoff
        dgb = dgm.astype(BF16)
        dc_ref[...] = dcm + _dot(dgb, bb, 1, 0)
        db_ref[...] = dbm + _dot(dgb, cb, 0, 0)
        colt = cols.T
        sub8 = lax.broadcasted_iota(jnp.int32, (8, LANES), 0)
        da_rows = jnp.zeros((8, LANES), F32)
        ddt_part = []
        for k in range(4):
            rs = colt[k:k + 1, :]
            uo = colt[4 + k:5 + k, :]
            dwl = colt[8 + k:9 + k, :]
            es = jnp.exp(al_all[k] - ar[k:k + 1, :])
            wrow = es * dr[k:k + 1, :]
            dwl_w = dwl * wrow
            da_k = rs - rows_q[k] + uo - dwl_w
            tail = jnp.sum(dwl_w, axis=-1, keepdims=True) + jnp.exp(al_all[k]) * dcd_all[k]
            da_k = da_k + jnp.where(lane_row == L - 1, tail, 0.0)
            da_rows = jnp.where(sub8 == k, da_k, da_rows)
            ddt_part.append(rows_ddt[k] + dwl * es)
        dda = _dot(da_rows, _tri(True), 1, 0, HI)
        for k in range(4):
            dda_k = dda[k:k + 1, :]
            ddt_ref[0, k:k + 1, :] = ddt_part[k] + dda_k * ag[k:k + 1, :]
            da_ref[0, k:k + 1, :] += dda_k * dr[k:k + 1, :] * ag[k:k + 1, :]

    return _pc(body, name, (SSM_GROUPS, nc),
               [sp["x"], sp["b"], sp["c"], sp["col"], sp["col"], sp["row"], sp["row"], sp["grp"], sp["grow"], sp["hs"], sp["x"]],
               [sp["x"], sp["bc"], sp["bc"], sp["row"], sp["grow"], sp["grp"]],
               [_sds((s, SSM_INNER)), _sds((s, 1024)), _sds((s, 1024)), _sds((SSM_GROUPS, 4, s)),
                _sds((SSM_GROUPS, 4, LANES)), _sds((SSM_GROUPS, 1, 256))],
               [pltpu.VMEM((256, SSM_STATE), F32)])(xbc, xbc, xbc, dtc, acol, drow, arow, dskip, agrp, hs, dy)


def _gnorm_fwd(y, proj, nw, name):
    s = y.shape[0]
    ts = _row_tile(s)
    gw = SSM_INNER // SSM_GROUPS

    def body(y_ref, z_ref, w_ref, o_ref):
        for g in range(SSM_GROUPS):
            sl = slice(g * gw, (g + 1) * gw)
            z = z_ref[:, sl]
            y2 = y_ref[:, sl] * (z * _sigmoid(z))
            r = lax.rsqrt(jnp.mean(y2 * y2, axis=-1, keepdims=True) + RMS_EPS)
            o_ref[:, sl] = ((y2 * r) * w_ref[:, sl]).astype(BF16)

    row = pl.BlockSpec((ts, SSM_INNER), lambda i: (i, 0))
    return _pc(body, name, (s // ts,), [row, row, pl.BlockSpec((1, SSM_INNER), lambda i: (0, 0))], row,
               _sds((s, SSM_INNER), BF16))(y, proj, nw)


def _gnorm_bwd(y, proj, nw, dyn, name):
    s = y.shape[0]
    ts = _row_tile(s)
    gw = SSM_INNER // SSM_GROUPS

    def body(y_ref, z_ref, w_ref, d_ref, dy_ref, dz_ref, dw_ref):
        @pl.when(pl.program_id(0) == 0)
        def _():
            dw_ref[...] = jnp.zeros_like(dw_ref)

        for g in range(SSM_GROUPS):
            sl = slice(g * gw, (g + 1) * gw)
            z = z_ref[:, sl]
            yv = y_ref[:, sl]
            sg = _sigmoid(z)
            sz = z * sg
            y2 = yv * sz
            r = lax.rsqrt(jnp.mean(y2 * y2, axis=-1, keepdims=True) + RMS_EPS)
            yn = y2 * r
            dout = d_ref[:, sl]
            dyg = dout * w_ref[:, sl]
            dy2 = r * (dyg - yn * jnp.mean(dyg * yn, axis=-1, keepdims=True))
            dy_ref[:, sl] = dy2 * sz
            dz_ref[:, sl] = (dy2 * yv * (sg * (1.0 + z * (1.0 - sg)))).astype(BF16)
            dw_ref[:, sl] += jnp.sum(dout * yn, axis=0, keepdims=True)

    row = pl.BlockSpec((ts, SSM_INNER), lambda i: (i, 0))
    vec = pl.BlockSpec((1, SSM_INNER), lambda i: (0, 0))
    return _pc(body, name, (s // ts,), [row, row, vec, row], [row, row, vec],
               [_sds((s, SSM_INNER)), _sds((s, SSM_INNER), BF16), _sds((1, SSM_INNER))])(y, proj, nw, dyn)


def _head_layouts(v, s):
    return v.reshape(s, SSM_GROUPS, 4).transpose(1, 0, 2), v.T.reshape(SSM_GROUPS, 4, s)


def _ssd_fwd(x, norm_w, w_in, conv_w, conv_b, dt_bias, a_log, d_skip, gnorm_w, w_out):
    s = x.shape[0]
    h = _rms_fwd(x, norm_w, "ssd_norm")
    proj = _mm(h, w_in, "nn", "ssd_in")
    xbc = _ssd_conv_fwd(proj, conv_w, conv_b, "ssd_conv")
    pad = ((0, 0), (0, LANES - SSM_HEADS))
    a_neg = -jnp.exp(a_log)
    bias = jnp.pad(dt_bias, pad)
    dt, acum = _ssd_dt_fwd(proj, bias, jnp.pad(a_neg, pad), "ssd_dt")
    dtc, drow = _head_layouts(dt[:, :SSM_HEADS], s)
    acol, arow = _head_layouts(acum[:, :SSM_HEADS], s)
    dskip = jnp.repeat(d_skip.reshape(SSM_GROUPS, 1, 4), HEAD_DIM, axis=2)
    y, hs = _ssd_scan_fwd(xbc, dtc, acol, drow, arow, dskip, "ssd_scan")
    yn = _gnorm_fwd(y, proj, gnorm_w, "ssd_gnorm")
    out = _mm(yn, w_out, "nn", "ssd_out", res=x)
    return out, (x, h, proj, xbc, bias, a_neg, dtc, acol, drow, arow, dskip, y, hs, yn)


def _ssd_bwd(dout, saved, norm_w, w_in, conv_w, conv_b, gnorm_w, w_out):
    x, h, proj, xbc, bias, a_neg, dtc, acol, drow, arow, dskip, y, hs, yn = saved
    s = x.shape[0]
    dyn = _mm(dout, w_out, "nt", "ssd_dyn")
    g_out = _mm(yn, dout, "tn", "ssd_gout", out_dtype=BF16)
    dy, dz, g_gnorm = _gnorm_bwd(y, proj, gnorm_w, dyn, "ssd_dgnorm")
    agrp = jnp.broadcast_to(a_neg.reshape(SSM_GROUPS, 4, 1), (SSM_GROUPS, 4, LANES))
    dxs, db, dc, ddt_row, da_acc, dd_acc = _ssd_scan_bwd(xbc, dtc, acol, drow, arow, dskip, agrp, hs, dy, "ssd_dscan")
    dxbc = jnp.concatenate([dxs, db, dc], axis=1)
    gact, g_cb = _ssd_conv_bwd_act(proj, dxbc, conv_w, conv_b, "ssd_dconv_act")
    dpre, g_cw = _ssd_conv_bwd_in(proj, gact, conv_w, "ssd_dconv_in")
    ddt = jnp.pad(ddt_row.reshape(SSM_HEADS, s).T, ((0, 0), (0, LANES - SSM_HEADS)))
    ddtraw, g_dtb = _ssd_dt_bwd(proj, bias, ddt, "ssd_ddt")
    dproj = jnp.concatenate([dz, dpre, ddtraw], axis=1)
    g_in = _mm(h, dproj, "tn", "ssd_gin", out_dtype=BF16)
    dh = _mm(dproj, w_in, "nt", "ssd_dh")
    dx, g_norm = _rms_bwd(x, norm_w, dh, dout, "ssd_dnorm")
    g_alog = jnp.sum(da_acc, axis=-1).reshape(1, SSM_HEADS)
    g_d = jnp.sum(dd_acc.reshape(SSM_GROUPS, 4, HEAD_DIM), axis=-1).reshape(1, SSM_HEADS)
    return dx, g_norm, g_in[:, :SSM_IN], g_cw, g_cb, g_dtb[:, :SSM_HEADS], g_alog, g_d, g_gnorm, g_out


def _loss_head(y, target, name):
    s, d = y.shape
    ts = _row_tile(s)

    def body(y_ref, t_ref, dy_ref, l_ref):
        @pl.when(pl.program_id(0) == 0)
        def _():
            l_ref[...] = jnp.zeros_like(l_ref)

        e = y_ref[...] - t_ref[...]
        dy_ref[...] = e * (1.0 / d)
        part = jnp.sum(jnp.sum(e * e, axis=-1, keepdims=True), axis=0, keepdims=True) * (0.5 / d)
        l_ref[...] += jnp.broadcast_to(part, l_ref.shape)

    row = pl.BlockSpec((ts, d), lambda i: (i, 0))
    dy, lacc = _pc(body, name, (s // ts,), [row, row], [row, pl.BlockSpec((8, LANES), lambda i: (0, 0))],
                   [_sds((s, d)), _sds((8, LANES))])(y, target)
    return lacc[0, 0], dy


def _local_step(x, target, w):
    saved = []
    for i in range(DEPTH):
        kind, j = i % 3, i // 3
        mn = w["mix_norm"][i:i + 1]
        if kind == 0:
            x, sv = _conv_fwd(x, mn, w["conv_w_in"][j], w["conv_w_dw"][j], w["conv_w_out"][j], str(i))
        elif kind == 1:
            x, sv = _fox_fwd(x, mn, w["fox_w_in"], w["fox_b_f"], w["fox_q_gain"], w["fox_k_gain"], w["fox_w_out"])
        else:
            x, sv = _ssd_fwd(x, mn, w["ssd_w_in"], w["ssd_conv_w"], w["ssd_conv_b"], w["ssd_dt_bias"],
                             w["ssd_a_log"], w["ssd_d"], w["ssd_norm_w"], w["ssd_w_out"])
        x, sf = _ffn_fwd(x, w["ffn_norm"][i:i + 1], w["ffn_w_gu"][i], w["ffn_w_down"][i], str(i))
        saved.append((sv, sf))
    loss, dx = _loss_head(x, target, "loss_head")
    g = {k: [None] * n for k, n in (("mix_norm", DEPTH), ("ffn_norm", DEPTH), ("ffn_w_gu", DEPTH), ("ffn_w_down", DEPTH),
                                    ("conv_w_in", 2), ("conv_w_dw", 2), ("conv_w_out", 2))}
    for i in reversed(range(DEPTH)):
        kind, j = i % 3, i // 3
        sv, sf = saved[i]
        dx, g["ffn_norm"][i], g["ffn_w_gu"][i], g["ffn_w_down"][i] = _ffn_bwd(
            dx, sf, w["ffn_norm"][i:i + 1], w["ffn_w_gu"][i], w["ffn_w_down"][i], str(i))
        mn = w["mix_norm"][i:i + 1]
        if kind == 0:
            dx, g["mix_norm"][i], g["conv_w_in"][j], g["conv_w_dw"][j], g["conv_w_out"][j] = _conv_bwd(
                dx, sv, mn, w["conv_w_in"][j], w["conv_w_dw"][j], w["conv_w_out"][j], str(i))
        elif kind == 1:
            (dx, g["mix_norm"][i], g["fox_w_in"], g["fox_b_f"], g["fox_q_gain"], g["fox_k_gain"],
             g["fox_w_out"]) = _fox_bwd(dx, sv, mn, w["fox_w_in"], w["fox_w_out"])
        else:
            (dx, g["mix_norm"][i], g["ssd_w_in"], g["ssd_conv_w"], g["ssd_conv_b"], g["ssd_dt_bias"], g["ssd_a_log"],
             g["ssd_d"], g["ssd_norm_w"], g["ssd_w_out"]) = _ssd_bwd(
                 dx, sv, mn, w["ssd_w_in"], w["ssd_conv_w"], w["ssd_conv_b"], w["ssd_norm_w"], w["ssd_w_out"])
    g["mix_norm"] = jnp.concatenate(g["mix_norm"], axis=0)
    g["ffn_norm"] = jnp.concatenate(g["ffn_norm"], axis=0)
    g["conv_w_dw"] = jnp.stack(g["conv_w_dw"], axis=0)
    g["ssd_conv_w"] = g["ssd_conv_w"][None]
    return loss, dx, g


def _mesh_position():
    return lax.axis_index("x") * 4 + lax.axis_index("y") * 2 + lax.axis_index("c")


def _device_of(t):
    return (lax.shift_right_logical(t, 2), lax.bitwise_and(lax.shift_right_logical(t, 1), 1), lax.bitwise_and(t, 1))


def _exchange_run(srcs_of, out_refs, send_sems, recv_sems, local_sems):
    me = _mesh_position()
    na = len(out_refs)
    locals_ = [pltpu.make_async_copy(srcs_of(a, me), out_refs[a].at[me], local_sems.at[a]) for a in range(na)]
    for cp in locals_:
        cp.start()
    copies = []
    for j in range(1, NDEV):
        t = lax.rem(me + j, NDEV)
        for a in range(na):
            copies.append(pltpu.make_async_remote_copy(
                src_ref=srcs_of(a, t), dst_ref=out_refs[a].at[me], send_sem=send_sems.at[a, j - 1],
                recv_sem=recv_sems.at[a, j - 1], device_id=_device_of(t), device_id_type=pl.DeviceIdType.MESH))
    for cp in copies:
        cp.start()
    for cp in copies:
        cp.wait_send()
    for j in range(1, NDEV):
        frm = lax.rem(me + NDEV - j, NDEV)
        for a in range(na):
            pltpu.make_async_remote_copy(
                src_ref=srcs_of(a, me), dst_ref=out_refs[a].at[frm], send_sem=send_sems.at[a, j - 1],
                recv_sem=recv_sems.at[a, j - 1], device_id=_device_of(frm), device_id_type=pl.DeviceIdType.MESH).wait_recv()
    for cp in locals_:
        cp.wait()


def _exchange(arrays, name, gather):
    na = len(arrays)

    def body(*refs):
        srcs, outs = refs[:na], refs[na:2 * na]
        send_sems, recv_sems, local_sems = refs[2 * na:]
        _exchange_run((lambda a, t: srcs[a]) if gather else (lambda a, t: srcs[a].at[t]), outs, send_sems, recv_sems, local_sems)

    hbm = pl.BlockSpec(memory_space=pl.ANY)
    outs = [_sds(((NDEV,) + a.shape) if gather else a.shape, a.dtype) for a in arrays]
    return pl.pallas_call(
        body, name=name, in_specs=[hbm] * na, out_specs=[hbm] * na, out_shape=outs,
        scratch_shapes=[pltpu.SemaphoreType.DMA((na, NDEV - 1)), pltpu.SemaphoreType.DMA((na, NDEV - 1)),
                        pltpu.SemaphoreType.DMA((na,))])(*arrays)


def _all_sum_small(pack, name):
    def body(src_ref, out_ref, buf_ref, send_sems, recv_sems, local_sems):
        _exchange_run(lambda a, t: src_ref, [buf_ref], send_sems, recv_sems, local_sems)
        acc = buf_ref[0]
        for d in range(1, NDEV):
            acc = acc + buf_ref[d]
        out_ref[...] = acc

    vmem = pl.BlockSpec(memory_space=pltpu.VMEM)
    return pl.pallas_call(
        body, name=name, in_specs=[vmem], out_specs=vmem, out_shape=_sds(pack.shape, pack.dtype),
        scratch_shapes=[pltpu.VMEM((NDEV,) + pack.shape, pack.dtype), pltpu.SemaphoreType.DMA((1, NDEV - 1)),
                        pltpu.SemaphoreType.DMA((1, NDEV - 1)), pltpu.SemaphoreType.DMA((1,))])(pack)


def _sum_slabs(slabs, name):
    _, r, c = slabs.shape
    tr = r
    for cand in (256, 352):
        if r % cand == 0:
            tr = cand
            break

    def body(s_ref, o_ref):
        acc = s_ref[0].astype(F32)
        for d in range(1, NDEV):
            acc = acc + s_ref[d].astype(F32)
        o_ref[...] = acc

    return _pc(body, name, (r // tr,), [pl.BlockSpec((NDEV, tr, c), lambda i: (0, i, 0))],
               pl.BlockSpec((tr, c), lambda i: (i, 0)), _sds((r, c)))(slabs)


def _adamw(wt, g, m, v, name):
    shape = wt.shape
    w2, g2, m2, v2 = (a.reshape(-1, shape[-1]) for a in (wt, g, m, v))
    r, c = w2.shape
    tr = r
    for cand in (512, 352, 256):
        if r % cand == 0:
            tr = cand
            break
    c1 = 1.0 - ADAM_B1 ** ADAM_STEP
    c2 = 1.0 - ADAM_B2 ** ADAM_STEP

    def body(w_ref, g_ref, m_ref, v_ref, d_ref, mo_ref, vo_ref):
        gv = g_ref[...]
        mn = ADAM_B1 * m_ref[...] + (1.0 - ADAM_B1) * gv
        vn = ADAM_B2 * v_ref[...] + (1.0 - ADAM_B2) * (gv * gv)
        mo_ref[...] = mn
        vo_ref[...] = vn
        d_ref[...] = -ADAM_LR * ((mn / c1) / (jnp.sqrt(vn / c2) + ADAM_EPS) + ADAM_WD * w_ref[...])

    spec = pl.BlockSpec((tr, c), lambda i: (i, 0))
    outs = _pc(body, name, (r // tr,), [spec] * 4, [spec] * 3, [_sds((r, c))] * 3)(w2, g2, m2, v2)
    return tuple(o.reshape(shape) for o in outs)


_NAMES = ["mix_norm", "ffn_norm", "ffn_w_gu", "ffn_w_down", "conv_w_in", "conv_w_dw", "conv_w_out", "fox_w_in", "fox_b_f",
          "fox_q_gain", "fox_k_gain", "fox_w_out", "ssd_w_in", "ssd_conv_w", "ssd_conv_b", "ssd_dt_bias", "ssd_a_log",
          "ssd_d", "ssd_norm_w", "ssd_w_out"]
_MATRICES = ["ffn_w_gu", "ffn_w_down", "conv_w_in", "conv_w_out", "fox_w_in", "fox_w_out", "ssd_w_in", "ssd_w_out"]
_VECTORS = {"conv_w_dw": 2, "ssd_conv_w": 2, "ssd_conv_b": 1, "ssd_norm_w": 1}
_REPLICATED = ["mix_norm", "ffn_norm", "fox_b_f", "fox_q_gain", "fox_k_gain", "ssd_dt_bias", "ssd_a_log", "ssd_d"]


def _to_rows(flat):
    n = flat.shape[0]
    rows = -(-n // (8 * D_MODEL)) * 8
    return jnp.pad(flat, (0, rows * D_MODEL - n)).reshape(rows, D_MODEL)


def _full_shape(local_shape, axis):
    shp = list(local_shape)
    shp[axis] *= NDEV
    return tuple(shp)


def _cols_from_blocks(g):
    return jnp.moveaxis(g, 0, 1).reshape(g.shape[1], NDEV * g.shape[2])


def _blocks_from_cols(full):
    k, n8 = full.shape
    return jnp.moveaxis(full.reshape(k, NDEV, n8 // NDEV), 1, 0)


def kernel(x, mix_norm, ffn_norm, ffn_w_gu, ffn_w_down, conv_w_in, conv_w_dw, conv_w_out, fox_w_in, fox_b_f, fox_q_gain, fox_k_gain, fox_w_out, ssd_w_in, ssd_conv_w, ssd_conv_b, ssd_dt_bias, ssd_a_log, ssd_d, ssd_norm_w, ssd_w_out, loss_target, m_mix_norm, m_ffn_norm, m_ffn_w_gu, m_ffn_w_down, m_conv_w_in, m_conv_w_dw, m_conv_w_out, m_fox_w_in, m_fox_b_f, m_fox_q_gain, m_fox_k_gain, m_fox_w_out, m_ssd_w_in, m_ssd_conv_w, m_ssd_conv_b, m_ssd_dt_bias, m_ssd_a_log, m_ssd_d, m_ssd_norm_w, m_ssd_w_out, v_mix_norm, v_ffn_norm, v_ffn_w_gu, v_ffn_w_down, v_conv_w_in, v_conv_w_dw, v_conv_w_out, v_fox_w_in, v_fox_b_f, v_fox_q_gain, v_fox_k_gain, v_fox_w_out, v_ssd_w_in, v_ssd_conv_w, v_ssd_conv_b, v_ssd_dt_bias, v_ssd_a_log, v_ssd_d, v_ssd_norm_w, v_ssd_w_out):
    local = dict(mix_norm=mix_norm, ffn_norm=ffn_norm, ffn_w_gu=ffn_w_gu, ffn_w_down=ffn_w_down, conv_w_in=conv_w_in,
                 conv_w_dw=conv_w_dw, conv_w_out=conv_w_out, fox_w_in=fox_w_in, fox_b_f=fox_b_f, fox_q_gain=fox_q_gain,
                 fox_k_gain=fox_k_gain, fox_w_out=fox_w_out, ssd_w_in=ssd_w_in, ssd_conv_w=ssd_conv_w, ssd_conv_b=ssd_conv_b,
                 ssd_dt_bias=ssd_dt_bias, ssd_a_log=ssd_a_log, ssd_d=ssd_d, ssd_norm_w=ssd_norm_w, ssd_w_out=ssd_w_out)
    mom = dict(zip(_NAMES, [m_mix_norm, m_ffn_norm, m_ffn_w_gu, m_ffn_w_down, m_conv_w_in, m_conv_w_dw, m_conv_w_out, m_fox_w_in,
                            m_fox_b_f, m_fox_q_gain, m_fox_k_gain, m_fox_w_out, m_ssd_w_in, m_ssd_conv_w, m_ssd_conv_b,
                            m_ssd_dt_bias, m_ssd_a_log, m_ssd_d, m_ssd_norm_w, m_ssd_w_out]))
    var = dict(zip(_NAMES, [v_mix_norm, v_ffn_norm, v_ffn_w_gu, v_ffn_w_down, v_conv_w_in, v_conv_w_dw, v_conv_w_out, v_fox_w_in,
                            v_fox_b_f, v_fox_q_gain, v_fox_k_gain, v_fox_w_out, v_ssd_w_in, v_ssd_conv_w, v_ssd_conv_b,
                            v_ssd_dt_bias, v_ssd_a_log, v_ssd_d, v_ssd_norm_w, v_ssd_w_out]))

    vec_pack = _to_rows(jnp.concatenate([local[k].reshape(-1) for k in _VECTORS]))
    gathered = _exchange([local[k].astype(BF16) for k in _MATRICES] + [vec_pack], "gather_weights", True)
    gm = dict(zip(_MATRICES, gathered[:-1]))
    gvec = gathered[-1].reshape(NDEV, -1)
    full = {k: local[k] for k in _REPLICATED}
    off = 0
    for k, axis in _VECTORS.items():
        n = local[k].size
        blk = jnp.moveaxis(gvec[:, off:off + n].reshape((NDEV,) + local[k].shape), 0, axis)
        full[k] = blk.reshape(_full_shape(local[k].shape, axis))
        off += n
    full["ffn_w_gu"] = [gm["ffn_w_gu"][:, i] for i in range(DEPTH)]
    full["ffn_w_down"] = [gm["ffn_w_down"][:, i].reshape(4, FF_BLOCK, D_MODEL) for i in range(DEPTH)]
    full["conv_w_in"] = [gm["conv_w_in"][:, j] for j in range(2)]
    full["conv_w_out"] = [gm["conv_w_out"][:, j].reshape(D_MODEL, D_MODEL) for j in range(2)]
    full["fox_w_in"] = jnp.pad(_cols_from_blocks(gm["fox_w_in"][:, 0]), ((0, 0), (0, FOX_IN_PAD - FOX_IN)))
    full["fox_w_out"] = gm["fox_w_out"].reshape(D_MODEL, D_MODEL)
    full["ssd_w_in"] = jnp.pad(_cols_from_blocks(gm["ssd_w_in"][:, 0]), ((0, 0), (0, SSM_IN_PAD - SSM_IN)))
    full["ssd_w_out"] = gm["ssd_w_out"].reshape(SSM_INNER, D_MODEL)
    full["ssd_conv_w"] = full["ssd_conv_w"][0]

    loss_part, dx, grads = _local_step(x[0], loss_target[0], full)

    send = ([grads["ffn_w_gu"][i] for i in range(DEPTH)]
            + [grads["ffn_w_down"][i].reshape(NDEV, D_FF // NDEV, D_MODEL) for i in range(DEPTH)]
            + [grads["conv_w_in"][j] for j in range(2)]
            + [grads["conv_w_out"][j].reshape(NDEV, D_MODEL // NDEV, D_MODEL) for j in range(2)]
            + [_blocks_from_cols(grads["fox_w_in"]), grads["fox_w_out"].reshape(NDEV, D_MODEL // NDEV, D_MODEL),
               _blocks_from_cols(grads["ssd_w_in"]), grads["ssd_w_out"].reshape(NDEV, SSM_INNER // NDEV, D_MODEL)])
    recv = _exchange(send, "scatter_grads", False)
    sums = [_sum_slabs(r, f"sum_grads_{n}") for n, r in enumerate(recv)]
    shard_grad = {
        "ffn_w_gu": jnp.stack(sums[0:4]), "ffn_w_down": jnp.stack(sums[4:8]), "conv_w_in": jnp.stack(sums[8:10]),
        "conv_w_out": jnp.stack(sums[10:12]), "fox_w_in": sums[12][None], "fox_w_out": sums[13][None],
        "ssd_w_in": sums[14][None], "ssd_w_out": sums[15][None]}

    small_names = _REPLICATED + list(_VECTORS)
    small = [jnp.reshape(loss_part, (1,))] + [grads[k].reshape(-1) for k in small_names]
    total = _all_sum_small(_to_rows(jnp.concatenate(small)), "sum_small").reshape(-1)
    loss = total[0]
    off = 1
    me = _mesh_position()
    for k, part in zip(small_names, small[1:]):
        gk = total[off:off + part.shape[0]]
        off += part.shape[0]
        if k in _VECTORS:
            axis = _VECTORS[k]
            shp = local[k].shape
            gfull = gk.reshape(shp[:axis] + (NDEV, shp[axis]) + shp[axis + 1:])
            shard_grad[k] = lax.dynamic_index_in_dim(gfull, me, axis, keepdims=False)
        else:
            shard_grad[k] = gk.reshape(local[k].shape)

    deltas, new_m, new_v = {}, {}, {}
    for k in _NAMES:
        deltas[k], new_m[k], new_v[k] = _adamw(local[k], shard_grad[k], mom[k], var[k], f"adamw_{k}")
    return (loss, dx[None], *[shard_grad[k] for k in _NAMES], *[deltas[k] for k in _NAMES],
            *[new_m[k] for k in _NAMES], *[new_v[k] for k in _NAMES])
```

```python
import numpy as np

import jax
import jax.numpy as jnp
from jax import lax
from jax.experimental import pallas as pl
from jax.experimental.pallas import tpu as pltpu

F32 = jnp.float32
BF16 = jnp.bfloat16
HI = lax.Precision.HIGHEST

NDEV = 8
D_MODEL = 1024
DEPTH = 4
D_FF = 2816
FF_BLOCK = 2 * D_FF // NDEV
CONV_BLOCK = 3 * D_MODEL // NDEV
RMS_EPS = 1e-6
HEAD_DIM = 64
ATTN_HEADS = 16
FOX_IN = 3 * D_MODEL + ATTN_HEADS
FOX_IN_PAD = 3200
SSM_INNER = 2048
SSM_HEADS = 32
SSM_GROUPS = 8
SSM_STATE = 128
SSM_CHUNK = 128
SSM_CONV_DIM = 4096
SSM_IN = SSM_INNER + SSM_CONV_DIM + SSM_HEADS
SSM_IN_PAD = 6272
LANES = 128
V7X_VMEM_BYTES = 64 * 1024 * 1024
VMEM_LIMIT_BYTES = (V7X_VMEM_BYTES * 3) // 4
LOG2E = 1.4426950408889634
LN2 = 0.6931471805599453
ATTN_ROWS = 64

ADAM_LR = 0.001
ADAM_B1 = 0.9
ADAM_B2 = 0.999
ADAM_EPS = 1e-08
ADAM_WD = 0.01
ADAM_STEP = 10

_TILE_CANDIDATES = (1024, 1408, 896, 768, 640, 512, 384, 256, 128)


def _pick_tile(n):
    for c in _TILE_CANDIDATES:
        if n % c == 0:
            return c
    raise ValueError(f"no tile for {n}")


def _params(ngrid):
    return pltpu.CompilerParams(dimension_semantics=("arbitrary",) * ngrid, vmem_limit_bytes=VMEM_LIMIT_BYTES)


def _pc(body, name, grid, in_specs, out_specs, out_shape, scratch=()):
    return pl.pallas_call(
        body, name=name, grid=grid, in_specs=in_specs, out_specs=out_specs, out_shape=out_shape,
        scratch_shapes=list(scratch), compiler_params=_params(len(grid)))


def _dot(a, b, ca, cb, prec=None):
    return lax.dot_general(a, b, (((ca,), (cb,)), ((), ())), preferred_element_type=F32, precision=prec)


def _sds(shape, dtype=F32):
    return jax.ShapeDtypeStruct(shape, dtype)


def _row_tile(s, want=256):
    return want if s % want == 0 else s


def _sigmoid(x):
    return 1.0 / (1.0 + jnp.exp(-x))


def _softplus(x):
    return jnp.maximum(x, 0.0) + jnp.log(1.0 + jnp.exp(-jnp.abs(x)))


def _mm_spec(a, b, name, grid, a_spec, b_spec, o_spec, out, ca, cb, acc_shape, drop=(0, 0, 0), res=None, r_spec=None):
    nk = grid[2]
    da, db, do_ = drop
    has_res = res is not None

    def body(*refs):
        if has_res:
            a_ref, b_ref, r_ref, o_ref, acc_ref = refs
        else:
            a_ref, b_ref, o_ref, acc_ref = refs
        k = pl.program_id(2)

        @pl.when(k == 0)
        def _():
            acc_ref[...] = jnp.zeros_like(acc_ref)

        av = a_ref[(0,) * da] if da else a_ref[...]
        bv = b_ref[(0,) * db] if db else b_ref[...]
        acc_ref[...] += _dot(av.astype(BF16), bv.astype(BF16), ca, cb)

        @pl.when(k == nk - 1)
        def _():
            val = acc_ref[...]
            if has_res:
                val = val + r_ref[...]
            if do_:
                o_ref[(0,) * do_] = val.astype(out.dtype)
            else:
                o_ref[...] = val.astype(out.dtype)

    in_specs = [a_spec, b_spec] + ([r_spec] if has_res else [])
    args = (a, b) + ((res,) if has_res else ())
    return _pc(body, name, grid, in_specs, o_spec, out, [pltpu.VMEM(acc_shape, F32)])(*args)


def _mm(a, b, mode, name, out_dtype=F32, res=None):
    if mode == "tn":
        r, m = a.shape
        n = b.shape[1]
        tm, tn, tk = _pick_tile(m), _pick_tile(n), _pick_tile(r)
        grid = (m // tm, n // tn, r // tk)
        a_spec = pl.BlockSpec((tk, tm), lambda i, j, k: (k, i))
        b_spec = pl.BlockSpec((tk, tn), lambda i, j, k: (k, j))
        ca, cb = 0, 0
    else:
        m, kd = a.shape
        n = b.shape[1] if mode == "nn" else b.shape[0]
        tm, tn, tk = _pick_tile(m), _pick_tile(n), _pick_tile(kd)
        grid = (m // tm, n // tn, kd // tk)
        a_spec = pl.BlockSpec((tm, tk), lambda i, j, k: (i, k))
        if mode == "nn":
            b_spec = pl.BlockSpec((tk, tn), lambda i, j, k: (k, j))
            ca, cb = 1, 0
        else:
            b_spec = pl.BlockSpec((tn, tk), lambda i, j, k: (j, k))
            ca, cb = 1, 1
    o_spec = pl.BlockSpec((tm, tn), lambda i, j, k: (i, j))
    return _mm_spec(a, b, name, grid, a_spec, b_spec, o_spec, _sds((m, n), out_dtype), ca, cb, (tm, tn), res=res, r_spec=o_spec)


def _rms_fwd(x, w, name):
    s, d = x.shape
    ts = _row_tile(s)

    def body(x_ref, w_ref, o_ref):
        xv = x_ref[...]
        r = lax.rsqrt(jnp.mean(xv * xv, axis=-1, keepdims=True) + RMS_EPS)
        o_ref[...] = ((xv * r) * w_ref[...]).astype(BF16)

    row = pl.BlockSpec((ts, d), lambda i: (i, 0))
    return _pc(body, name, (s // ts,), [row, pl.BlockSpec((1, d), lambda i: (0, 0))], row, _sds((s, d), BF16))(x, w)


def _rms_bwd(x, w, dh, dres, name):
    s, d = x.shape
    ts = _row_tile(s)

    def body(x_ref, w_ref, dh_ref, dr_ref, dx_ref, dw_ref):
        i = pl.program_id(0)
        xv = x_ref[...]
        r = lax.rsqrt(jnp.mean(xv * xv, axis=-1, keepdims=True) + RMS_EPS)
        xhat = xv * r
        dhv = dh_ref[...]
        g = dhv * w_ref[...]
        dx_ref[...] = dr_ref[...] + r * (g - xhat * jnp.mean(g * xhat, axis=-1, keepdims=True))

        @pl.when(i == 0)
        def _():
            dw_ref[...] = jnp.zeros_like(dw_ref)

        dw_ref[...] += jnp.sum(dhv * xhat, axis=0, keepdims=True)

    row = pl.BlockSpec((ts, d), lambda i: (i, 0))
    vec = pl.BlockSpec((1, d), lambda i: (0, 0))
    return _pc(body, name, (s // ts,), [row, vec, row, row], [row, vec], [_sds((s, d)), _sds((1, d))])(x, w, dh, dres)


def _swiglu_fwd(gu, name):
    s = gu.shape[2]
    ts = _row_tile(s)

    def body(gu_ref, o_ref):
        g = gu_ref[0, 0]
        u = gu_ref[0, 1]
        o_ref[0] = (g * _sigmoid(g) * u).astype(BF16)

    return _pc(body, name, (s // ts, 4), [pl.BlockSpec((1, 2, ts, FF_BLOCK), lambda i, k: (k, 0, i, 0))],
               pl.BlockSpec((1, ts, FF_BLOCK), lambda i, k: (k, i, 0)), _sds((4, s, FF_BLOCK), BF16))(gu)


def _swiglu_bwd(gu, da, name):
    s = gu.shape[2]
    ts = _row_tile(s)

    def body(gu_ref, da_ref, o_ref):
        g = gu_ref[0, 0]
        u = gu_ref[0, 1]
        dav = da_ref[0]
        sg = _sigmoid(g)
        o_ref[0, 0] = (dav * u * (sg * (1.0 + g * (1.0 - sg)))).astype(BF16)
        o_ref[0, 1] = (dav * (g * sg)).astype(BF16)

    pair = pl.BlockSpec((1, 2, ts, FF_BLOCK), lambda i, k: (k, 0, i, 0))
    return _pc(body, name, (s // ts, 4), [pair, pl.BlockSpec((1, ts, FF_BLOCK), lambda i, k: (k, i, 0))], pair,
               _sds((4, 2, s, FF_BLOCK), BF16))(gu, da)


def _ffn_fwd(x, norm_w, w_gu, w_down, tag):
    s = x.shape[0]
    tm = _pick_tile(s)
    h = _rms_fwd(x, norm_w, f"ffn_norm_{tag}")
    gu = _mm_spec(h, w_gu, f"ffn_gu_{tag}", (s // tm, NDEV, 1),
                  pl.BlockSpec((tm, D_MODEL), lambda i, j, k: (i, 0)),
                  pl.BlockSpec((1, D_MODEL, FF_BLOCK), lambda i, j, k: (j, 0, 0)),
                  pl.BlockSpec((1, 1, tm, FF_BLOCK), lambda i, j, k: (j % 4, j // 4, i, 0)),
                  _sds((4, 2, s, FF_BLOCK)), 1, 0, (tm, FF_BLOCK), drop=(0, 1, 2))
    a = _swiglu_fwd(gu, f"ffn_act_{tag}")
    xspec = pl.BlockSpec((tm, D_MODEL), lambda i, j, k: (i, 0))
    y = _mm_spec(a, w_down, f"ffn_down_{tag}", (s // tm, 1, 4),
                 pl.BlockSpec((1, tm, FF_BLOCK), lambda i, j, k: (k, i, 0)),
                 pl.BlockSpec((1, FF_BLOCK, D_MODEL), lambda i, j, k: (k, 0, 0)),
                 xspec, _sds((s, D_MODEL)), 1, 0, (tm, D_MODEL), drop=(1, 1, 0), res=x, r_spec=xspec)
    return y, (x, h, gu, a)


def _ffn_bwd(dy, saved, norm_w, w_gu, w_down, tag):
    x, h, gu, a = saved
    s = x.shape[0]
    tm = _pick_tile(s)
    row = pl.BlockSpec((tm, D_MODEL), lambda i, j, k: (i, 0))
    da = _mm_spec(dy, w_down, f"ffn_dact_{tag}", (s // tm, 4, 1), row,
                  pl.BlockSpec((1, FF_BLOCK, D_MODEL), lambda i, j, k: (j, 0, 0)),
                  pl.BlockSpec((1, tm, FF_BLOCK), lambda i, j, k: (j, i, 0)),
                  _sds((4, s, FF_BLOCK)), 1, 1, (tm, FF_BLOCK), drop=(0, 1, 1))
    g_down = _mm_spec(a, dy, f"ffn_gdown_{tag}", (4, 1, s // tm),
                      pl.BlockSpec((1, tm, FF_BLOCK), lambda i, j, k: (i, k, 0)),
                      pl.BlockSpec((tm, D_MODEL), lambda i, j, k: (k, 0)),
                      pl.BlockSpec((1, FF_BLOCK, D_MODEL), lambda i, j, k: (i, 0, 0)),
                      _sds((4, FF_BLOCK, D_MODEL), BF16), 0, 0, (FF_BLOCK, D_MODEL), drop=(1, 0, 1))
    dgu = _swiglu_bwd(gu, da, f"ffn_dgu_{tag}")
    g_gu = _mm_spec(h, dgu, f"ffn_ggu_{tag}", (NDEV, 1, s // tm),
                    pl.BlockSpec((tm, D_MODEL), lambda i, j, k: (k, 0)),
                    pl.BlockSpec((1, 1, tm, FF_BLOCK), lambda i, j, k: (i % 4, i // 4, k, 0)),
                    pl.BlockSpec((1, D_MODEL, FF_BLOCK), lambda i, j, k: (i, 0, 0)),
                    _sds((NDEV, D_MODEL, FF_BLOCK), BF16), 0, 0, (D_MODEL, FF_BLOCK), drop=(0, 2, 1))
    dh = _mm_spec(dgu, w_gu, f"ffn_dh_{tag}", (s // tm, 1, NDEV),
                  pl.BlockSpec((1, 1, tm, FF_BLOCK), lambda i, j, k: (k % 4, k // 4, i, 0)),
                  pl.BlockSpec((1, D_MODEL, FF_BLOCK), lambda i, j, k: (k, 0, 0)),
                  row, _sds((s, D_MODEL)), 1, 1, (tm, D_MODEL), drop=(2, 1, 0))
    dx, g_norm = _rms_bwd(x, norm_w, dh, dy, f"ffn_dnorm_{tag}")
    return dx, g_norm, g_gu, g_down


def _prev_rows(cur, halo, j, first):
    rid = lax.broadcasted_iota(jnp.int32, cur.shape, 0)
    hid = lax.broadcasted_iota(jnp.int32, halo.shape, 0)
    out = pltpu.roll(cur, j, 0)
    for t in range(j):
        row = jnp.sum(jnp.where(hid == 8 - j + t, halo, 0.0), axis=0, keepdims=True)
        row = jnp.where(first, 0.0, row)
        out = jnp.where(rid == t, row, out)
    return out


def _next_rows(cur, halo, j, last):
    ts = cur.shape[0]
    rid = lax.broadcasted_iota(jnp.int32, cur.shape, 0)
    hid = lax.broadcasted_iota(jnp.int32, halo.shape, 0)
    out = pltpu.roll(cur, ts - j, 0)
    for t in range(j):
        row = jnp.sum(jnp.where(hid == t, halo, 0.0), axis=0, keepdims=True)
        row = jnp.where(last, 0.0, row)
        out = jnp.where(rid == ts - j + t, row, out)
    return out


def _halo_specs(ts, s, width, col):
    per = ts // 8
    nblk = s // 8
    prev = pl.BlockSpec((8, width), lambda i: (jnp.maximum(i * per - 1, 0), col))
    nxt = pl.BlockSpec((8, width), lambda i: (jnp.minimum((i + 1) * per, nblk - 1), col))
    return prev, nxt


def _cgate_fwd(p, w_dw, name):
    s = p.shape[0]
    d = D_MODEL
    ts = _row_tile(s)
    prev, _ = _halo_specs(ts, s, 3 * d, 0)

    def body(p_ref, h_ref, w_ref, z_ref):
        first = pl.program_id(0) == 0
        b = p_ref[:, :d]
        cv = p_ref[:, d:2 * d] * p_ref[:, 2 * d:]
        hcv = h_ref[:, d:2 * d] * h_ref[:, 2 * d:]
        u = w_ref[2:3, :] * cv + w_ref[1:2, :] * _prev_rows(cv, hcv, 1, first) + w_ref[0:1, :] * _prev_rows(cv, hcv, 2, first)
        z_ref[...] = (b * u).astype(BF16)

    return _pc(body, name, (s // ts,),
               [pl.BlockSpec((ts, 3 * d), lambda i: (i, 0)), prev, pl.BlockSpec((3, d), lambda i: (0, 0))],
               pl.BlockSpec((ts, d), lambda i: (i, 0)), _sds((s, d), BF16))(p, p, w_dw)


def _cgate_bwd(p, dz, w_dw, name):
    s = p.shape[0]
    d = D_MODEL
    ts = _row_tile(s)
    nt = s // ts
    p_prev, p_next = _halo_specs(ts, s, 3 * d, 0)
    _, dz_next = _halo_specs(ts, s, d, 0)

    def body(p_ref, hp_ref, hn_ref, dz_ref, dzn_ref, w_ref, dp_ref, dw_ref):
        i = pl.program_id(0)
        first = i == 0
        last = i == nt - 1
        b = p_ref[:, :d]
        c = p_ref[:, d:2 * d]
        v = p_ref[:, 2 * d:]
        cv = c * v
        hcv = hp_ref[:, d:2 * d] * hp_ref[:, 2 * d:]
        cv1 = _prev_rows(cv, hcv, 1, first)
        cv2 = _prev_rows(cv, hcv, 2, first)
        w0, w1, w2 = w_ref[0:1, :], w_ref[1:2, :], w_ref[2:3, :]
        u = w2 * cv + w1 * cv1 + w0 * cv2
        dzv = dz_ref[...]
        du = dzv * b
        dun = dzn_ref[...] * hn_ref[:, :d]
        dcv = w2 * du + w1 * _next_rows(du, dun, 1, last) + w0 * _next_rows(du, dun, 2, last)
        dp_ref[:, :d] = (dzv * u).astype(BF16)
        dp_ref[:, d:2 * d] = (dcv * v).astype(BF16)
        dp_ref[:, 2 * d:] = (dcv * c).astype(BF16)

        @pl.when(first)
        def _():
            dw_ref[...] = jnp.zeros_like(dw_ref)

        dw_ref[0:1, :] += jnp.sum(du * cv2, axis=0, keepdims=True)
        dw_ref[1:2, :] += jnp.sum(du * cv1, axis=0, keepdims=True)
        dw_ref[2:3, :] += jnp.sum(du * cv, axis=0, keepdims=True)

    wide = pl.BlockSpec((ts, 3 * d), lambda i: (i, 0))
    wspec = pl.BlockSpec((3, d), lambda i: (0, 0))
    return _pc(body, name, (nt,),
               [wide, p_prev, p_next, pl.BlockSpec((ts, d), lambda i: (i, 0)), dz_next, wspec],
               [wide, wspec], [_sds((s, 3 * d), BF16), _sds((3, d))])(p, p, p, dz, dz, w_dw)


def _conv_fwd(x, norm_w, w_in, w_dw, w_out, tag):
    s = x.shape[0]
    tm = _pick_tile(s)
    h = _rms_fwd(x, norm_w, f"conv_norm_{tag}")
    p = _mm_spec(h, w_in, f"conv_in_{tag}", (s // tm, NDEV, 1),
                 pl.BlockSpec((tm, D_MODEL), lambda i, j, k: (i, 0)),
                 pl.BlockSpec((1, D_MODEL, CONV_BLOCK), lambda i, j, k: (j, 0, 0)),
                 pl.BlockSpec((tm, CONV_BLOCK), lambda i, j, k: (i, j)),
                 _sds((s, 3 * D_MODEL)), 1, 0, (tm, CONV_BLOCK), drop=(0, 1, 0))
    z = _cgate_fwd(p, w_dw, f"conv_gate_{tag}")
    y = _mm(z, w_out, "nn", f"conv_out_{tag}", res=x)
    return y, (x, h, p, z)


def _conv_bwd(dy, saved, norm_w, w_in, w_dw, w_out, tag):
    x, h, p, z = saved
    s = x.shape[0]
    tm = _pick_tile(s)
    dz = _mm(dy, w_out, "nt", f"conv_dz_{tag}")
    g_out = _mm(z, dy, "tn", f"conv_gout_{tag}", out_dtype=BF16)
    dp, g_dw = _cgate_bwd(p, dz, w_dw, f"conv_dgate_{tag}")
    g_in = _mm_spec(h, dp, f"conv_gin_{tag}", (NDEV, 1, s // tm),
                    pl.BlockSpec((tm, D_MODEL), lambda i, j, k: (k, 0)),
                    pl.BlockSpec((tm, CONV_BLOCK), lambda i, j, k: (k, i)),
                    pl.BlockSpec((1, D_MODEL, CONV_BLOCK), lambda i, j, k: (i, 0, 0)),
                    _sds((NDEV, D_MODEL, CONV_BLOCK), BF16), 0, 0, (D_MODEL, CONV_BLOCK), drop=(0, 0, 1))
    dh = _mm_spec(dp, w_in, f"conv_dh_{tag}", (s // tm, 1, NDEV),
                  pl.BlockSpec((tm, CONV_BLOCK), lambda i, j, k: (i, k)),
                  pl.BlockSpec((1, D_MODEL, CONV_BLOCK), lambda i, j, k: (k, 0, 0)),
                  pl.BlockSpec((tm, D_MODEL), lambda i, j, k: (i, 0)),
                  _sds((s, D_MODEL)), 1, 1, (tm, D_MODEL), drop=(0, 1, 0))
    dx, g_norm = _rms_bwd(x, norm_w, dh, dy, f"conv_dnorm_{tag}")
    return dx, g_norm, g_in, g_dw, g_out


def _tri(lower):
    r = lax.broadcasted_iota(jnp.int32, (LANES, LANES), 0)
    c = lax.broadcasted_iota(jnp.int32, (LANES, LANES), 1)
    return jnp.where((r >= c) if lower else (r <= c), 1.0, 0.0).astype(F32)


def _cumsum_rows(v, reverse, name):
    s = v.shape[0]
    n = s // LANES
    idx = (lambda i: (n - 1 - i, 0)) if reverse else (lambda i: (i, 0))

    def body(v_ref, o_ref, carry_ref):
        @pl.when(pl.program_id(0) == 0)
        def _():
            carry_ref[...] = jnp.zeros_like(carry_ref)

        blk = v_ref[...]
        o_ref[...] = _dot(_tri(not reverse), blk, 1, 0, HI) + carry_ref[0:1, :]
        carry_ref[...] += jnp.sum(blk, axis=0, keepdims=True)

    spec = pl.BlockSpec((LANES, LANES), idx)
    return _pc(body, name, (n,), [spec], spec, _sds((s, LANES)), [pltpu.VMEM((8, LANES), F32)])(v)


def _lo_mask(shape):
    return lax.broadcasted_iota(jnp.int32, shape, len(shape) - 1) < HEAD_DIM


def _half_sums(v, lo):
    sa = jnp.sum(jnp.where(lo, v, 0.0), axis=-1, keepdims=True)
    sb = jnp.sum(jnp.where(lo, 0.0, v), axis=-1, keepdims=True)
    return jnp.where(lo, sa, sb)


def _fox_prep_fwd(proj, gq, gk, name):
    s = proj.shape[0]
    ts = _row_tile(s, 512)
    qscale = HEAD_DIM ** -0.5 * LOG2E

    def body(q_ref, k_ref, v_ref, gq_ref, gk_ref, qo_ref, ko_ref, vo_ref):
        lo = _lo_mask((ts, LANES))

        def hnorm(xv, g):
            ms = _half_sums(xv * xv, lo) * (1.0 / HEAD_DIM)
            return (xv * lax.rsqrt(ms + RMS_EPS)) * g

        qo_ref[...] = (hnorm(q_ref[...], gq_ref[...]) * qscale).astype(BF16)
        ko_ref[...] = hnorm(k_ref[...], gk_ref[...]).astype(BF16)
        vo_ref[...] = v_ref[...].astype(BF16)

    def col(off):
        return pl.BlockSpec((ts, LANES), lambda i, p: (i, off + p))

    gspec = pl.BlockSpec((1, LANES), lambda i, p: (0, 0))
    out = _sds((s, D_MODEL), BF16)
    return _pc(body, name, (s // ts, 8), [col(0), col(8), col(16), gspec, gspec], [col(0)] * 3, [out] * 3)(
        proj, proj, proj, gq, gk)


def _fox_logf(proj, bf, name):
    s = proj.shape[0]
    ts = _row_tile(s, 512)

    def body(f_ref, b_ref, o_ref):
        z = f_ref[...] + b_ref[...]
        lf = jnp.minimum(z, 0.0) - jnp.log(1.0 + jnp.exp(-jnp.abs(z)))
        real = lax.broadcasted_iota(jnp.int32, (ts, LANES), 1) < ATTN_HEADS
        o_ref[...] = jnp.where(real, lf, 0.0)

    return _pc(body, name, (s // ts,), [pl.BlockSpec((ts, LANES), lambda i: (i, 24)), pl.BlockSpec((1, LANES), lambda i: (0, 0))],
               pl.BlockSpec((ts, LANES), lambda i: (i, 0)), _sds((s, LANES)))(proj, bf)


def _fox_dlogf(proj, bf, dlf, name):
    s = proj.shape[0]
    ts = _row_tile(s, 512)

    def body(f_ref, b_ref, d_ref, o_ref, db_ref):
        z = f_ref[...] + b_ref[...]
        real = lax.broadcasted_iota(jnp.int32, (ts, LANES), 1) < ATTN_HEADS
        g = jnp.where(real, d_ref[...] * _sigmoid(-z), 0.0)
        o_ref[...] = g.astype(BF16)

        @pl.when(pl.program_id(0) == 0)
        def _():
            db_ref[...] = jnp.zeros_like(db_ref)

        db_ref[...] += jnp.sum(g, axis=0, keepdims=True)

    vec = pl.BlockSpec((1, LANES), lambda i: (0, 0))
    row = pl.BlockSpec((ts, LANES), lambda i: (i, 0))
    return _pc(body, name, (s // ts,), [pl.BlockSpec((ts, LANES), lambda i: (i, 24)), vec, row], [row, vec],
               [_sds((s, LANES), BF16), _sds((1, LANES))])(proj, bf, dlf)


def _decay_terms(cum):
    s = cum.shape[0]
    c2 = cum * LOG2E
    hi = lax.reduce_precision(c2, 8, 7)
    mid = lax.reduce_precision(c2 - hi, 8, 7)
    low = lax.reduce_precision(c2 - hi - mid, 8, 7)
    one = jnp.ones_like(hi)

    def place(terms):
        tt = jnp.stack(terms, axis=-1).astype(BF16).reshape(s, 8, 2, 6)
        z = jnp.zeros((s, 8, HEAD_DIM - 6), BF16)
        return jnp.concatenate([tt[:, :, 1], z, tt[:, :, 0], z], axis=-1).reshape(s, D_MODEL)

    return place([hi, mid, low, one, one, one]), place([one, one, one, -hi, -mid, -low])


def _attn_tiles(s):
    t = 512 if s % 512 == 0 else s
    return t, s // t


def _tri_steps(n, by_key):
    if by_key:
        pairs = [(q, k) for k in range(n) for q in range(k, n)]
    else:
        pairs = [(q, k) for q in range(n) for k in range(q + 1)]
    arr = np.asarray(pairs, np.int32)
    return jnp.asarray(arr[:, 0]), jnp.asarray(arr[:, 1])


def _attn_call(body, name, s, by_key, inputs, in_kinds, out_kinds, out_shapes, scratch):
    t, n = _attn_tiles(s)
    qi_arr, ki_arr = _tri_steps(n, by_key)
    specs = {
        "q": pl.BlockSpec((t, LANES), lambda p, i, qi, ki: (qi[i], p)),
        "k": pl.BlockSpec((t, LANES), lambda p, i, qi, ki: (ki[i], p)),
        "r": pl.BlockSpec((1, 2, t), lambda p, i, qi, ki: (p, 0, qi[i])),
        "m": pl.BlockSpec((1, t, t), lambda p, i, qi, ki: (jnp.where(qi[i] == ki[i], 1, 0), 0, 0)),
    }
    grid_spec = pltpu.PrefetchScalarGridSpec(
        num_scalar_prefetch=2, grid=(8, int(qi_arr.shape[0])), in_specs=[specs[c] for c in in_kinds],
        out_specs=[specs[c] for c in out_kinds], scratch_shapes=list(scratch))
    return pl.pallas_call(body, name=name, grid_spec=grid_spec, out_shape=out_shapes, compiler_params=_params(2))(
        qi_arr, ki_arr, *inputs)


def _biased_kq(q2, k2, aq, ak, lo):
    sa = _dot(jnp.where(lo, k2, ak), jnp.where(lo, q2, aq), 1, 1)
    sb = _dot(jnp.where(lo, ak, k2), jnp.where(lo, aq, q2), 1, 1)
    return sa, sb


def _causal_bias(s):
    t, _ = _attn_tiles(s)
    kid = lax.broadcasted_iota(jnp.int32, (t, t), 0)
    qid = lax.broadcasted_iota(jnp.int32, (t, t), 1)
    return jnp.stack([jnp.zeros((t, t), F32), jnp.where(kid > qid, -jnp.inf, 0.0).astype(F32)])


def _fold8(v, op):
    return op(v.reshape(v.shape[0] // 8, 8, v.shape[1]), axis=0)


def _chunk(ref, mask_ref, hd, r):
    rows = slice(r * ATTN_ROWS, (r + 1) * ATTN_ROWS)
    return rows, ref[hd, rows, :] + mask_ref[0, rows, :]


def _flash_fwd(qs, kn, vb, augq, augk, cmask, name):
    s = qs.shape[0]
    t, n = _attn_tiles(s)
    nch = t // ATTN_ROWS

    def body(qi_ref, ki_ref, q_ref, k_ref, v_ref, aq_ref, ak_ref, mk_ref, o_ref, lse_ref, s_ref, p_ref, m_ref, l_ref, acc_ref):
        i = pl.program_id(1)
        qi = qi_ref[i]
        ki = ki_ref[i]

        @pl.when(ki == 0)
        def _():
            m_ref[...] = jnp.full_like(m_ref, -jnp.inf)
            l_ref[...] = jnp.zeros_like(l_ref)
            acc_ref[...] = jnp.zeros_like(acc_ref)

        lo = _lo_mask((t, LANES))
        rowlo = lax.broadcasted_iota(jnp.int32, (LANES, t), 0) < HEAD_DIM
        v2 = v_ref[...]
        sa, sb = _biased_kq(q_ref[...], k_ref[...], aq_ref[...], ak_ref[...], lo)
        s_ref[0] = sa
        s_ref[1] = sb
        alphas, pvs = [], []
        for hd in range(2):
            mx = jnp.full((8, t), -jnp.inf, F32)
            for r in range(nch):
                _, sc = _chunk(s_ref, mk_ref, hd, r)
                mx = jnp.maximum(mx, _fold8(sc, jnp.max))
            m_prev = m_ref[hd:hd + 1, :]
            m_new = jnp.maximum(m_prev, jnp.max(mx, axis=0, keepdims=True))
            ls = jnp.zeros((8, t), F32)
            for r in range(nch):
                rows, sc = _chunk(s_ref, mk_ref, hd, r)
                pm = jnp.exp2(sc - m_new)
                ls = ls + _fold8(pm, jnp.sum)
                p_ref[hd, rows, :] = pm.astype(BF16)
            alpha = jnp.exp2(m_prev - m_new)
            l_ref[hd:hd + 1, :] = alpha * l_ref[hd:hd + 1, :] + jnp.sum(ls, axis=0, keepdims=True)
            m_ref[hd:hd + 1, :] = m_new
            alphas.append(alpha)
            pvs.append(_dot(v2, p_ref[hd], 0, 0))
        acc_ref[...] = jnp.where(rowlo, alphas[0], alphas[1]) * acc_ref[...] + jnp.where(rowlo, pvs[0], pvs[1])

        @pl.when(ki == qi)
        def _():
            o_ref[...] = (acc_ref[...] / jnp.where(rowlo, l_ref[0:1, :], l_ref[1:2, :])).T
            lse_ref[0] = m_ref[0:2, :] + jnp.log2(l_ref[0:2, :])

    stat = pltpu.VMEM((8, t), F32)
    return _attn_call(body, name, s, False, (qs, kn, vb, augq, augk, cmask), "qkkqkm", "qr",
                      [_sds((s, D_MODEL)), _sds((8, 2, s))],
                      [pltpu.VMEM((2, t, t), F32), pltpu.VMEM((2, t, t), BF16), stat, stat, pltpu.VMEM((LANES, t), F32)])


def _fox_delta(do, o, name):
    s = do.shape[0]
    ts = _row_tile(s, 512)

    def body(do_ref, o_ref, d_ref):
        d_ref[...] = _half_sums(do_ref[...] * o_ref[...], _lo_mask((ts, LANES)))

    spec = pl.BlockSpec((ts, LANES), lambda i, p: (i, p))
    return _pc(body, name, (s // ts, 8), [spec, spec], spec, _sds((s, D_MODEL)))(do, o)


def _bwd_tile(q_ref, k_ref, v_ref, aq_ref, ak_ref, do_ref, s_ref, dp_ref, lo):
    do2 = do_ref[...].astype(BF16)
    zero = jnp.zeros_like(do2)
    v2 = v_ref[...]
    sa, sb = _biased_kq(q_ref[...], k_ref[...], aq_ref[...], ak_ref[...], lo)
    s_ref[0] = sa
    s_ref[1] = sb
    dp_ref[0] = _dot(v2, jnp.where(lo, do2, zero), 1, 1)
    dp_ref[1] = _dot(v2, jnp.where(lo, zero, do2), 1, 1)
    return do2


def _bwd_chunk(s_ref, dp_ref, mk_ref, lse_ref, dl_ref, hd, r):
    rows, sc = _chunk(s_ref, mk_ref, hd, r)
    pm = jnp.exp2(sc - lse_ref[0, hd:hd + 1, :])
    ds = pm * (dp_ref[hd, rows, :] - dl_ref[0, hd:hd + 1, :])
    return rows, pm, ds


def _flash_bwd_dq(qs, kn, vb, augq, augk, cmask, do, lse, delta, name):
    s = qs.shape[0]
    t, n = _attn_tiles(s)
    nch = t // ATTN_ROWS

    def body(qi_ref, ki_ref, q_ref, k_ref, v_ref, aq_ref, ak_ref, mk_ref, do_ref, lse_ref, dl_ref, dq_ref, dcq_ref,
             s_ref, dp_ref, ds_ref, acc_ref, racc_ref):
        i = pl.program_id(1)
        qi = qi_ref[i]
        ki = ki_ref[i]

        @pl.when(ki == 0)
        def _():
            acc_ref[...] = jnp.zeros_like(acc_ref)
            racc_ref[...] = jnp.zeros_like(racc_ref)

        lo = _lo_mask((t, LANES))
        rowlo = lax.broadcasted_iota(jnp.int32, (LANES, t), 0) < HEAD_DIM
        _bwd_tile(q_ref, k_ref, v_ref, aq_ref, ak_ref, do_ref, s_ref, dp_ref, lo)
        k2 = k_ref[...]
        dqs = []
        for hd in range(2):
            rs = jnp.zeros((8, t), F32)
            for r in range(nch):
                rows, _, ds = _bwd_chunk(s_ref, dp_ref, mk_ref, lse_ref, dl_ref, hd, r)
                rs = rs + _fold8(ds, jnp.sum)
                ds_ref[hd, rows, :] = ds.astype(BF16)
            racc_ref[hd:hd + 1, :] += jnp.sum(rs, axis=0, keepdims=True)
            dqs.append(_dot(k2, ds_ref[hd], 0, 0))
        acc_ref[...] += jnp.where(rowlo, dqs[0], dqs[1])

        @pl.when(ki == qi)
        def _():
            dq_ref[...] = acc_ref[...].T
            dcq_ref[0] = racc_ref[0:2, :]

    return _attn_call(body, name, s, False, (qs, kn, vb, augq, augk, cmask, do, lse, delta), "qkkqkmqrr", "qr",
                      [_sds((s, D_MODEL)), _sds((8, 2, s))],
                      [pltpu.VMEM((2, t, t), F32), pltpu.VMEM((2, t, t), F32), pltpu.VMEM((2, t, t), BF16),
                       pltpu.VMEM((LANES, t), F32), pltpu.VMEM((8, t), F32)])


def _flash_bwd_dkv(qs, kn, vb, augq, augk, cmask, do, lse, delta, name):
    s = qs.shape[0]
    t, n = _attn_tiles(s)
    nch = t // ATTN_ROWS

    def body(qi_ref, ki_ref, q_ref, k_ref, v_ref, aq_ref, ak_ref, mk_ref, do_ref, lse_ref, dl_ref, dk_ref, dv_ref, dc_ref,
             s_ref, dp_ref, p_ref, ds_ref, dka_ref, dva_ref, dca_ref):
        i = pl.program_id(1)
        qi = qi_ref[i]
        ki = ki_ref[i]

        @pl.when(qi == ki)
        def _():
            dka_ref[...] = jnp.zeros_like(dka_ref)
            dva_ref[...] = jnp.zeros_like(dva_ref)
            dca_ref[...] = jnp.zeros_like(dca_ref)

        lo = _lo_mask((t, LANES))
        do2 = _bwd_tile(q_ref, k_ref, v_ref, aq_ref, ak_ref, do_ref, s_ref, dp_ref, lo)
        q2 = q_ref[...]
        dvs, dks = [], []
        for hd in range(2):
            for r in range(nch):
                rows, pm, ds = _bwd_chunk(s_ref, dp_ref, mk_ref, lse_ref, dl_ref, hd, r)
                part = ds[:, 0:LANES]
                for c in range(1, t // LANES):
                    part = part + ds[:, c * LANES:(c + 1) * LANES]
                dca_ref[hd, rows, :] += part
                p_ref[hd, rows, :] = pm.astype(BF16)
                ds_ref[hd, rows, :] = ds.astype(BF16)
            dvs.append(_dot(p_ref[hd], do2, 1, 0))
            dks.append(_dot(ds_ref[hd], q2, 1, 0))
        dva_ref[...] += jnp.where(lo, dvs[0], dvs[1])
        dka_ref[...] += jnp.where(lo, dks[0], dks[1])

        @pl.when(qi == n - 1)
        def _():
            dk_ref[...] = dka_ref[...] * LN2
            dv_ref[...] = dva_ref[...]
            dc_ref[...] = -jnp.where(lo, jnp.sum(dca_ref[0], axis=-1, keepdims=True), jnp.sum(dca_ref[1], axis=-1, keepdims=True))

    out = _sds((s, D_MODEL))
    return _attn_call(body, name, s, True, (qs, kn, vb, augq, augk, cmask, do, lse, delta), "qkkqkmqrr", "kkk", [out, out, out],
                      [pltpu.VMEM((2, t, t), F32), pltpu.VMEM((2, t, t), F32), pltpu.VMEM((2, t, t), BF16),
                       pltpu.VMEM((2, t, t), BF16), pltpu.VMEM((t, LANES), F32), pltpu.VMEM((t, LANES), F32),
                       pltpu.VMEM((2, t, LANES), F32)])


def _fox_prep_bwd(proj, dqs, dk, dv, gq, gk, name):
    s = proj.shape[0]
    ts = _row_tile(s, 512)
    scale = HEAD_DIM ** -0.5

    def body(q_ref, k_ref, dq_ref, dk_ref, dv_ref, gq_ref, gk_ref, oq_ref, ok_ref, ov_ref, dgq_ref, dgk_ref):
        lo = _lo_mask((ts, LANES))

        @pl.when(jnp.logical_and(pl.program_id(0) == 0, pl.program_id(1) == 0))
        def _():
            dgq_ref[...] = jnp.zeros_like(dgq_ref)
            dgk_ref[...] = jnp.zeros_like(dgk_ref)

        def back(xv, dout, g):
            r = lax.rsqrt(_half_sums(xv * xv, lo) * (1.0 / HEAD_DIM) + RMS_EPS)
            y = xv * r
            dy = dout * g
            dx = r * (dy - y * (_half_sums(dy * y, lo) * (1.0 / HEAD_DIM)))
            return dx, jnp.sum(dout * y, axis=0, keepdims=True)

        dxq, dgq = back(q_ref[...], dq_ref[...] * scale, gq_ref[...])
        dxk, dgk = back(k_ref[...], dk_ref[...], gk_ref[...])
        oq_ref[...] = dxq.astype(BF16)
        ok_ref[...] = dxk.astype(BF16)
        ov_ref[...] = dv_ref[...].astype(BF16)
        dgq_ref[...] += dgq
        dgk_ref[...] += dgk

    def col(off):
        return pl.BlockSpec((ts, LANES), lambda i, p: (i, off + p))

    gspec = pl.BlockSpec((1, LANES), lambda i, p: (0, 0))
    out = _sds((s, D_MODEL), BF16)
    return _pc(body, name, (s // ts, 8), [col(0), col(8), col(0), col(0), col(0), gspec, gspec],
               [col(0)] * 3 + [gspec] * 2, [out] * 3 + [_sds((1, LANES))] * 2)(proj, proj, dqs, dk, dv, gq, gk)


def _fox_fwd(x, norm_w, w_in, b_f, q_gain, k_gain, w_out):
    h = _rms_fwd(x, norm_w, "fox_norm")
    proj = _mm(h, w_in, "nn", "fox_in")
    gq = jnp.tile(q_gain, (1, 2))
    gk = jnp.tile(k_gain, (1, 2))
    bf = jnp.pad(b_f, ((0, 0), (0, LANES - ATTN_HEADS)))
    qs, kn, vb = _fox_prep_fwd(proj, gq, gk, "fox_prep")
    cum = _cumsum_rows(_fox_logf(proj, bf, "fox_logf"), False, "fox_cum")[:, :ATTN_HEADS]
    augq, augk = _decay_terms(cum)
    cmask = _causal_bias(x.shape[0])
    o, lse = _flash_fwd(qs, kn, vb, augq, augk, cmask, "fox_attn")
    y = _mm(o, w_out, "nn", "fox_out", res=x)
    return y, (x, h, proj, gq, gk, bf, qs, kn, vb, augq, augk, cmask, o, lse)


def _fox_bwd(dy, saved, norm_w, w_in, w_out):
    x, h, proj, gq, gk, bf, qs, kn, vb, augq, augk, cmask, o, lse = saved
    s = x.shape[0]
    do = _mm(dy, w_out, "nt", "fox_do")
    g_out = _mm(o, dy, "tn", "fox_gout", out_dtype=BF16)
    delta = _fox_delta(do, o, "fox_delta")[:, ::HEAD_DIM].T.reshape(8, 2, s)
    dqs, dcq = _flash_bwd_dq(qs, kn, vb, augq, augk, cmask, do, lse, delta, "fox_dq")
    dk, dv, dck = _flash_bwd_dkv(qs, kn, vb, augq, augk, cmask, do, lse, delta, "fox_dkv")
    dcum = jnp.pad(dcq.reshape(ATTN_HEADS, s).T + dck[:, ::HEAD_DIM], ((0, 0), (0, LANES - ATTN_HEADS)))
    dlf = _cumsum_rows(dcum, True, "fox_dcum")
    dfl, g_bf = _fox_dlogf(proj, bf, dlf, "fox_dlogf")
    dq_o, dk_o, dv_o, g_gq, g_gk = _fox_prep_bwd(proj, dqs, dk, dv, gq, gk, "fox_dprep")
    dproj = jnp.concatenate([dq_o, dk_o, dv_o, dfl], axis=1)
    g_in = _mm(h, dproj, "tn", "fox_gin", out_dtype=BF16)
    dh = _mm(dproj, w_in, "nt", "fox_dh")
    dx, g_norm = _rms_bwd(x, norm_w, dh, dy, "fox_dnorm")
    g_q = g_gq[:, :HEAD_DIM] + g_gq[:, HEAD_DIM:]
    g_k = g_gk[:, :HEAD_DIM] + g_gk[:, HEAD_DIM:]
    return dx, g_norm, g_in[:, :FOX_IN], g_bf[:, :ATTN_HEADS], g_q, g_k, g_out


def _ssd_conv_fwd(proj, cw, cb, name):
    s = proj.shape[0]
    ts = _row_tile(s)
    w = 1024
    per = ts // 8

    def body(p_ref, h_ref, w_ref, b_ref, o_ref):
        first = pl.program_id(0) == 0
        cur = p_ref[...]
        halo = h_ref[...]
        u = w_ref[3:4, :] * cur + b_ref[...]
        for j in range(1, 4):
            u = u + w_ref[3 - j:4 - j, :] * _prev_rows(cur, halo, j, first)
        o_ref[...] = u * _sigmoid(u)

    return _pc(body, name, (s // ts, 4),
               [pl.BlockSpec((ts, w), lambda i, j: (i, 2 + j)),
                pl.BlockSpec((8, w), lambda i, j: (jnp.maximum(i * per - 1, 0), 2 + j)),
                pl.BlockSpec((4, w), lambda i, j: (0, j)), pl.BlockSpec((1, w), lambda i, j: (0, j))],
               pl.BlockSpec((ts, w), lambda i, j: (i, j)), _sds((s, SSM_CONV_DIM)))(proj, proj, cw, cb)


def _ssd_conv_bwd_act(proj, dxbc, cw, cb, name):
    s = proj.shape[0]
    ts = _row_tile(s)
    w = 1024
    per = ts // 8

    def body(p_ref, h_ref, d_ref, w_ref, b_ref, g_ref, db_ref):
        first = pl.program_id(1) == 0
        cur = p_ref[...]
        halo = h_ref[...]
        u = w_ref[3:4, :] * cur + b_ref[...]
        for j in range(1, 4):
            u = u + w_ref[3 - j:4 - j, :] * _prev_rows(cur, halo, j, first)
        sg = _sigmoid(u)
        g = d_ref[...] * (sg * (1.0 + u * (1.0 - sg)))
        g_ref[...] = g

        @pl.when(first)
        def _():
            db_ref[...] = jnp.zeros_like(db_ref)

        db_ref[...] += jnp.sum(g, axis=0, keepdims=True)

    vec = pl.BlockSpec((1, w), lambda j, i: (0, j))
    tile = pl.BlockSpec((ts, w), lambda j, i: (i, j))
    return _pc(body, name, (4, s // ts),
               [pl.BlockSpec((ts, w), lambda j, i: (i, 2 + j)),
                pl.BlockSpec((8, w), lambda j, i: (jnp.maximum(i * per - 1, 0), 2 + j)),
                tile, pl.BlockSpec((4, w), lambda j, i: (0, j)), vec],
               [tile, vec], [_sds((s, SSM_CONV_DIM)), _sds((1, SSM_CONV_DIM))])(proj, proj, dxbc, cw, cb)


def _ssd_conv_bwd_in(proj, g, cw, name):
    s = proj.shape[0]
    ts = _row_tile(s)
    nt = s // ts
    w = 1024
    per = ts // 8
    nblk = s // 8

    def body(p_ref, h_ref, g_ref, gn_ref, w_ref, o_ref, dw_ref):
        i = pl.program_id(1)
        first = i == 0
        last = i == nt - 1
        cur = p_ref[...]
        halo = h_ref[...]
        gv = g_ref[...]
        gn = gn_ref[...]

        @pl.when(first)
        def _():
            dw_ref[...] = jnp.zeros_like(dw_ref)

        dpre = w_ref[3:4, :] * gv
        dw_ref[3:4, :] += jnp.sum(gv * cur, axis=0, keepdims=True)
        for j in range(1, 4):
            dpre = dpre + w_ref[3 - j:4 - j, :] * _next_rows(gv, gn, j, last)
            dw_ref[3 - j:4 - j, :] += jnp.sum(gv * _prev_rows(cur, halo, j, first), axis=0, keepdims=True)
        o_ref[...] = dpre.astype(BF16)

    tile = pl.BlockSpec((ts, w), lambda j, i: (i, j))
    wspec = pl.BlockSpec((4, w), lambda j, i: (0, j))
    return _pc(body, name, (4, nt),
               [pl.BlockSpec((ts, w), lambda j, i: (i, 2 + j)),
                pl.BlockSpec((8, w), lambda j, i: (jnp.maximum(i * per - 1, 0), 2 + j)),
                tile, pl.BlockSpec((8, w), lambda j, i: (jnp.minimum((i + 1) * per, nblk - 1), j)), wspec],
               [tile, wspec], [_sds((s, SSM_CONV_DIM), BF16), _sds((4, SSM_CONV_DIM))])(proj, proj, g, g, cw)


def _ssd_dt_fwd(proj, bias, a_neg, name):
    s = proj.shape[0]
    n = s // SSM_CHUNK

    def body(r_ref, b_ref, a_ref, dt_ref, ac_ref):
        real = lax.broadcasted_iota(jnp.int32, (SSM_CHUNK, LANES), 1) < SSM_HEADS
        dt = jnp.where(real, _softplus(r_ref[...] + b_ref[...]), 0.0)
        dt_ref[...] = dt
        ac_ref[...] = _dot(_tri(True), dt * a_ref[...], 1, 0, HI)

    vec = pl.BlockSpec((1, LANES), lambda c: (0, 0))
    row = pl.BlockSpec((SSM_CHUNK, LANES), lambda c: (c, 0))
    return _pc(body, name, (n,), [pl.BlockSpec((SSM_CHUNK, LANES), lambda c: (c, 48)), vec, vec], [row, row],
               [_sds((s, LANES)), _sds((s, LANES))])(proj, bias, a_neg)


def _ssd_dt_bwd(proj, bias, ddt, name):
    s = proj.shape[0]
    ts = _row_tile(s, 512)

    def body(r_ref, b_ref, d_ref, o_ref, db_ref):
        real = lax.broadcasted_iota(jnp.int32, (ts, LANES), 1) < SSM_HEADS
        g = jnp.where(real, d_ref[...] * _sigmoid(r_ref[...] + b_ref[...]), 0.0)
        o_ref[...] = g.astype(BF16)

        @pl.when(pl.program_id(0) == 0)
        def _():
            db_ref[...] = jnp.zeros_like(db_ref)

        db_ref[...] += jnp.sum(g, axis=0, keepdims=True)

    vec = pl.BlockSpec((1, LANES), lambda i: (0, 0))
    row = pl.BlockSpec((ts, LANES), lambda i: (i, 0))
    return _pc(body, name, (s // ts,), [pl.BlockSpec((ts, LANES), lambda i: (i, 48)), vec, row], [row, vec],
               [_sds((s, LANES), BF16), _sds((1, LANES))])(proj, bias, ddt)


def _pair_cols(cols, k0, lo):
    return jnp.where(lo, cols[:, k0:k0 + 1], cols[:, k0 + 1:k0 + 2])


def _last_lane(row):
    lane = lax.broadcasted_iota(jnp.int32, row.shape, 1)
    return jnp.sum(jnp.where(lane == SSM_CHUNK - 1, row, 0.0), axis=-1, keepdims=True)


def _ssd_specs(nc, rev):
    cc = (lambda c: nc - 1 - c) if rev else (lambda c: c)
    return dict(
        x=pl.BlockSpec((SSM_CHUNK, 256), lambda g, c: (cc(c), g)),
        b=pl.BlockSpec((SSM_CHUNK, LANES), lambda g, c: (cc(c), 16 + g)),
        c=pl.BlockSpec((SSM_CHUNK, LANES), lambda g, c: (cc(c), 24 + g)),
        col=pl.BlockSpec((1, SSM_CHUNK, 4), lambda g, c: (g, cc(c), 0)),
        row=pl.BlockSpec((1, 4, SSM_CHUNK), lambda g, c: (g, 0, cc(c))),
        grp=pl.BlockSpec((1, 1, 256), lambda g, c: (g, 0, 0)),
        grow=pl.BlockSpec((1, 4, LANES), lambda g, c: (g, 0, 0)),
        hs=pl.BlockSpec((1, 1, 256, SSM_STATE), lambda g, c: (cc(c), g, 0, 0)),
        bc=pl.BlockSpec((SSM_CHUNK, LANES), lambda g, c: (cc(c), g)),
    )


def _ssd_scan_fwd(xbc, dtc, acol, drow, arow, dskip, name):
    s = xbc.shape[0]
    nc = s // SSM_CHUNK
    sp = _ssd_specs(nc, False)
    L = SSM_CHUNK

    def body(x_ref, b_ref, c_ref, dtc_ref, ac_ref, dr_ref, ar_ref, dk_ref, y_ref, hs_ref, h_ref):
        @pl.when(pl.program_id(1) == 0)
        def _():
            h_ref[...] = jnp.zeros_like(h_ref)

        bb = b_ref[...].astype(BF16)
        cb = c_ref[...].astype(BF16)
        gm = _dot(cb, bb, 1, 1)
        dtc = dtc_ref[0]
        ac = ac_ref[0]
        dr = dr_ref[0]
        ar = ar_ref[0]
        dsk = dk_ref[0]
        hs_ref[0, 0] = h_ref[...]
        tril = lax.broadcasted_iota(jnp.int32, (L, L), 0) >= lax.broadcasted_iota(jnp.int32, (L, L), 1)
        lo = _lo_mask((L, LANES))
        rowlo = lax.broadcasted_iota(jnp.int32, (L, LANES), 0) < HEAD_DIM
        for pr in range(2):
            k0 = 2 * pr
            xp = x_ref[:, pr * LANES:(pr + 1) * LANES]
            xpb = xp.astype(BF16)
            hp = h_ref[pr * LANES:(pr + 1) * LANES, :]
            yd, al = [], []
            for k in (k0, k0 + 1):
                seg = ac[:, k:k + 1] - ar[k:k + 1, :]
                wk = gm * jnp.exp(jnp.where(tril, seg, -jnp.inf)) * dr[k:k + 1, :]
                yd.append(_dot(wk.astype(BF16), xpb, 1, 0))
                al.append(_last_lane(ar[k:k + 1, :]))
            e = jnp.exp(_pair_cols(ac, k0, lo))
            yo = _dot(cb, hp.astype(BF16), 1, 1) * e
            y_ref[:, pr * LANES:(pr + 1) * LANES] = jnp.where(lo, yd[0], yd[1]) + yo + dsk[:, pr * LANES:(pr + 1) * LANES] * xp
            wp = jnp.where(lo, jnp.exp(al[0] - ac[:, k0:k0 + 1]) * dtc[:, k0:k0 + 1],
                           jnp.exp(al[1] - ac[:, k0 + 1:k0 + 2]) * dtc[:, k0 + 1:k0 + 2])
            st = _dot((xp * wp).astype(BF16), bb, 0, 0)
            dec = jnp.where(rowlo, jnp.exp(al[0]), jnp.exp(al[1]))
            h_ref[pr * LANES:(pr + 1) * LANES, :] = dec * hp + st

    return _pc(body, name, (SSM_GROUPS, nc),
               [sp["x"], sp["b"], sp["c"], sp["col"], sp["col"], sp["row"], sp["row"], sp["grp"]],
               [sp["x"], sp["hs"]], [_sds((s, SSM_INNER)), _sds((nc, SSM_GROUPS, 256, SSM_STATE))],
               [pltpu.VMEM((256, SSM_STATE), F32)])(xbc, xbc, xbc, dtc, acol, drow, arow, dskip)


def _ssd_scan_bwd(xbc, dtc, acol, drow, arow, dskip, agrp, hs, dy, name):
    s = xbc.shape[0]
    nc = s // SSM_CHUNK
    sp = _ssd_specs(nc, True)
    L = SSM_CHUNK

    def body(x_ref, b_ref, c_ref, dtc_ref, ac_ref, dr_ref, ar_ref, dk_ref, ag_ref, hs_ref, dy_ref,
             dx_ref, db_ref, dc_ref, ddt_ref, da_ref, dd_ref, dh_ref):
        @pl.when(pl.program_id(1) == 0)
        def _():
            dh_ref[...] = jnp.zeros_like(dh_ref)
            da_ref[...] = jnp.zeros_like(da_ref)
            dd_ref[...] = jnp.zeros_like(dd_ref)

        bb = b_ref[...].astype(BF16)
        cb = c_ref[...].astype(BF16)
        gm = _dot(cb, bb, 1, 1)
        dtc = dtc_ref[0]
        ac = ac_ref[0]
        dr = dr_ref[0]
        ar = ar_ref[0]
        dsk = dk_ref[0]
        ag = ag_ref[0]
        tril = lax.broadcasted_iota(jnp.int32, (L, L), 0) >= lax.broadcasted_iota(jnp.int32, (L, L), 1)
        lo = _lo_mask((L, LANES))
        nlo = jnp.logical_not(lo)
        rowlo = lax.broadcasted_iota(jnp.int32, (L, LANES), 0) < HEAD_DIM
        lane = lax.broadcasted_iota(jnp.int32, (L, LANES), 1)
        lane_row = lax.broadcasted_iota(jnp.int32, (1, LANES), 1)
        dgm = jnp.zeros((L, L), F32)
        dcm = jnp.zeros((L, SSM_STATE), F32)
        dbm = jnp.zeros((L, SSM_STATE), F32)
        cols = jnp.zeros((L, LANES), F32)
        rows_ddt, rows_q, al_all, dcd_all = [], [], [], []
        for pr in range(2):
            k0 = 2 * pr
            xp = x_ref[:, pr * LANES:(pr + 1) * LANES]
            xpb = xp.astype(BF16)
            dyp = dy_ref[:, pr * LANES:(pr + 1) * LANES]
            dypb = dyp.astype(BF16)
            zero = jnp.zeros_like(dypb)
            hp = hs_ref[0, 0, pr * LANES:(pr + 1) * LANES, :]
            hpb = hp.astype(BF16)
            dst = dh_ref[pr * LANES:(pr + 1) * LANES, :]
            dstb = dst.astype(BF16)
            dxd, al = [], []
            for k in (k0, k0 + 1):
                sel = lo if k == k0 else nlo
                seg = ac[:, k:k + 1] - ar[k:k + 1, :]
                lam = jnp.exp(jnp.where(tril, seg, -jnp.inf))
                wk = gm * lam * dr[k:k + 1, :]
                dwk = _dot(jnp.where(sel, dypb, zero), xpb, 1, 1)
                mk = dwk * gm * lam
                qk = mk * dr[k:k + 1, :]
                dgm = dgm + dwk * lam * dr[k:k + 1, :]
                rows_ddt.append(jnp.sum(mk, axis=0, keepdims=True))
                rows_q.append(jnp.sum(qk, axis=0, keepdims=True))
                cols = jnp.where(lane == k, jnp.sum(qk, axis=-1, keepdims=True), cols)
                dxd.append(_dot(wk.astype(BF16), dypb, 0, 0))
                al.append(_last_lane(ar[k:k + 1, :]))
            al_all += al
            dxp = jnp.where(lo, dxd[0], dxd[1])
            e = jnp.exp(_pair_cols(ac, k0, lo))
            dye = dyp * e
            dyeb = dye.astype(BF16)
            dcm = dcm + _dot(dyeb, hpb, 1, 0)
            dh_yoff = _dot(dyeb, cb, 0, 0)
            tq = dye * _dot(cb, hpb, 1, 1)
            cols = jnp.where(lane == 4 + k0, jnp.sum(jnp.where(lo, tq, 0.0), axis=-1, keepdims=True), cols)
            cols = jnp.where(lane == 5 + k0, jnp.sum(jnp.where(lo, 0.0, tq), axis=-1, keepdims=True), cols)
            wp = jnp.where(lo, jnp.exp(al[0] - ac[:, k0:k0 + 1]) * dtc[:, k0:k0 + 1],
                           jnp.exp(al[1] - ac[:, k0 + 1:k0 + 2]) * dtc[:, k0 + 1:k0 + 2])
            dxw = _dot(bb, dstb, 1, 1)
            dxp = dxp + dxw * wp
            tw = xp * dxw
            cols = jnp.where(lane == 8 + k0, jnp.sum(jnp.where(lo, tw, 0.0), axis=-1, keepdims=True), cols)
            cols = jnp.where(lane == 9 + k0, jnp.sum(jnp.where(lo, 0.0, tw), axis=-1, keepdims=True), cols)
            dbm = dbm + _dot((xp * wp).astype(BF16), dstb, 1, 0)
            dsl = dsk[:, pr * LANES:(pr + 1) * LANES]
            dx_ref[:, pr * LANES:(pr + 1) * LANES] = dxp + dsl * dyp
            dd_ref[0, :, pr * LANES:(pr + 1) * LANES] += jnp.sum(dyp * xp, axis=0, keepdims=True)
            prod = dst * hp
            dcd_all.append(jnp.sum(jnp.sum(jnp.where(rowlo, prod, 0.0), axis=-1, keepdims=True), axis=0, keepdims=True))
            dcd_all.append(jnp.sum(jnp.sum(jnp.where(rowlo, 0.0, prod), axis=-1, keepdims=True), axis=0, keepdims=True))
            dec = jnp.where(rowlo, jnp.exp(al[0]), jnp.exp(al[1]))
            dh_ref[pr * LANES:(pr + 1) * LANES, :] = dec * dst + dh_yoff
        dgb = dgm.astype(BF16)
        dc_ref[...] = dcm + _dot(dgb, bb, 1, 0)
        db_ref[...] = dbm + _dot(dgb, cb, 0, 0)
        colt = cols.T
        sub8 = lax.broadcasted_iota(jnp.int32, (8, LANES), 0)
        da_rows = jnp.zeros((8, LANES), F32)
        ddt_part = []
        for k in range(4):
            rs = colt[k:k + 1, :]
            uo = colt[4 + k:5 + k, :]
            dwl = colt[8 + k:9 + k, :]
            es = jnp.exp(al_all[k] - ar[k:k + 1, :])
            wrow = es * dr[k:k + 1, :]
            dwl_w = dwl * wrow
            da_k = rs - rows_q[k] + uo - dwl_w
            tail = jnp.sum(dwl_w, axis=-1, keepdims=True) + jnp.exp(al_all[k]) * dcd_all[k]
            da_k = da_k + jnp.where(lane_row == L - 1, tail, 0.0)
            da_rows = jnp.where(sub8 == k, da_k, da_rows)
            ddt_part.append(rows_ddt[k] + dwl * es)
        dda = _dot(da_rows, _tri(True), 1, 0, HI)
        for k in range(4):
            dda_k = dda[k:k + 1, :]
            ddt_ref[0, k:k + 1, :] = ddt_part[k] + dda_k * ag[k:k + 1, :]
            da_ref[0, k:k + 1, :] += dda_k * dr[k:k + 1, :] * ag[k:k + 1, :]

    return _pc(body, name, (SSM_GROUPS, nc),
               [sp["x"], sp["b"], sp["c"], sp["col"], sp["col"], sp["row"], sp["row"], sp["grp"], sp["grow"], sp["hs"], sp["x"]],
               [sp["x"], sp["bc"], sp["bc"], sp["row"], sp["grow"], sp["grp"]],
               [_sds((s, SSM_INNER)), _sds((s, 1024)), _sds((s, 1024)), _sds((SSM_GROUPS, 4, s)),
                _sds((SSM_GROUPS, 4, LANES)), _sds((SSM_GROUPS, 1, 256))],
               [pltpu.VMEM((256, SSM_STATE), F32)])(xbc, xbc, xbc, dtc, acol, drow, arow, dskip, agrp, hs, dy)


def _gnorm_fwd(y, proj, nw, name):
    s = y.shape[0]
    ts = _row_tile(s)
    gw = SSM_INNER // SSM_GROUPS

    def body(y_ref, z_ref, w_ref, o_ref):
        for g in range(SSM_GROUPS):
            sl = slice(g * gw, (g + 1) * gw)
            z = z_ref[:, sl]
            y2 = y_ref[:, sl] * (z * _sigmoid(z))
            r = lax.rsqrt(jnp.mean(y2 * y2, axis=-1, keepdims=True) + RMS_EPS)
            o_ref[:, sl] = ((y2 * r) * w_ref[:, sl]).astype(BF16)

    row = pl.BlockSpec((ts, SSM_INNER), lambda i: (i, 0))
    return _pc(body, name, (s // ts,), [row, row, pl.BlockSpec((1, SSM_INNER), lambda i: (0, 0))], row,
               _sds((s, SSM_INNER), BF16))(y, proj, nw)


def _gnorm_bwd(y, proj, nw, dyn, name):
    s = y.shape[0]
    ts = _row_tile(s)
    gw = SSM_INNER // SSM_GROUPS

    def body(y_ref, z_ref, w_ref, d_ref, dy_ref, dz_ref, dw_ref):
        @pl.when(pl.program_id(0) == 0)
        def _():
            dw_ref[...] = jnp.zeros_like(dw_ref)

        for g in range(SSM_GROUPS):
            sl = slice(g * gw, (g + 1) * gw)
            z = z_ref[:, sl]
            yv = y_ref[:, sl]
            sg = _sigmoid(z)
            sz = z * sg
            y2 = yv * sz
            r = lax.rsqrt(jnp.mean(y2 * y2, axis=-1, keepdims=True) + RMS_EPS)
            yn = y2 * r
            dout = d_ref[:, sl]
            dyg = dout * w_ref[:, sl]
            dy2 = r * (dyg - yn * jnp.mean(dyg * yn, axis=-1, keepdims=True))
            dy_ref[:, sl] = dy2 * sz
            dz_ref[:, sl] = (dy2 * yv * (sg * (1.0 + z * (1.0 - sg)))).astype(BF16)
            dw_ref[:, sl] += jnp.sum(dout * yn, axis=0, keepdims=True)

    row = pl.BlockSpec((ts, SSM_INNER), lambda i: (i, 0))
    vec = pl.BlockSpec((1, SSM_INNER), lambda i: (0, 0))
    return _pc(body, name, (s // ts,), [row, row, vec, row], [row, row, vec],
               [_sds((s, SSM_INNER)), _sds((s, SSM_INNER), BF16), _sds((1, SSM_INNER))])(y, proj, nw, dyn)


def _head_layouts(v, s):
    return v.reshape(s, SSM_GROUPS, 4).transpose(1, 0, 2), v.T.reshape(SSM_GROUPS, 4, s)


def _ssd_fwd(x, norm_w, w_in, conv_w, conv_b, dt_bias, a_log, d_skip, gnorm_w, w_out):
    s = x.shape[0]
    h = _rms_fwd(x, norm_w, "ssd_norm")
    proj = _mm(h, w_in, "nn", "ssd_in")
    xbc = _ssd_conv_fwd(proj, conv_w, conv_b, "ssd_conv")
    pad = ((0, 0), (0, LANES - SSM_HEADS))
    a_neg = -jnp.exp(a_log)
    bias = jnp.pad(dt_bias, pad)
    dt, acum = _ssd_dt_fwd(proj, bias, jnp.pad(a_neg, pad), "ssd_dt")
    dtc, drow = _head_layouts(dt[:, :SSM_HEADS], s)
    acol, arow = _head_layouts(acum[:, :SSM_HEADS], s)
    dskip = jnp.repeat(d_skip.reshape(SSM_GROUPS, 1, 4), HEAD_DIM, axis=2)
    y, hs = _ssd_scan_fwd(xbc, dtc, acol, drow, arow, dskip, "ssd_scan")
    yn = _gnorm_fwd(y, proj, gnorm_w, "ssd_gnorm")
    out = _mm(yn, w_out, "nn", "ssd_out", res=x)
    return out, (x, h, proj, xbc, bias, a_neg, dtc, acol, drow, arow, dskip, y, hs, yn)


def _ssd_bwd(dout, saved, norm_w, w_in, conv_w, conv_b, gnorm_w, w_out):
    x, h, proj, xbc, bias, a_neg, dtc, acol, drow, arow, dskip, y, hs, yn = saved
    s = x.shape[0]
    dyn = _mm(dout, w_out, "nt", "ssd_dyn")
    g_out = _mm(yn, dout, "tn", "ssd_gout", out_dtype=BF16)
    dy, dz, g_gnorm = _gnorm_bwd(y, proj, gnorm_w, dyn, "ssd_dgnorm")
    agrp = jnp.broadcast_to(a_neg.reshape(SSM_GROUPS, 4, 1), (SSM_GROUPS, 4, LANES))
    dxs, db, dc, ddt_row, da_acc, dd_acc = _ssd_scan_bwd(xbc, dtc, acol, drow, arow, dskip, agrp, hs, dy, "ssd_dscan")
    dxbc = jnp.concatenate([dxs, db, dc], axis=1)
    gact, g_cb = _ssd_conv_bwd_act(proj, dxbc, conv_w, conv_b, "ssd_dconv_act")
    dpre, g_cw = _ssd_conv_bwd_in(proj, gact, conv_w, "ssd_dconv_in")
    ddt = jnp.pad(ddt_row.reshape(SSM_HEADS, s).T, ((0, 0), (0, LANES - SSM_HEADS)))
    ddtraw, g_dtb = _ssd_dt_bwd(proj, bias, ddt, "ssd_ddt")
    dproj = jnp.concatenate([dz, dpre, ddtraw], axis=1)
    g_in = _mm(h, dproj, "tn", "ssd_gin", out_dtype=BF16)
    dh = _mm(dproj, w_in, "nt", "ssd_dh")
    dx, g_norm = _rms_bwd(x, norm_w, dh, dout, "ssd_dnorm")
    g_alog = jnp.sum(da_acc, axis=-1).reshape(1, SSM_HEADS)
    g_d = jnp.sum(dd_acc.reshape(SSM_GROUPS, 4, HEAD_DIM), axis=-1).reshape(1, SSM_HEADS)
    return dx, g_norm, g_in[:, :SSM_IN], g_cw, g_cb, g_dtb[:, :SSM_HEADS], g_alog, g_d, g_gnorm, g_out


def _loss_head(y, target, name):
    s, d = y.shape
    ts = _row_tile(s)

    def body(y_ref, t_ref, dy_ref, l_ref):
        @pl.when(pl.program_id(0) == 0)
        def _():
            l_ref[...] = jnp.zeros_like(l_ref)

        e = y_ref[...] - t_ref[...]
        dy_ref[...] = e * (1.0 / d)
        part = jnp.sum(jnp.sum(e * e, axis=-1, keepdims=True), axis=0, keepdims=True) * (0.5 / d)
        l_ref[...] += jnp.broadcast_to(part, l_ref.shape)

    row = pl.BlockSpec((ts, d), lambda i: (i, 0))
    dy, lacc = _pc(body, name, (s // ts,), [row, row], [row, pl.BlockSpec((8, LANES), lambda i: (0, 0))],
                   [_sds((s, d)), _sds((8, LANES))])(y, target)
    return lacc[0, 0], dy


def _local_step(x, target, w):
    saved = []
    for i in range(DEPTH):
        kind, j = i % 3, i // 3
        mn = w["mix_norm"][i:i + 1]
        if kind == 0:
            x, sv = _conv_fwd(x, mn, w["conv_w_in"][j], w["conv_w_dw"][j], w["conv_w_out"][j], str(i))
        elif kind == 1:
            x, sv = _fox_fwd(x, mn, w["fox_w_in"], w["fox_b_f"], w["fox_q_gain"], w["fox_k_gain"], w["fox_w_out"])
        else:
            x, sv = _ssd_fwd(x, mn, w["ssd_w_in"], w["ssd_conv_w"], w["ssd_conv_b"], w["ssd_dt_bias"],
                             w["ssd_a_log"], w["ssd_d"], w["ssd_norm_w"], w["ssd_w_out"])
        x, sf = _ffn_fwd(x, w["ffn_norm"][i:i + 1], w["ffn_w_gu"][i], w["ffn_w_down"][i], str(i))
        saved.append((sv, sf))
    loss, dx = _loss_head(x, target, "loss_head")
    g = {k: [None] * n for k, n in (("mix_norm", DEPTH), ("ffn_norm", DEPTH), ("ffn_w_gu", DEPTH), ("ffn_w_down", DEPTH),
                                    ("conv_w_in", 2), ("conv_w_dw", 2), ("conv_w_out", 2))}
    for i in reversed(range(DEPTH)):
        kind, j = i % 3, i // 3
        sv, sf = saved[i]
        dx, g["ffn_norm"][i], g["ffn_w_gu"][i], g["ffn_w_down"][i] = _ffn_bwd(
            dx, sf, w["ffn_norm"][i:i + 1], w["ffn_w_gu"][i], w["ffn_w_down"][i], str(i))
        mn = w["mix_norm"][i:i + 1]
        if kind == 0:
            dx, g["mix_norm"][i], g["conv_w_in"][j], g["conv_w_dw"][j], g["conv_w_out"][j] = _conv_bwd(
                dx, sv, mn, w["conv_w_in"][j], w["conv_w_dw"][j], w["conv_w_out"][j], str(i))
        elif kind == 1:
            (dx, g["mix_norm"][i], g["fox_w_in"], g["fox_b_f"], g["fox_q_gain"], g["fox_k_gain"],
             g["fox_w_out"]) = _fox_bwd(dx, sv, mn, w["fox_w_in"], w["fox_w_out"])
        else:
            (dx, g["mix_norm"][i], g["ssd_w_in"], g["ssd_conv_w"], g["ssd_conv_b"], g["ssd_dt_bias"], g["ssd_a_log"],
             g["ssd_d"], g["ssd_norm_w"], g["ssd_w_out"]) = _ssd_bwd(
                 dx, sv, mn, w["ssd_w_in"], w["ssd_conv_w"], w["ssd_conv_b"], w["ssd_norm_w"], w["ssd_w_out"])
    g["mix_norm"] = jnp.concatenate(g["mix_norm"], axis=0)
    g["ffn_norm"] = jnp.concatenate(g["ffn_norm"], axis=0)
    g["conv_w_dw"] = jnp.stack(g["conv_w_dw"], axis=0)
    g["ssd_conv_w"] = g["ssd_conv_w"][None]
    return loss, dx, g


def _mesh_position():
    return lax.axis_index("x") * 4 + lax.axis_index("y") * 2 + lax.axis_index("c")


def _device_of(t):
    return (lax.shift_right_logical(t, 2), lax.bitwise_and(lax.shift_right_logical(t, 1), 1), lax.bitwise_and(t, 1))


def _exchange_run(srcs_of, out_refs, send_sems, recv_sems, local_sems):
    me = _mesh_position()
    na = len(out_refs)
    locals_ = [pltpu.make_async_copy(srcs_of(a, me), out_refs[a].at[me], local_sems.at[a]) for a in range(na)]
    for cp in locals_:
        cp.start()
    copies = []
    for j in range(1, NDEV):
        t = lax.rem(me + j, NDEV)
        for a in range(na):
            copies.append(pltpu.make_async_remote_copy(
                src_ref=srcs_of(a, t), dst_ref=out_refs[a].at[me], send_sem=send_sems.at[a, j - 1],
                recv_sem=recv_sems.at[a, j - 1], device_id=_device_of(t), device_id_type=pl.DeviceIdType.MESH))
    for cp in copies:
        cp.start()
    for cp in copies:
        cp.wait_send()
    for j in range(1, NDEV):
        frm = lax.rem(me + NDEV - j, NDEV)
        for a in range(na):
            pltpu.make_async_remote_copy(
                src_ref=srcs_of(a, me), dst_ref=out_refs[a].at[frm], send_sem=send_sems.at[a, j - 1],
                recv_sem=recv_sems.at[a, j - 1], device_id=_device_of(frm), device_id_type=pl.DeviceIdType.MESH).wait_recv()
    for cp in locals_:
        cp.wait()


def _exchange(arrays, name, gather):
    na = len(arrays)

    def body(*refs):
        srcs, outs = refs[:na], refs[na:2 * na]
        send_sems, recv_sems, local_sems = refs[2 * na:]
        _exchange_run((lambda a, t: srcs[a]) if gather else (lambda a, t: srcs[a].at[t]), outs, send_sems, recv_sems, local_sems)

    hbm = pl.BlockSpec(memory_space=pl.ANY)
    outs = [_sds(((NDEV,) + a.shape) if gather else a.shape, a.dtype) for a in arrays]
    return pl.pallas_call(
        body, name=name, in_specs=[hbm] * na, out_specs=[hbm] * na, out_shape=outs,
        scratch_shapes=[pltpu.SemaphoreType.DMA((na, NDEV - 1)), pltpu.SemaphoreType.DMA((na, NDEV - 1)),
                        pltpu.SemaphoreType.DMA((na,))])(*arrays)


def _all_sum_small(pack, name):
    def body(src_ref, out_ref, buf_ref, send_sems, recv_sems, local_sems):
        _exchange_run(lambda a, t: src_ref, [buf_ref], send_sems, recv_sems, local_sems)
        acc = buf_ref[0]
        for d in range(1, NDEV):
            acc = acc + buf_ref[d]
        out_ref[...] = acc

    vmem = pl.BlockSpec(memory_space=pltpu.VMEM)
    return pl.pallas_call(
        body, name=name, in_specs=[vmem], out_specs=vmem, out_shape=_sds(pack.shape, pack.dtype),
        scratch_shapes=[pltpu.VMEM((NDEV,) + pack.shape, pack.dtype), pltpu.SemaphoreType.DMA((1, NDEV - 1)),
                        pltpu.SemaphoreType.DMA((1, NDEV - 1)), pltpu.SemaphoreType.DMA((1,))])(pack)


def _sum_slabs(slabs, name):
    _, r, c = slabs.shape
    tr = r
    for cand in (256, 352):
        if r % cand == 0:
            tr = cand
            break

    def body(s_ref, o_ref):
        acc = s_ref[0].astype(F32)
        for d in range(1, NDEV):
            acc = acc + s_ref[d].astype(F32)
        o_ref[...] = acc

    return _pc(body, name, (r // tr,), [pl.BlockSpec((NDEV, tr, c), lambda i: (0, i, 0))],
               pl.BlockSpec((tr, c), lambda i: (i, 0)), _sds((r, c)))(slabs)


def _adamw(wt, g, m, v, name):
    shape = wt.shape
    w2, g2, m2, v2 = (a.reshape(-1, shape[-1]) for a in (wt, g, m, v))
    r, c = w2.shape
    tr = r
    for cand in (512, 352, 256):
        if r % cand == 0:
            tr = cand
            break
    c1 = 1.0 - ADAM_B1 ** ADAM_STEP
    c2 = 1.0 - ADAM_B2 ** ADAM_STEP

    def body(w_ref, g_ref, m_ref, v_ref, d_ref, mo_ref, vo_ref):
        gv = g_ref[...]
        mn = ADAM_B1 * m_ref[...] + (1.0 - ADAM_B1) * gv
        vn = ADAM_B2 * v_ref[...] + (1.0 - ADAM_B2) * (gv * gv)
        mo_ref[...] = mn
        vo_ref[...] = vn
        d_ref[...] = -ADAM_LR * ((mn / c1) / (jnp.sqrt(vn / c2) + ADAM_EPS) + ADAM_WD * w_ref[...])

    spec = pl.BlockSpec((tr, c), lambda i: (i, 0))
    outs = _pc(body, name, (r // tr,), [spec] * 4, [spec] * 3, [_sds((r, c))] * 3)(w2, g2, m2, v2)
    return tuple(o.reshape(shape) for o in outs)


_NAMES = ["mix_norm", "ffn_norm", "ffn_w_gu", "ffn_w_down", "conv_w_in", "conv_w_dw", "conv_w_out", "fox_w_in", "fox_b_f",
          "fox_q_gain", "fox_k_gain", "fox_w_out", "ssd_w_in", "ssd_conv_w", "ssd_conv_b", "ssd_dt_bias", "ssd_a_log",
          "ssd_d", "ssd_norm_w", "ssd_w_out"]
_MATRICES = ["ffn_w_gu", "ffn_w_down", "conv_w_in", "conv_w_out", "fox_w_in", "fox_w_out", "ssd_w_in", "ssd_w_out"]
_VECTORS = {"conv_w_dw": 2, "ssd_conv_w": 2, "ssd_conv_b": 1, "ssd_norm_w": 1}
_REPLICATED = ["mix_norm", "ffn_norm", "fox_b_f", "fox_q_gain", "fox_k_gain", "ssd_dt_bias", "ssd_a_log", "ssd_d"]


def _to_rows(flat):
    n = flat.shape[0]
    rows = -(-n // (8 * D_MODEL)) * 8
    return jnp.pad(flat, (0, rows * D_MODEL - n)).reshape(rows, D_MODEL)


def _full_shape(local_shape, axis):
    shp = list(local_shape)
    shp[axis] *= NDEV
    return tuple(shp)


def _cols_from_blocks(g):
    return jnp.moveaxis(g, 0, 1).reshape(g.shape[1], NDEV * g.shape[2])


def _blocks_from_cols(full):
    k, n8 = full.shape
    return jnp.moveaxis(full.reshape(k, NDEV, n8 // NDEV), 1, 0)


def kernel(x, mix_norm, ffn_norm, ffn_w_gu, ffn_w_down, conv_w_in, conv_w_dw, conv_w_out, fox_w_in, fox_b_f, fox_q_gain, fox_k_gain, fox_w_out, ssd_w_in, ssd_conv_w, ssd_conv_b, ssd_dt_bias, ssd_a_log, ssd_d, ssd_norm_w, ssd_w_out, loss_target, m_mix_norm, m_ffn_norm, m_ffn_w_gu, m_ffn_w_down, m_conv_w_in, m_conv_w_dw, m_conv_w_out, m_fox_w_in, m_fox_b_f, m_fox_q_gain, m_fox_k_gain, m_fox_w_out, m_ssd_w_in, m_ssd_conv_w, m_ssd_conv_b, m_ssd_dt_bias, m_ssd_a_log, m_ssd_d, m_ssd_norm_w, m_ssd_w_out, v_mix_norm, v_ffn_norm, v_ffn_w_gu, v_ffn_w_down, v_conv_w_in, v_conv_w_dw, v_conv_w_out, v_fox_w_in, v_fox_b_f, v_fox_q_gain, v_fox_k_gain, v_fox_w_out, v_ssd_w_in, v_ssd_conv_w, v_ssd_conv_b, v_ssd_dt_bias, v_ssd_a_log, v_ssd_d, v_ssd_norm_w, v_ssd_w_out):
    local = dict(mix_norm=mix_norm, ffn_norm=ffn_norm, ffn_w_gu=ffn_w_gu, ffn_w_down=ffn_w_down, conv_w_in=conv_w_in,
                 conv_w_dw=conv_w_dw, conv_w_out=conv_w_out, fox_w_in=fox_w_in, fox_b_f=fox_b_f, fox_q_gain=fox_q_gain,
                 fox_k_gain=fox_k_gain, fox_w_out=fox_w_out, ssd_w_in=ssd_w_in, ssd_conv_w=ssd_conv_w, ssd_conv_b=ssd_conv_b,
                 ssd_dt_bias=ssd_dt_bias, ssd_a_log=ssd_a_log, ssd_d=ssd_d, ssd_norm_w=ssd_norm_w, ssd_w_out=ssd_w_out)
    mom = dict(zip(_NAMES, [m_mix_norm, m_ffn_norm, m_ffn_w_gu, m_ffn_w_down, m_conv_w_in, m_conv_w_dw, m_conv_w_out, m_fox_w_in,
                            m_fox_b_f, m_fox_q_gain, m_fox_k_gain, m_fox_w_out, m_ssd_w_in, m_ssd_conv_w, m_ssd_conv_b,
                            m_ssd_dt_bias, m_ssd_a_log, m_ssd_d, m_ssd_norm_w, m_ssd_w_out]))
    var = dict(zip(_NAMES, [v_mix_norm, v_ffn_norm, v_ffn_w_gu, v_ffn_w_down, v_conv_w_in, v_conv_w_dw, v_conv_w_out, v_fox_w_in,
                            v_fox_b_f, v_fox_q_gain, v_fox_k_gain, v_fox_w_out, v_ssd_w_in, v_ssd_conv_w, v_ssd_conv_b,
                            v_ssd_dt_bias, v_ssd_a_log, v_ssd_d, v_ssd_norm_w, v_ssd_w_out]))

    vec_pack = _to_rows(jnp.concatenate([local[k].reshape(-1) for k in _VECTORS]))
    gathered = _exchange([local[k].astype(BF16) for k in _MATRICES] + [vec_pack], "gather_weights", True)
    gm = dict(zip(_MATRICES, gathered[:-1]))
    gvec = gathered[-1].reshape(NDEV, -1)
    full = {k: local[k] for k in _REPLICATED}
    off = 0
    for k, axis in _VECTORS.items():
        n = local[k].size
        blk = jnp.moveaxis(gvec[:, off:off + n].reshape((NDEV,) + local[k].shape), 0, axis)
        full[k] = blk.reshape(_full_shape(local[k].shape, axis))
        off += n
    full["ffn_w_gu"] = [gm["ffn_w_gu"][:, i] for i in range(DEPTH)]
    full["ffn_w_down"] = [gm["ffn_w_down"][:, i].reshape(4, FF_BLOCK, D_MODEL) for i in range(DEPTH)]
    full["conv_w_in"] = [gm["conv_w_in"][:, j] for j in range(2)]
    full["conv_w_out"] = [gm["conv_w_out"][:, j].reshape(D_MODEL, D_MODEL) for j in range(2)]
    full["fox_w_in"] = jnp.pad(_cols_from_blocks(gm["fox_w_in"][:, 0]), ((0, 0), (0, FOX_IN_PAD - FOX_IN)))
    full["fox_w_out"] = gm["fox_w_out"].reshape(D_MODEL, D_MODEL)
    full["ssd_w_in"] = jnp.pad(_cols_from_blocks(gm["ssd_w_in"][:, 0]), ((0, 0), (0, SSM_IN_PAD - SSM_IN)))
    full["ssd_w_out"] = gm["ssd_w_out"].reshape(SSM_INNER, D_MODEL)
    full["ssd_conv_w"] = full["ssd_conv_w"][0]

    loss_part, dx, grads = _local_step(x[0], loss_target[0], full)

    send = ([grads["ffn_w_gu"][i] for i in range(DEPTH)]
            + [grads["ffn_w_down"][i].reshape(NDEV, D_FF // NDEV, D_MODEL) for i in range(DEPTH)]
            + [grads["conv_w_in"][j] for j in range(2)]
            + [grads["conv_w_out"][j].reshape(NDEV, D_MODEL // NDEV, D_MODEL) for j in range(2)]
            + [_blocks_from_cols(grads["fox_w_in"]), grads["fox_w_out"].reshape(NDEV, D_MODEL // NDEV, D_MODEL),
               _blocks_from_cols(grads["ssd_w_in"]), grads["ssd_w_out"].reshape(NDEV, SSM_INNER // NDEV, D_MODEL)])
    recv = _exchange(send, "scatter_grads", False)
    sums = [_sum_slabs(r, f"sum_grads_{n}") for n, r in enumerate(recv)]
    shard_grad = {
        "ffn_w_gu": jnp.stack(sums[0:4]), "ffn_w_down": jnp.stack(sums[4:8]), "conv_w_in": jnp.stack(sums[8:10]),
        "conv_w_out": jnp.stack(sums[10:12]), "fox_w_in": sums[12][None], "fox_w_out": sums[13][None],
        "ssd_w_in": sums[14][None], "ssd_w_out": sums[15][None]}

    small_names = _REPLICATED + list(_VECTORS)
    small = [jnp.reshape(loss_part, (1,))] + [grads[k].reshape(-1) for k in small_names]
    total = _all_sum_small(_to_rows(jnp.concatenate(small)), "sum_small").reshape(-1)
    loss = total[0]
    off = 1
    me = _mesh_position()
    for k, part in zip(small_names, small[1:]):
        gk = total[off:off + part.shape[0]]
        off += part.shape[0]
        if k in _VECTORS:
            axis = _VECTORS[k]
            shp = local[k].shape
            gfull = gk.reshape(shp[:axis] + (NDEV, shp[axis]) + shp[axis + 1:])
            shard_grad[k] = lax.dynamic_index_in_dim(gfull, me, axis, keepdims=False)
        else:
            shard_grad[k] = gk.reshape(local[k].shape)

    deltas, new_m, new_v = {}, {}, {}
    for k in _NAMES:
        deltas[k], new_m[k], new_v[k] = _adamw(local[k], shard_grad[k], mom[k], var[k], f"adamw_{k}")
    return (loss, dx[None], *[shard_grad[k] for k in _NAMES], *[deltas[k] for k in _NAMES],
            *[new_m[k] for k in _NAMES], *[new_v[k] for k in _NAMES])
```

```python
import numpy as np

import jax
import jax.numpy as jnp
from jax import lax
from jax.experimental import pallas as pl
from jax.experimental.pallas import tpu as pltpu

F32 = jnp.float32
BF16 = jnp.bfloat16
HI = lax.Precision.HIGHEST

NDEV = 8
D_MODEL = 1024
DEPTH = 4
D_FF = 2816
FF_BLOCK = 2 * D_FF // NDEV
CONV_BLOCK = 3 * D_MODEL // NDEV
RMS_EPS = 1e-6
HEAD_DIM = 64
ATTN_HEADS = 16
FOX_IN = 3 * D_MODEL + ATTN_HEADS
FOX_IN_PAD = 3200
SSM_INNER = 2048
SSM_HEADS = 32
SSM_GROUPS = 8
SSM_STATE = 128
SSM_CHUNK = 128
SSM_CONV_DIM = 4096
SSM_IN = SSM_INNER + SSM_CONV_DIM + SSM_HEADS
SSM_IN_PAD = 6272
LANES = 128
V7X_VMEM_BYTES = 64 * 1024 * 1024
VMEM_LIMIT_BYTES = (V7X_VMEM_BYTES * 3) // 4
LOG2E = 1.4426950408889634
LN2 = 0.6931471805599453
ATTN_ROWS = 64

ADAM_LR = 0.001
ADAM_B1 = 0.9
ADAM_B2 = 0.999
ADAM_EPS = 1e-08
ADAM_WD = 0.01
ADAM_STEP = 10

_TILE_CANDIDATES = (1024, 1408, 896, 768, 640, 512, 384, 256, 128)


def _pick_tile(n):
    for c in _TILE_CANDIDATES:
        if n % c == 0:
            return c
    raise ValueError(f"no tile for {n}")


def _params(ngrid):
    return pltpu.CompilerParams(dimension_semantics=("arbitrary",) * ngrid, vmem_limit_bytes=VMEM_LIMIT_BYTES)


def _pc(body, name, grid, in_specs, out_specs, out_shape, scratch=()):
    return pl.pallas_call(
        body, name=name, grid=grid, in_specs=in_specs, out_specs=out_specs, out_shape=out_shape,
        scratch_shapes=list(scratch), compiler_params=_params(len(grid)))


def _dot(a, b, ca, cb, prec=None):
    return lax.dot_general(a, b, (((ca,), (cb,)), ((), ())), preferred_element_type=F32, precision=prec)


def _sds(shape, dtype=F32):
    return jax.ShapeDtypeStruct(shape, dtype)


def _row_tile(s, want=256):
    return want if s % want == 0 else s


def _sigmoid(x):
    return 1.0 / (1.0 + jnp.exp(-x))


def _softplus(x):
    return jnp.maximum(x, 0.0) + jnp.log(1.0 + jnp.exp(-jnp.abs(x)))


def _mm_spec(a, b, name, grid, a_spec, b_spec, o_spec, out, ca, cb, acc_shape, drop=(0, 0, 0), res=None, r_spec=None):
    nk = grid[2]
    da, db, do_ = drop
    has_res = res is not None

    def body(*refs):
        if has_res:
            a_ref, b_ref, r_ref, o_ref, acc_ref = refs
        else:
            a_ref, b_ref, o_ref, acc_ref = refs
        k = pl.program_id(2)

        @pl.when(k == 0)
        def _():
            acc_ref[...] = jnp.zeros_like(acc_ref)

        av = a_ref[(0,) * da] if da else a_ref[...]
        bv = b_ref[(0,) * db] if db else b_ref[...]
        acc_ref[...] += _dot(av.astype(BF16), bv.astype(BF16), ca, cb)

        @pl.when(k == nk - 1)
        def _():
            val = acc_ref[...]
            if has_res:
                val = val + r_ref[...]
            if do_:
                o_ref[(0,) * do_] = val.astype(out.dtype)
            else:
                o_ref[...] = val.astype(out.dtype)

    in_specs = [a_spec, b_spec] + ([r_spec] if has_res else [])
    args = (a, b) + ((res,) if has_res else ())
    return _pc(body, name, grid, in_specs, o_spec, out, [pltpu.VMEM(acc_shape, F32)])(*args)


def _mm(a, b, mode, name, out_dtype=F32, res=None):
    if mode == "tn":
        r, m = a.shape
        n = b.shape[1]
        tm, tn, tk = _pick_tile(m), _pick_tile(n), _pick_tile(r)
        grid = (m // tm, n // tn, r // tk)
        a_spec = pl.BlockSpec((tk, tm), lambda i, j, k: (k, i))
        b_spec = pl.BlockSpec((tk, tn), lambda i, j, k: (k, j))
        ca, cb = 0, 0
    else:
        m, kd = a.shape
        n = b.shape[1] if mode == "nn" else b.shape[0]
        tm, tn, tk = _pick_tile(m), _pick_tile(n), _pick_tile(kd)
        grid = (m // tm, n // tn, kd // tk)
        a_spec = pl.BlockSpec((tm, tk), lambda i, j, k: (i, k))
        if mode == "nn":
            b_spec = pl.BlockSpec((tk, tn), lambda i, j, k: (k, j))
            ca, cb = 1, 0
        else:
            b_spec = pl.BlockSpec((tn, tk), lambda i, j, k: (j, k))
            ca, cb = 1, 1
    o_spec = pl.BlockSpec((tm, tn), lambda i, j, k: (i, j))
    return _mm_spec(a, b, name, grid, a_spec, b_spec, o_spec, _sds((m, n), out_dtype), ca, cb, (tm, tn), res=res, r_spec=o_spec)


def _rms_fwd(x, w, name):
    s, d = x.shape
    ts = _row_tile(s)

    def body(x_ref, w_ref, o_ref):
        xv = x_ref[...]
        r = lax.rsqrt(jnp.mean(xv * xv, axis=-1, keepdims=True) + RMS_EPS)
        o_ref[...] = ((xv * r) * w_ref[...]).astype(BF16)

    row = pl.BlockSpec((ts, d), lambda i: (i, 0))
    return _pc(body, name, (s // ts,), [row, pl.BlockSpec((1, d), lambda i: (0, 0))], row, _sds((s, d), BF16))(x, w)


def _rms_bwd(x, w, dh, dres, name):
    s, d = x.shape
    ts = _row_tile(s)

    def body(x_ref, w_ref, dh_ref, dr_ref, dx_ref, dw_ref):
        i = pl.program_id(0)
        xv = x_ref[...]
        r = lax.rsqrt(jnp.mean(xv * xv, axis=-1, keepdims=True) + RMS_EPS)
        xhat = xv * r
        dhv = dh_ref[...]
        g = dhv * w_ref[...]
        dx_ref[...] = dr_ref[...] + r * (g - xhat * jnp.mean(g * xhat, axis=-1, keepdims=True))

        @pl.when(i == 0)
        def _():
            dw_ref[...] = jnp.zeros_like(dw_ref)

        dw_ref[...] += jnp.sum(dhv * xhat, axis=0, keepdims=True)

    row = pl.BlockSpec((ts, d), lambda i: (i, 0))
    vec = pl.BlockSpec((1, d), lambda i: (0, 0))
    return _pc(body, name, (s // ts,), [row, vec, row, row], [row, vec], [_sds((s, d)), _sds((1, d))])(x, w, dh, dres)


def _swiglu_fwd(gu, name):
    s = gu.shape[2]
    ts = _row_tile(s)

    def body(gu_ref, o_ref):
        g = gu_ref[0, 0]
        u = gu_ref[0, 1]
        o_ref[0] = (g * _sigmoid(g) * u).astype(BF16)

    return _pc(body, name, (s // ts, 4), [pl.BlockSpec((1, 2, ts, FF_BLOCK), lambda i, k: (k, 0, i, 0))],
               pl.BlockSpec((1, ts, FF_BLOCK), lambda i, k: (k, i, 0)), _sds((4, s, FF_BLOCK), BF16))(gu)


def _swiglu_bwd(gu, da, name):
    s = gu.shape[2]
    ts = _row_tile(s)

    def body(gu_ref, da_ref, o_ref):
        g = gu_ref[0, 0]
        u = gu_ref[0, 1]
        dav = da_ref[0]
        sg = _sigmoid(g)
        o_ref[0, 0] = (dav * u * (sg * (1.0 + g * (1.0 - sg)))).astype(BF16)
        o_ref[0, 1] = (dav * (g * sg)).astype(BF16)

    pair = pl.BlockSpec((1, 2, ts, FF_BLOCK), lambda i, k: (k, 0, i, 0))
    return _pc(body, name, (s // ts, 4), [pair, pl.BlockSpec((1, ts, FF_BLOCK), lambda i, k: (k, i, 0))], pair,
               _sds((4, 2, s, FF_BLOCK), BF16))(gu, da)


def _ffn_fwd(x, norm_w, w_gu, w_down, tag):
    s = x.shape[0]
    tm = _pick_tile(s)
    h = _rms_fwd(x, norm_w, f"ffn_norm_{tag}")
    gu = _mm_spec(h, w_gu, f"ffn_gu_{tag}", (s // tm, NDEV, 1),
                  pl.BlockSpec((tm, D_MODEL), lambda i, j, k: (i, 0)),
                  pl.BlockSpec((1, D_MODEL, FF_BLOCK), lambda i, j, k: (j, 0, 0)),
                  pl.BlockSpec((1, 1, tm, FF_BLOCK), lambda i, j, k: (j % 4, j // 4, i, 0)),
                  _sds((4, 2, s, FF_BLOCK)), 1, 0, (tm, FF_BLOCK), drop=(0, 1, 2))
    a = _swiglu_fwd(gu, f"ffn_act_{tag}")
    xspec = pl.BlockSpec((tm, D_MODEL), lambda i, j, k: (i, 0))
    y = _mm_spec(a, w_down, f"ffn_down_{tag}", (s // tm, 1, 4),
                 pl.BlockSpec((1, tm, FF_BLOCK), lambda i, j, k: (k, i, 0)),
                 pl.BlockSpec((1, FF_BLOCK, D_MODEL), lambda i, j, k: (k, 0, 0)),
                 xspec, _sds((s, D_MODEL)), 1, 0, (tm, D_MODEL), drop=(1, 1, 0), res=x, r_spec=xspec)
    return y, (x, h, gu, a)


def _ffn_bwd(dy, saved, norm_w, w_gu, w_down, tag):
    x, h, gu, a = saved
    s = x.shape[0]
    tm = _pick_tile(s)
    row = pl.BlockSpec((tm, D_MODEL), lambda i, j, k: (i, 0))
    da = _mm_spec(dy, w_down, f"ffn_dact_{tag}", (s // tm, 4, 1), row,
                  pl.BlockSpec((1, FF_BLOCK, D_MODEL), lambda i, j, k: (j, 0, 0)),
                  pl.BlockSpec((1, tm, FF_BLOCK), lambda i, j, k: (j, i, 0)),
                  _sds((4, s, FF_BLOCK)), 1, 1, (tm, FF_BLOCK), drop=(0, 1, 1))
    g_down = _mm_spec(a, dy, f"ffn_gdown_{tag}", (4, 1, s // tm),
                      pl.BlockSpec((1, tm, FF_BLOCK), lambda i, j, k: (i, k, 0)),
                      pl.BlockSpec((tm, D_MODEL), lambda i, j, k: (k, 0)),
                      pl.BlockSpec((1, FF_BLOCK, D_MODEL), lambda i, j, k: (i, 0, 0)),
                      _sds((4, FF_BLOCK, D_MODEL), BF16), 0, 0, (FF_BLOCK, D_MODEL), drop=(1, 0, 1))
    dgu = _swiglu_bwd(gu, da, f"ffn_dgu_{tag}")
    g_gu = _mm_spec(h, dgu, f"ffn_ggu_{tag}", (NDEV, 1, s // tm),
                    pl.BlockSpec((tm, D_MODEL), lambda i, j, k: (k, 0)),
                    pl.BlockSpec((1, 1, tm, FF_BLOCK), lambda i, j, k: (i % 4, i // 4, k, 0)),
                    pl.BlockSpec((1, D_MODEL, FF_BLOCK), lambda i, j, k: (i, 0, 0)),
                    _sds((NDEV, D_MODEL, FF_BLOCK), BF16), 0, 0, (D_MODEL, FF_BLOCK), drop=(0, 2, 1))
    dh = _mm_spec(dgu, w_gu, f"ffn_dh_{tag}", (s // tm, 1, NDEV),
                  pl.BlockSpec((1, 1, tm, FF_BLOCK), lambda i, j, k: (k % 4, k // 4, i, 0)),
                  pl.BlockSpec((1, D_MODEL, FF_BLOCK), lambda i, j, k: (k, 0, 0)),
                  row, _sds((s, D_MODEL)), 1, 1, (tm, D_MODEL), drop=(2, 1, 0))
    dx, g_norm = _rms_bwd(x, norm_w, dh, dy, f"ffn_dnorm_{tag}")
    return dx, g_norm, g_gu, g_down


def _prev_rows(cur, halo, j, first):
    rid = lax.broadcasted_iota(jnp.int32, cur.shape, 0)
    hid = lax.broadcasted_iota(jnp.int32, halo.shape, 0)
    out = pltpu.roll(cur, j, 0)
    for t in range(j):
        row = jnp.sum(jnp.where(hid == 8 - j + t, halo, 0.0), axis=0, keepdims=True)
        row = jnp.where(first, 0.0, row)
        out = jnp.where(rid == t, row, out)
    return out


def _next_rows(cur, halo, j, last):
    ts = cur.shape[0]
    rid = lax.broadcasted_iota(jnp.int32, cur.shape, 0)
    hid = lax.broadcasted_iota(jnp.int32, halo.shape, 0)
    out = pltpu.roll(cur, ts - j, 0)
    for t in range(j):
        row = jnp.sum(jnp.where(hid == t, halo, 0.0), axis=0, keepdims=True)
        row = jnp.where(last, 0.0, row)
        out = jnp.where(rid == ts - j + t, row, out)
    return out


def _halo_specs(ts, s, width, col):
    per = ts // 8
    nblk = s // 8
    prev = pl.BlockSpec((8, width), lambda i: (jnp.maximum(i * per - 1, 0), col))
    nxt = pl.BlockSpec((8, width), lambda i: (jnp.minimum((i + 1) * per, nblk - 1), col))
    return prev, nxt


def _cgate_fwd(p, w_dw, name):
    s = p.shape[0]
    d = D_MODEL
    ts = _row_tile(s)
    prev, _ = _halo_specs(ts, s, 3 * d, 0)

    def body(p_ref, h_ref, w_ref, z_ref):
        first = pl.program_id(0) == 0
        b = p_ref[:, :d]
        cv = p_ref[:, d:2 * d] * p_ref[:, 2 * d:]
        hcv = h_ref[:, d:2 * d] * h_ref[:, 2 * d:]
        u = w_ref[2:3, :] * cv + w_ref[1:2, :] * _prev_rows(cv, hcv, 1, first) + w_ref[0:1, :] * _prev_rows(cv, hcv, 2, first)
        z_ref[...] = (b * u).astype(BF16)

    return _pc(body, name, (s // ts,),
               [pl.BlockSpec((ts, 3 * d), lambda i: (i, 0)), prev, pl.BlockSpec((3, d), lambda i: (0, 0))],
               pl.BlockSpec((ts, d), lambda i: (i, 0)), _sds((s, d), BF16))(p, p, w_dw)


def _cgate_bwd(p, dz, w_dw, name):
    s = p.shape[0]
    d = D_MODEL
    ts = _row_tile(s)
    nt = s // ts
    p_prev, p_next = _halo_specs(ts, s, 3 * d, 0)
    _, dz_next = _halo_specs(ts, s, d, 0)

    def body(p_ref, hp_ref, hn_ref, dz_ref, dzn_ref, w_ref, dp_ref, dw_ref):
        i = pl.program_id(0)
        first = i == 0
        last = i == nt - 1
        b = p_ref[:, :d]
        c = p_ref[:, d:2 * d]
        v = p_ref[:, 2 * d:]
        cv = c * v
        hcv = hp_ref[:, d:2 * d] * hp_ref[:, 2 * d:]
        cv1 = _prev_rows(cv, hcv, 1, first)
        cv2 = _prev_rows(cv, hcv, 2, first)
        w0, w1, w2 = w_ref[0:1, :], w_ref[1:2, :], w_ref[2:3, :]
        u = w2 * cv + w1 * cv1 + w0 * cv2
        dzv = dz_ref[...]
        du = dzv * b
        dun = dzn_ref[...] * hn_ref[:, :d]
        dcv = w2 * du + w1 * _next_rows(du, dun, 1, last) + w0 * _next_rows(du, dun, 2, last)
        dp_ref[:, :d] = (dzv * u).astype(BF16)
        dp_ref[:, d:2 * d] = (dcv * v).astype(BF16)
        dp_ref[:, 2 * d:] = (dcv * c).astype(BF16)

        @pl.when(first)
        def _():
            dw_ref[...] = jnp.zeros_like(dw_ref)

        dw_ref[0:1, :] += jnp.sum(du * cv2, axis=0, keepdims=True)
        dw_ref[1:2, :] += jnp.sum(du * cv1, axis=0, keepdims=True)
        dw_ref[2:3, :] += jnp.sum(du * cv, axis=0, keepdims=True)

    wide = pl.BlockSpec((ts, 3 * d), lambda i: (i, 0))
    wspec = pl.BlockSpec((3, d), lambda i: (0, 0))
    return _pc(body, name, (nt,),
               [wide, p_prev, p_next, pl.BlockSpec((ts, d), lambda i: (i, 0)), dz_next, wspec],
               [wide, wspec], [_sds((s, 3 * d), BF16), _sds((3, d))])(p, p, p, dz, dz, w_dw)


def _conv_fwd(x, norm_w, w_in, w_dw, w_out, tag):
    s = x.shape[0]
    tm = _pick_tile(s)
    h = _rms_fwd(x, norm_w, f"conv_norm_{tag}")
    p = _mm_spec(h, w_in, f"conv_in_{tag}", (s // tm, NDEV, 1),
                 pl.BlockSpec((tm, D_MODEL), lambda i, j, k: (i, 0)),
                 pl.BlockSpec((1, D_MODEL, CONV_BLOCK), lambda i, j, k: (j, 0, 0)),
                 pl.BlockSpec((tm, CONV_BLOCK), lambda i, j, k: (i, j)),
                 _sds((s, 3 * D_MODEL)), 1, 0, (tm, CONV_BLOCK), drop=(0, 1, 0))
    z = _cgate_fwd(p, w_dw, f"conv_gate_{tag}")
    y = _mm(z, w_out, "nn", f"conv_out_{tag}", res=x)
    return y, (x, h, p, z)


def _conv_bwd(dy, saved, norm_w, w_in, w_dw, w_out, tag):
    x, h, p, z = saved
    s = x.shape[0]
    tm = _pick_tile(s)
    dz = _mm(dy, w_out, "nt", f"conv_dz_{tag}")
    g_out = _mm(z, dy, "tn", f"conv_gout_{tag}", out_dtype=BF16)
    dp, g_dw = _cgate_bwd(p, dz, w_dw, f"conv_dgate_{tag}")
    g_in = _mm_spec(h, dp, f"conv_gin_{tag}", (NDEV, 1, s // tm),
                    pl.BlockSpec((tm, D_MODEL), lambda i, j, k: (k, 0)),
                    pl.BlockSpec((tm, CONV_BLOCK), lambda i, j, k: (k, i)),
                    pl.BlockSpec((1, D_MODEL, CONV_BLOCK), lambda i, j, k: (i, 0, 0)),
                    _sds((NDEV, D_MODEL, CONV_BLOCK), BF16), 0, 0, (D_MODEL, CONV_BLOCK), drop=(0, 0, 1))
    dh = _mm_spec(dp, w_in, f"conv_dh_{tag}", (s // tm, 1, NDEV),
                  pl.BlockSpec((tm, CONV_BLOCK), lambda i, j, k: (i, k)),
                  pl.BlockSpec((1, D_MODEL, CONV_BLOCK), lambda i, j, k: (k, 0, 0)),
                  pl.BlockSpec((tm, D_MODEL), lambda i, j, k: (i, 0)),
                  _sds((s, D_MODEL)), 1, 1, (tm, D_MODEL), drop=(0, 1, 0))
    dx, g_norm = _rms_bwd(x, norm_w, dh, dy, f"conv_dnorm_{tag}")
    return dx, g_norm, g_in, g_dw, g_out


def _tri(lower):
    r = lax.broadcasted_iota(jnp.int32, (LANES, LANES), 0)
    c = lax.broadcasted_iota(jnp.int32, (LANES, LANES), 1)
    return jnp.where((r >= c) if lower else (r <= c), 1.0, 0.0).astype(F32)


def _cumsum_rows(v, reverse, name):
    s = v.shape[0]
    n = s // LANES
    idx = (lambda i: (n - 1 - i, 0)) if reverse else (lambda i: (i, 0))

    def body(v_ref, o_ref, carry_ref):
        @pl.when(pl.program_id(0) == 0)
        def _():
            carry_ref[...] = jnp.zeros_like(carry_ref)

        blk = v_ref[...]
        o_ref[...] = _dot(_tri(not reverse), blk, 1, 0, HI) + carry_ref[0:1, :]
        carry_ref[...] += jnp.sum(blk, axis=0, keepdims=True)

    spec = pl.BlockSpec((LANES, LANES), idx)
    return _pc(body, name, (n,), [spec], spec, _sds((s, LANES)), [pltpu.VMEM((8, LANES), F32)])(v)


def _lo_mask(shape):
    return lax.broadcasted_iota(jnp.int32, shape, len(shape) - 1) < HEAD_DIM


def _half_sums(v, lo):
    sa = jnp.sum(jnp.where(lo, v, 0.0), axis=-1, keepdims=True)
    sb = jnp.sum(jnp.where(lo, 0.0, v), axis=-1, keepdims=True)
    return jnp.where(lo, sa, sb)


def _fox_prep_fwd(proj, gq, gk, name):
    s = proj.shape[0]
    ts = _row_tile(s, 512)
    qscale = HEAD_DIM ** -0.5 * LOG2E

    def body(q_ref, k_ref, v_ref, gq_ref, gk_ref, qo_ref, ko_ref, vo_ref):
        lo = _lo_mask((ts, LANES))

        def hnorm(xv, g):
            ms = _half_sums(xv * xv, lo) * (1.0 / HEAD_DIM)
            return (xv * lax.rsqrt(ms + RMS_EPS)) * g

        qo_ref[...] = (hnorm(q_ref[...], gq_ref[...]) * qscale).astype(BF16)
        ko_ref[...] = hnorm(k_ref[...], gk_ref[...]).astype(BF16)
        vo_ref[...] = v_ref[...].astype(BF16)

    def col(off):
        return pl.BlockSpec((ts, LANES), lambda i, p: (i, off + p))

    gspec = pl.BlockSpec((1, LANES), lambda i, p: (0, 0))
    out = _sds((s, D_MODEL), BF16)
    return _pc(body, name, (s // ts, 8), [col(0), col(8), col(16), gspec, gspec], [col(0)] * 3, [out] * 3)(
        proj, proj, proj, gq, gk)


def _fox_logf(proj, bf, name):
    s = proj.shape[0]
    ts = _row_tile(s, 512)

    def body(f_ref, b_ref, o_ref):
        z = f_ref[...] + b_ref[...]
        lf = jnp.minimum(z, 0.0) - jnp.log(1.0 + jnp.exp(-jnp.abs(z)))
        real = lax.broadcasted_iota(jnp.int32, (ts, LANES), 1) < ATTN_HEADS
        o_ref[...] = jnp.where(real, lf, 0.0)

    return _pc(body, name, (s // ts,), [pl.BlockSpec((ts, LANES), lambda i: (i, 24)), pl.BlockSpec((1, LANES), lambda i: (0, 0))],
               pl.BlockSpec((ts, LANES), lambda i: (i, 0)), _sds((s, LANES)))(proj, bf)


def _fox_dlogf(proj, bf, dlf, name):
    s = proj.shape[0]
    ts = _row_tile(s, 512)

    def body(f_ref, b_ref, d_ref, o_ref, db_ref):
        z = f_ref[...] + b_ref[...]
        real = lax.broadcasted_iota(jnp.int32, (ts, LANES), 1) < ATTN_HEADS
        g = jnp.where(real, d_ref[...] * _sigmoid(-z), 0.0)
        o_ref[...] = g.astype(BF16)

        @pl.when(pl.program_id(0) == 0)
        def _():
            db_ref[...] = jnp.zeros_like(db_ref)

        db_ref[...] += jnp.sum(g, axis=0, keepdims=True)

    vec = pl.BlockSpec((1, LANES), lambda i: (0, 0))
    row = pl.BlockSpec((ts, LANES), lambda i: (i, 0))
    return _pc(body, name, (s // ts,), [pl.BlockSpec((ts, LANES), lambda i: (i, 24)), vec, row], [row, vec],
               [_sds((s, LANES), BF16), _sds((1, LANES))])(proj, bf, dlf)


def _decay_terms(cum):
    s = cum.shape[0]
    c2 = cum * LOG2E
    hi = lax.reduce_precision(c2, 8, 7)
    mid = lax.reduce_precision(c2 - hi, 8, 7)
    low = lax.reduce_precision(c2 - hi - mid, 8, 7)
    one = jnp.ones_like(hi)

    def place(terms):
        tt = jnp.stack(terms, axis=-1).astype(BF16).reshape(s, 8, 2, 6)
        z = jnp.zeros((s, 8, HEAD_DIM - 6), BF16)
        return jnp.concatenate([tt[:, :, 1], z, tt[:, :, 0], z], axis=-1).reshape(s, D_MODEL)

    return place([hi, mid, low, one, one, one]), place([one, one, one, -hi, -mid, -low])


def _attn_tiles(s):
    t = 512 if s % 512 == 0 else s
    return t, s // t


def _tri_steps(n, by_key):
    if by_key:
        pairs = [(q, k) for k in range(n) for q in range(k, n)]
    else:
        pairs = [(q, k) for q in range(n) for k in range(q + 1)]
    arr = np.asarray(pairs, np.int32)
    return jnp.asarray(arr[:, 0]), jnp.asarray(arr[:, 1])


def _attn_call(body, name, s, by_key, inputs, in_kinds, out_kinds, out_shapes, scratch, hosted=None):
    t, n = _attn_tiles(s)
    qi_arr, ki_arr = _tri_steps(n, by_key)
    nsteps = int(qi_arr.shape[0])
    specs = {
        "q": pl.BlockSpec((t, LANES), lambda p, i, qi, ki: (qi[i], p)),
        "k": pl.BlockSpec((t, LANES), lambda p, i, qi, ki: (ki[i], p)),
        "r": pl.BlockSpec((1, 2, t), lambda p, i, qi, ki: (p, 0, qi[i])),
        "m": pl.BlockSpec((1, t, t), lambda p, i, qi, ki: (jnp.where(qi[i] == ki[i], 1, 0), 0, 0)),
    }
    in_specs = [specs[c] for c in in_kinds]
    out_specs = [specs[c] for c in out_kinds]
    out_shapes, scratch, inputs = list(out_shapes), list(scratch), list(inputs)
    run = body
    if hosted is not None:
        arrays, gather = hosted
        na, n_in, n_out, n_scr = len(arrays), len(inputs), len(out_kinds), len(scratch)
        pick, xouts, sems = _exchange_parts(arrays, gather)

        def run(qi_ref, ki_ref, *refs):
            ins, srcs = refs[:n_in], refs[n_in:n_in + na]
            outs, dsts = refs[n_in + na:n_in + na + n_out], refs[n_in + na + n_out:n_in + 2 * na + n_out]
            scr, xsems = refs[n_in + 2 * na + n_out:n_in + 2 * na + n_out + n_scr], refs[n_in + 2 * na + n_out + n_scr:]
            p = pl.program_id(0)
            i = pl.program_id(1)

            @pl.when(jnp.logical_and(p == 0, i == 0))
            def _():
                _exchange_start(_exchange_copies(pick(srcs), dsts, *xsems))

            body(qi_ref, ki_ref, *ins, *outs, *scr)

            @pl.when(jnp.logical_and(p == 7, i == nsteps - 1))
            def _():
                _exchange_wait(_exchange_copies(pick(srcs), dsts, *xsems))

        hbm = pl.BlockSpec(memory_space=pl.ANY)
        in_specs += [hbm] * na
        out_specs += [hbm] * na
        out_shapes += xouts
        scratch += sems
        inputs += list(arrays)
    grid_spec = pltpu.PrefetchScalarGridSpec(
        num_scalar_prefetch=2, grid=(8, nsteps), in_specs=in_specs, out_specs=out_specs, scratch_shapes=scratch)
    return pl.pallas_call(run, name=name, grid_spec=grid_spec, out_shape=out_shapes, compiler_params=_params(2))(
        qi_arr, ki_arr, *inputs)


def _biased_kq(q2, k2, aq, ak, lo):
    sa = _dot(jnp.where(lo, k2, ak), jnp.where(lo, q2, aq), 1, 1)
    sb = _dot(jnp.where(lo, ak, k2), jnp.where(lo, aq, q2), 1, 1)
    return sa, sb


def _causal_bias(s):
    t, _ = _attn_tiles(s)
    kid = lax.broadcasted_iota(jnp.int32, (t, t), 0)
    qid = lax.broadcasted_iota(jnp.int32, (t, t), 1)
    return jnp.stack([jnp.zeros((t, t), F32), jnp.where(kid > qid, -jnp.inf, 0.0).astype(F32)])


def _fold8(v, op):
    return op(v.reshape(v.shape[0] // 8, 8, v.shape[1]), axis=0)


def _chunk(ref, mask_ref, hd, r):
    rows = slice(r * ATTN_ROWS, (r + 1) * ATTN_ROWS)
    return rows, ref[hd, rows, :] + mask_ref[0, rows, :]


def _flash_fwd(qs, kn, vb, augq, augk, cmask, name, hosted=None):
    s = qs.shape[0]
    t, n = _attn_tiles(s)
    nch = t // ATTN_ROWS

    def body(qi_ref, ki_ref, q_ref, k_ref, v_ref, aq_ref, ak_ref, mk_ref, o_ref, lse_ref, s_ref, p_ref, m_ref, l_ref, acc_ref):
        i = pl.program_id(1)
        qi = qi_ref[i]
        ki = ki_ref[i]

        @pl.when(ki == 0)
        def _():
            m_ref[...] = jnp.full_like(m_ref, -jnp.inf)
            l_ref[...] = jnp.zeros_like(l_ref)
            acc_ref[...] = jnp.zeros_like(acc_ref)

        lo = _lo_mask((t, LANES))
        rowlo = lax.broadcasted_iota(jnp.int32, (LANES, t), 0) < HEAD_DIM
        v2 = v_ref[...]
        sa, sb = _biased_kq(q_ref[...], k_ref[...], aq_ref[...], ak_ref[...], lo)
        s_ref[0] = sa
        s_ref[1] = sb
        alphas, pvs = [], []
        for hd in range(2):
            mx = jnp.full((8, t), -jnp.inf, F32)
            for r in range(nch):
                _, sc = _chunk(s_ref, mk_ref, hd, r)
                mx = jnp.maximum(mx, _fold8(sc, jnp.max))
            m_prev = m_ref[hd:hd + 1, :]
            m_new = jnp.maximum(m_prev, jnp.max(mx, axis=0, keepdims=True))
            ls = jnp.zeros((8, t), F32)
            for r in range(nch):
                rows, sc = _chunk(s_ref, mk_ref, hd, r)
                pm = jnp.exp2(sc - m_new)
                ls = ls + _fold8(pm, jnp.sum)
                p_ref[hd, rows, :] = pm.astype(BF16)
            alpha = jnp.exp2(m_prev - m_new)
            l_ref[hd:hd + 1, :] = alpha * l_ref[hd:hd + 1, :] + jnp.sum(ls, axis=0, keepdims=True)
            m_ref[hd:hd + 1, :] = m_new
            alphas.append(alpha)
            pvs.append(_dot(v2, p_ref[hd], 0, 0))
        acc_ref[...] = jnp.where(rowlo, alphas[0], alphas[1]) * acc_ref[...] + jnp.where(rowlo, pvs[0], pvs[1])

        @pl.when(ki == qi)
        def _():
            o_ref[...] = (acc_ref[...] / jnp.where(rowlo, l_ref[0:1, :], l_ref[1:2, :])).T
            lse_ref[0] = m_ref[0:2, :] + jnp.log2(l_ref[0:2, :])

    stat = pltpu.VMEM((8, t), F32)
    return _attn_call(body, name, s, False, (qs, kn, vb, augq, augk, cmask), "qkkqkm", "qr",
                      [_sds((s, D_MODEL)), _sds((8, 2, s))],
                      [pltpu.VMEM((2, t, t), F32), pltpu.VMEM((2, t, t), BF16), stat, stat, pltpu.VMEM((LANES, t), F32)],
                      hosted=hosted)


def _fox_delta(do, o, name):
    s = do.shape[0]
    ts = _row_tile(s, 512)

    def body(do_ref, o_ref, d_ref):
        d_ref[...] = _half_sums(do_ref[...] * o_ref[...], _lo_mask((ts, LANES)))

    spec = pl.BlockSpec((ts, LANES), lambda i, p: (i, p))
    return _pc(body, name, (s // ts, 8), [spec, spec], spec, _sds((s, D_MODEL)))(do, o)


def _bwd_tile(q_ref, k_ref, v_ref, aq_ref, ak_ref, do_ref, s_ref, dp_ref, lo):
    do2 = do_ref[...].astype(BF16)
    zero = jnp.zeros_like(do2)
    v2 = v_ref[...]
    sa, sb = _biased_kq(q_ref[...], k_ref[...], aq_ref[...], ak_ref[...], lo)
    s_ref[0] = sa
    s_ref[1] = sb
    dp_ref[0] = _dot(v2, jnp.where(lo, do2, zero), 1, 1)
    dp_ref[1] = _dot(v2, jnp.where(lo, zero, do2), 1, 1)
    return do2


def _bwd_chunk(s_ref, dp_ref, mk_ref, lse_ref, dl_ref, hd, r):
    rows, sc = _chunk(s_ref, mk_ref, hd, r)
    pm = jnp.exp2(sc - lse_ref[0, hd:hd + 1, :])
    ds = pm * (dp_ref[hd, rows, :] - dl_ref[0, hd:hd + 1, :])
    return rows, pm, ds


def _flash_bwd_dq(qs, kn, vb, augq, augk, cmask, do, lse, delta, name, hosted=None):
    s = qs.shape[0]
    t, n = _attn_tiles(s)
    nch = t // ATTN_ROWS

    def body(qi_ref, ki_ref, q_ref, k_ref, v_ref, aq_ref, ak_ref, mk_ref, do_ref, lse_ref, dl_ref, dq_ref, dcq_ref,
             s_ref, dp_ref, ds_ref, acc_ref, racc_ref):
        i = pl.program_id(1)
        qi = qi_ref[i]
        ki = ki_ref[i]

        @pl.when(ki == 0)
        def _():
            acc_ref[...] = jnp.zeros_like(acc_ref)
            racc_ref[...] = jnp.zeros_like(racc_ref)

        lo = _lo_mask((t, LANES))
        rowlo = lax.broadcasted_iota(jnp.int32, (LANES, t), 0) < HEAD_DIM
        _bwd_tile(q_ref, k_ref, v_ref, aq_ref, ak_ref, do_ref, s_ref, dp_ref, lo)
        k2 = k_ref[...]
        dqs = []
        for hd in range(2):
            rs = jnp.zeros((8, t), F32)
            for r in range(nch):
                rows, _, ds = _bwd_chunk(s_ref, dp_ref, mk_ref, lse_ref, dl_ref, hd, r)
                rs = rs + _fold8(ds, jnp.sum)
                ds_ref[hd, rows, :] = ds.astype(BF16)
            racc_ref[hd:hd + 1, :] += jnp.sum(rs, axis=0, keepdims=True)
            dqs.append(_dot(k2, ds_ref[hd], 0, 0))
        acc_ref[...] += jnp.where(rowlo, dqs[0], dqs[1])

        @pl.when(ki == qi)
        def _():
            dq_ref[...] = acc_ref[...].T
            dcq_ref[0] = racc_ref[0:2, :]

    return _attn_call(body, name, s, False, (qs, kn, vb, augq, augk, cmask, do, lse, delta), "qkkqkmqrr", "qr",
                      [_sds((s, D_MODEL)), _sds((8, 2, s))],
                      [pltpu.VMEM((2, t, t), F32), pltpu.VMEM((2, t, t), F32), pltpu.VMEM((2, t, t), BF16),
                       pltpu.VMEM((LANES, t), F32), pltpu.VMEM((8, t), F32)], hosted=hosted)


def _flash_bwd_dkv(qs, kn, vb, augq, augk, cmask, do, lse, delta, name):
    s = qs.shape[0]
    t, n = _attn_tiles(s)
    nch = t // ATTN_ROWS

    def body(qi_ref, ki_ref, q_ref, k_ref, v_ref, aq_ref, ak_ref, mk_ref, do_ref, lse_ref, dl_ref, dk_ref, dv_ref, dc_ref,
             s_ref, dp_ref, p_ref, ds_ref, dka_ref, dva_ref, dca_ref):
        i = pl.program_id(1)
        qi = qi_ref[i]
        ki = ki_ref[i]

        @pl.when(qi == ki)
        def _():
            dka_ref[...] = jnp.zeros_like(dka_ref)
            dva_ref[...] = jnp.zeros_like(dva_ref)
            dca_ref[...] = jnp.zeros_like(dca_ref)

        lo = _lo_mask((t, LANES))
        do2 = _bwd_tile(q_ref, k_ref, v_ref, aq_ref, ak_ref, do_ref, s_ref, dp_ref, lo)
        q2 = q_ref[...]
        dvs, dks = [], []
        for hd in range(2):
            for r in range(nch):
                rows, pm, ds = _bwd_chunk(s_ref, dp_ref, mk_ref, lse_ref, dl_ref, hd, r)
                part = ds[:, 0:LANES]
                for c in range(1, t // LANES):
                    part = part + ds[:, c * LANES:(c + 1) * LANES]
                dca_ref[hd, rows, :] += part
                p_ref[hd, rows, :] = pm.astype(BF16)
                ds_ref[hd, rows, :] = ds.astype(BF16)
            dvs.append(_dot(p_ref[hd], do2, 1, 0))
            dks.append(_dot(ds_ref[hd], q2, 1, 0))
        dva_ref[...] += jnp.where(lo, dvs[0], dvs[1])
        dka_ref[...] += jnp.where(lo, dks[0], dks[1])

        @pl.when(qi == n - 1)
        def _():
            dk_ref[...] = dka_ref[...] * LN2
            dv_ref[...] = dva_ref[...]
            dc_ref[...] = -jnp.where(lo, jnp.sum(dca_ref[0], axis=-1, keepdims=True), jnp.sum(dca_ref[1], axis=-1, keepdims=True))

    out = _sds((s, D_MODEL))
    return _attn_call(body, name, s, True, (qs, kn, vb, augq, augk, cmask, do, lse, delta), "qkkqkmqrr", "kkk", [out, out, out],
                      [pltpu.VMEM((2, t, t), F32), pltpu.VMEM((2, t, t), F32), pltpu.VMEM((2, t, t), BF16),
                       pltpu.VMEM((2, t, t), BF16), pltpu.VMEM((t, LANES), F32), pltpu.VMEM((t, LANES), F32),
                       pltpu.VMEM((2, t, LANES), F32)])


def _fox_prep_bwd(proj, dqs, dk, dv, gq, gk, name):
    s = proj.shape[0]
    ts = _row_tile(s, 512)
    scale = HEAD_DIM ** -0.5

    def body(q_ref, k_ref, dq_ref, dk_ref, dv_ref, gq_ref, gk_ref, oq_ref, ok_ref, ov_ref, dgq_ref, dgk_ref):
        lo = _lo_mask((ts, LANES))

        @pl.when(jnp.logical_and(pl.program_id(0) == 0, pl.program_id(1) == 0))
        def _():
            dgq_ref[...] = jnp.zeros_like(dgq_ref)
            dgk_ref[...] = jnp.zeros_like(dgk_ref)

        def back(xv, dout, g):
            r = lax.rsqrt(_half_sums(xv * xv, lo) * (1.0 / HEAD_DIM) + RMS_EPS)
            y = xv * r
            dy = dout * g
            dx = r * (dy - y * (_half_sums(dy * y, lo) * (1.0 / HEAD_DIM)))
            return dx, jnp.sum(dout * y, axis=0, keepdims=True)

        dxq, dgq = back(q_ref[...], dq_ref[...] * scale, gq_ref[...])
        dxk, dgk = back(k_ref[...], dk_ref[...], gk_ref[...])
        oq_ref[...] = dxq.astype(BF16)
        ok_ref[...] = dxk.astype(BF16)
        ov_ref[...] = dv_ref[...].astype(BF16)
        dgq_ref[...] += dgq
        dgk_ref[...] += dgk

    def col(off):
        return pl.BlockSpec((ts, LANES), lambda i, p: (i, off + p))

    gspec = pl.BlockSpec((1, LANES), lambda i, p: (0, 0))
    out = _sds((s, D_MODEL), BF16)
    return _pc(body, name, (s // ts, 8), [col(0), col(8), col(0), col(0), col(0), gspec, gspec],
               [col(0)] * 3 + [gspec] * 2, [out] * 3 + [_sds((1, LANES))] * 2)(proj, proj, dqs, dk, dv, gq, gk)


def _fox_fwd(x, norm_w, w_in, b_f, q_gain, k_gain, w_out, hosted=None):
    h = _rms_fwd(x, norm_w, "fox_norm")
    proj = _mm(h, w_in, "nn", "fox_in")
    gq = jnp.tile(q_gain, (1, 2))
    gk = jnp.tile(k_gain, (1, 2))
    bf = jnp.pad(b_f, ((0, 0), (0, LANES - ATTN_HEADS)))
    qs, kn, vb = _fox_prep_fwd(proj, gq, gk, "fox_prep")
    cum = _cumsum_rows(_fox_logf(proj, bf, "fox_logf"), False, "fox_cum")[:, :ATTN_HEADS]
    augq, augk = _decay_terms(cum)
    cmask = _causal_bias(x.shape[0])
    o, lse, *got = _flash_fwd(qs, kn, vb, augq, augk, cmask, "fox_attn", hosted=hosted)
    y = _mm(o, w_out, "nn", "fox_out", res=x)
    return y, (x, h, proj, gq, gk, bf, qs, kn, vb, augq, augk, cmask, o, lse), got


def _fox_bwd(dy, saved, norm_w, w_in, w_out, hosted=None):
    x, h, proj, gq, gk, bf, qs, kn, vb, augq, augk, cmask, o, lse = saved
    s = x.shape[0]
    do = _mm(dy, w_out, "nt", "fox_do")
    g_out = _mm(o, dy, "tn", "fox_gout", out_dtype=BF16)
    delta = _fox_delta(do, o, "fox_delta")[:, ::HEAD_DIM].T.reshape(8, 2, s)
    dqs, dcq, *got = _flash_bwd_dq(qs, kn, vb, augq, augk, cmask, do, lse, delta, "fox_dq", hosted=hosted)
    dk, dv, dck = _flash_bwd_dkv(qs, kn, vb, augq, augk, cmask, do, lse, delta, "fox_dkv")
    dcum = jnp.pad(dcq.reshape(ATTN_HEADS, s).T + dck[:, ::HEAD_DIM], ((0, 0), (0, LANES - ATTN_HEADS)))
    dlf = _cumsum_rows(dcum, True, "fox_dcum")
    dfl, g_bf = _fox_dlogf(proj, bf, dlf, "fox_dlogf")
    dq_o, dk_o, dv_o, g_gq, g_gk = _fox_prep_bwd(proj, dqs, dk, dv, gq, gk, "fox_dprep")
    dproj = jnp.concatenate([dq_o, dk_o, dv_o, dfl], axis=1)
    g_in = _mm(h, dproj, "tn", "fox_gin", out_dtype=BF16)
    dh = _mm(dproj, w_in, "nt", "fox_dh")
    dx, g_norm = _rms_bwd(x, norm_w, dh, dy, "fox_dnorm")
    g_q = g_gq[:, :HEAD_DIM] + g_gq[:, HEAD_DIM:]
    g_k = g_gk[:, :HEAD_DIM] + g_gk[:, HEAD_DIM:]
    return dx, g_norm, g_in[:, :FOX_IN], g_bf[:, :ATTN_HEADS], g_q, g_k, g_out, got


def _ssd_conv_fwd(proj, cw, cb, name):
    s = proj.shape[0]
    ts = _row_tile(s)
    w = 1024
    per = ts // 8

    def body(p_ref, h_ref, w_ref, b_ref, o_ref):
        first = pl.program_id(0) == 0
        cur = p_ref[...]
        halo = h_ref[...]
        u = w_ref[3:4, :] * cur + b_ref[...]
        for j in range(1, 4):
            u = u + w_ref[3 - j:4 - j, :] * _prev_rows(cur, halo, j, first)
        o_ref[...] = u * _sigmoid(u)

    return _pc(body, name, (s // ts, 4),
               [pl.BlockSpec((ts, w), lambda i, j: (i, 2 + j)),
                pl.BlockSpec((8, w), lambda i, j: (jnp.maximum(i * per - 1, 0), 2 + j)),
                pl.BlockSpec((4, w), lambda i, j: (0, j)), pl.BlockSpec((1, w), lambda i, j: (0, j))],
               pl.BlockSpec((ts, w), lambda i, j: (i, j)), _sds((s, SSM_CONV_DIM)))(proj, proj, cw, cb)


def _ssd_conv_bwd_act(proj, dxbc, cw, cb, name):
    s = proj.shape[0]
    ts = _row_tile(s)
    w = 1024
    per = ts // 8

    def body(p_ref, h_ref, d_ref, w_ref, b_ref, g_ref, db_ref):
        first = pl.program_id(1) == 0
        cur = p_ref[...]
        halo = h_ref[...]
        u = w_ref[3:4, :] * cur + b_ref[...]
        for j in range(1, 4):
            u = u + w_ref[3 - j:4 - j, :] * _prev_rows(cur, halo, j, first)
        sg = _sigmoid(u)
        g = d_ref[...] * (sg * (1.0 + u * (1.0 - sg)))
        g_ref[...] = g

        @pl.when(first)
        def _():
            db_ref[...] = jnp.zeros_like(db_ref)

        db_ref[...] += jnp.sum(g, axis=0, keepdims=True)

    vec = pl.BlockSpec((1, w), lambda j, i: (0, j))
    tile = pl.BlockSpec((ts, w), lambda j, i: (i, j))
    return _pc(body, name, (4, s // ts),
               [pl.BlockSpec((ts, w), lambda j, i: (i, 2 + j)),
                pl.BlockSpec((8, w), lambda j, i: (jnp.maximum(i * per - 1, 0), 2 + j)),
                tile, pl.BlockSpec((4, w), lambda j, i: (0, j)), vec],
               [tile, vec], [_sds((s, SSM_CONV_DIM)), _sds((1, SSM_CONV_DIM))])(proj, proj, dxbc, cw, cb)


def _ssd_conv_bwd_in(proj, g, cw, name):
    s = proj.shape[0]
    ts = _row_tile(s)
    nt = s // ts
    w = 1024
    per = ts // 8
    nblk = s // 8

    def body(p_ref, h_ref, g_ref, gn_ref, w_ref, o_ref, dw_ref):
        i = pl.program_id(1)
        first = i == 0
        last = i == nt - 1
        cur = p_ref[...]
        halo = h_ref[...]
        gv = g_ref[...]
        gn = gn_ref[...]

        @pl.when(first)
        def _():
            dw_ref[...] = jnp.zeros_like(dw_ref)

        dpre = w_ref[3:4, :] * gv
        dw_ref[3:4, :] += jnp.sum(gv * cur, axis=0, keepdims=True)
        for j in range(1, 4):
            dpre = dpre + w_ref[3 - j:4 - j, :] * _next_rows(gv, gn, j, last)
            dw_ref[3 - j:4 - j, :] += jnp.sum(gv * _prev_rows(cur, halo, j, first), axis=0, keepdims=True)
        o_ref[...] = dpre.astype(BF16)

    tile = pl.BlockSpec((ts, w), lambda j, i: (i, j))
    wspec = pl.BlockSpec((4, w), lambda j, i: (0, j))
    return _pc(body, name, (4, nt),
               [pl.BlockSpec((ts, w), lambda j, i: (i, 2 + j)),
                pl.BlockSpec((8, w), lambda j, i: (jnp.maximum(i * per - 1, 0), 2 + j)),
                tile, pl.BlockSpec((8, w), lambda j, i: (jnp.minimum((i + 1) * per, nblk - 1), j)), wspec],
               [tile, wspec], [_sds((s, SSM_CONV_DIM), BF16), _sds((4, SSM_CONV_DIM))])(proj, proj, g, g, cw)


def _ssd_dt_fwd(proj, bias, a_neg, name):
    s = proj.shape[0]
    n = s // SSM_CHUNK

    def body(r_ref, b_ref, a_ref, dt_ref, ac_ref):
        real = lax.broadcasted_iota(jnp.int32, (SSM_CHUNK, LANES), 1) < SSM_HEADS
        dt = jnp.where(real, _softplus(r_ref[...] + b_ref[...]), 0.0)
        dt_ref[...] = dt
        ac_ref[...] = _dot(_tri(True), dt * a_ref[...], 1, 0, HI)

    vec = pl.BlockSpec((1, LANES), lambda c: (0, 0))
    row = pl.BlockSpec((SSM_CHUNK, LANES), lambda c: (c, 0))
    return _pc(body, name, (n,), [pl.BlockSpec((SSM_CHUNK, LANES), lambda c: (c, 48)), vec, vec], [row, row],
               [_sds((s, LANES)), _sds((s, LANES))])(proj, bias, a_neg)


def _ssd_dt_bwd(proj, bias, ddt, name):
    s = proj.shape[0]
    ts = _row_tile(s, 512)

    def body(r_ref, b_ref, d_ref, o_ref, db_ref):
        real = lax.broadcasted_iota(jnp.int32, (ts, LANES), 1) < SSM_HEADS
        g = jnp.where(real, d_ref[...] * _sigmoid(r_ref[...] + b_ref[...]), 0.0)
        o_ref[...] = g.astype(BF16)

        @pl.when(pl.program_id(0) == 0)
        def _():
            db_ref[...] = jnp.zeros_like(db_ref)

        db_ref[...] += jnp.sum(g, axis=0, keepdims=True)

    vec = pl.BlockSpec((1, LANES), lambda i: (0, 0))
    row = pl.BlockSpec((ts, LANES), lambda i: (i, 0))
    return _pc(body, name, (s // ts,), [pl.BlockSpec((ts, LANES), lambda i: (i, 48)), vec, row], [row, vec],
               [_sds((s, LANES), BF16), _sds((1, LANES))])(proj, bias, ddt)


def _pair_cols(cols, k0, lo):
    return jnp.where(lo, cols[:, k0:k0 + 1], cols[:, k0 + 1:k0 + 2])


def _last_lane(row):
    lane = lax.broadcasted_iota(jnp.int32, row.shape, 1)
    return jnp.sum(jnp.where(lane == SSM_CHUNK - 1, row, 0.0), axis=-1, keepdims=True)


def _ssd_specs(nc, rev):
    cc = (lambda c: nc - 1 - c) if rev else (lambda c: c)
    return dict(
        x=pl.BlockSpec((SSM_CHUNK, 256), lambda g, c: (cc(c), g)),
        b=pl.BlockSpec((SSM_CHUNK, LANES), lambda g, c: (cc(c), 16 + g)),
        c=pl.BlockSpec((SSM_CHUNK, LANES), lambda g, c: (cc(c), 24 + g)),
        col=pl.BlockSpec((1, SSM_CHUNK, 4), lambda g, c: (g, cc(c), 0)),
        row=pl.BlockSpec((1, 4, SSM_CHUNK), lambda g, c: (g, 0, cc(c))),
        grp=pl.BlockSpec((1, 1, 256), lambda g, c: (g, 0, 0)),
        grow=pl.BlockSpec((1, 4, LANES), lambda g, c: (g, 0, 0)),
        hs=pl.BlockSpec((1, 1, 256, SSM_STATE), lambda g, c: (cc(c), g, 0, 0)),
        bc=pl.BlockSpec((SSM_CHUNK, LANES), lambda g, c: (cc(c), g)),
    )


def _ssd_scan_fwd(xbc, dtc, acol, drow, arow, dskip, name):
    s = xbc.shape[0]
    nc = s // SSM_CHUNK
    sp = _ssd_specs(nc, False)
    L = SSM_CHUNK

    def body(x_ref, b_ref, c_ref, dtc_ref, ac_ref, dr_ref, ar_ref, dk_ref, y_ref, hs_ref, h_ref):
        @pl.when(pl.program_id(1) == 0)
        def _():
            h_ref[...] = jnp.zeros_like(h_ref)

        bb = b_ref[...].astype(BF16)
        cb = c_ref[...].astype(BF16)
        gm = _dot(cb, bb, 1, 1)
        dtc = dtc_ref[0]
        ac = ac_ref[0]
        dr = dr_ref[0]
        ar = ar_ref[0]
        dsk = dk_ref[0]
        hs_ref[0, 0] = h_ref[...]
        tril = lax.broadcasted_iota(jnp.int32, (L, L), 0) >= lax.broadcasted_iota(jnp.int32, (L, L), 1)
        lo = _lo_mask((L, LANES))
        rowlo = lax.broadcasted_iota(jnp.int32, (L, LANES), 0) < HEAD_DIM
        for pr in range(2):
            k0 = 2 * pr
            xp = x_ref[:, pr * LANES:(pr + 1) * LANES]
            xpb = xp.astype(BF16)
            hp = h_ref[pr * LANES:(pr + 1) * LANES, :]
            yd, al = [], []
            for k in (k0, k0 + 1):
                seg = ac[:, k:k + 1] - ar[k:k + 1, :]
                wk = gm * jnp.exp(jnp.where(tril, seg, -jnp.inf)) * dr[k:k + 1, :]
                yd.append(_dot(wk.astype(BF16), xpb, 1, 0))
                al.append(_last_lane(ar[k:k + 1, :]))
            e = jnp.exp(_pair_cols(ac, k0, lo))
            yo = _dot(cb, hp.astype(BF16), 1, 1) * e
            y_ref[:, pr * LANES:(pr + 1) * LANES] = jnp.where(lo, yd[0], yd[1]) + yo + dsk[:, pr * LANES:(pr + 1) * LANES] * xp
            wp = jnp.where(lo, jnp.exp(al[0] - ac[:, k0:k0 + 1]) * dtc[:, k0:k0 + 1],
                           jnp.exp(al[1] - ac[:, k0 + 1:k0 + 2]) * dtc[:, k0 + 1:k0 + 2])
            st = _dot((xp * wp).astype(BF16), bb, 0, 0)
            dec = jnp.where(rowlo, jnp.exp(al[0]), jnp.exp(al[1]))
            h_ref[pr * LANES:(pr + 1) * LANES, :] = dec * hp + st

    return _pc(body, name, (SSM_GROUPS, nc),
               [sp["x"], sp["b"], sp["c"], sp["col"], sp["col"], sp["row"], sp["row"], sp["grp"]],
               [sp["x"], sp["hs"]], [_sds((s, SSM_INNER)), _sds((nc, SSM_GROUPS, 256, SSM_STATE))],
               [pltpu.VMEM((256, SSM_STATE), F32)])(xbc, xbc, xbc, dtc, acol, drow, arow, dskip)


def _ssd_scan_bwd(xbc, dtc, acol, drow, arow, dskip, agrp, hs, dy, name):
    s = xbc.shape[0]
    nc = s // SSM_CHUNK
    sp = _ssd_specs(nc, True)
    L = SSM_CHUNK

    def body(x_ref, b_ref, c_ref, dtc_ref, ac_ref, dr_ref, ar_ref, dk_ref, ag_ref, hs_ref, dy_ref,
             dx_ref, db_ref, dc_ref, ddt_ref, da_ref, dd_ref, dh_ref):
        @pl.when(pl.program_id(1) == 0)
        def _():
            dh_ref[...] = jnp.zeros_like(dh_ref)
            da_ref[...] = jnp.zeros_like(da_ref)
            dd_ref[...] = jnp.zeros_like(dd_ref)

        bb = b_ref[...].astype(BF16)
        cb = c_ref[...].astype(BF16)
        gm = _dot(cb, bb, 1, 1)
        dtc = dtc_ref[0]
        ac = ac_ref[0]
        dr = dr_ref[0]
        ar = ar_ref[0]
        dsk = dk_ref[0]
        ag = ag_ref[0]
        tril = lax.broadcasted_iota(jnp.int32, (L, L), 0) >= lax.broadcasted_iota(jnp.int32, (L, L), 1)
        lo = _lo_mask((L, LANES))
        nlo = jnp.logical_not(lo)
        rowlo = lax.broadcasted_iota(jnp.int32, (L, LANES), 0) < HEAD_DIM
        lane = lax.broadcasted_iota(jnp.int32, (L, LANES), 1)
        lane_row = lax.broadcasted_iota(jnp.int32, (1, LANES), 1)
        dgm = jnp.zeros((L, L), F32)
        dcm = jnp.zeros((L, SSM_STATE), F32)
        dbm = jnp.zeros((L, SSM_STATE), F32)
        cols = jnp.zeros((L, LANES), F32)
        rows_ddt, rows_q, al_all, dcd_all = [], [], [], []
        for pr in range(2):
            k0 = 2 * pr
            xp = x_ref[:, pr * LANES:(pr + 1) * LANES]
            xpb = xp.astype(BF16)
            dyp = dy_ref[:, pr * LANES:(pr + 1) * LANES]
            dypb = dyp.astype(BF16)
            zero = jnp.zeros_like(dypb)
            hp = hs_ref[0, 0, pr * LANES:(pr + 1) * LANES, :]
            hpb = hp.astype(BF16)
            dst = dh_ref[pr * LANES:(pr + 1) * LANES, :]
            dstb = dst.astype(BF16)
            dxd, al = [], []
            for k in (k0, k0 + 1):
                sel = lo if k == k0 else nlo
                seg = ac[:, k:k + 1] - ar[k:k + 1, :]
                lam = jnp.exp(jnp.where(tril, seg, -jnp.inf))
                wk = gm * lam * dr[k:k + 1, :]
                dwk = _dot(jnp.where(sel, dypb, zero), xpb, 1, 1)
                mk = dwk * gm * lam
                qk = mk * dr[k:k + 1, :]
                dgm = dgm + dwk * lam * dr[k:k + 1, :]
                rows_ddt.append(jnp.sum(mk, axis=0, keepdims=True))
                rows_q.append(jnp.sum(qk, axis=0, keepdims=True))
                cols = jnp.where(lane == k, jnp.sum(qk, axis=-1, keepdims=True), cols)
                dxd.append(_dot(wk.astype(BF16), dypb, 0, 0))
                al.append(_last_lane(ar[k:k + 1, :]))
            al_all += al
            dxp = jnp.where(lo, dxd[0], dxd[1])
            e = jnp.exp(_pair_cols(ac, k0, lo))
            dye = dyp * e
            dyeb = dye.astype(BF16)
            dcm = dcm + _dot(dyeb, hpb, 1, 0)
            dh_yoff = _dot(dyeb, cb, 0, 0)
            tq = dye * _dot(cb, hpb, 1, 1)
            cols = jnp.where(lane == 4 + k0, jnp.sum(jnp.where(lo, tq, 0.0), axis=-1, keepdims=True), cols)
            cols = jnp.where(lane == 5 + k0, jnp.sum(jnp.where(lo, 0.0, tq), axis=-1, keepdims=True), cols)
            wp = jnp.where(lo, jnp.exp(al[0] - ac[:, k0:k0 + 1]) * dtc[:, k0:k0 + 1],
                           jnp.exp(al[1] - ac[:, k0 + 1:k0 + 2]) * dtc[:, k0 + 1:k0 + 2])
            dxw = _dot(bb, dstb, 1, 1)
            dxp = dxp + dxw * wp
            tw = xp * dxw
            cols = jnp.where(lane == 8 + k0, jnp.sum(jnp.where(lo, tw, 0.0), axis=-1, keepdims=True), cols)
            cols = jnp.where(lane == 9 + k0, jnp.sum(jnp.where(lo, 0.0, tw), axis=-1, keepdims=True), cols)
            dbm = dbm + _dot((xp * wp).astype(BF16), dstb, 1, 0)
            dsl = dsk[:, pr * LANES:(pr + 1) * LANES]
            dx_ref[:, pr * LANES:(pr + 1) * LANES] = dxp + dsl * dyp
            dd_ref[0, :, pr * LANES:(pr + 1) * LANES] += jnp.sum(dyp * xp, axis=0, keepdims=True)
            prod = dst * hp
            dcd_all.append(jnp.sum(jnp.sum(jnp.where(rowlo, prod, 0.0), axis=-1, keepdims=True), axis=0, keepdims=True))
            dcd_all.append(jnp.sum(jnp.sum(jnp.where(rowlo, 0.0, prod), axis=-1, keepdims=True), axis=0, keepdims=True))
            dec = jnp.where(rowlo, jnp.exp(al[0]), jnp.exp(al[1]))
            dh_ref[pr * LANES:(pr + 1) * LANES, :] = dec * dst + dh_yoff
        dgb = dgm.astype(BF16)
        dc_ref[...] = dcm + _dot(dgb, bb, 1, 0)
        db_ref[...] = dbm + _dot(dgb, cb, 0, 0)
        colt = cols.T
        sub8 = lax.broadcasted_iota(jnp.int32, (8, LANES), 0)
        da_rows = jnp.zeros((8, LANES), F32)
        ddt_part = []
        for k in range(4):
            rs = colt[k:k + 1, :]
            uo = colt[4 + k:5 + k, :]
            dwl = colt[8 + k:9 + k, :]
            es = jnp.exp(al_all[k] - ar[k:k + 1, :])
            wrow = es * dr[k:k + 1, :]
            dwl_w = dwl * wrow
            da_k = rs - rows_q[k] + uo - dwl_w
            tail = jnp.sum(dwl_w, axis=-1, keepdims=True) + jnp.exp(al_all[k]) * dcd_all[k]
            da_k = da_k + jnp.where(lane_row == L - 1, tail, 0.0)
            da_rows = jnp.where(sub8 == k, da_k, da_rows)
            ddt_part.append(rows_ddt[k] + dwl * es)
        dda = _dot(da_rows, _tri(True), 1, 0, HI)
        for k in range(4):
            dda_k = dda[k:k + 1, :]
            ddt_ref[0, k:k + 1, :] = ddt_part[k] + dda_k * ag[k:k + 1, :]
            da_ref[0, k:k + 1, :] += dda_k * dr[k:k + 1, :] * ag[k:k + 1, :]

    return _pc(body, name, (SSM_GROUPS, nc),
               [sp["x"], sp["b"], sp["c"], sp["col"], sp["col"], sp["row"], sp["row"], sp["grp"], sp["grow"], sp["hs"], sp["x"]],
               [sp["x"], sp["bc"], sp["bc"], sp["row"], sp["grow"], sp["grp"]],
               [_sds((s, SSM_INNER)), _sds((s, 1024)), _sds((s, 1024)), _sds((SSM_GROUPS, 4, s)),
                _sds((SSM_GROUPS, 4, LANES)), _sds((SSM_GROUPS, 1, 256))],
               [pltpu.VMEM((256, SSM_STATE), F32)])(xbc, xbc, xbc, dtc, acol, drow, arow, dskip, agrp, hs, dy)


def _gnorm_fwd(y, proj, nw, name):
    s = y.shape[0]
    ts = _row_tile(s)
    gw = SSM_INNER // SSM_GROUPS

    def body(y_ref, z_ref, w_ref, o_ref):
        for g in range(SSM_GROUPS):
            sl = slice(g * gw, (g + 1) * gw)
            z = z_ref[:, sl]
            y2 = y_ref[:, sl] * (z * _sigmoid(z))
            r = lax.rsqrt(jnp.mean(y2 * y2, axis=-1, keepdims=True) + RMS_EPS)
            o_ref[:, sl] = ((y2 * r) * w_ref[:, sl]).astype(BF16)

    row = pl.BlockSpec((ts, SSM_INNER), lambda i: (i, 0))
    return _pc(body, name, (s // ts,), [row, row, pl.BlockSpec((1, SSM_INNER), lambda i: (0, 0))], row,
               _sds((s, SSM_INNER), BF16))(y, proj, nw)


def _gnorm_bwd(y, proj, nw, dyn, name):
    s = y.shape[0]
    ts = _row_tile(s)
    gw = SSM_INNER // SSM_GROUPS

    def body(y_ref, z_ref, w_ref, d_ref, dy_ref, dz_ref, dw_ref):
        @pl.when(pl.program_id(0) == 0)
        def _():
            dw_ref[...] = jnp.zeros_like(dw_ref)

        for g in range(SSM_GROUPS):
            sl = slice(g * gw, (g + 1) * gw)
            z = z_ref[:, sl]
            yv = y_ref[:, sl]
            sg = _sigmoid(z)
            sz = z * sg
            y2 = yv * sz
            r = lax.rsqrt(jnp.mean(y2 * y2, axis=-1, keepdims=True) + RMS_EPS)
            yn = y2 * r
            dout = d_ref[:, sl]
            dyg = dout * w_ref[:, sl]
            dy2 = r * (dyg - yn * jnp.mean(dyg * yn, axis=-1, keepdims=True))
            dy_ref[:, sl] = dy2 * sz
            dz_ref[:, sl] = (dy2 * yv * (sg * (1.0 + z * (1.0 - sg)))).astype(BF16)
            dw_ref[:, sl] += jnp.sum(dout * yn, axis=0, keepdims=True)

    row = pl.BlockSpec((ts, SSM_INNER), lambda i: (i, 0))
    vec = pl.BlockSpec((1, SSM_INNER), lambda i: (0, 0))
    return _pc(body, name, (s // ts,), [row, row, vec, row], [row, row, vec],
               [_sds((s, SSM_INNER)), _sds((s, SSM_INNER), BF16), _sds((1, SSM_INNER))])(y, proj, nw, dyn)


def _head_layouts(v, s):
    return v.reshape(s, SSM_GROUPS, 4).transpose(1, 0, 2), v.T.reshape(SSM_GROUPS, 4, s)


def _ssd_fwd(x, norm_w, w_in, conv_w, conv_b, dt_bias, a_log, d_skip, gnorm_w, w_out):
    s = x.shape[0]
    h = _rms_fwd(x, norm_w, "ssd_norm")
    proj = _mm(h, w_in, "nn", "ssd_in")
    xbc = _ssd_conv_fwd(proj, conv_w, conv_b, "ssd_conv")
    pad = ((0, 0), (0, LANES - SSM_HEADS))
    a_neg = -jnp.exp(a_log)
    bias = jnp.pad(dt_bias, pad)
    dt, acum = _ssd_dt_fwd(proj, bias, jnp.pad(a_neg, pad), "ssd_dt")
    dtc, drow = _head_layouts(dt[:, :SSM_HEADS], s)
    acol, arow = _head_layouts(acum[:, :SSM_HEADS], s)
    dskip = jnp.repeat(d_skip.reshape(SSM_GROUPS, 1, 4), HEAD_DIM, axis=2)
    y, hs = _ssd_scan_fwd(xbc, dtc, acol, drow, arow, dskip, "ssd_scan")
    yn = _gnorm_fwd(y, proj, gnorm_w, "ssd_gnorm")
    out = _mm(yn, w_out, "nn", "ssd_out", res=x)
    return out, (x, h, proj, xbc, bias, a_neg, dtc, acol, drow, arow, dskip, y, hs, yn)


def _ssd_bwd(dout, saved, norm_w, w_in, conv_w, conv_b, gnorm_w, w_out):
    x, h, proj, xbc, bias, a_neg, dtc, acol, drow, arow, dskip, y, hs, yn = saved
    s = x.shape[0]
    dyn = _mm(dout, w_out, "nt", "ssd_dyn")
    g_out = _mm(yn, dout, "tn", "ssd_gout", out_dtype=BF16)
    dy, dz, g_gnorm = _gnorm_bwd(y, proj, gnorm_w, dyn, "ssd_dgnorm")
    agrp = jnp.broadcast_to(a_neg.reshape(SSM_GROUPS, 4, 1), (SSM_GROUPS, 4, LANES))
    dxs, db, dc, ddt_row, da_acc, dd_acc = _ssd_scan_bwd(xbc, dtc, acol, drow, arow, dskip, agrp, hs, dy, "ssd_dscan")
    dxbc = jnp.concatenate([dxs, db, dc], axis=1)
    gact, g_cb = _ssd_conv_bwd_act(proj, dxbc, conv_w, conv_b, "ssd_dconv_act")
    dpre, g_cw = _ssd_conv_bwd_in(proj, gact, conv_w, "ssd_dconv_in")
    ddt = jnp.pad(ddt_row.reshape(SSM_HEADS, s).T, ((0, 0), (0, LANES - SSM_HEADS)))
    ddtraw, g_dtb = _ssd_dt_bwd(proj, bias, ddt, "ssd_ddt")
    dproj = jnp.concatenate([dz, dpre, ddtraw], axis=1)
    g_in = _mm(h, dproj, "tn", "ssd_gin", out_dtype=BF16)
    dh = _mm(dproj, w_in, "nt", "ssd_dh")
    dx, g_norm = _rms_bwd(x, norm_w, dh, dout, "ssd_dnorm")
    g_alog = jnp.sum(da_acc, axis=-1).reshape(1, SSM_HEADS)
    g_d = jnp.sum(dd_acc.reshape(SSM_GROUPS, 4, HEAD_DIM), axis=-1).reshape(1, SSM_HEADS)
    return dx, g_norm, g_in[:, :SSM_IN], g_cw, g_cb, g_dtb[:, :SSM_HEADS], g_alog, g_d, g_gnorm, g_out


def _loss_head(y, target, name):
    s, d = y.shape
    ts = _row_tile(s)

    def body(y_ref, t_ref, dy_ref, l_ref):
        @pl.when(pl.program_id(0) == 0)
        def _():
            l_ref[...] = jnp.zeros_like(l_ref)

        e = y_ref[...] - t_ref[...]
        dy_ref[...] = e * (1.0 / d)
        part = jnp.sum(jnp.sum(e * e, axis=-1, keepdims=True), axis=0, keepdims=True) * (0.5 / d)
        l_ref[...] += jnp.broadcast_to(part, l_ref.shape)

    row = pl.BlockSpec((ts, d), lambda i: (i, 0))
    dy, lacc = _pc(body, name, (s // ts,), [row, row], [row, pl.BlockSpec((8, LANES), lambda i: (0, 0))],
                   [_sds((s, d)), _sds((8, LANES))])(y, target)
    return lacc[0, 0], dy


def _local_step(x, target, w, gather_rest=None, scatter_first=None):
    saved = []
    received = None
    for i in range(DEPTH):
        kind, j = i % 3, i // 3
        mn = w["mix_norm"][i:i + 1]
        if kind == 0:
            x, sv = _conv_fwd(x, mn, w["conv_w_in"][j], w["conv_w_dw"][j], w["conv_w_out"][j], str(i))
        elif kind == 1:
            hosted = None if gather_rest is None else (gather_rest[0], True)
            x, sv, got = _fox_fwd(x, mn, w["fox_w_in"], w["fox_b_f"], w["fox_q_gain"], w["fox_k_gain"], w["fox_w_out"], hosted)
            if gather_rest is not None:
                w = gather_rest[1](w, got)
        else:
            x, sv = _ssd_fwd(x, mn, w["ssd_w_in"], w["ssd_conv_w"], w["ssd_conv_b"], w["ssd_dt_bias"],
                             w["ssd_a_log"], w["ssd_d"], w["ssd_norm_w"], w["ssd_w_out"])
        x, sf = _ffn_fwd(x, w["ffn_norm"][i:i + 1], w["ffn_w_gu"][i], w["ffn_w_down"][i], str(i))
        saved.append((sv, sf))
    loss, dx = _loss_head(x, target, "loss_head")
    g = {k: [None] * n for k, n in (("mix_norm", DEPTH), ("ffn_norm", DEPTH), ("ffn_w_gu", DEPTH), ("ffn_w_down", DEPTH),
                                    ("conv_w_in", 2), ("conv_w_dw", 2), ("conv_w_out", 2))}
    for i in reversed(range(DEPTH)):
        kind, j = i % 3, i // 3
        sv, sf = saved[i]
        dx, g["ffn_norm"][i], g["ffn_w_gu"][i], g["ffn_w_down"][i] = _ffn_bwd(
            dx, sf, w["ffn_norm"][i:i + 1], w["ffn_w_gu"][i], w["ffn_w_down"][i], str(i))
        mn = w["mix_norm"][i:i + 1]
        if kind == 0:
            dx, g["mix_norm"][i], g["conv_w_in"][j], g["conv_w_dw"][j], g["conv_w_out"][j] = _conv_bwd(
                dx, sv, mn, w["conv_w_in"][j], w["conv_w_dw"][j], w["conv_w_out"][j], str(i))
        elif kind == 1:
            hosted = None if scatter_first is None else (scatter_first(g), False)
            (dx, g["mix_norm"][i], g["fox_w_in"], g["fox_b_f"], g["fox_q_gain"], g["fox_k_gain"],
             g["fox_w_out"], received) = _fox_bwd(dx, sv, mn, w["fox_w_in"], w["fox_w_out"], hosted)
        else:
            (dx, g["mix_norm"][i], g["ssd_w_in"], g["ssd_conv_w"], g["ssd_conv_b"], g["ssd_dt_bias"], g["ssd_a_log"],
             g["ssd_d"], g["ssd_norm_w"], g["ssd_w_out"]) = _ssd_bwd(
                 dx, sv, mn, w["ssd_w_in"], w["ssd_conv_w"], w["ssd_conv_b"], w["ssd_norm_w"], w["ssd_w_out"])
    g["mix_norm"] = jnp.concatenate(g["mix_norm"], axis=0)
    g["ffn_norm"] = jnp.concatenate(g["ffn_norm"], axis=0)
    g["conv_w_dw"] = jnp.stack(g["conv_w_dw"], axis=0)
    g["ssd_conv_w"] = g["ssd_conv_w"][None]
    return loss, dx, g, received


def _mesh_position():
    return lax.axis_index("x") * 4 + lax.axis_index("y") * 2 + lax.axis_index("c")


def _device_of(t):
    return (lax.shift_right_logical(t, 2), lax.bitwise_and(lax.shift_right_logical(t, 1), 1), lax.bitwise_and(t, 1))


def _exchange_copies(srcs_of, out_refs, send_sems, recv_sems, local_sems):
    me = _mesh_position()
    na = len(out_refs)
    locals_ = [pltpu.make_async_copy(srcs_of(a, me), out_refs[a].at[me], local_sems.at[a]) for a in range(na)]
    sends, arrivals = [], []
    for j in range(1, NDEV):
        t = lax.rem(me + j, NDEV)
        frm = lax.rem(me + NDEV - j, NDEV)
        for a in range(na):
            sends.append(pltpu.make_async_remote_copy(
                src_ref=srcs_of(a, t), dst_ref=out_refs[a].at[me], send_sem=send_sems.at[a, j - 1],
                recv_sem=recv_sems.at[a, j - 1], device_id=_device_of(t), device_id_type=pl.DeviceIdType.MESH))
            arrivals.append(pltpu.make_async_remote_copy(
                src_ref=srcs_of(a, me), dst_ref=out_refs[a].at[frm], send_sem=send_sems.at[a, j - 1],
                recv_sem=recv_sems.at[a, j - 1], device_id=_device_of(frm), device_id_type=pl.DeviceIdType.MESH))
    return locals_, sends, arrivals


def _exchange_start(copies):
    locals_, sends, _ = copies
    for cp in locals_ + sends:
        cp.start()


def _exchange_wait(copies):
    locals_, sends, arrivals = copies
    for cp in sends:
        cp.wait_send()
    for cp in arrivals:
        cp.wait_recv()
    for cp in locals_:
        cp.wait()


def _exchange_run(srcs_of, out_refs, send_sems, recv_sems, local_sems):
    copies = _exchange_copies(srcs_of, out_refs, send_sems, recv_sems, local_sems)
    _exchange_start(copies)
    _exchange_wait(copies)


def _exchange_parts(arrays, gather):
    na = len(arrays)
    outs = [_sds(((NDEV,) + a.shape) if gather else a.shape, a.dtype) for a in arrays]
    sems = [pltpu.SemaphoreType.DMA((na, NDEV - 1)), pltpu.SemaphoreType.DMA((na, NDEV - 1)), pltpu.SemaphoreType.DMA((na,))]
    pick = (lambda srcs: (lambda a, t: srcs[a])) if gather else (lambda srcs: (lambda a, t: srcs[a].at[t]))
    return pick, outs, sems


def _exchange(arrays, name, gather):
    na = len(arrays)
    pick, outs, sems = _exchange_parts(arrays, gather)

    def body(*refs):
        _exchange_run(pick(refs[:na]), refs[na:2 * na], *refs[2 * na:])

    hbm = pl.BlockSpec(memory_space=pl.ANY)
    return pl.pallas_call(body, name=name, in_specs=[hbm] * na, out_specs=[hbm] * na, out_shape=outs, scratch_shapes=sems)(*arrays)


def _all_sum_small(pack, name):
    def body(src_ref, out_ref, buf_ref, send_sems, recv_sems, local_sems):
        _exchange_run(lambda a, t: src_ref, [buf_ref], send_sems, recv_sems, local_sems)
        acc = buf_ref[0]
        for d in range(1, NDEV):
            acc = acc + buf_ref[d]
        out_ref[...] = acc

    vmem = pl.BlockSpec(memory_space=pltpu.VMEM)
    return pl.pallas_call(
        body, name=name, in_specs=[vmem], out_specs=vmem, out_shape=_sds(pack.shape, pack.dtype),
        scratch_shapes=[pltpu.VMEM((NDEV,) + pack.shape, pack.dtype), pltpu.SemaphoreType.DMA((1, NDEV - 1)),
                        pltpu.SemaphoreType.DMA((1, NDEV - 1)), pltpu.SemaphoreType.DMA((1,))])(pack)


def _sum_slabs(slabs, name):
    _, r, c = slabs.shape
    tr = r
    for cand in (256, 352):
        if r % cand == 0:
            tr = cand
            break

    def body(s_ref, o_ref):
        acc = s_ref[0].astype(F32)
        for d in range(1, NDEV):
            acc = acc + s_ref[d].astype(F32)
        o_ref[...] = acc

    return _pc(body, name, (r // tr,), [pl.BlockSpec((NDEV, tr, c), lambda i: (0, i, 0))],
               pl.BlockSpec((tr, c), lambda i: (i, 0)), _sds((r, c)))(slabs)


def _adamw(wt, g, m, v, name):
    shape = wt.shape
    w2, g2, m2, v2 = (a.reshape(-1, shape[-1]) for a in (wt, g, m, v))
    r, c = w2.shape
    tr = r
    for cand in (512, 352, 256):
        if r % cand == 0:
            tr = cand
            break
    c1 = 1.0 - ADAM_B1 ** ADAM_STEP
    c2 = 1.0 - ADAM_B2 ** ADAM_STEP

    def body(w_ref, g_ref, m_ref, v_ref, d_ref, mo_ref, vo_ref):
        gv = g_ref[...]
        mn = ADAM_B1 * m_ref[...] + (1.0 - ADAM_B1) * gv
        vn = ADAM_B2 * v_ref[...] + (1.0 - ADAM_B2) * (gv * gv)
        mo_ref[...] = mn
        vo_ref[...] = vn
        d_ref[...] = -ADAM_LR * ((mn / c1) / (jnp.sqrt(vn / c2) + ADAM_EPS) + ADAM_WD * w_ref[...])

    spec = pl.BlockSpec((tr, c), lambda i: (i, 0))
    outs = _pc(body, name, (r // tr,), [spec] * 4, [spec] * 3, [_sds((r, c))] * 3)(w2, g2, m2, v2)
    return tuple(o.reshape(shape) for o in outs)


_NAMES = ["mix_norm", "ffn_norm", "ffn_w_gu", "ffn_w_down", "conv_w_in", "conv_w_dw", "conv_w_out", "fox_w_in", "fox_b_f",
          "fox_q_gain", "fox_k_gain", "fox_w_out", "ssd_w_in", "ssd_conv_w", "ssd_conv_b", "ssd_dt_bias", "ssd_a_log",
          "ssd_d", "ssd_norm_w", "ssd_w_out"]
_MATRICES = ["ffn_w_gu", "ffn_w_down", "conv_w_in", "conv_w_out", "fox_w_in", "fox_w_out", "ssd_w_in", "ssd_w_out"]
_VECTORS = {"conv_w_dw": 2, "ssd_conv_w": 2, "ssd_conv_b": 1, "ssd_norm_w": 1}
_REPLICATED = ["mix_norm", "ffn_norm", "fox_b_f", "fox_q_gain", "fox_k_gain", "ssd_dt_bias", "ssd_a_log", "ssd_d"]


def _to_rows(flat):
    n = flat.shape[0]
    rows = -(-n // (8 * D_MODEL)) * 8
    return jnp.pad(flat, (0, rows * D_MODEL - n)).reshape(rows, D_MODEL)


def _full_shape(local_shape, axis):
    shp = list(local_shape)
    shp[axis] *= NDEV
    return tuple(shp)


def _cols_from_blocks(g):
    return jnp.moveaxis(g, 0, 1).reshape(g.shape[1], NDEV * g.shape[2])


def _blocks_from_cols(full):
    k, n8 = full.shape
    return jnp.moveaxis(full.reshape(k, NDEV, n8 // NDEV), 1, 0)


def kernel(x, mix_norm, ffn_norm, ffn_w_gu, ffn_w_down, conv_w_in, conv_w_dw, conv_w_out, fox_w_in, fox_b_f, fox_q_gain, fox_k_gain, fox_w_out, ssd_w_in, ssd_conv_w, ssd_conv_b, ssd_dt_bias, ssd_a_log, ssd_d, ssd_norm_w, ssd_w_out, loss_target, m_mix_norm, m_ffn_norm, m_ffn_w_gu, m_ffn_w_down, m_conv_w_in, m_conv_w_dw, m_conv_w_out, m_fox_w_in, m_fox_b_f, m_fox_q_gain, m_fox_k_gain, m_fox_w_out, m_ssd_w_in, m_ssd_conv_w, m_ssd_conv_b, m_ssd_dt_bias, m_ssd_a_log, m_ssd_d, m_ssd_norm_w, m_ssd_w_out, v_mix_norm, v_ffn_norm, v_ffn_w_gu, v_ffn_w_down, v_conv_w_in, v_conv_w_dw, v_conv_w_out, v_fox_w_in, v_fox_b_f, v_fox_q_gain, v_fox_k_gain, v_fox_w_out, v_ssd_w_in, v_ssd_conv_w, v_ssd_conv_b, v_ssd_dt_bias, v_ssd_a_log, v_ssd_d, v_ssd_norm_w, v_ssd_w_out):
    local = dict(mix_norm=mix_norm, ffn_norm=ffn_norm, ffn_w_gu=ffn_w_gu, ffn_w_down=ffn_w_down, conv_w_in=conv_w_in,
                 conv_w_dw=conv_w_dw, conv_w_out=conv_w_out, fox_w_in=fox_w_in, fox_b_f=fox_b_f, fox_q_gain=fox_q_gain,
                 fox_k_gain=fox_k_gain, fox_w_out=fox_w_out, ssd_w_in=ssd_w_in, ssd_conv_w=ssd_conv_w, ssd_conv_b=ssd_conv_b,
                 ssd_dt_bias=ssd_dt_bias, ssd_a_log=ssd_a_log, ssd_d=ssd_d, ssd_norm_w=ssd_norm_w, ssd_w_out=ssd_w_out)
    mom = dict(zip(_NAMES, [m_mix_norm, m_ffn_norm, m_ffn_w_gu, m_ffn_w_down, m_conv_w_in, m_conv_w_dw, m_conv_w_out, m_fox_w_in,
                            m_fox_b_f, m_fox_q_gain, m_fox_k_gain, m_fox_w_out, m_ssd_w_in, m_ssd_conv_w, m_ssd_conv_b,
                            m_ssd_dt_bias, m_ssd_a_log, m_ssd_d, m_ssd_norm_w, m_ssd_w_out]))
    var = dict(zip(_NAMES, [v_mix_norm, v_ffn_norm, v_ffn_w_gu, v_ffn_w_down, v_conv_w_in, v_conv_w_dw, v_conv_w_out, v_fox_w_in,
                            v_fox_b_f, v_fox_q_gain, v_fox_k_gain, v_fox_w_out, v_ssd_w_in, v_ssd_conv_w, v_ssd_conv_b,
                            v_ssd_dt_bias, v_ssd_a_log, v_ssd_d, v_ssd_norm_w, v_ssd_w_out]))

    shard = {k: local[k].astype(BF16) for k in _MATRICES}
    vec_pack = _to_rows(jnp.concatenate([local[k].reshape(-1) for k in _VECTORS]))
    first = _exchange([shard["ffn_w_gu"][0:1], shard["ffn_w_down"][0:1], shard["conv_w_in"][0:1], shard["conv_w_out"][0:1],
                       shard["fox_w_in"], shard["fox_w_out"], vec_pack], "gather_first", True)
    gvec = first[6].reshape(NDEV, -1)
    full = {k: local[k] for k in _REPLICATED}
    off = 0
    for k, axis in _VECTORS.items():
        n = local[k].size
        blk = jnp.moveaxis(gvec[:, off:off + n].reshape((NDEV,) + local[k].shape), 0, axis)
        full[k] = blk.reshape(_full_shape(local[k].shape, axis))
        off += n
    full["ssd_conv_w"] = full["ssd_conv_w"][0]
    full["ffn_w_gu"] = [first[0][:, 0]]
    full["ffn_w_down"] = [first[1][:, 0].reshape(4, FF_BLOCK, D_MODEL)]
    full["conv_w_in"] = [first[2][:, 0]]
    full["conv_w_out"] = [first[3][:, 0].reshape(D_MODEL, D_MODEL)]
    full["fox_w_in"] = jnp.pad(_cols_from_blocks(first[4][:, 0]), ((0, 0), (0, FOX_IN_PAD - FOX_IN)))
    full["fox_w_out"] = first[5].reshape(D_MODEL, D_MODEL)

    rest = [shard["ffn_w_gu"][1:], shard["ffn_w_down"][1:], shard["conv_w_in"][1:], shard["conv_w_out"][1:],
            shard["ssd_w_in"], shard["ssd_w_out"]]

    def finish(w, got):
        w = dict(w)
        w["ffn_w_gu"] = w["ffn_w_gu"] + [got[0][:, i] for i in range(DEPTH - 1)]
        w["ffn_w_down"] = w["ffn_w_down"] + [got[1][:, i].reshape(4, FF_BLOCK, D_MODEL) for i in range(DEPTH - 1)]
        w["conv_w_in"] = w["conv_w_in"] + [got[2][:, 0]]
        w["conv_w_out"] = w["conv_w_out"] + [got[3][:, 0].reshape(D_MODEL, D_MODEL)]
        w["ssd_w_in"] = jnp.pad(_cols_from_blocks(got[4][:, 0]), ((0, 0), (0, SSM_IN_PAD - SSM_IN)))
        w["ssd_w_out"] = got[5].reshape(SSM_INNER, D_MODEL)
        return w

    def early_slabs(g):
        return ([g["ffn_w_gu"][i] for i in range(1, DEPTH)]
                + [g["ffn_w_down"][i].reshape(NDEV, D_FF // NDEV, D_MODEL) for i in range(1, DEPTH)]
                + [g["conv_w_in"][1], g["conv_w_out"][1].reshape(NDEV, D_MODEL // NDEV, D_MODEL),
                   _blocks_from_cols(g["ssd_w_in"]), g["ssd_w_out"].reshape(NDEV, SSM_INNER // NDEV, D_MODEL)])

    loss_part, dx, grads, early = _local_step(x[0], loss_target[0], full, (rest, finish), early_slabs)

    late = _exchange([grads["ffn_w_gu"][0], grads["ffn_w_down"][0].reshape(NDEV, D_FF // NDEV, D_MODEL), grads["conv_w_in"][0],
                      grads["conv_w_out"][0].reshape(NDEV, D_MODEL // NDEV, D_MODEL), _blocks_from_cols(grads["fox_w_in"]),
                      grads["fox_w_out"].reshape(NDEV, D_MODEL // NDEV, D_MODEL)], "scatter_last", False)
    se = [_sum_slabs(r, f"sum_early_{n}") for n, r in enumerate(early)]
    sl = [_sum_slabs(r, f"sum_late_{n}") for n, r in enumerate(late)]
    shard_grad = {
        "ffn_w_gu": jnp.stack([sl[0]] + se[0:3]), "ffn_w_down": jnp.stack([sl[1]] + se[3:6]),
        "conv_w_in": jnp.stack([sl[2], se[6]]), "conv_w_out": jnp.stack([sl[3], se[7]]),
        "fox_w_in": sl[4][None], "fox_w_out": sl[5][None], "ssd_w_in": se[8][None], "ssd_w_out": se[9][None]}

    small_names = _REPLICATED + list(_VECTORS)
    small = [jnp.reshape(loss_part, (1,))] + [grads[k].reshape(-1) for k in small_names]
    total = _all_sum_small(_to_rows(jnp.concatenate(small)), "sum_small").reshape(-1)
    loss = total[0]
    off = 1
    me = _mesh_position()
    for k, part in zip(small_names, small[1:]):
        gk = total[off:off + part.shape[0]]
        off += part.shape[0]
        if k in _VECTORS:
            axis = _VECTORS[k]
            shp = local[k].shape
            gfull = gk.reshape(shp[:axis] + (NDEV, shp[axis]) + shp[axis + 1:])
            shard_grad[k] = lax.dynamic_index_in_dim(gfull, me, axis, keepdims=False)
        else:
            shard_grad[k] = gk.reshape(local[k].shape)

    deltas, new_m, new_v = {}, {}, {}
    for k in _NAMES:
        deltas[k], new_m[k], new_v[k] = _adamw(local[k], shard_grad[k], mom[k], var[k], f"adamw_{k}")
    return (loss, dx[None], *[shard_grad[k] for k in _NAMES], *[deltas[k] for k in _NAMES],
            *[new_m[k] for k in _NAMES], *[new_v[k] for k in _NAMES])
```

```python
import numpy as np

import jax
import jax.numpy as jnp
from jax import lax
from jax.experimental import pallas as pl
from jax.experimental.pallas import tpu as pltpu

F32 = jnp.float32
BF16 = jnp.bfloat16
HI = lax.Precision.HIGHEST

NDEV = 8
D_MODEL = 1024
DEPTH = 4
D_FF = 2816
FF_BLOCK = 2 * D_FF // NDEV
CONV_BLOCK = 3 * D_MODEL // NDEV
RMS_EPS = 1e-6
HEAD_DIM = 64
ATTN_HEADS = 16
FOX_IN = 3 * D_MODEL + ATTN_HEADS
FOX_IN_PAD = 3200
SSM_INNER = 2048
SSM_HEADS = 32
SSM_GROUPS = 8
SSM_STATE = 128
SSM_CHUNK = 128
SSM_CONV_DIM = 4096
SSM_IN = SSM_INNER + SSM_CONV_DIM + SSM_HEADS
SSM_IN_PAD = 6272
LANES = 128
V7X_VMEM_BYTES = 64 * 1024 * 1024
VMEM_LIMIT_BYTES = (V7X_VMEM_BYTES * 3) // 4
LOG2E = 1.4426950408889634
LN2 = 0.6931471805599453
ATTN_TILE = 1024
ATTN_ROWS = 32

ADAM_LR = 0.001
ADAM_B1 = 0.9
ADAM_B2 = 0.999
ADAM_EPS = 1e-08
ADAM_WD = 0.01
ADAM_STEP = 10

_TILE_CANDIDATES = (1024, 1408, 896, 768, 640, 512, 384, 256, 128)


def _pick_tile(n):
    for c in _TILE_CANDIDATES:
        if n % c == 0:
            return c
    raise ValueError(f"no tile for {n}")


def _params(ngrid):
    return pltpu.CompilerParams(dimension_semantics=("arbitrary",) * ngrid, vmem_limit_bytes=VMEM_LIMIT_BYTES)


def _pc(body, name, grid, in_specs, out_specs, out_shape, scratch=()):
    return pl.pallas_call(
        body, name=name, grid=grid, in_specs=in_specs, out_specs=out_specs, out_shape=out_shape,
        scratch_shapes=list(scratch), compiler_params=_params(len(grid)))


def _dot(a, b, ca, cb, prec=None):
    return lax.dot_general(a, b, (((ca,), (cb,)), ((), ())), preferred_element_type=F32, precision=prec)


def _sds(shape, dtype=F32):
    return jax.ShapeDtypeStruct(shape, dtype)


def _row_tile(s, want=256):
    return want if s % want == 0 else s


def _sigmoid(x):
    return 1.0 / (1.0 + jnp.exp(-x))


def _softplus(x):
    return jnp.maximum(x, 0.0) + jnp.log(1.0 + jnp.exp(-jnp.abs(x)))


def _mm_spec(a, b, name, grid, a_spec, b_spec, o_spec, out, ca, cb, acc_shape, drop=(0, 0, 0), res=None, r_spec=None):
    nk = grid[2]
    da, db, do_ = drop
    has_res = res is not None

    def body(*refs):
        if has_res:
            a_ref, b_ref, r_ref, o_ref, acc_ref = refs
        else:
            a_ref, b_ref, o_ref, acc_ref = refs
        k = pl.program_id(2)

        @pl.when(k == 0)
        def _():
            acc_ref[...] = jnp.zeros_like(acc_ref)

        av = a_ref[(0,) * da] if da else a_ref[...]
        bv = b_ref[(0,) * db] if db else b_ref[...]
        acc_ref[...] += _dot(av.astype(BF16), bv.astype(BF16), ca, cb)

        @pl.when(k == nk - 1)
        def _():
            val = acc_ref[...]
            if has_res:
                val = val + r_ref[...]
            if do_:
                o_ref[(0,) * do_] = val.astype(out.dtype)
            else:
                o_ref[...] = val.astype(out.dtype)

    in_specs = [a_spec, b_spec] + ([r_spec] if has_res else [])
    args = (a, b) + ((res,) if has_res else ())
    return _pc(body, name, grid, in_specs, o_spec, out, [pltpu.VMEM(acc_shape, F32)])(*args)


def _mm(a, b, mode, name, out_dtype=F32, res=None):
    if mode == "tn":
        r, m = a.shape
        n = b.shape[1]
        tm, tn, tk = _pick_tile(m), _pick_tile(n), _pick_tile(r)
        grid = (m // tm, n // tn, r // tk)
        a_spec = pl.BlockSpec((tk, tm), lambda i, j, k: (k, i))
        b_spec = pl.BlockSpec((tk, tn), lambda i, j, k: (k, j))
        ca, cb = 0, 0
    else:
        m, kd = a.shape
        n = b.shape[1] if mode == "nn" else b.shape[0]
        tm, tn, tk = _pick_tile(m), _pick_tile(n), _pick_tile(kd)
        grid = (m // tm, n // tn, kd // tk)
        a_spec = pl.BlockSpec((tm, tk), lambda i, j, k: (i, k))
        if mode == "nn":
            b_spec = pl.BlockSpec((tk, tn), lambda i, j, k: (k, j))
            ca, cb = 1, 0
        else:
            b_spec = pl.BlockSpec((tn, tk), lambda i, j, k: (j, k))
            ca, cb = 1, 1
    o_spec = pl.BlockSpec((tm, tn), lambda i, j, k: (i, j))
    return _mm_spec(a, b, name, grid, a_spec, b_spec, o_spec, _sds((m, n), out_dtype), ca, cb, (tm, tn), res=res, r_spec=o_spec)


def _rms_fwd(x, w, name):
    s, d = x.shape
    ts = _row_tile(s)

    def body(x_ref, w_ref, o_ref):
        xv = x_ref[...]
        r = lax.rsqrt(jnp.mean(xv * xv, axis=-1, keepdims=True) + RMS_EPS)
        o_ref[...] = ((xv * r) * w_ref[...]).astype(BF16)

    row = pl.BlockSpec((ts, d), lambda i: (i, 0))
    return _pc(body, name, (s // ts,), [row, pl.BlockSpec((1, d), lambda i: (0, 0))], row, _sds((s, d), BF16))(x, w)


def _rms_bwd(x, w, dh, dres, name):
    s, d = x.shape
    ts = _row_tile(s)

    def body(x_ref, w_ref, dh_ref, dr_ref, dx_ref, dw_ref):
        i = pl.program_id(0)
        xv = x_ref[...]
        r = lax.rsqrt(jnp.mean(xv * xv, axis=-1, keepdims=True) + RMS_EPS)
        xhat = xv * r
        dhv = dh_ref[...]
        g = dhv * w_ref[...]
        dx_ref[...] = dr_ref[...] + r * (g - xhat * jnp.mean(g * xhat, axis=-1, keepdims=True))

        @pl.when(i == 0)
        def _():
            dw_ref[...] = jnp.zeros_like(dw_ref)

        dw_ref[...] += jnp.sum(dhv * xhat, axis=0, keepdims=True)

    row = pl.BlockSpec((ts, d), lambda i: (i, 0))
    vec = pl.BlockSpec((1, d), lambda i: (0, 0))
    return _pc(body, name, (s // ts,), [row, vec, row, row], [row, vec], [_sds((s, d)), _sds((1, d))])(x, w, dh, dres)


def _ffn_gate_up(h, w_gu, name):
    s = h.shape[0]
    tm = _pick_tile(s)

    def body(h_ref, wg_ref, wu_ref, gu_ref, a_ref):
        hv = h_ref[...]
        g = _dot(hv, wg_ref[0], 1, 0)
        u = _dot(hv, wu_ref[0], 1, 0)
        gu_ref[0, 0] = g.astype(BF16)
        gu_ref[0, 1] = u.astype(BF16)
        a_ref[0] = (g * _sigmoid(g) * u).astype(BF16)

    wblk = lambda off: pl.BlockSpec((1, D_MODEL, FF_BLOCK), lambda i, k: (k + off, 0, 0))
    return _pc(body, name, (s // tm, 4), [pl.BlockSpec((tm, D_MODEL), lambda i, k: (i, 0)), wblk(0), wblk(4)],
               [pl.BlockSpec((1, 2, tm, FF_BLOCK), lambda i, k: (k, 0, i, 0)), pl.BlockSpec((1, tm, FF_BLOCK), lambda i, k: (k, i, 0))],
               [_sds((4, 2, s, FF_BLOCK), BF16), _sds((4, s, FF_BLOCK), BF16)])(h, w_gu, w_gu)


def _ffn_dgate_up(dy, w_down, gu, name):
    s = dy.shape[0]
    tm = _pick_tile(s)

    def body(dy_ref, w_ref, gu_ref, o_ref):
        dav = _dot(dy_ref[...].astype(BF16), w_ref[0], 1, 1)
        g = gu_ref[0, 0].astype(F32)
        u = gu_ref[0, 1].astype(F32)
        sg = _sigmoid(g)
        o_ref[0, 0] = (dav * u * (sg * (1.0 + g * (1.0 - sg)))).astype(BF16)
        o_ref[0, 1] = (dav * (g * sg)).astype(BF16)

    pair = pl.BlockSpec((1, 2, tm, FF_BLOCK), lambda i, k: (k, 0, i, 0))
    return _pc(body, name, (s // tm, 4),
               [pl.BlockSpec((tm, D_MODEL), lambda i, k: (i, 0)), pl.BlockSpec((1, FF_BLOCK, D_MODEL), lambda i, k: (k, 0, 0)), pair],
               pair, _sds((4, 2, s, FF_BLOCK), BF16))(dy, w_down, gu)


def _ffn_fwd(x, norm_w, w_gu, w_down, tag):
    s = x.shape[0]
    tm = _pick_tile(s)
    h = _rms_fwd(x, norm_w, f"ffn_norm_{tag}")
    gu, a = _ffn_gate_up(h, w_gu, f"ffn_gu_{tag}")
    xspec = pl.BlockSpec((tm, D_MODEL), lambda i, j, k: (i, 0))
    y = _mm_spec(a, w_down, f"ffn_down_{tag}", (s // tm, 1, 4),
                 pl.BlockSpec((1, tm, FF_BLOCK), lambda i, j, k: (k, i, 0)),
                 pl.BlockSpec((1, FF_BLOCK, D_MODEL), lambda i, j, k: (k, 0, 0)),
                 xspec, _sds((s, D_MODEL)), 1, 0, (tm, D_MODEL), drop=(1, 1, 0), res=x, r_spec=xspec)
    return y, (x, h, gu, a)


def _ffn_bwd(dy, saved, norm_w, w_gu, w_down, tag):
    x, h, gu, a = saved
    s = x.shape[0]
    tm = _pick_tile(s)
    row = pl.BlockSpec((tm, D_MODEL), lambda i, j, k: (i, 0))
    g_down = _mm_spec(a, dy, f"ffn_gdown_{tag}", (4, 1, s // tm),
                      pl.BlockSpec((1, tm, FF_BLOCK), lambda i, j, k: (i, k, 0)),
                      pl.BlockSpec((tm, D_MODEL), lambda i, j, k: (k, 0)),
                      pl.BlockSpec((1, FF_BLOCK, D_MODEL), lambda i, j, k: (i, 0, 0)),
                      _sds((4, FF_BLOCK, D_MODEL), BF16), 0, 0, (FF_BLOCK, D_MODEL), drop=(1, 0, 1))
    dgu = _ffn_dgate_up(dy, w_down, gu, f"ffn_dgu_{tag}")
    g_gu = _mm_spec(h, dgu, f"ffn_ggu_{tag}", (NDEV, 1, s // tm),
                    pl.BlockSpec((tm, D_MODEL), lambda i, j, k: (k, 0)),
                    pl.BlockSpec((1, 1, tm, FF_BLOCK), lambda i, j, k: (i % 4, i // 4, k, 0)),
                    pl.BlockSpec((1, D_MODEL, FF_BLOCK), lambda i, j, k: (i, 0, 0)),
                    _sds((NDEV, D_MODEL, FF_BLOCK), BF16), 0, 0, (D_MODEL, FF_BLOCK), drop=(0, 2, 1))
    dh = _mm_spec(dgu, w_gu, f"ffn_dh_{tag}", (s // tm, 1, NDEV),
                  pl.BlockSpec((1, 1, tm, FF_BLOCK), lambda i, j, k: (k % 4, k // 4, i, 0)),
                  pl.BlockSpec((1, D_MODEL, FF_BLOCK), lambda i, j, k: (k, 0, 0)),
                  row, _sds((s, D_MODEL)), 1, 1, (tm, D_MODEL), drop=(2, 1, 0))
    dx, g_norm = _rms_bwd(x, norm_w, dh, dy, f"ffn_dnorm_{tag}")
    return dx, g_norm, g_gu, g_down


def _prev_rows(cur, halo, j, first):
    rid = lax.broadcasted_iota(jnp.int32, cur.shape, 0)
    hid = lax.broadcasted_iota(jnp.int32, halo.shape, 0)
    out = pltpu.roll(cur, j, 0)
    for t in range(j):
        row = jnp.sum(jnp.where(hid == 8 - j + t, halo, 0.0), axis=0, keepdims=True)
        row = jnp.where(first, 0.0, row)
        out = jnp.where(rid == t, row, out)
    return out


def _next_rows(cur, halo, j, last):
    ts = cur.shape[0]
    rid = lax.broadcasted_iota(jnp.int32, cur.shape, 0)
    hid = lax.broadcasted_iota(jnp.int32, halo.shape, 0)
    out = pltpu.roll(cur, ts - j, 0)
    for t in range(j):
        row = jnp.sum(jnp.where(hid == t, halo, 0.0), axis=0, keepdims=True)
        row = jnp.where(last, 0.0, row)
        out = jnp.where(rid == ts - j + t, row, out)
    return out


def _halo_specs(ts, s, width, col):
    per = ts // 8
    nblk = s // 8
    prev = pl.BlockSpec((8, width), lambda i: (jnp.maximum(i * per - 1, 0), col))
    nxt = pl.BlockSpec((8, width), lambda i: (jnp.minimum((i + 1) * per, nblk - 1), col))
    return prev, nxt


def _cgate_fwd(p, w_dw, name):
    s = p.shape[0]
    d = D_MODEL
    ts = _row_tile(s)
    prev, _ = _halo_specs(ts, s, 3 * d, 0)

    def body(p_ref, h_ref, w_ref, z_ref):
        first = pl.program_id(0) == 0
        b = p_ref[:, :d]
        cv = p_ref[:, d:2 * d] * p_ref[:, 2 * d:]
        hcv = h_ref[:, d:2 * d] * h_ref[:, 2 * d:]
        u = w_ref[2:3, :] * cv + w_ref[1:2, :] * _prev_rows(cv, hcv, 1, first) + w_ref[0:1, :] * _prev_rows(cv, hcv, 2, first)
        z_ref[...] = (b * u).astype(BF16)

    return _pc(body, name, (s // ts,),
               [pl.BlockSpec((ts, 3 * d), lambda i: (i, 0)), prev, pl.BlockSpec((3, d), lambda i: (0, 0))],
               pl.BlockSpec((ts, d), lambda i: (i, 0)), _sds((s, d), BF16))(p, p, w_dw)


def _cgate_bwd(p, dz, w_dw, name):
    s = p.shape[0]
    d = D_MODEL
    ts = _row_tile(s)
    nt = s // ts
    p_prev, p_next = _halo_specs(ts, s, 3 * d, 0)
    _, dz_next = _halo_specs(ts, s, d, 0)

    def body(p_ref, hp_ref, hn_ref, dz_ref, dzn_ref, w_ref, dp_ref, dw_ref):
        i = pl.program_id(0)
        first = i == 0
        last = i == nt - 1
        b = p_ref[:, :d]
        c = p_ref[:, d:2 * d]
        v = p_ref[:, 2 * d:]
        cv = c * v
        hcv = hp_ref[:, d:2 * d] * hp_ref[:, 2 * d:]
        cv1 = _prev_rows(cv, hcv, 1, first)
        cv2 = _prev_rows(cv, hcv, 2, first)
        w0, w1, w2 = w_ref[0:1, :], w_ref[1:2, :], w_ref[2:3, :]
        u = w2 * cv + w1 * cv1 + w0 * cv2
        dzv = dz_ref[...]
        du = dzv * b
        dun = dzn_ref[...] * hn_ref[:, :d]
        dcv = w2 * du + w1 * _next_rows(du, dun, 1, last) + w0 * _next_rows(du, dun, 2, last)
        dp_ref[:, :d] = (dzv * u).astype(BF16)
        dp_ref[:, d:2 * d] = (dcv * v).astype(BF16)
        dp_ref[:, 2 * d:] = (dcv * c).astype(BF16)

        @pl.when(first)
        def _():
            dw_ref[...] = jnp.zeros_like(dw_ref)

        dw_ref[0:1, :] += jnp.sum(du * cv2, axis=0, keepdims=True)
        dw_ref[1:2, :] += jnp.sum(du * cv1, axis=0, keepdims=True)
        dw_ref[2:3, :] += jnp.sum(du * cv, axis=0, keepdims=True)

    wide = pl.BlockSpec((ts, 3 * d), lambda i: (i, 0))
    wspec = pl.BlockSpec((3, d), lambda i: (0, 0))
    return _pc(body, name, (nt,),
               [wide, p_prev, p_next, pl.BlockSpec((ts, d), lambda i: (i, 0)), dz_next, wspec],
               [wide, wspec], [_sds((s, 3 * d), BF16), _sds((3, d))])(p, p, p, dz, dz, w_dw)


def _conv_fwd(x, norm_w, w_in, w_dw, w_out, tag):
    s = x.shape[0]
    tm = _pick_tile(s)
    h = _rms_fwd(x, norm_w, f"conv_norm_{tag}")
    p = _mm_spec(h, w_in, f"conv_in_{tag}", (s // tm, NDEV, 1),
                 pl.BlockSpec((tm, D_MODEL), lambda i, j, k: (i, 0)),
                 pl.BlockSpec((1, D_MODEL, CONV_BLOCK), lambda i, j, k: (j, 0, 0)),
                 pl.BlockSpec((tm, CONV_BLOCK), lambda i, j, k: (i, j)),
                 _sds((s, 3 * D_MODEL)), 1, 0, (tm, CONV_BLOCK), drop=(0, 1, 0))
    z = _cgate_fwd(p, w_dw, f"conv_gate_{tag}")
    y = _mm(z, w_out, "nn", f"conv_out_{tag}", res=x)
    return y, (x, h, p, z)


def _conv_bwd(dy, saved, norm_w, w_in, w_dw, w_out, tag):
    x, h, p, z = saved
    s = x.shape[0]
    tm = _pick_tile(s)
    dz = _mm(dy, w_out, "nt", f"conv_dz_{tag}")
    g_out = _mm(z, dy, "tn", f"conv_gout_{tag}", out_dtype=BF16)
    dp, g_dw = _cgate_bwd(p, dz, w_dw, f"conv_dgate_{tag}")
    g_in = _mm_spec(h, dp, f"conv_gin_{tag}", (NDEV, 1, s // tm),
                    pl.BlockSpec((tm, D_MODEL), lambda i, j, k: (k, 0)),
                    pl.BlockSpec((tm, CONV_BLOCK), lambda i, j, k: (k, i)),
                    pl.BlockSpec((1, D_MODEL, CONV_BLOCK), lambda i, j, k: (i, 0, 0)),
                    _sds((NDEV, D_MODEL, CONV_BLOCK), BF16), 0, 0, (D_MODEL, CONV_BLOCK), drop=(0, 0, 1))
    dh = _mm_spec(dp, w_in, f"conv_dh_{tag}", (s // tm, 1, NDEV),
                  pl.BlockSpec((tm, CONV_BLOCK), lambda i, j, k: (i, k)),
                  pl.BlockSpec((1, D_MODEL, CONV_BLOCK), lambda i, j, k: (k, 0, 0)),
                  pl.BlockSpec((tm, D_MODEL), lambda i, j, k: (i, 0)),
                  _sds((s, D_MODEL)), 1, 1, (tm, D_MODEL), drop=(0, 1, 0))
    dx, g_norm = _rms_bwd(x, norm_w, dh, dy, f"conv_dnorm_{tag}")
    return dx, g_norm, g_in, g_dw, g_out


def _tri(lower):
    r = lax.broadcasted_iota(jnp.int32, (LANES, LANES), 0)
    c = lax.broadcasted_iota(jnp.int32, (LANES, LANES), 1)
    return jnp.where((r >= c) if lower else (r <= c), 1.0, 0.0).astype(F32)


def _cumsum_rows(v, reverse, name):
    s = v.shape[0]
    n = s // LANES
    idx = (lambda i: (n - 1 - i, 0)) if reverse else (lambda i: (i, 0))

    def body(v_ref, o_ref, carry_ref):
        @pl.when(pl.program_id(0) == 0)
        def _():
            carry_ref[...] = jnp.zeros_like(carry_ref)

        blk = v_ref[...]
        o_ref[...] = _dot(_tri(not reverse), blk, 1, 0, HI) + carry_ref[0:1, :]
        carry_ref[...] += jnp.sum(blk, axis=0, keepdims=True)

    spec = pl.BlockSpec((LANES, LANES), idx)
    return _pc(body, name, (n,), [spec], spec, _sds((s, LANES)), [pltpu.VMEM((8, LANES), F32)])(v)


def _lo_mask(shape):
    return lax.broadcasted_iota(jnp.int32, shape, len(shape) - 1) < HEAD_DIM


def _half_sums(v, lo):
    sa = jnp.sum(jnp.where(lo, v, 0.0), axis=-1, keepdims=True)
    sb = jnp.sum(jnp.where(lo, 0.0, v), axis=-1, keepdims=True)
    return jnp.where(lo, sa, sb)


def _fox_prep_fwd(proj, gq, gk, name):
    s = proj.shape[0]
    ts = _row_tile(s, 512)
    qscale = HEAD_DIM ** -0.5 * LOG2E

    def body(q_ref, k_ref, v_ref, gq_ref, gk_ref, qo_ref, ko_ref, vo_ref):
        lo = _lo_mask((ts, LANES))

        def hnorm(xv, g):
            ms = _half_sums(xv * xv, lo) * (1.0 / HEAD_DIM)
            return (xv * lax.rsqrt(ms + RMS_EPS)) * g

        qo_ref[...] = (hnorm(q_ref[...], gq_ref[...]) * qscale).astype(BF16)
        ko_ref[...] = hnorm(k_ref[...], gk_ref[...]).astype(BF16)
        vo_ref[...] = v_ref[...].astype(BF16)

    def col(off):
        return pl.BlockSpec((ts, LANES), lambda i, p: (i, off + p))

    gspec = pl.BlockSpec((1, LANES), lambda i, p: (0, 0))
    out = _sds((s, D_MODEL), BF16)
    return _pc(body, name, (s // ts, 8), [col(0), col(8), col(16), gspec, gspec], [col(0)] * 3, [out] * 3)(
        proj, proj, proj, gq, gk)


def _fox_logf(proj, bf, name):
    s = proj.shape[0]
    ts = _row_tile(s, 512)

    def body(f_ref, b_ref, o_ref):
        z = f_ref[...] + b_ref[...]
        lf = jnp.minimum(z, 0.0) - jnp.log(1.0 + jnp.exp(-jnp.abs(z)))
        real = lax.broadcasted_iota(jnp.int32, (ts, LANES), 1) < ATTN_HEADS
        o_ref[...] = jnp.where(real, lf, 0.0)

    return _pc(body, name, (s // ts,), [pl.BlockSpec((ts, LANES), lambda i: (i, 24)), pl.BlockSpec((1, LANES), lambda i: (0, 0))],
               pl.BlockSpec((ts, LANES), lambda i: (i, 0)), _sds((s, LANES)))(proj, bf)


def _fox_dlogf(proj, bf, dlf, name):
    s = proj.shape[0]
    ts = _row_tile(s, 512)

    def body(f_ref, b_ref, d_ref, o_ref, db_ref):
        z = f_ref[...] + b_ref[...]
        real = lax.broadcasted_iota(jnp.int32, (ts, LANES), 1) < ATTN_HEADS
        g = jnp.where(real, d_ref[...] * _sigmoid(-z), 0.0)
        o_ref[...] = g.astype(BF16)

        @pl.when(pl.program_id(0) == 0)
        def _():
            db_ref[...] = jnp.zeros_like(db_ref)

        db_ref[...] += jnp.sum(g, axis=0, keepdims=True)

    vec = pl.BlockSpec((1, LANES), lambda i: (0, 0))
    row = pl.BlockSpec((ts, LANES), lambda i: (i, 0))
    return _pc(body, name, (s // ts,), [pl.BlockSpec((ts, LANES), lambda i: (i, 24)), vec, row], [row, vec],
               [_sds((s, LANES), BF16), _sds((1, LANES))])(proj, bf, dlf)


def _decay_terms(cum):
    s = cum.shape[0]
    c2 = cum * LOG2E
    hi = lax.reduce_precision(c2, 8, 7)
    mid = lax.reduce_precision(c2 - hi, 8, 7)
    low = lax.reduce_precision(c2 - hi - mid, 8, 7)
    one = jnp.ones_like(hi)

    def place(terms):
        tt = jnp.stack(terms, axis=-1).astype(BF16).reshape(s, 8, 2, 6)
        z = jnp.zeros((s, 8, HEAD_DIM - 6), BF16)
        return jnp.concatenate([tt[:, :, 1], z, tt[:, :, 0], z], axis=-1).reshape(s, D_MODEL)

    return place([hi, mid, low, one, one, one]), place([one, one, one, -hi, -mid, -low])


def _attn_tiles(s):
    t = s
    for cand in (ATTN_TILE, ATTN_TILE // 2):
        if s % cand == 0:
            t = cand
            break
    return t, s // t


def _tri_steps(n, by_key):
    if by_key:
        pairs = [(q, k) for k in range(n) for q in range(k, n)]
    else:
        pairs = [(q, k) for q in range(n) for k in range(q + 1)]
    arr = np.asarray(pairs, np.int32)
    return jnp.asarray(arr[:, 0]), jnp.asarray(arr[:, 1])


def _attn_call(body, name, s, by_key, inputs, in_kinds, out_kinds, out_shapes, scratch, hosted=None):
    t, n = _attn_tiles(s)
    qi_arr, ki_arr = _tri_steps(n, by_key)
    nsteps = int(qi_arr.shape[0])
    specs = {
        "q": pl.BlockSpec((t, LANES), lambda p, i, qi, ki: (qi[i], p)),
        "k": pl.BlockSpec((t, LANES), lambda p, i, qi, ki: (ki[i], p)),
        "r": pl.BlockSpec((1, 2, t), lambda p, i, qi, ki: (p, 0, qi[i])),
        "m": pl.BlockSpec((1, t, t), lambda p, i, qi, ki: (jnp.where(qi[i] == ki[i], 1, 0), 0, 0)),
    }
    in_specs = [specs[c] for c in in_kinds]
    out_specs = [specs[c] for c in out_kinds]
    out_shapes, scratch, inputs = list(out_shapes), list(scratch), list(inputs)
    run = body
    if hosted is not None:
        arrays, gather = hosted
        na, n_in, n_out, n_scr = len(arrays), len(inputs), len(out_kinds), len(scratch)
        pick, xouts, sems = _exchange_parts(arrays, gather)

        def run(qi_ref, ki_ref, *refs):
            ins, srcs = refs[:n_in], refs[n_in:n_in + na]
            outs, dsts = refs[n_in + na:n_in + na + n_out], refs[n_in + na + n_out:n_in + 2 * na + n_out]
            scr, xsems = refs[n_in + 2 * na + n_out:n_in + 2 * na + n_out + n_scr], refs[n_in + 2 * na + n_out + n_scr:]
            p = pl.program_id(0)
            i = pl.program_id(1)

            @pl.when(jnp.logical_and(p == 0, i == 0))
            def _():
                _exchange_start(_exchange_copies(pick(srcs), dsts, *xsems))

            body(qi_ref, ki_ref, *ins, *outs, *scr)

            @pl.when(jnp.logical_and(p == 7, i == nsteps - 1))
            def _():
                _exchange_wait(_exchange_copies(pick(srcs), dsts, *xsems))

        hbm = pl.BlockSpec(memory_space=pl.ANY)
        in_specs += [hbm] * na
        out_specs += [hbm] * na
        out_shapes += xouts
        scratch += sems
        inputs += list(arrays)
    grid_spec = pltpu.PrefetchScalarGridSpec(
        num_scalar_prefetch=2, grid=(8, nsteps), in_specs=in_specs, out_specs=out_specs, scratch_shapes=scratch)
    return pl.pallas_call(run, name=name, grid_spec=grid_spec, out_shape=out_shapes, compiler_params=_params(2))(
        qi_arr, ki_arr, *inputs)


def _biased_kq(q2, k2, aq, ak, lo):
    sa = _dot(jnp.where(lo, k2, ak), jnp.where(lo, q2, aq), 1, 1)
    sb = _dot(jnp.where(lo, ak, k2), jnp.where(lo, aq, q2), 1, 1)
    return sa, sb


def _causal_bias(s):
    t, _ = _attn_tiles(s)
    kid = lax.broadcasted_iota(jnp.int32, (t, t), 0)
    qid = lax.broadcasted_iota(jnp.int32, (t, t), 1)
    return jnp.stack([jnp.zeros((t, t), F32), jnp.where(kid > qid, -jnp.inf, 0.0).astype(F32)])


def _fold8(v, op):
    return op(v.reshape(v.shape[0] // 8, 8, v.shape[1]), axis=0)


def _chunk(ref, mask_ref, hd, r):
    rows = slice(r * ATTN_ROWS, (r + 1) * ATTN_ROWS)
    return rows, ref[hd, rows, :] + mask_ref[0, rows, :]


def _flash_fwd(qs, kn, vb, augq, augk, cmask, name, hosted=None):
    s = qs.shape[0]
    t, n = _attn_tiles(s)
    nch = t // ATTN_ROWS

    def body(qi_ref, ki_ref, q_ref, k_ref, v_ref, aq_ref, ak_ref, mk_ref, o_ref, lse_ref, s_ref, p_ref, m_ref, l_ref, acc_ref):
        i = pl.program_id(1)
        qi = qi_ref[i]
        ki = ki_ref[i]

        @pl.when(ki == 0)
        def _():
            m_ref[...] = jnp.full_like(m_ref, -jnp.inf)
            l_ref[...] = jnp.zeros_like(l_ref)
            acc_ref[...] = jnp.zeros_like(acc_ref)

        lo = _lo_mask((t, LANES))
        rowlo = lax.broadcasted_iota(jnp.int32, (LANES, t), 0) < HEAD_DIM
        v2 = v_ref[...]
        sa, sb = _biased_kq(q_ref[...], k_ref[...], aq_ref[...], ak_ref[...], lo)
        s_ref[0] = sa
        s_ref[1] = sb
        alphas, pvs = [], []
        for hd in range(2):
            mx = jnp.full((8, t), -jnp.inf, F32)
            for r in range(nch):
                _, sc = _chunk(s_ref, mk_ref, hd, r)
                mx = jnp.maximum(mx, _fold8(sc, jnp.max))
            m_prev = m_ref[hd:hd + 1, :]
            m_new = jnp.maximum(m_prev, jnp.max(mx, axis=0, keepdims=True))
            ls = jnp.zeros((8, t), F32)
            for r in range(nch):
                rows, sc = _chunk(s_ref, mk_ref, hd, r)
                pm = jnp.exp2(sc - m_new)
                ls = ls + _fold8(pm, jnp.sum)
                p_ref[hd, rows, :] = pm.astype(BF16)
            alpha = jnp.exp2(m_prev - m_new)
            l_ref[hd:hd + 1, :] = alpha * l_ref[hd:hd + 1, :] + jnp.sum(ls, axis=0, keepdims=True)
            m_ref[hd:hd + 1, :] = m_new
            alphas.append(alpha)
            pvs.append(_dot(v2, p_ref[hd], 0, 0))
        acc_ref[...] = jnp.where(rowlo, alphas[0], alphas[1]) * acc_ref[...] + jnp.where(rowlo, pvs[0], pvs[1])

        @pl.when(ki == qi)
        def _():
            o_ref[...] = (acc_ref[...] / jnp.where(rowlo, l_ref[0:1, :], l_ref[1:2, :])).T
            lse_ref[0] = m_ref[0:2, :] + jnp.log2(l_ref[0:2, :])

    stat = pltpu.VMEM((8, t), F32)
    return _attn_call(body, name, s, False, (qs, kn, vb, augq, augk, cmask), "qkkqkm", "qr",
                      [_sds((s, D_MODEL)), _sds((8, 2, s))],
                      [pltpu.VMEM((2, t, t), F32), pltpu.VMEM((2, t, t), BF16), stat, stat, pltpu.VMEM((LANES, t), F32)],
                      hosted=hosted)


def _fox_delta(do, o, name):
    s = do.shape[0]
    ts = _row_tile(s, 512)

    def body(do_ref, o_ref, d_ref):
        d_ref[...] = _half_sums(do_ref[...] * o_ref[...], _lo_mask((ts, LANES)))

    spec = pl.BlockSpec((ts, LANES), lambda i, p: (i, p))
    return _pc(body, name, (s // ts, 8), [spec, spec], spec, _sds((s, D_MODEL)))(do, o)


def _bwd_tile(q_ref, k_ref, v_ref, aq_ref, ak_ref, do_ref, s_ref, dp_ref, lo):
    do2 = do_ref[...].astype(BF16)
    zero = jnp.zeros_like(do2)
    v2 = v_ref[...]
    sa, sb = _biased_kq(q_ref[...], k_ref[...], aq_ref[...], ak_ref[...], lo)
    s_ref[0] = sa
    s_ref[1] = sb
    dp_ref[0] = _dot(v2, jnp.where(lo, do2, zero), 1, 1)
    dp_ref[1] = _dot(v2, jnp.where(lo, zero, do2), 1, 1)
    return do2


def _bwd_chunk(s_ref, dp_ref, mk_ref, lse_ref, dl_ref, hd, r):
    rows, sc = _chunk(s_ref, mk_ref, hd, r)
    pm = jnp.exp2(sc - lse_ref[0, hd:hd + 1, :])
    ds = pm * (dp_ref[hd, rows, :] - dl_ref[0, hd:hd + 1, :])
    return rows, pm, ds


def _flash_bwd_dq(qs, kn, vb, augq, augk, cmask, do, lse, delta, name, hosted=None):
    s = qs.shape[0]
    t, n = _attn_tiles(s)
    nch = t // ATTN_ROWS

    def body(qi_ref, ki_ref, q_ref, k_ref, v_ref, aq_ref, ak_ref, mk_ref, do_ref, lse_ref, dl_ref, dq_ref, dcq_ref,
             s_ref, dp_ref, ds_ref, acc_ref, racc_ref):
        i = pl.program_id(1)
        qi = qi_ref[i]
        ki = ki_ref[i]

        @pl.when(ki == 0)
        def _():
            acc_ref[...] = jnp.zeros_like(acc_ref)
            racc_ref[...] = jnp.zeros_like(racc_ref)

        lo = _lo_mask((t, LANES))
        rowlo = lax.broadcasted_iota(jnp.int32, (LANES, t), 0) < HEAD_DIM
        _bwd_tile(q_ref, k_ref, v_ref, aq_ref, ak_ref, do_ref, s_ref, dp_ref, lo)
        k2 = k_ref[...]
        dqs = []
        for hd in range(2):
            rs = jnp.zeros((8, t), F32)
            for r in range(nch):
                rows, _, ds = _bwd_chunk(s_ref, dp_ref, mk_ref, lse_ref, dl_ref, hd, r)
                rs = rs + _fold8(ds, jnp.sum)
                ds_ref[hd, rows, :] = ds.astype(BF16)
            racc_ref[hd:hd + 1, :] += jnp.sum(rs, axis=0, keepdims=True)
            dqs.append(_dot(k2, ds_ref[hd], 0, 0))
        acc_ref[...] += jnp.where(rowlo, dqs[0], dqs[1])

        @pl.when(ki == qi)
        def _():
            dq_ref[...] = acc_ref[...].T
            dcq_ref[0] = racc_ref[0:2, :]

    return _attn_call(body, name, s, False, (qs, kn, vb, augq, augk, cmask, do, lse, delta), "qkkqkmqrr", "qr",
                      [_sds((s, D_MODEL)), _sds((8, 2, s))],
                      [pltpu.VMEM((2, t, t), F32), pltpu.VMEM((2, t, t), F32), pltpu.VMEM((2, t, t), BF16),
                       pltpu.VMEM((LANES, t), F32), pltpu.VMEM((8, t), F32)], hosted=hosted)


def _flash_bwd_dkv(qs, kn, vb, augq, augk, cmask, do, lse, delta, name):
    s = qs.shape[0]
    t, n = _attn_tiles(s)
    nch = t // ATTN_ROWS

    def body(qi_ref, ki_ref, q_ref, k_ref, v_ref, aq_ref, ak_ref, mk_ref, do_ref, lse_ref, dl_ref, dk_ref, dv_ref, dc_ref,
             s_ref, dp_ref, p_ref, ds_ref, dka_ref, dva_ref, dca_ref):
        i = pl.program_id(1)
        qi = qi_ref[i]
        ki = ki_ref[i]

        @pl.when(qi == ki)
        def _():
            dka_ref[...] = jnp.zeros_like(dka_ref)
            dva_ref[...] = jnp.zeros_like(dva_ref)
            dca_ref[...] = jnp.zeros_like(dca_ref)

        lo = _lo_mask((t, LANES))
        do2 = _bwd_tile(q_ref, k_ref, v_ref, aq_ref, ak_ref, do_ref, s_ref, dp_ref, lo)
        q2 = q_ref[...]
        dvs, dks = [], []
        for hd in range(2):
            for r in range(nch):
                rows, pm, ds = _bwd_chunk(s_ref, dp_ref, mk_ref, lse_ref, dl_ref, hd, r)
                part = ds[:, 0:LANES]
                for c in range(1, t // LANES):
                    part = part + ds[:, c * LANES:(c + 1) * LANES]
                dca_ref[hd, rows, :] += part
                p_ref[hd, rows, :] = pm.astype(BF16)
                ds_ref[hd, rows, :] = ds.astype(BF16)
            dvs.append(_dot(p_ref[hd], do2, 1, 0))
            dks.append(_dot(ds_ref[hd], q2, 1, 0))
        dva_ref[...] += jnp.where(lo, dvs[0], dvs[1])
        dka_ref[...] += jnp.where(lo, dks[0], dks[1])

        @pl.when(qi == n - 1)
        def _():
            dk_ref[...] = dka_ref[...] * LN2
            dv_ref[...] = dva_ref[...]
            dc_ref[...] = -jnp.where(lo, jnp.sum(dca_ref[0], axis=-1, keepdims=True), jnp.sum(dca_ref[1], axis=-1, keepdims=True))

    out = _sds((s, D_MODEL))
    return _attn_call(body, name, s, True, (qs, kn, vb, augq, augk, cmask, do, lse, delta), "qkkqkmqrr", "kkk", [out, out, out],
                      [pltpu.VMEM((2, t, t), F32), pltpu.VMEM((2, t, t), F32), pltpu.VMEM((2, t, t), BF16),
                       pltpu.VMEM((2, t, t), BF16), pltpu.VMEM((t, LANES), F32), pltpu.VMEM((t, LANES), F32),
                       pltpu.VMEM((2, t, LANES), F32)])


def _fox_prep_bwd(proj, dqs, dk, dv, gq, gk, name):
    s = proj.shape[0]
    ts = _row_tile(s, 512)
    scale = HEAD_DIM ** -0.5

    def body(q_ref, k_ref, dq_ref, dk_ref, dv_ref, gq_ref, gk_ref, oq_ref, ok_ref, ov_ref, dgq_ref, dgk_ref):
        lo = _lo_mask((ts, LANES))

        @pl.when(jnp.logical_and(pl.program_id(0) == 0, pl.program_id(1) == 0))
        def _():
            dgq_ref[...] = jnp.zeros_like(dgq_ref)
            dgk_ref[...] = jnp.zeros_like(dgk_ref)

        def back(xv, dout, g):
            r = lax.rsqrt(_half_sums(xv * xv, lo) * (1.0 / HEAD_DIM) + RMS_EPS)
            y = xv * r
            dy = dout * g
            dx = r * (dy - y * (_half_sums(dy * y, lo) * (1.0 / HEAD_DIM)))
            return dx, jnp.sum(dout * y, axis=0, keepdims=True)

        dxq, dgq = back(q_ref[...], dq_ref[...] * scale, gq_ref[...])
        dxk, dgk = back(k_ref[...], dk_ref[...], gk_ref[...])
        oq_ref[...] = dxq.astype(BF16)
        ok_ref[...] = dxk.astype(BF16)
        ov_ref[...] = dv_ref[...].astype(BF16)
        dgq_ref[...] += dgq
        dgk_ref[...] += dgk

    def col(off):
        return pl.BlockSpec((ts, LANES), lambda i, p: (i, off + p))

    gspec = pl.BlockSpec((1, LANES), lambda i, p: (0, 0))
    out = _sds((s, D_MODEL), BF16)
    return _pc(body, name, (s // ts, 8), [col(0), col(8), col(0), col(0), col(0), gspec, gspec],
               [col(0)] * 3 + [gspec] * 2, [out] * 3 + [_sds((1, LANES))] * 2)(proj, proj, dqs, dk, dv, gq, gk)


def _fox_fwd(x, norm_w, w_in, b_f, q_gain, k_gain, w_out, hosted=None):
    h = _rms_fwd(x, norm_w, "fox_norm")
    proj = _mm(h, w_in, "nn", "fox_in")
    gq = jnp.tile(q_gain, (1, 2))
    gk = jnp.tile(k_gain, (1, 2))
    bf = jnp.pad(b_f, ((0, 0), (0, LANES - ATTN_HEADS)))
    qs, kn, vb = _fox_prep_fwd(proj, gq, gk, "fox_prep")
    cum = _cumsum_rows(_fox_logf(proj, bf, "fox_logf"), False, "fox_cum")[:, :ATTN_HEADS]
    augq, augk = _decay_terms(cum)
    cmask = _causal_bias(x.shape[0])
    o, lse, *got = _flash_fwd(qs, kn, vb, augq, augk, cmask, "fox_attn", hosted=hosted)
    y = _mm(o, w_out, "nn", "fox_out", res=x)
    return y, (x, h, proj, gq, gk, bf, qs, kn, vb, augq, augk, cmask, o, lse), got


def _fox_bwd(dy, saved, norm_w, w_in, w_out, hosted=None):
    x, h, proj, gq, gk, bf, qs, kn, vb, augq, augk, cmask, o, lse = saved
    s = x.shape[0]
    do = _mm(dy, w_out, "nt", "fox_do")
    g_out = _mm(o, dy, "tn", "fox_gout", out_dtype=BF16)
    delta = _fox_delta(do, o, "fox_delta")[:, ::HEAD_DIM].T.reshape(8, 2, s)
    dqs, dcq, *got = _flash_bwd_dq(qs, kn, vb, augq, augk, cmask, do, lse, delta, "fox_dq", hosted=hosted)
    dk, dv, dck = _flash_bwd_dkv(qs, kn, vb, augq, augk, cmask, do, lse, delta, "fox_dkv")
    dcum = jnp.pad(dcq.reshape(ATTN_HEADS, s).T + dck[:, ::HEAD_DIM], ((0, 0), (0, LANES - ATTN_HEADS)))
    dlf = _cumsum_rows(dcum, True, "fox_dcum")
    dfl, g_bf = _fox_dlogf(proj, bf, dlf, "fox_dlogf")
    dq_o, dk_o, dv_o, g_gq, g_gk = _fox_prep_bwd(proj, dqs, dk, dv, gq, gk, "fox_dprep")
    dproj = jnp.concatenate([dq_o, dk_o, dv_o, dfl], axis=1)
    g_in = _mm(h, dproj, "tn", "fox_gin", out_dtype=BF16)
    dh = _mm(dproj, w_in, "nt", "fox_dh")
    dx, g_norm = _rms_bwd(x, norm_w, dh, dy, "fox_dnorm")
    g_q = g_gq[:, :HEAD_DIM] + g_gq[:, HEAD_DIM:]
    g_k = g_gk[:, :HEAD_DIM] + g_gk[:, HEAD_DIM:]
    return dx, g_norm, g_in[:, :FOX_IN], g_bf[:, :ATTN_HEADS], g_q, g_k, g_out, got


def _ssd_conv_fwd(proj, cw, cb, name):
    s = proj.shape[0]
    ts = _row_tile(s)
    w = 1024
    per = ts // 8

    def body(p_ref, h_ref, w_ref, b_ref, o_ref):
        first = pl.program_id(0) == 0
        cur = p_ref[...]
        halo = h_ref[...]
        u = w_ref[3:4, :] * cur + b_ref[...]
        for j in range(1, 4):
            u = u + w_ref[3 - j:4 - j, :] * _prev_rows(cur, halo, j, first)
        o_ref[...] = u * _sigmoid(u)

    return _pc(body, name, (s // ts, 4),
               [pl.BlockSpec((ts, w), lambda i, j: (i, 2 + j)),
                pl.BlockSpec((8, w), lambda i, j: (jnp.maximum(i * per - 1, 0), 2 + j)),
                pl.BlockSpec((4, w), lambda i, j: (0, j)), pl.BlockSpec((1, w), lambda i, j: (0, j))],
               pl.BlockSpec((ts, w), lambda i, j: (i, j)), _sds((s, SSM_CONV_DIM)))(proj, proj, cw, cb)


def _ssd_conv_bwd_act(proj, dxbc, cw, cb, name):
    s = proj.shape[0]
    ts = _row_tile(s)
    w = 1024
    per = ts // 8

    def body(p_ref, h_ref, d_ref, w_ref, b_ref, g_ref, db_ref):
        first = pl.program_id(1) == 0
        cur = p_ref[...]
        halo = h_ref[...]
        u = w_ref[3:4, :] * cur + b_ref[...]
        for j in range(1, 4):
            u = u + w_ref[3 - j:4 - j, :] * _prev_rows(cur, halo, j, first)
        sg = _sigmoid(u)
        g = d_ref[...] * (sg * (1.0 + u * (1.0 - sg)))
        g_ref[...] = g

        @pl.when(first)
        def _():
            db_ref[...] = jnp.zeros_like(db_ref)

        db_ref[...] += jnp.sum(g, axis=0, keepdims=True)

    vec = pl.BlockSpec((1, w), lambda j, i: (0, j))
    tile = pl.BlockSpec((ts, w), lambda j, i: (i, j))
    return _pc(body, name, (4, s // ts),
               [pl.BlockSpec((ts, w), lambda j, i: (i, 2 + j)),
                pl.BlockSpec((8, w), lambda j, i: (jnp.maximum(i * per - 1, 0), 2 + j)),
                tile, pl.BlockSpec((4, w), lambda j, i: (0, j)), vec],
               [tile, vec], [_sds((s, SSM_CONV_DIM)), _sds((1, SSM_CONV_DIM))])(proj, proj, dxbc, cw, cb)


def _ssd_conv_bwd_in(proj, g, cw, name):
    s = proj.shape[0]
    ts = _row_tile(s)
    nt = s // ts
    w = 1024
    per = ts // 8
    nblk = s // 8

    def body(p_ref, h_ref, g_ref, gn_ref, w_ref, o_ref, dw_ref):
        i = pl.program_id(1)
        first = i == 0
        last = i == nt - 1
        cur = p_ref[...]
        halo = h_ref[...]
        gv = g_ref[...]
        gn = gn_ref[...]

        @pl.when(first)
        def _():
            dw_ref[...] = jnp.zeros_like(dw_ref)

        dpre = w_ref[3:4, :] * gv
        dw_ref[3:4, :] += jnp.sum(gv * cur, axis=0, keepdims=True)
        for j in range(1, 4):
            dpre = dpre + w_ref[3 - j:4 - j, :] * _next_rows(gv, gn, j, last)
            dw_ref[3 - j:4 - j, :] += jnp.sum(gv * _prev_rows(cur, halo, j, first), axis=0, keepdims=True)
        o_ref[...] = dpre.astype(BF16)

    tile = pl.BlockSpec((ts, w), lambda j, i: (i, j))
    wspec = pl.BlockSpec((4, w), lambda j, i: (0, j))
    return _pc(body, name, (4, nt),
               [pl.BlockSpec((ts, w), lambda j, i: (i, 2 + j)),
                pl.BlockSpec((8, w), lambda j, i: (jnp.maximum(i * per - 1, 0), 2 + j)),
                tile, pl.BlockSpec((8, w), lambda j, i: (jnp.minimum((i + 1) * per, nblk - 1), j)), wspec],
               [tile, wspec], [_sds((s, SSM_CONV_DIM), BF16), _sds((4, SSM_CONV_DIM))])(proj, proj, g, g, cw)


def _ssd_dt_fwd(proj, bias, a_neg, name):
    s = proj.shape[0]
    n = s // SSM_CHUNK

    def body(r_ref, b_ref, a_ref, dt_ref, ac_ref):
        real = lax.broadcasted_iota(jnp.int32, (SSM_CHUNK, LANES), 1) < SSM_HEADS
        dt = jnp.where(real, _softplus(r_ref[...] + b_ref[...]), 0.0)
        dt_ref[...] = dt
        ac_ref[...] = _dot(_tri(True), dt * a_ref[...], 1, 0, HI)

    vec = pl.BlockSpec((1, LANES), lambda c: (0, 0))
    row = pl.BlockSpec((SSM_CHUNK, LANES), lambda c: (c, 0))
    return _pc(body, name, (n,), [pl.BlockSpec((SSM_CHUNK, LANES), lambda c: (c, 48)), vec, vec], [row, row],
               [_sds((s, LANES)), _sds((s, LANES))])(proj, bias, a_neg)


def _ssd_dt_bwd(proj, bias, ddt, name):
    s = proj.shape[0]
    ts = _row_tile(s, 512)

    def body(r_ref, b_ref, d_ref, o_ref, db_ref):
        real = lax.broadcasted_iota(jnp.int32, (ts, LANES), 1) < SSM_HEADS
        g = jnp.where(real, d_ref[...] * _sigmoid(r_ref[...] + b_ref[...]), 0.0)
        o_ref[...] = g.astype(BF16)

        @pl.when(pl.program_id(0) == 0)
        def _():
            db_ref[...] = jnp.zeros_like(db_ref)

        db_ref[...] += jnp.sum(g, axis=0, keepdims=True)

    vec = pl.BlockSpec((1, LANES), lambda i: (0, 0))
    row = pl.BlockSpec((ts, LANES), lambda i: (i, 0))
    return _pc(body, name, (s // ts,), [pl.BlockSpec((ts, LANES), lambda i: (i, 48)), vec, row], [row, vec],
               [_sds((s, LANES), BF16), _sds((1, LANES))])(proj, bias, ddt)


def _pair_cols(cols, k0, lo):
    return jnp.where(lo, cols[:, k0:k0 + 1], cols[:, k0 + 1:k0 + 2])


def _last_lane(row):
    lane = lax.broadcasted_iota(jnp.int32, row.shape, 1)
    return jnp.sum(jnp.where(lane == SSM_CHUNK - 1, row, 0.0), axis=-1, keepdims=True)


def _ssd_specs(nc, rev):
    cc = (lambda c: nc - 1 - c) if rev else (lambda c: c)
    return dict(
        x=pl.BlockSpec((SSM_CHUNK, 256), lambda g, c: (cc(c), g)),
        b=pl.BlockSpec((SSM_CHUNK, LANES), lambda g, c: (cc(c), 16 + g)),
        c=pl.BlockSpec((SSM_CHUNK, LANES), lambda g, c: (cc(c), 24 + g)),
        col=pl.BlockSpec((1, SSM_CHUNK, 4), lambda g, c: (g, cc(c), 0)),
        row=pl.BlockSpec((1, 4, SSM_CHUNK), lambda g, c: (g, 0, cc(c))),
        grp=pl.BlockSpec((1, 1, 256), lambda g, c: (g, 0, 0)),
        grow=pl.BlockSpec((1, 4, LANES), lambda g, c: (g, 0, 0)),
        hs=pl.BlockSpec((1, 1, 256, SSM_STATE), lambda g, c: (cc(c), g, 0, 0)),
        bc=pl.BlockSpec((SSM_CHUNK, LANES), lambda g, c: (cc(c), g)),
    )


def _ssd_scan_fwd(xbc, dtc, acol, drow, arow, dskip, name):
    s = xbc.shape[0]
    nc = s // SSM_CHUNK
    sp = _ssd_specs(nc, False)
    L = SSM_CHUNK

    def body(x_ref, b_ref, c_ref, dtc_ref, ac_ref, dr_ref, ar_ref, dk_ref, y_ref, hs_ref, h_ref):
        @pl.when(pl.program_id(1) == 0)
        def _():
            h_ref[...] = jnp.zeros_like(h_ref)

        bb = b_ref[...].astype(BF16)
        cb = c_ref[...].astype(BF16)
        gm = _dot(cb, bb, 1, 1)
        dtc = dtc_ref[0]
        ac = ac_ref[0]
        dr = dr_ref[0]
        ar = ar_ref[0]
        dsk = dk_ref[0]
        hs_ref[0, 0] = h_ref[...]
        tril = lax.broadcasted_iota(jnp.int32, (L, L), 0) >= lax.broadcasted_iota(jnp.int32, (L, L), 1)
        lo = _lo_mask((L, LANES))
        rowlo = lax.broadcasted_iota(jnp.int32, (L, LANES), 0) < HEAD_DIM
        for pr in range(2):
            k0 = 2 * pr
            xp = x_ref[:, pr * LANES:(pr + 1) * LANES]
            xpb = xp.astype(BF16)
            hp = h_ref[pr * LANES:(pr + 1) * LANES, :]
            yd, al = [], []
            for k in (k0, k0 + 1):
                seg = ac[:, k:k + 1] - ar[k:k + 1, :]
                wk = gm * jnp.exp(jnp.where(tril, seg, -jnp.inf)) * dr[k:k + 1, :]
                yd.append(_dot(wk.astype(BF16), xpb, 1, 0))
                al.append(_last_lane(ar[k:k + 1, :]))
            e = jnp.exp(_pair_cols(ac, k0, lo))
            yo = _dot(cb, hp.astype(BF16), 1, 1) * e
            y_ref[:, pr * LANES:(pr + 1) * LANES] = jnp.where(lo, yd[0], yd[1]) + yo + dsk[:, pr * LANES:(pr + 1) * LANES] * xp
            wp = jnp.where(lo, jnp.exp(al[0] - ac[:, k0:k0 + 1]) * dtc[:, k0:k0 + 1],
                           jnp.exp(al[1] - ac[:, k0 + 1:k0 + 2]) * dtc[:, k0 + 1:k0 + 2])
            st = _dot((xp * wp).astype(BF16), bb, 0, 0)
            dec = jnp.where(rowlo, jnp.exp(al[0]), jnp.exp(al[1]))
            h_ref[pr * LANES:(pr + 1) * LANES, :] = dec * hp + st

    return _pc(body, name, (SSM_GROUPS, nc),
               [sp["x"], sp["b"], sp["c"], sp["col"], sp["col"], sp["row"], sp["row"], sp["grp"]],
               [sp["x"], sp["hs"]], [_sds((s, SSM_INNER)), _sds((nc, SSM_GROUPS, 256, SSM_STATE))],
               [pltpu.VMEM((256, SSM_STATE), F32)])(xbc, xbc, xbc, dtc, acol, drow, arow, dskip)


def _ssd_scan_bwd(xbc, dtc, acol, drow, arow, dskip, agrp, hs, dy, name):
    s = xbc.shape[0]
    nc = s // SSM_CHUNK
    sp = _ssd_specs(nc, True)
    L = SSM_CHUNK

    def body(x_ref, b_ref, c_ref, dtc_ref, ac_ref, dr_ref, ar_ref, dk_ref, ag_ref, hs_ref, dy_ref,
             dx_ref, db_ref, dc_ref, ddt_ref, da_ref, dd_ref, dh_ref):
        @pl.when(pl.program_id(1) == 0)
        def _():
            dh_ref[...] = jnp.zeros_like(dh_ref)
            da_ref[...] = jnp.zeros_like(da_ref)
            dd_ref[...] = jnp.zeros_like(dd_ref)

        bb = b_ref[...].astype(BF16)
        cb = c_ref[...].astype(BF16)
        gm = _dot(cb, bb, 1, 1)
        dtc = dtc_ref[0]
        ac = ac_ref[0]
        dr = dr_ref[0]
        ar = ar_ref[0]
        dsk = dk_ref[0]
        ag = ag_ref[0]
        tril = lax.broadcasted_iota(jnp.int32, (L, L), 0) >= lax.broadcasted_iota(jnp.int32, (L, L), 1)
        lo = _lo_mask((L, LANES))
        nlo = jnp.logical_not(lo)
        rowlo = lax.broadcasted_iota(jnp.int32, (L, LANES), 0) < HEAD_DIM
        lane = lax.broadcasted_iota(jnp.int32, (L, LANES), 1)
        lane_row = lax.broadcasted_iota(jnp.int32, (1, LANES), 1)
        dgm = jnp.zeros((L, L), F32)
        dcm = jnp.zeros((L, SSM_STATE), F32)
        dbm = jnp.zeros((L, SSM_STATE), F32)
        cols = jnp.zeros((L, LANES), F32)
        rows_ddt, rows_q, al_all, dcd_all = [], [], [], []
        for pr in range(2):
            k0 = 2 * pr
            xp = x_ref[:, pr * LANES:(pr + 1) * LANES]
            xpb = xp.astype(BF16)
            dyp = dy_ref[:, pr * LANES:(pr + 1) * LANES]
            dypb = dyp.astype(BF16)
            zero = jnp.zeros_like(dypb)
            hp = hs_ref[0, 0, pr * LANES:(pr + 1) * LANES, :]
            hpb = hp.astype(BF16)
            dst = dh_ref[pr * LANES:(pr + 1) * LANES, :]
            dstb = dst.astype(BF16)
            dxd, al = [], []
            for k in (k0, k0 + 1):
                sel = lo if k == k0 else nlo
                seg = ac[:, k:k + 1] - ar[k:k + 1, :]
                lam = jnp.exp(jnp.where(tril, seg, -jnp.inf))
                wk = gm * lam * dr[k:k + 1, :]
                dwk = _dot(jnp.where(sel, dypb, zero), xpb, 1, 1)
                mk = dwk * gm * lam
                qk = mk * dr[k:k + 1, :]
                dgm = dgm + dwk * lam * dr[k:k + 1, :]
                rows_ddt.append(jnp.sum(mk, axis=0, keepdims=True))
                rows_q.append(jnp.sum(qk, axis=0, keepdims=True))
                cols = jnp.where(lane == k, jnp.sum(qk, axis=-1, keepdims=True), cols)
                dxd.append(_dot(wk.astype(BF16), dypb, 0, 0))
                al.append(_last_lane(ar[k:k + 1, :]))
            al_all += al
            dxp = jnp.where(lo, dxd[0], dxd[1])
            e = jnp.exp(_pair_cols(ac, k0, lo))
            dye = dyp * e
            dyeb = dye.astype(BF16)
            dcm = dcm + _dot(dyeb, hpb, 1, 0)
            dh_yoff = _dot(dyeb, cb, 0, 0)
            tq = dye * _dot(cb, hpb, 1, 1)
            cols = jnp.where(lane == 4 + k0, jnp.sum(jnp.where(lo, tq, 0.0), axis=-1, keepdims=True), cols)
            cols = jnp.where(lane == 5 + k0, jnp.sum(jnp.where(lo, 0.0, tq), axis=-1, keepdims=True), cols)
            wp = jnp.where(lo, jnp.exp(al[0] - ac[:, k0:k0 + 1]) * dtc[:, k0:k0 + 1],
                           jnp.exp(al[1] - ac[:, k0 + 1:k0 + 2]) * dtc[:, k0 + 1:k0 + 2])
            dxw = _dot(bb, dstb, 1, 1)
            dxp = dxp + dxw * wp
            tw = xp * dxw
            cols = jnp.where(lane == 8 + k0, jnp.sum(jnp.where(lo, tw, 0.0), axis=-1, keepdims=True), cols)
            cols = jnp.where(lane == 9 + k0, jnp.sum(jnp.where(lo, 0.0, tw), axis=-1, keepdims=True), cols)
            dbm = dbm + _dot((xp * wp).astype(BF16), dstb, 1, 0)
            dsl = dsk[:, pr * LANES:(pr + 1) * LANES]
            dx_ref[:, pr * LANES:(pr + 1) * LANES] = dxp + dsl * dyp
            dd_ref[0, :, pr * LANES:(pr + 1) * LANES] += jnp.sum(dyp * xp, axis=0, keepdims=True)
            prod = dst * hp
            dcd_all.append(jnp.sum(jnp.sum(jnp.where(rowlo, prod, 0.0), axis=-1, keepdims=True), axis=0, keepdims=True))
            dcd_all.append(jnp.sum(jnp.sum(jnp.where(rowlo, 0.0, prod), axis=-1, keepdims=True), axis=0, keepdims=True))
            dec = jnp.where(rowlo, jnp.exp(al[0]), jnp.exp(al[1]))
            dh_ref[pr * LANES:(pr + 1) * LANES, :] = dec * dst + dh_yoff
        dgb = dgm.astype(BF16)
        dc_ref[...] = dcm + _dot(dgb, bb, 1, 0)
        db_ref[...] = dbm + _dot(dgb, cb, 0, 0)
        colt = cols.T
        sub8 = lax.broadcasted_iota(jnp.int32, (8, LANES), 0)
        da_rows = jnp.zeros((8, LANES), F32)
        ddt_part = []
        for k in range(4):
            rs = colt[k:k + 1, :]
            uo = colt[4 + k:5 + k, :]
            dwl = colt[8 + k:9 + k, :]
            es = jnp.exp(al_all[k] - ar[k:k + 1, :])
            wrow = es * dr[k:k + 1, :]
            dwl_w = dwl * wrow
            da_k = rs - rows_q[k] + uo - dwl_w
            tail = jnp.sum(dwl_w, axis=-1, keepdims=True) + jnp.exp(al_all[k]) * dcd_all[k]
            da_k = da_k + jnp.where(lane_row == L - 1, tail, 0.0)
            da_rows = jnp.where(sub8 == k, da_k, da_rows)
            ddt_part.append(rows_ddt[k] + dwl * es)
        dda = _dot(da_rows, _tri(True), 1, 0, HI)
        for k in range(4):
            dda_k = dda[k:k + 1, :]
            ddt_ref[0, k:k + 1, :] = ddt_part[k] + dda_k * ag[k:k + 1, :]
            da_ref[0, k:k + 1, :] += dda_k * dr[k:k + 1, :] * ag[k:k + 1, :]

    return _pc(body, name, (SSM_GROUPS, nc),
               [sp["x"], sp["b"], sp["c"], sp["col"], sp["col"], sp["row"], sp["row"], sp["grp"], sp["grow"], sp["hs"], sp["x"]],
               [sp["x"], sp["bc"], sp["bc"], sp["row"], sp["grow"], sp["grp"]],
               [_sds((s, SSM_INNER)), _sds((s, 1024)), _sds((s, 1024)), _sds((SSM_GROUPS, 4, s)),
                _sds((SSM_GROUPS, 4, LANES)), _sds((SSM_GROUPS, 1, 256))],
               [pltpu.VMEM((256, SSM_STATE), F32)])(xbc, xbc, xbc, dtc, acol, drow, arow, dskip, agrp, hs, dy)


def _gnorm_fwd(y, proj, nw, name):
    s = y.shape[0]
    ts = _row_tile(s)
    gw = SSM_INNER // SSM_GROUPS

    def body(y_ref, z_ref, w_ref, o_ref):
        for g in range(SSM_GROUPS):
            sl = slice(g * gw, (g + 1) * gw)
            z = z_ref[:, sl]
            y2 = y_ref[:, sl] * (z * _sigmoid(z))
            r = lax.rsqrt(jnp.mean(y2 * y2, axis=-1, keepdims=True) + RMS_EPS)
            o_ref[:, sl] = ((y2 * r) * w_ref[:, sl]).astype(BF16)

    row = pl.BlockSpec((ts, SSM_INNER), lambda i: (i, 0))
    return _pc(body, name, (s // ts,), [row, row, pl.BlockSpec((1, SSM_INNER), lambda i: (0, 0))], row,
               _sds((s, SSM_INNER), BF16))(y, proj, nw)


def _gnorm_bwd(y, proj, nw, dyn, name):
    s = y.shape[0]
    ts = _row_tile(s)
    gw = SSM_INNER // SSM_GROUPS

    def body(y_ref, z_ref, w_ref, d_ref, dy_ref, dz_ref, dw_ref):
        @pl.when(pl.program_id(0) == 0)
        def _():
            dw_ref[...] = jnp.zeros_like(dw_ref)

        for g in range(SSM_GROUPS):
            sl = slice(g * gw, (g + 1) * gw)
            z = z_ref[:, sl]
            yv = y_ref[:, sl]
            sg = _sigmoid(z)
            sz = z * sg
            y2 = yv * sz
            r = lax.rsqrt(jnp.mean(y2 * y2, axis=-1, keepdims=True) + RMS_EPS)
            yn = y2 * r
            dout = d_ref[:, sl]
            dyg = dout * w_ref[:, sl]
            dy2 = r * (dyg - yn * jnp.mean(dyg * yn, axis=-1, keepdims=True))
            dy_ref[:, sl] = dy2 * sz
            dz_ref[:, sl] = (dy2 * yv * (sg * (1.0 + z * (1.0 - sg)))).astype(BF16)
            dw_ref[:, sl] += jnp.sum(dout * yn, axis=0, keepdims=True)

    row = pl.BlockSpec((ts, SSM_INNER), lambda i: (i, 0))
    vec = pl.BlockSpec((1, SSM_INNER), lambda i: (0, 0))
    return _pc(body, name, (s // ts,), [row, row, vec, row], [row, row, vec],
               [_sds((s, SSM_INNER)), _sds((s, SSM_INNER), BF16), _sds((1, SSM_INNER))])(y, proj, nw, dyn)


def _head_layouts(v, s):
    return v.reshape(s, SSM_GROUPS, 4).transpose(1, 0, 2), v.T.reshape(SSM_GROUPS, 4, s)


def _ssd_fwd(x, norm_w, w_in, conv_w, conv_b, dt_bias, a_log, d_skip, gnorm_w, w_out):
    s = x.shape[0]
    h = _rms_fwd(x, norm_w, "ssd_norm")
    proj = _mm(h, w_in, "nn", "ssd_in")
    xbc = _ssd_conv_fwd(proj, conv_w, conv_b, "ssd_conv")
    pad = ((0, 0), (0, LANES - SSM_HEADS))
    a_neg = -jnp.exp(a_log)
    bias = jnp.pad(dt_bias, pad)
    dt, acum = _ssd_dt_fwd(proj, bias, jnp.pad(a_neg, pad), "ssd_dt")
    dtc, drow = _head_layouts(dt[:, :SSM_HEADS], s)
    acol, arow = _head_layouts(acum[:, :SSM_HEADS], s)
    dskip = jnp.repeat(d_skip.reshape(SSM_GROUPS, 1, 4), HEAD_DIM, axis=2)
    y, hs = _ssd_scan_fwd(xbc, dtc, acol, drow, arow, dskip, "ssd_scan")
    yn = _gnorm_fwd(y, proj, gnorm_w, "ssd_gnorm")
    out = _mm(yn, w_out, "nn", "ssd_out", res=x)
    return out, (x, h, proj, xbc, bias, a_neg, dtc, acol, drow, arow, dskip, y, hs, yn)


def _ssd_bwd(dout, saved, norm_w, w_in, conv_w, conv_b, gnorm_w, w_out):
    x, h, proj, xbc, bias, a_neg, dtc, acol, drow, arow, dskip, y, hs, yn = saved
    s = x.shape[0]
    dyn = _mm(dout, w_out, "nt", "ssd_dyn")
    g_out = _mm(yn, dout, "tn", "ssd_gout", out_dtype=BF16)
    dy, dz, g_gnorm = _gnorm_bwd(y, proj, gnorm_w, dyn, "ssd_dgnorm")
    agrp = jnp.broadcast_to(a_neg.reshape(SSM_GROUPS, 4, 1), (SSM_GROUPS, 4, LANES))
    dxs, db, dc, ddt_row, da_acc, dd_acc = _ssd_scan_bwd(xbc, dtc, acol, drow, arow, dskip, agrp, hs, dy, "ssd_dscan")
    dxbc = jnp.concatenate([dxs, db, dc], axis=1)
    gact, g_cb = _ssd_conv_bwd_act(proj, dxbc, conv_w, conv_b, "ssd_dconv_act")
    dpre, g_cw = _ssd_conv_bwd_in(proj, gact, conv_w, "ssd_dconv_in")
    ddt = jnp.pad(ddt_row.reshape(SSM_HEADS, s).T, ((0, 0), (0, LANES - SSM_HEADS)))
    ddtraw, g_dtb = _ssd_dt_bwd(proj, bias, ddt, "ssd_ddt")
    dproj = jnp.concatenate([dz, dpre, ddtraw], axis=1)
    g_in = _mm(h, dproj, "tn", "ssd_gin", out_dtype=BF16)
    dh = _mm(dproj, w_in, "nt", "ssd_dh")
    dx, g_norm = _rms_bwd(x, norm_w, dh, dout, "ssd_dnorm")
    g_alog = jnp.sum(da_acc, axis=-1).reshape(1, SSM_HEADS)
    g_d = jnp.sum(dd_acc.reshape(SSM_GROUPS, 4, HEAD_DIM), axis=-1).reshape(1, SSM_HEADS)
    return dx, g_norm, g_in[:, :SSM_IN], g_cw, g_cb, g_dtb[:, :SSM_HEADS], g_alog, g_d, g_gnorm, g_out


def _loss_head(y, target, name):
    s, d = y.shape
    ts = _row_tile(s)

    def body(y_ref, t_ref, dy_ref, l_ref):
        @pl.when(pl.program_id(0) == 0)
        def _():
            l_ref[...] = jnp.zeros_like(l_ref)

        e = y_ref[...] - t_ref[...]
        dy_ref[...] = e * (1.0 / d)
        part = jnp.sum(jnp.sum(e * e, axis=-1, keepdims=True), axis=0, keepdims=True) * (0.5 / d)
        l_ref[...] += jnp.broadcast_to(part, l_ref.shape)

    row = pl.BlockSpec((ts, d), lambda i: (i, 0))
    dy, lacc = _pc(body, name, (s // ts,), [row, row], [row, pl.BlockSpec((8, LANES), lambda i: (0, 0))],
                   [_sds((s, d)), _sds((8, LANES))])(y, target)
    return lacc[0, 0], dy


def _local_step(x, target, w, gather_rest=None, scatter_first=None):
    saved = []
    received = None
    for i in range(DEPTH):
        kind, j = i % 3, i // 3
        mn = w["mix_norm"][i:i + 1]
        if kind == 0:
            x, sv = _conv_fwd(x, mn, w["conv_w_in"][j], w["conv_w_dw"][j], w["conv_w_out"][j], str(i))
        elif kind == 1:
            hosted = None if gather_rest is None else (gather_rest[0], True)
            x, sv, got = _fox_fwd(x, mn, w["fox_w_in"], w["fox_b_f"], w["fox_q_gain"], w["fox_k_gain"], w["fox_w_out"], hosted)
            if gather_rest is not None:
                w = gather_rest[1](w, got)
        else:
            x, sv = _ssd_fwd(x, mn, w["ssd_w_in"], w["ssd_conv_w"], w["ssd_conv_b"], w["ssd_dt_bias"],
                             w["ssd_a_log"], w["ssd_d"], w["ssd_norm_w"], w["ssd_w_out"])
        x, sf = _ffn_fwd(x, w["ffn_norm"][i:i + 1], w["ffn_w_gu"][i], w["ffn_w_down"][i], str(i))
        saved.append((sv, sf))
    loss, dx = _loss_head(x, target, "loss_head")
    g = {k: [None] * n for k, n in (("mix_norm", DEPTH), ("ffn_norm", DEPTH), ("ffn_w_gu", DEPTH), ("ffn_w_down", DEPTH),
                                    ("conv_w_in", 2), ("conv_w_dw", 2), ("conv_w_out", 2))}
    for i in reversed(range(DEPTH)):
        kind, j = i % 3, i // 3
        sv, sf = saved[i]
        dx, g["ffn_norm"][i], g["ffn_w_gu"][i], g["ffn_w_down"][i] = _ffn_bwd(
            dx, sf, w["ffn_norm"][i:i + 1], w["ffn_w_gu"][i], w["ffn_w_down"][i], str(i))
        mn = w["mix_norm"][i:i + 1]
        if kind == 0:
            dx, g["mix_norm"][i], g["conv_w_in"][j], g["conv_w_dw"][j], g["conv_w_out"][j] = _conv_bwd(
                dx, sv, mn, w["conv_w_in"][j], w["conv_w_dw"][j], w["conv_w_out"][j], str(i))
        elif kind == 1:
            hosted = None if scatter_first is None else (scatter_first(g), False)
            (dx, g["mix_norm"][i], g["fox_w_in"], g["fox_b_f"], g["fox_q_gain"], g["fox_k_gain"],
             g["fox_w_out"], received) = _fox_bwd(dx, sv, mn, w["fox_w_in"], w["fox_w_out"], hosted)
        else:
            (dx, g["mix_norm"][i], g["ssd_w_in"], g["ssd_conv_w"], g["ssd_conv_b"], g["ssd_dt_bias"], g["ssd_a_log"],
             g["ssd_d"], g["ssd_norm_w"], g["ssd_w_out"]) = _ssd_bwd(
                 dx, sv, mn, w["ssd_w_in"], w["ssd_conv_w"], w["ssd_conv_b"], w["ssd_norm_w"], w["ssd_w_out"])
    g["mix_norm"] = jnp.concatenate(g["mix_norm"], axis=0)
    g["ffn_norm"] = jnp.concatenate(g["ffn_norm"], axis=0)
    g["conv_w_dw"] = jnp.stack(g["conv_w_dw"], axis=0)
    g["ssd_conv_w"] = g["ssd_conv_w"][None]
    return loss, dx, g, received


def _mesh_position():
    return lax.axis_index("x") * 4 + lax.axis_index("y") * 2 + lax.axis_index("c")


def _device_of(t):
    return (lax.shift_right_logical(t, 2), lax.bitwise_and(lax.shift_right_logical(t, 1), 1), lax.bitwise_and(t, 1))


def _exchange_copies(srcs_of, out_refs, send_sems, recv_sems, local_sems):
    me = _mesh_position()
    na = len(out_refs)
    locals_ = [pltpu.make_async_copy(srcs_of(a, me), out_refs[a].at[me], local_sems.at[a]) for a in range(na)]
    sends, arrivals = [], []
    for j in range(1, NDEV):
        t = lax.rem(me + j, NDEV)
        frm = lax.rem(me + NDEV - j, NDEV)
        for a in range(na):
            sends.append(pltpu.make_async_remote_copy(
                src_ref=srcs_of(a, t), dst_ref=out_refs[a].at[me], send_sem=send_sems.at[a, j - 1],
                recv_sem=recv_sems.at[a, j - 1], device_id=_device_of(t), device_id_type=pl.DeviceIdType.MESH))
            arrivals.append(pltpu.make_async_remote_copy(
                src_ref=srcs_of(a, me), dst_ref=out_refs[a].at[frm], send_sem=send_sems.at[a, j - 1],
                recv_sem=recv_sems.at[a, j - 1], device_id=_device_of(frm), device_id_type=pl.DeviceIdType.MESH))
    return locals_, sends, arrivals


def _exchange_start(copies):
    locals_, sends, _ = copies
    for cp in locals_ + sends:
        cp.start()


def _exchange_wait(copies):
    locals_, sends, arrivals = copies
    for cp in sends:
        cp.wait_send()
    for cp in arrivals:
        cp.wait_recv()
    for cp in locals_:
        cp.wait()


def _exchange_run(srcs_of, out_refs, send_sems, recv_sems, local_sems):
    copies = _exchange_copies(srcs_of, out_refs, send_sems, recv_sems, local_sems)
    _exchange_start(copies)
    _exchange_wait(copies)


def _exchange_parts(arrays, gather):
    na = len(arrays)
    outs = [_sds(((NDEV,) + a.shape) if gather else a.shape, a.dtype) for a in arrays]
    sems = [pltpu.SemaphoreType.DMA((na, NDEV - 1)), pltpu.SemaphoreType.DMA((na, NDEV - 1)), pltpu.SemaphoreType.DMA((na,))]
    pick = (lambda srcs: (lambda a, t: srcs[a])) if gather else (lambda srcs: (lambda a, t: srcs[a].at[t]))
    return pick, outs, sems


def _exchange(arrays, name, gather):
    na = len(arrays)
    pick, outs, sems = _exchange_parts(arrays, gather)

    def body(*refs):
        _exchange_run(pick(refs[:na]), refs[na:2 * na], *refs[2 * na:])

    hbm = pl.BlockSpec(memory_space=pl.ANY)
    return pl.pallas_call(body, name=name, in_specs=[hbm] * na, out_specs=[hbm] * na, out_shape=outs, scratch_shapes=sems)(*arrays)


def _all_sum_small(pack, name):
    def body(src_ref, out_ref, buf_ref, send_sems, recv_sems, local_sems):
        _exchange_run(lambda a, t: src_ref, [buf_ref], send_sems, recv_sems, local_sems)
        acc = buf_ref[0]
        for d in range(1, NDEV):
            acc = acc + buf_ref[d]
        out_ref[...] = acc

    vmem = pl.BlockSpec(memory_space=pltpu.VMEM)
    return pl.pallas_call(
        body, name=name, in_specs=[vmem], out_specs=vmem, out_shape=_sds(pack.shape, pack.dtype),
        scratch_shapes=[pltpu.VMEM((NDEV,) + pack.shape, pack.dtype), pltpu.SemaphoreType.DMA((1, NDEV - 1)),
                        pltpu.SemaphoreType.DMA((1, NDEV - 1)), pltpu.SemaphoreType.DMA((1,))])(pack)


def _sum_slabs(slabs, name):
    _, r, c = slabs.shape
    tr = r
    for cand in (256, 352):
        if r % cand == 0:
            tr = cand
            break

    def body(s_ref, o_ref):
        acc = s_ref[0].astype(F32)
        for d in range(1, NDEV):
            acc = acc + s_ref[d].astype(F32)
        o_ref[...] = acc

    return _pc(body, name, (r // tr,), [pl.BlockSpec((NDEV, tr, c), lambda i: (0, i, 0))],
               pl.BlockSpec((tr, c), lambda i: (i, 0)), _sds((r, c)))(slabs)


def _adamw(wt, g, m, v, name):
    shape = wt.shape
    w2, g2, m2, v2 = (a.reshape(-1, shape[-1]) for a in (wt, g, m, v))
    r, c = w2.shape
    tr = r
    for cand in (512, 352, 256):
        if r % cand == 0:
            tr = cand
            break
    c1 = 1.0 - ADAM_B1 ** ADAM_STEP
    c2 = 1.0 - ADAM_B2 ** ADAM_STEP

    def body(w_ref, g_ref, m_ref, v_ref, d_ref, mo_ref, vo_ref):
        gv = g_ref[...]
        mn = ADAM_B1 * m_ref[...] + (1.0 - ADAM_B1) * gv
        vn = ADAM_B2 * v_ref[...] + (1.0 - ADAM_B2) * (gv * gv)
        mo_ref[...] = mn
        vo_ref[...] = vn
        d_ref[...] = -ADAM_LR * ((mn / c1) / (jnp.sqrt(vn / c2) + ADAM_EPS) + ADAM_WD * w_ref[...])

    spec = pl.BlockSpec((tr, c), lambda i: (i, 0))
    outs = _pc(body, name, (r // tr,), [spec] * 4, [spec] * 3, [_sds((r, c))] * 3)(w2, g2, m2, v2)
    return tuple(o.reshape(shape) for o in outs)


_NAMES = ["mix_norm", "ffn_norm", "ffn_w_gu", "ffn_w_down", "conv_w_in", "conv_w_dw", "conv_w_out", "fox_w_in", "fox_b_f",
          "fox_q_gain", "fox_k_gain", "fox_w_out", "ssd_w_in", "ssd_conv_w", "ssd_conv_b", "ssd_dt_bias", "ssd_a_log",
          "ssd_d", "ssd_norm_w", "ssd_w_out"]
_MATRICES = ["ffn_w_gu", "ffn_w_down", "conv_w_in", "conv_w_out", "fox_w_in", "fox_w_out", "ssd_w_in", "ssd_w_out"]
_VECTORS = {"conv_w_dw": 2, "ssd_conv_w": 2, "ssd_conv_b": 1, "ssd_norm_w": 1}
_REPLICATED = ["mix_norm", "ffn_norm", "fox_b_f", "fox_q_gain", "fox_k_gain", "ssd_dt_bias", "ssd_a_log", "ssd_d"]


def _to_rows(flat):
    n = flat.shape[0]
    rows = -(-n // (8 * D_MODEL)) * 8
    return jnp.pad(flat, (0, rows * D_MODEL - n)).reshape(rows, D_MODEL)


def _full_shape(local_shape, axis):
    shp = list(local_shape)
    shp[axis] *= NDEV
    return tuple(shp)


def _cols_from_blocks(g):
    return jnp.moveaxis(g, 0, 1).reshape(g.shape[1], NDEV * g.shape[2])


def _blocks_from_cols(full):
    k, n8 = full.shape
    return jnp.moveaxis(full.reshape(k, NDEV, n8 // NDEV), 1, 0)


def kernel(x, mix_norm, ffn_norm, ffn_w_gu, ffn_w_down, conv_w_in, conv_w_dw, conv_w_out, fox_w_in, fox_b_f, fox_q_gain, fox_k_gain, fox_w_out, ssd_w_in, ssd_conv_w, ssd_conv_b, ssd_dt_bias, ssd_a_log, ssd_d, ssd_norm_w, ssd_w_out, loss_target, m_mix_norm, m_ffn_norm, m_ffn_w_gu, m_ffn_w_down, m_conv_w_in, m_conv_w_dw, m_conv_w_out, m_fox_w_in, m_fox_b_f, m_fox_q_gain, m_fox_k_gain, m_fox_w_out, m_ssd_w_in, m_ssd_conv_w, m_ssd_conv_b, m_ssd_dt_bias, m_ssd_a_log, m_ssd_d, m_ssd_norm_w, m_ssd_w_out, v_mix_norm, v_ffn_norm, v_ffn_w_gu, v_ffn_w_down, v_conv_w_in, v_conv_w_dw, v_conv_w_out, v_fox_w_in, v_fox_b_f, v_fox_q_gain, v_fox_k_gain, v_fox_w_out, v_ssd_w_in, v_ssd_conv_w, v_ssd_conv_b, v_ssd_dt_bias, v_ssd_a_log, v_ssd_d, v_ssd_norm_w, v_ssd_w_out):
    local = dict(mix_norm=mix_norm, ffn_norm=ffn_norm, ffn_w_gu=ffn_w_gu, ffn_w_down=ffn_w_down, conv_w_in=conv_w_in,
                 conv_w_dw=conv_w_dw, conv_w_out=conv_w_out, fox_w_in=fox_w_in, fox_b_f=fox_b_f, fox_q_gain=fox_q_gain,
                 fox_k_gain=fox_k_gain, fox_w_out=fox_w_out, ssd_w_in=ssd_w_in, ssd_conv_w=ssd_conv_w, ssd_conv_b=ssd_conv_b,
                 ssd_dt_bias=ssd_dt_bias, ssd_a_log=ssd_a_log, ssd_d=ssd_d, ssd_norm_w=ssd_norm_w, ssd_w_out=ssd_w_out)
    mom = dict(zip(_NAMES, [m_mix_norm, m_ffn_norm, m_ffn_w_gu, m_ffn_w_down, m_conv_w_in, m_conv_w_dw, m_conv_w_out, m_fox_w_in,
                            m_fox_b_f, m_fox_q_gain, m_fox_k_gain, m_fox_w_out, m_ssd_w_in, m_ssd_conv_w, m_ssd_conv_b,
                            m_ssd_dt_bias, m_ssd_a_log, m_ssd_d, m_ssd_norm_w, m_ssd_w_out]))
    var = dict(zip(_NAMES, [v_mix_norm, v_ffn_norm, v_ffn_w_gu, v_ffn_w_down, v_conv_w_in, v_conv_w_dw, v_conv_w_out, v_fox_w_in,
                            v_fox_b_f, v_fox_q_gain, v_fox_k_gain, v_fox_w_out, v_ssd_w_in, v_ssd_conv_w, v_ssd_conv_b,
                            v_ssd_dt_bias, v_ssd_a_log, v_ssd_d, v_ssd_norm_w, v_ssd_w_out]))

    shard = {k: local[k].astype(BF16) for k in _MATRICES}
    vec_pack = _to_rows(jnp.concatenate([local[k].reshape(-1) for k in _VECTORS]))
    first = _exchange([shard["ffn_w_gu"][0:1], shard["ffn_w_down"][0:1], shard["conv_w_in"][0:1], shard["conv_w_out"][0:1],
                       shard["fox_w_in"], shard["fox_w_out"], vec_pack], "gather_first", True)
    gvec = first[6].reshape(NDEV, -1)
    full = {k: local[k] for k in _REPLICATED}
    off = 0
    for k, axis in _VECTORS.items():
        n = local[k].size
        blk = jnp.moveaxis(gvec[:, off:off + n].reshape((NDEV,) + local[k].shape), 0, axis)
        full[k] = blk.reshape(_full_shape(local[k].shape, axis))
        off += n
    full["ssd_conv_w"] = full["ssd_conv_w"][0]
    full["ffn_w_gu"] = [first[0][:, 0]]
    full["ffn_w_down"] = [first[1][:, 0].reshape(4, FF_BLOCK, D_MODEL)]
    full["conv_w_in"] = [first[2][:, 0]]
    full["conv_w_out"] = [first[3][:, 0].reshape(D_MODEL, D_MODEL)]
    full["fox_w_in"] = jnp.pad(_cols_from_blocks(first[4][:, 0]), ((0, 0), (0, FOX_IN_PAD - FOX_IN)))
    full["fox_w_out"] = first[5].reshape(D_MODEL, D_MODEL)

    rest = [shard["ffn_w_gu"][1:], shard["ffn_w_down"][1:], shard["conv_w_in"][1:], shard["conv_w_out"][1:],
            shard["ssd_w_in"], shard["ssd_w_out"]]

    def finish(w, got):
        w = dict(w)
        w["ffn_w_gu"] = w["ffn_w_gu"] + [got[0][:, i] for i in range(DEPTH - 1)]
        w["ffn_w_down"] = w["ffn_w_down"] + [got[1][:, i].reshape(4, FF_BLOCK, D_MODEL) for i in range(DEPTH - 1)]
        w["conv_w_in"] = w["conv_w_in"] + [got[2][:, 0]]
        w["conv_w_out"] = w["conv_w_out"] + [got[3][:, 0].reshape(D_MODEL, D_MODEL)]
        w["ssd_w_in"] = jnp.pad(_cols_from_blocks(got[4][:, 0]), ((0, 0), (0, SSM_IN_PAD - SSM_IN)))
        w["ssd_w_out"] = got[5].reshape(SSM_INNER, D_MODEL)
        return w

    def early_slabs(g):
        return ([g["ffn_w_gu"][i] for i in range(1, DEPTH)]
                + [g["ffn_w_down"][i].reshape(NDEV, D_FF // NDEV, D_MODEL) for i in range(1, DEPTH)]
                + [g["conv_w_in"][1], g["conv_w_out"][1].reshape(NDEV, D_MODEL // NDEV, D_MODEL),
                   _blocks_from_cols(g["ssd_w_in"]), g["ssd_w_out"].reshape(NDEV, SSM_INNER // NDEV, D_MODEL)])

    loss_part, dx, grads, early = _local_step(x[0], loss_target[0], full, (rest, finish), early_slabs)

    late = _exchange([grads["ffn_w_gu"][0], grads["ffn_w_down"][0].reshape(NDEV, D_FF // NDEV, D_MODEL), grads["conv_w_in"][0],
                      grads["conv_w_out"][0].reshape(NDEV, D_MODEL // NDEV, D_MODEL), _blocks_from_cols(grads["fox_w_in"]),
                      grads["fox_w_out"].reshape(NDEV, D_MODEL // NDEV, D_MODEL)], "scatter_last", False)
    se = [_sum_slabs(r, f"sum_early_{n}") for n, r in enumerate(early)]
    sl = [_sum_slabs(r, f"sum_late_{n}") for n, r in enumerate(late)]
    shard_grad = {
        "ffn_w_gu": jnp.stack([sl[0]] + se[0:3]), "ffn_w_down": jnp.stack([sl[1]] + se[3:6]),
        "conv_w_in": jnp.stack([sl[2], se[6]]), "conv_w_out": jnp.stack([sl[3], se[7]]),
        "fox_w_in": sl[4][None], "fox_w_out": sl[5][None], "ssd_w_in": se[8][None], "ssd_w_out": se[9][None]}

    small_names = _REPLICATED + list(_VECTORS)
    small = [jnp.reshape(loss_part, (1,))] + [grads[k].reshape(-1) for k in small_names]
    total = _all_sum_small(_to_rows(jnp.concatenate(small)), "sum_small").reshape(-1)
    loss = total[0]
    off = 1
    me = _mesh_position()
    for k, part in zip(small_names, small[1:]):
        gk = total[off:off + part.shape[0]]
        off += part.shape[0]
        if k in _VECTORS:
            axis = _VECTORS[k]
            shp = local[k].shape
            gfull = gk.reshape(shp[:axis] + (NDEV, shp[axis]) + shp[axis + 1:])
            shard_grad[k] = lax.dynamic_index_in_dim(gfull, me, axis, keepdims=False)
        else:
            shard_grad[k] = gk.reshape(local[k].shape)

    deltas, new_m, new_v = {}, {}, {}
    for k in _NAMES:
        deltas[k], new_m[k], new_v[k] = _adamw(local[k], shard_grad[k], mom[k], var[k], f"adamw_{k}")
    return (loss, dx[None], *[shard_grad[k] for k in _NAMES], *[deltas[k] for k in _NAMES],
            *[new_m[k] for k in _NAMES], *[new_v[k] for k in _NAMES])
```

```python
import numpy as np

import jax
import jax.numpy as jnp
from jax import lax
from jax.experimental import pallas as pl
from jax.experimental.pallas import tpu as pltpu

F32 = jnp.float32
BF16 = jnp.bfloat16
HI = lax.Precision.HIGHEST

NDEV = 8
D_MODEL = 1024
DEPTH = 4
D_FF = 2816
FF_BLOCK = 2 * D_FF // NDEV
CONV_BLOCK = 3 * D_MODEL // NDEV
RMS_EPS = 1e-6
HEAD_DIM = 64
ATTN_HEADS = 16
FOX_IN = 3 * D_MODEL + ATTN_HEADS
FOX_IN_PAD = 3200
SSM_INNER = 2048
SSM_HEADS = 32
SSM_GROUPS = 8
SSM_STATE = 128
SSM_CHUNK = 128
SSM_CONV_DIM = 4096
SSM_IN = SSM_INNER + SSM_CONV_DIM + SSM_HEADS
SSM_IN_PAD = 6272
LANES = 128
V7X_VMEM_BYTES = 64 * 1024 * 1024
VMEM_LIMIT_BYTES = (V7X_VMEM_BYTES * 3) // 4
ATTN_BWD_VMEM_BYTES = (V7X_VMEM_BYTES * 7) // 8
LOG2E = 1.4426950408889634
LN2 = 0.6931471805599453
ATTN_TILE = 1024
ATTN_ROWS = 32

ADAM_LR = 0.001
ADAM_B1 = 0.9
ADAM_B2 = 0.999
ADAM_EPS = 1e-08
ADAM_WD = 0.01
ADAM_STEP = 10

_TILE_CANDIDATES = (1024, 1408, 896, 768, 640, 512, 384, 256, 128)


def _pick_tile(n):
    for c in _TILE_CANDIDATES:
        if n % c == 0:
            return c
    raise ValueError(f"no tile for {n}")


def _params(ngrid):
    return pltpu.CompilerParams(dimension_semantics=("arbitrary",) * ngrid, vmem_limit_bytes=VMEM_LIMIT_BYTES)


def _pc(body, name, grid, in_specs, out_specs, out_shape, scratch=()):
    return pl.pallas_call(
        body, name=name, grid=grid, in_specs=in_specs, out_specs=out_specs, out_shape=out_shape,
        scratch_shapes=list(scratch), compiler_params=_params(len(grid)))


def _dot(a, b, ca, cb, prec=None):
    return lax.dot_general(a, b, (((ca,), (cb,)), ((), ())), preferred_element_type=F32, precision=prec)


def _sds(shape, dtype=F32):
    return jax.ShapeDtypeStruct(shape, dtype)


def _row_tile(s, want=256):
    return want if s % want == 0 else s


def _sigmoid(x):
    return 1.0 / (1.0 + jnp.exp(-x))


def _softplus(x):
    return jnp.maximum(x, 0.0) + jnp.log(1.0 + jnp.exp(-jnp.abs(x)))


def _mm_spec(a, b, name, grid, a_spec, b_spec, o_spec, out, ca, cb, acc_shape, drop=(0, 0, 0), res=None, r_spec=None):
    nk = grid[2]
    da, db, do_ = drop
    has_res = res is not None

    def body(*refs):
        if has_res:
            a_ref, b_ref, r_ref, o_ref, acc_ref = refs
        else:
            a_ref, b_ref, o_ref, acc_ref = refs
        k = pl.program_id(2)

        @pl.when(k == 0)
        def _():
            acc_ref[...] = jnp.zeros_like(acc_ref)

        av = a_ref[(0,) * da] if da else a_ref[...]
        bv = b_ref[(0,) * db] if db else b_ref[...]
        acc_ref[...] += _dot(av.astype(BF16), bv.astype(BF16), ca, cb)

        @pl.when(k == nk - 1)
        def _():
            val = acc_ref[...]
            if has_res:
                val = val + r_ref[...]
            if do_:
                o_ref[(0,) * do_] = val.astype(out.dtype)
            else:
                o_ref[...] = val.astype(out.dtype)

    in_specs = [a_spec, b_spec] + ([r_spec] if has_res else [])
    args = (a, b) + ((res,) if has_res else ())
    return _pc(body, name, grid, in_specs, o_spec, out, [pltpu.VMEM(acc_shape, F32)])(*args)


def _mm(a, b, mode, name, out_dtype=F32, res=None):
    if mode == "tn":
        r, m = a.shape
        n = b.shape[1]
        tm, tn, tk = _pick_tile(m), _pick_tile(n), _pick_tile(r)
        grid = (m // tm, n // tn, r // tk)
        a_spec = pl.BlockSpec((tk, tm), lambda i, j, k: (k, i))
        b_spec = pl.BlockSpec((tk, tn), lambda i, j, k: (k, j))
        ca, cb = 0, 0
    else:
        m, kd = a.shape
        n = b.shape[1] if mode == "nn" else b.shape[0]
        tm, tn, tk = _pick_tile(m), _pick_tile(n), _pick_tile(kd)
        grid = (m // tm, n // tn, kd // tk)
        a_spec = pl.BlockSpec((tm, tk), lambda i, j, k: (i, k))
        if mode == "nn":
            b_spec = pl.BlockSpec((tk, tn), lambda i, j, k: (k, j))
            ca, cb = 1, 0
        else:
            b_spec = pl.BlockSpec((tn, tk), lambda i, j, k: (j, k))
            ca, cb = 1, 1
    o_spec = pl.BlockSpec((tm, tn), lambda i, j, k: (i, j))
    return _mm_spec(a, b, name, grid, a_spec, b_spec, o_spec, _sds((m, n), out_dtype), ca, cb, (tm, tn), res=res, r_spec=o_spec)


def _rms_fwd(x, w, name):
    s, d = x.shape
    ts = _row_tile(s)

    def body(x_ref, w_ref, o_ref):
        xv = x_ref[...]
        r = lax.rsqrt(jnp.mean(xv * xv, axis=-1, keepdims=True) + RMS_EPS)
        o_ref[...] = ((xv * r) * w_ref[...]).astype(BF16)

    row = pl.BlockSpec((ts, d), lambda i: (i, 0))
    return _pc(body, name, (s // ts,), [row, pl.BlockSpec((1, d), lambda i: (0, 0))], row, _sds((s, d), BF16))(x, w)


def _mm_dnorm(a, b, name, nk, a_spec, b_spec, ca, cb, drop, x, w, dres):
    s, d = x.shape
    tm = _pick_tile(s)
    da, db = drop

    def body(a_ref, b_ref, x_ref, w_ref, r_ref, dx_ref, dw_ref, acc_ref):
        i = pl.program_id(0)
        k = pl.program_id(2)

        @pl.when(k == 0)
        def _():
            acc_ref[...] = jnp.zeros_like(acc_ref)

        av = a_ref[(0,) * da] if da else a_ref[...]
        bv = b_ref[(0,) * db] if db else b_ref[...]
        acc_ref[...] += _dot(av.astype(BF16), bv.astype(BF16), ca, cb)

        @pl.when(k == nk - 1)
        def _():
            dhv = acc_ref[...]
            xv = x_ref[...]
            r = lax.rsqrt(jnp.mean(xv * xv, axis=-1, keepdims=True) + RMS_EPS)
            xhat = xv * r
            g = dhv * w_ref[...]
            dx_ref[...] = r_ref[...] + r * (g - xhat * jnp.mean(g * xhat, axis=-1, keepdims=True))
            part = jnp.sum(dhv * xhat, axis=0, keepdims=True)

            @pl.when(i == 0)
            def _():
                dw_ref[...] = part

            @pl.when(i > 0)
            def _():
                dw_ref[...] += part

    row = pl.BlockSpec((tm, d), lambda i, j, k: (i, 0))
    vec = pl.BlockSpec((1, d), lambda i, j, k: (0, 0))
    return _pc(body, name, (s // tm, 1, nk), [a_spec, b_spec, row, vec, row], [row, vec], [_sds((s, d)), _sds((1, d))],
               [pltpu.VMEM((tm, d), F32)])(a, b, x, w, dres)


def _mm_dnorm_nt(dproj, w_in, name, x, w, dres):
    tm = _pick_tile(x.shape[0])
    tk = _pick_tile(dproj.shape[1])
    return _mm_dnorm(dproj, w_in, name, dproj.shape[1] // tk, pl.BlockSpec((tm, tk), lambda i, j, k: (i, k)),
                     pl.BlockSpec((D_MODEL, tk), lambda i, j, k: (0, k)), 1, 1, (0, 0), x, w, dres)


def _ffn_gate_up(h, w_gu, name):
    s = h.shape[0]
    tm = _pick_tile(s)

    def body(h_ref, wg_ref, wu_ref, gu_ref, a_ref):
        hv = h_ref[...]
        g = _dot(hv, wg_ref[0], 1, 0)
        u = _dot(hv, wu_ref[0], 1, 0)
        gu_ref[0, 0] = g.astype(BF16)
        gu_ref[0, 1] = u.astype(BF16)
        a_ref[0] = (g * _sigmoid(g) * u).astype(BF16)

    wblk = lambda off: pl.BlockSpec((1, D_MODEL, FF_BLOCK), lambda i, k: (k + off, 0, 0))
    return _pc(body, name, (s // tm, 4), [pl.BlockSpec((tm, D_MODEL), lambda i, k: (i, 0)), wblk(0), wblk(4)],
               [pl.BlockSpec((1, 2, tm, FF_BLOCK), lambda i, k: (k, 0, i, 0)), pl.BlockSpec((1, tm, FF_BLOCK), lambda i, k: (k, i, 0))],
               [_sds((4, 2, s, FF_BLOCK), BF16), _sds((4, s, FF_BLOCK), BF16)])(h, w_gu, w_gu)


def _ffn_dgate_up(dy, w_down, gu, name):
    s = dy.shape[0]
    tm = _pick_tile(s)

    def body(dy_ref, w_ref, gu_ref, o_ref):
        dav = _dot(dy_ref[...].astype(BF16), w_ref[0], 1, 1)
        g = gu_ref[0, 0].astype(F32)
        u = gu_ref[0, 1].astype(F32)
        sg = _sigmoid(g)
        o_ref[0, 0] = (dav * u * (sg * (1.0 + g * (1.0 - sg)))).astype(BF16)
        o_ref[0, 1] = (dav * (g * sg)).astype(BF16)

    pair = pl.BlockSpec((1, 2, tm, FF_BLOCK), lambda i, k: (k, 0, i, 0))
    return _pc(body, name, (s // tm, 4),
               [pl.BlockSpec((tm, D_MODEL), lambda i, k: (i, 0)), pl.BlockSpec((1, FF_BLOCK, D_MODEL), lambda i, k: (k, 0, 0)), pair],
               pair, _sds((4, 2, s, FF_BLOCK), BF16))(dy, w_down, gu)


def _ffn_fwd(x, norm_w, w_gu, w_down, tag):
    s = x.shape[0]
    tm = _pick_tile(s)
    h = _rms_fwd(x, norm_w, f"ffn_norm_{tag}")
    gu, a = _ffn_gate_up(h, w_gu, f"ffn_gu_{tag}")
    xspec = pl.BlockSpec((tm, D_MODEL), lambda i, j, k: (i, 0))
    y = _mm_spec(a, w_down, f"ffn_down_{tag}", (s // tm, 1, 4),
                 pl.BlockSpec((1, tm, FF_BLOCK), lambda i, j, k: (k, i, 0)),
                 pl.BlockSpec((1, FF_BLOCK, D_MODEL), lambda i, j, k: (k, 0, 0)),
                 xspec, _sds((s, D_MODEL)), 1, 0, (tm, D_MODEL), drop=(1, 1, 0), res=x, r_spec=xspec)
    return y, (x, h, gu, a)


def _ffn_bwd(dy, saved, norm_w, w_gu, w_down, tag):
    x, h, gu, a = saved
    s = x.shape[0]
    tm = _pick_tile(s)
    g_down = _mm_spec(a, dy, f"ffn_gdown_{tag}", (4, 1, s // tm),
                      pl.BlockSpec((1, tm, FF_BLOCK), lambda i, j, k: (i, k, 0)),
                      pl.BlockSpec((tm, D_MODEL), lambda i, j, k: (k, 0)),
                      pl.BlockSpec((1, FF_BLOCK, D_MODEL), lambda i, j, k: (i, 0, 0)),
                      _sds((4, FF_BLOCK, D_MODEL), BF16), 0, 0, (FF_BLOCK, D_MODEL), drop=(1, 0, 1))
    dgu = _ffn_dgate_up(dy, w_down, gu, f"ffn_dgu_{tag}")
    g_gu = _mm_spec(h, dgu, f"ffn_ggu_{tag}", (NDEV, 1, s // tm),
                    pl.BlockSpec((tm, D_MODEL), lambda i, j, k: (k, 0)),
                    pl.BlockSpec((1, 1, tm, FF_BLOCK), lambda i, j, k: (i % 4, i // 4, k, 0)),
                    pl.BlockSpec((1, D_MODEL, FF_BLOCK), lambda i, j, k: (i, 0, 0)),
                    _sds((NDEV, D_MODEL, FF_BLOCK), BF16), 0, 0, (D_MODEL, FF_BLOCK), drop=(0, 2, 1))
    dx, g_norm = _mm_dnorm(dgu, w_gu, f"ffn_dh_{tag}", NDEV,
                           pl.BlockSpec((1, 1, tm, FF_BLOCK), lambda i, j, k: (k % 4, k // 4, i, 0)),
                           pl.BlockSpec((1, D_MODEL, FF_BLOCK), lambda i, j, k: (k, 0, 0)), 1, 1, (2, 1), x, norm_w, dy)
    return dx, g_norm, g_gu, g_down


def _prev_rows(cur, halo, j, first):
    rid = lax.broadcasted_iota(jnp.int32, cur.shape, 0)
    hid = lax.broadcasted_iota(jnp.int32, halo.shape, 0)
    out = pltpu.roll(cur, j, 0)
    for t in range(j):
        row = jnp.sum(jnp.where(hid == 8 - j + t, halo, 0.0), axis=0, keepdims=True)
        row = jnp.where(first, 0.0, row)
        out = jnp.where(rid == t, row, out)
    return out


def _next_rows(cur, halo, j, last):
    ts = cur.shape[0]
    rid = lax.broadcasted_iota(jnp.int32, cur.shape, 0)
    hid = lax.broadcasted_iota(jnp.int32, halo.shape, 0)
    out = pltpu.roll(cur, ts - j, 0)
    for t in range(j):
        row = jnp.sum(jnp.where(hid == t, halo, 0.0), axis=0, keepdims=True)
        row = jnp.where(last, 0.0, row)
        out = jnp.where(rid == ts - j + t, row, out)
    return out


def _halo_specs(ts, s, width, col):
    per = ts // 8
    nblk = s // 8
    prev = pl.BlockSpec((8, width), lambda i: (jnp.maximum(i * per - 1, 0), col))
    nxt = pl.BlockSpec((8, width), lambda i: (jnp.minimum((i + 1) * per, nblk - 1), col))
    return prev, nxt


def _cgate_fwd(p, w_dw, name):
    s = p.shape[0]
    d = D_MODEL
    ts = _row_tile(s)
    prev, _ = _halo_specs(ts, s, 3 * d, 0)

    def body(p_ref, h_ref, w_ref, z_ref):
        first = pl.program_id(0) == 0
        b = p_ref[:, :d]
        cv = p_ref[:, d:2 * d] * p_ref[:, 2 * d:]
        hcv = h_ref[:, d:2 * d] * h_ref[:, 2 * d:]
        u = w_ref[2:3, :] * cv + w_ref[1:2, :] * _prev_rows(cv, hcv, 1, first) + w_ref[0:1, :] * _prev_rows(cv, hcv, 2, first)
        z_ref[...] = (b * u).astype(BF16)

    return _pc(body, name, (s // ts,),
               [pl.BlockSpec((ts, 3 * d), lambda i: (i, 0)), prev, pl.BlockSpec((3, d), lambda i: (0, 0))],
               pl.BlockSpec((ts, d), lambda i: (i, 0)), _sds((s, d), BF16))(p, p, w_dw)


def _cgate_bwd(p, dz, w_dw, name):
    s = p.shape[0]
    d = D_MODEL
    ts = _row_tile(s)
    nt = s // ts
    p_prev, p_next = _halo_specs(ts, s, 3 * d, 0)
    _, dz_next = _halo_specs(ts, s, d, 0)

    def body(p_ref, hp_ref, hn_ref, dz_ref, dzn_ref, w_ref, dp_ref, dw_ref):
        i = pl.program_id(0)
        first = i == 0
        last = i == nt - 1
        b = p_ref[:, :d]
        c = p_ref[:, d:2 * d]
        v = p_ref[:, 2 * d:]
        cv = c * v
        hcv = hp_ref[:, d:2 * d] * hp_ref[:, 2 * d:]
        cv1 = _prev_rows(cv, hcv, 1, first)
        cv2 = _prev_rows(cv, hcv, 2, first)
        w0, w1, w2 = w_ref[0:1, :], w_ref[1:2, :], w_ref[2:3, :]
        u = w2 * cv + w1 * cv1 + w0 * cv2
        dzv = dz_ref[...]
        du = dzv * b
        dun = dzn_ref[...] * hn_ref[:, :d]
        dcv = w2 * du + w1 * _next_rows(du, dun, 1, last) + w0 * _next_rows(du, dun, 2, last)
        dp_ref[:, :d] = (dzv * u).astype(BF16)
        dp_ref[:, d:2 * d] = (dcv * v).astype(BF16)
        dp_ref[:, 2 * d:] = (dcv * c).astype(BF16)

        @pl.when(first)
        def _():
            dw_ref[...] = jnp.zeros_like(dw_ref)

        dw_ref[0:1, :] += jnp.sum(du * cv2, axis=0, keepdims=True)
        dw_ref[1:2, :] += jnp.sum(du * cv1, axis=0, keepdims=True)
        dw_ref[2:3, :] += jnp.sum(du * cv, axis=0, keepdims=True)

    wide = pl.BlockSpec((ts, 3 * d), lambda i: (i, 0))
    wspec = pl.BlockSpec((3, d), lambda i: (0, 0))
    return _pc(body, name, (nt,),
               [wide, p_prev, p_next, pl.BlockSpec((ts, d), lambda i: (i, 0)), dz_next, wspec],
               [wide, wspec], [_sds((s, 3 * d), BF16), _sds((3, d))])(p, p, p, dz, dz, w_dw)


def _conv_fwd(x, norm_w, w_in, w_dw, w_out, tag):
    s = x.shape[0]
    tm = _pick_tile(s)
    h = _rms_fwd(x, norm_w, f"conv_norm_{tag}")
    p = _mm_spec(h, w_in, f"conv_in_{tag}", (s // tm, NDEV, 1),
                 pl.BlockSpec((tm, D_MODEL), lambda i, j, k: (i, 0)),
                 pl.BlockSpec((1, D_MODEL, CONV_BLOCK), lambda i, j, k: (j, 0, 0)),
                 pl.BlockSpec((tm, CONV_BLOCK), lambda i, j, k: (i, j)),
                 _sds((s, 3 * D_MODEL)), 1, 0, (tm, CONV_BLOCK), drop=(0, 1, 0))
    z = _cgate_fwd(p, w_dw, f"conv_gate_{tag}")
    y = _mm(z, w_out, "nn", f"conv_out_{tag}", res=x)
    return y, (x, h, p, z)


def _conv_bwd(dy, saved, norm_w, w_in, w_dw, w_out, tag):
    x, h, p, z = saved
    s = x.shape[0]
    tm = _pick_tile(s)
    dz = _mm(dy, w_out, "nt", f"conv_dz_{tag}")
    g_out = _mm(z, dy, "tn", f"conv_gout_{tag}", out_dtype=BF16)
    dp, g_dw = _cgate_bwd(p, dz, w_dw, f"conv_dgate_{tag}")
    g_in = _mm_spec(h, dp, f"conv_gin_{tag}", (NDEV, 1, s // tm),
                    pl.BlockSpec((tm, D_MODEL), lambda i, j, k: (k, 0)),
                    pl.BlockSpec((tm, CONV_BLOCK), lambda i, j, k: (k, i)),
                    pl.BlockSpec((1, D_MODEL, CONV_BLOCK), lambda i, j, k: (i, 0, 0)),
                    _sds((NDEV, D_MODEL, CONV_BLOCK), BF16), 0, 0, (D_MODEL, CONV_BLOCK), drop=(0, 0, 1))
    dx, g_norm = _mm_dnorm(dp, w_in, f"conv_dh_{tag}", NDEV, pl.BlockSpec((tm, CONV_BLOCK), lambda i, j, k: (i, k)),
                           pl.BlockSpec((1, D_MODEL, CONV_BLOCK), lambda i, j, k: (k, 0, 0)), 1, 1, (0, 1), x, norm_w, dy)
    return dx, g_norm, g_in, g_dw, g_out


def _tri(lower):
    r = lax.broadcasted_iota(jnp.int32, (LANES, LANES), 0)
    c = lax.broadcasted_iota(jnp.int32, (LANES, LANES), 1)
    return jnp.where((r >= c) if lower else (r <= c), 1.0, 0.0).astype(F32)


def _cumsum_rows(v, reverse, name):
    s = v.shape[0]
    n = s // LANES
    idx = (lambda i: (n - 1 - i, 0)) if reverse else (lambda i: (i, 0))

    def body(v_ref, o_ref, carry_ref):
        @pl.when(pl.program_id(0) == 0)
        def _():
            carry_ref[...] = jnp.zeros_like(carry_ref)

        blk = v_ref[...]
        o_ref[...] = _dot(_tri(not reverse), blk, 1, 0, HI) + carry_ref[0:1, :]
        carry_ref[...] += jnp.sum(blk, axis=0, keepdims=True)

    spec = pl.BlockSpec((LANES, LANES), idx)
    return _pc(body, name, (n,), [spec], spec, _sds((s, LANES)), [pltpu.VMEM((8, LANES), F32)])(v)


def _lo_mask(shape):
    return lax.broadcasted_iota(jnp.int32, shape, len(shape) - 1) < HEAD_DIM


def _half_sums(v, lo):
    sa = jnp.sum(jnp.where(lo, v, 0.0), axis=-1, keepdims=True)
    sb = jnp.sum(jnp.where(lo, 0.0, v), axis=-1, keepdims=True)
    return jnp.where(lo, sa, sb)


def _fox_prep_fwd(proj, gq, gk, name):
    s = proj.shape[0]
    ts = _row_tile(s, 512)
    qscale = HEAD_DIM ** -0.5 * LOG2E

    def body(q_ref, k_ref, v_ref, gq_ref, gk_ref, qo_ref, ko_ref, vo_ref):
        lo = _lo_mask((ts, LANES))

        def hnorm(xv, g):
            ms = _half_sums(xv * xv, lo) * (1.0 / HEAD_DIM)
            return (xv * lax.rsqrt(ms + RMS_EPS)) * g

        qo_ref[...] = (hnorm(q_ref[...], gq_ref[...]) * qscale).astype(BF16)
        ko_ref[...] = hnorm(k_ref[...], gk_ref[...]).astype(BF16)
        vo_ref[...] = v_ref[...].astype(BF16)

    def col(off):
        return pl.BlockSpec((ts, LANES), lambda i, p: (i, off + p))

    gspec = pl.BlockSpec((1, LANES), lambda i, p: (0, 0))
    out = _sds((s, D_MODEL), BF16)
    return _pc(body, name, (s // ts, 8), [col(0), col(8), col(16), gspec, gspec], [col(0)] * 3, [out] * 3)(
        proj, proj, proj, gq, gk)


def _fox_logf(proj, bf, name):
    s = proj.shape[0]
    ts = _row_tile(s, 512)

    def body(f_ref, b_ref, o_ref):
        z = f_ref[...] + b_ref[...]
        lf = jnp.minimum(z, 0.0) - jnp.log(1.0 + jnp.exp(-jnp.abs(z)))
        real = lax.broadcasted_iota(jnp.int32, (ts, LANES), 1) < ATTN_HEADS
        o_ref[...] = jnp.where(real, lf, 0.0)

    return _pc(body, name, (s // ts,), [pl.BlockSpec((ts, LANES), lambda i: (i, 24)), pl.BlockSpec((1, LANES), lambda i: (0, 0))],
               pl.BlockSpec((ts, LANES), lambda i: (i, 0)), _sds((s, LANES)))(proj, bf)


def _fox_dlogf(proj, bf, dlf, name):
    s = proj.shape[0]
    ts = _row_tile(s, 512)

    def body(f_ref, b_ref, d_ref, o_ref, db_ref):
        z = f_ref[...] + b_ref[...]
        real = lax.broadcasted_iota(jnp.int32, (ts, LANES), 1) < ATTN_HEADS
        g = jnp.where(real, d_ref[...] * _sigmoid(-z), 0.0)
        o_ref[...] = g.astype(BF16)

        @pl.when(pl.program_id(0) == 0)
        def _():
            db_ref[...] = jnp.zeros_like(db_ref)

        db_ref[...] += jnp.sum(g, axis=0, keepdims=True)

    vec = pl.BlockSpec((1, LANES), lambda i: (0, 0))
    row = pl.BlockSpec((ts, LANES), lambda i: (i, 0))
    return _pc(body, name, (s // ts,), [pl.BlockSpec((ts, LANES), lambda i: (i, 24)), vec, row], [row, vec],
               [_sds((s, LANES), BF16), _sds((1, LANES))])(proj, bf, dlf)


def _decay_terms(cum):
    s = cum.shape[0]
    c2 = cum * LOG2E
    hi = lax.reduce_precision(c2, 8, 7)
    mid = lax.reduce_precision(c2 - hi, 8, 7)
    low = lax.reduce_precision(c2 - hi - mid, 8, 7)
    one = jnp.ones_like(hi)

    def place(terms):
        tt = jnp.stack(terms, axis=-1).astype(BF16).reshape(s, 8, 2, 6)
        z = jnp.zeros((s, 8, HEAD_DIM - 6), BF16)
        return jnp.concatenate([tt[:, :, 1], z, tt[:, :, 0], z], axis=-1).reshape(s, D_MODEL)

    return place([hi, mid, low, one, one, one]), place([one, one, one, -hi, -mid, -low])


def _attn_tiles(s):
    t = s
    for cand in (ATTN_TILE, ATTN_TILE // 2):
        if s % cand == 0:
            t = cand
            break
    return t, s // t


def _tri_steps(n, by_key):
    if by_key:
        pairs = [(q, k) for k in range(n) for q in range(k, n)]
    else:
        pairs = [(q, k) for q in range(n) for k in range(q + 1)]
    arr = np.asarray(pairs, np.int32)
    return jnp.asarray(arr[:, 0]), jnp.asarray(arr[:, 1])


def _attn_call(body, name, s, by_key, inputs, in_kinds, out_kinds, out_shapes, scratch, hosted=None, vmem=VMEM_LIMIT_BYTES):
    t, n = _attn_tiles(s)
    qi_arr, ki_arr = _tri_steps(n, by_key)
    nsteps = int(qi_arr.shape[0])
    specs = {
        "q": pl.BlockSpec((t, LANES), lambda p, i, qi, ki: (qi[i], p)),
        "k": pl.BlockSpec((t, LANES), lambda p, i, qi, ki: (ki[i], p)),
        "r": pl.BlockSpec((1, 2, t), lambda p, i, qi, ki: (p, 0, qi[i])),
        "m": pl.BlockSpec((1, t, t), lambda p, i, qi, ki: (jnp.where(qi[i] == ki[i], 1, 0), 0, 0)),
        "Q": pl.BlockSpec((1, LANES, s), lambda p, i, qi, ki: (p, 0, 0)),
        "R": pl.BlockSpec((1, 2, s), lambda p, i, qi, ki: (p, 0, 0)),
    }
    in_specs = [specs[c] for c in in_kinds]
    out_specs = [specs[c] for c in out_kinds]
    out_shapes, scratch, inputs = list(out_shapes), list(scratch), list(inputs)
    run = body
    if hosted is not None:
        arrays, gather = hosted
        na, n_in, n_out, n_scr = len(arrays), len(inputs), len(out_kinds), len(scratch)
        pick, xouts, sems = _exchange_parts(arrays, gather)

        def run(qi_ref, ki_ref, *refs):
            ins, srcs = refs[:n_in], refs[n_in:n_in + na]
            outs, dsts = refs[n_in + na:n_in + na + n_out], refs[n_in + na + n_out:n_in + 2 * na + n_out]
            scr, xsems = refs[n_in + 2 * na + n_out:n_in + 2 * na + n_out + n_scr], refs[n_in + 2 * na + n_out + n_scr:]
            p = pl.program_id(0)
            i = pl.program_id(1)

            @pl.when(jnp.logical_and(p == 0, i == 0))
            def _():
                _exchange_start(_exchange_copies(pick(srcs), dsts, *xsems))

            body(qi_ref, ki_ref, *ins, *outs, *scr)

            @pl.when(jnp.logical_and(p == 7, i == nsteps - 1))
            def _():
                _exchange_wait(_exchange_copies(pick(srcs), dsts, *xsems))

        hbm = pl.BlockSpec(memory_space=pl.ANY)
        in_specs += [hbm] * na
        out_specs += [hbm] * na
        out_shapes += xouts
        scratch += sems
        inputs += list(arrays)
    grid_spec = pltpu.PrefetchScalarGridSpec(
        num_scalar_prefetch=2, grid=(8, nsteps), in_specs=in_specs, out_specs=out_specs, scratch_shapes=scratch)
    params = pltpu.CompilerParams(dimension_semantics=("arbitrary", "arbitrary"), vmem_limit_bytes=vmem)
    return pl.pallas_call(run, name=name, grid_spec=grid_spec, out_shape=out_shapes, compiler_params=params)(
        qi_arr, ki_arr, *inputs)


def _biased_kq(q2, k2, aq, ak, lo):
    sa = _dot(jnp.where(lo, k2, ak), jnp.where(lo, q2, aq), 1, 1)
    sb = _dot(jnp.where(lo, ak, k2), jnp.where(lo, aq, q2), 1, 1)
    return sa, sb


def _causal_bias(s):
    t, _ = _attn_tiles(s)
    kid = lax.broadcasted_iota(jnp.int32, (t, t), 0)
    qid = lax.broadcasted_iota(jnp.int32, (t, t), 1)
    return jnp.stack([jnp.zeros((t, t), BF16), jnp.where(kid > qid, -jnp.inf, 0.0).astype(BF16)])


def _fold8(v, op):
    return op(v.reshape(v.shape[0] // 8, 8, v.shape[1]), axis=0)


def _chunk(ref, mask_ref, hd, r):
    rows = slice(r * ATTN_ROWS, (r + 1) * ATTN_ROWS)
    return rows, ref[hd, rows, :] + mask_ref[0, rows, :].astype(F32)


def _flash_fwd(qs, kn, vb, augq, augk, cmask, name, hosted=None):
    s = qs.shape[0]
    t, n = _attn_tiles(s)
    nch = t // ATTN_ROWS

    def body(qi_ref, ki_ref, q_ref, k_ref, v_ref, aq_ref, ak_ref, mk_ref, o_ref, lse_ref, s_ref, p_ref, m_ref, l_ref, acc_ref):
        i = pl.program_id(1)
        qi = qi_ref[i]
        ki = ki_ref[i]

        @pl.when(ki == 0)
        def _():
            m_ref[...] = jnp.full_like(m_ref, -jnp.inf)
            l_ref[...] = jnp.zeros_like(l_ref)
            acc_ref[...] = jnp.zeros_like(acc_ref)

        lo = _lo_mask((t, LANES))
        rowlo = lax.broadcasted_iota(jnp.int32, (LANES, t), 0) < HEAD_DIM
        v2 = v_ref[...]
        sa, sb = _biased_kq(q_ref[...], k_ref[...], aq_ref[...], ak_ref[...], lo)
        s_ref[0] = sa
        s_ref[1] = sb
        alphas, pvs = [], []
        for hd in range(2):
            mx = jnp.full((8, t), -jnp.inf, F32)
            for r in range(nch):
                _, sc = _chunk(s_ref, mk_ref, hd, r)
                mx = jnp.maximum(mx, _fold8(sc, jnp.max))
            m_prev = m_ref[hd:hd + 1, :]
            m_new = jnp.maximum(m_prev, jnp.max(mx, axis=0, keepdims=True))
            ls = jnp.zeros((8, t), F32)
            for r in range(nch):
                rows, sc = _chunk(s_ref, mk_ref, hd, r)
                pm = jnp.exp2(sc - m_new)
                ls = ls + _fold8(pm, jnp.sum)
                p_ref[hd, rows, :] = pm.astype(BF16)
            alpha = jnp.exp2(m_prev - m_new)
            l_ref[hd:hd + 1, :] = alpha * l_ref[hd:hd + 1, :] + jnp.sum(ls, axis=0, keepdims=True)
            m_ref[hd:hd + 1, :] = m_new
            alphas.append(alpha)
            pvs.append(_dot(v2, p_ref[hd], 0, 0))
        acc_ref[...] = jnp.where(rowlo, alphas[0], alphas[1]) * acc_ref[...] + jnp.where(rowlo, pvs[0], pvs[1])

        @pl.when(ki == qi)
        def _():
            o_ref[...] = (acc_ref[...] / jnp.where(rowlo, l_ref[0:1, :], l_ref[1:2, :])).T
            lse_ref[0] = m_ref[0:2, :] + jnp.log2(l_ref[0:2, :])

    stat = pltpu.VMEM((8, t), F32)
    return _attn_call(body, name, s, False, (qs, kn, vb, augq, augk, cmask), "qkkqkm", "qr",
                      [_sds((s, D_MODEL)), _sds((8, 2, s))],
                      [pltpu.VMEM((2, t, t), F32), pltpu.VMEM((2, t, t), BF16), stat, stat, pltpu.VMEM((LANES, t), F32)],
                      hosted=hosted)


def _fox_delta(do, o, name):
    s = do.shape[0]
    ts = _row_tile(s, 512)

    def body(do_ref, o_ref, d_ref):
        d_ref[...] = _half_sums(do_ref[...] * o_ref[...], _lo_mask((ts, LANES)))

    spec = pl.BlockSpec((ts, LANES), lambda i, p: (i, p))
    return _pc(body, name, (s // ts, 8), [spec, spec], spec, _sds((s, D_MODEL)))(do, o)


def _bwd_tile(q_ref, k_ref, v_ref, aq_ref, ak_ref, do_ref, s_ref, dp_ref, lo):
    do2 = do_ref[...].astype(BF16)
    zero = jnp.zeros_like(do2)
    v2 = v_ref[...]
    sa, sb = _biased_kq(q_ref[...], k_ref[...], aq_ref[...], ak_ref[...], lo)
    s_ref[0] = sa
    s_ref[1] = sb
    dp_ref[0] = _dot(v2, jnp.where(lo, do2, zero), 1, 1)
    dp_ref[1] = _dot(v2, jnp.where(lo, zero, do2), 1, 1)
    return do2


def _bwd_chunk(s_ref, dp_ref, mk_ref, lse_ref, dl_ref, hd, r):
    rows, sc = _chunk(s_ref, mk_ref, hd, r)
    pm = jnp.exp2(sc - lse_ref[0, hd:hd + 1, :])
    ds = pm * (dp_ref[hd, rows, :] - dl_ref[0, hd:hd + 1, :])
    return rows, pm, ds


def _flash_bwd(qs, kn, vb, augq, augk, cmask, do, lse, delta, name, hosted=None):
    s = qs.shape[0]
    t, n = _attn_tiles(s)
    nch = t // ATTN_ROWS

    def body(qi_ref, ki_ref, q_ref, k_ref, v_ref, aq_ref, ak_ref, mk_ref, do_ref, lse_ref, dl_ref,
             dk_ref, dv_ref, dc_ref, dq_ref, dcq_ref, s_ref, dp_ref, p_ref, ds_ref, dka_ref, dva_ref, dca_ref):
        i = pl.program_id(1)
        qi = qi_ref[i]
        ki = ki_ref[i]

        @pl.when(i == 0)
        def _():
            dq_ref[...] = jnp.zeros_like(dq_ref)
            dcq_ref[...] = jnp.zeros_like(dcq_ref)

        @pl.when(qi == ki)
        def _():
            dka_ref[...] = jnp.zeros_like(dka_ref)
            dva_ref[...] = jnp.zeros_like(dva_ref)
            dca_ref[...] = jnp.zeros_like(dca_ref)

        lo = _lo_mask((t, LANES))
        rowlo = lax.broadcasted_iota(jnp.int32, (LANES, t), 0) < HEAD_DIM
        do2 = _bwd_tile(q_ref, k_ref, v_ref, aq_ref, ak_ref, do_ref, s_ref, dp_ref, lo)
        q2 = q_ref[...]
        k2 = k_ref[...]
        qcols = pl.ds(pl.multiple_of(qi * t, t), t)
        dvs, dks, dqs = [], [], []
        for hd in range(2):
            rs = jnp.zeros((8, t), F32)
            for r in range(nch):
                rows, pm, ds = _bwd_chunk(s_ref, dp_ref, mk_ref, lse_ref, dl_ref, hd, r)
                rs = rs + _fold8(ds, jnp.sum)
                part = ds[:, 0:LANES]
                for c in range(1, t // LANES):
                    part = part + ds[:, c * LANES:(c + 1) * LANES]
                dca_ref[hd, rows, :] += part
                p_ref[hd, rows, :] = pm.astype(BF16)
                ds_ref[hd, rows, :] = ds.astype(BF16)
            dcq_ref[0, hd:hd + 1, qcols] += jnp.sum(rs, axis=0, keepdims=True)
            dvs.append(_dot(p_ref[hd], do2, 1, 0))
            dks.append(_dot(ds_ref[hd], q2, 1, 0))
            dqs.append(_dot(k2, ds_ref[hd], 0, 0))
        dva_ref[...] += jnp.where(lo, dvs[0], dvs[1])
        dka_ref[...] += jnp.where(lo, dks[0], dks[1])
        dq_ref[0, :, qcols] += jnp.where(rowlo, dqs[0], dqs[1])

        @pl.when(qi == n - 1)
        def _():
            dk_ref[...] = dka_ref[...] * LN2
            dv_ref[...] = dva_ref[...]
            dc_ref[...] = -jnp.where(lo, jnp.sum(dca_ref[0], axis=-1, keepdims=True), jnp.sum(dca_ref[1], axis=-1, keepdims=True))

    out = _sds((s, D_MODEL))
    return _attn_call(body, name, s, True, (qs, kn, vb, augq, augk, cmask, do, lse, delta), "qkkqkmqrr", "kkkQR",
                      [out, out, out, _sds((8, LANES, s)), _sds((8, 2, s))],
                      [pltpu.VMEM((2, t, t), F32), pltpu.VMEM((2, t, t), F32), pltpu.VMEM((2, t, t), BF16),
                       pltpu.VMEM((2, t, t), BF16), pltpu.VMEM((t, LANES), F32), pltpu.VMEM((t, LANES), F32),
                       pltpu.VMEM((2, t, LANES), F32)], hosted=hosted, vmem=ATTN_BWD_VMEM_BYTES)


def _fox_prep_bwd(proj, dqs, dk, dv, gq, gk, name):
    s = proj.shape[0]
    ts = _row_tile(s, 512)
    scale = HEAD_DIM ** -0.5

    def body(q_ref, k_ref, dq_ref, dk_ref, dv_ref, gq_ref, gk_ref, oq_ref, ok_ref, ov_ref, dgq_ref, dgk_ref):
        lo = _lo_mask((ts, LANES))

        @pl.when(jnp.logical_and(pl.program_id(0) == 0, pl.program_id(1) == 0))
        def _():
            dgq_ref[...] = jnp.zeros_like(dgq_ref)
            dgk_ref[...] = jnp.zeros_like(dgk_ref)

        def back(xv, dout, g):
            r = lax.rsqrt(_half_sums(xv * xv, lo) * (1.0 / HEAD_DIM) + RMS_EPS)
            y = xv * r
            dy = dout * g
            dx = r * (dy - y * (_half_sums(dy * y, lo) * (1.0 / HEAD_DIM)))
            return dx, jnp.sum(dout * y, axis=0, keepdims=True)

        dxq, dgq = back(q_ref[...], dq_ref[0].T * scale, gq_ref[...])
        dxk, dgk = back(k_ref[...], dk_ref[...], gk_ref[...])
        oq_ref[...] = dxq.astype(BF16)
        ok_ref[...] = dxk.astype(BF16)
        ov_ref[...] = dv_ref[...].astype(BF16)
        dgq_ref[...] += dgq
        dgk_ref[...] += dgk

    def col(off):
        return pl.BlockSpec((ts, LANES), lambda i, p: (i, off + p))

    gspec = pl.BlockSpec((1, LANES), lambda i, p: (0, 0))
    out = _sds((s, D_MODEL), BF16)
    dqt = pl.BlockSpec((1, LANES, ts), lambda i, p: (p, 0, i))
    return _pc(body, name, (s // ts, 8), [col(0), col(8), dqt, col(0), col(0), gspec, gspec],
               [col(0)] * 3 + [gspec] * 2, [out] * 3 + [_sds((1, LANES))] * 2)(proj, proj, dqs, dk, dv, gq, gk)


def _fox_fwd(x, norm_w, w_in, b_f, q_gain, k_gain, w_out, hosted=None):
    h = _rms_fwd(x, norm_w, "fox_norm")
    proj = _mm(h, w_in, "nn", "fox_in")
    gq = jnp.tile(q_gain, (1, 2))
    gk = jnp.tile(k_gain, (1, 2))
    bf = jnp.pad(b_f, ((0, 0), (0, LANES - ATTN_HEADS)))
    qs, kn, vb = _fox_prep_fwd(proj, gq, gk, "fox_prep")
    cum = _cumsum_rows(_fox_logf(proj, bf, "fox_logf"), False, "fox_cum")[:, :ATTN_HEADS]
    augq, augk = _decay_terms(cum)
    cmask = _causal_bias(x.shape[0])
    o, lse, *got = _flash_fwd(qs, kn, vb, augq, augk, cmask, "fox_attn", hosted=hosted)
    y = _mm(o, w_out, "nn", "fox_out", res=x)
    return y, (x, h, proj, gq, gk, bf, qs, kn, vb, augq, augk, cmask, o, lse), got


def _fox_bwd(dy, saved, norm_w, w_in, w_out, hosted=None):
    x, h, proj, gq, gk, bf, qs, kn, vb, augq, augk, cmask, o, lse = saved
    s = x.shape[0]
    do = _mm(dy, w_out, "nt", "fox_do")
    g_out = _mm(o, dy, "tn", "fox_gout", out_dtype=BF16)
    delta = _fox_delta(do, o, "fox_delta")[:, ::HEAD_DIM].T.reshape(8, 2, s)
    dk, dv, dck, dqs, dcq, *got = _flash_bwd(qs, kn, vb, augq, augk, cmask, do, lse, delta, "fox_dattn", hosted=hosted)
    dcum = jnp.pad(dcq.reshape(ATTN_HEADS, s).T + dck[:, ::HEAD_DIM], ((0, 0), (0, LANES - ATTN_HEADS)))
    dlf = _cumsum_rows(dcum, True, "fox_dcum")
    dfl, g_bf = _fox_dlogf(proj, bf, dlf, "fox_dlogf")
    dq_o, dk_o, dv_o, g_gq, g_gk = _fox_prep_bwd(proj, dqs, dk, dv, gq, gk, "fox_dprep")
    dproj = jnp.concatenate([dq_o, dk_o, dv_o, dfl], axis=1)
    g_in = _mm(h, dproj, "tn", "fox_gin", out_dtype=BF16)
    dx, g_norm = _mm_dnorm_nt(dproj, w_in, "fox_dh", x, norm_w, dy)
    g_q = g_gq[:, :HEAD_DIM] + g_gq[:, HEAD_DIM:]
    g_k = g_gk[:, :HEAD_DIM] + g_gk[:, HEAD_DIM:]
    return dx, g_norm, g_in[:, :FOX_IN], g_bf[:, :ATTN_HEADS], g_q, g_k, g_out, got


def _ssd_conv_fwd(proj, cw, cb, name):
    s = proj.shape[0]
    ts = _row_tile(s)
    w = 1024
    per = ts // 8

    def body(p_ref, h_ref, w_ref, b_ref, o_ref):
        first = pl.program_id(0) == 0
        cur = p_ref[...]
        halo = h_ref[...]
        u = w_ref[3:4, :] * cur + b_ref[...]
        for j in range(1, 4):
            u = u + w_ref[3 - j:4 - j, :] * _prev_rows(cur, halo, j, first)
        o_ref[...] = u * _sigmoid(u)

    return _pc(body, name, (s // ts, 4),
               [pl.BlockSpec((ts, w), lambda i, j: (i, 2 + j)),
                pl.BlockSpec((8, w), lambda i, j: (jnp.maximum(i * per - 1, 0), 2 + j)),
                pl.BlockSpec((4, w), lambda i, j: (0, j)), pl.BlockSpec((1, w), lambda i, j: (0, j))],
               pl.BlockSpec((ts, w), lambda i, j: (i, j)), _sds((s, SSM_CONV_DIM)))(proj, proj, cw, cb)


def _ssd_conv_bwd_act(proj, dxbc, cw, cb, name):
    s = proj.shape[0]
    ts = _row_tile(s)
    w = 1024
    per = ts // 8

    def body(p_ref, h_ref, d_ref, w_ref, b_ref, g_ref, db_ref):
        first = pl.program_id(1) == 0
        cur = p_ref[...]
        halo = h_ref[...]
        u = w_ref[3:4, :] * cur + b_ref[...]
        for j in range(1, 4):
            u = u + w_ref[3 - j:4 - j, :] * _prev_rows(cur, halo, j, first)
        sg = _sigmoid(u)
        g = d_ref[...] * (sg * (1.0 + u * (1.0 - sg)))
        g_ref[...] = g

        @pl.when(first)
        def _():
            db_ref[...] = jnp.zeros_like(db_ref)

        db_ref[...] += jnp.sum(g, axis=0, keepdims=True)

    vec = pl.BlockSpec((1, w), lambda j, i: (0, j))
    tile = pl.BlockSpec((ts, w), lambda j, i: (i, j))
    return _pc(body, name, (4, s // ts),
               [pl.BlockSpec((ts, w), lambda j, i: (i, 2 + j)),
                pl.BlockSpec((8, w), lambda j, i: (jnp.maximum(i * per - 1, 0), 2 + j)),
                tile, pl.BlockSpec((4, w), lambda j, i: (0, j)), vec],
               [tile, vec], [_sds((s, SSM_CONV_DIM)), _sds((1, SSM_CONV_DIM))])(proj, proj, dxbc, cw, cb)


def _ssd_conv_bwd_in(proj, g, cw, name):
    s = proj.shape[0]
    ts = _row_tile(s)
    nt = s // ts
    w = 1024
    per = ts // 8
    nblk = s // 8

    def body(p_ref, h_ref, g_ref, gn_ref, w_ref, o_ref, dw_ref):
        i = pl.program_id(1)
        first = i == 0
        last = i == nt - 1
        cur = p_ref[...]
        halo = h_ref[...]
        gv = g_ref[...]
        gn = gn_ref[...]

        @pl.when(first)
        def _():
            dw_ref[...] = jnp.zeros_like(dw_ref)

        dpre = w_ref[3:4, :] * gv
        dw_ref[3:4, :] += jnp.sum(gv * cur, axis=0, keepdims=True)
        for j in range(1, 4):
            dpre = dpre + w_ref[3 - j:4 - j, :] * _next_rows(gv, gn, j, last)
            dw_ref[3 - j:4 - j, :] += jnp.sum(gv * _prev_rows(cur, halo, j, first), axis=0, keepdims=True)
        o_ref[...] = dpre.astype(BF16)

    tile = pl.BlockSpec((ts, w), lambda j, i: (i, j))
    wspec = pl.BlockSpec((4, w), lambda j, i: (0, j))
    return _pc(body, name, (4, nt),
               [pl.BlockSpec((ts, w), lambda j, i: (i, 2 + j)),
                pl.BlockSpec((8, w), lambda j, i: (jnp.maximum(i * per - 1, 0), 2 + j)),
                tile, pl.BlockSpec((8, w), lambda j, i: (jnp.minimum((i + 1) * per, nblk - 1), j)), wspec],
               [tile, wspec], [_sds((s, SSM_CONV_DIM), BF16), _sds((4, SSM_CONV_DIM))])(proj, proj, g, g, cw)


def _ssd_dt_fwd(proj, bias, a_neg, name):
    s = proj.shape[0]
    n = s // SSM_CHUNK

    def body(r_ref, b_ref, a_ref, dt_ref, ac_ref):
        real = lax.broadcasted_iota(jnp.int32, (SSM_CHUNK, LANES), 1) < SSM_HEADS
        dt = jnp.where(real, _softplus(r_ref[...] + b_ref[...]), 0.0)
        dt_ref[...] = dt
        ac_ref[...] = _dot(_tri(True), dt * a_ref[...], 1, 0, HI)

    vec = pl.BlockSpec((1, LANES), lambda c: (0, 0))
    row = pl.BlockSpec((SSM_CHUNK, LANES), lambda c: (c, 0))
    return _pc(body, name, (n,), [pl.BlockSpec((SSM_CHUNK, LANES), lambda c: (c, 48)), vec, vec], [row, row],
               [_sds((s, LANES)), _sds((s, LANES))])(proj, bias, a_neg)


def _ssd_dt_bwd(proj, bias, ddt, name):
    s = proj.shape[0]
    ts = _row_tile(s, 512)

    def body(r_ref, b_ref, d_ref, o_ref, db_ref):
        real = lax.broadcasted_iota(jnp.int32, (ts, LANES), 1) < SSM_HEADS
        g = jnp.where(real, d_ref[...] * _sigmoid(r_ref[...] + b_ref[...]), 0.0)
        o_ref[...] = g.astype(BF16)

        @pl.when(pl.program_id(0) == 0)
        def _():
            db_ref[...] = jnp.zeros_like(db_ref)

        db_ref[...] += jnp.sum(g, axis=0, keepdims=True)

    vec = pl.BlockSpec((1, LANES), lambda i: (0, 0))
    row = pl.BlockSpec((ts, LANES), lambda i: (i, 0))
    return _pc(body, name, (s // ts,), [pl.BlockSpec((ts, LANES), lambda i: (i, 48)), vec, row], [row, vec],
               [_sds((s, LANES), BF16), _sds((1, LANES))])(proj, bias, ddt)


def _pair_cols(cols, k0, lo):
    return jnp.where(lo, cols[:, k0:k0 + 1], cols[:, k0 + 1:k0 + 2])


def _last_lane(row):
    lane = lax.broadcasted_iota(jnp.int32, row.shape, 1)
    return jnp.sum(jnp.where(lane == SSM_CHUNK - 1, row, 0.0), axis=-1, keepdims=True)


def _ssd_specs(nc, rev):
    cc = (lambda c: nc - 1 - c) if rev else (lambda c: c)
    return dict(
        x=pl.BlockSpec((SSM_CHUNK, 256), lambda g, c: (cc(c), g)),
        b=pl.BlockSpec((SSM_CHUNK, LANES), lambda g, c: (cc(c), 16 + g)),
        c=pl.BlockSpec((SSM_CHUNK, LANES), lambda g, c: (cc(c), 24 + g)),
        col=pl.BlockSpec((1, SSM_CHUNK, 4), lambda g, c: (g, cc(c), 0)),
        row=pl.BlockSpec((1, 4, SSM_CHUNK), lambda g, c: (g, 0, cc(c))),
        grp=pl.BlockSpec((1, 1, 256), lambda g, c: (g, 0, 0)),
        grow=pl.BlockSpec((1, 4, LANES), lambda g, c: (g, 0, 0)),
        hs=pl.BlockSpec((1, 1, 256, SSM_STATE), lambda g, c: (cc(c), g, 0, 0)),
        bc=pl.BlockSpec((SSM_CHUNK, LANES), lambda g, c: (cc(c), g)),
    )


def _ssd_scan_fwd(xbc, dtc, acol, drow, arow, dskip, name):
    s = xbc.shape[0]
    nc = s // SSM_CHUNK
    sp = _ssd_specs(nc, False)
    L = SSM_CHUNK

    def body(x_ref, b_ref, c_ref, dtc_ref, ac_ref, dr_ref, ar_ref, dk_ref, y_ref, hs_ref, h_ref):
        @pl.when(pl.program_id(1) == 0)
        def _():
            h_ref[...] = jnp.zeros_like(h_ref)

        bb = b_ref[...].astype(BF16)
        cb = c_ref[...].astype(BF16)
        gm = _dot(cb, bb, 1, 1)
        dtc = dtc_ref[0]
        ac = ac_ref[0]
        dr = dr_ref[0]
        ar = ar_ref[0]
        dsk = dk_ref[0]
        hs_ref[0, 0] = h_ref[...]
        tril = lax.broadcasted_iota(jnp.int32, (L, L), 0) >= lax.broadcasted_iota(jnp.int32, (L, L), 1)
        lo = _lo_mask((L, LANES))
        rowlo = lax.broadcasted_iota(jnp.int32, (L, LANES), 0) < HEAD_DIM
        for pr in range(2):
            k0 = 2 * pr
            xp = x_ref[:, pr * LANES:(pr + 1) * LANES]
            xpb = xp.astype(BF16)
            hp = h_ref[pr * LANES:(pr + 1) * LANES, :]
            yd, al = [], []
            for k in (k0, k0 + 1):
                seg = ac[:, k:k + 1] - ar[k:k + 1, :]
                wk = gm * jnp.exp(jnp.where(tril, seg, -jnp.inf)) * dr[k:k + 1, :]
                yd.append(_dot(wk.astype(BF16), xpb, 1, 0))
                al.append(_last_lane(ar[k:k + 1, :]))
            e = jnp.exp(_pair_cols(ac, k0, lo))
            yo = _dot(cb, hp.astype(BF16), 1, 1) * e
            y_ref[:, pr * LANES:(pr + 1) * LANES] = jnp.where(lo, yd[0], yd[1]) + yo + dsk[:, pr * LANES:(pr + 1) * LANES] * xp
            wp = jnp.where(lo, jnp.exp(al[0] - ac[:, k0:k0 + 1]) * dtc[:, k0:k0 + 1],
                           jnp.exp(al[1] - ac[:, k0 + 1:k0 + 2]) * dtc[:, k0 + 1:k0 + 2])
            st = _dot((xp * wp).astype(BF16), bb, 0, 0)
            dec = jnp.where(rowlo, jnp.exp(al[0]), jnp.exp(al[1]))
            h_ref[pr * LANES:(pr + 1) * LANES, :] = dec * hp + st

    return _pc(body, name, (SSM_GROUPS, nc),
               [sp["x"], sp["b"], sp["c"], sp["col"], sp["col"], sp["row"], sp["row"], sp["grp"]],
               [sp["x"], sp["hs"]], [_sds((s, SSM_INNER)), _sds((nc, SSM_GROUPS, 256, SSM_STATE))],
               [pltpu.VMEM((256, SSM_STATE), F32)])(xbc, xbc, xbc, dtc, acol, drow, arow, dskip)


def _ssd_scan_bwd(xbc, dtc, acol, drow, arow, dskip, agrp, hs, dy, name):
    s = xbc.shape[0]
    nc = s // SSM_CHUNK
    sp = _ssd_specs(nc, True)
    L = SSM_CHUNK

    def body(x_ref, b_ref, c_ref, dtc_ref, ac_ref, dr_ref, ar_ref, dk_ref, ag_ref, hs_ref, dy_ref,
             dx_ref, db_ref, dc_ref, ddt_ref, da_ref, dd_ref, dh_ref):
        @pl.when(pl.program_id(1) == 0)
        def _():
            dh_ref[...] = jnp.zeros_like(dh_ref)
            da_ref[...] = jnp.zeros_like(da_ref)
            dd_ref[...] = jnp.zeros_like(dd_ref)

        bb = b_ref[...].astype(BF16)
        cb = c_ref[...].astype(BF16)
        gm = _dot(cb, bb, 1, 1)
        dtc = dtc_ref[0]
        ac = ac_ref[0]
        dr = dr_ref[0]
        ar = ar_ref[0]
        dsk = dk_ref[0]
        ag = ag_ref[0]
        tril = lax.broadcasted_iota(jnp.int32, (L, L), 0) >= lax.broadcasted_iota(jnp.int32, (L, L), 1)
        lo = _lo_mask((L, LANES))
        nlo = jnp.logical_not(lo)
        rowlo = lax.broadcasted_iota(jnp.int32, (L, LANES), 0) < HEAD_DIM
        lane = lax.broadcasted_iota(jnp.int32, (L, LANES), 1)
        lane_row = lax.broadcasted_iota(jnp.int32, (1, LANES), 1)
        dgm = jnp.zeros((L, L), F32)
        dcm = jnp.zeros((L, SSM_STATE), F32)
        dbm = jnp.zeros((L, SSM_STATE), F32)
        cols = jnp.zeros((L, LANES), F32)
        rows_ddt, rows_q, al_all, dcd_all = [], [], [], []
        for pr in range(2):
            k0 = 2 * pr
            xp = x_ref[:, pr * LANES:(pr + 1) * LANES]
            xpb = xp.astype(BF16)
            dyp = dy_ref[:, pr * LANES:(pr + 1) * LANES]
            dypb = dyp.astype(BF16)
            zero = jnp.zeros_like(dypb)
            hp = hs_ref[0, 0, pr * LANES:(pr + 1) * LANES, :]
            hpb = hp.astype(BF16)
            dst = dh_ref[pr * LANES:(pr + 1) * LANES, :]
            dstb = dst.astype(BF16)
            dxd, al = [], []
            for k in (k0, k0 + 1):
                sel = lo if k == k0 else nlo
                seg = ac[:, k:k + 1] - ar[k:k + 1, :]
                lam = jnp.exp(jnp.where(tril, seg, -jnp.inf))
                wk = gm * lam * dr[k:k + 1, :]
                dwk = _dot(jnp.where(sel, dypb, zero), xpb, 1, 1)
                mk = dwk * gm * lam
                qk = mk * dr[k:k + 1, :]
                dgm = dgm + dwk * lam * dr[k:k + 1, :]
                rows_ddt.append(jnp.sum(mk, axis=0, keepdims=True))
                rows_q.append(jnp.sum(qk, axis=0, keepdims=True))
                cols = jnp.where(lane == k, jnp.sum(qk, axis=-1, keepdims=True), cols)
                dxd.append(_dot(wk.astype(BF16), dypb, 0, 0))
                al.append(_last_lane(ar[k:k + 1, :]))
            al_all += al
            dxp = jnp.where(lo, dxd[0], dxd[1])
            e = jnp.exp(_pair_cols(ac, k0, lo))
            dye = dyp * e
            dyeb = dye.astype(BF16)
            dcm = dcm + _dot(dyeb, hpb, 1, 0)
            dh_yoff = _dot(dyeb, cb, 0, 0)
            tq = dye * _dot(cb, hpb, 1, 1)
            cols = jnp.where(lane == 4 + k0, jnp.sum(jnp.where(lo, tq, 0.0), axis=-1, keepdims=True), cols)
            cols = jnp.where(lane == 5 + k0, jnp.sum(jnp.where(lo, 0.0, tq), axis=-1, keepdims=True), cols)
            wp = jnp.where(lo, jnp.exp(al[0] - ac[:, k0:k0 + 1]) * dtc[:, k0:k0 + 1],
                           jnp.exp(al[1] - ac[:, k0 + 1:k0 + 2]) * dtc[:, k0 + 1:k0 + 2])
            dxw = _dot(bb, dstb, 1, 1)
            dxp = dxp + dxw * wp
            tw = xp * dxw
            cols = jnp.where(lane == 8 + k0, jnp.sum(jnp.where(lo, tw, 0.0), axis=-1, keepdims=True), cols)
            cols = jnp.where(lane == 9 + k0, jnp.sum(jnp.where(lo, 0.0, tw), axis=-1, keepdims=True), cols)
            dbm = dbm + _dot((xp * wp).astype(BF16), dstb, 1, 0)
            dsl = dsk[:, pr * LANES:(pr + 1) * LANES]
            dx_ref[:, pr * LANES:(pr + 1) * LANES] = dxp + dsl * dyp
            dd_ref[0, :, pr * LANES:(pr + 1) * LANES] += jnp.sum(dyp * xp, axis=0, keepdims=True)
            prod = dst * hp
            dcd_all.append(jnp.sum(jnp.sum(jnp.where(rowlo, prod, 0.0), axis=-1, keepdims=True), axis=0, keepdims=True))
            dcd_all.append(jnp.sum(jnp.sum(jnp.where(rowlo, 0.0, prod), axis=-1, keepdims=True), axis=0, keepdims=True))
            dec = jnp.where(rowlo, jnp.exp(al[0]), jnp.exp(al[1]))
            dh_ref[pr * LANES:(pr + 1) * LANES, :] = dec * dst + dh_yoff
        dgb = dgm.astype(BF16)
        dc_ref[...] = dcm + _dot(dgb, bb, 1, 0)
        db_ref[...] = dbm + _dot(dgb, cb, 0, 0)
        colt = cols.T
        sub8 = lax.broadcasted_iota(jnp.int32, (8, LANES), 0)
        da_rows = jnp.zeros((8, LANES), F32)
        ddt_part = []
        for k in range(4):
            rs = colt[k:k + 1, :]
            uo = colt[4 + k:5 + k, :]
            dwl = colt[8 + k:9 + k, :]
            es = jnp.exp(al_all[k] - ar[k:k + 1, :])
            wrow = es * dr[k:k + 1, :]
            dwl_w = dwl * wrow
            da_k = rs - rows_q[k] + uo - dwl_w
            tail = jnp.sum(dwl_w, axis=-1, keepdims=True) + jnp.exp(al_all[k]) * dcd_all[k]
            da_k = da_k + jnp.where(lane_row == L - 1, tail, 0.0)
            da_rows = jnp.where(sub8 == k, da_k, da_rows)
            ddt_part.append(rows_ddt[k] + dwl * es)
        dda = _dot(da_rows, _tri(True), 1, 0, HI)
        for k in range(4):
            dda_k = dda[k:k + 1, :]
            ddt_ref[0, k:k + 1, :] = ddt_part[k] + dda_k * ag[k:k + 1, :]
            da_ref[0, k:k + 1, :] += dda_k * dr[k:k + 1, :] * ag[k:k + 1, :]

    return _pc(body, name, (SSM_GROUPS, nc),
               [sp["x"], sp["b"], sp["c"], sp["col"], sp["col"], sp["row"], sp["row"], sp["grp"], sp["grow"], sp["hs"], sp["x"]],
               [sp["x"], sp["bc"], sp["bc"], sp["row"], sp["grow"], sp["grp"]],
               [_sds((s, SSM_INNER)), _sds((s, 1024)), _sds((s, 1024)), _sds((SSM_GROUPS, 4, s)),
                _sds((SSM_GROUPS, 4, LANES)), _sds((SSM_GROUPS, 1, 256))],
               [pltpu.VMEM((256, SSM_STATE), F32)])(xbc, xbc, xbc, dtc, acol, drow, arow, dskip, agrp, hs, dy)


def _gnorm_fwd(y, proj, nw, name):
    s = y.shape[0]
    ts = _row_tile(s)
    gw = SSM_INNER // SSM_GROUPS

    def body(y_ref, z_ref, w_ref, o_ref):
        for g in range(SSM_GROUPS):
            sl = slice(g * gw, (g + 1) * gw)
            z = z_ref[:, sl]
            y2 = y_ref[:, sl] * (z * _sigmoid(z))
            r = lax.rsqrt(jnp.mean(y2 * y2, axis=-1, keepdims=True) + RMS_EPS)
            o_ref[:, sl] = ((y2 * r) * w_ref[:, sl]).astype(BF16)

    row = pl.BlockSpec((ts, SSM_INNER), lambda i: (i, 0))
    return _pc(body, name, (s // ts,), [row, row, pl.BlockSpec((1, SSM_INNER), lambda i: (0, 0))], row,
               _sds((s, SSM_INNER), BF16))(y, proj, nw)


def _gnorm_bwd(y, proj, nw, dyn, name):
    s = y.shape[0]
    ts = _row_tile(s)
    gw = SSM_INNER // SSM_GROUPS

    def body(y_ref, z_ref, w_ref, d_ref, dy_ref, dz_ref, dw_ref):
        @pl.when(pl.program_id(0) == 0)
        def _():
            dw_ref[...] = jnp.zeros_like(dw_ref)

        for g in range(SSM_GROUPS):
            sl = slice(g * gw, (g + 1) * gw)
            z = z_ref[:, sl]
            yv = y_ref[:, sl]
            sg = _sigmoid(z)
            sz = z * sg
            y2 = yv * sz
            r = lax.rsqrt(jnp.mean(y2 * y2, axis=-1, keepdims=True) + RMS_EPS)
            yn = y2 * r
            dout = d_ref[:, sl]
            dyg = dout * w_ref[:, sl]
            dy2 = r * (dyg - yn * jnp.mean(dyg * yn, axis=-1, keepdims=True))
            dy_ref[:, sl] = dy2 * sz
            dz_ref[:, sl] = (dy2 * yv * (sg * (1.0 + z * (1.0 - sg)))).astype(BF16)
            dw_ref[:, sl] += jnp.sum(dout * yn, axis=0, keepdims=True)

    row = pl.BlockSpec((ts, SSM_INNER), lambda i: (i, 0))
    vec = pl.BlockSpec((1, SSM_INNER), lambda i: (0, 0))
    return _pc(body, name, (s // ts,), [row, row, vec, row], [row, row, vec],
               [_sds((s, SSM_INNER)), _sds((s, SSM_INNER), BF16), _sds((1, SSM_INNER))])(y, proj, nw, dyn)


def _head_layouts(v, s):
    return v.reshape(s, SSM_GROUPS, 4).transpose(1, 0, 2), v.T.reshape(SSM_GROUPS, 4, s)


def _ssd_fwd(x, norm_w, w_in, conv_w, conv_b, dt_bias, a_log, d_skip, gnorm_w, w_out):
    s = x.shape[0]
    h = _rms_fwd(x, norm_w, "ssd_norm")
    proj = _mm(h, w_in, "nn", "ssd_in")
    xbc = _ssd_conv_fwd(proj, conv_w, conv_b, "ssd_conv")
    pad = ((0, 0), (0, LANES - SSM_HEADS))
    a_neg = -jnp.exp(a_log)
    bias = jnp.pad(dt_bias, pad)
    dt, acum = _ssd_dt_fwd(proj, bias, jnp.pad(a_neg, pad), "ssd_dt")
    dtc, drow = _head_layouts(dt[:, :SSM_HEADS], s)
    acol, arow = _head_layouts(acum[:, :SSM_HEADS], s)
    dskip = jnp.repeat(d_skip.reshape(SSM_GROUPS, 1, 4), HEAD_DIM, axis=2)
    y, hs = _ssd_scan_fwd(xbc, dtc, acol, drow, arow, dskip, "ssd_scan")
    yn = _gnorm_fwd(y, proj, gnorm_w, "ssd_gnorm")
    out = _mm(yn, w_out, "nn", "ssd_out", res=x)
    return out, (x, h, proj, xbc, bias, a_neg, dtc, acol, drow, arow, dskip, y, hs, yn)


def _ssd_bwd(dout, saved, norm_w, w_in, conv_w, conv_b, gnorm_w, w_out):
    x, h, proj, xbc, bias, a_neg, dtc, acol, drow, arow, dskip, y, hs, yn = saved
    s = x.shape[0]
    dyn = _mm(dout, w_out, "nt", "ssd_dyn")
    g_out = _mm(yn, dout, "tn", "ssd_gout", out_dtype=BF16)
    dy, dz, g_gnorm = _gnorm_bwd(y, proj, gnorm_w, dyn, "ssd_dgnorm")
    agrp = jnp.broadcast_to(a_neg.reshape(SSM_GROUPS, 4, 1), (SSM_GROUPS, 4, LANES))
    dxs, db, dc, ddt_row, da_acc, dd_acc = _ssd_scan_bwd(xbc, dtc, acol, drow, arow, dskip, agrp, hs, dy, "ssd_dscan")
    dxbc = jnp.concatenate([dxs, db, dc], axis=1)
    gact, g_cb = _ssd_conv_bwd_act(proj, dxbc, conv_w, conv_b, "ssd_dconv_act")
    dpre, g_cw = _ssd_conv_bwd_in(proj, gact, conv_w, "ssd_dconv_in")
    ddt = jnp.pad(ddt_row.reshape(SSM_HEADS, s).T, ((0, 0), (0, LANES - SSM_HEADS)))
    ddtraw, g_dtb = _ssd_dt_bwd(proj, bias, ddt, "ssd_ddt")
    dproj = jnp.concatenate([dz, dpre, ddtraw], axis=1)
    g_in = _mm(h, dproj, "tn", "ssd_gin", out_dtype=BF16)
    dx, g_norm = _mm_dnorm_nt(dproj, w_in, "ssd_dh", x, norm_w, dout)
    g_alog = jnp.sum(da_acc, axis=-1).reshape(1, SSM_HEADS)
    g_d = jnp.sum(dd_acc.reshape(SSM_GROUPS, 4, HEAD_DIM), axis=-1).reshape(1, SSM_HEADS)
    return dx, g_norm, g_in[:, :SSM_IN], g_cw, g_cb, g_dtb[:, :SSM_HEADS], g_alog, g_d, g_gnorm, g_out


def _loss_head(y, target, name):
    s, d = y.shape
    ts = _row_tile(s)

    def body(y_ref, t_ref, dy_ref, l_ref):
        @pl.when(pl.program_id(0) == 0)
        def _():
            l_ref[...] = jnp.zeros_like(l_ref)

        e = y_ref[...] - t_ref[...]
        dy_ref[...] = e * (1.0 / d)
        part = jnp.sum(jnp.sum(e * e, axis=-1, keepdims=True), axis=0, keepdims=True) * (0.5 / d)
        l_ref[...] += jnp.broadcast_to(part, l_ref.shape)

    row = pl.BlockSpec((ts, d), lambda i: (i, 0))
    dy, lacc = _pc(body, name, (s // ts,), [row, row], [row, pl.BlockSpec((8, LANES), lambda i: (0, 0))],
                   [_sds((s, d)), _sds((8, LANES))])(y, target)
    return lacc[0, 0], dy


def _local_step(x, target, w, gather_rest=None, scatter_first=None):
    saved = []
    received = None
    for i in range(DEPTH):
        kind, j = i % 3, i // 3
        mn = w["mix_norm"][i:i + 1]
        if kind == 0:
            x, sv = _conv_fwd(x, mn, w["conv_w_in"][j], w["conv_w_dw"][j], w["conv_w_out"][j], str(i))
        elif kind == 1:
            hosted = None if gather_rest is None else (gather_rest[0], True)
            x, sv, got = _fox_fwd(x, mn, w["fox_w_in"], w["fox_b_f"], w["fox_q_gain"], w["fox_k_gain"], w["fox_w_out"], hosted)
            if gather_rest is not None:
                w = gather_rest[1](w, got)
        else:
            x, sv = _ssd_fwd(x, mn, w["ssd_w_in"], w["ssd_conv_w"], w["ssd_conv_b"], w["ssd_dt_bias"],
                             w["ssd_a_log"], w["ssd_d"], w["ssd_norm_w"], w["ssd_w_out"])
        x, sf = _ffn_fwd(x, w["ffn_norm"][i:i + 1], w["ffn_w_gu"][i], w["ffn_w_down"][i], str(i))
        saved.append((sv, sf))
    loss, dx = _loss_head(x, target, "loss_head")
    g = {k: [None] * n for k, n in (("mix_norm", DEPTH), ("ffn_norm", DEPTH), ("ffn_w_gu", DEPTH), ("ffn_w_down", DEPTH),
                                    ("conv_w_in", 2), ("conv_w_dw", 2), ("conv_w_out", 2))}
    for i in reversed(range(DEPTH)):
        kind, j = i % 3, i // 3
        sv, sf = saved[i]
        dx, g["ffn_norm"][i], g["ffn_w_gu"][i], g["ffn_w_down"][i] = _ffn_bwd(
            dx, sf, w["ffn_norm"][i:i + 1], w["ffn_w_gu"][i], w["ffn_w_down"][i], str(i))
        mn = w["mix_norm"][i:i + 1]
        if kind == 0:
            dx, g["mix_norm"][i], g["conv_w_in"][j], g["conv_w_dw"][j], g["conv_w_out"][j] = _conv_bwd(
                dx, sv, mn, w["conv_w_in"][j], w["conv_w_dw"][j], w["conv_w_out"][j], str(i))
        elif kind == 1:
            hosted = None if scatter_first is None else (scatter_first(g), False)
            (dx, g["mix_norm"][i], g["fox_w_in"], g["fox_b_f"], g["fox_q_gain"], g["fox_k_gain"],
             g["fox_w_out"], received) = _fox_bwd(dx, sv, mn, w["fox_w_in"], w["fox_w_out"], hosted)
        else:
            (dx, g["mix_norm"][i], g["ssd_w_in"], g["ssd_conv_w"], g["ssd_conv_b"], g["ssd_dt_bias"], g["ssd_a_log"],
             g["ssd_d"], g["ssd_norm_w"], g["ssd_w_out"]) = _ssd_bwd(
                 dx, sv, mn, w["ssd_w_in"], w["ssd_conv_w"], w["ssd_conv_b"], w["ssd_norm_w"], w["ssd_w_out"])
    g["mix_norm"] = jnp.concatenate(g["mix_norm"], axis=0)
    g["ffn_norm"] = jnp.concatenate(g["ffn_norm"], axis=0)
    g["conv_w_dw"] = jnp.stack(g["conv_w_dw"], axis=0)
    g["ssd_conv_w"] = g["ssd_conv_w"][None]
    return loss, dx, g, received


def _mesh_position():
    return lax.axis_index("x") * 4 + lax.axis_index("y") * 2 + lax.axis_index("c")


def _device_of(t):
    return (lax.shift_right_logical(t, 2), lax.bitwise_and(lax.shift_right_logical(t, 1), 1), lax.bitwise_and(t, 1))


def _exchange_copies(srcs_of, out_refs, send_sems, recv_sems, local_sems):
    me = _mesh_position()
    na = len(out_refs)
    locals_ = [pltpu.make_async_copy(srcs_of(a, me), out_refs[a].at[me], local_sems.at[a]) for a in range(na)]
    sends, arrivals = [], []
    for j in range(1, NDEV):
        t = lax.rem(me + j, NDEV)
        frm = lax.rem(me + NDEV - j, NDEV)
        for a in range(na):
            sends.append(pltpu.make_async_remote_copy(
                src_ref=srcs_of(a, t), dst_ref=out_refs[a].at[me], send_sem=send_sems.at[a, j - 1],
                recv_sem=recv_sems.at[a, j - 1], device_id=_device_of(t), device_id_type=pl.DeviceIdType.MESH))
            arrivals.append(pltpu.make_async_remote_copy(
                src_ref=srcs_of(a, me), dst_ref=out_refs[a].at[frm], send_sem=send_sems.at[a, j - 1],
                recv_sem=recv_sems.at[a, j - 1], device_id=_device_of(frm), device_id_type=pl.DeviceIdType.MESH))
    return locals_, sends, arrivals


def _exchange_start(copies):
    locals_, sends, _ = copies
    for cp in locals_ + sends:
        cp.start()


def _exchange_wait(copies):
    locals_, sends, arrivals = copies
    for cp in sends:
        cp.wait_send()
    for cp in arrivals:
        cp.wait_recv()
    for cp in locals_:
        cp.wait()


def _exchange_run(srcs_of, out_refs, send_sems, recv_sems, local_sems):
    copies = _exchange_copies(srcs_of, out_refs, send_sems, recv_sems, local_sems)
    _exchange_start(copies)
    _exchange_wait(copies)


def _exchange_parts(arrays, gather):
    na = len(arrays)
    outs = [_sds(((NDEV,) + a.shape) if gather else a.shape, a.dtype) for a in arrays]
    sems = [pltpu.SemaphoreType.DMA((na, NDEV - 1)), pltpu.SemaphoreType.DMA((na, NDEV - 1)), pltpu.SemaphoreType.DMA((na,))]
    pick = (lambda srcs: (lambda a, t: srcs[a])) if gather else (lambda srcs: (lambda a, t: srcs[a].at[t]))
    return pick, outs, sems


def _exchange(arrays, name, gather):
    na = len(arrays)
    pick, outs, sems = _exchange_parts(arrays, gather)

    def body(*refs):
        _exchange_run(pick(refs[:na]), refs[na:2 * na], *refs[2 * na:])

    hbm = pl.BlockSpec(memory_space=pl.ANY)
    return pl.pallas_call(body, name=name, in_specs=[hbm] * na, out_specs=[hbm] * na, out_shape=outs, scratch_shapes=sems)(*arrays)


def _all_sum_small(pack, name):
    def body(src_ref, out_ref, buf_ref, send_sems, recv_sems, local_sems):
        _exchange_run(lambda a, t: src_ref, [buf_ref], send_sems, recv_sems, local_sems)
        acc = buf_ref[0]
        for d in range(1, NDEV):
            acc = acc + buf_ref[d]
        out_ref[...] = acc

    vmem = pl.BlockSpec(memory_space=pltpu.VMEM)
    return pl.pallas_call(
        body, name=name, in_specs=[vmem], out_specs=vmem, out_shape=_sds(pack.shape, pack.dtype),
        scratch_shapes=[pltpu.VMEM((NDEV,) + pack.shape, pack.dtype), pltpu.SemaphoreType.DMA((1, NDEV - 1)),
                        pltpu.SemaphoreType.DMA((1, NDEV - 1)), pltpu.SemaphoreType.DMA((1,))])(pack)


def _sum_slabs(slabs, name):
    _, r, c = slabs.shape
    tr = r
    for cand in (256, 352):
        if r % cand == 0:
            tr = cand
            break

    def body(s_ref, o_ref):
        acc = s_ref[0].astype(F32)
        for d in range(1, NDEV):
            acc = acc + s_ref[d].astype(F32)
        o_ref[...] = acc

    return _pc(body, name, (r // tr,), [pl.BlockSpec((NDEV, tr, c), lambda i: (0, i, 0))],
               pl.BlockSpec((tr, c), lambda i: (i, 0)), _sds((r, c)))(slabs)


def _adamw(wt, g, m, v, name):
    shape = wt.shape
    w2, g2, m2, v2 = (a.reshape(-1, shape[-1]) for a in (wt, g, m, v))
    r, c = w2.shape
    tr = r
    for cand in (512, 352, 256):
        if r % cand == 0:
            tr = cand
            break
    c1 = 1.0 - ADAM_B1 ** ADAM_STEP
    c2 = 1.0 - ADAM_B2 ** ADAM_STEP

    def body(w_ref, g_ref, m_ref, v_ref, d_ref, mo_ref, vo_ref):
        gv = g_ref[...]
        mn = ADAM_B1 * m_ref[...] + (1.0 - ADAM_B1) * gv
        vn = ADAM_B2 * v_ref[...] + (1.0 - ADAM_B2) * (gv * gv)
        mo_ref[...] = mn
        vo_ref[...] = vn
        d_ref[...] = -ADAM_LR * ((mn / c1) / (jnp.sqrt(vn / c2) + ADAM_EPS) + ADAM_WD * w_ref[...])

    spec = pl.BlockSpec((tr, c), lambda i: (i, 0))
    outs = _pc(body, name, (r // tr,), [spec] * 4, [spec] * 3, [_sds((r, c))] * 3)(w2, g2, m2, v2)
    return tuple(o.reshape(shape) for o in outs)


_NAMES = ["mix_norm", "ffn_norm", "ffn_w_gu", "ffn_w_down", "conv_w_in", "conv_w_dw", "conv_w_out", "fox_w_in", "fox_b_f",
          "fox_q_gain", "fox_k_gain", "fox_w_out", "ssd_w_in", "ssd_conv_w", "ssd_conv_b", "ssd_dt_bias", "ssd_a_log",
          "ssd_d", "ssd_norm_w", "ssd_w_out"]
_MATRICES = ["ffn_w_gu", "ffn_w_down", "conv_w_in", "conv_w_out", "fox_w_in", "fox_w_out", "ssd_w_in", "ssd_w_out"]
_VECTORS = {"conv_w_dw": 2, "ssd_conv_w": 2, "ssd_conv_b": 1, "ssd_norm_w": 1}
_REPLICATED = ["mix_norm", "ffn_norm", "fox_b_f", "fox_q_gain", "fox_k_gain", "ssd_dt_bias", "ssd_a_log", "ssd_d"]


def _to_rows(flat):
    n = flat.shape[0]
    rows = -(-n // (8 * D_MODEL)) * 8
    return jnp.pad(flat, (0, rows * D_MODEL - n)).reshape(rows, D_MODEL)


def _full_shape(local_shape, axis):
    shp = list(local_shape)
    shp[axis] *= NDEV
    return tuple(shp)


def _cols_from_blocks(g):
    return jnp.moveaxis(g, 0, 1).reshape(g.shape[1], NDEV * g.shape[2])


def _blocks_from_cols(full):
    k, n8 = full.shape
    return jnp.moveaxis(full.reshape(k, NDEV, n8 // NDEV), 1, 0)


def kernel(x, mix_norm, ffn_norm, ffn_w_gu, ffn_w_down, conv_w_in, conv_w_dw, conv_w_out, fox_w_in, fox_b_f, fox_q_gain, fox_k_gain, fox_w_out, ssd_w_in, ssd_conv_w, ssd_conv_b, ssd_dt_bias, ssd_a_log, ssd_d, ssd_norm_w, ssd_w_out, loss_target, m_mix_norm, m_ffn_norm, m_ffn_w_gu, m_ffn_w_down, m_conv_w_in, m_conv_w_dw, m_conv_w_out, m_fox_w_in, m_fox_b_f, m_fox_q_gain, m_fox_k_gain, m_fox_w_out, m_ssd_w_in, m_ssd_conv_w, m_ssd_conv_b, m_ssd_dt_bias, m_ssd_a_log, m_ssd_d, m_ssd_norm_w, m_ssd_w_out, v_mix_norm, v_ffn_norm, v_ffn_w_gu, v_ffn_w_down, v_conv_w_in, v_conv_w_dw, v_conv_w_out, v_fox_w_in, v_fox_b_f, v_fox_q_gain, v_fox_k_gain, v_fox_w_out, v_ssd_w_in, v_ssd_conv_w, v_ssd_conv_b, v_ssd_dt_bias, v_ssd_a_log, v_ssd_d, v_ssd_norm_w, v_ssd_w_out):
    local = dict(mix_norm=mix_norm, ffn_norm=ffn_norm, ffn_w_gu=ffn_w_gu, ffn_w_down=ffn_w_down, conv_w_in=conv_w_in,
                 conv_w_dw=conv_w_dw, conv_w_out=conv_w_out, fox_w_in=fox_w_in, fox_b_f=fox_b_f, fox_q_gain=fox_q_gain,
                 fox_k_gain=fox_k_gain, fox_w_out=fox_w_out, ssd_w_in=ssd_w_in, ssd_conv_w=ssd_conv_w, ssd_conv_b=ssd_conv_b,
                 ssd_dt_bias=ssd_dt_bias, ssd_a_log=ssd_a_log, ssd_d=ssd_d, ssd_norm_w=ssd_norm_w, ssd_w_out=ssd_w_out)
    mom = dict(zip(_NAMES, [m_mix_norm, m_ffn_norm, m_ffn_w_gu, m_ffn_w_down, m_conv_w_in, m_conv_w_dw, m_conv_w_out, m_fox_w_in,
                            m_fox_b_f, m_fox_q_gain, m_fox_k_gain, m_fox_w_out, m_ssd_w_in, m_ssd_conv_w, m_ssd_conv_b,
                            m_ssd_dt_bias, m_ssd_a_log, m_ssd_d, m_ssd_norm_w, m_ssd_w_out]))
    var = dict(zip(_NAMES, [v_mix_norm, v_ffn_norm, v_ffn_w_gu, v_ffn_w_down, v_conv_w_in, v_conv_w_dw, v_conv_w_out, v_fox_w_in,
                            v_fox_b_f, v_fox_q_gain, v_fox_k_gain, v_fox_w_out, v_ssd_w_in, v_ssd_conv_w, v_ssd_conv_b,
                            v_ssd_dt_bias, v_ssd_a_log, v_ssd_d, v_ssd_norm_w, v_ssd_w_out]))

    shard = {k: local[k].astype(BF16) for k in _MATRICES}
    vec_pack = _to_rows(jnp.concatenate([local[k].reshape(-1) for k in _VECTORS]))
    first = _exchange([shard["ffn_w_gu"][0:1], shard["ffn_w_down"][0:1], shard["conv_w_in"][0:1], shard["conv_w_out"][0:1],
                       shard["fox_w_in"], shard["fox_w_out"], vec_pack], "gather_first", True)
    gvec = first[6].reshape(NDEV, -1)
    full = {k: local[k] for k in _REPLICATED}
    off = 0
    for k, axis in _VECTORS.items():
        n = local[k].size
        blk = jnp.moveaxis(gvec[:, off:off + n].reshape((NDEV,) + local[k].shape), 0, axis)
        full[k] = blk.reshape(_full_shape(local[k].shape, axis))
        off += n
    full["ssd_conv_w"] = full["ssd_conv_w"][0]
    full["ffn_w_gu"] = [first[0][:, 0]]
    full["ffn_w_down"] = [first[1][:, 0].reshape(4, FF_BLOCK, D_MODEL)]
    full["conv_w_in"] = [first[2][:, 0]]
    full["conv_w_out"] = [first[3][:, 0].reshape(D_MODEL, D_MODEL)]
    full["fox_w_in"] = jnp.pad(_cols_from_blocks(first[4][:, 0]), ((0, 0), (0, FOX_IN_PAD - FOX_IN)))
    full["fox_w_out"] = first[5].reshape(D_MODEL, D_MODEL)

    rest = [shard["ffn_w_gu"][1:], shard["ffn_w_down"][1:], shard["conv_w_in"][1:], shard["conv_w_out"][1:],
            shard["ssd_w_in"], shard["ssd_w_out"]]

    def finish(w, got):
        w = dict(w)
        w["ffn_w_gu"] = w["ffn_w_gu"] + [got[0][:, i] for i in range(DEPTH - 1)]
        w["ffn_w_down"] = w["ffn_w_down"] + [got[1][:, i].reshape(4, FF_BLOCK, D_MODEL) for i in range(DEPTH - 1)]
        w["conv_w_in"] = w["conv_w_in"] + [got[2][:, 0]]
        w["conv_w_out"] = w["conv_w_out"] + [got[3][:, 0].reshape(D_MODEL, D_MODEL)]
        w["ssd_w_in"] = jnp.pad(_cols_from_blocks(got[4][:, 0]), ((0, 0), (0, SSM_IN_PAD - SSM_IN)))
        w["ssd_w_out"] = got[5].reshape(SSM_INNER, D_MODEL)
        return w

    def early_slabs(g):
        return ([g["ffn_w_gu"][i] for i in range(1, DEPTH)]
                + [g["ffn_w_down"][i].reshape(NDEV, D_FF // NDEV, D_MODEL) for i in range(1, DEPTH)]
                + [g["conv_w_in"][1], g["conv_w_out"][1].reshape(NDEV, D_MODEL // NDEV, D_MODEL),
                   _blocks_from_cols(g["ssd_w_in"]), g["ssd_w_out"].reshape(NDEV, SSM_INNER // NDEV, D_MODEL)])

    loss_part, dx, grads, early = _local_step(x[0], loss_target[0], full, (rest, finish), early_slabs)

    late = _exchange([grads["ffn_w_gu"][0], grads["ffn_w_down"][0].reshape(NDEV, D_FF // NDEV, D_MODEL), grads["conv_w_in"][0],
                      grads["conv_w_out"][0].reshape(NDEV, D_MODEL // NDEV, D_MODEL), _blocks_from_cols(grads["fox_w_in"]),
                      grads["fox_w_out"].reshape(NDEV, D_MODEL // NDEV, D_MODEL)], "scatter_last", False)
    se = [_sum_slabs(r, f"sum_early_{n}") for n, r in enumerate(early)]
    sl = [_sum_slabs(r, f"sum_late_{n}") for n, r in enumerate(late)]
    shard_grad = {
        "ffn_w_gu": jnp.stack([sl[0]] + se[0:3]), "ffn_w_down": jnp.stack([sl[1]] + se[3:6]),
        "conv_w_in": jnp.stack([sl[2], se[6]]), "conv_w_out": jnp.stack([sl[3], se[7]]),
        "fox_w_in": sl[4][None], "fox_w_out": sl[5][None], "ssd_w_in": se[8][None], "ssd_w_out": se[9][None]}

    small_names = _REPLICATED + list(_VECTORS)
    small = [jnp.reshape(loss_part, (1,))] + [grads[k].reshape(-1) for k in small_names]
    total = _all_sum_small(_to_rows(jnp.concatenate(small)), "sum_small").reshape(-1)
    loss = total[0]
    off = 1
    me = _mesh_position()
    for k, part in zip(small_names, small[1:]):
        gk = total[off:off + part.shape[0]]
        off += part.shape[0]
        if k in _VECTORS:
            axis = _VECTORS[k]
            shp = local[k].shape
            gfull = gk.reshape(shp[:axis] + (NDEV, shp[axis]) + shp[axis + 1:])
            shard_grad[k] = lax.dynamic_index_in_dim(gfull, me, axis, keepdims=False)
        else:
            shard_grad[k] = gk.reshape(local[k].shape)

    deltas, new_m, new_v = {}, {}, {}
    for k in _NAMES:
        deltas[k], new_m[k], new_v[k] = _adamw(local[k], shard_grad[k], mom[k], var[k], f"adamw_{k}")
    return (loss, dx[None], *[shard_grad[k] for k in _NAMES], *[deltas[k] for k in _NAMES],
            *[new_m[k] for k in _NAMES], *[new_v[k] for k in _NAMES])
```

```python
import numpy as np

import jax
import jax.numpy as jnp
from jax import lax
from jax.experimental import pallas as pl
from jax.experimental.pallas import tpu as pltpu

F32 = jnp.float32
BF16 = jnp.bfloat16
HI = lax.Precision.HIGHEST

NDEV = 8
D_MODEL = 1024
DEPTH = 4
D_FF = 2816
FF_BLOCK = 2 * D_FF // NDEV
RMS_EPS = 1e-6
HEAD_DIM = 64
ATTN_HEADS = 16
FOX_IN = 3 * D_MODEL + ATTN_HEADS
FOX_IN_PAD = 3200
SSM_INNER = 2048
SSM_HEADS = 32
SSM_GROUPS = 8
SSM_STATE = 128
SSM_CHUNK = 128
SSM_CONV_DIM = 4096
SSM_IN = SSM_INNER + SSM_CONV_DIM + SSM_HEADS
SSM_IN_PAD = 6272
LANES = 128
V7X_VMEM_BYTES = 64 * 1024 * 1024
VMEM_LIMIT_BYTES = (V7X_VMEM_BYTES * 3) // 4
ATTN_BWD_VMEM_BYTES = (V7X_VMEM_BYTES * 7) // 8
LOG2E = 1.4426950408889634
LN2 = 0.6931471805599453
ATTN_TILE = 1024
ATTN_ROWS = 32

ADAM_LR = 0.001
ADAM_B1 = 0.9
ADAM_B2 = 0.999
ADAM_EPS = 1e-08
ADAM_WD = 0.01
ADAM_STEP = 10

_TILE_CANDIDATES = (1024, 1408, 896, 768, 640, 512, 384, 256, 128)


def _pick_tile(n):
    for c in _TILE_CANDIDATES:
        if n % c == 0:
            return c
    raise ValueError(f"no tile for {n}")


def _params(ngrid):
    return pltpu.CompilerParams(dimension_semantics=("arbitrary",) * ngrid, vmem_limit_bytes=VMEM_LIMIT_BYTES)


def _pc(body, name, grid, in_specs, out_specs, out_shape, scratch=()):
    return pl.pallas_call(
        body, name=name, grid=grid, in_specs=in_specs, out_specs=out_specs, out_shape=out_shape,
        scratch_shapes=list(scratch), compiler_params=_params(len(grid)))


def _dot(a, b, ca, cb, prec=None):
    return lax.dot_general(a, b, (((ca,), (cb,)), ((), ())), preferred_element_type=F32, precision=prec)


def _sds(shape, dtype=F32):
    return jax.ShapeDtypeStruct(shape, dtype)


def _row_tile(s, want=256):
    return want if s % want == 0 else s


def _sigmoid(x):
    return 1.0 / (1.0 + jnp.exp(-x))


def _softplus(x):
    return jnp.maximum(x, 0.0) + jnp.log(1.0 + jnp.exp(-jnp.abs(x)))


def _mm_spec(a, b, name, grid, a_spec, b_spec, o_spec, out, ca, cb, acc_shape, drop=(0, 0, 0), res=None, r_spec=None):
    nk = grid[2]
    da, db, do_ = drop
    has_res = res is not None

    def body(*refs):
        if has_res:
            a_ref, b_ref, r_ref, o_ref, acc_ref = refs
        else:
            a_ref, b_ref, o_ref, acc_ref = refs
        k = pl.program_id(2)

        @pl.when(k == 0)
        def _():
            acc_ref[...] = jnp.zeros_like(acc_ref)

        av = a_ref[(0,) * da] if da else a_ref[...]
        bv = b_ref[(0,) * db] if db else b_ref[...]
        acc_ref[...] += _dot(av.astype(BF16), bv.astype(BF16), ca, cb)

        @pl.when(k == nk - 1)
        def _():
            val = acc_ref[...]
            if has_res:
                val = val + r_ref[...]
            if do_:
                o_ref[(0,) * do_] = val.astype(out.dtype)
            else:
                o_ref[...] = val.astype(out.dtype)

    in_specs = [a_spec, b_spec] + ([r_spec] if has_res else [])
    args = (a, b) + ((res,) if has_res else ())
    return _pc(body, name, grid, in_specs, o_spec, out, [pltpu.VMEM(acc_shape, F32)])(*args)


def _mm(a, b, mode, name, out_dtype=F32, res=None):
    if mode == "tn":
        r, m = a.shape
        n = b.shape[1]
        tm, tn, tk = _pick_tile(m), _pick_tile(n), _pick_tile(r)
        grid = (m // tm, n // tn, r // tk)
        a_spec = pl.BlockSpec((tk, tm), lambda i, j, k: (k, i))
        b_spec = pl.BlockSpec((tk, tn), lambda i, j, k: (k, j))
        ca, cb = 0, 0
    else:
        m, kd = a.shape
        n = b.shape[1] if mode == "nn" else b.shape[0]
        tm, tn, tk = _pick_tile(m), _pick_tile(n), _pick_tile(kd)
        grid = (m // tm, n // tn, kd // tk)
        a_spec = pl.BlockSpec((tm, tk), lambda i, j, k: (i, k))
        if mode == "nn":
            b_spec = pl.BlockSpec((tk, tn), lambda i, j, k: (k, j))
            ca, cb = 1, 0
        else:
            b_spec = pl.BlockSpec((tn, tk), lambda i, j, k: (j, k))
            ca, cb = 1, 1
    o_spec = pl.BlockSpec((tm, tn), lambda i, j, k: (i, j))
    return _mm_spec(a, b, name, grid, a_spec, b_spec, o_spec, _sds((m, n), out_dtype), ca, cb, (tm, tn), res=res, r_spec=o_spec)


def _rms_fwd(x, w, name):
    s, d = x.shape
    ts = _row_tile(s)

    def body(x_ref, w_ref, o_ref):
        xv = x_ref[...]
        r = lax.rsqrt(jnp.mean(xv * xv, axis=-1, keepdims=True) + RMS_EPS)
        o_ref[...] = ((xv * r) * w_ref[...]).astype(BF16)

    row = pl.BlockSpec((ts, d), lambda i: (i, 0))
    return _pc(body, name, (s // ts,), [row, pl.BlockSpec((1, d), lambda i: (0, 0))], row, _sds((s, d), BF16))(x, w)


def _mm_dnorm(a, b, name, nk, a_spec, b_spec, ca, cb, drop, x, w, dres):
    s, d = x.shape
    tm = _pick_tile(s)
    da, db = drop

    def body(a_ref, b_ref, x_ref, w_ref, r_ref, dx_ref, dw_ref, acc_ref):
        i = pl.program_id(0)
        k = pl.program_id(2)

        @pl.when(k == 0)
        def _():
            acc_ref[...] = jnp.zeros_like(acc_ref)

        av = a_ref[(0,) * da] if da else a_ref[...]
        bv = b_ref[(0,) * db] if db else b_ref[...]
        acc_ref[...] += _dot(av.astype(BF16), bv.astype(BF16), ca, cb)

        @pl.when(k == nk - 1)
        def _():
            dhv = acc_ref[...]
            xv = x_ref[...]
            r = lax.rsqrt(jnp.mean(xv * xv, axis=-1, keepdims=True) + RMS_EPS)
            xhat = xv * r
            g = dhv * w_ref[...]
            dx_ref[...] = r_ref[...] + r * (g - xhat * jnp.mean(g * xhat, axis=-1, keepdims=True))
            part = jnp.sum(dhv * xhat, axis=0, keepdims=True)

            @pl.when(i == 0)
            def _():
                dw_ref[...] = part

            @pl.when(i > 0)
            def _():
                dw_ref[...] += part

    row = pl.BlockSpec((tm, d), lambda i, j, k: (i, 0))
    vec = pl.BlockSpec((1, d), lambda i, j, k: (0, 0))
    return _pc(body, name, (s // tm, 1, nk), [a_spec, b_spec, row, vec, row], [row, vec], [_sds((s, d)), _sds((1, d))],
               [pltpu.VMEM((tm, d), F32)])(a, b, x, w, dres)


def _mm_dnorm_nt(dproj, w_in, name, x, w, dres):
    tm = _pick_tile(x.shape[0])
    tk = _pick_tile(dproj.shape[1])
    return _mm_dnorm(dproj, w_in, name, dproj.shape[1] // tk, pl.BlockSpec((tm, tk), lambda i, j, k: (i, k)),
                     pl.BlockSpec((D_MODEL, tk), lambda i, j, k: (0, k)), 1, 1, (0, 0), x, w, dres)


def _ffn_gate_up(h, w_gu, name):
    s = h.shape[0]
    tm = _pick_tile(s)

    def body(h_ref, wg_ref, wu_ref, gu_ref, a_ref):
        hv = h_ref[...]
        g = _dot(hv, wg_ref[0], 1, 0)
        u = _dot(hv, wu_ref[0], 1, 0)
        gu_ref[0, 0] = g.astype(BF16)
        gu_ref[0, 1] = u.astype(BF16)
        a_ref[0] = (g * _sigmoid(g) * u).astype(BF16)

    wblk = lambda off: pl.BlockSpec((1, D_MODEL, FF_BLOCK), lambda i, k: (k + off, 0, 0))
    return _pc(body, name, (s // tm, 4), [pl.BlockSpec((tm, D_MODEL), lambda i, k: (i, 0)), wblk(0), wblk(4)],
               [pl.BlockSpec((1, 2, tm, FF_BLOCK), lambda i, k: (k, 0, i, 0)), pl.BlockSpec((1, tm, FF_BLOCK), lambda i, k: (k, i, 0))],
               [_sds((4, 2, s, FF_BLOCK), BF16), _sds((4, s, FF_BLOCK), BF16)])(h, w_gu, w_gu)


def _ffn_dgate_up(dy, w_down, gu, name):
    s = dy.shape[0]
    tm = _pick_tile(s)

    def body(dy_ref, w_ref, gu_ref, o_ref):
        dav = _dot(dy_ref[...].astype(BF16), w_ref[0], 1, 1)
        g = gu_ref[0, 0].astype(F32)
        u = gu_ref[0, 1].astype(F32)
        sg = _sigmoid(g)
        o_ref[0, 0] = (dav * u * (sg * (1.0 + g * (1.0 - sg)))).astype(BF16)
        o_ref[0, 1] = (dav * (g * sg)).astype(BF16)

    pair = pl.BlockSpec((1, 2, tm, FF_BLOCK), lambda i, k: (k, 0, i, 0))
    return _pc(body, name, (s // tm, 4),
               [pl.BlockSpec((tm, D_MODEL), lambda i, k: (i, 0)), pl.BlockSpec((1, FF_BLOCK, D_MODEL), lambda i, k: (k, 0, 0)), pair],
               pair, _sds((4, 2, s, FF_BLOCK), BF16))(dy, w_down, gu)


def _ffn_fwd(x, norm_w, w_gu, w_down, tag):
    s = x.shape[0]
    tm = _pick_tile(s)
    h = _rms_fwd(x, norm_w, f"ffn_norm_{tag}")
    gu, a = _ffn_gate_up(h, w_gu, f"ffn_gu_{tag}")
    xspec = pl.BlockSpec((tm, D_MODEL), lambda i, j, k: (i, 0))
    y = _mm_spec(a, w_down, f"ffn_down_{tag}", (s // tm, 1, 4),
                 pl.BlockSpec((1, tm, FF_BLOCK), lambda i, j, k: (k, i, 0)),
                 pl.BlockSpec((1, FF_BLOCK, D_MODEL), lambda i, j, k: (k, 0, 0)),
                 xspec, _sds((s, D_MODEL)), 1, 0, (tm, D_MODEL), drop=(1, 1, 0), res=x, r_spec=xspec)
    return y, (x, h, gu, a)


def _ffn_bwd(dy, saved, norm_w, w_gu, w_down, tag):
    x, h, gu, a = saved
    s = x.shape[0]
    tm = _pick_tile(s)
    g_down = _mm_spec(a, dy, f"ffn_gdown_{tag}", (4, 1, s // tm),
                      pl.BlockSpec((1, tm, FF_BLOCK), lambda i, j, k: (i, k, 0)),
                      pl.BlockSpec((tm, D_MODEL), lambda i, j, k: (k, 0)),
                      pl.BlockSpec((1, FF_BLOCK, D_MODEL), lambda i, j, k: (i, 0, 0)),
                      _sds((4, FF_BLOCK, D_MODEL), BF16), 0, 0, (FF_BLOCK, D_MODEL), drop=(1, 0, 1))
    dgu = _ffn_dgate_up(dy, w_down, gu, f"ffn_dgu_{tag}")
    g_gu = _mm_spec(h, dgu, f"ffn_ggu_{tag}", (NDEV, 1, s // tm),
                    pl.BlockSpec((tm, D_MODEL), lambda i, j, k: (k, 0)),
                    pl.BlockSpec((1, 1, tm, FF_BLOCK), lambda i, j, k: (i % 4, i // 4, k, 0)),
                    pl.BlockSpec((1, D_MODEL, FF_BLOCK), lambda i, j, k: (i, 0, 0)),
                    _sds((NDEV, D_MODEL, FF_BLOCK), BF16), 0, 0, (D_MODEL, FF_BLOCK), drop=(0, 2, 1))
    dx, g_norm = _mm_dnorm(dgu, w_gu, f"ffn_dh_{tag}", NDEV,
                           pl.BlockSpec((1, 1, tm, FF_BLOCK), lambda i, j, k: (k % 4, k // 4, i, 0)),
                           pl.BlockSpec((1, D_MODEL, FF_BLOCK), lambda i, j, k: (k, 0, 0)), 1, 1, (2, 1), x, norm_w, dy)
    return dx, g_norm, g_gu, g_down


def _prev_rows(cur, halo, j, first):
    rid = lax.broadcasted_iota(jnp.int32, cur.shape, 0)
    hid = lax.broadcasted_iota(jnp.int32, halo.shape, 0)
    out = pltpu.roll(cur, j, 0)
    for t in range(j):
        row = jnp.sum(jnp.where(hid == 8 - j + t, halo, 0.0), axis=0, keepdims=True)
        row = jnp.where(first, 0.0, row)
        out = jnp.where(rid == t, row, out)
    return out


def _next_rows(cur, halo, j, last):
    ts = cur.shape[0]
    rid = lax.broadcasted_iota(jnp.int32, cur.shape, 0)
    hid = lax.broadcasted_iota(jnp.int32, halo.shape, 0)
    out = pltpu.roll(cur, ts - j, 0)
    for t in range(j):
        row = jnp.sum(jnp.where(hid == t, halo, 0.0), axis=0, keepdims=True)
        row = jnp.where(last, 0.0, row)
        out = jnp.where(rid == ts - j + t, row, out)
    return out


def _halo_specs(ts, s, width, col):
    per = ts // 8
    nblk = s // 8
    prev = pl.BlockSpec((8, width), lambda i: (jnp.maximum(i * per - 1, 0), col))
    nxt = pl.BlockSpec((8, width), lambda i: (jnp.minimum((i + 1) * per, nblk - 1), col))
    return prev, nxt


def _cgate_fwd(p, w_dw, name):
    s = p.shape[0]
    d = D_MODEL
    ts = _row_tile(s)
    prev, _ = _halo_specs(ts, s, 3 * d, 0)

    def body(p_ref, h_ref, w_ref, z_ref):
        first = pl.program_id(0) == 0
        b = p_ref[:, :d]
        cv = p_ref[:, d:2 * d] * p_ref[:, 2 * d:]
        hcv = h_ref[:, d:2 * d] * h_ref[:, 2 * d:]
        u = w_ref[2:3, :] * cv + w_ref[1:2, :] * _prev_rows(cv, hcv, 1, first) + w_ref[0:1, :] * _prev_rows(cv, hcv, 2, first)
        z_ref[...] = (b * u).astype(BF16)

    return _pc(body, name, (s // ts,),
               [pl.BlockSpec((ts, 3 * d), lambda i: (i, 0)), prev, pl.BlockSpec((3, d), lambda i: (0, 0))],
               pl.BlockSpec((ts, d), lambda i: (i, 0)), _sds((s, d), BF16))(p, p, w_dw)


def _cgate_bwd(p, dz, w_dw, name):
    s = p.shape[0]
    d = D_MODEL
    ts = _row_tile(s)
    nt = s // ts
    p_prev, p_next = _halo_specs(ts, s, 3 * d, 0)
    _, dz_next = _halo_specs(ts, s, d, 0)

    def body(p_ref, hp_ref, hn_ref, dz_ref, dzn_ref, w_ref, dp_ref, dw_ref):
        i = pl.program_id(0)
        first = i == 0
        last = i == nt - 1
        b = p_ref[:, :d]
        c = p_ref[:, d:2 * d]
        v = p_ref[:, 2 * d:]
        cv = c * v
        hcv = hp_ref[:, d:2 * d] * hp_ref[:, 2 * d:]
        cv1 = _prev_rows(cv, hcv, 1, first)
        cv2 = _prev_rows(cv, hcv, 2, first)
        w0, w1, w2 = w_ref[0:1, :], w_ref[1:2, :], w_ref[2:3, :]
        u = w2 * cv + w1 * cv1 + w0 * cv2
        dzv = dz_ref[...]
        du = dzv * b
        dun = dzn_ref[...] * hn_ref[:, :d]
        dcv = w2 * du + w1 * _next_rows(du, dun, 1, last) + w0 * _next_rows(du, dun, 2, last)
        dp_ref[:, :d] = (dzv * u).astype(BF16)
        dp_ref[:, d:2 * d] = (dcv * v).astype(BF16)
        dp_ref[:, 2 * d:] = (dcv * c).astype(BF16)

        @pl.when(first)
        def _():
            dw_ref[...] = jnp.zeros_like(dw_ref)

        dw_ref[0:1, :] += jnp.sum(du * cv2, axis=0, keepdims=True)
        dw_ref[1:2, :] += jnp.sum(du * cv1, axis=0, keepdims=True)
        dw_ref[2:3, :] += jnp.sum(du * cv, axis=0, keepdims=True)

    wide = pl.BlockSpec((ts, 3 * d), lambda i: (i, 0))
    wspec = pl.BlockSpec((3, d), lambda i: (0, 0))
    return _pc(body, name, (nt,),
               [wide, p_prev, p_next, pl.BlockSpec((ts, d), lambda i: (i, 0)), dz_next, wspec],
               [wide, wspec], [_sds((s, 3 * d), BF16), _sds((3, d))])(p, p, p, dz, dz, w_dw)


def _conv_fwd(x, norm_w, w_in, w_dw, w_out, tag):
    wn = _cols_from_blocks(w_in)
    h = _rms_fwd(x, norm_w, f"conv_norm_{tag}")
    p = _mm(h, wn, "nn", f"conv_in_{tag}")
    z = _cgate_fwd(p, w_dw, f"conv_gate_{tag}")
    y = _mm(z, w_out, "nn", f"conv_out_{tag}", res=x)
    return y, (x, h, p, z, wn)


def _conv_bwd(dy, saved, norm_w, w_in, w_dw, w_out, tag):
    x, h, p, z, wn = saved
    dz = _mm(dy, w_out, "nt", f"conv_dz_{tag}")
    g_out = _mm(z, dy, "tn", f"conv_gout_{tag}", out_dtype=BF16)
    dp, g_dw = _cgate_bwd(p, dz, w_dw, f"conv_dgate_{tag}")
    g_in = _blocks_from_cols(_mm(h, dp, "tn", f"conv_gin_{tag}", out_dtype=BF16))
    dx, g_norm = _mm_dnorm_nt(dp, wn, f"conv_dh_{tag}", x, norm_w, dy)
    return dx, g_norm, g_in, g_dw, g_out


def _tri(lower):
    r = lax.broadcasted_iota(jnp.int32, (LANES, LANES), 0)
    c = lax.broadcasted_iota(jnp.int32, (LANES, LANES), 1)
    return jnp.where((r >= c) if lower else (r <= c), 1.0, 0.0).astype(F32)


def _cumsum_rows(v, reverse, name):
    s = v.shape[0]
    n = s // LANES
    idx = (lambda i: (n - 1 - i, 0)) if reverse else (lambda i: (i, 0))

    def body(v_ref, o_ref, carry_ref):
        @pl.when(pl.program_id(0) == 0)
        def _():
            carry_ref[...] = jnp.zeros_like(carry_ref)

        blk = v_ref[...]
        o_ref[...] = _dot(_tri(not reverse), blk, 1, 0, HI) + carry_ref[0:1, :]
        carry_ref[...] += jnp.sum(blk, axis=0, keepdims=True)

    spec = pl.BlockSpec((LANES, LANES), idx)
    return _pc(body, name, (n,), [spec], spec, _sds((s, LANES)), [pltpu.VMEM((8, LANES), F32)])(v)


def _lo_mask(shape):
    return lax.broadcasted_iota(jnp.int32, shape, len(shape) - 1) < HEAD_DIM


def _half_sums(v, lo):
    sa = jnp.sum(jnp.where(lo, v, 0.0), axis=-1, keepdims=True)
    sb = jnp.sum(jnp.where(lo, 0.0, v), axis=-1, keepdims=True)
    return jnp.where(lo, sa, sb)


def _fox_prep_fwd(proj, gq, gk, name):
    s = proj.shape[0]
    ts = _row_tile(s, 512)
    qscale = HEAD_DIM ** -0.5 * LOG2E

    def body(q_ref, k_ref, v_ref, gq_ref, gk_ref, qo_ref, ko_ref, vo_ref):
        lo = _lo_mask((ts, LANES))

        def hnorm(xv, g):
            ms = _half_sums(xv * xv, lo) * (1.0 / HEAD_DIM)
            return (xv * lax.rsqrt(ms + RMS_EPS)) * g

        qo_ref[...] = (hnorm(q_ref[...], gq_ref[...]) * qscale).astype(BF16)
        ko_ref[...] = hnorm(k_ref[...], gk_ref[...]).astype(BF16)
        vo_ref[...] = v_ref[...].astype(BF16)

    def col(off):
        return pl.BlockSpec((ts, LANES), lambda i, p: (i, off + p))

    gspec = pl.BlockSpec((1, LANES), lambda i, p: (0, 0))
    out = _sds((s, D_MODEL), BF16)
    return _pc(body, name, (s // ts, 8), [col(0), col(8), col(16), gspec, gspec], [col(0)] * 3, [out] * 3)(
        proj, proj, proj, gq, gk)


def _fox_logf(proj, bf, name):
    s = proj.shape[0]
    ts = _row_tile(s, 512)

    def body(f_ref, b_ref, o_ref):
        z = f_ref[...] + b_ref[...]
        lf = jnp.minimum(z, 0.0) - jnp.log(1.0 + jnp.exp(-jnp.abs(z)))
        real = lax.broadcasted_iota(jnp.int32, (ts, LANES), 1) < ATTN_HEADS
        o_ref[...] = jnp.where(real, lf, 0.0)

    return _pc(body, name, (s // ts,), [pl.BlockSpec((ts, LANES), lambda i: (i, 24)), pl.BlockSpec((1, LANES), lambda i: (0, 0))],
               pl.BlockSpec((ts, LANES), lambda i: (i, 0)), _sds((s, LANES)))(proj, bf)


def _fox_dlogf(proj, bf, dlf, name):
    s = proj.shape[0]
    ts = _row_tile(s, 512)

    def body(f_ref, b_ref, d_ref, o_ref, db_ref):
        z = f_ref[...] + b_ref[...]
        real = lax.broadcasted_iota(jnp.int32, (ts, LANES), 1) < ATTN_HEADS
        g = jnp.where(real, d_ref[...] * _sigmoid(-z), 0.0)
        o_ref[...] = g.astype(BF16)

        @pl.when(pl.program_id(0) == 0)
        def _():
            db_ref[...] = jnp.zeros_like(db_ref)

        db_ref[...] += jnp.sum(g, axis=0, keepdims=True)

    vec = pl.BlockSpec((1, LANES), lambda i: (0, 0))
    row = pl.BlockSpec((ts, LANES), lambda i: (i, 0))
    return _pc(body, name, (s // ts,), [pl.BlockSpec((ts, LANES), lambda i: (i, 24)), vec, row], [row, vec],
               [_sds((s, LANES), BF16), _sds((1, LANES))])(proj, bf, dlf)


def _decay_terms(cum):
    s = cum.shape[0]
    c2 = cum * LOG2E
    hi = lax.reduce_precision(c2, 8, 7)
    mid = lax.reduce_precision(c2 - hi, 8, 7)
    low = lax.reduce_precision(c2 - hi - mid, 8, 7)
    one = jnp.ones_like(hi)

    def place(terms):
        tt = jnp.stack(terms, axis=-1).astype(BF16).reshape(s, 8, 2, 6)
        z = jnp.zeros((s, 8, HEAD_DIM - 6), BF16)
        return jnp.concatenate([tt[:, :, 1], z, tt[:, :, 0], z], axis=-1).reshape(s, D_MODEL)

    return place([hi, mid, low, one, one, one]), place([one, one, one, -hi, -mid, -low])


def _attn_tiles(s):
    t = s
    for cand in (ATTN_TILE, ATTN_TILE // 2):
        if s % cand == 0:
            t = cand
            break
    return t, s // t


def _tri_steps(n, by_key):
    if by_key:
        pairs = [(q, k) for k in range(n) for q in range(k, n)]
    else:
        pairs = [(q, k) for q in range(n) for k in range(q + 1)]
    arr = np.asarray(pairs, np.int32)
    return jnp.asarray(arr[:, 0]), jnp.asarray(arr[:, 1])


def _attn_call(body, name, s, by_key, inputs, in_kinds, out_kinds, out_shapes, scratch, hosted=None, vmem=VMEM_LIMIT_BYTES):
    t, n = _attn_tiles(s)
    qi_arr, ki_arr = _tri_steps(n, by_key)
    nsteps = int(qi_arr.shape[0])
    specs = {
        "q": pl.BlockSpec((t, LANES), lambda p, i, qi, ki: (qi[i], p)),
        "k": pl.BlockSpec((t, LANES), lambda p, i, qi, ki: (ki[i], p)),
        "r": pl.BlockSpec((1, 2, t), lambda p, i, qi, ki: (p, 0, qi[i])),
        "m": pl.BlockSpec((1, t, t), lambda p, i, qi, ki: (jnp.where(qi[i] == ki[i], 1, 0), 0, 0)),
        "Q": pl.BlockSpec((1, LANES, s), lambda p, i, qi, ki: (p, 0, 0)),
        "R": pl.BlockSpec((1, 2, s), lambda p, i, qi, ki: (p, 0, 0)),
    }
    in_specs = [specs[c] for c in in_kinds]
    out_specs = [specs[c] for c in out_kinds]
    out_shapes, scratch, inputs = list(out_shapes), list(scratch), list(inputs)
    run = body
    if hosted is not None:
        arrays, gather = hosted
        na, n_in, n_out, n_scr = len(arrays), len(inputs), len(out_kinds), len(scratch)
        pick, xouts, sems = _exchange_parts(arrays, gather)

        def run(qi_ref, ki_ref, *refs):
            ins, srcs = refs[:n_in], refs[n_in:n_in + na]
            outs, dsts = refs[n_in + na:n_in + na + n_out], refs[n_in + na + n_out:n_in + 2 * na + n_out]
            scr, xsems = refs[n_in + 2 * na + n_out:n_in + 2 * na + n_out + n_scr], refs[n_in + 2 * na + n_out + n_scr:]
            p = pl.program_id(0)
            i = pl.program_id(1)

            @pl.when(jnp.logical_and(p == 0, i == 0))
            def _():
                _exchange_start(_exchange_copies(pick(srcs), dsts, *xsems))

            body(qi_ref, ki_ref, *ins, *outs, *scr)

            @pl.when(jnp.logical_and(p == 7, i == nsteps - 1))
            def _():
                _exchange_wait(_exchange_copies(pick(srcs), dsts, *xsems))

        hbm = pl.BlockSpec(memory_space=pl.ANY)
        in_specs += [hbm] * na
        out_specs += [hbm] * na
        out_shapes += xouts
        scratch += sems
        inputs += list(arrays)
    grid_spec = pltpu.PrefetchScalarGridSpec(
        num_scalar_prefetch=2, grid=(8, nsteps), in_specs=in_specs, out_specs=out_specs, scratch_shapes=scratch)
    params = pltpu.CompilerParams(dimension_semantics=("arbitrary", "arbitrary"), vmem_limit_bytes=vmem)
    return pl.pallas_call(run, name=name, grid_spec=grid_spec, out_shape=out_shapes, compiler_params=params)(
        qi_arr, ki_arr, *inputs)


def _biased_kq(q2, k2, aq, ak, lo):
    sa = _dot(jnp.where(lo, k2, ak), jnp.where(lo, q2, aq), 1, 1)
    sb = _dot(jnp.where(lo, ak, k2), jnp.where(lo, aq, q2), 1, 1)
    return sa, sb


def _causal_bias(s):
    t, _ = _attn_tiles(s)
    kid = lax.broadcasted_iota(jnp.int32, (t, t), 0)
    qid = lax.broadcasted_iota(jnp.int32, (t, t), 1)
    return jnp.stack([jnp.zeros((t, t), BF16), jnp.where(kid > qid, -jnp.inf, 0.0).astype(BF16)])


def _fold8(v, op):
    return op(v.reshape(v.shape[0] // 8, 8, v.shape[1]), axis=0)


def _chunk(ref, mask_ref, hd, r):
    rows = slice(r * ATTN_ROWS, (r + 1) * ATTN_ROWS)
    return rows, ref[hd, rows, :] + mask_ref[0, rows, :].astype(F32)


def _flash_fwd(qs, kn, vb, augq, augk, cmask, name, hosted=None):
    s = qs.shape[0]
    t, n = _attn_tiles(s)
    nch = t // ATTN_ROWS

    def body(qi_ref, ki_ref, q_ref, k_ref, v_ref, aq_ref, ak_ref, mk_ref, o_ref, lse_ref, s_ref, p_ref, m_ref, l_ref, acc_ref):
        i = pl.program_id(1)
        qi = qi_ref[i]
        ki = ki_ref[i]

        @pl.when(ki == 0)
        def _():
            m_ref[...] = jnp.full_like(m_ref, -jnp.inf)
            l_ref[...] = jnp.zeros_like(l_ref)
            acc_ref[...] = jnp.zeros_like(acc_ref)

        lo = _lo_mask((t, LANES))
        rowlo = lax.broadcasted_iota(jnp.int32, (LANES, t), 0) < HEAD_DIM
        v2 = v_ref[...]
        sa, sb = _biased_kq(q_ref[...], k_ref[...], aq_ref[...], ak_ref[...], lo)
        s_ref[0] = sa
        s_ref[1] = sb
        alphas, pvs = [], []
        for hd in range(2):
            mx = jnp.full((8, t), -jnp.inf, F32)
            for r in range(nch):
                _, sc = _chunk(s_ref, mk_ref, hd, r)
                mx = jnp.maximum(mx, _fold8(sc, jnp.max))
            m_prev = m_ref[hd:hd + 1, :]
            m_new = jnp.maximum(m_prev, jnp.max(mx, axis=0, keepdims=True))
            ls = jnp.zeros((8, t), F32)
            for r in range(nch):
                rows, sc = _chunk(s_ref, mk_ref, hd, r)
                pm = jnp.exp2(sc - m_new)
                ls = ls + _fold8(pm, jnp.sum)
                p_ref[hd, rows, :] = pm.astype(BF16)
            alpha = jnp.exp2(m_prev - m_new)
            l_ref[hd:hd + 1, :] = alpha * l_ref[hd:hd + 1, :] + jnp.sum(ls, axis=0, keepdims=True)
            m_ref[hd:hd + 1, :] = m_new
            alphas.append(alpha)
            pvs.append(_dot(v2, p_ref[hd], 0, 0))
        acc_ref[...] = jnp.where(rowlo, alphas[0], alphas[1]) * acc_ref[...] + jnp.where(rowlo, pvs[0], pvs[1])

        @pl.when(ki == qi)
        def _():
            o_ref[...] = (acc_ref[...] / jnp.where(rowlo, l_ref[0:1, :], l_ref[1:2, :])).T
            lse_ref[0] = m_ref[0:2, :] + jnp.log2(l_ref[0:2, :])

    stat = pltpu.VMEM((8, t), F32)
    return _attn_call(body, name, s, False, (qs, kn, vb, augq, augk, cmask), "qkkqkm", "qr",
                      [_sds((s, D_MODEL)), _sds((8, 2, s))],
                      [pltpu.VMEM((2, t, t), F32), pltpu.VMEM((2, t, t), BF16), stat, stat, pltpu.VMEM((LANES, t), F32)],
                      hosted=hosted)


def _fox_delta(do, o, name):
    s = do.shape[0]
    ts = _row_tile(s, 512)

    def body(do_ref, o_ref, d_ref):
        d_ref[...] = _half_sums(do_ref[...] * o_ref[...], _lo_mask((ts, LANES)))

    spec = pl.BlockSpec((ts, LANES), lambda i, p: (i, p))
    return _pc(body, name, (s // ts, 8), [spec, spec], spec, _sds((s, D_MODEL)))(do, o)


def _bwd_tile(q_ref, k_ref, v_ref, aq_ref, ak_ref, do_ref, s_ref, dp_ref, lo):
    do2 = do_ref[...].astype(BF16)
    zero = jnp.zeros_like(do2)
    v2 = v_ref[...]
    sa, sb = _biased_kq(q_ref[...], k_ref[...], aq_ref[...], ak_ref[...], lo)
    s_ref[0] = sa
    s_ref[1] = sb
    dp_ref[0] = _dot(v2, jnp.where(lo, do2, zero), 1, 1)
    dp_ref[1] = _dot(v2, jnp.where(lo, zero, do2), 1, 1)
    return do2


def _bwd_chunk(s_ref, dp_ref, mk_ref, lse_ref, dl_ref, hd, r):
    rows, sc = _chunk(s_ref, mk_ref, hd, r)
    pm = jnp.exp2(sc - lse_ref[0, hd:hd + 1, :])
    ds = pm * (dp_ref[hd, rows, :] - dl_ref[0, hd:hd + 1, :])
    return rows, pm, ds


def _flash_bwd(qs, kn, vb, augq, augk, cmask, do, lse, delta, name, hosted=None):
    s = qs.shape[0]
    t, n = _attn_tiles(s)
    nch = t // ATTN_ROWS

    def body(qi_ref, ki_ref, q_ref, k_ref, v_ref, aq_ref, ak_ref, mk_ref, do_ref, lse_ref, dl_ref,
             dk_ref, dv_ref, dc_ref, dq_ref, dcq_ref, s_ref, dp_ref, p_ref, ds_ref, dka_ref, dva_ref, dca_ref):
        i = pl.program_id(1)
        qi = qi_ref[i]
        ki = ki_ref[i]

        @pl.when(i == 0)
        def _():
            dq_ref[...] = jnp.zeros_like(dq_ref)
            dcq_ref[...] = jnp.zeros_like(dcq_ref)

        @pl.when(qi == ki)
        def _():
            dka_ref[...] = jnp.zeros_like(dka_ref)
            dva_ref[...] = jnp.zeros_like(dva_ref)
            dca_ref[...] = jnp.zeros_like(dca_ref)

        lo = _lo_mask((t, LANES))
        rowlo = lax.broadcasted_iota(jnp.int32, (LANES, t), 0) < HEAD_DIM
        do2 = _bwd_tile(q_ref, k_ref, v_ref, aq_ref, ak_ref, do_ref, s_ref, dp_ref, lo)
        q2 = q_ref[...]
        k2 = k_ref[...]
        qcols = pl.ds(pl.multiple_of(qi * t, t), t)
        dvs, dks, dqs = [], [], []
        for hd in range(2):
            rs = jnp.zeros((8, t), F32)
            for r in range(nch):
                rows, pm, ds = _bwd_chunk(s_ref, dp_ref, mk_ref, lse_ref, dl_ref, hd, r)
                rs = rs + _fold8(ds, jnp.sum)
                part = ds[:, 0:LANES]
                for c in range(1, t // LANES):
                    part = part + ds[:, c * LANES:(c + 1) * LANES]
                dca_ref[hd, rows, :] += part
                p_ref[hd, rows, :] = pm.astype(BF16)
                ds_ref[hd, rows, :] = ds.astype(BF16)
            dcq_ref[0, hd:hd + 1, qcols] += jnp.sum(rs, axis=0, keepdims=True)
            dvs.append(_dot(p_ref[hd], do2, 1, 0))
            dks.append(_dot(ds_ref[hd], q2, 1, 0))
            dqs.append(_dot(k2, ds_ref[hd], 0, 0))
        dva_ref[...] += jnp.where(lo, dvs[0], dvs[1])
        dka_ref[...] += jnp.where(lo, dks[0], dks[1])
        dq_ref[0, :, qcols] += jnp.where(rowlo, dqs[0], dqs[1])

        @pl.when(qi == n - 1)
        def _():
            dk_ref[...] = dka_ref[...] * LN2
            dv_ref[...] = dva_ref[...]
            dc_ref[...] = -jnp.where(lo, jnp.sum(dca_ref[0], axis=-1, keepdims=True), jnp.sum(dca_ref[1], axis=-1, keepdims=True))

    out = _sds((s, D_MODEL))
    return _attn_call(body, name, s, True, (qs, kn, vb, augq, augk, cmask, do, lse, delta), "qkkqkmqrr", "kkkQR",
                      [out, out, out, _sds((8, LANES, s)), _sds((8, 2, s))],
                      [pltpu.VMEM((2, t, t), F32), pltpu.VMEM((2, t, t), F32), pltpu.VMEM((2, t, t), BF16),
                       pltpu.VMEM((2, t, t), BF16), pltpu.VMEM((t, LANES), F32), pltpu.VMEM((t, LANES), F32),
                       pltpu.VMEM((2, t, LANES), F32)], hosted=hosted, vmem=ATTN_BWD_VMEM_BYTES)


def _fox_prep_bwd(proj, dqs, dk, dv, gq, gk, name):
    s = proj.shape[0]
    ts = _row_tile(s, 512)
    scale = HEAD_DIM ** -0.5

    def body(q_ref, k_ref, dq_ref, dk_ref, dv_ref, gq_ref, gk_ref, oq_ref, ok_ref, ov_ref, dgq_ref, dgk_ref):
        lo = _lo_mask((ts, LANES))

        @pl.when(jnp.logical_and(pl.program_id(0) == 0, pl.program_id(1) == 0))
        def _():
            dgq_ref[...] = jnp.zeros_like(dgq_ref)
            dgk_ref[...] = jnp.zeros_like(dgk_ref)

        def back(xv, dout, g):
            r = lax.rsqrt(_half_sums(xv * xv, lo) * (1.0 / HEAD_DIM) + RMS_EPS)
            y = xv * r
            dy = dout * g
            dx = r * (dy - y * (_half_sums(dy * y, lo) * (1.0 / HEAD_DIM)))
            return dx, jnp.sum(dout * y, axis=0, keepdims=True)

        dxq, dgq = back(q_ref[...], dq_ref[0].T * scale, gq_ref[...])
        dxk, dgk = back(k_ref[...], dk_ref[...], gk_ref[...])
        oq_ref[...] = dxq.astype(BF16)
        ok_ref[...] = dxk.astype(BF16)
        ov_ref[...] = dv_ref[...].astype(BF16)
        dgq_ref[...] += dgq
        dgk_ref[...] += dgk

    def col(off):
        return pl.BlockSpec((ts, LANES), lambda i, p: (i, off + p))

    gspec = pl.BlockSpec((1, LANES), lambda i, p: (0, 0))
    out = _sds((s, D_MODEL), BF16)
    dqt = pl.BlockSpec((1, LANES, ts), lambda i, p: (p, 0, i))
    return _pc(body, name, (s // ts, 8), [col(0), col(8), dqt, col(0), col(0), gspec, gspec],
               [col(0)] * 3 + [gspec] * 2, [out] * 3 + [_sds((1, LANES))] * 2)(proj, proj, dqs, dk, dv, gq, gk)


def _fox_fwd(x, norm_w, w_in, b_f, q_gain, k_gain, w_out, hosted=None):
    h = _rms_fwd(x, norm_w, "fox_norm")
    proj = _mm(h, w_in, "nn", "fox_in")
    gq = jnp.tile(q_gain, (1, 2))
    gk = jnp.tile(k_gain, (1, 2))
    bf = jnp.pad(b_f, ((0, 0), (0, LANES - ATTN_HEADS)))
    qs, kn, vb = _fox_prep_fwd(proj, gq, gk, "fox_prep")
    cum = _cumsum_rows(_fox_logf(proj, bf, "fox_logf"), False, "fox_cum")[:, :ATTN_HEADS]
    augq, augk = _decay_terms(cum)
    cmask = _causal_bias(x.shape[0])
    o, lse, *got = _flash_fwd(qs, kn, vb, augq, augk, cmask, "fox_attn", hosted=hosted)
    y = _mm(o, w_out, "nn", "fox_out", res=x)
    return y, (x, h, proj, gq, gk, bf, qs, kn, vb, augq, augk, cmask, o, lse), got


def _fox_bwd(dy, saved, norm_w, w_in, w_out, hosted=None):
    x, h, proj, gq, gk, bf, qs, kn, vb, augq, augk, cmask, o, lse = saved
    s = x.shape[0]
    do = _mm(dy, w_out, "nt", "fox_do")
    g_out = _mm(o, dy, "tn", "fox_gout", out_dtype=BF16)
    delta = _fox_delta(do, o, "fox_delta")[:, ::HEAD_DIM].T.reshape(8, 2, s)
    dk, dv, dck, dqs, dcq, *got = _flash_bwd(qs, kn, vb, augq, augk, cmask, do, lse, delta, "fox_dattn", hosted=hosted)
    dcum = jnp.pad(dcq.reshape(ATTN_HEADS, s).T + dck[:, ::HEAD_DIM], ((0, 0), (0, LANES - ATTN_HEADS)))
    dlf = _cumsum_rows(dcum, True, "fox_dcum")
    dfl, g_bf = _fox_dlogf(proj, bf, dlf, "fox_dlogf")
    dq_o, dk_o, dv_o, g_gq, g_gk = _fox_prep_bwd(proj, dqs, dk, dv, gq, gk, "fox_dprep")
    dproj = jnp.concatenate([dq_o, dk_o, dv_o, dfl], axis=1)
    g_in = _mm(h, dproj, "tn", "fox_gin", out_dtype=BF16)
    dx, g_norm = _mm_dnorm_nt(dproj, w_in, "fox_dh", x, norm_w, dy)
    g_q = g_gq[:, :HEAD_DIM] + g_gq[:, HEAD_DIM:]
    g_k = g_gk[:, :HEAD_DIM] + g_gk[:, HEAD_DIM:]
    return dx, g_norm, g_in[:, :FOX_IN], g_bf[:, :ATTN_HEADS], g_q, g_k, g_out, got


def _ssd_conv_fwd(proj, cw, cb, name):
    s = proj.shape[0]
    ts = _row_tile(s)
    w = 1024
    per = ts // 8

    def body(p_ref, h_ref, w_ref, b_ref, o_ref):
        first = pl.program_id(0) == 0
        cur = p_ref[...]
        halo = h_ref[...]
        u = w_ref[3:4, :] * cur + b_ref[...]
        for j in range(1, 4):
            u = u + w_ref[3 - j:4 - j, :] * _prev_rows(cur, halo, j, first)
        o_ref[...] = u * _sigmoid(u)

    return _pc(body, name, (s // ts, 4),
               [pl.BlockSpec((ts, w), lambda i, j: (i, 2 + j)),
                pl.BlockSpec((8, w), lambda i, j: (jnp.maximum(i * per - 1, 0), 2 + j)),
                pl.BlockSpec((4, w), lambda i, j: (0, j)), pl.BlockSpec((1, w), lambda i, j: (0, j))],
               pl.BlockSpec((ts, w), lambda i, j: (i, j)), _sds((s, SSM_CONV_DIM)))(proj, proj, cw, cb)


def _ssd_conv_bwd_act(proj, dxbc, cw, cb, name):
    s = proj.shape[0]
    ts = _row_tile(s)
    w = 1024
    per = ts // 8

    def body(p_ref, h_ref, d_ref, w_ref, b_ref, g_ref, db_ref):
        first = pl.program_id(1) == 0
        cur = p_ref[...]
        halo = h_ref[...]
        u = w_ref[3:4, :] * cur + b_ref[...]
        for j in range(1, 4):
            u = u + w_ref[3 - j:4 - j, :] * _prev_rows(cur, halo, j, first)
        sg = _sigmoid(u)
        g = d_ref[...] * (sg * (1.0 + u * (1.0 - sg)))
        g_ref[...] = g

        @pl.when(first)
        def _():
            db_ref[...] = jnp.zeros_like(db_ref)

        db_ref[...] += jnp.sum(g, axis=0, keepdims=True)

    vec = pl.BlockSpec((1, w), lambda j, i: (0, j))
    tile = pl.BlockSpec((ts, w), lambda j, i: (i, j))
    return _pc(body, name, (4, s // ts),
               [pl.BlockSpec((ts, w), lambda j, i: (i, 2 + j)),
                pl.BlockSpec((8, w), lambda j, i: (jnp.maximum(i * per - 1, 0), 2 + j)),
                tile, pl.BlockSpec((4, w), lambda j, i: (0, j)), vec],
               [tile, vec], [_sds((s, SSM_CONV_DIM)), _sds((1, SSM_CONV_DIM))])(proj, proj, dxbc, cw, cb)


def _ssd_conv_bwd_in(proj, g, cw, name):
    s = proj.shape[0]
    ts = _row_tile(s)
    nt = s // ts
    w = 1024
    per = ts // 8
    nblk = s // 8

    def body(p_ref, h_ref, g_ref, gn_ref, w_ref, o_ref, dw_ref):
        i = pl.program_id(1)
        first = i == 0
        last = i == nt - 1
        cur = p_ref[...]
        halo = h_ref[...]
        gv = g_ref[...]
        gn = gn_ref[...]

        @pl.when(first)
        def _():
            dw_ref[...] = jnp.zeros_like(dw_ref)

        dpre = w_ref[3:4, :] * gv
        dw_ref[3:4, :] += jnp.sum(gv * cur, axis=0, keepdims=True)
        for j in range(1, 4):
            dpre = dpre + w_ref[3 - j:4 - j, :] * _next_rows(gv, gn, j, last)
            dw_ref[3 - j:4 - j, :] += jnp.sum(gv * _prev_rows(cur, halo, j, first), axis=0, keepdims=True)
        o_ref[...] = dpre.astype(BF16)

    tile = pl.BlockSpec((ts, w), lambda j, i: (i, j))
    wspec = pl.BlockSpec((4, w), lambda j, i: (0, j))
    return _pc(body, name, (4, nt),
               [pl.BlockSpec((ts, w), lambda j, i: (i, 2 + j)),
                pl.BlockSpec((8, w), lambda j, i: (jnp.maximum(i * per - 1, 0), 2 + j)),
                tile, pl.BlockSpec((8, w), lambda j, i: (jnp.minimum((i + 1) * per, nblk - 1), j)), wspec],
               [tile, wspec], [_sds((s, SSM_CONV_DIM), BF16), _sds((4, SSM_CONV_DIM))])(proj, proj, g, g, cw)


def _ssd_dt_fwd(proj, bias, a_neg, name):
    s = proj.shape[0]
    n = s // SSM_CHUNK

    def body(r_ref, b_ref, a_ref, dt_ref, ac_ref):
        real = lax.broadcasted_iota(jnp.int32, (SSM_CHUNK, LANES), 1) < SSM_HEADS
        dt = jnp.where(real, _softplus(r_ref[...] + b_ref[...]), 0.0)
        dt_ref[...] = dt
        ac_ref[...] = _dot(_tri(True), dt * a_ref[...], 1, 0, HI)

    vec = pl.BlockSpec((1, LANES), lambda c: (0, 0))
    row = pl.BlockSpec((SSM_CHUNK, LANES), lambda c: (c, 0))
    return _pc(body, name, (n,), [pl.BlockSpec((SSM_CHUNK, LANES), lambda c: (c, 48)), vec, vec], [row, row],
               [_sds((s, LANES)), _sds((s, LANES))])(proj, bias, a_neg)


def _ssd_dt_bwd(proj, bias, ddt, name):
    s = proj.shape[0]
    ts = _row_tile(s, 512)

    def body(r_ref, b_ref, d_ref, o_ref, db_ref):
        real = lax.broadcasted_iota(jnp.int32, (ts, LANES), 1) < SSM_HEADS
        g = jnp.where(real, d_ref[...] * _sigmoid(r_ref[...] + b_ref[...]), 0.0)
        o_ref[...] = g.astype(BF16)

        @pl.when(pl.program_id(0) == 0)
        def _():
            db_ref[...] = jnp.zeros_like(db_ref)

        db_ref[...] += jnp.sum(g, axis=0, keepdims=True)

    vec = pl.BlockSpec((1, LANES), lambda i: (0, 0))
    row = pl.BlockSpec((ts, LANES), lambda i: (i, 0))
    return _pc(body, name, (s // ts,), [pl.BlockSpec((ts, LANES), lambda i: (i, 48)), vec, row], [row, vec],
               [_sds((s, LANES), BF16), _sds((1, LANES))])(proj, bias, ddt)


def _pair_cols(cols, k0, lo):
    return jnp.where(lo, cols[:, k0:k0 + 1], cols[:, k0 + 1:k0 + 2])


def _last_lane(row):
    lane = lax.broadcasted_iota(jnp.int32, row.shape, 1)
    return jnp.sum(jnp.where(lane == SSM_CHUNK - 1, row, 0.0), axis=-1, keepdims=True)


SSD_GROUPS_PER_STEP = 2


def _ssd_specs(nc, rev):
    cc = (lambda c: nc - 1 - c) if rev else (lambda c: c)
    n = SSD_GROUPS_PER_STEP
    nb = SSM_INNER // (LANES * n)
    return dict(
        x=pl.BlockSpec((SSM_CHUNK, 256 * n), lambda g, c: (cc(c), g)),
        b=pl.BlockSpec((SSM_CHUNK, LANES * n), lambda g, c: (cc(c), nb + g)),
        c=pl.BlockSpec((SSM_CHUNK, LANES * n), lambda g, c: (cc(c), nb + SSM_GROUPS // n + g)),
        col=pl.BlockSpec((n, SSM_CHUNK, 4), lambda g, c: (g, cc(c), 0)),
        row=pl.BlockSpec((n, 4, SSM_CHUNK), lambda g, c: (g, 0, cc(c))),
        grp=pl.BlockSpec((n, 1, 256), lambda g, c: (g, 0, 0)),
        grow=pl.BlockSpec((n, 4, LANES), lambda g, c: (g, 0, 0)),
        hs=pl.BlockSpec((1, n, 256, SSM_STATE), lambda g, c: (cc(c), g, 0, 0)),
        bc=pl.BlockSpec((SSM_CHUNK, LANES * n), lambda g, c: (cc(c), g)),
    )


def _ssd_scan_fwd(xbc, dtc, acol, drow, arow, dskip, name):
    s = xbc.shape[0]
    nc = s // SSM_CHUNK
    sp = _ssd_specs(nc, False)
    L = SSM_CHUNK

    def body(x_ref, b_ref, c_ref, dtc_ref, ac_ref, dr_ref, ar_ref, dk_ref, y_ref, hs_ref, h_ref):
        @pl.when(pl.program_id(1) == 0)
        def _():
            h_ref[...] = jnp.zeros_like(h_ref)

        for gi in range(SSD_GROUPS_PER_STEP):
            group(gi, x_ref, b_ref, c_ref, dtc_ref, ac_ref, dr_ref, ar_ref, dk_ref, y_ref, hs_ref, h_ref)

    def group(gi, x_ref, b_ref, c_ref, dtc_ref, ac_ref, dr_ref, ar_ref, dk_ref, y_ref, hs_ref, h_ref):
        x0 = gi * 256
        bb = b_ref[:, gi * LANES:(gi + 1) * LANES].astype(BF16)
        cb = c_ref[:, gi * LANES:(gi + 1) * LANES].astype(BF16)
        gm = _dot(cb, bb, 1, 1)
        dtc = dtc_ref[gi]
        ac = ac_ref[gi]
        dr = dr_ref[gi]
        ar = ar_ref[gi]
        dsk = dk_ref[gi]
        hs_ref[0, gi] = h_ref[gi]
        tril = lax.broadcasted_iota(jnp.int32, (L, L), 0) >= lax.broadcasted_iota(jnp.int32, (L, L), 1)
        lo = _lo_mask((L, LANES))
        rowlo = lax.broadcasted_iota(jnp.int32, (L, LANES), 0) < HEAD_DIM
        for pr in range(2):
            k0 = 2 * pr
            xp = x_ref[:, x0 + pr * LANES:x0 + (pr + 1) * LANES]
            xpb = xp.astype(BF16)
            hp = h_ref[gi, pr * LANES:(pr + 1) * LANES, :]
            yd, al = [], []
            for k in (k0, k0 + 1):
                seg = ac[:, k:k + 1] - ar[k:k + 1, :]
                wk = gm * jnp.exp(jnp.where(tril, seg, -jnp.inf)) * dr[k:k + 1, :]
                yd.append(_dot(wk.astype(BF16), xpb, 1, 0))
                al.append(_last_lane(ar[k:k + 1, :]))
            e = jnp.exp(_pair_cols(ac, k0, lo))
            yo = _dot(cb, hp.astype(BF16), 1, 1) * e
            y_ref[:, x0 + pr * LANES:x0 + (pr + 1) * LANES] = (
                jnp.where(lo, yd[0], yd[1]) + yo + dsk[:, pr * LANES:(pr + 1) * LANES] * xp)
            wp = jnp.where(lo, jnp.exp(al[0] - ac[:, k0:k0 + 1]) * dtc[:, k0:k0 + 1],
                           jnp.exp(al[1] - ac[:, k0 + 1:k0 + 2]) * dtc[:, k0 + 1:k0 + 2])
            st = _dot((xp * wp).astype(BF16), bb, 0, 0)
            dec = jnp.where(rowlo, jnp.exp(al[0]), jnp.exp(al[1]))
            h_ref[gi, pr * LANES:(pr + 1) * LANES, :] = dec * hp + st

    return _pc(body, name, (SSM_GROUPS // SSD_GROUPS_PER_STEP, nc),
               [sp["x"], sp["b"], sp["c"], sp["col"], sp["col"], sp["row"], sp["row"], sp["grp"]],
               [sp["x"], sp["hs"]], [_sds((s, SSM_INNER)), _sds((nc, SSM_GROUPS, 256, SSM_STATE))],
               [pltpu.VMEM((SSD_GROUPS_PER_STEP, 256, SSM_STATE), F32)])(xbc, xbc, xbc, dtc, acol, drow, arow, dskip)


def _ssd_scan_bwd(xbc, dtc, acol, drow, arow, dskip, agrp, hs, dy, name):
    s = xbc.shape[0]
    nc = s // SSM_CHUNK
    sp = _ssd_specs(nc, True)
    L = SSM_CHUNK

    def body(x_ref, b_ref, c_ref, dtc_ref, ac_ref, dr_ref, ar_ref, dk_ref, ag_ref, hs_ref, dy_ref,
             dx_ref, db_ref, dc_ref, ddt_ref, da_ref, dd_ref, dh_ref):
        @pl.when(pl.program_id(1) == 0)
        def _():
            dh_ref[...] = jnp.zeros_like(dh_ref)
            da_ref[...] = jnp.zeros_like(da_ref)
            dd_ref[...] = jnp.zeros_like(dd_ref)

        for gi in range(SSD_GROUPS_PER_STEP):
            group(gi, x_ref, b_ref, c_ref, dtc_ref, ac_ref, dr_ref, ar_ref, dk_ref, ag_ref, hs_ref, dy_ref,
                  dx_ref, db_ref, dc_ref, ddt_ref, da_ref, dd_ref, dh_ref)

    def group(gi, x_ref, b_ref, c_ref, dtc_ref, ac_ref, dr_ref, ar_ref, dk_ref, ag_ref, hs_ref, dy_ref,
              dx_ref, db_ref, dc_ref, ddt_ref, da_ref, dd_ref, dh_ref):
        x0 = gi * 256
        bcols = slice(gi * LANES, (gi + 1) * LANES)
        bb = b_ref[:, bcols].astype(BF16)
        cb = c_ref[:, bcols].astype(BF16)
        gm = _dot(cb, bb, 1, 1)
        dtc = dtc_ref[gi]
        ac = ac_ref[gi]
        dr = dr_ref[gi]
        ar = ar_ref[gi]
        dsk = dk_ref[gi]
        ag = ag_ref[gi]
        tril = lax.broadcasted_iota(jnp.int32, (L, L), 0) >= lax.broadcasted_iota(jnp.int32, (L, L), 1)
        lo = _lo_mask((L, LANES))
        nlo = jnp.logical_not(lo)
        rowlo = lax.broadcasted_iota(jnp.int32, (L, LANES), 0) < HEAD_DIM
        lane = lax.broadcasted_iota(jnp.int32, (L, LANES), 1)
        lane_row = lax.broadcasted_iota(jnp.int32, (1, LANES), 1)
        dgm = jnp.zeros((L, L), F32)
        dcm = jnp.zeros((L, SSM_STATE), F32)
        dbm = jnp.zeros((L, SSM_STATE), F32)
        cols = jnp.zeros((L, LANES), F32)
        rows_ddt, rows_q, al_all, dcd_all = [], [], [], []
        for pr in range(2):
            k0 = 2 * pr
            xcols = slice(x0 + pr * LANES, x0 + (pr + 1) * LANES)
            xp = x_ref[:, xcols]
            xpb = xp.astype(BF16)
            dyp = dy_ref[:, xcols]
            dypb = dyp.astype(BF16)
            zero = jnp.zeros_like(dypb)
            hp = hs_ref[0, gi, pr * LANES:(pr + 1) * LANES, :]
            hpb = hp.astype(BF16)
            dst = dh_ref[gi, pr * LANES:(pr + 1) * LANES, :]
            dstb = dst.astype(BF16)
            dxd, al = [], []
            for k in (k0, k0 + 1):
                sel = lo if k == k0 else nlo
                seg = ac[:, k:k + 1] - ar[k:k + 1, :]
                lam = jnp.exp(jnp.where(tril, seg, -jnp.inf))
                wk = gm * lam * dr[k:k + 1, :]
                dwk = _dot(jnp.where(sel, dypb, zero), xpb, 1, 1)
                mk = dwk * gm * lam
                qk = mk * dr[k:k + 1, :]
                dgm = dgm + dwk * lam * dr[k:k + 1, :]
                rows_ddt.append(jnp.sum(mk, axis=0, keepdims=True))
                rows_q.append(jnp.sum(qk, axis=0, keepdims=True))
                cols = jnp.where(lane == k, jnp.sum(qk, axis=-1, keepdims=True), cols)
                dxd.append(_dot(wk.astype(BF16), dypb, 0, 0))
                al.append(_last_lane(ar[k:k + 1, :]))
            al_all += al
            dxp = jnp.where(lo, dxd[0], dxd[1])
            e = jnp.exp(_pair_cols(ac, k0, lo))
            dye = dyp * e
            dyeb = dye.astype(BF16)
            dcm = dcm + _dot(dyeb, hpb, 1, 0)
            dh_yoff = _dot(dyeb, cb, 0, 0)
            tq = dye * _dot(cb, hpb, 1, 1)
            cols = jnp.where(lane == 4 + k0, jnp.sum(jnp.where(lo, tq, 0.0), axis=-1, keepdims=True), cols)
            cols = jnp.where(lane == 5 + k0, jnp.sum(jnp.where(lo, 0.0, tq), axis=-1, keepdims=True), cols)
            wp = jnp.where(lo, jnp.exp(al[0] - ac[:, k0:k0 + 1]) * dtc[:, k0:k0 + 1],
                           jnp.exp(al[1] - ac[:, k0 + 1:k0 + 2]) * dtc[:, k0 + 1:k0 + 2])
            dxw = _dot(bb, dstb, 1, 1)
            dxp = dxp + dxw * wp
            tw = xp * dxw
            cols = jnp.where(lane == 8 + k0, jnp.sum(jnp.where(lo, tw, 0.0), axis=-1, keepdims=True), cols)
            cols = jnp.where(lane == 9 + k0, jnp.sum(jnp.where(lo, 0.0, tw), axis=-1, keepdims=True), cols)
            dbm = dbm + _dot((xp * wp).astype(BF16), dstb, 1, 0)
            dsl = dsk[:, pr * LANES:(pr + 1) * LANES]
            dx_ref[:, xcols] = dxp + dsl * dyp
            dd_ref[gi, :, pr * LANES:(pr + 1) * LANES] += jnp.sum(dyp * xp, axis=0, keepdims=True)
            prod = dst * hp
            dcd_all.append(jnp.sum(jnp.sum(jnp.where(rowlo, prod, 0.0), axis=-1, keepdims=True), axis=0, keepdims=True))
            dcd_all.append(jnp.sum(jnp.sum(jnp.where(rowlo, 0.0, prod), axis=-1, keepdims=True), axis=0, keepdims=True))
            dec = jnp.where(rowlo, jnp.exp(al[0]), jnp.exp(al[1]))
            dh_ref[gi, pr * LANES:(pr + 1) * LANES, :] = dec * dst + dh_yoff
        dgb = dgm.astype(BF16)
        dc_ref[:, bcols] = dcm + _dot(dgb, bb, 1, 0)
        db_ref[:, bcols] = dbm + _dot(dgb, cb, 0, 0)
        colt = cols.T
        sub8 = lax.broadcasted_iota(jnp.int32, (8, LANES), 0)
        da_rows = jnp.zeros((8, LANES), F32)
        ddt_part = []
        for k in range(4):
            rs = colt[k:k + 1, :]
            uo = colt[4 + k:5 + k, :]
            dwl = colt[8 + k:9 + k, :]
            es = jnp.exp(al_all[k] - ar[k:k + 1, :])
            wrow = es * dr[k:k + 1, :]
            dwl_w = dwl * wrow
            da_k = rs - rows_q[k] + uo - dwl_w
            tail = jnp.sum(dwl_w, axis=-1, keepdims=True) + jnp.exp(al_all[k]) * dcd_all[k]
            da_k = da_k + jnp.where(lane_row == L - 1, tail, 0.0)
            da_rows = jnp.where(sub8 == k, da_k, da_rows)
            ddt_part.append(rows_ddt[k] + dwl * es)
        dda = _dot(da_rows, _tri(True), 1, 0, HI)
        for k in range(4):
            dda_k = dda[k:k + 1, :]
            ddt_ref[gi, k:k + 1, :] = ddt_part[k] + dda_k * ag[k:k + 1, :]
            da_ref[gi, k:k + 1, :] += dda_k * dr[k:k + 1, :] * ag[k:k + 1, :]

    return _pc(body, name, (SSM_GROUPS // SSD_GROUPS_PER_STEP, nc),
               [sp["x"], sp["b"], sp["c"], sp["col"], sp["col"], sp["row"], sp["row"], sp["grp"], sp["grow"], sp["hs"], sp["x"]],
               [sp["x"], sp["bc"], sp["bc"], sp["row"], sp["grow"], sp["grp"]],
               [_sds((s, SSM_INNER)), _sds((s, 1024)), _sds((s, 1024)), _sds((SSM_GROUPS, 4, s)),
                _sds((SSM_GROUPS, 4, LANES)), _sds((SSM_GROUPS, 1, 256))],
               [pltpu.VMEM((SSD_GROUPS_PER_STEP, 256, SSM_STATE), F32)])(xbc, xbc, xbc, dtc, acol, drow, arow, dskip, agrp, hs, dy)


def _gnorm_fwd(y, proj, nw, name):
    s = y.shape[0]
    ts = _row_tile(s)
    gw = SSM_INNER // SSM_GROUPS

    def body(y_ref, z_ref, w_ref, o_ref):
        for g in range(SSM_GROUPS):
            sl = slice(g * gw, (g + 1) * gw)
            z = z_ref[:, sl]
            y2 = y_ref[:, sl] * (z * _sigmoid(z))
            r = lax.rsqrt(jnp.mean(y2 * y2, axis=-1, keepdims=True) + RMS_EPS)
            o_ref[:, sl] = ((y2 * r) * w_ref[:, sl]).astype(BF16)

    row = pl.BlockSpec((ts, SSM_INNER), lambda i: (i, 0))
    return _pc(body, name, (s // ts,), [row, row, pl.BlockSpec((1, SSM_INNER), lambda i: (0, 0))], row,
               _sds((s, SSM_INNER), BF16))(y, proj, nw)


def _gnorm_bwd(y, proj, nw, dyn, name):
    s = y.shape[0]
    ts = _row_tile(s)
    gw = SSM_INNER // SSM_GROUPS

    def body(y_ref, z_ref, w_ref, d_ref, dy_ref, dz_ref, dw_ref):
        @pl.when(pl.program_id(0) == 0)
        def _():
            dw_ref[...] = jnp.zeros_like(dw_ref)

        for g in range(SSM_GROUPS):
            sl = slice(g * gw, (g + 1) * gw)
            z = z_ref[:, sl]
            yv = y_ref[:, sl]
            sg = _sigmoid(z)
            sz = z * sg
            y2 = yv * sz
            r = lax.rsqrt(jnp.mean(y2 * y2, axis=-1, keepdims=True) + RMS_EPS)
            yn = y2 * r
            dout = d_ref[:, sl]
            dyg = dout * w_ref[:, sl]
            dy2 = r * (dyg - yn * jnp.mean(dyg * yn, axis=-1, keepdims=True))
            dy_ref[:, sl] = dy2 * sz
            dz_ref[:, sl] = (dy2 * yv * (sg * (1.0 + z * (1.0 - sg)))).astype(BF16)
            dw_ref[:, sl] += jnp.sum(dout * yn, axis=0, keepdims=True)

    row = pl.BlockSpec((ts, SSM_INNER), lambda i: (i, 0))
    vec = pl.BlockSpec((1, SSM_INNER), lambda i: (0, 0))
    return _pc(body, name, (s // ts,), [row, row, vec, row], [row, row, vec],
               [_sds((s, SSM_INNER)), _sds((s, SSM_INNER), BF16), _sds((1, SSM_INNER))])(y, proj, nw, dyn)


def _head_layouts(v, s):
    return v.reshape(s, SSM_GROUPS, 4).transpose(1, 0, 2), v.T.reshape(SSM_GROUPS, 4, s)


def _ssd_fwd(x, norm_w, w_in, conv_w, conv_b, dt_bias, a_log, d_skip, gnorm_w, w_out):
    s = x.shape[0]
    h = _rms_fwd(x, norm_w, "ssd_norm")
    proj = _mm(h, w_in, "nn", "ssd_in")
    xbc = _ssd_conv_fwd(proj, conv_w, conv_b, "ssd_conv")
    pad = ((0, 0), (0, LANES - SSM_HEADS))
    a_neg = -jnp.exp(a_log)
    bias = jnp.pad(dt_bias, pad)
    dt, acum = _ssd_dt_fwd(proj, bias, jnp.pad(a_neg, pad), "ssd_dt")
    dtc, drow = _head_layouts(dt[:, :SSM_HEADS], s)
    acol, arow = _head_layouts(acum[:, :SSM_HEADS], s)
    dskip = jnp.repeat(d_skip.reshape(SSM_GROUPS, 1, 4), HEAD_DIM, axis=2)
    y, hs = _ssd_scan_fwd(xbc, dtc, acol, drow, arow, dskip, "ssd_scan")
    yn = _gnorm_fwd(y, proj, gnorm_w, "ssd_gnorm")
    out = _mm(yn, w_out, "nn", "ssd_out", res=x)
    return out, (x, h, proj, xbc, bias, a_neg, dtc, acol, drow, arow, dskip, y, hs, yn)


def _ssd_bwd(dout, saved, norm_w, w_in, conv_w, conv_b, gnorm_w, w_out):
    x, h, proj, xbc, bias, a_neg, dtc, acol, drow, arow, dskip, y, hs, yn = saved
    s = x.shape[0]
    dyn = _mm(dout, w_out, "nt", "ssd_dyn")
    g_out = _mm(yn, dout, "tn", "ssd_gout", out_dtype=BF16)
    dy, dz, g_gnorm = _gnorm_bwd(y, proj, gnorm_w, dyn, "ssd_dgnorm")
    agrp = jnp.broadcast_to(a_neg.reshape(SSM_GROUPS, 4, 1), (SSM_GROUPS, 4, LANES))
    dxs, db, dc, ddt_row, da_acc, dd_acc = _ssd_scan_bwd(xbc, dtc, acol, drow, arow, dskip, agrp, hs, dy, "ssd_dscan")
    dxbc = jnp.concatenate([dxs, db, dc], axis=1)
    gact, g_cb = _ssd_conv_bwd_act(proj, dxbc, conv_w, conv_b, "ssd_dconv_act")
    dpre, g_cw = _ssd_conv_bwd_in(proj, gact, conv_w, "ssd_dconv_in")
    ddt = jnp.pad(ddt_row.reshape(SSM_HEADS, s).T, ((0, 0), (0, LANES - SSM_HEADS)))
    ddtraw, g_dtb = _ssd_dt_bwd(proj, bias, ddt, "ssd_ddt")
    dproj = jnp.concatenate([dz, dpre, ddtraw], axis=1)
    g_in = _mm(h, dproj, "tn", "ssd_gin", out_dtype=BF16)
    dx, g_norm = _mm_dnorm_nt(dproj, w_in, "ssd_dh", x, norm_w, dout)
    g_alog = jnp.sum(da_acc, axis=-1).reshape(1, SSM_HEADS)
    g_d = jnp.sum(dd_acc.reshape(SSM_GROUPS, 4, HEAD_DIM), axis=-1).reshape(1, SSM_HEADS)
    return dx, g_norm, g_in[:, :SSM_IN], g_cw, g_cb, g_dtb[:, :SSM_HEADS], g_alog, g_d, g_gnorm, g_out


def _loss_head(y, target, name):
    s, d = y.shape
    ts = _row_tile(s)

    def body(y_ref, t_ref, dy_ref, l_ref):
        @pl.when(pl.program_id(0) == 0)
        def _():
            l_ref[...] = jnp.zeros_like(l_ref)

        e = y_ref[...] - t_ref[...]
        dy_ref[...] = e * (1.0 / d)
        part = jnp.sum(jnp.sum(e * e, axis=-1, keepdims=True), axis=0, keepdims=True) * (0.5 / d)
        l_ref[...] += jnp.broadcast_to(part, l_ref.shape)

    row = pl.BlockSpec((ts, d), lambda i: (i, 0))
    dy, lacc = _pc(body, name, (s // ts,), [row, row], [row, pl.BlockSpec((8, LANES), lambda i: (0, 0))],
                   [_sds((s, d)), _sds((8, LANES))])(y, target)
    return lacc[0, 0], dy


def _local_step(x, target, w, gather_rest=None, scatter_first=None):
    saved = []
    received = None
    for i in range(DEPTH):
        kind, j = i % 3, i // 3
        mn = w["mix_norm"][i:i + 1]
        if kind == 0:
            x, sv = _conv_fwd(x, mn, w["conv_w_in"][j], w["conv_w_dw"][j], w["conv_w_out"][j], str(i))
        elif kind == 1:
            hosted = None if gather_rest is None else (gather_rest[0], True)
            x, sv, got = _fox_fwd(x, mn, w["fox_w_in"], w["fox_b_f"], w["fox_q_gain"], w["fox_k_gain"], w["fox_w_out"], hosted)
            if gather_rest is not None:
                w = gather_rest[1](w, got)
        else:
            x, sv = _ssd_fwd(x, mn, w["ssd_w_in"], w["ssd_conv_w"], w["ssd_conv_b"], w["ssd_dt_bias"],
                             w["ssd_a_log"], w["ssd_d"], w["ssd_norm_w"], w["ssd_w_out"])
        x, sf = _ffn_fwd(x, w["ffn_norm"][i:i + 1], w["ffn_w_gu"][i], w["ffn_w_down"][i], str(i))
        saved.append((sv, sf))
    loss, dx = _loss_head(x, target, "loss_head")
    g = {k: [None] * n for k, n in (("mix_norm", DEPTH), ("ffn_norm", DEPTH), ("ffn_w_gu", DEPTH), ("ffn_w_down", DEPTH),
                                    ("conv_w_in", 2), ("conv_w_dw", 2), ("conv_w_out", 2))}
    for i in reversed(range(DEPTH)):
        kind, j = i % 3, i // 3
        sv, sf = saved[i]
        dx, g["ffn_norm"][i], g["ffn_w_gu"][i], g["ffn_w_down"][i] = _ffn_bwd(
            dx, sf, w["ffn_norm"][i:i + 1], w["ffn_w_gu"][i], w["ffn_w_down"][i], str(i))
        mn = w["mix_norm"][i:i + 1]
        if kind == 0:
            dx, g["mix_norm"][i], g["conv_w_in"][j], g["conv_w_dw"][j], g["conv_w_out"][j] = _conv_bwd(
                dx, sv, mn, w["conv_w_in"][j], w["conv_w_dw"][j], w["conv_w_out"][j], str(i))
        elif kind == 1:
            hosted = None if scatter_first is None else (scatter_first(g), False)
            (dx, g["mix_norm"][i], g["fox_w_in"], g["fox_b_f"], g["fox_q_gain"], g["fox_k_gain"],
             g["fox_w_out"], received) = _fox_bwd(dx, sv, mn, w["fox_w_in"], w["fox_w_out"], hosted)
        else:
            (dx, g["mix_norm"][i], g["ssd_w_in"], g["ssd_conv_w"], g["ssd_conv_b"], g["ssd_dt_bias"], g["ssd_a_log"],
             g["ssd_d"], g["ssd_norm_w"], g["ssd_w_out"]) = _ssd_bwd(
                 dx, sv, mn, w["ssd_w_in"], w["ssd_conv_w"], w["ssd_conv_b"], w["ssd_norm_w"], w["ssd_w_out"])
    g["mix_norm"] = jnp.concatenate(g["mix_norm"], axis=0)
    g["ffn_norm"] = jnp.concatenate(g["ffn_norm"], axis=0)
    g["conv_w_dw"] = jnp.stack(g["conv_w_dw"], axis=0)
    g["ssd_conv_w"] = g["ssd_conv_w"][None]
    return loss, dx, g, received


def _mesh_position():
    return lax.axis_index("x") * 4 + lax.axis_index("y") * 2 + lax.axis_index("c")


def _device_of(t):
    return (lax.shift_right_logical(t, 2), lax.bitwise_and(lax.shift_right_logical(t, 1), 1), lax.bitwise_and(t, 1))


def _exchange_copies(srcs_of, out_refs, send_sems, recv_sems, local_sems):
    me = _mesh_position()
    na = len(out_refs)
    locals_ = [pltpu.make_async_copy(srcs_of(a, me), out_refs[a].at[me], local_sems.at[a]) for a in range(na)]
    sends, arrivals = [], []
    for j in range(1, NDEV):
        t = lax.rem(me + j, NDEV)
        frm = lax.rem(me + NDEV - j, NDEV)
        for a in range(na):
            sends.append(pltpu.make_async_remote_copy(
                src_ref=srcs_of(a, t), dst_ref=out_refs[a].at[me], send_sem=send_sems.at[a, j - 1],
                recv_sem=recv_sems.at[a, j - 1], device_id=_device_of(t), device_id_type=pl.DeviceIdType.MESH))
            arrivals.append(pltpu.make_async_remote_copy(
                src_ref=srcs_of(a, me), dst_ref=out_refs[a].at[frm], send_sem=send_sems.at[a, j - 1],
                recv_sem=recv_sems.at[a, j - 1], device_id=_device_of(frm), device_id_type=pl.DeviceIdType.MESH))
    return locals_, sends, arrivals


def _exchange_start(copies):
    locals_, sends, _ = copies
    for cp in locals_ + sends:
        cp.start()


def _exchange_wait(copies):
    locals_, sends, arrivals = copies
    for cp in sends:
        cp.wait_send()
    for cp in arrivals:
        cp.wait_recv()
    for cp in locals_:
        cp.wait()


def _exchange_run(srcs_of, out_refs, send_sems, recv_sems, local_sems):
    copies = _exchange_copies(srcs_of, out_refs, send_sems, recv_sems, local_sems)
    _exchange_start(copies)
    _exchange_wait(copies)


def _exchange_parts(arrays, gather):
    na = len(arrays)
    outs = [_sds(((NDEV,) + a.shape) if gather else a.shape, a.dtype) for a in arrays]
    sems = [pltpu.SemaphoreType.DMA((na, NDEV - 1)), pltpu.SemaphoreType.DMA((na, NDEV - 1)), pltpu.SemaphoreType.DMA((na,))]
    pick = (lambda srcs: (lambda a, t: srcs[a])) if gather else (lambda srcs: (lambda a, t: srcs[a].at[t]))
    return pick, outs, sems


def _exchange(arrays, name, gather):
    na = len(arrays)
    pick, outs, sems = _exchange_parts(arrays, gather)

    def body(*refs):
        _exchange_run(pick(refs[:na]), refs[na:2 * na], *refs[2 * na:])

    hbm = pl.BlockSpec(memory_space=pl.ANY)
    return pl.pallas_call(body, name=name, in_specs=[hbm] * na, out_specs=[hbm] * na, out_shape=outs, scratch_shapes=sems)(*arrays)


def _all_sum_small(pack, name):
    def body(src_ref, out_ref, buf_ref, send_sems, recv_sems, local_sems):
        _exchange_run(lambda a, t: src_ref, [buf_ref], send_sems, recv_sems, local_sems)
        acc = buf_ref[0]
        for d in range(1, NDEV):
            acc = acc + buf_ref[d]
        out_ref[...] = acc

    vmem = pl.BlockSpec(memory_space=pltpu.VMEM)
    return pl.pallas_call(
        body, name=name, in_specs=[vmem], out_specs=vmem, out_shape=_sds(pack.shape, pack.dtype),
        scratch_shapes=[pltpu.VMEM((NDEV,) + pack.shape, pack.dtype), pltpu.SemaphoreType.DMA((1, NDEV - 1)),
                        pltpu.SemaphoreType.DMA((1, NDEV - 1)), pltpu.SemaphoreType.DMA((1,))])(pack)


def _sum_slabs(slabs, name):
    _, r, c = slabs.shape
    tr = r
    for cand in (256, 352):
        if r % cand == 0:
            tr = cand
            break

    def body(s_ref, o_ref):
        acc = s_ref[0].astype(F32)
        for d in range(1, NDEV):
            acc = acc + s_ref[d].astype(F32)
        o_ref[...] = acc

    return _pc(body, name, (r // tr,), [pl.BlockSpec((NDEV, tr, c), lambda i: (0, i, 0))],
               pl.BlockSpec((tr, c), lambda i: (i, 0)), _sds((r, c)))(slabs)


def _adamw(wt, g, m, v, name):
    shape = wt.shape
    w2, g2, m2, v2 = (a.reshape(-1, shape[-1]) for a in (wt, g, m, v))
    r, c = w2.shape
    tr = r
    for cand in (512, 352, 256):
        if r % cand == 0:
            tr = cand
            break
    c1 = 1.0 - ADAM_B1 ** ADAM_STEP
    c2 = 1.0 - ADAM_B2 ** ADAM_STEP

    def body(w_ref, g_ref, m_ref, v_ref, d_ref, mo_ref, vo_ref):
        gv = g_ref[...]
        mn = ADAM_B1 * m_ref[...] + (1.0 - ADAM_B1) * gv
        vn = ADAM_B2 * v_ref[...] + (1.0 - ADAM_B2) * (gv * gv)
        mo_ref[...] = mn
        vo_ref[...] = vn
        d_ref[...] = -ADAM_LR * ((mn / c1) / (jnp.sqrt(vn / c2) + ADAM_EPS) + ADAM_WD * w_ref[...])

    spec = pl.BlockSpec((tr, c), lambda i: (i, 0))
    outs = _pc(body, name, (r // tr,), [spec] * 4, [spec] * 3, [_sds((r, c))] * 3)(w2, g2, m2, v2)
    return tuple(o.reshape(shape) for o in outs)


_NAMES = ["mix_norm", "ffn_norm", "ffn_w_gu", "ffn_w_down", "conv_w_in", "conv_w_dw", "conv_w_out", "fox_w_in", "fox_b_f",
          "fox_q_gain", "fox_k_gain", "fox_w_out", "ssd_w_in", "ssd_conv_w", "ssd_conv_b", "ssd_dt_bias", "ssd_a_log",
          "ssd_d", "ssd_norm_w", "ssd_w_out"]
_MATRICES = ["ffn_w_gu", "ffn_w_down", "conv_w_in", "conv_w_out", "fox_w_in", "fox_w_out", "ssd_w_in", "ssd_w_out"]
_VECTORS = {"conv_w_dw": 2, "ssd_conv_w": 2, "ssd_conv_b": 1, "ssd_norm_w": 1}
_REPLICATED = ["mix_norm", "ffn_norm", "fox_b_f", "fox_q_gain", "fox_k_gain", "ssd_dt_bias", "ssd_a_log", "ssd_d"]


def _to_rows(flat):
    n = flat.shape[0]
    rows = -(-n // (8 * D_MODEL)) * 8
    return jnp.pad(flat, (0, rows * D_MODEL - n)).reshape(rows, D_MODEL)


def _full_shape(local_shape, axis):
    shp = list(local_shape)
    shp[axis] *= NDEV
    return tuple(shp)


def _cols_from_blocks(g):
    return jnp.moveaxis(g, 0, 1).reshape(g.shape[1], NDEV * g.shape[2])


def _blocks_from_cols(full):
    k, n8 = full.shape
    return jnp.moveaxis(full.reshape(k, NDEV, n8 // NDEV), 1, 0)


def kernel(x, mix_norm, ffn_norm, ffn_w_gu, ffn_w_down, conv_w_in, conv_w_dw, conv_w_out, fox_w_in, fox_b_f, fox_q_gain, fox_k_gain, fox_w_out, ssd_w_in, ssd_conv_w, ssd_conv_b, ssd_dt_bias, ssd_a_log, ssd_d, ssd_norm_w, ssd_w_out, loss_target, m_mix_norm, m_ffn_norm, m_ffn_w_gu, m_ffn_w_down, m_conv_w_in, m_conv_w_dw, m_conv_w_out, m_fox_w_in, m_fox_b_f, m_fox_q_gain, m_fox_k_gain, m_fox_w_out, m_ssd_w_in, m_ssd_conv_w, m_ssd_conv_b, m_ssd_dt_bias, m_ssd_a_log, m_ssd_d, m_ssd_norm_w, m_ssd_w_out, v_mix_norm, v_ffn_norm, v_ffn_w_gu, v_ffn_w_down, v_conv_w_in, v_conv_w_dw, v_conv_w_out, v_fox_w_in, v_fox_b_f, v_fox_q_gain, v_fox_k_gain, v_fox_w_out, v_ssd_w_in, v_ssd_conv_w, v_ssd_conv_b, v_ssd_dt_bias, v_ssd_a_log, v_ssd_d, v_ssd_norm_w, v_ssd_w_out):
    local = dict(mix_norm=mix_norm, ffn_norm=ffn_norm, ffn_w_gu=ffn_w_gu, ffn_w_down=ffn_w_down, conv_w_in=conv_w_in,
                 conv_w_dw=conv_w_dw, conv_w_out=conv_w_out, fox_w_in=fox_w_in, fox_b_f=fox_b_f, fox_q_gain=fox_q_gain,
                 fox_k_gain=fox_k_gain, fox_w_out=fox_w_out, ssd_w_in=ssd_w_in, ssd_conv_w=ssd_conv_w, ssd_conv_b=ssd_conv_b,
                 ssd_dt_bias=ssd_dt_bias, ssd_a_log=ssd_a_log, ssd_d=ssd_d, ssd_norm_w=ssd_norm_w, ssd_w_out=ssd_w_out)
    mom = dict(zip(_NAMES, [m_mix_norm, m_ffn_norm, m_ffn_w_gu, m_ffn_w_down, m_conv_w_in, m_conv_w_dw, m_conv_w_out, m_fox_w_in,
                            m_fox_b_f, m_fox_q_gain, m_fox_k_gain, m_fox_w_out, m_ssd_w_in, m_ssd_conv_w, m_ssd_conv_b,
                            m_ssd_dt_bias, m_ssd_a_log, m_ssd_d, m_ssd_norm_w, m_ssd_w_out]))
    var = dict(zip(_NAMES, [v_mix_norm, v_ffn_norm, v_ffn_w_gu, v_ffn_w_down, v_conv_w_in, v_conv_w_dw, v_conv_w_out, v_fox_w_in,
                            v_fox_b_f, v_fox_q_gain, v_fox_k_gain, v_fox_w_out, v_ssd_w_in, v_ssd_conv_w, v_ssd_conv_b,
                            v_ssd_dt_bias, v_ssd_a_log, v_ssd_d, v_ssd_norm_w, v_ssd_w_out]))

    shard = {k: local[k].astype(BF16) for k in _MATRICES}
    vec_pack = _to_rows(jnp.concatenate([local[k].reshape(-1) for k in _VECTORS]))
    first = _exchange([shard["ffn_w_gu"][0:1], shard["ffn_w_down"][0:1], shard["conv_w_in"][0:1], shard["conv_w_out"][0:1],
                       shard["fox_w_in"], shard["fox_w_out"], vec_pack], "gather_first", True)
    gvec = first[6].reshape(NDEV, -1)
    full = {k: local[k] for k in _REPLICATED}
    off = 0
    for k, axis in _VECTORS.items():
        n = local[k].size
        blk = jnp.moveaxis(gvec[:, off:off + n].reshape((NDEV,) + local[k].shape), 0, axis)
        full[k] = blk.reshape(_full_shape(local[k].shape, axis))
        off += n
    full["ssd_conv_w"] = full["ssd_conv_w"][0]
    full["ffn_w_gu"] = [first[0][:, 0]]
    full["ffn_w_down"] = [first[1][:, 0].reshape(4, FF_BLOCK, D_MODEL)]
    full["conv_w_in"] = [first[2][:, 0]]
    full["conv_w_out"] = [first[3][:, 0].reshape(D_MODEL, D_MODEL)]
    full["fox_w_in"] = jnp.pad(_cols_from_blocks(first[4][:, 0]), ((0, 0), (0, FOX_IN_PAD - FOX_IN)))
    full["fox_w_out"] = first[5].reshape(D_MODEL, D_MODEL)

    rest = [shard["ffn_w_gu"][1:], shard["ffn_w_down"][1:], shard["conv_w_in"][1:], shard["conv_w_out"][1:],
            shard["ssd_w_in"], shard["ssd_w_out"]]

    def finish(w, got):
        w = dict(w)
        w["ffn_w_gu"] = w["ffn_w_gu"] + [got[0][:, i] for i in range(DEPTH - 1)]
        w["ffn_w_down"] = w["ffn_w_down"] + [got[1][:, i].reshape(4, FF_BLOCK, D_MODEL) for i in range(DEPTH - 1)]
        w["conv_w_in"] = w["conv_w_in"] + [got[2][:, 0]]
        w["conv_w_out"] = w["conv_w_out"] + [got[3][:, 0].reshape(D_MODEL, D_MODEL)]
        w["ssd_w_in"] = jnp.pad(_cols_from_blocks(got[4][:, 0]), ((0, 0), (0, SSM_IN_PAD - SSM_IN)))
        w["ssd_w_out"] = got[5].reshape(SSM_INNER, D_MODEL)
        return w

    def early_slabs(g):
        return ([g["ffn_w_gu"][i] for i in range(1, DEPTH)]
                + [g["ffn_w_down"][i].reshape(NDEV, D_FF // NDEV, D_MODEL) for i in range(1, DEPTH)]
                + [g["conv_w_in"][1], g["conv_w_out"][1].reshape(NDEV, D_MODEL // NDEV, D_MODEL),
                   _blocks_from_cols(g["ssd_w_in"]), g["ssd_w_out"].reshape(NDEV, SSM_INNER // NDEV, D_MODEL)])

    loss_part, dx, grads, early = _local_step(x[0], loss_target[0], full, (rest, finish), early_slabs)

    late = _exchange([grads["ffn_w_gu"][0], grads["ffn_w_down"][0].reshape(NDEV, D_FF // NDEV, D_MODEL), grads["conv_w_in"][0],
                      grads["conv_w_out"][0].reshape(NDEV, D_MODEL // NDEV, D_MODEL), _blocks_from_cols(grads["fox_w_in"]),
                      grads["fox_w_out"].reshape(NDEV, D_MODEL // NDEV, D_MODEL)], "scatter_last", False)
    se = [_sum_slabs(r, f"sum_early_{n}") for n, r in enumerate(early)]
    sl = [_sum_slabs(r, f"sum_late_{n}") for n, r in enumerate(late)]
    shard_grad = {
        "ffn_w_gu": jnp.stack([sl[0]] + se[0:3]), "ffn_w_down": jnp.stack([sl[1]] + se[3:6]),
        "conv_w_in": jnp.stack([sl[2], se[6]]), "conv_w_out": jnp.stack([sl[3], se[7]]),
        "fox_w_in": sl[4][None], "fox_w_out": sl[5][None], "ssd_w_in": se[8][None], "ssd_w_out": se[9][None]}

    small_names = _REPLICATED + list(_VECTORS)
    small = [jnp.reshape(loss_part, (1,))] + [grads[k].reshape(-1) for k in small_names]
    total = _all_sum_small(_to_rows(jnp.concatenate(small)), "sum_small").reshape(-1)
    loss = total[0]
    off = 1
    me = _mesh_position()
    for k, part in zip(small_names, small[1:]):
        gk = total[off:off + part.shape[0]]
        off += part.shape[0]
        if k in _VECTORS:
            axis = _VECTORS[k]
            shp = local[k].shape
            gfull = gk.reshape(shp[:axis] + (NDEV, shp[axis]) + shp[axis + 1:])
            shard_grad[k] = lax.dynamic_index_in_dim(gfull, me, axis, keepdims=False)
        else:
            shard_grad[k] = gk.reshape(local[k].shape)

    deltas, new_m, new_v = {}, {}, {}
    for k in _NAMES:
        deltas[k], new_m[k], new_v[k] = _adamw(local[k], shard_grad[k], mom[k], var[k], f"adamw_{k}")
    return (loss, dx[None], *[shard_grad[k] for k in _NAMES], *[deltas[k] for k in _NAMES],
            *[new_m[k] for k in _NAMES], *[new_v[k] for k in _NAMES])
```

```python
import numpy as np

import jax
import jax.numpy as jnp
from jax import lax
from jax.experimental import pallas as pl
from jax.experimental.pallas import tpu as pltpu

F32 = jnp.float32
BF16 = jnp.bfloat16
HI = lax.Precision.HIGHEST

NDEV = 8
D_MODEL = 1024
DEPTH = 4
D_FF = 2816
FF_BLOCK = 2 * D_FF // NDEV
RMS_EPS = 1e-6
HEAD_DIM = 64
ATTN_HEADS = 16
FOX_IN = 3 * D_MODEL + ATTN_HEADS
FOX_IN_PAD = 3200
SSM_INNER = 2048
SSM_HEADS = 32
SSM_GROUPS = 8
SSM_STATE = 128
SSM_CHUNK = 128
SSM_CONV_DIM = 4096
SSM_IN = SSM_INNER + SSM_CONV_DIM + SSM_HEADS
SSM_IN_PAD = 6272
LANES = 128
V7X_VMEM_BYTES = 64 * 1024 * 1024
VMEM_LIMIT_BYTES = (V7X_VMEM_BYTES * 3) // 4
ATTN_BWD_VMEM_BYTES = (V7X_VMEM_BYTES * 7) // 8
LOG2E = 1.4426950408889634
LN2 = 0.6931471805599453
ATTN_TILE = 1024
ATTN_ROWS = 32

ADAM_LR = 0.001
ADAM_B1 = 0.9
ADAM_B2 = 0.999
ADAM_EPS = 1e-08
ADAM_WD = 0.01
ADAM_STEP = 10

_TILE_CANDIDATES = (1024, 1408, 896, 768, 640, 512, 384, 256, 128)


def _pick_tile(n):
    for c in _TILE_CANDIDATES:
        if n % c == 0:
            return c
    raise ValueError(f"no tile for {n}")


def _params(ngrid):
    return pltpu.CompilerParams(dimension_semantics=("arbitrary",) * ngrid, vmem_limit_bytes=VMEM_LIMIT_BYTES)


def _pc(body, name, grid, in_specs, out_specs, out_shape, scratch=()):
    return pl.pallas_call(
        body, name=name, grid=grid, in_specs=in_specs, out_specs=out_specs, out_shape=out_shape,
        scratch_shapes=list(scratch), compiler_params=_params(len(grid)))


def _dot(a, b, ca, cb, prec=None):
    return lax.dot_general(a, b, (((ca,), (cb,)), ((), ())), preferred_element_type=F32, precision=prec)


def _sds(shape, dtype=F32):
    return jax.ShapeDtypeStruct(shape, dtype)


def _row_tile(s, want=256):
    return want if s % want == 0 else s


def _sigmoid(x):
    return 1.0 / (1.0 + jnp.exp(-x))


def _softplus(x):
    return jnp.maximum(x, 0.0) + jnp.log(1.0 + jnp.exp(-jnp.abs(x)))


def _mm_spec(a, b, name, grid, a_spec, b_spec, o_spec, out, ca, cb, acc_shape, drop=(0, 0, 0), res=None, r_spec=None):
    nk = grid[2]
    da, db, do_ = drop
    has_res = res is not None

    def body(*refs):
        if has_res:
            a_ref, b_ref, r_ref, o_ref, acc_ref = refs
        else:
            a_ref, b_ref, o_ref, acc_ref = refs
        k = pl.program_id(2)

        @pl.when(k == 0)
        def _():
            acc_ref[...] = jnp.zeros_like(acc_ref)

        av = a_ref[(0,) * da] if da else a_ref[...]
        bv = b_ref[(0,) * db] if db else b_ref[...]
        acc_ref[...] += _dot(av.astype(BF16), bv.astype(BF16), ca, cb)

        @pl.when(k == nk - 1)
        def _():
            val = acc_ref[...]
            if has_res:
                val = val + r_ref[...]
            if do_:
                o_ref[(0,) * do_] = val.astype(out.dtype)
            else:
                o_ref[...] = val.astype(out.dtype)

    in_specs = [a_spec, b_spec] + ([r_spec] if has_res else [])
    args = (a, b) + ((res,) if has_res else ())
    return _pc(body, name, grid, in_specs, o_spec, out, [pltpu.VMEM(acc_shape, F32)])(*args)


def _mm(a, b, mode, name, out_dtype=F32, res=None):
    if mode == "tn":
        r, m = a.shape
        n = b.shape[1]
        tm, tn, tk = _pick_tile(m), _pick_tile(n), _pick_tile(r)
        grid = (m // tm, n // tn, r // tk)
        a_spec = pl.BlockSpec((tk, tm), lambda i, j, k: (k, i))
        b_spec = pl.BlockSpec((tk, tn), lambda i, j, k: (k, j))
        ca, cb = 0, 0
    else:
        m, kd = a.shape
        n = b.shape[1] if mode == "nn" else b.shape[0]
        tm, tn, tk = _pick_tile(m), _pick_tile(n), _pick_tile(kd)
        grid = (m // tm, n // tn, kd // tk)
        a_spec = pl.BlockSpec((tm, tk), lambda i, j, k: (i, k))
        if mode == "nn":
            b_spec = pl.BlockSpec((tk, tn), lambda i, j, k: (k, j))
            ca, cb = 1, 0
        else:
            b_spec = pl.BlockSpec((tn, tk), lambda i, j, k: (j, k))
            ca, cb = 1, 1
    o_spec = pl.BlockSpec((tm, tn), lambda i, j, k: (i, j))
    return _mm_spec(a, b, name, grid, a_spec, b_spec, o_spec, _sds((m, n), out_dtype), ca, cb, (tm, tn), res=res, r_spec=o_spec)


def _rms_fwd(x, w, name):
    s, d = x.shape
    ts = _row_tile(s)

    def body(x_ref, w_ref, o_ref):
        xv = x_ref[...]
        r = lax.rsqrt(jnp.mean(xv * xv, axis=-1, keepdims=True) + RMS_EPS)
        o_ref[...] = ((xv * r) * w_ref[...]).astype(BF16)

    row = pl.BlockSpec((ts, d), lambda i: (i, 0))
    return _pc(body, name, (s // ts,), [row, pl.BlockSpec((1, d), lambda i: (0, 0))], row, _sds((s, d), BF16))(x, w)


def _mm_dnorm(a, b, name, nk, a_spec, b_spec, ca, cb, drop, x, w, dres):
    s, d = x.shape
    tm = _pick_tile(s)
    da, db = drop

    def body(a_ref, b_ref, x_ref, w_ref, r_ref, dx_ref, dw_ref, acc_ref):
        i = pl.program_id(0)
        k = pl.program_id(2)

        @pl.when(k == 0)
        def _():
            acc_ref[...] = jnp.zeros_like(acc_ref)

        av = a_ref[(0,) * da] if da else a_ref[...]
        bv = b_ref[(0,) * db] if db else b_ref[...]
        acc_ref[...] += _dot(av.astype(BF16), bv.astype(BF16), ca, cb)

        @pl.when(k == nk - 1)
        def _():
            dhv = acc_ref[...]
            xv = x_ref[...]
            r = lax.rsqrt(jnp.mean(xv * xv, axis=-1, keepdims=True) + RMS_EPS)
            xhat = xv * r
            g = dhv * w_ref[...]
            dx_ref[...] = r_ref[...] + r * (g - xhat * jnp.mean(g * xhat, axis=-1, keepdims=True))
            part = jnp.sum(dhv * xhat, axis=0, keepdims=True)

            @pl.when(i == 0)
            def _():
                dw_ref[...] = part

            @pl.when(i > 0)
            def _():
                dw_ref[...] += part

    row = pl.BlockSpec((tm, d), lambda i, j, k: (i, 0))
    vec = pl.BlockSpec((1, d), lambda i, j, k: (0, 0))
    return _pc(body, name, (s // tm, 1, nk), [a_spec, b_spec, row, vec, row], [row, vec], [_sds((s, d)), _sds((1, d))],
               [pltpu.VMEM((tm, d), F32)])(a, b, x, w, dres)


def _mm_dnorm_nt(dproj, w_in, name, x, w, dres):
    tm = _pick_tile(x.shape[0])
    tk = _pick_tile(dproj.shape[1])
    return _mm_dnorm(dproj, w_in, name, dproj.shape[1] // tk, pl.BlockSpec((tm, tk), lambda i, j, k: (i, k)),
                     pl.BlockSpec((D_MODEL, tk), lambda i, j, k: (0, k)), 1, 1, (0, 0), x, w, dres)


def _ffn_gate_up(h, w_gu, name):
    s = h.shape[0]
    tm = _pick_tile(s)

    def body(h_ref, wg_ref, wu_ref, gu_ref, a_ref):
        hv = h_ref[...]
        g = _dot(hv, wg_ref[0], 1, 0)
        u = _dot(hv, wu_ref[0], 1, 0)
        gu_ref[0, 0] = g.astype(BF16)
        gu_ref[0, 1] = u.astype(BF16)
        a_ref[0] = (g * _sigmoid(g) * u).astype(BF16)

    wblk = lambda off: pl.BlockSpec((1, D_MODEL, FF_BLOCK), lambda i, k: (k + off, 0, 0))
    return _pc(body, name, (s // tm, 4), [pl.BlockSpec((tm, D_MODEL), lambda i, k: (i, 0)), wblk(0), wblk(4)],
               [pl.BlockSpec((1, 2, tm, FF_BLOCK), lambda i, k: (k, 0, i, 0)), pl.BlockSpec((1, tm, FF_BLOCK), lambda i, k: (k, i, 0))],
               [_sds((4, 2, s, FF_BLOCK), BF16), _sds((4, s, FF_BLOCK), BF16)])(h, w_gu, w_gu)


def _ffn_dgate_up(dy, w_down, gu, name):
    s = dy.shape[0]
    tm = _pick_tile(s)

    def body(dy_ref, w_ref, gu_ref, o_ref):
        dav = _dot(dy_ref[...].astype(BF16), w_ref[0], 1, 1)
        g = gu_ref[0, 0].astype(F32)
        u = gu_ref[0, 1].astype(F32)
        sg = _sigmoid(g)
        o_ref[0, 0] = (dav * u * (sg * (1.0 + g * (1.0 - sg)))).astype(BF16)
        o_ref[0, 1] = (dav * (g * sg)).astype(BF16)

    pair = pl.BlockSpec((1, 2, tm, FF_BLOCK), lambda i, k: (k, 0, i, 0))
    return _pc(body, name, (s // tm, 4),
               [pl.BlockSpec((tm, D_MODEL), lambda i, k: (i, 0)), pl.BlockSpec((1, FF_BLOCK, D_MODEL), lambda i, k: (k, 0, 0)), pair],
               pair, _sds((4, 2, s, FF_BLOCK), BF16))(dy, w_down, gu)


def _ffn_fwd(x, norm_w, w_gu, w_down, tag):
    s = x.shape[0]
    tm = _pick_tile(s)
    h = _rms_fwd(x, norm_w, f"ffn_norm_{tag}")
    gu, a = _ffn_gate_up(h, w_gu, f"ffn_gu_{tag}")
    xspec = pl.BlockSpec((tm, D_MODEL), lambda i, j, k: (i, 0))
    y = _mm_spec(a, w_down, f"ffn_down_{tag}", (s // tm, 1, 4),
                 pl.BlockSpec((1, tm, FF_BLOCK), lambda i, j, k: (k, i, 0)),
                 pl.BlockSpec((1, FF_BLOCK, D_MODEL), lambda i, j, k: (k, 0, 0)),
                 xspec, _sds((s, D_MODEL)), 1, 0, (tm, D_MODEL), drop=(1, 1, 0), res=x, r_spec=xspec)
    return y, (x, h, gu, a)


def _ffn_bwd(dy, saved, norm_w, w_gu, w_down, tag):
    x, h, gu, a = saved
    s = x.shape[0]
    tm = _pick_tile(s)
    g_down = _mm_spec(a, dy, f"ffn_gdown_{tag}", (4, 1, s // tm),
                      pl.BlockSpec((1, tm, FF_BLOCK), lambda i, j, k: (i, k, 0)),
                      pl.BlockSpec((tm, D_MODEL), lambda i, j, k: (k, 0)),
                      pl.BlockSpec((1, FF_BLOCK, D_MODEL), lambda i, j, k: (i, 0, 0)),
                      _sds((4, FF_BLOCK, D_MODEL), BF16), 0, 0, (FF_BLOCK, D_MODEL), drop=(1, 0, 1))
    dgu = _ffn_dgate_up(dy, w_down, gu, f"ffn_dgu_{tag}")
    g_gu = _mm_spec(h, dgu, f"ffn_ggu_{tag}", (NDEV, 1, s // tm),
                    pl.BlockSpec((tm, D_MODEL), lambda i, j, k: (k, 0)),
                    pl.BlockSpec((1, 1, tm, FF_BLOCK), lambda i, j, k: (i % 4, i // 4, k, 0)),
                    pl.BlockSpec((1, D_MODEL, FF_BLOCK), lambda i, j, k: (i, 0, 0)),
                    _sds((NDEV, D_MODEL, FF_BLOCK), BF16), 0, 0, (D_MODEL, FF_BLOCK), drop=(0, 2, 1))
    dx, g_norm = _mm_dnorm(dgu, w_gu, f"ffn_dh_{tag}", NDEV,
                           pl.BlockSpec((1, 1, tm, FF_BLOCK), lambda i, j, k: (k % 4, k // 4, i, 0)),
                           pl.BlockSpec((1, D_MODEL, FF_BLOCK), lambda i, j, k: (k, 0, 0)), 1, 1, (2, 1), x, norm_w, dy)
    return dx, g_norm, g_gu, g_down


def _prev_rows(cur, halo, j, first):
    rid = lax.broadcasted_iota(jnp.int32, cur.shape, 0)
    hid = lax.broadcasted_iota(jnp.int32, halo.shape, 0)
    out = pltpu.roll(cur, j, 0)
    for t in range(j):
        row = jnp.sum(jnp.where(hid == 8 - j + t, halo, 0.0), axis=0, keepdims=True)
        row = jnp.where(first, 0.0, row)
        out = jnp.where(rid == t, row, out)
    return out


def _next_rows(cur, halo, j, last):
    ts = cur.shape[0]
    rid = lax.broadcasted_iota(jnp.int32, cur.shape, 0)
    hid = lax.broadcasted_iota(jnp.int32, halo.shape, 0)
    out = pltpu.roll(cur, ts - j, 0)
    for t in range(j):
        row = jnp.sum(jnp.where(hid == t, halo, 0.0), axis=0, keepdims=True)
        row = jnp.where(last, 0.0, row)
        out = jnp.where(rid == ts - j + t, row, out)
    return out


def _halo_specs(ts, s, width, col):
    per = ts // 8
    nblk = s // 8
    prev = pl.BlockSpec((8, width), lambda i: (jnp.maximum(i * per - 1, 0), col))
    nxt = pl.BlockSpec((8, width), lambda i: (jnp.minimum((i + 1) * per, nblk - 1), col))
    return prev, nxt


def _cgate_fwd(p, w_dw, name):
    s = p.shape[0]
    d = D_MODEL
    ts = _row_tile(s)
    prev, _ = _halo_specs(ts, s, 3 * d, 0)

    def body(p_ref, h_ref, w_ref, z_ref):
        first = pl.program_id(0) == 0
        b = p_ref[:, :d]
        cv = p_ref[:, d:2 * d] * p_ref[:, 2 * d:]
        hcv = h_ref[:, d:2 * d] * h_ref[:, 2 * d:]
        u = w_ref[2:3, :] * cv + w_ref[1:2, :] * _prev_rows(cv, hcv, 1, first) + w_ref[0:1, :] * _prev_rows(cv, hcv, 2, first)
        z_ref[...] = (b * u).astype(BF16)

    return _pc(body, name, (s // ts,),
               [pl.BlockSpec((ts, 3 * d), lambda i: (i, 0)), prev, pl.BlockSpec((3, d), lambda i: (0, 0))],
               pl.BlockSpec((ts, d), lambda i: (i, 0)), _sds((s, d), BF16))(p, p, w_dw)


def _cgate_bwd(p, dz, w_dw, name):
    s = p.shape[0]
    d = D_MODEL
    ts = _row_tile(s)
    nt = s // ts
    p_prev, p_next = _halo_specs(ts, s, 3 * d, 0)
    _, dz_next = _halo_specs(ts, s, d, 0)

    def body(p_ref, hp_ref, hn_ref, dz_ref, dzn_ref, w_ref, dp_ref, dw_ref):
        i = pl.program_id(0)
        first = i == 0
        last = i == nt - 1
        b = p_ref[:, :d]
        c = p_ref[:, d:2 * d]
        v = p_ref[:, 2 * d:]
        cv = c * v
        hcv = hp_ref[:, d:2 * d] * hp_ref[:, 2 * d:]
        cv1 = _prev_rows(cv, hcv, 1, first)
        cv2 = _prev_rows(cv, hcv, 2, first)
        w0, w1, w2 = w_ref[0:1, :], w_ref[1:2, :], w_ref[2:3, :]
        u = w2 * cv + w1 * cv1 + w0 * cv2
        dzv = dz_ref[...]
        du = dzv * b
        dun = dzn_ref[...] * hn_ref[:, :d]
        dcv = w2 * du + w1 * _next_rows(du, dun, 1, last) + w0 * _next_rows(du, dun, 2, last)
        dp_ref[:, :d] = (dzv * u).astype(BF16)
        dp_ref[:, d:2 * d] = (dcv * v).astype(BF16)
        dp_ref[:, 2 * d:] = (dcv * c).astype(BF16)

        @pl.when(first)
        def _():
            dw_ref[...] = jnp.zeros_like(dw_ref)

        dw_ref[0:1, :] += jnp.sum(du * cv2, axis=0, keepdims=True)
        dw_ref[1:2, :] += jnp.sum(du * cv1, axis=0, keepdims=True)
        dw_ref[2:3, :] += jnp.sum(du * cv, axis=0, keepdims=True)

    wide = pl.BlockSpec((ts, 3 * d), lambda i: (i, 0))
    wspec = pl.BlockSpec((3, d), lambda i: (0, 0))
    return _pc(body, name, (nt,),
               [wide, p_prev, p_next, pl.BlockSpec((ts, d), lambda i: (i, 0)), dz_next, wspec],
               [wide, wspec], [_sds((s, 3 * d), BF16), _sds((3, d))])(p, p, p, dz, dz, w_dw)


def _conv_fwd(x, norm_w, w_in, w_dw, w_out, tag):
    wn = _cols_from_blocks(w_in)
    h = _rms_fwd(x, norm_w, f"conv_norm_{tag}")
    p = _mm(h, wn, "nn", f"conv_in_{tag}")
    z = _cgate_fwd(p, w_dw, f"conv_gate_{tag}")
    y = _mm(z, w_out, "nn", f"conv_out_{tag}", res=x)
    return y, (x, h, p, z, wn)


def _conv_bwd(dy, saved, norm_w, w_in, w_dw, w_out, tag):
    x, h, p, z, wn = saved
    dz = _mm(dy, w_out, "nt", f"conv_dz_{tag}")
    g_out = _mm(z, dy, "tn", f"conv_gout_{tag}", out_dtype=BF16)
    dp, g_dw = _cgate_bwd(p, dz, w_dw, f"conv_dgate_{tag}")
    g_in = _blocks_from_cols(_mm(h, dp, "tn", f"conv_gin_{tag}", out_dtype=BF16))
    dx, g_norm = _mm_dnorm_nt(dp, wn, f"conv_dh_{tag}", x, norm_w, dy)
    return dx, g_norm, g_in, g_dw, g_out


def _tri(lower):
    r = lax.broadcasted_iota(jnp.int32, (LANES, LANES), 0)
    c = lax.broadcasted_iota(jnp.int32, (LANES, LANES), 1)
    return jnp.where((r >= c) if lower else (r <= c), 1.0, 0.0).astype(F32)


def _cumsum_rows(v, reverse, name):
    s = v.shape[0]
    n = s // LANES
    idx = (lambda i: (n - 1 - i, 0)) if reverse else (lambda i: (i, 0))

    def body(v_ref, o_ref, carry_ref):
        @pl.when(pl.program_id(0) == 0)
        def _():
            carry_ref[...] = jnp.zeros_like(carry_ref)

        blk = v_ref[...]
        o_ref[...] = _dot(_tri(not reverse), blk, 1, 0, HI) + carry_ref[0:1, :]
        carry_ref[...] += jnp.sum(blk, axis=0, keepdims=True)

    spec = pl.BlockSpec((LANES, LANES), idx)
    return _pc(body, name, (n,), [spec], spec, _sds((s, LANES)), [pltpu.VMEM((8, LANES), F32)])(v)


def _lo_mask(shape):
    return lax.broadcasted_iota(jnp.int32, shape, len(shape) - 1) < HEAD_DIM


def _half_sums(v, lo):
    sa = jnp.sum(jnp.where(lo, v, 0.0), axis=-1, keepdims=True)
    sb = jnp.sum(jnp.where(lo, 0.0, v), axis=-1, keepdims=True)
    return jnp.where(lo, sa, sb)


def _fox_prep_fwd(proj, gq, gk, name):
    s = proj.shape[0]
    ts = _row_tile(s)
    qscale = HEAD_DIM ** -0.5 * LOG2E

    def body(q_ref, k_ref, v_ref, gq_ref, gk_ref, qo_ref, ko_ref, vo_ref):
        lo = _lo_mask((ts, LANES))

        def hnorm(xv, g):
            ms = _half_sums(xv * xv, lo) * (1.0 / HEAD_DIM)
            return (xv * lax.rsqrt(ms + RMS_EPS)) * g

        for p in range(8):
            cols = slice(p * LANES, (p + 1) * LANES)
            qo_ref[:, cols] = (hnorm(q_ref[:, cols], gq_ref[...]) * qscale).astype(BF16)
            ko_ref[:, cols] = hnorm(k_ref[:, cols], gk_ref[...]).astype(BF16)
        vo_ref[...] = v_ref[...].astype(BF16)

    def wide(blk):
        return pl.BlockSpec((ts, D_MODEL), lambda i: (i, blk))

    gspec = pl.BlockSpec((1, LANES), lambda i: (0, 0))
    out = _sds((s, D_MODEL), BF16)
    return _pc(body, name, (s // ts,), [wide(0), wide(1), wide(2), gspec, gspec], [wide(0)] * 3, [out] * 3)(
        proj, proj, proj, gq, gk)


def _fox_logf(proj, bf, name):
    s = proj.shape[0]
    ts = _row_tile(s, 512)

    def body(f_ref, b_ref, o_ref):
        z = f_ref[...] + b_ref[...]
        lf = jnp.minimum(z, 0.0) - jnp.log(1.0 + jnp.exp(-jnp.abs(z)))
        real = lax.broadcasted_iota(jnp.int32, (ts, LANES), 1) < ATTN_HEADS
        o_ref[...] = jnp.where(real, lf, 0.0)

    return _pc(body, name, (s // ts,), [pl.BlockSpec((ts, LANES), lambda i: (i, 24)), pl.BlockSpec((1, LANES), lambda i: (0, 0))],
               pl.BlockSpec((ts, LANES), lambda i: (i, 0)), _sds((s, LANES)))(proj, bf)


def _fox_dlogf(proj, bf, dlf, name):
    s = proj.shape[0]
    ts = _row_tile(s, 512)

    def body(f_ref, b_ref, d_ref, o_ref, db_ref):
        z = f_ref[...] + b_ref[...]
        real = lax.broadcasted_iota(jnp.int32, (ts, LANES), 1) < ATTN_HEADS
        g = jnp.where(real, d_ref[...] * _sigmoid(-z), 0.0)
        o_ref[...] = g.astype(BF16)

        @pl.when(pl.program_id(0) == 0)
        def _():
            db_ref[...] = jnp.zeros_like(db_ref)

        db_ref[...] += jnp.sum(g, axis=0, keepdims=True)

    vec = pl.BlockSpec((1, LANES), lambda i: (0, 0))
    row = pl.BlockSpec((ts, LANES), lambda i: (i, 0))
    return _pc(body, name, (s // ts,), [pl.BlockSpec((ts, LANES), lambda i: (i, 24)), vec, row], [row, vec],
               [_sds((s, LANES), BF16), _sds((1, LANES))])(proj, bf, dlf)


def _decay_terms(cum):
    s = cum.shape[0]
    c2 = cum * LOG2E
    hi = lax.reduce_precision(c2, 8, 7)
    mid = lax.reduce_precision(c2 - hi, 8, 7)
    low = lax.reduce_precision(c2 - hi - mid, 8, 7)
    one = jnp.ones_like(hi)

    def place(terms):
        tt = jnp.stack(terms, axis=-1).astype(BF16).reshape(s, 8, 2, 6)
        z = jnp.zeros((s, 8, HEAD_DIM - 6), BF16)
        return jnp.concatenate([tt[:, :, 1], z, tt[:, :, 0], z], axis=-1).reshape(s, D_MODEL)

    return place([hi, mid, low, one, one, one]), place([one, one, one, -hi, -mid, -low])


def _attn_tiles(s):
    t = s
    for cand in (ATTN_TILE, ATTN_TILE // 2):
        if s % cand == 0:
            t = cand
            break
    return t, s // t


def _tri_steps(n, by_key):
    if by_key:
        pairs = [(q, k) for k in range(n) for q in range(k, n)]
    else:
        pairs = [(q, k) for q in range(n) for k in range(q + 1)]
    arr = np.asarray(pairs, np.int32)
    return jnp.asarray(arr[:, 0]), jnp.asarray(arr[:, 1])


def _attn_call(body, name, s, by_key, inputs, in_kinds, out_kinds, out_shapes, scratch, hosted=None, vmem=VMEM_LIMIT_BYTES):
    t, n = _attn_tiles(s)
    qi_arr, ki_arr = _tri_steps(n, by_key)
    nsteps = int(qi_arr.shape[0])
    specs = {
        "q": pl.BlockSpec((t, LANES), lambda p, i, qi, ki: (qi[i], p)),
        "k": pl.BlockSpec((t, LANES), lambda p, i, qi, ki: (ki[i], p)),
        "r": pl.BlockSpec((1, 2, t), lambda p, i, qi, ki: (p, 0, qi[i])),
        "m": pl.BlockSpec((1, t, t), lambda p, i, qi, ki: (jnp.where(qi[i] == ki[i], 1, 0), 0, 0)),
        "Q": pl.BlockSpec((1, LANES, s), lambda p, i, qi, ki: (p, 0, 0)),
        "R": pl.BlockSpec((1, 2, s), lambda p, i, qi, ki: (p, 0, 0)),
    }
    in_specs = [specs[c] for c in in_kinds]
    out_specs = [specs[c] for c in out_kinds]
    out_shapes, scratch, inputs = list(out_shapes), list(scratch), list(inputs)
    run = body
    if hosted is not None:
        arrays, gather = hosted
        na, n_in, n_out, n_scr = len(arrays), len(inputs), len(out_kinds), len(scratch)
        pick, xouts, sems = _exchange_parts(arrays, gather)

        def run(qi_ref, ki_ref, *refs):
            ins, srcs = refs[:n_in], refs[n_in:n_in + na]
            outs, dsts = refs[n_in + na:n_in + na + n_out], refs[n_in + na + n_out:n_in + 2 * na + n_out]
            scr, xsems = refs[n_in + 2 * na + n_out:n_in + 2 * na + n_out + n_scr], refs[n_in + 2 * na + n_out + n_scr:]
            p = pl.program_id(0)
            i = pl.program_id(1)

            @pl.when(jnp.logical_and(p == 0, i == 0))
            def _():
                _exchange_start(_exchange_copies(pick(srcs), dsts, *xsems))

            body(qi_ref, ki_ref, *ins, *outs, *scr)

            @pl.when(jnp.logical_and(p == 7, i == nsteps - 1))
            def _():
                _exchange_wait(_exchange_copies(pick(srcs), dsts, *xsems))

        hbm = pl.BlockSpec(memory_space=pl.ANY)
        in_specs += [hbm] * na
        out_specs += [hbm] * na
        out_shapes += xouts
        scratch += sems
        inputs += list(arrays)
    grid_spec = pltpu.PrefetchScalarGridSpec(
        num_scalar_prefetch=2, grid=(8, nsteps), in_specs=in_specs, out_specs=out_specs, scratch_shapes=scratch)
    params = pltpu.CompilerParams(dimension_semantics=("arbitrary", "arbitrary"), vmem_limit_bytes=vmem)
    return pl.pallas_call(run, name=name, grid_spec=grid_spec, out_shape=out_shapes, compiler_params=params)(
        qi_arr, ki_arr, *inputs)


def _biased_kq(q2, k2, aq, ak, lo):
    sa = _dot(jnp.where(lo, k2, ak), jnp.where(lo, q2, aq), 1, 1)
    sb = _dot(jnp.where(lo, ak, k2), jnp.where(lo, aq, q2), 1, 1)
    return sa, sb


def _causal_bias(s):
    t, _ = _attn_tiles(s)
    kid = lax.broadcasted_iota(jnp.int32, (t, t), 0)
    qid = lax.broadcasted_iota(jnp.int32, (t, t), 1)
    return jnp.stack([jnp.zeros((t, t), BF16), jnp.where(kid > qid, -jnp.inf, 0.0).astype(BF16)])


def _fold8(v, op):
    return op(v.reshape(v.shape[0] // 8, 8, v.shape[1]), axis=0)


def _chunk(ref, mask_ref, hd, r):
    rows = slice(r * ATTN_ROWS, (r + 1) * ATTN_ROWS)
    return rows, ref[hd, rows, :] + mask_ref[0, rows, :].astype(F32)


def _flash_fwd(qs, kn, vb, augq, augk, cmask, name, hosted=None):
    s = qs.shape[0]
    t, n = _attn_tiles(s)
    nch = t // ATTN_ROWS

    def body(qi_ref, ki_ref, q_ref, k_ref, v_ref, aq_ref, ak_ref, mk_ref, o_ref, lse_ref, s_ref, p_ref, m_ref, l_ref, acc_ref):
        i = pl.program_id(1)
        qi = qi_ref[i]
        ki = ki_ref[i]

        @pl.when(ki == 0)
        def _():
            m_ref[...] = jnp.full_like(m_ref, -jnp.inf)
            l_ref[...] = jnp.zeros_like(l_ref)
            acc_ref[...] = jnp.zeros_like(acc_ref)

        lo = _lo_mask((t, LANES))
        rowlo = lax.broadcasted_iota(jnp.int32, (LANES, t), 0) < HEAD_DIM
        v2 = v_ref[...]
        sa, sb = _biased_kq(q_ref[...], k_ref[...], aq_ref[...], ak_ref[...], lo)
        s_ref[0] = sa
        s_ref[1] = sb
        alphas, pvs = [], []
        for hd in range(2):
            mx = jnp.full((8, t), -jnp.inf, F32)
            for r in range(nch):
                _, sc = _chunk(s_ref, mk_ref, hd, r)
                mx = jnp.maximum(mx, _fold8(sc, jnp.max))
            m_prev = m_ref[hd:hd + 1, :]
            m_new = jnp.maximum(m_prev, jnp.max(mx, axis=0, keepdims=True))
            ls = jnp.zeros((8, t), F32)
            for r in range(nch):
                rows, sc = _chunk(s_ref, mk_ref, hd, r)
                pm = jnp.exp2(sc - m_new)
                ls = ls + _fold8(pm, jnp.sum)
                p_ref[hd, rows, :] = pm.astype(BF16)
            alpha = jnp.exp2(m_prev - m_new)
            l_ref[hd:hd + 1, :] = alpha * l_ref[hd:hd + 1, :] + jnp.sum(ls, axis=0, keepdims=True)
            m_ref[hd:hd + 1, :] = m_new
            alphas.append(alpha)
            pvs.append(_dot(v2, p_ref[hd], 0, 0))
        acc_ref[...] = jnp.where(rowlo, alphas[0], alphas[1]) * acc_ref[...] + jnp.where(rowlo, pvs[0], pvs[1])

        @pl.when(ki == qi)
        def _():
            o_ref[...] = (acc_ref[...] / jnp.where(rowlo, l_ref[0:1, :], l_ref[1:2, :])).T
            lse_ref[0] = m_ref[0:2, :] + jnp.log2(l_ref[0:2, :])

    stat = pltpu.VMEM((8, t), F32)
    return _attn_call(body, name, s, False, (qs, kn, vb, augq, augk, cmask), "qkkqkm", "qr",
                      [_sds((s, D_MODEL)), _sds((8, 2, s))],
                      [pltpu.VMEM((2, t, t), F32), pltpu.VMEM((2, t, t), BF16), stat, stat, pltpu.VMEM((LANES, t), F32)],
                      hosted=hosted)


def _fox_delta(do, o, name):
    s = do.shape[0]
    ts = _row_tile(s)

    def body(do_ref, o_ref, d_ref):
        lo = _lo_mask((ts, LANES))
        for p in range(8):
            cols = slice(p * LANES, (p + 1) * LANES)
            d_ref[:, cols] = _half_sums(do_ref[:, cols] * o_ref[:, cols], lo)

    spec = pl.BlockSpec((ts, D_MODEL), lambda i: (i, 0))
    return _pc(body, name, (s // ts,), [spec, spec], spec, _sds((s, D_MODEL)))(do, o)


def _bwd_tile(q_ref, k_ref, v_ref, aq_ref, ak_ref, do_ref, s_ref, dp_ref, lo):
    do2 = do_ref[...].astype(BF16)
    zero = jnp.zeros_like(do2)
    v2 = v_ref[...]
    sa, sb = _biased_kq(q_ref[...], k_ref[...], aq_ref[...], ak_ref[...], lo)
    s_ref[0] = sa
    s_ref[1] = sb
    dp_ref[0] = _dot(v2, jnp.where(lo, do2, zero), 1, 1)
    dp_ref[1] = _dot(v2, jnp.where(lo, zero, do2), 1, 1)
    return do2


def _bwd_chunk(s_ref, dp_ref, mk_ref, lse_ref, dl_ref, hd, r):
    rows, sc = _chunk(s_ref, mk_ref, hd, r)
    pm = jnp.exp2(sc - lse_ref[0, hd:hd + 1, :])
    ds = pm * (dp_ref[hd, rows, :] - dl_ref[0, hd:hd + 1, :])
    return rows, pm, ds


def _flash_bwd(qs, kn, vb, augq, augk, cmask, do, lse, delta, name, hosted=None):
    s = qs.shape[0]
    t, n = _attn_tiles(s)
    nch = t // ATTN_ROWS

    def body(qi_ref, ki_ref, q_ref, k_ref, v_ref, aq_ref, ak_ref, mk_ref, do_ref, lse_ref, dl_ref,
             dk_ref, dv_ref, dc_ref, dq_ref, dcq_ref, s_ref, dp_ref, p_ref, ds_ref, dka_ref, dva_ref, dca_ref):
        i = pl.program_id(1)
        qi = qi_ref[i]
        ki = ki_ref[i]

        @pl.when(i == 0)
        def _():
            dq_ref[...] = jnp.zeros_like(dq_ref)
            dcq_ref[...] = jnp.zeros_like(dcq_ref)

        @pl.when(qi == ki)
        def _():
            dka_ref[...] = jnp.zeros_like(dka_ref)
            dva_ref[...] = jnp.zeros_like(dva_ref)
            dca_ref[...] = jnp.zeros_like(dca_ref)

        lo = _lo_mask((t, LANES))
        rowlo = lax.broadcasted_iota(jnp.int32, (LANES, t), 0) < HEAD_DIM
        do2 = _bwd_tile(q_ref, k_ref, v_ref, aq_ref, ak_ref, do_ref, s_ref, dp_ref, lo)
        q2 = q_ref[...]
        k2 = k_ref[...]
        qcols = pl.ds(pl.multiple_of(qi * t, t), t)
        dvs, dks, dqs = [], [], []
        for hd in range(2):
            rs = jnp.zeros((8, t), F32)
            for r in range(nch):
                rows, pm, ds = _bwd_chunk(s_ref, dp_ref, mk_ref, lse_ref, dl_ref, hd, r)
                rs = rs + _fold8(ds, jnp.sum)
                part = ds[:, 0:LANES]
                for c in range(1, t // LANES):
                    part = part + ds[:, c * LANES:(c + 1) * LANES]
                dca_ref[hd, rows, :] += part
                p_ref[hd, rows, :] = pm.astype(BF16)
                ds_ref[hd, rows, :] = ds.astype(BF16)
            dcq_ref[0, hd:hd + 1, qcols] += jnp.sum(rs, axis=0, keepdims=True)
            dvs.append(_dot(p_ref[hd], do2, 1, 0))
            dks.append(_dot(ds_ref[hd], q2, 1, 0))
            dqs.append(_dot(k2, ds_ref[hd], 0, 0))
        dva_ref[...] += jnp.where(lo, dvs[0], dvs[1])
        dka_ref[...] += jnp.where(lo, dks[0], dks[1])
        dq_ref[0, :, qcols] += jnp.where(rowlo, dqs[0], dqs[1])

        @pl.when(qi == n - 1)
        def _():
            dk_ref[...] = dka_ref[...] * LN2
            dv_ref[...] = dva_ref[...]
            dc_ref[...] = -jnp.where(lo, jnp.sum(dca_ref[0], axis=-1, keepdims=True), jnp.sum(dca_ref[1], axis=-1, keepdims=True))

    out = _sds((s, D_MODEL))
    return _attn_call(body, name, s, True, (qs, kn, vb, augq, augk, cmask, do, lse, delta), "qkkqkmqrr", "kkkQR",
                      [out, out, out, _sds((8, LANES, s)), _sds((8, 2, s))],
                      [pltpu.VMEM((2, t, t), F32), pltpu.VMEM((2, t, t), F32), pltpu.VMEM((2, t, t), BF16),
                       pltpu.VMEM((2, t, t), BF16), pltpu.VMEM((t, LANES), F32), pltpu.VMEM((t, LANES), F32),
                       pltpu.VMEM((2, t, LANES), F32)], hosted=hosted, vmem=ATTN_BWD_VMEM_BYTES)


def _fox_prep_bwd(proj, dqs, dk, dv, gq, gk, name):
    s = proj.shape[0]
    ts = _row_tile(s)
    scale = HEAD_DIM ** -0.5

    def body(q_ref, k_ref, dq_ref, dk_ref, dv_ref, gq_ref, gk_ref, oq_ref, ok_ref, ov_ref, dgq_ref, dgk_ref):
        lo = _lo_mask((ts, LANES))

        @pl.when(pl.program_id(0) == 0)
        def _():
            dgq_ref[...] = jnp.zeros_like(dgq_ref)
            dgk_ref[...] = jnp.zeros_like(dgk_ref)

        def back(xv, dout, g):
            r = lax.rsqrt(_half_sums(xv * xv, lo) * (1.0 / HEAD_DIM) + RMS_EPS)
            y = xv * r
            dy = dout * g
            dx = r * (dy - y * (_half_sums(dy * y, lo) * (1.0 / HEAD_DIM)))
            return dx, jnp.sum(dout * y, axis=0, keepdims=True)

        for p in range(8):
            cols = slice(p * LANES, (p + 1) * LANES)
            dxq, dgq = back(q_ref[:, cols], dq_ref[p].T * scale, gq_ref[...])
            dxk, dgk = back(k_ref[:, cols], dk_ref[:, cols], gk_ref[...])
            oq_ref[:, cols] = dxq.astype(BF16)
            ok_ref[:, cols] = dxk.astype(BF16)
            dgq_ref[...] += dgq
            dgk_ref[...] += dgk
        ov_ref[...] = dv_ref[...].astype(BF16)

    def wide(blk):
        return pl.BlockSpec((ts, D_MODEL), lambda i: (i, blk))

    gspec = pl.BlockSpec((1, LANES), lambda i: (0, 0))
    out = _sds((s, D_MODEL), BF16)
    dqt = pl.BlockSpec((8, LANES, ts), lambda i: (0, 0, i))
    return _pc(body, name, (s // ts,), [wide(0), wide(1), dqt, wide(0), wide(0), gspec, gspec],
               [wide(0)] * 3 + [gspec] * 2, [out] * 3 + [_sds((1, LANES))] * 2)(proj, proj, dqs, dk, dv, gq, gk)


def _fox_fwd(x, norm_w, w_in, b_f, q_gain, k_gain, w_out, hosted=None):
    h = _rms_fwd(x, norm_w, "fox_norm")
    proj = _mm(h, w_in, "nn", "fox_in")
    gq = jnp.tile(q_gain, (1, 2))
    gk = jnp.tile(k_gain, (1, 2))
    bf = jnp.pad(b_f, ((0, 0), (0, LANES - ATTN_HEADS)))
    qs, kn, vb = _fox_prep_fwd(proj, gq, gk, "fox_prep")
    cum = _cumsum_rows(_fox_logf(proj, bf, "fox_logf"), False, "fox_cum")[:, :ATTN_HEADS]
    augq, augk = _decay_terms(cum)
    cmask = _causal_bias(x.shape[0])
    o, lse, *got = _flash_fwd(qs, kn, vb, augq, augk, cmask, "fox_attn", hosted=hosted)
    y = _mm(o, w_out, "nn", "fox_out", res=x)
    return y, (x, h, proj, gq, gk, bf, qs, kn, vb, augq, augk, cmask, o, lse), got


def _fox_bwd(dy, saved, norm_w, w_in, w_out, hosted=None):
    x, h, proj, gq, gk, bf, qs, kn, vb, augq, augk, cmask, o, lse = saved
    s = x.shape[0]
    do = _mm(dy, w_out, "nt", "fox_do")
    g_out = _mm(o, dy, "tn", "fox_gout", out_dtype=BF16)
    delta = _fox_delta(do, o, "fox_delta")[:, ::HEAD_DIM].T.reshape(8, 2, s)
    dk, dv, dck, dqs, dcq, *got = _flash_bwd(qs, kn, vb, augq, augk, cmask, do, lse, delta, "fox_dattn", hosted=hosted)
    dcum = jnp.pad(dcq.reshape(ATTN_HEADS, s).T + dck[:, ::HEAD_DIM], ((0, 0), (0, LANES - ATTN_HEADS)))
    dlf = _cumsum_rows(dcum, True, "fox_dcum")
    dfl, g_bf = _fox_dlogf(proj, bf, dlf, "fox_dlogf")
    dq_o, dk_o, dv_o, g_gq, g_gk = _fox_prep_bwd(proj, dqs, dk, dv, gq, gk, "fox_dprep")
    dproj = jnp.concatenate([dq_o, dk_o, dv_o, dfl], axis=1)
    g_in = _mm(h, dproj, "tn", "fox_gin", out_dtype=BF16)
    dx, g_norm = _mm_dnorm_nt(dproj, w_in, "fox_dh", x, norm_w, dy)
    g_q = g_gq[:, :HEAD_DIM] + g_gq[:, HEAD_DIM:]
    g_k = g_gk[:, :HEAD_DIM] + g_gk[:, HEAD_DIM:]
    return dx, g_norm, g_in[:, :FOX_IN], g_bf[:, :ATTN_HEADS], g_q, g_k, g_out, got


def _ssd_conv_fwd(proj, cw, cb, name):
    s = proj.shape[0]
    ts = _row_tile(s)
    w = 1024
    per = ts // 8

    def body(p_ref, h_ref, w_ref, b_ref, o_ref):
        first = pl.program_id(0) == 0
        cur = p_ref[...]
        halo = h_ref[...]
        u = w_ref[3:4, :] * cur + b_ref[...]
        for j in range(1, 4):
            u = u + w_ref[3 - j:4 - j, :] * _prev_rows(cur, halo, j, first)
        o_ref[...] = u * _sigmoid(u)

    return _pc(body, name, (s // ts, 4),
               [pl.BlockSpec((ts, w), lambda i, j: (i, 2 + j)),
                pl.BlockSpec((8, w), lambda i, j: (jnp.maximum(i * per - 1, 0), 2 + j)),
                pl.BlockSpec((4, w), lambda i, j: (0, j)), pl.BlockSpec((1, w), lambda i, j: (0, j))],
               pl.BlockSpec((ts, w), lambda i, j: (i, j)), _sds((s, SSM_CONV_DIM)))(proj, proj, cw, cb)


def _ssd_conv_bwd(proj, dxbc, cw, cb, name):
    s = proj.shape[0]
    ts = _row_tile(s)
    nt = s // ts
    w = 1024
    per = ts // 8
    nblk = s // 8

    def body(p_ref, hp_ref, hn_ref, d_ref, dn_ref, w_ref, b_ref, o_ref, dw_ref, db_ref):
        i = pl.program_id(1)
        first = i == 0
        last = i == nt - 1
        cur = p_ref[...]
        prev = [cur] + [_prev_rows(cur, hp_ref[...], j, first) for j in range(1, 4)]
        nxt = hn_ref[...]
        tail = cur[ts - 8:, :]
        u = b_ref[...]
        un = b_ref[...]
        for j in range(4):
            u = u + w_ref[3 - j:4 - j, :] * prev[j]
            un = un + w_ref[3 - j:4 - j, :] * (nxt if j == 0 else _prev_rows(nxt, tail, j, False))
        sg = _sigmoid(u)
        g = d_ref[...] * (sg * (1.0 + u * (1.0 - sg)))
        sn = _sigmoid(un)
        gn = dn_ref[...] * (sn * (1.0 + un * (1.0 - sn)))

        @pl.when(first)
        def _():
            dw_ref[...] = jnp.zeros_like(dw_ref)
            db_ref[...] = jnp.zeros_like(db_ref)

        dpre = w_ref[3:4, :] * g
        for j in range(1, 4):
            dpre = dpre + w_ref[3 - j:4 - j, :] * _next_rows(g, gn, j, last)
        for j in range(4):
            dw_ref[3 - j:4 - j, :] += jnp.sum(g * prev[j], axis=0, keepdims=True)
        db_ref[...] += jnp.sum(g, axis=0, keepdims=True)
        o_ref[...] = dpre.astype(BF16)

    tile = pl.BlockSpec((ts, w), lambda j, i: (i, j))
    wspec = pl.BlockSpec((4, w), lambda j, i: (0, j))
    vec = pl.BlockSpec((1, w), lambda j, i: (0, j))
    nxt_blk = lambda off: pl.BlockSpec((8, w), lambda j, i: (jnp.minimum((i + 1) * per, nblk - 1), off + j))
    return _pc(body, name, (4, nt),
               [pl.BlockSpec((ts, w), lambda j, i: (i, 2 + j)),
                pl.BlockSpec((8, w), lambda j, i: (jnp.maximum(i * per - 1, 0), 2 + j)), nxt_blk(2),
                tile, nxt_blk(0), wspec, vec],
               [tile, wspec, vec], [_sds((s, SSM_CONV_DIM), BF16), _sds((4, SSM_CONV_DIM)), _sds((1, SSM_CONV_DIM))])(
                   proj, proj, proj, dxbc, dxbc, cw, cb)


def _ssd_dt_fwd(proj, bias, a_neg, name):
    s = proj.shape[0]
    n = s // SSM_CHUNK

    def body(r_ref, b_ref, a_ref, dt_ref, ac_ref):
        real = lax.broadcasted_iota(jnp.int32, (SSM_CHUNK, LANES), 1) < SSM_HEADS
        dt = jnp.where(real, _softplus(r_ref[...] + b_ref[...]), 0.0)
        dt_ref[...] = dt
        ac_ref[...] = _dot(_tri(True), dt * a_ref[...], 1, 0, HI)

    vec = pl.BlockSpec((1, LANES), lambda c: (0, 0))
    row = pl.BlockSpec((SSM_CHUNK, LANES), lambda c: (c, 0))
    return _pc(body, name, (n,), [pl.BlockSpec((SSM_CHUNK, LANES), lambda c: (c, 48)), vec, vec], [row, row],
               [_sds((s, LANES)), _sds((s, LANES))])(proj, bias, a_neg)


def _ssd_dt_bwd(proj, bias, ddt, name):
    s = proj.shape[0]
    ts = _row_tile(s, 512)

    def body(r_ref, b_ref, d_ref, o_ref, db_ref):
        real = lax.broadcasted_iota(jnp.int32, (ts, LANES), 1) < SSM_HEADS
        g = jnp.where(real, d_ref[...] * _sigmoid(r_ref[...] + b_ref[...]), 0.0)
        o_ref[...] = g.astype(BF16)

        @pl.when(pl.program_id(0) == 0)
        def _():
            db_ref[...] = jnp.zeros_like(db_ref)

        db_ref[...] += jnp.sum(g, axis=0, keepdims=True)

    vec = pl.BlockSpec((1, LANES), lambda i: (0, 0))
    row = pl.BlockSpec((ts, LANES), lambda i: (i, 0))
    return _pc(body, name, (s // ts,), [pl.BlockSpec((ts, LANES), lambda i: (i, 48)), vec, row], [row, vec],
               [_sds((s, LANES), BF16), _sds((1, LANES))])(proj, bias, ddt)


def _pair_cols(cols, k0, lo):
    return jnp.where(lo, cols[:, k0:k0 + 1], cols[:, k0 + 1:k0 + 2])


def _last_lane(row):
    lane = lax.broadcasted_iota(jnp.int32, row.shape, 1)
    return jnp.sum(jnp.where(lane == SSM_CHUNK - 1, row, 0.0), axis=-1, keepdims=True)


SSD_FWD_GROUPS = 2
SSD_BWD_GROUPS = 1


def _ssd_specs(nc, rev, n):
    cc = (lambda c: nc - 1 - c) if rev else (lambda c: c)
    nb = SSM_INNER // (LANES * n)
    return dict(
        x=pl.BlockSpec((SSM_CHUNK, 256 * n), lambda g, c: (cc(c), g)),
        b=pl.BlockSpec((SSM_CHUNK, LANES * n), lambda g, c: (cc(c), nb + g)),
        c=pl.BlockSpec((SSM_CHUNK, LANES * n), lambda g, c: (cc(c), nb + SSM_GROUPS // n + g)),
        col=pl.BlockSpec((n, SSM_CHUNK, 4), lambda g, c: (g, cc(c), 0)),
        row=pl.BlockSpec((n, 4, SSM_CHUNK), lambda g, c: (g, 0, cc(c))),
        grp=pl.BlockSpec((n, 1, 256), lambda g, c: (g, 0, 0)),
        grow=pl.BlockSpec((n, 4, LANES), lambda g, c: (g, 0, 0)),
        hs=pl.BlockSpec((1, n, 256, SSM_STATE), lambda g, c: (cc(c), g, 0, 0)),
        bc=pl.BlockSpec((SSM_CHUNK, LANES * n), lambda g, c: (cc(c), g)),
    )


def _ssd_scan_fwd(xbc, dtc, acol, drow, arow, dskip, name):
    s = xbc.shape[0]
    nc = s // SSM_CHUNK
    n = SSD_FWD_GROUPS
    sp = _ssd_specs(nc, False, n)
    L = SSM_CHUNK

    def body(x_ref, b_ref, c_ref, dtc_ref, ac_ref, dr_ref, ar_ref, dk_ref, y_ref, hs_ref, h_ref):
        @pl.when(pl.program_id(1) == 0)
        def _():
            h_ref[...] = jnp.zeros_like(h_ref)

        for gi in range(n):
            group(gi, x_ref, b_ref, c_ref, dtc_ref, ac_ref, dr_ref, ar_ref, dk_ref, y_ref, hs_ref, h_ref)

    def group(gi, x_ref, b_ref, c_ref, dtc_ref, ac_ref, dr_ref, ar_ref, dk_ref, y_ref, hs_ref, h_ref):
        x0 = gi * 256
        bb = b_ref[:, gi * LANES:(gi + 1) * LANES].astype(BF16)
        cb = c_ref[:, gi * LANES:(gi + 1) * LANES].astype(BF16)
        gm = _dot(cb, bb, 1, 1)
        dtc = dtc_ref[gi]
        ac = ac_ref[gi]
        dr = dr_ref[gi]
        ar = ar_ref[gi]
        dsk = dk_ref[gi]
        hs_ref[0, gi] = h_ref[gi]
        tril = lax.broadcasted_iota(jnp.int32, (L, L), 0) >= lax.broadcasted_iota(jnp.int32, (L, L), 1)
        lo = _lo_mask((L, LANES))
        rowlo = lax.broadcasted_iota(jnp.int32, (L, LANES), 0) < HEAD_DIM
        for pr in range(2):
            k0 = 2 * pr
            xp = x_ref[:, x0 + pr * LANES:x0 + (pr + 1) * LANES]
            xpb = xp.astype(BF16)
            hp = h_ref[gi, pr * LANES:(pr + 1) * LANES, :]
            yd, al = [], []
            for k in (k0, k0 + 1):
                seg = ac[:, k:k + 1] - ar[k:k + 1, :]
                wk = gm * jnp.exp(jnp.where(tril, seg, -jnp.inf)) * dr[k:k + 1, :]
                yd.append(_dot(wk.astype(BF16), xpb, 1, 0))
                al.append(_last_lane(ar[k:k + 1, :]))
            e = jnp.exp(_pair_cols(ac, k0, lo))
            yo = _dot(cb, hp.astype(BF16), 1, 1) * e
            y_ref[:, x0 + pr * LANES:x0 + (pr + 1) * LANES] = (
                jnp.where(lo, yd[0], yd[1]) + yo + dsk[:, pr * LANES:(pr + 1) * LANES] * xp)
            wp = jnp.where(lo, jnp.exp(al[0] - ac[:, k0:k0 + 1]) * dtc[:, k0:k0 + 1],
                           jnp.exp(al[1] - ac[:, k0 + 1:k0 + 2]) * dtc[:, k0 + 1:k0 + 2])
            st = _dot((xp * wp).astype(BF16), bb, 0, 0)
            dec = jnp.where(rowlo, jnp.exp(al[0]), jnp.exp(al[1]))
            h_ref[gi, pr * LANES:(pr + 1) * LANES, :] = dec * hp + st

    return _pc(body, name, (SSM_GROUPS // n, nc),
               [sp["x"], sp["b"], sp["c"], sp["col"], sp["col"], sp["row"], sp["row"], sp["grp"]],
               [sp["x"], sp["hs"]], [_sds((s, SSM_INNER)), _sds((nc, SSM_GROUPS, 256, SSM_STATE))],
               [pltpu.VMEM((n, 256, SSM_STATE), F32)])(xbc, xbc, xbc, dtc, acol, drow, arow, dskip)


def _ssd_scan_bwd(xbc, dtc, acol, drow, arow, dskip, agrp, hs, dy, name):
    s = xbc.shape[0]
    nc = s // SSM_CHUNK
    n = SSD_BWD_GROUPS
    sp = _ssd_specs(nc, True, n)
    L = SSM_CHUNK

    def body(x_ref, b_ref, c_ref, dtc_ref, ac_ref, dr_ref, ar_ref, dk_ref, ag_ref, hs_ref, dy_ref,
             dx_ref, db_ref, dc_ref, ddt_ref, da_ref, dd_ref, dh_ref):
        @pl.when(pl.program_id(1) == 0)
        def _():
            dh_ref[...] = jnp.zeros_like(dh_ref)
            da_ref[...] = jnp.zeros_like(da_ref)
            dd_ref[...] = jnp.zeros_like(dd_ref)

        for gi in range(n):
            group(gi, x_ref, b_ref, c_ref, dtc_ref, ac_ref, dr_ref, ar_ref, dk_ref, ag_ref, hs_ref, dy_ref,
                  dx_ref, db_ref, dc_ref, ddt_ref, da_ref, dd_ref, dh_ref)

    def group(gi, x_ref, b_ref, c_ref, dtc_ref, ac_ref, dr_ref, ar_ref, dk_ref, ag_ref, hs_ref, dy_ref,
              dx_ref, db_ref, dc_ref, ddt_ref, da_ref, dd_ref, dh_ref):
        x0 = gi * 256
        bcols = slice(gi * LANES, (gi + 1) * LANES)
        bb = b_ref[:, bcols].astype(BF16)
        cb = c_ref[:, bcols].astype(BF16)
        gm = _dot(cb, bb, 1, 1)
        dtc = dtc_ref[gi]
        ac = ac_ref[gi]
        dr = dr_ref[gi]
        ar = ar_ref[gi]
        dsk = dk_ref[gi]
        ag = ag_ref[gi]
        tril = lax.broadcasted_iota(jnp.int32, (L, L), 0) >= lax.broadcasted_iota(jnp.int32, (L, L), 1)
        lo = _lo_mask((L, LANES))
        nlo = jnp.logical_not(lo)
        rowlo = lax.broadcasted_iota(jnp.int32, (L, LANES), 0) < HEAD_DIM
        lane = lax.broadcasted_iota(jnp.int32, (L, LANES), 1)
        lane_row = lax.broadcasted_iota(jnp.int32, (1, LANES), 1)
        dgm = jnp.zeros((L, L), F32)
        dcm = jnp.zeros((L, SSM_STATE), F32)
        dbm = jnp.zeros((L, SSM_STATE), F32)
        cols = jnp.zeros((L, LANES), F32)
        rows_ddt, rows_q, al_all, dcd_all = [], [], [], []
        for pr in range(2):
            k0 = 2 * pr
            xcols = slice(x0 + pr * LANES, x0 + (pr + 1) * LANES)
            xp = x_ref[:, xcols]
            xpb = xp.astype(BF16)
            dyp = dy_ref[:, xcols]
            dypb = dyp.astype(BF16)
            zero = jnp.zeros_like(dypb)
            hp = hs_ref[0, gi, pr * LANES:(pr + 1) * LANES, :]
            hpb = hp.astype(BF16)
            dst = dh_ref[gi, pr * LANES:(pr + 1) * LANES, :]
            dstb = dst.astype(BF16)
            dxd, al = [], []
            for k in (k0, k0 + 1):
                sel = lo if k == k0 else nlo
                seg = ac[:, k:k + 1] - ar[k:k + 1, :]
                lam = jnp.exp(jnp.where(tril, seg, -jnp.inf))
                wk = gm * lam * dr[k:k + 1, :]
                dwk = _dot(jnp.where(sel, dypb, zero), xpb, 1, 1)
                mk = dwk * gm * lam
                qk = mk * dr[k:k + 1, :]
                dgm = dgm + dwk * lam * dr[k:k + 1, :]
                rows_ddt.append(jnp.sum(mk, axis=0, keepdims=True))
                rows_q.append(jnp.sum(qk, axis=0, keepdims=True))
                cols = jnp.where(lane == k, jnp.sum(qk, axis=-1, keepdims=True), cols)
                dxd.append(_dot(wk.astype(BF16), dypb, 0, 0))
                al.append(_last_lane(ar[k:k + 1, :]))
            al_all += al
            dxp = jnp.where(lo, dxd[0], dxd[1])
            e = jnp.exp(_pair_cols(ac, k0, lo))
            dye = dyp * e
            dyeb = dye.astype(BF16)
            dcm = dcm + _dot(dyeb, hpb, 1, 0)
            dh_yoff = _dot(dyeb, cb, 0, 0)
            tq = dye * _dot(cb, hpb, 1, 1)
            cols = jnp.where(lane == 4 + k0, jnp.sum(jnp.where(lo, tq, 0.0), axis=-1, keepdims=True), cols)
            cols = jnp.where(lane == 5 + k0, jnp.sum(jnp.where(lo, 0.0, tq), axis=-1, keepdims=True), cols)
            wp = jnp.where(lo, jnp.exp(al[0] - ac[:, k0:k0 + 1]) * dtc[:, k0:k0 + 1],
                           jnp.exp(al[1] - ac[:, k0 + 1:k0 + 2]) * dtc[:, k0 + 1:k0 + 2])
            dxw = _dot(bb, dstb, 1, 1)
            dxp = dxp + dxw * wp
            tw = xp * dxw
            cols = jnp.where(lane == 8 + k0, jnp.sum(jnp.where(lo, tw, 0.0), axis=-1, keepdims=True), cols)
            cols = jnp.where(lane == 9 + k0, jnp.sum(jnp.where(lo, 0.0, tw), axis=-1, keepdims=True), cols)
            dbm = dbm + _dot((xp * wp).astype(BF16), dstb, 1, 0)
            dsl = dsk[:, pr * LANES:(pr + 1) * LANES]
            dx_ref[:, xcols] = dxp + dsl * dyp
            dd_ref[gi, :, pr * LANES:(pr + 1) * LANES] += jnp.sum(dyp * xp, axis=0, keepdims=True)
            prod = dst * hp
            dcd_all.append(jnp.sum(jnp.sum(jnp.where(rowlo, prod, 0.0), axis=-1, keepdims=True), axis=0, keepdims=True))
            dcd_all.append(jnp.sum(jnp.sum(jnp.where(rowlo, 0.0, prod), axis=-1, keepdims=True), axis=0, keepdims=True))
            dec = jnp.where(rowlo, jnp.exp(al[0]), jnp.exp(al[1]))
            dh_ref[gi, pr * LANES:(pr + 1) * LANES, :] = dec * dst + dh_yoff
        dgb = dgm.astype(BF16)
        dc_ref[:, bcols] = dcm + _dot(dgb, bb, 1, 0)
        db_ref[:, bcols] = dbm + _dot(dgb, cb, 0, 0)
        colt = cols.T
        sub8 = lax.broadcasted_iota(jnp.int32, (8, LANES), 0)
        da_rows = jnp.zeros((8, LANES), F32)
        ddt_part = []
        for k in range(4):
            rs = colt[k:k + 1, :]
            uo = colt[4 + k:5 + k, :]
            dwl = colt[8 + k:9 + k, :]
            es = jnp.exp(al_all[k] - ar[k:k + 1, :])
            wrow = es * dr[k:k + 1, :]
            dwl_w = dwl * wrow
            da_k = rs - rows_q[k] + uo - dwl_w
            tail = jnp.sum(dwl_w, axis=-1, keepdims=True) + jnp.exp(al_all[k]) * dcd_all[k]
            da_k = da_k + jnp.where(lane_row == L - 1, tail, 0.0)
            da_rows = jnp.where(sub8 == k, da_k, da_rows)
            ddt_part.append(rows_ddt[k] + dwl * es)
        dda = _dot(da_rows, _tri(True), 1, 0, HI)
        for k in range(4):
            dda_k = dda[k:k + 1, :]
            ddt_ref[gi, k:k + 1, :] = ddt_part[k] + dda_k * ag[k:k + 1, :]
            da_ref[gi, k:k + 1, :] += dda_k * dr[k:k + 1, :] * ag[k:k + 1, :]

    return _pc(body, name, (SSM_GROUPS // n, nc),
               [sp["x"], sp["b"], sp["c"], sp["col"], sp["col"], sp["row"], sp["row"], sp["grp"], sp["grow"], sp["hs"], sp["x"]],
               [sp["x"], sp["bc"], sp["bc"], sp["row"], sp["grow"], sp["grp"]],
               [_sds((s, SSM_INNER)), _sds((s, 1024)), _sds((s, 1024)), _sds((SSM_GROUPS, 4, s)),
                _sds((SSM_GROUPS, 4, LANES)), _sds((SSM_GROUPS, 1, 256))],
               [pltpu.VMEM((n, 256, SSM_STATE), F32)])(xbc, xbc, xbc, dtc, acol, drow, arow, dskip, agrp, hs, dy)


def _gnorm_fwd(y, proj, nw, name):
    s = y.shape[0]
    ts = _row_tile(s)
    gw = SSM_INNER // SSM_GROUPS

    def body(y_ref, z_ref, w_ref, o_ref):
        for g in range(SSM_GROUPS):
            sl = slice(g * gw, (g + 1) * gw)
            z = z_ref[:, sl]
            y2 = y_ref[:, sl] * (z * _sigmoid(z))
            r = lax.rsqrt(jnp.mean(y2 * y2, axis=-1, keepdims=True) + RMS_EPS)
            o_ref[:, sl] = ((y2 * r) * w_ref[:, sl]).astype(BF16)

    row = pl.BlockSpec((ts, SSM_INNER), lambda i: (i, 0))
    return _pc(body, name, (s // ts,), [row, row, pl.BlockSpec((1, SSM_INNER), lambda i: (0, 0))], row,
               _sds((s, SSM_INNER), BF16))(y, proj, nw)


def _gnorm_bwd(y, proj, nw, dyn, name):
    s = y.shape[0]
    ts = _row_tile(s)
    gw = SSM_INNER // SSM_GROUPS

    def body(y_ref, z_ref, w_ref, d_ref, dy_ref, dz_ref, dw_ref):
        @pl.when(pl.program_id(0) == 0)
        def _():
            dw_ref[...] = jnp.zeros_like(dw_ref)

        for g in range(SSM_GROUPS):
            sl = slice(g * gw, (g + 1) * gw)
            z = z_ref[:, sl]
            yv = y_ref[:, sl]
            sg = _sigmoid(z)
            sz = z * sg
            y2 = yv * sz
            r = lax.rsqrt(jnp.mean(y2 * y2, axis=-1, keepdims=True) + RMS_EPS)
            yn = y2 * r
            dout = d_ref[:, sl]
            dyg = dout * w_ref[:, sl]
            dy2 = r * (dyg - yn * jnp.mean(dyg * yn, axis=-1, keepdims=True))
            dy_ref[:, sl] = dy2 * sz
            dz_ref[:, sl] = (dy2 * yv * (sg * (1.0 + z * (1.0 - sg)))).astype(BF16)
            dw_ref[:, sl] += jnp.sum(dout * yn, axis=0, keepdims=True)

    row = pl.BlockSpec((ts, SSM_INNER), lambda i: (i, 0))
    vec = pl.BlockSpec((1, SSM_INNER), lambda i: (0, 0))
    return _pc(body, name, (s // ts,), [row, row, vec, row], [row, row, vec],
               [_sds((s, SSM_INNER)), _sds((s, SSM_INNER), BF16), _sds((1, SSM_INNER))])(y, proj, nw, dyn)


def _head_layouts(v, s):
    return v.reshape(s, SSM_GROUPS, 4).transpose(1, 0, 2), v.T.reshape(SSM_GROUPS, 4, s)


def _ssd_fwd(x, norm_w, w_in, conv_w, conv_b, dt_bias, a_log, d_skip, gnorm_w, w_out):
    s = x.shape[0]
    h = _rms_fwd(x, norm_w, "ssd_norm")
    proj = _mm(h, w_in, "nn", "ssd_in")
    xbc = _ssd_conv_fwd(proj, conv_w, conv_b, "ssd_conv")
    pad = ((0, 0), (0, LANES - SSM_HEADS))
    a_neg = -jnp.exp(a_log)
    bias = jnp.pad(dt_bias, pad)
    dt, acum = _ssd_dt_fwd(proj, bias, jnp.pad(a_neg, pad), "ssd_dt")
    dtc, drow = _head_layouts(dt[:, :SSM_HEADS], s)
    acol, arow = _head_layouts(acum[:, :SSM_HEADS], s)
    dskip = jnp.repeat(d_skip.reshape(SSM_GROUPS, 1, 4), HEAD_DIM, axis=2)
    y, hs = _ssd_scan_fwd(xbc, dtc, acol, drow, arow, dskip, "ssd_scan")
    yn = _gnorm_fwd(y, proj, gnorm_w, "ssd_gnorm")
    out = _mm(yn, w_out, "nn", "ssd_out", res=x)
    return out, (x, h, proj, xbc, bias, a_neg, dtc, acol, drow, arow, dskip, y, hs, yn)


def _ssd_bwd(dout, saved, norm_w, w_in, conv_w, conv_b, gnorm_w, w_out):
    x, h, proj, xbc, bias, a_neg, dtc, acol, drow, arow, dskip, y, hs, yn = saved
    s = x.shape[0]
    dyn = _mm(dout, w_out, "nt", "ssd_dyn")
    g_out = _mm(yn, dout, "tn", "ssd_gout", out_dtype=BF16)
    dy, dz, g_gnorm = _gnorm_bwd(y, proj, gnorm_w, dyn, "ssd_dgnorm")
    agrp = jnp.broadcast_to(a_neg.reshape(SSM_GROUPS, 4, 1), (SSM_GROUPS, 4, LANES))
    dxs, db, dc, ddt_row, da_acc, dd_acc = _ssd_scan_bwd(xbc, dtc, acol, drow, arow, dskip, agrp, hs, dy, "ssd_dscan")
    dxbc = jnp.concatenate([dxs, db, dc], axis=1)
    dpre, g_cw, g_cb = _ssd_conv_bwd(proj, dxbc, conv_w, conv_b, "ssd_dconv")
    ddt = jnp.pad(ddt_row.reshape(SSM_HEADS, s).T, ((0, 0), (0, LANES - SSM_HEADS)))
    ddtraw, g_dtb = _ssd_dt_bwd(proj, bias, ddt, "ssd_ddt")
    dproj = jnp.concatenate([dz, dpre, ddtraw], axis=1)
    g_in = _mm(h, dproj, "tn", "ssd_gin", out_dtype=BF16)
    dx, g_norm = _mm_dnorm_nt(dproj, w_in, "ssd_dh", x, norm_w, dout)
    g_alog = jnp.sum(da_acc, axis=-1).reshape(1, SSM_HEADS)
    g_d = jnp.sum(dd_acc.reshape(SSM_GROUPS, 4, HEAD_DIM), axis=-1).reshape(1, SSM_HEADS)
    return dx, g_norm, g_in[:, :SSM_IN], g_cw, g_cb, g_dtb[:, :SSM_HEADS], g_alog, g_d, g_gnorm, g_out


def _loss_head(y, target, name):
    s, d = y.shape
    ts = _row_tile(s)

    def body(y_ref, t_ref, dy_ref, l_ref):
        @pl.when(pl.program_id(0) == 0)
        def _():
            l_ref[...] = jnp.zeros_like(l_ref)

        e = y_ref[...] - t_ref[...]
        dy_ref[...] = e * (1.0 / d)
        part = jnp.sum(jnp.sum(e * e, axis=-1, keepdims=True), axis=0, keepdims=True) * (0.5 / d)
        l_ref[...] += jnp.broadcast_to(part, l_ref.shape)

    row = pl.BlockSpec((ts, d), lambda i: (i, 0))
    dy, lacc = _pc(body, name, (s // ts,), [row, row], [row, pl.BlockSpec((8, LANES), lambda i: (0, 0))],
                   [_sds((s, d)), _sds((8, LANES))])(y, target)
    return lacc[0, 0], dy


def _local_step(x, target, w, gather_rest=None, scatter_first=None):
    saved = []
    received = None
    for i in range(DEPTH):
        kind, j = i % 3, i // 3
        mn = w["mix_norm"][i:i + 1]
        if kind == 0:
            x, sv = _conv_fwd(x, mn, w["conv_w_in"][j], w["conv_w_dw"][j], w["conv_w_out"][j], str(i))
        elif kind == 1:
            hosted = None if gather_rest is None else (gather_rest[0], True)
            x, sv, got = _fox_fwd(x, mn, w["fox_w_in"], w["fox_b_f"], w["fox_q_gain"], w["fox_k_gain"], w["fox_w_out"], hosted)
            if gather_rest is not None:
                w = gather_rest[1](w, got)
        else:
            x, sv = _ssd_fwd(x, mn, w["ssd_w_in"], w["ssd_conv_w"], w["ssd_conv_b"], w["ssd_dt_bias"],
                             w["ssd_a_log"], w["ssd_d"], w["ssd_norm_w"], w["ssd_w_out"])
        x, sf = _ffn_fwd(x, w["ffn_norm"][i:i + 1], w["ffn_w_gu"][i], w["ffn_w_down"][i], str(i))
        saved.append((sv, sf))
    loss, dx = _loss_head(x, target, "loss_head")
    g = {k: [None] * n for k, n in (("mix_norm", DEPTH), ("ffn_norm", DEPTH), ("ffn_w_gu", DEPTH), ("ffn_w_down", DEPTH),
                                    ("conv_w_in", 2), ("conv_w_dw", 2), ("conv_w_out", 2))}
    for i in reversed(range(DEPTH)):
        kind, j = i % 3, i // 3
        sv, sf = saved[i]
        dx, g["ffn_norm"][i], g["ffn_w_gu"][i], g["ffn_w_down"][i] = _ffn_bwd(
            dx, sf, w["ffn_norm"][i:i + 1], w["ffn_w_gu"][i], w["ffn_w_down"][i], str(i))
        mn = w["mix_norm"][i:i + 1]
        if kind == 0:
            dx, g["mix_norm"][i], g["conv_w_in"][j], g["conv_w_dw"][j], g["conv_w_out"][j] = _conv_bwd(
                dx, sv, mn, w["conv_w_in"][j], w["conv_w_dw"][j], w["conv_w_out"][j], str(i))
        elif kind == 1:
            hosted = None if scatter_first is None else (scatter_first(g), False)
            (dx, g["mix_norm"][i], g["fox_w_in"], g["fox_b_f"], g["fox_q_gain"], g["fox_k_gain"],
             g["fox_w_out"], received) = _fox_bwd(dx, sv, mn, w["fox_w_in"], w["fox_w_out"], hosted)
        else:
            (dx, g["mix_norm"][i], g["ssd_w_in"], g["ssd_conv_w"], g["ssd_conv_b"], g["ssd_dt_bias"], g["ssd_a_log"],
             g["ssd_d"], g["ssd_norm_w"], g["ssd_w_out"]) = _ssd_bwd(
                 dx, sv, mn, w["ssd_w_in"], w["ssd_conv_w"], w["ssd_conv_b"], w["ssd_norm_w"], w["ssd_w_out"])
    g["mix_norm"] = jnp.concatenate(g["mix_norm"], axis=0)
    g["ffn_norm"] = jnp.concatenate(g["ffn_norm"], axis=0)
    g["conv_w_dw"] = jnp.stack(g["conv_w_dw"], axis=0)
    g["ssd_conv_w"] = g["ssd_conv_w"][None]
    return loss, dx, g, received


def _mesh_position():
    return lax.axis_index("x") * 4 + lax.axis_index("y") * 2 + lax.axis_index("c")


def _device_of(t):
    return (lax.shift_right_logical(t, 2), lax.bitwise_and(lax.shift_right_logical(t, 1), 1), lax.bitwise_and(t, 1))


def _exchange_copies(srcs_of, out_refs, send_sems, recv_sems, local_sems):
    me = _mesh_position()
    na = len(out_refs)
    locals_ = [pltpu.make_async_copy(srcs_of(a, me), out_refs[a].at[me], local_sems.at[a]) for a in range(na)]
    sends, arrivals = [], []
    for j in range(1, NDEV):
        t = lax.rem(me + j, NDEV)
        frm = lax.rem(me + NDEV - j, NDEV)
        for a in range(na):
            sends.append(pltpu.make_async_remote_copy(
                src_ref=srcs_of(a, t), dst_ref=out_refs[a].at[me], send_sem=send_sems.at[a, j - 1],
                recv_sem=recv_sems.at[a, j - 1], device_id=_device_of(t), device_id_type=pl.DeviceIdType.MESH))
            arrivals.append(pltpu.make_async_remote_copy(
                src_ref=srcs_of(a, me), dst_ref=out_refs[a].at[frm], send_sem=send_sems.at[a, j - 1],
                recv_sem=recv_sems.at[a, j - 1], device_id=_device_of(frm), device_id_type=pl.DeviceIdType.MESH))
    return locals_, sends, arrivals


def _exchange_start(copies):
    locals_, sends, _ = copies
    for cp in locals_ + sends:
        cp.start()


def _exchange_wait(copies):
    locals_, sends, arrivals = copies
    for cp in sends:
        cp.wait_send()
    for cp in arrivals:
        cp.wait_recv()
    for cp in locals_:
        cp.wait()


def _exchange_run(srcs_of, out_refs, send_sems, recv_sems, local_sems):
    copies = _exchange_copies(srcs_of, out_refs, send_sems, recv_sems, local_sems)
    _exchange_start(copies)
    _exchange_wait(copies)


def _exchange_parts(arrays, gather):
    na = len(arrays)
    outs = [_sds(((NDEV,) + a.shape) if gather else a.shape, a.dtype) for a in arrays]
    sems = [pltpu.SemaphoreType.DMA((na, NDEV - 1)), pltpu.SemaphoreType.DMA((na, NDEV - 1)), pltpu.SemaphoreType.DMA((na,))]
    pick = (lambda srcs: (lambda a, t: srcs[a])) if gather else (lambda srcs: (lambda a, t: srcs[a].at[t]))
    return pick, outs, sems


def _exchange(arrays, name, gather):
    na = len(arrays)
    pick, outs, sems = _exchange_parts(arrays, gather)

    def body(*refs):
        _exchange_run(pick(refs[:na]), refs[na:2 * na], *refs[2 * na:])

    hbm = pl.BlockSpec(memory_space=pl.ANY)
    return pl.pallas_call(body, name=name, in_specs=[hbm] * na, out_specs=[hbm] * na, out_shape=outs, scratch_shapes=sems)(*arrays)


def _all_sum_small(pack, name):
    def body(src_ref, out_ref, buf_ref, send_sems, recv_sems, local_sems):
        _exchange_run(lambda a, t: src_ref, [buf_ref], send_sems, recv_sems, local_sems)
        acc = buf_ref[0]
        for d in range(1, NDEV):
            acc = acc + buf_ref[d]
        out_ref[...] = acc

    vmem = pl.BlockSpec(memory_space=pltpu.VMEM)
    return pl.pallas_call(
        body, name=name, in_specs=[vmem], out_specs=vmem, out_shape=_sds(pack.shape, pack.dtype),
        scratch_shapes=[pltpu.VMEM((NDEV,) + pack.shape, pack.dtype), pltpu.SemaphoreType.DMA((1, NDEV - 1)),
                        pltpu.SemaphoreType.DMA((1, NDEV - 1)), pltpu.SemaphoreType.DMA((1,))])(pack)


def _sum_slabs(slabs, name):
    _, r, c = slabs.shape
    tr = r
    for cand in (256, 352):
        if r % cand == 0:
            tr = cand
            break

    def body(s_ref, o_ref):
        acc = s_ref[0].astype(F32)
        for d in range(1, NDEV):
            acc = acc + s_ref[d].astype(F32)
        o_ref[...] = acc

    return _pc(body, name, (r // tr,), [pl.BlockSpec((NDEV, tr, c), lambda i: (0, i, 0))],
               pl.BlockSpec((tr, c), lambda i: (i, 0)), _sds((r, c)))(slabs)


def _adamw(wt, g, m, v, name):
    shape = wt.shape
    w2, g2, m2, v2 = (a.reshape(-1, shape[-1]) for a in (wt, g, m, v))
    r, c = w2.shape
    tr = r
    for cand in (512, 352, 256):
        if r % cand == 0:
            tr = cand
            break
    c1 = 1.0 - ADAM_B1 ** ADAM_STEP
    c2 = 1.0 - ADAM_B2 ** ADAM_STEP

    def body(w_ref, g_ref, m_ref, v_ref, d_ref, mo_ref, vo_ref):
        gv = g_ref[...]
        mn = ADAM_B1 * m_ref[...] + (1.0 - ADAM_B1) * gv
        vn = ADAM_B2 * v_ref[...] + (1.0 - ADAM_B2) * (gv * gv)
        mo_ref[...] = mn
        vo_ref[...] = vn
        d_ref[...] = -ADAM_LR * ((mn / c1) / (jnp.sqrt(vn / c2) + ADAM_EPS) + ADAM_WD * w_ref[...])

    spec = pl.BlockSpec((tr, c), lambda i: (i, 0))
    outs = _pc(body, name, (r // tr,), [spec] * 4, [spec] * 3, [_sds((r, c))] * 3)(w2, g2, m2, v2)
    return tuple(o.reshape(shape) for o in outs)


_NAMES = ["mix_norm", "ffn_norm", "ffn_w_gu", "ffn_w_down", "conv_w_in", "conv_w_dw", "conv_w_out", "fox_w_in", "fox_b_f",
          "fox_q_gain", "fox_k_gain", "fox_w_out", "ssd_w_in", "ssd_conv_w", "ssd_conv_b", "ssd_dt_bias", "ssd_a_log",
          "ssd_d", "ssd_norm_w", "ssd_w_out"]
_MATRICES = ["ffn_w_gu", "ffn_w_down", "conv_w_in", "conv_w_out", "fox_w_in", "fox_w_out", "ssd_w_in", "ssd_w_out"]
_VECTORS = {"conv_w_dw": 2, "ssd_conv_w": 2, "ssd_conv_b": 1, "ssd_norm_w": 1}
_REPLICATED = ["mix_norm", "ffn_norm", "fox_b_f", "fox_q_gain", "fox_k_gain", "ssd_dt_bias", "ssd_a_log", "ssd_d"]


def _to_rows(flat):
    n = flat.shape[0]
    rows = -(-n // (8 * D_MODEL)) * 8
    return jnp.pad(flat, (0, rows * D_MODEL - n)).reshape(rows, D_MODEL)


def _full_shape(local_shape, axis):
    shp = list(local_shape)
    shp[axis] *= NDEV
    return tuple(shp)


def _cols_from_blocks(g):
    return jnp.moveaxis(g, 0, 1).reshape(g.shape[1], NDEV * g.shape[2])


def _blocks_from_cols(full):
    k, n8 = full.shape
    return jnp.moveaxis(full.reshape(k, NDEV, n8 // NDEV), 1, 0)


def kernel(x, mix_norm, ffn_norm, ffn_w_gu, ffn_w_down, conv_w_in, conv_w_dw, conv_w_out, fox_w_in, fox_b_f, fox_q_gain, fox_k_gain, fox_w_out, ssd_w_in, ssd_conv_w, ssd_conv_b, ssd_dt_bias, ssd_a_log, ssd_d, ssd_norm_w, ssd_w_out, loss_target, m_mix_norm, m_ffn_norm, m_ffn_w_gu, m_ffn_w_down, m_conv_w_in, m_conv_w_dw, m_conv_w_out, m_fox_w_in, m_fox_b_f, m_fox_q_gain, m_fox_k_gain, m_fox_w_out, m_ssd_w_in, m_ssd_conv_w, m_ssd_conv_b, m_ssd_dt_bias, m_ssd_a_log, m_ssd_d, m_ssd_norm_w, m_ssd_w_out, v_mix_norm, v_ffn_norm, v_ffn_w_gu, v_ffn_w_down, v_conv_w_in, v_conv_w_dw, v_conv_w_out, v_fox_w_in, v_fox_b_f, v_fox_q_gain, v_fox_k_gain, v_fox_w_out, v_ssd_w_in, v_ssd_conv_w, v_ssd_conv_b, v_ssd_dt_bias, v_ssd_a_log, v_ssd_d, v_ssd_norm_w, v_ssd_w_out):
    local = dict(mix_norm=mix_norm, ffn_norm=ffn_norm, ffn_w_gu=ffn_w_gu, ffn_w_down=ffn_w_down, conv_w_in=conv_w_in,
                 conv_w_dw=conv_w_dw, conv_w_out=conv_w_out, fox_w_in=fox_w_in, fox_b_f=fox_b_f, fox_q_gain=fox_q_gain,
                 fox_k_gain=fox_k_gain, fox_w_out=fox_w_out, ssd_w_in=ssd_w_in, ssd_conv_w=ssd_conv_w, ssd_conv_b=ssd_conv_b,
                 ssd_dt_bias=ssd_dt_bias, ssd_a_log=ssd_a_log, ssd_d=ssd_d, ssd_norm_w=ssd_norm_w, ssd_w_out=ssd_w_out)
    mom = dict(zip(_NAMES, [m_mix_norm, m_ffn_norm, m_ffn_w_gu, m_ffn_w_down, m_conv_w_in, m_conv_w_dw, m_conv_w_out, m_fox_w_in,
                            m_fox_b_f, m_fox_q_gain, m_fox_k_gain, m_fox_w_out, m_ssd_w_in, m_ssd_conv_w, m_ssd_conv_b,
                            m_ssd_dt_bias, m_ssd_a_log, m_ssd_d, m_ssd_norm_w, m_ssd_w_out]))
    var = dict(zip(_NAMES, [v_mix_norm, v_ffn_norm, v_ffn_w_gu, v_ffn_w_down, v_conv_w_in, v_conv_w_dw, v_conv_w_out, v_fox_w_in,
                            v_fox_b_f, v_fox_q_gain, v_fox_k_gain, v_fox_w_out, v_ssd_w_in, v_ssd_conv_w, v_ssd_conv_b,
                            v_ssd_dt_bias, v_ssd_a_log, v_ssd_d, v_ssd_norm_w, v_ssd_w_out]))

    shard = {k: local[k].astype(BF16) for k in _MATRICES}
    vec_pack = _to_rows(jnp.concatenate([local[k].reshape(-1) for k in _VECTORS]))
    first = _exchange([shard["ffn_w_gu"][0:1], shard["ffn_w_down"][0:1], shard["conv_w_in"][0:1], shard["conv_w_out"][0:1],
                       shard["fox_w_in"], shard["fox_w_out"], vec_pack], "gather_first", True)
    gvec = first[6].reshape(NDEV, -1)
    full = {k: local[k] for k in _REPLICATED}
    off = 0
    for k, axis in _VECTORS.items():
        n = local[k].size
        blk = jnp.moveaxis(gvec[:, off:off + n].reshape((NDEV,) + local[k].shape), 0, axis)
        full[k] = blk.reshape(_full_shape(local[k].shape, axis))
        off += n
    full["ssd_conv_w"] = full["ssd_conv_w"][0]
    full["ffn_w_gu"] = [first[0][:, 0]]
    full["ffn_w_down"] = [first[1][:, 0].reshape(4, FF_BLOCK, D_MODEL)]
    full["conv_w_in"] = [first[2][:, 0]]
    full["conv_w_out"] = [first[3][:, 0].reshape(D_MODEL, D_MODEL)]
    full["fox_w_in"] = jnp.pad(_cols_from_blocks(first[4][:, 0]), ((0, 0), (0, FOX_IN_PAD - FOX_IN)))
    full["fox_w_out"] = first[5].reshape(D_MODEL, D_MODEL)

    rest = [shard["ffn_w_gu"][1:], shard["ffn_w_down"][1:], shard["conv_w_in"][1:], shard["conv_w_out"][1:],
            shard["ssd_w_in"], shard["ssd_w_out"]]

    def finish(w, got):
        w = dict(w)
        w["ffn_w_gu"] = w["ffn_w_gu"] + [got[0][:, i] for i in range(DEPTH - 1)]
        w["ffn_w_down"] = w["ffn_w_down"] + [got[1][:, i].reshape(4, FF_BLOCK, D_MODEL) for i in range(DEPTH - 1)]
        w["conv_w_in"] = w["conv_w_in"] + [got[2][:, 0]]
        w["conv_w_out"] = w["conv_w_out"] + [got[3][:, 0].reshape(D_MODEL, D_MODEL)]
        w["ssd_w_in"] = jnp.pad(_cols_from_blocks(got[4][:, 0]), ((0, 0), (0, SSM_IN_PAD - SSM_IN)))
        w["ssd_w_out"] = got[5].reshape(SSM_INNER, D_MODEL)
        return w

    def early_slabs(g):
        return ([g["ffn_w_gu"][i] for i in range(1, DEPTH)]
                + [g["ffn_w_down"][i].reshape(NDEV, D_FF // NDEV, D_MODEL) for i in range(1, DEPTH)]
                + [g["conv_w_in"][1], g["conv_w_out"][1].reshape(NDEV, D_MODEL // NDEV, D_MODEL),
                   _blocks_from_cols(g["ssd_w_in"]), g["ssd_w_out"].reshape(NDEV, SSM_INNER // NDEV, D_MODEL)])

    loss_part, dx, grads, early = _local_step(x[0], loss_target[0], full, (rest, finish), early_slabs)

    late = _exchange([grads["ffn_w_gu"][0], grads["ffn_w_down"][0].reshape(NDEV, D_FF // NDEV, D_MODEL), grads["conv_w_in"][0],
                      grads["conv_w_out"][0].reshape(NDEV, D_MODEL // NDEV, D_MODEL), _blocks_from_cols(grads["fox_w_in"]),
                      grads["fox_w_out"].reshape(NDEV, D_MODEL // NDEV, D_MODEL)], "scatter_last", False)
    se = [_sum_slabs(r, f"sum_early_{n}") for n, r in enumerate(early)]
    sl = [_sum_slabs(r, f"sum_late_{n}") for n, r in enumerate(late)]
    shard_grad = {
        "ffn_w_gu": jnp.stack([sl[0]] + se[0:3]), "ffn_w_down": jnp.stack([sl[1]] + se[3:6]),
        "conv_w_in": jnp.stack([sl[2], se[6]]), "conv_w_out": jnp.stack([sl[3], se[7]]),
        "fox_w_in": sl[4][None], "fox_w_out": sl[5][None], "ssd_w_in": se[8][None], "ssd_w_out": se[9][None]}

    small_names = _REPLICATED + list(_VECTORS)
    small = [jnp.reshape(loss_part, (1,))] + [grads[k].reshape(-1) for k in small_names]
    total = _all_sum_small(_to_rows(jnp.concatenate(small)), "sum_small").reshape(-1)
    loss = total[0]
    off = 1
    me = _mesh_position()
    for k, part in zip(small_names, small[1:]):
        gk = total[off:off + part.shape[0]]
        off += part.shape[0]
        if k in _VECTORS:
            axis = _VECTORS[k]
            shp = local[k].shape
            gfull = gk.reshape(shp[:axis] + (NDEV, shp[axis]) + shp[axis + 1:])
            shard_grad[k] = lax.dynamic_index_in_dim(gfull, me, axis, keepdims=False)
        else:
            shard_grad[k] = gk.reshape(local[k].shape)

    deltas, new_m, new_v = {}, {}, {}
    for k in _NAMES:
        deltas[k], new_m[k], new_v[k] = _adamw(local[k], shard_grad[k], mom[k], var[k], f"adamw_{k}")
    return (loss, dx[None], *[shard_grad[k] for k in _NAMES], *[deltas[k] for k in _NAMES],
            *[new_m[k] for k in _NAMES], *[new_v[k] for k in _NAMES])
```

```python
import numpy as np

import jax
import jax.numpy as jnp
from jax import lax
from jax.experimental import pallas as pl
from jax.experimental.pallas import tpu as pltpu

F32 = jnp.float32
BF16 = jnp.bfloat16
HI = lax.Precision.HIGHEST

NDEV = 8
D_MODEL = 1024
DEPTH = 4
D_FF = 2816
FF_BLOCK = 2 * D_FF // NDEV
RMS_EPS = 1e-6
HEAD_DIM = 64
ATTN_HEADS = 16
FOX_IN = 3 * D_MODEL + ATTN_HEADS
FOX_IN_PAD = 3200
SSM_INNER = 2048
SSM_HEADS = 32
SSM_GROUPS = 8
SSM_STATE = 128
SSM_CHUNK = 128
SSM_CONV_DIM = 4096
SSM_IN = SSM_INNER + SSM_CONV_DIM + SSM_HEADS
SSM_IN_PAD = 6272
LANES = 128
V7X_VMEM_BYTES = 64 * 1024 * 1024
VMEM_LIMIT_BYTES = (V7X_VMEM_BYTES * 3) // 4
ATTN_BWD_VMEM_BYTES = (V7X_VMEM_BYTES * 7) // 8
LOG2E = 1.4426950408889634
LN2 = 0.6931471805599453
ATTN_TILE = 1024
ATTN_ROWS = 32

ADAM_LR = 0.001
ADAM_B1 = 0.9
ADAM_B2 = 0.999
ADAM_EPS = 1e-08
ADAM_WD = 0.01
ADAM_STEP = 10

_TILE_CANDIDATES = (1024, 1408, 896, 768, 640, 512, 384, 256, 128)


def _pick_tile(n):
    for c in _TILE_CANDIDATES:
        if n % c == 0:
            return c
    raise ValueError(f"no tile for {n}")


def _params(ngrid):
    return pltpu.CompilerParams(dimension_semantics=("arbitrary",) * ngrid, vmem_limit_bytes=VMEM_LIMIT_BYTES)


def _pc(body, name, grid, in_specs, out_specs, out_shape, scratch=(), hosted=None):
    if hosted is None:
        return pl.pallas_call(
            body, name=name, grid=grid, in_specs=in_specs, out_specs=out_specs, out_shape=out_shape,
            scratch_shapes=list(scratch), compiler_params=_params(len(grid)))
    arrays, gather = hosted
    single = not isinstance(out_shape, (list, tuple))
    outs = [out_shape] if single else list(out_shape)
    ospecs = [out_specs] if single else list(out_specs)
    na, n_in, n_out, n_scr = len(arrays), len(in_specs), len(outs), len(scratch)
    pick, xouts, sems = _exchange_parts(arrays, gather)

    def run(*refs):
        ins, srcs = refs[:n_in], refs[n_in:n_in + na]
        res, dsts = refs[n_in + na:n_in + na + n_out], refs[n_in + na + n_out:n_in + 2 * na + n_out]
        scr, xsems = refs[n_in + 2 * na + n_out:n_in + 2 * na + n_out + n_scr], refs[n_in + 2 * na + n_out + n_scr:]
        first = pl.program_id(0) == 0
        last = pl.program_id(0) == grid[0] - 1
        for d in range(1, len(grid)):
            first = jnp.logical_and(first, pl.program_id(d) == 0)
            last = jnp.logical_and(last, pl.program_id(d) == grid[d] - 1)

        @pl.when(first)
        def _():
            _exchange_start(_exchange_copies(pick(srcs), dsts, *xsems))

        body(*ins, *res, *scr)

        @pl.when(last)
        def _():
            _exchange_wait(_exchange_copies(pick(srcs), dsts, *xsems))

    hbm = pl.BlockSpec(memory_space=pl.ANY)
    call = pl.pallas_call(
        run, name=name, grid=grid, in_specs=list(in_specs) + [hbm] * na, out_specs=ospecs + [hbm] * na,
        out_shape=outs + xouts, scratch_shapes=list(scratch) + sems, compiler_params=_params(len(grid)))
    return lambda *args: call(*args, *arrays)


def _dot(a, b, ca, cb, prec=None):
    return lax.dot_general(a, b, (((ca,), (cb,)), ((), ())), preferred_element_type=F32, precision=prec)


def _sds(shape, dtype=F32):
    return jax.ShapeDtypeStruct(shape, dtype)


def _row_tile(s, want=256):
    return want if s % want == 0 else s


def _sigmoid(x):
    return 1.0 / (1.0 + jnp.exp(-x))


def _softplus(x):
    return jnp.maximum(x, 0.0) + jnp.log(1.0 + jnp.exp(-jnp.abs(x)))


def _mm_spec(a, b, name, grid, a_spec, b_spec, o_spec, out, ca, cb, acc_shape, drop=(0, 0, 0), res=None, r_spec=None,
             norm_w=None):
    nk = grid[2]
    da, db, do_ = drop
    has_res = res is not None
    has_norm = norm_w is not None

    def body(*refs):
        refs = list(refs)
        a_ref, b_ref = refs[:2]
        r_ref = refs[2] if has_res else None
        w_ref = refs[2 + has_res] if has_norm else None
        o_ref = refs[2 + has_res + has_norm]
        h_ref = refs[3 + has_res + has_norm] if has_norm else None
        acc_ref = refs[-1]
        k = pl.program_id(2)

        @pl.when(k == 0)
        def _():
            acc_ref[...] = jnp.zeros_like(acc_ref)

        av = a_ref[(0,) * da] if da else a_ref[...]
        bv = b_ref[(0,) * db] if db else b_ref[...]
        acc_ref[...] += _dot(av.astype(BF16), bv.astype(BF16), ca, cb)

        @pl.when(k == nk - 1)
        def _():
            val = acc_ref[...]
            if has_res:
                val = val + r_ref[...]
            if do_:
                o_ref[(0,) * do_] = val.astype(out.dtype)
            else:
                o_ref[...] = val.astype(out.dtype)
            if has_norm:
                r = lax.rsqrt(jnp.mean(val * val, axis=-1, keepdims=True) + RMS_EPS)
                h_ref[...] = ((val * r) * w_ref[...]).astype(BF16)

    in_specs = [a_spec, b_spec] + ([r_spec] if has_res else [])
    args = (a, b) + ((res,) if has_res else ())
    out_specs, outs = o_spec, out
    if has_norm:
        assert acc_shape[1] == norm_w.shape[1] == out.shape[-1]
        in_specs.append(pl.BlockSpec((1, acc_shape[1]), lambda i, j, k: (0, 0)))
        args += (norm_w,)
        out_specs, outs = [o_spec, o_spec], [out, _sds(out.shape, BF16)]
    return _pc(body, name, grid, in_specs, out_specs, outs, [pltpu.VMEM(acc_shape, F32)])(*args)


def _mm(a, b, mode, name, out_dtype=F32, res=None, norm_w=None):
    if mode == "tn":
        r, m = a.shape
        n = b.shape[1]
        tm, tn, tk = _pick_tile(m), _pick_tile(n), _pick_tile(r)
        grid = (m // tm, n // tn, r // tk)
        a_spec = pl.BlockSpec((tk, tm), lambda i, j, k: (k, i))
        b_spec = pl.BlockSpec((tk, tn), lambda i, j, k: (k, j))
        ca, cb = 0, 0
    else:
        m, kd = a.shape
        n = b.shape[1] if mode == "nn" else b.shape[0]
        tm, tn, tk = _pick_tile(m), _pick_tile(n), _pick_tile(kd)
        grid = (m // tm, n // tn, kd // tk)
        a_spec = pl.BlockSpec((tm, tk), lambda i, j, k: (i, k))
        if mode == "nn":
            b_spec = pl.BlockSpec((tk, tn), lambda i, j, k: (k, j))
            ca, cb = 1, 0
        else:
            b_spec = pl.BlockSpec((tn, tk), lambda i, j, k: (j, k))
            ca, cb = 1, 1
    o_spec = pl.BlockSpec((tm, tn), lambda i, j, k: (i, j))
    return _mm_spec(a, b, name, grid, a_spec, b_spec, o_spec, _sds((m, n), out_dtype), ca, cb, (tm, tn), res=res, r_spec=o_spec,
                    norm_w=norm_w)


def _rms_fwd(x, w, name):
    s, d = x.shape
    ts = _row_tile(s)

    def body(x_ref, w_ref, o_ref):
        xv = x_ref[...]
        r = lax.rsqrt(jnp.mean(xv * xv, axis=-1, keepdims=True) + RMS_EPS)
        o_ref[...] = ((xv * r) * w_ref[...]).astype(BF16)

    row = pl.BlockSpec((ts, d), lambda i: (i, 0))
    return _pc(body, name, (s // ts,), [row, pl.BlockSpec((1, d), lambda i: (0, 0))], row, _sds((s, d), BF16))(x, w)


def _mm_dnorm(a, b, name, nk, a_spec, b_spec, ca, cb, drop, x, w, dres, hosted=None):
    s, d = x.shape
    tm = _pick_tile(s)
    da, db = drop

    def body(a_ref, b_ref, x_ref, w_ref, r_ref, dx_ref, dw_ref, acc_ref):
        i = pl.program_id(0)
        k = pl.program_id(2)

        @pl.when(k == 0)
        def _():
            acc_ref[...] = jnp.zeros_like(acc_ref)

        av = a_ref[(0,) * da] if da else a_ref[...]
        bv = b_ref[(0,) * db] if db else b_ref[...]
        acc_ref[...] += _dot(av.astype(BF16), bv.astype(BF16), ca, cb)

        @pl.when(k == nk - 1)
        def _():
            dhv = acc_ref[...]
            xv = x_ref[...]
            r = lax.rsqrt(jnp.mean(xv * xv, axis=-1, keepdims=True) + RMS_EPS)
            xhat = xv * r
            g = dhv * w_ref[...]
            dx_ref[...] = r_ref[...] + r * (g - xhat * jnp.mean(g * xhat, axis=-1, keepdims=True))
            part = jnp.sum(dhv * xhat, axis=0, keepdims=True)

            @pl.when(i == 0)
            def _():
                dw_ref[...] = part

            @pl.when(i > 0)
            def _():
                dw_ref[...] += part

    row = pl.BlockSpec((tm, d), lambda i, j, k: (i, 0))
    vec = pl.BlockSpec((1, d), lambda i, j, k: (0, 0))
    return list(_pc(body, name, (s // tm, 1, nk), [a_spec, b_spec, row, vec, row], [row, vec], [_sds((s, d)), _sds((1, d))],
                    [pltpu.VMEM((tm, d), F32)], hosted=hosted)(a, b, x, w, dres))


def _mm_dnorm_nt(dproj, w_in, name, x, w, dres):
    tm = _pick_tile(x.shape[0])
    tk = _pick_tile(dproj.shape[1])
    return _mm_dnorm(dproj, w_in, name, dproj.shape[1] // tk, pl.BlockSpec((tm, tk), lambda i, j, k: (i, k)),
                     pl.BlockSpec((D_MODEL, tk), lambda i, j, k: (0, k)), 1, 1, (0, 0), x, w, dres)


def _ffn_gate_up(h, w_gu, name, hosted=None):
    s = h.shape[0]
    tm = _pick_tile(s)

    def body(h_ref, wg_ref, wu_ref, gu_ref, a_ref):
        hv = h_ref[...]
        g = _dot(hv, wg_ref[0], 1, 0)
        u = _dot(hv, wu_ref[0], 1, 0)
        gu_ref[0, 0] = g.astype(BF16)
        gu_ref[0, 1] = u.astype(BF16)
        a_ref[0] = (g * _sigmoid(g) * u).astype(BF16)

    wblk = lambda off: pl.BlockSpec((1, D_MODEL, FF_BLOCK), lambda i, k: (k + off, 0, 0))
    return _pc(body, name, (s // tm, 4), [pl.BlockSpec((tm, D_MODEL), lambda i, k: (i, 0)), wblk(0), wblk(4)],
               [pl.BlockSpec((1, 2, tm, FF_BLOCK), lambda i, k: (k, 0, i, 0)), pl.BlockSpec((1, tm, FF_BLOCK), lambda i, k: (k, i, 0))],
               [_sds((4, 2, s, FF_BLOCK), BF16), _sds((4, s, FF_BLOCK), BF16)], hosted=hosted)(h, w_gu, w_gu)


def _ffn_dgate_up(dy, w_down, gu, name):
    s = dy.shape[0]
    tm = _pick_tile(s)

    def body(dy_ref, w_ref, gu_ref, o_ref):
        dav = _dot(dy_ref[...].astype(BF16), w_ref[0], 1, 1)
        g = gu_ref[0, 0].astype(F32)
        u = gu_ref[0, 1].astype(F32)
        sg = _sigmoid(g)
        o_ref[0, 0] = (dav * u * (sg * (1.0 + g * (1.0 - sg)))).astype(BF16)
        o_ref[0, 1] = (dav * (g * sg)).astype(BF16)

    pair = pl.BlockSpec((1, 2, tm, FF_BLOCK), lambda i, k: (k, 0, i, 0))
    return _pc(body, name, (s // tm, 4),
               [pl.BlockSpec((tm, D_MODEL), lambda i, k: (i, 0)), pl.BlockSpec((1, FF_BLOCK, D_MODEL), lambda i, k: (k, 0, 0)), pair],
               pair, _sds((4, 2, s, FF_BLOCK), BF16))(dy, w_down, gu)


def _ffn_fwd(x, h, w_gu, w_down, tag, next_norm=None, hosted=None):
    s = x.shape[0]
    tm = _pick_tile(s)
    gu, a, *got = _ffn_gate_up(h, w_gu, f"ffn_gu_{tag}", hosted)
    xspec = pl.BlockSpec((tm, D_MODEL), lambda i, j, k: (i, 0))
    y = _mm_spec(a, w_down, f"ffn_down_{tag}", (s // tm, 1, 4),
                 pl.BlockSpec((1, tm, FF_BLOCK), lambda i, j, k: (k, i, 0)),
                 pl.BlockSpec((1, FF_BLOCK, D_MODEL), lambda i, j, k: (k, 0, 0)),
                 xspec, _sds((s, D_MODEL)), 1, 0, (tm, D_MODEL), drop=(1, 1, 0), res=x, r_spec=xspec, norm_w=next_norm)
    y, h_next = y if next_norm is not None else (y, None)
    return y, h_next, (x, h, gu, a), got


def _ffn_bwd(dy, saved, norm_w, w_gu, w_down, tag, hosted=None):
    x, h, gu, a = saved
    s = x.shape[0]
    tm = _pick_tile(s)
    g_down = _mm_spec(a, dy, f"ffn_gdown_{tag}", (4, 1, s // tm),
                      pl.BlockSpec((1, tm, FF_BLOCK), lambda i, j, k: (i, k, 0)),
                      pl.BlockSpec((tm, D_MODEL), lambda i, j, k: (k, 0)),
                      pl.BlockSpec((1, FF_BLOCK, D_MODEL), lambda i, j, k: (i, 0, 0)),
                      _sds((4, FF_BLOCK, D_MODEL), BF16), 0, 0, (FF_BLOCK, D_MODEL), drop=(1, 0, 1))
    dgu = _ffn_dgate_up(dy, w_down, gu, f"ffn_dgu_{tag}")
    g_gu = _mm_spec(h, dgu, f"ffn_ggu_{tag}", (NDEV, 1, s // tm),
                    pl.BlockSpec((tm, D_MODEL), lambda i, j, k: (k, 0)),
                    pl.BlockSpec((1, 1, tm, FF_BLOCK), lambda i, j, k: (i % 4, i // 4, k, 0)),
                    pl.BlockSpec((1, D_MODEL, FF_BLOCK), lambda i, j, k: (i, 0, 0)),
                    _sds((NDEV, D_MODEL, FF_BLOCK), BF16), 0, 0, (D_MODEL, FF_BLOCK), drop=(0, 2, 1))
    dx, g_norm, *got = _mm_dnorm(dgu, w_gu, f"ffn_dh_{tag}", NDEV,
                                 pl.BlockSpec((1, 1, tm, FF_BLOCK), lambda i, j, k: (k % 4, k // 4, i, 0)),
                                 pl.BlockSpec((1, D_MODEL, FF_BLOCK), lambda i, j, k: (k, 0, 0)), 1, 1, (2, 1), x, norm_w, dy,
                                 hosted=hosted)
    return dx, g_norm, g_gu, g_down, got


def _prev_rows(cur, halo, j, first):
    rid = lax.broadcasted_iota(jnp.int32, cur.shape, 0)
    hid = lax.broadcasted_iota(jnp.int32, halo.shape, 0)
    out = pltpu.roll(cur, j, 0)
    for t in range(j):
        row = jnp.sum(jnp.where(hid == 8 - j + t, halo, 0.0), axis=0, keepdims=True)
        row = jnp.where(first, 0.0, row)
        out = jnp.where(rid == t, row, out)
    return out


def _next_rows(cur, halo, j, last):
    ts = cur.shape[0]
    rid = lax.broadcasted_iota(jnp.int32, cur.shape, 0)
    hid = lax.broadcasted_iota(jnp.int32, halo.shape, 0)
    out = pltpu.roll(cur, ts - j, 0)
    for t in range(j):
        row = jnp.sum(jnp.where(hid == t, halo, 0.0), axis=0, keepdims=True)
        row = jnp.where(last, 0.0, row)
        out = jnp.where(rid == ts - j + t, row, out)
    return out


def _halo_specs(ts, s, width, col):
    per = ts // 8
    nblk = s // 8
    prev = pl.BlockSpec((8, width), lambda i: (jnp.maximum(i * per - 1, 0), col))
    nxt = pl.BlockSpec((8, width), lambda i: (jnp.minimum((i + 1) * per, nblk - 1), col))
    return prev, nxt


def _cgate_fwd(p, w_dw, name):
    s = p.shape[0]
    d = D_MODEL
    ts = _row_tile(s)
    prev, _ = _halo_specs(ts, s, 3 * d, 0)

    def body(p_ref, h_ref, w_ref, z_ref):
        first = pl.program_id(0) == 0
        b = p_ref[:, :d]
        cv = p_ref[:, d:2 * d] * p_ref[:, 2 * d:]
        hcv = h_ref[:, d:2 * d] * h_ref[:, 2 * d:]
        u = w_ref[2:3, :] * cv + w_ref[1:2, :] * _prev_rows(cv, hcv, 1, first) + w_ref[0:1, :] * _prev_rows(cv, hcv, 2, first)
        z_ref[...] = (b * u).astype(BF16)

    return _pc(body, name, (s // ts,),
               [pl.BlockSpec((ts, 3 * d), lambda i: (i, 0)), prev, pl.BlockSpec((3, d), lambda i: (0, 0))],
               pl.BlockSpec((ts, d), lambda i: (i, 0)), _sds((s, d), BF16))(p, p, w_dw)


def _cgate_bwd(p, dz, w_dw, name):
    s = p.shape[0]
    d = D_MODEL
    ts = _row_tile(s)
    nt = s // ts
    p_prev, p_next = _halo_specs(ts, s, 3 * d, 0)
    _, dz_next = _halo_specs(ts, s, d, 0)

    def body(p_ref, hp_ref, hn_ref, dz_ref, dzn_ref, w_ref, dp_ref, dw_ref):
        i = pl.program_id(0)
        first = i == 0
        last = i == nt - 1
        b = p_ref[:, :d]
        c = p_ref[:, d:2 * d]
        v = p_ref[:, 2 * d:]
        cv = c * v
        hcv = hp_ref[:, d:2 * d] * hp_ref[:, 2 * d:]
        cv1 = _prev_rows(cv, hcv, 1, first)
        cv2 = _prev_rows(cv, hcv, 2, first)
        w0, w1, w2 = w_ref[0:1, :], w_ref[1:2, :], w_ref[2:3, :]
        u = w2 * cv + w1 * cv1 + w0 * cv2
        dzv = dz_ref[...]
        du = dzv * b
        dun = dzn_ref[...] * hn_ref[:, :d]
        dcv = w2 * du + w1 * _next_rows(du, dun, 1, last) + w0 * _next_rows(du, dun, 2, last)
        dp_ref[:, :d] = (dzv * u).astype(BF16)
        dp_ref[:, d:2 * d] = (dcv * v).astype(BF16)
        dp_ref[:, 2 * d:] = (dcv * c).astype(BF16)

        @pl.when(first)
        def _():
            dw_ref[...] = jnp.zeros_like(dw_ref)

        dw_ref[0:1, :] += jnp.sum(du * cv2, axis=0, keepdims=True)
        dw_ref[1:2, :] += jnp.sum(du * cv1, axis=0, keepdims=True)
        dw_ref[2:3, :] += jnp.sum(du * cv, axis=0, keepdims=True)

    wide = pl.BlockSpec((ts, 3 * d), lambda i: (i, 0))
    wspec = pl.BlockSpec((3, d), lambda i: (0, 0))
    return _pc(body, name, (nt,),
               [wide, p_prev, p_next, pl.BlockSpec((ts, d), lambda i: (i, 0)), dz_next, wspec],
               [wide, wspec], [_sds((s, 3 * d), BF16), _sds((3, d))])(p, p, p, dz, dz, w_dw)


def _conv_fwd(x, h, w_in, w_dw, w_out, tag, next_norm):
    wn = _cols_from_blocks(w_in)
    p = _mm(h, wn, "nn", f"conv_in_{tag}")
    z = _cgate_fwd(p, w_dw, f"conv_gate_{tag}")
    y, h_next = _mm(z, w_out, "nn", f"conv_out_{tag}", res=x, norm_w=next_norm)
    return y, h_next, (x, h, p, z, wn)


def _conv_bwd(dy, saved, norm_w, w_in, w_dw, w_out, tag):
    x, h, p, z, wn = saved
    dz = _mm(dy, w_out, "nt", f"conv_dz_{tag}")
    g_out = _mm(z, dy, "tn", f"conv_gout_{tag}", out_dtype=BF16)
    dp, g_dw = _cgate_bwd(p, dz, w_dw, f"conv_dgate_{tag}")
    g_in = _blocks_from_cols(_mm(h, dp, "tn", f"conv_gin_{tag}", out_dtype=BF16))
    dx, g_norm = _mm_dnorm_nt(dp, wn, f"conv_dh_{tag}", x, norm_w, dy)
    return dx, g_norm, g_in, g_dw, g_out


def _tri(lower):
    r = lax.broadcasted_iota(jnp.int32, (LANES, LANES), 0)
    c = lax.broadcasted_iota(jnp.int32, (LANES, LANES), 1)
    return jnp.where((r >= c) if lower else (r <= c), 1.0, 0.0).astype(F32)


def _cumsum_rows(v, reverse, name):
    s = v.shape[0]
    n = s // LANES
    idx = (lambda i: (n - 1 - i, 0)) if reverse else (lambda i: (i, 0))

    def body(v_ref, o_ref, carry_ref):
        @pl.when(pl.program_id(0) == 0)
        def _():
            carry_ref[...] = jnp.zeros_like(carry_ref)

        blk = v_ref[...]
        o_ref[...] = _dot(_tri(not reverse), blk, 1, 0, HI) + carry_ref[0:1, :]
        carry_ref[...] += jnp.sum(blk, axis=0, keepdims=True)

    spec = pl.BlockSpec((LANES, LANES), idx)
    return _pc(body, name, (n,), [spec], spec, _sds((s, LANES)), [pltpu.VMEM((8, LANES), F32)])(v)


def _lo_mask(shape):
    return lax.broadcasted_iota(jnp.int32, shape, len(shape) - 1) < HEAD_DIM


def _half_sums(v, lo):
    sa = jnp.sum(jnp.where(lo, v, 0.0), axis=-1, keepdims=True)
    sb = jnp.sum(jnp.where(lo, 0.0, v), axis=-1, keepdims=True)
    return jnp.where(lo, sa, sb)


def _fox_prep_fwd(proj, gq, gk, name):
    s = proj.shape[0]
    ts = _row_tile(s)
    qscale = HEAD_DIM ** -0.5 * LOG2E

    def body(q_ref, k_ref, v_ref, gq_ref, gk_ref, qo_ref, ko_ref, vo_ref):
        lo = _lo_mask((ts, LANES))

        def hnorm(xv, g):
            ms = _half_sums(xv * xv, lo) * (1.0 / HEAD_DIM)
            return (xv * lax.rsqrt(ms + RMS_EPS)) * g

        for p in range(8):
            cols = slice(p * LANES, (p + 1) * LANES)
            qo_ref[:, cols] = (hnorm(q_ref[:, cols], gq_ref[...]) * qscale).astype(BF16)
            ko_ref[:, cols] = hnorm(k_ref[:, cols], gk_ref[...]).astype(BF16)
        vo_ref[...] = v_ref[...].astype(BF16)

    def wide(blk):
        return pl.BlockSpec((ts, D_MODEL), lambda i: (i, blk))

    gspec = pl.BlockSpec((1, LANES), lambda i: (0, 0))
    out = _sds((s, D_MODEL), BF16)
    return _pc(body, name, (s // ts,), [wide(0), wide(1), wide(2), gspec, gspec], [wide(0)] * 3, [out] * 3)(
        proj, proj, proj, gq, gk)


def _fox_logf(proj, bf, name):
    s = proj.shape[0]
    ts = _row_tile(s, 512)

    def body(f_ref, b_ref, o_ref):
        z = f_ref[...] + b_ref[...]
        lf = jnp.minimum(z, 0.0) - jnp.log(1.0 + jnp.exp(-jnp.abs(z)))
        real = lax.broadcasted_iota(jnp.int32, (ts, LANES), 1) < ATTN_HEADS
        o_ref[...] = jnp.where(real, lf, 0.0)

    return _pc(body, name, (s // ts,), [pl.BlockSpec((ts, LANES), lambda i: (i, 24)), pl.BlockSpec((1, LANES), lambda i: (0, 0))],
               pl.BlockSpec((ts, LANES), lambda i: (i, 0)), _sds((s, LANES)))(proj, bf)


def _fox_dlogf(proj, bf, dlf, name):
    s = proj.shape[0]
    ts = _row_tile(s, 512)

    def body(f_ref, b_ref, d_ref, o_ref, db_ref):
        z = f_ref[...] + b_ref[...]
        real = lax.broadcasted_iota(jnp.int32, (ts, LANES), 1) < ATTN_HEADS
        g = jnp.where(real, d_ref[...] * _sigmoid(-z), 0.0)
        o_ref[...] = g.astype(BF16)

        @pl.when(pl.program_id(0) == 0)
        def _():
            db_ref[...] = jnp.zeros_like(db_ref)

        db_ref[...] += jnp.sum(g, axis=0, keepdims=True)

    vec = pl.BlockSpec((1, LANES), lambda i: (0, 0))
    row = pl.BlockSpec((ts, LANES), lambda i: (i, 0))
    return _pc(body, name, (s // ts,), [pl.BlockSpec((ts, LANES), lambda i: (i, 24)), vec, row], [row, vec],
               [_sds((s, LANES), BF16), _sds((1, LANES))])(proj, bf, dlf)


def _decay_terms(cum):
    s = cum.shape[0]
    c2 = cum * LOG2E
    hi = lax.reduce_precision(c2, 8, 7)
    mid = lax.reduce_precision(c2 - hi, 8, 7)
    low = lax.reduce_precision(c2 - hi - mid, 8, 7)
    one = jnp.ones_like(hi)

    def place(terms):
        tt = jnp.stack(terms, axis=-1).astype(BF16).reshape(s, 8, 2, 6)
        z = jnp.zeros((s, 8, HEAD_DIM - 6), BF16)
        return jnp.concatenate([tt[:, :, 1], z, tt[:, :, 0], z], axis=-1).reshape(s, D_MODEL)

    return place([hi, mid, low, one, one, one]), place([one, one, one, -hi, -mid, -low])


def _attn_tiles(s):
    t = s
    for cand in (ATTN_TILE, ATTN_TILE // 2):
        if s % cand == 0:
            t = cand
            break
    return t, s // t


def _tri_steps(n, by_key):
    if by_key:
        pairs = [(q, k) for k in range(n) for q in range(k, n)]
    else:
        pairs = [(q, k) for q in range(n) for k in range(q + 1)]
    arr = np.asarray(pairs, np.int32)
    return jnp.asarray(arr[:, 0]), jnp.asarray(arr[:, 1])


def _attn_call(body, name, s, by_key, inputs, in_kinds, out_kinds, out_shapes, scratch, hosted=None, vmem=VMEM_LIMIT_BYTES):
    t, n = _attn_tiles(s)
    qi_arr, ki_arr = _tri_steps(n, by_key)
    nsteps = int(qi_arr.shape[0])
    specs = {
        "q": pl.BlockSpec((t, LANES), lambda p, i, qi, ki: (qi[i], p)),
        "k": pl.BlockSpec((t, LANES), lambda p, i, qi, ki: (ki[i], p)),
        "r": pl.BlockSpec((1, 2, t), lambda p, i, qi, ki: (p, 0, qi[i])),
        "m": pl.BlockSpec((1, t, t), lambda p, i, qi, ki: (jnp.where(qi[i] == ki[i], 1, 0), 0, 0)),
        "Q": pl.BlockSpec((1, LANES, s), lambda p, i, qi, ki: (p, 0, 0)),
        "R": pl.BlockSpec((1, 2, s), lambda p, i, qi, ki: (p, 0, 0)),
    }
    in_specs = [specs[c] for c in in_kinds]
    out_specs = [specs[c] for c in out_kinds]
    out_shapes, scratch, inputs = list(out_shapes), list(scratch), list(inputs)
    run = body
    if hosted is not None:
        arrays, gather = hosted
        na, n_in, n_out, n_scr = len(arrays), len(inputs), len(out_kinds), len(scratch)
        pick, xouts, sems = _exchange_parts(arrays, gather)

        def run(qi_ref, ki_ref, *refs):
            ins, srcs = refs[:n_in], refs[n_in:n_in + na]
            outs, dsts = refs[n_in + na:n_in + na + n_out], refs[n_in + na + n_out:n_in + 2 * na + n_out]
            scr, xsems = refs[n_in + 2 * na + n_out:n_in + 2 * na + n_out + n_scr], refs[n_in + 2 * na + n_out + n_scr:]
            p = pl.program_id(0)
            i = pl.program_id(1)

            @pl.when(jnp.logical_and(p == 0, i == 0))
            def _():
                _exchange_start(_exchange_copies(pick(srcs), dsts, *xsems))

            body(qi_ref, ki_ref, *ins, *outs, *scr)

            @pl.when(jnp.logical_and(p == 7, i == nsteps - 1))
            def _():
                _exchange_wait(_exchange_copies(pick(srcs), dsts, *xsems))

        hbm = pl.BlockSpec(memory_space=pl.ANY)
        in_specs += [hbm] * na
        out_specs += [hbm] * na
        out_shapes += xouts
        scratch += sems
        inputs += list(arrays)
    grid_spec = pltpu.PrefetchScalarGridSpec(
        num_scalar_prefetch=2, grid=(8, nsteps), in_specs=in_specs, out_specs=out_specs, scratch_shapes=scratch)
    params = pltpu.CompilerParams(dimension_semantics=("arbitrary", "arbitrary"), vmem_limit_bytes=vmem)
    return pl.pallas_call(run, name=name, grid_spec=grid_spec, out_shape=out_shapes, compiler_params=params)(
        qi_arr, ki_arr, *inputs)


def _biased_kq(q2, k2, aq, ak, lo):
    sa = _dot(jnp.where(lo, k2, ak), jnp.where(lo, q2, aq), 1, 1)
    sb = _dot(jnp.where(lo, ak, k2), jnp.where(lo, aq, q2), 1, 1)
    return sa, sb


def _causal_bias(s):
    t, _ = _attn_tiles(s)
    kid = lax.broadcasted_iota(jnp.int32, (t, t), 0)
    qid = lax.broadcasted_iota(jnp.int32, (t, t), 1)
    return jnp.stack([jnp.zeros((t, t), BF16), jnp.where(kid > qid, -jnp.inf, 0.0).astype(BF16)])


def _fold8(v, op):
    return op(v.reshape(v.shape[0] // 8, 8, v.shape[1]), axis=0)


def _chunk(ref, mask_ref, hd, r):
    rows = slice(r * ATTN_ROWS, (r + 1) * ATTN_ROWS)
    return rows, ref[hd, rows, :] + mask_ref[0, rows, :].astype(F32)


def _flash_fwd(qs, kn, vb, augq, augk, cmask, name, hosted=None):
    s = qs.shape[0]
    t, n = _attn_tiles(s)
    nch = t // ATTN_ROWS

    def body(qi_ref, ki_ref, q_ref, k_ref, v_ref, aq_ref, ak_ref, mk_ref, o_ref, lse_ref, s_ref, p_ref, m_ref, l_ref, acc_ref):
        i = pl.program_id(1)
        qi = qi_ref[i]
        ki = ki_ref[i]

        @pl.when(ki == 0)
        def _():
            m_ref[...] = jnp.full_like(m_ref, -jnp.inf)
            l_ref[...] = jnp.zeros_like(l_ref)
            acc_ref[...] = jnp.zeros_like(acc_ref)

        lo = _lo_mask((t, LANES))
        rowlo = lax.broadcasted_iota(jnp.int32, (LANES, t), 0) < HEAD_DIM
        v2 = v_ref[...]
        sa, sb = _biased_kq(q_ref[...], k_ref[...], aq_ref[...], ak_ref[...], lo)
        s_ref[0] = sa
        s_ref[1] = sb
        alphas, pvs = [], []
        for hd in range(2):
            mx = jnp.full((8, t), -jnp.inf, F32)
            for r in range(nch):
                _, sc = _chunk(s_ref, mk_ref, hd, r)
                mx = jnp.maximum(mx, _fold8(sc, jnp.max))
            m_prev = m_ref[hd:hd + 1, :]
            m_new = jnp.maximum(m_prev, jnp.max(mx, axis=0, keepdims=True))
            ls = jnp.zeros((8, t), F32)
            for r in range(nch):
                rows, sc = _chunk(s_ref, mk_ref, hd, r)
                pm = jnp.exp2(sc - m_new)
                ls = ls + _fold8(pm, jnp.sum)
                p_ref[hd, rows, :] = pm.astype(BF16)
            alpha = jnp.exp2(m_prev - m_new)
            l_ref[hd:hd + 1, :] = alpha * l_ref[hd:hd + 1, :] + jnp.sum(ls, axis=0, keepdims=True)
            m_ref[hd:hd + 1, :] = m_new
            alphas.append(alpha)
            pvs.append(_dot(v2, p_ref[hd], 0, 0))
        acc_ref[...] = jnp.where(rowlo, alphas[0], alphas[1]) * acc_ref[...] + jnp.where(rowlo, pvs[0], pvs[1])

        @pl.when(ki == qi)
        def _():
            o_ref[...] = (acc_ref[...] / jnp.where(rowlo, l_ref[0:1, :], l_ref[1:2, :])).T
            lse_ref[0] = m_ref[0:2, :] + jnp.log2(l_ref[0:2, :])

    stat = pltpu.VMEM((8, t), F32)
    return _attn_call(body, name, s, False, (qs, kn, vb, augq, augk, cmask), "qkkqkm", "qr",
                      [_sds((s, D_MODEL)), _sds((8, 2, s))],
                      [pltpu.VMEM((2, t, t), F32), pltpu.VMEM((2, t, t), BF16), stat, stat, pltpu.VMEM((LANES, t), F32)],
                      hosted=hosted)


def _fox_delta(do, o, name):
    s = do.shape[0]
    ts = _row_tile(s)

    def body(do_ref, o_ref, d_ref):
        lo = _lo_mask((ts, LANES))
        for p in range(8):
            cols = slice(p * LANES, (p + 1) * LANES)
            d_ref[:, cols] = _half_sums(do_ref[:, cols] * o_ref[:, cols], lo)

    spec = pl.BlockSpec((ts, D_MODEL), lambda i: (i, 0))
    return _pc(body, name, (s // ts,), [spec, spec], spec, _sds((s, D_MODEL)))(do, o)


def _bwd_tile(q_ref, k_ref, v_ref, aq_ref, ak_ref, do_ref, s_ref, dp_ref, lo):
    do2 = do_ref[...].astype(BF16)
    zero = jnp.zeros_like(do2)
    v2 = v_ref[...]
    sa, sb = _biased_kq(q_ref[...], k_ref[...], aq_ref[...], ak_ref[...], lo)
    s_ref[0] = sa
    s_ref[1] = sb
    dp_ref[0] = _dot(v2, jnp.where(lo, do2, zero), 1, 1)
    dp_ref[1] = _dot(v2, jnp.where(lo, zero, do2), 1, 1)
    return do2


def _bwd_chunk(s_ref, dp_ref, mk_ref, lse_ref, dl_ref, hd, r):
    rows, sc = _chunk(s_ref, mk_ref, hd, r)
    pm = jnp.exp2(sc - lse_ref[0, hd:hd + 1, :])
    ds = pm * (dp_ref[hd, rows, :] - dl_ref[0, hd:hd + 1, :])
    return rows, pm, ds


def _flash_bwd(qs, kn, vb, augq, augk, cmask, do, lse, delta, name, hosted=None):
    s = qs.shape[0]
    t, n = _attn_tiles(s)
    nch = t // ATTN_ROWS

    def body(qi_ref, ki_ref, q_ref, k_ref, v_ref, aq_ref, ak_ref, mk_ref, do_ref, lse_ref, dl_ref,
             dk_ref, dv_ref, dc_ref, dq_ref, dcq_ref, s_ref, dp_ref, p_ref, ds_ref, dka_ref, dva_ref, dca_ref):
        i = pl.program_id(1)
        qi = qi_ref[i]
        ki = ki_ref[i]

        @pl.when(i == 0)
        def _():
            dq_ref[...] = jnp.zeros_like(dq_ref)
            dcq_ref[...] = jnp.zeros_like(dcq_ref)

        @pl.when(qi == ki)
        def _():
            dka_ref[...] = jnp.zeros_like(dka_ref)
            dva_ref[...] = jnp.zeros_like(dva_ref)
            dca_ref[...] = jnp.zeros_like(dca_ref)

        lo = _lo_mask((t, LANES))
        rowlo = lax.broadcasted_iota(jnp.int32, (LANES, t), 0) < HEAD_DIM
        do2 = _bwd_tile(q_ref, k_ref, v_ref, aq_ref, ak_ref, do_ref, s_ref, dp_ref, lo)
        q2 = q_ref[...]
        k2 = k_ref[...]
        qcols = pl.ds(pl.multiple_of(qi * t, t), t)
        dvs, dks, dqs = [], [], []
        for hd in range(2):
            rs = jnp.zeros((8, t), F32)
            for r in range(nch):
                rows, pm, ds = _bwd_chunk(s_ref, dp_ref, mk_ref, lse_ref, dl_ref, hd, r)
                rs = rs + _fold8(ds, jnp.sum)
                part = ds[:, 0:LANES]
                for c in range(1, t // LANES):
                    part = part + ds[:, c * LANES:(c + 1) * LANES]
                dca_ref[hd, rows, :] += part
                p_ref[hd, rows, :] = pm.astype(BF16)
                ds_ref[hd, rows, :] = ds.astype(BF16)
            dcq_ref[0, hd:hd + 1, qcols] += jnp.sum(rs, axis=0, keepdims=True)
            dvs.append(_dot(p_ref[hd], do2, 1, 0))
            dks.append(_dot(ds_ref[hd], q2, 1, 0))
            dqs.append(_dot(k2, ds_ref[hd], 0, 0))
        dva_ref[...] += jnp.where(lo, dvs[0], dvs[1])
        dka_ref[...] += jnp.where(lo, dks[0], dks[1])
        dq_ref[0, :, qcols] += jnp.where(rowlo, dqs[0], dqs[1])

        @pl.when(qi == n - 1)
        def _():
            dk_ref[...] = dka_ref[...] * LN2
            dv_ref[...] = dva_ref[...]
            dc_ref[...] = -jnp.where(lo, jnp.sum(dca_ref[0], axis=-1, keepdims=True), jnp.sum(dca_ref[1], axis=-1, keepdims=True))

    out = _sds((s, D_MODEL))
    return _attn_call(body, name, s, True, (qs, kn, vb, augq, augk, cmask, do, lse, delta), "qkkqkmqrr", "kkkQR",
                      [out, out, out, _sds((8, LANES, s)), _sds((8, 2, s))],
                      [pltpu.VMEM((2, t, t), F32), pltpu.VMEM((2, t, t), F32), pltpu.VMEM((2, t, t), BF16),
                       pltpu.VMEM((2, t, t), BF16), pltpu.VMEM((t, LANES), F32), pltpu.VMEM((t, LANES), F32),
                       pltpu.VMEM((2, t, LANES), F32)], hosted=hosted, vmem=ATTN_BWD_VMEM_BYTES)


def _fox_prep_bwd(proj, dqs, dk, dv, gq, gk, name):
    s = proj.shape[0]
    ts = _row_tile(s)
    scale = HEAD_DIM ** -0.5

    def body(q_ref, k_ref, dq_ref, dk_ref, dv_ref, gq_ref, gk_ref, oq_ref, ok_ref, ov_ref, dgq_ref, dgk_ref):
        lo = _lo_mask((ts, LANES))

        @pl.when(pl.program_id(0) == 0)
        def _():
            dgq_ref[...] = jnp.zeros_like(dgq_ref)
            dgk_ref[...] = jnp.zeros_like(dgk_ref)

        def back(xv, dout, g):
            r = lax.rsqrt(_half_sums(xv * xv, lo) * (1.0 / HEAD_DIM) + RMS_EPS)
            y = xv * r
            dy = dout * g
            dx = r * (dy - y * (_half_sums(dy * y, lo) * (1.0 / HEAD_DIM)))
            return dx, jnp.sum(dout * y, axis=0, keepdims=True)

        for p in range(8):
            cols = slice(p * LANES, (p + 1) * LANES)
            dxq, dgq = back(q_ref[:, cols], dq_ref[p].T * scale, gq_ref[...])
            dxk, dgk = back(k_ref[:, cols], dk_ref[:, cols], gk_ref[...])
            oq_ref[:, cols] = dxq.astype(BF16)
            ok_ref[:, cols] = dxk.astype(BF16)
            dgq_ref[...] += dgq
            dgk_ref[...] += dgk
        ov_ref[...] = dv_ref[...].astype(BF16)

    def wide(blk):
        return pl.BlockSpec((ts, D_MODEL), lambda i: (i, blk))

    gspec = pl.BlockSpec((1, LANES), lambda i: (0, 0))
    out = _sds((s, D_MODEL), BF16)
    dqt = pl.BlockSpec((8, LANES, ts), lambda i: (0, 0, i))
    return _pc(body, name, (s // ts,), [wide(0), wide(1), dqt, wide(0), wide(0), gspec, gspec],
               [wide(0)] * 3 + [gspec] * 2, [out] * 3 + [_sds((1, LANES))] * 2)(proj, proj, dqs, dk, dv, gq, gk)


def _fox_fwd(x, h, w_in, b_f, q_gain, k_gain, w_out, next_norm, hosted=None):
    proj = _mm(h, w_in, "nn", "fox_in")
    gq = jnp.tile(q_gain, (1, 2))
    gk = jnp.tile(k_gain, (1, 2))
    bf = jnp.pad(b_f, ((0, 0), (0, LANES - ATTN_HEADS)))
    qs, kn, vb = _fox_prep_fwd(proj, gq, gk, "fox_prep")
    cum = _cumsum_rows(_fox_logf(proj, bf, "fox_logf"), False, "fox_cum")[:, :ATTN_HEADS]
    augq, augk = _decay_terms(cum)
    cmask = _causal_bias(x.shape[0])
    o, lse, *got = _flash_fwd(qs, kn, vb, augq, augk, cmask, "fox_attn", hosted=hosted)
    y, h_next = _mm(o, w_out, "nn", "fox_out", res=x, norm_w=next_norm)
    return y, h_next, (x, h, proj, gq, gk, bf, qs, kn, vb, augq, augk, cmask, o, lse), got


def _fox_bwd(dy, saved, norm_w, w_in, w_out, hosted=None):
    x, h, proj, gq, gk, bf, qs, kn, vb, augq, augk, cmask, o, lse = saved
    s = x.shape[0]
    do = _mm(dy, w_out, "nt", "fox_do")
    g_out = _mm(o, dy, "tn", "fox_gout", out_dtype=BF16)
    delta = _fox_delta(do, o, "fox_delta")[:, ::HEAD_DIM].T.reshape(8, 2, s)
    dk, dv, dck, dqs, dcq, *got = _flash_bwd(qs, kn, vb, augq, augk, cmask, do, lse, delta, "fox_dattn", hosted=hosted)
    dcum = jnp.pad(dcq.reshape(ATTN_HEADS, s).T + dck[:, ::HEAD_DIM], ((0, 0), (0, LANES - ATTN_HEADS)))
    dlf = _cumsum_rows(dcum, True, "fox_dcum")
    dfl, g_bf = _fox_dlogf(proj, bf, dlf, "fox_dlogf")
    dq_o, dk_o, dv_o, g_gq, g_gk = _fox_prep_bwd(proj, dqs, dk, dv, gq, gk, "fox_dprep")
    dproj = jnp.concatenate([dq_o, dk_o, dv_o, dfl], axis=1)
    g_in = _mm(h, dproj, "tn", "fox_gin", out_dtype=BF16)
    dx, g_norm = _mm_dnorm_nt(dproj, w_in, "fox_dh", x, norm_w, dy)
    g_q = g_gq[:, :HEAD_DIM] + g_gq[:, HEAD_DIM:]
    g_k = g_gk[:, :HEAD_DIM] + g_gk[:, HEAD_DIM:]
    return dx, g_norm, g_in[:, :FOX_IN], g_bf[:, :ATTN_HEADS], g_q, g_k, g_out, got


def _ssd_conv_fwd(proj, cw, cb, name):
    s = proj.shape[0]
    ts = _row_tile(s)
    w = 1024
    per = ts // 8

    def body(p_ref, h_ref, w_ref, b_ref, o_ref):
        first = pl.program_id(0) == 0
        cur = p_ref[...]
        halo = h_ref[...]
        u = w_ref[3:4, :] * cur + b_ref[...]
        for j in range(1, 4):
            u = u + w_ref[3 - j:4 - j, :] * _prev_rows(cur, halo, j, first)
        o_ref[...] = u * _sigmoid(u)

    return _pc(body, name, (s // ts, 4),
               [pl.BlockSpec((ts, w), lambda i, j: (i, 2 + j)),
                pl.BlockSpec((8, w), lambda i, j: (jnp.maximum(i * per - 1, 0), 2 + j)),
                pl.BlockSpec((4, w), lambda i, j: (0, j)), pl.BlockSpec((1, w), lambda i, j: (0, j))],
               pl.BlockSpec((ts, w), lambda i, j: (i, j)), _sds((s, SSM_CONV_DIM)))(proj, proj, cw, cb)


def _ssd_conv_bwd(proj, dxbc, cw, cb, name):
    s = proj.shape[0]
    ts = _row_tile(s)
    nt = s // ts
    w = 1024
    per = ts // 8
    nblk = s // 8

    def body(p_ref, hp_ref, hn_ref, d_ref, dn_ref, w_ref, b_ref, o_ref, dw_ref, db_ref):
        i = pl.program_id(1)
        first = i == 0
        last = i == nt - 1
        cur = p_ref[...]
        prev = [cur] + [_prev_rows(cur, hp_ref[...], j, first) for j in range(1, 4)]
        nxt = hn_ref[...]
        tail = cur[ts - 8:, :]
        u = b_ref[...]
        un = b_ref[...]
        for j in range(4):
            u = u + w_ref[3 - j:4 - j, :] * prev[j]
            un = un + w_ref[3 - j:4 - j, :] * (nxt if j == 0 else _prev_rows(nxt, tail, j, False))
        sg = _sigmoid(u)
        g = d_ref[...] * (sg * (1.0 + u * (1.0 - sg)))
        sn = _sigmoid(un)
        gn = dn_ref[...] * (sn * (1.0 + un * (1.0 - sn)))

        @pl.when(first)
        def _():
            dw_ref[...] = jnp.zeros_like(dw_ref)
            db_ref[...] = jnp.zeros_like(db_ref)

        dpre = w_ref[3:4, :] * g
        for j in range(1, 4):
            dpre = dpre + w_ref[3 - j:4 - j, :] * _next_rows(g, gn, j, last)
        for j in range(4):
            dw_ref[3 - j:4 - j, :] += jnp.sum(g * prev[j], axis=0, keepdims=True)
        db_ref[...] += jnp.sum(g, axis=0, keepdims=True)
        o_ref[...] = dpre.astype(BF16)

    tile = pl.BlockSpec((ts, w), lambda j, i: (i, j))
    wspec = pl.BlockSpec((4, w), lambda j, i: (0, j))
    vec = pl.BlockSpec((1, w), lambda j, i: (0, j))
    nxt_blk = lambda off: pl.BlockSpec((8, w), lambda j, i: (jnp.minimum((i + 1) * per, nblk - 1), off + j))
    return _pc(body, name, (4, nt),
               [pl.BlockSpec((ts, w), lambda j, i: (i, 2 + j)),
                pl.BlockSpec((8, w), lambda j, i: (jnp.maximum(i * per - 1, 0), 2 + j)), nxt_blk(2),
                tile, nxt_blk(0), wspec, vec],
               [tile, wspec, vec], [_sds((s, SSM_CONV_DIM), BF16), _sds((4, SSM_CONV_DIM)), _sds((1, SSM_CONV_DIM))])(
                   proj, proj, proj, dxbc, dxbc, cw, cb)


def _ssd_dt_fwd(proj, bias, a_neg, name):
    s = proj.shape[0]
    n = s // SSM_CHUNK

    def body(r_ref, b_ref, a_ref, dt_ref, ac_ref):
        real = lax.broadcasted_iota(jnp.int32, (SSM_CHUNK, LANES), 1) < SSM_HEADS
        dt = jnp.where(real, _softplus(r_ref[...] + b_ref[...]), 0.0)
        dt_ref[...] = dt
        ac_ref[...] = _dot(_tri(True), dt * a_ref[...], 1, 0, HI)

    vec = pl.BlockSpec((1, LANES), lambda c: (0, 0))
    row = pl.BlockSpec((SSM_CHUNK, LANES), lambda c: (c, 0))
    return _pc(body, name, (n,), [pl.BlockSpec((SSM_CHUNK, LANES), lambda c: (c, 48)), vec, vec], [row, row],
               [_sds((s, LANES)), _sds((s, LANES))])(proj, bias, a_neg)


def _ssd_dt_bwd(proj, bias, ddt, name):
    s = proj.shape[0]
    ts = _row_tile(s, 512)

    def body(r_ref, b_ref, d_ref, o_ref, db_ref):
        real = lax.broadcasted_iota(jnp.int32, (ts, LANES), 1) < SSM_HEADS
        g = jnp.where(real, d_ref[...] * _sigmoid(r_ref[...] + b_ref[...]), 0.0)
        o_ref[...] = g.astype(BF16)

        @pl.when(pl.program_id(0) == 0)
        def _():
            db_ref[...] = jnp.zeros_like(db_ref)

        db_ref[...] += jnp.sum(g, axis=0, keepdims=True)

    vec = pl.BlockSpec((1, LANES), lambda i: (0, 0))
    row = pl.BlockSpec((ts, LANES), lambda i: (i, 0))
    return _pc(body, name, (s // ts,), [pl.BlockSpec((ts, LANES), lambda i: (i, 48)), vec, row], [row, vec],
               [_sds((s, LANES), BF16), _sds((1, LANES))])(proj, bias, ddt)


def _pair_cols(cols, k0, lo):
    return jnp.where(lo, cols[:, k0:k0 + 1], cols[:, k0 + 1:k0 + 2])


def _last_lane(row):
    lane = lax.broadcasted_iota(jnp.int32, row.shape, 1)
    return jnp.sum(jnp.where(lane == SSM_CHUNK - 1, row, 0.0), axis=-1, keepdims=True)


SSD_FWD_GROUPS = 2
SSD_BWD_GROUPS = 1


def _ssd_specs(nc, rev, n):
    cc = (lambda c: nc - 1 - c) if rev else (lambda c: c)
    nb = SSM_INNER // (LANES * n)
    return dict(
        x=pl.BlockSpec((SSM_CHUNK, 256 * n), lambda g, c: (cc(c), g)),
        b=pl.BlockSpec((SSM_CHUNK, LANES * n), lambda g, c: (cc(c), nb + g)),
        c=pl.BlockSpec((SSM_CHUNK, LANES * n), lambda g, c: (cc(c), nb + SSM_GROUPS // n + g)),
        col=pl.BlockSpec((n, SSM_CHUNK, 4), lambda g, c: (g, cc(c), 0)),
        row=pl.BlockSpec((n, 4, SSM_CHUNK), lambda g, c: (g, 0, cc(c))),
        grp=pl.BlockSpec((n, 1, 256), lambda g, c: (g, 0, 0)),
        grow=pl.BlockSpec((n, 4, LANES), lambda g, c: (g, 0, 0)),
        hs=pl.BlockSpec((1, n, 256, SSM_STATE), lambda g, c: (cc(c), g, 0, 0)),
        bc=pl.BlockSpec((SSM_CHUNK, LANES * n), lambda g, c: (cc(c), g)),
    )


def _ssd_scan_fwd(xbc, dtc, acol, drow, arow, dskip, name):
    s = xbc.shape[0]
    nc = s // SSM_CHUNK
    n = SSD_FWD_GROUPS
    sp = _ssd_specs(nc, False, n)
    L = SSM_CHUNK

    def body(x_ref, b_ref, c_ref, dtc_ref, ac_ref, dr_ref, ar_ref, dk_ref, y_ref, hs_ref, h_ref):
        @pl.when(pl.program_id(1) == 0)
        def _():
            h_ref[...] = jnp.zeros_like(h_ref)

        for gi in range(n):
            group(gi, x_ref, b_ref, c_ref, dtc_ref, ac_ref, dr_ref, ar_ref, dk_ref, y_ref, hs_ref, h_ref)

    def group(gi, x_ref, b_ref, c_ref, dtc_ref, ac_ref, dr_ref, ar_ref, dk_ref, y_ref, hs_ref, h_ref):
        x0 = gi * 256
        bb = b_ref[:, gi * LANES:(gi + 1) * LANES].astype(BF16)
        cb = c_ref[:, gi * LANES:(gi + 1) * LANES].astype(BF16)
        gm = _dot(cb, bb, 1, 1)
        dtc = dtc_ref[gi]
        ac = ac_ref[gi]
        dr = dr_ref[gi]
        ar = ar_ref[gi]
        dsk = dk_ref[gi]
        hs_ref[0, gi] = h_ref[gi]
        tril = lax.broadcasted_iota(jnp.int32, (L, L), 0) >= lax.broadcasted_iota(jnp.int32, (L, L), 1)
        lo = _lo_mask((L, LANES))
        rowlo = lax.broadcasted_iota(jnp.int32, (L, LANES), 0) < HEAD_DIM
        for pr in range(2):
            k0 = 2 * pr
            xp = x_ref[:, x0 + pr * LANES:x0 + (pr + 1) * LANES]
            xpb = xp.astype(BF16)
            hp = h_ref[gi, pr * LANES:(pr + 1) * LANES, :]
            yd, al = [], []
            for k in (k0, k0 + 1):
                seg = ac[:, k:k + 1] - ar[k:k + 1, :]
                wk = gm * jnp.exp(jnp.where(tril, seg, -jnp.inf)) * dr[k:k + 1, :]
                yd.append(_dot(wk.astype(BF16), xpb, 1, 0))
                al.append(_last_lane(ar[k:k + 1, :]))
            e = jnp.exp(_pair_cols(ac, k0, lo))
            yo = _dot(cb, hp.astype(BF16), 1, 1) * e
            y_ref[:, x0 + pr * LANES:x0 + (pr + 1) * LANES] = (
                jnp.where(lo, yd[0], yd[1]) + yo + dsk[:, pr * LANES:(pr + 1) * LANES] * xp)
            wp = jnp.where(lo, jnp.exp(al[0] - ac[:, k0:k0 + 1]) * dtc[:, k0:k0 + 1],
                           jnp.exp(al[1] - ac[:, k0 + 1:k0 + 2]) * dtc[:, k0 + 1:k0 + 2])
            st = _dot((xp * wp).astype(BF16), bb, 0, 0)
            dec = jnp.where(rowlo, jnp.exp(al[0]), jnp.exp(al[1]))
            h_ref[gi, pr * LANES:(pr + 1) * LANES, :] = dec * hp + st

    return _pc(body, name, (SSM_GROUPS // n, nc),
               [sp["x"], sp["b"], sp["c"], sp["col"], sp["col"], sp["row"], sp["row"], sp["grp"]],
               [sp["x"], sp["hs"]], [_sds((s, SSM_INNER)), _sds((nc, SSM_GROUPS, 256, SSM_STATE))],
               [pltpu.VMEM((n, 256, SSM_STATE), F32)])(xbc, xbc, xbc, dtc, acol, drow, arow, dskip)


def _ssd_scan_bwd(xbc, dtc, acol, drow, arow, dskip, agrp, hs, dy, name):
    s = xbc.shape[0]
    nc = s // SSM_CHUNK
    n = SSD_BWD_GROUPS
    sp = _ssd_specs(nc, True, n)
    L = SSM_CHUNK

    def body(x_ref, b_ref, c_ref, dtc_ref, ac_ref, dr_ref, ar_ref, dk_ref, ag_ref, hs_ref, dy_ref,
             dx_ref, db_ref, dc_ref, ddt_ref, da_ref, dd_ref, dh_ref):
        @pl.when(pl.program_id(1) == 0)
        def _():
            dh_ref[...] = jnp.zeros_like(dh_ref)
            da_ref[...] = jnp.zeros_like(da_ref)
            dd_ref[...] = jnp.zeros_like(dd_ref)

        for gi in range(n):
            group(gi, x_ref, b_ref, c_ref, dtc_ref, ac_ref, dr_ref, ar_ref, dk_ref, ag_ref, hs_ref, dy_ref,
                  dx_ref, db_ref, dc_ref, ddt_ref, da_ref, dd_ref, dh_ref)

    def group(gi, x_ref, b_ref, c_ref, dtc_ref, ac_ref, dr_ref, ar_ref, dk_ref, ag_ref, hs_ref, dy_ref,
              dx_ref, db_ref, dc_ref, ddt_ref, da_ref, dd_ref, dh_ref):
        x0 = gi * 256
        bcols = slice(gi * LANES, (gi + 1) * LANES)
        bb = b_ref[:, bcols].astype(BF16)
        cb = c_ref[:, bcols].astype(BF16)
        gm = _dot(cb, bb, 1, 1)
        dtc = dtc_ref[gi]
        ac = ac_ref[gi]
        dr = dr_ref[gi]
        ar = ar_ref[gi]
        dsk = dk_ref[gi]
        ag = ag_ref[gi]
        tril = lax.broadcasted_iota(jnp.int32, (L, L), 0) >= lax.broadcasted_iota(jnp.int32, (L, L), 1)
        lo = _lo_mask((L, LANES))
        nlo = jnp.logical_not(lo)
        rowlo = lax.broadcasted_iota(jnp.int32, (L, LANES), 0) < HEAD_DIM
        lane = lax.broadcasted_iota(jnp.int32, (L, LANES), 1)
        lane_row = lax.broadcasted_iota(jnp.int32, (1, LANES), 1)
        dgm = jnp.zeros((L, L), F32)
        dcm = jnp.zeros((L, SSM_STATE), F32)
        dbm = jnp.zeros((L, SSM_STATE), F32)
        cols = jnp.zeros((L, LANES), F32)
        rows_ddt, rows_q, al_all, dcd_all = [], [], [], []
        for pr in range(2):
            k0 = 2 * pr
            xcols = slice(x0 + pr * LANES, x0 + (pr + 1) * LANES)
            xp = x_ref[:, xcols]
            xpb = xp.astype(BF16)
            dyp = dy_ref[:, xcols]
            dypb = dyp.astype(BF16)
            zero = jnp.zeros_like(dypb)
            hp = hs_ref[0, gi, pr * LANES:(pr + 1) * LANES, :]
            hpb = hp.astype(BF16)
            dst = dh_ref[gi, pr * LANES:(pr + 1) * LANES, :]
            dstb = dst.astype(BF16)
            dxd, al = [], []
            for k in (k0, k0 + 1):
                sel = lo if k == k0 else nlo
                seg = ac[:, k:k + 1] - ar[k:k + 1, :]
                lam = jnp.exp(jnp.where(tril, seg, -jnp.inf))
                wk = gm * lam * dr[k:k + 1, :]
                dwk = _dot(jnp.where(sel, dypb, zero), xpb, 1, 1)
                mk = dwk * gm * lam
                qk = mk * dr[k:k + 1, :]
                dgm = dgm + dwk * lam * dr[k:k + 1, :]
                rows_ddt.append(jnp.sum(mk, axis=0, keepdims=True))
                rows_q.append(jnp.sum(qk, axis=0, keepdims=True))
                cols = jnp.where(lane == k, jnp.sum(qk, axis=-1, keepdims=True), cols)
                dxd.append(_dot(wk.astype(BF16), dypb, 0, 0))
                al.append(_last_lane(ar[k:k + 1, :]))
            al_all += al
            dxp = jnp.where(lo, dxd[0], dxd[1])
            e = jnp.exp(_pair_cols(ac, k0, lo))
            dye = dyp * e
            dyeb = dye.astype(BF16)
            dcm = dcm + _dot(dyeb, hpb, 1, 0)
            dh_yoff = _dot(dyeb, cb, 0, 0)
            tq = dye * _dot(cb, hpb, 1, 1)
            cols = jnp.where(lane == 4 + k0, jnp.sum(jnp.where(lo, tq, 0.0), axis=-1, keepdims=True), cols)
            cols = jnp.where(lane == 5 + k0, jnp.sum(jnp.where(lo, 0.0, tq), axis=-1, keepdims=True), cols)
            wp = jnp.where(lo, jnp.exp(al[0] - ac[:, k0:k0 + 1]) * dtc[:, k0:k0 + 1],
                           jnp.exp(al[1] - ac[:, k0 + 1:k0 + 2]) * dtc[:, k0 + 1:k0 + 2])
            dxw = _dot(bb, dstb, 1, 1)
            dxp = dxp + dxw * wp
            tw = xp * dxw
            cols = jnp.where(lane == 8 + k0, jnp.sum(jnp.where(lo, tw, 0.0), axis=-1, keepdims=True), cols)
            cols = jnp.where(lane == 9 + k0, jnp.sum(jnp.where(lo, 0.0, tw), axis=-1, keepdims=True), cols)
            dbm = dbm + _dot((xp * wp).astype(BF16), dstb, 1, 0)
            dsl = dsk[:, pr * LANES:(pr + 1) * LANES]
            dx_ref[:, xcols] = dxp + dsl * dyp
            dd_ref[gi, :, pr * LANES:(pr + 1) * LANES] += jnp.sum(dyp * xp, axis=0, keepdims=True)
            prod = dst * hp
            dcd_all.append(jnp.sum(jnp.sum(jnp.where(rowlo, prod, 0.0), axis=-1, keepdims=True), axis=0, keepdims=True))
            dcd_all.append(jnp.sum(jnp.sum(jnp.where(rowlo, 0.0, prod), axis=-1, keepdims=True), axis=0, keepdims=True))
            dec = jnp.where(rowlo, jnp.exp(al[0]), jnp.exp(al[1]))
            dh_ref[gi, pr * LANES:(pr + 1) * LANES, :] = dec * dst + dh_yoff
        dgb = dgm.astype(BF16)
        dc_ref[:, bcols] = dcm + _dot(dgb, bb, 1, 0)
        db_ref[:, bcols] = dbm + _dot(dgb, cb, 0, 0)
        colt = cols.T
        sub8 = lax.broadcasted_iota(jnp.int32, (8, LANES), 0)
        da_rows = jnp.zeros((8, LANES), F32)
        ddt_part = []
        for k in range(4):
            rs = colt[k:k + 1, :]
            uo = colt[4 + k:5 + k, :]
            dwl = colt[8 + k:9 + k, :]
            es = jnp.exp(al_all[k] - ar[k:k + 1, :])
            wrow = es * dr[k:k + 1, :]
            dwl_w = dwl * wrow
            da_k = rs - rows_q[k] + uo - dwl_w
            tail = jnp.sum(dwl_w, axis=-1, keepdims=True) + jnp.exp(al_all[k]) * dcd_all[k]
            da_k = da_k + jnp.where(lane_row == L - 1, tail, 0.0)
            da_rows = jnp.where(sub8 == k, da_k, da_rows)
            ddt_part.append(rows_ddt[k] + dwl * es)
        dda = _dot(da_rows, _tri(True), 1, 0, HI)
        for k in range(4):
            dda_k = dda[k:k + 1, :]
            ddt_ref[gi, k:k + 1, :] = ddt_part[k] + dda_k * ag[k:k + 1, :]
            da_ref[gi, k:k + 1, :] += dda_k * dr[k:k + 1, :] * ag[k:k + 1, :]

    return _pc(body, name, (SSM_GROUPS // n, nc),
               [sp["x"], sp["b"], sp["c"], sp["col"], sp["col"], sp["row"], sp["row"], sp["grp"], sp["grow"], sp["hs"], sp["x"]],
               [sp["x"], sp["bc"], sp["bc"], sp["row"], sp["grow"], sp["grp"]],
               [_sds((s, SSM_INNER)), _sds((s, 1024)), _sds((s, 1024)), _sds((SSM_GROUPS, 4, s)),
                _sds((SSM_GROUPS, 4, LANES)), _sds((SSM_GROUPS, 1, 256))],
               [pltpu.VMEM((n, 256, SSM_STATE), F32)])(xbc, xbc, xbc, dtc, acol, drow, arow, dskip, agrp, hs, dy)


def _gnorm_fwd(y, proj, nw, name):
    s = y.shape[0]
    ts = _row_tile(s)
    gw = SSM_INNER // SSM_GROUPS

    def body(y_ref, z_ref, w_ref, o_ref):
        for g in range(SSM_GROUPS):
            sl = slice(g * gw, (g + 1) * gw)
            z = z_ref[:, sl]
            y2 = y_ref[:, sl] * (z * _sigmoid(z))
            r = lax.rsqrt(jnp.mean(y2 * y2, axis=-1, keepdims=True) + RMS_EPS)
            o_ref[:, sl] = ((y2 * r) * w_ref[:, sl]).astype(BF16)

    row = pl.BlockSpec((ts, SSM_INNER), lambda i: (i, 0))
    return _pc(body, name, (s // ts,), [row, row, pl.BlockSpec((1, SSM_INNER), lambda i: (0, 0))], row,
               _sds((s, SSM_INNER), BF16))(y, proj, nw)


def _gnorm_bwd(y, proj, nw, dyn, name):
    s = y.shape[0]
    ts = _row_tile(s)
    gw = SSM_INNER // SSM_GROUPS

    def body(y_ref, z_ref, w_ref, d_ref, dy_ref, dz_ref, dw_ref):
        @pl.when(pl.program_id(0) == 0)
        def _():
            dw_ref[...] = jnp.zeros_like(dw_ref)

        for g in range(SSM_GROUPS):
            sl = slice(g * gw, (g + 1) * gw)
            z = z_ref[:, sl]
            yv = y_ref[:, sl]
            sg = _sigmoid(z)
            sz = z * sg
            y2 = yv * sz
            r = lax.rsqrt(jnp.mean(y2 * y2, axis=-1, keepdims=True) + RMS_EPS)
            yn = y2 * r
            dout = d_ref[:, sl]
            dyg = dout * w_ref[:, sl]
            dy2 = r * (dyg - yn * jnp.mean(dyg * yn, axis=-1, keepdims=True))
            dy_ref[:, sl] = dy2 * sz
            dz_ref[:, sl] = (dy2 * yv * (sg * (1.0 + z * (1.0 - sg)))).astype(BF16)
            dw_ref[:, sl] += jnp.sum(dout * yn, axis=0, keepdims=True)

    row = pl.BlockSpec((ts, SSM_INNER), lambda i: (i, 0))
    vec = pl.BlockSpec((1, SSM_INNER), lambda i: (0, 0))
    return _pc(body, name, (s // ts,), [row, row, vec, row], [row, row, vec],
               [_sds((s, SSM_INNER)), _sds((s, SSM_INNER), BF16), _sds((1, SSM_INNER))])(y, proj, nw, dyn)


def _head_layouts(v, s):
    return v.reshape(s, SSM_GROUPS, 4).transpose(1, 0, 2), v.T.reshape(SSM_GROUPS, 4, s)


def _ssd_fwd(x, h, w_in, conv_w, conv_b, dt_bias, a_log, d_skip, gnorm_w, w_out, next_norm):
    s = x.shape[0]
    proj = _mm(h, w_in, "nn", "ssd_in")
    xbc = _ssd_conv_fwd(proj, conv_w, conv_b, "ssd_conv")
    pad = ((0, 0), (0, LANES - SSM_HEADS))
    a_neg = -jnp.exp(a_log)
    bias = jnp.pad(dt_bias, pad)
    dt, acum = _ssd_dt_fwd(proj, bias, jnp.pad(a_neg, pad), "ssd_dt")
    dtc, drow = _head_layouts(dt[:, :SSM_HEADS], s)
    acol, arow = _head_layouts(acum[:, :SSM_HEADS], s)
    dskip = jnp.repeat(d_skip.reshape(SSM_GROUPS, 1, 4), HEAD_DIM, axis=2)
    y, hs = _ssd_scan_fwd(xbc, dtc, acol, drow, arow, dskip, "ssd_scan")
    yn = _gnorm_fwd(y, proj, gnorm_w, "ssd_gnorm")
    out, h_next = _mm(yn, w_out, "nn", "ssd_out", res=x, norm_w=next_norm)
    return out, h_next, (x, h, proj, xbc, bias, a_neg, dtc, acol, drow, arow, dskip, y, hs, yn)


def _ssd_bwd(dout, saved, norm_w, w_in, conv_w, conv_b, gnorm_w, w_out):
    x, h, proj, xbc, bias, a_neg, dtc, acol, drow, arow, dskip, y, hs, yn = saved
    s = x.shape[0]
    dyn = _mm(dout, w_out, "nt", "ssd_dyn")
    g_out = _mm(yn, dout, "tn", "ssd_gout", out_dtype=BF16)
    dy, dz, g_gnorm = _gnorm_bwd(y, proj, gnorm_w, dyn, "ssd_dgnorm")
    agrp = jnp.broadcast_to(a_neg.reshape(SSM_GROUPS, 4, 1), (SSM_GROUPS, 4, LANES))
    dxs, db, dc, ddt_row, da_acc, dd_acc = _ssd_scan_bwd(xbc, dtc, acol, drow, arow, dskip, agrp, hs, dy, "ssd_dscan")
    dxbc = jnp.concatenate([dxs, db, dc], axis=1)
    dpre, g_cw, g_cb = _ssd_conv_bwd(proj, dxbc, conv_w, conv_b, "ssd_dconv")
    ddt = jnp.pad(ddt_row.reshape(SSM_HEADS, s).T, ((0, 0), (0, LANES - SSM_HEADS)))
    ddtraw, g_dtb = _ssd_dt_bwd(proj, bias, ddt, "ssd_ddt")
    dproj = jnp.concatenate([dz, dpre, ddtraw], axis=1)
    g_in = _mm(h, dproj, "tn", "ssd_gin", out_dtype=BF16)
    dx, g_norm = _mm_dnorm_nt(dproj, w_in, "ssd_dh", x, norm_w, dout)
    g_alog = jnp.sum(da_acc, axis=-1).reshape(1, SSM_HEADS)
    g_d = jnp.sum(dd_acc.reshape(SSM_GROUPS, 4, HEAD_DIM), axis=-1).reshape(1, SSM_HEADS)
    return dx, g_norm, g_in[:, :SSM_IN], g_cw, g_cb, g_dtb[:, :SSM_HEADS], g_alog, g_d, g_gnorm, g_out


def _loss_head(y, target, name):
    s, d = y.shape
    ts = _row_tile(s)

    def body(y_ref, t_ref, dy_ref, l_ref):
        @pl.when(pl.program_id(0) == 0)
        def _():
            l_ref[...] = jnp.zeros_like(l_ref)

        e = y_ref[...] - t_ref[...]
        dy_ref[...] = e * (1.0 / d)
        part = jnp.sum(jnp.sum(e * e, axis=-1, keepdims=True), axis=0, keepdims=True) * (0.5 / d)
        l_ref[...] += jnp.broadcast_to(part, l_ref.shape)

    row = pl.BlockSpec((ts, d), lambda i: (i, 0))
    dy, lacc = _pc(body, name, (s // ts,), [row, row], [row, pl.BlockSpec((8, LANES), lambda i: (0, 0))],
                   [_sds((s, d)), _sds((8, LANES))])(y, target)
    return lacc[0, 0], dy


def _local_step(x, target, w, gather_fox=None, gather_rest=None, scatter_first=None, scatter_fox=None):
    saved = []
    received, received_fox = None, None
    h = _rms_fwd(x, w["mix_norm"][0:1], "first_norm")
    for i in range(DEPTH):
        kind, j = i % 3, i // 3
        fn = w["ffn_norm"][i:i + 1]
        if kind == 0:
            x, h, sv = _conv_fwd(x, h, w["conv_w_in"][j], w["conv_w_dw"][j], w["conv_w_out"][j], str(i), fn)
        elif kind == 1:
            hosted = None if gather_rest is None else (gather_rest[0], True)
            x, h, sv, got = _fox_fwd(x, h, w["fox_w_in"], w["fox_b_f"], w["fox_q_gain"], w["fox_k_gain"], w["fox_w_out"], fn, hosted)
            if gather_rest is not None:
                w = gather_rest[1](w, got)
        else:
            x, h, sv = _ssd_fwd(x, h, w["ssd_w_in"], w["ssd_conv_w"], w["ssd_conv_b"], w["ssd_dt_bias"],
                                w["ssd_a_log"], w["ssd_d"], w["ssd_norm_w"], w["ssd_w_out"], fn)
        hosted = (gather_fox[0], True) if (i == 0 and gather_fox is not None) else None
        nxt = w["mix_norm"][i + 1:i + 2] if i + 1 < DEPTH else None
        x, h, sf, got = _ffn_fwd(x, h, w["ffn_w_gu"][i], w["ffn_w_down"][i], str(i), nxt, hosted)
        if hosted is not None:
            w = gather_fox[1](w, got)
        saved.append((sv, sf))
    loss, dx = _loss_head(x, target, "loss_head")
    g = {k: [None] * n for k, n in (("mix_norm", DEPTH), ("ffn_norm", DEPTH), ("ffn_w_gu", DEPTH), ("ffn_w_down", DEPTH),
                                    ("conv_w_in", 2), ("conv_w_dw", 2), ("conv_w_out", 2))}
    for i in reversed(range(DEPTH)):
        kind, j = i % 3, i // 3
        sv, sf = saved[i]
        hosted = (scatter_fox(g), False) if (i == 0 and scatter_fox is not None) else None
        dx, g["ffn_norm"][i], g["ffn_w_gu"][i], g["ffn_w_down"][i], got = _ffn_bwd(
            dx, sf, w["ffn_norm"][i:i + 1], w["ffn_w_gu"][i], w["ffn_w_down"][i], str(i), hosted)
        if hosted is not None:
            received_fox = got
        mn = w["mix_norm"][i:i + 1]
        if kind == 0:
            dx, g["mix_norm"][i], g["conv_w_in"][j], g["conv_w_dw"][j], g["conv_w_out"][j] = _conv_bwd(
                dx, sv, mn, w["conv_w_in"][j], w["conv_w_dw"][j], w["conv_w_out"][j], str(i))
        elif kind == 1:
            hosted = None if scatter_first is None else (scatter_first(g), False)
            (dx, g["mix_norm"][i], g["fox_w_in"], g["fox_b_f"], g["fox_q_gain"], g["fox_k_gain"],
             g["fox_w_out"], received) = _fox_bwd(dx, sv, mn, w["fox_w_in"], w["fox_w_out"], hosted)
        else:
            (dx, g["mix_norm"][i], g["ssd_w_in"], g["ssd_conv_w"], g["ssd_conv_b"], g["ssd_dt_bias"], g["ssd_a_log"],
             g["ssd_d"], g["ssd_norm_w"], g["ssd_w_out"]) = _ssd_bwd(
                 dx, sv, mn, w["ssd_w_in"], w["ssd_conv_w"], w["ssd_conv_b"], w["ssd_norm_w"], w["ssd_w_out"])
    g["mix_norm"] = jnp.concatenate(g["mix_norm"], axis=0)
    g["ffn_norm"] = jnp.concatenate(g["ffn_norm"], axis=0)
    g["conv_w_dw"] = jnp.stack(g["conv_w_dw"], axis=0)
    g["ssd_conv_w"] = g["ssd_conv_w"][None]
    return loss, dx, g, received, received_fox


def _mesh_position():
    return lax.axis_index("x") * 4 + lax.axis_index("y") * 2 + lax.axis_index("c")


def _device_of(t):
    return (lax.shift_right_logical(t, 2), lax.bitwise_and(lax.shift_right_logical(t, 1), 1), lax.bitwise_and(t, 1))


def _exchange_copies(srcs_of, out_refs, send_sems, recv_sems, local_sems):
    me = _mesh_position()
    na = len(out_refs)
    locals_ = [pltpu.make_async_copy(srcs_of(a, me), out_refs[a].at[me], local_sems.at[a]) for a in range(na)]
    sends, arrivals = [], []
    for j in range(1, NDEV):
        t = lax.rem(me + j, NDEV)
        frm = lax.rem(me + NDEV - j, NDEV)
        for a in range(na):
            sends.append(pltpu.make_async_remote_copy(
                src_ref=srcs_of(a, t), dst_ref=out_refs[a].at[me], send_sem=send_sems.at[a, j - 1],
                recv_sem=recv_sems.at[a, j - 1], device_id=_device_of(t), device_id_type=pl.DeviceIdType.MESH))
            arrivals.append(pltpu.make_async_remote_copy(
                src_ref=srcs_of(a, me), dst_ref=out_refs[a].at[frm], send_sem=send_sems.at[a, j - 1],
                recv_sem=recv_sems.at[a, j - 1], device_id=_device_of(frm), device_id_type=pl.DeviceIdType.MESH))
    return locals_, sends, arrivals


def _exchange_start(copies):
    locals_, sends, _ = copies
    for cp in locals_ + sends:
        cp.start()


def _exchange_wait(copies):
    locals_, sends, arrivals = copies
    for cp in sends:
        cp.wait_send()
    for cp in arrivals:
        cp.wait_recv()
    for cp in locals_:
        cp.wait()


def _exchange_run(srcs_of, out_refs, send_sems, recv_sems, local_sems):
    copies = _exchange_copies(srcs_of, out_refs, send_sems, recv_sems, local_sems)
    _exchange_start(copies)
    _exchange_wait(copies)


def _exchange_parts(arrays, gather):
    na = len(arrays)
    outs = [_sds(((NDEV,) + a.shape) if gather else a.shape, a.dtype) for a in arrays]
    sems = [pltpu.SemaphoreType.DMA((na, NDEV - 1)), pltpu.SemaphoreType.DMA((na, NDEV - 1)), pltpu.SemaphoreType.DMA((na,))]
    pick = (lambda srcs: (lambda a, t: srcs[a])) if gather else (lambda srcs: (lambda a, t: srcs[a].at[t]))
    return pick, outs, sems


def _exchange(arrays, name, gather):
    na = len(arrays)
    pick, outs, sems = _exchange_parts(arrays, gather)

    def body(*refs):
        _exchange_run(pick(refs[:na]), refs[na:2 * na], *refs[2 * na:])

    hbm = pl.BlockSpec(memory_space=pl.ANY)
    return pl.pallas_call(body, name=name, in_specs=[hbm] * na, out_specs=[hbm] * na, out_shape=outs, scratch_shapes=sems)(*arrays)


def _all_sum_small(pack, name):
    def body(src_ref, out_ref, buf_ref, send_sems, recv_sems, local_sems):
        _exchange_run(lambda a, t: src_ref, [buf_ref], send_sems, recv_sems, local_sems)
        acc = buf_ref[0]
        for d in range(1, NDEV):
            acc = acc + buf_ref[d]
        out_ref[...] = acc

    vmem = pl.BlockSpec(memory_space=pltpu.VMEM)
    return pl.pallas_call(
        body, name=name, in_specs=[vmem], out_specs=vmem, out_shape=_sds(pack.shape, pack.dtype),
        scratch_shapes=[pltpu.VMEM((NDEV,) + pack.shape, pack.dtype), pltpu.SemaphoreType.DMA((1, NDEV - 1)),
                        pltpu.SemaphoreType.DMA((1, NDEV - 1)), pltpu.SemaphoreType.DMA((1,))])(pack)


def _sum_slabs(slabs, name):
    _, r, c = slabs.shape
    tr = r
    for cand in (256, 352):
        if r % cand == 0:
            tr = cand
            break

    def body(s_ref, o_ref):
        acc = s_ref[0].astype(F32)
        for d in range(1, NDEV):
            acc = acc + s_ref[d].astype(F32)
        o_ref[...] = acc

    return _pc(body, name, (r // tr,), [pl.BlockSpec((NDEV, tr, c), lambda i: (0, i, 0))],
               pl.BlockSpec((tr, c), lambda i: (i, 0)), _sds((r, c)))(slabs)


def _adamw(wt, g, m, v, name):
    shape = wt.shape
    w2, g2, m2, v2 = (a.reshape(-1, shape[-1]) for a in (wt, g, m, v))
    r, c = w2.shape
    tr = r
    for cand in (512, 352, 256):
        if r % cand == 0:
            tr = cand
            break
    c1 = 1.0 - ADAM_B1 ** ADAM_STEP
    c2 = 1.0 - ADAM_B2 ** ADAM_STEP

    def body(w_ref, g_ref, m_ref, v_ref, d_ref, mo_ref, vo_ref):
        gv = g_ref[...]
        mn = ADAM_B1 * m_ref[...] + (1.0 - ADAM_B1) * gv
        vn = ADAM_B2 * v_ref[...] + (1.0 - ADAM_B2) * (gv * gv)
        mo_ref[...] = mn
        vo_ref[...] = vn
        d_ref[...] = -ADAM_LR * ((mn / c1) / (jnp.sqrt(vn / c2) + ADAM_EPS) + ADAM_WD * w_ref[...])

    spec = pl.BlockSpec((tr, c), lambda i: (i, 0))
    outs = _pc(body, name, (r // tr,), [spec] * 4, [spec] * 3, [_sds((r, c))] * 3)(w2, g2, m2, v2)
    return tuple(o.reshape(shape) for o in outs)


_NAMES = ["mix_norm", "ffn_norm", "ffn_w_gu", "ffn_w_down", "conv_w_in", "conv_w_dw", "conv_w_out", "fox_w_in", "fox_b_f",
          "fox_q_gain", "fox_k_gain", "fox_w_out", "ssd_w_in", "ssd_conv_w", "ssd_conv_b", "ssd_dt_bias", "ssd_a_log",
          "ssd_d", "ssd_norm_w", "ssd_w_out"]
_MATRICES = ["ffn_w_gu", "ffn_w_down", "conv_w_in", "conv_w_out", "fox_w_in", "fox_w_out", "ssd_w_in", "ssd_w_out"]
_VECTORS = {"conv_w_dw": 2, "ssd_conv_w": 2, "ssd_conv_b": 1, "ssd_norm_w": 1}
_REPLICATED = ["mix_norm", "ffn_norm", "fox_b_f", "fox_q_gain", "fox_k_gain", "ssd_dt_bias", "ssd_a_log", "ssd_d"]


def _to_rows(flat):
    n = flat.shape[0]
    rows = -(-n // (8 * D_MODEL)) * 8
    return jnp.pad(flat, (0, rows * D_MODEL - n)).reshape(rows, D_MODEL)


def _full_shape(local_shape, axis):
    shp = list(local_shape)
    shp[axis] *= NDEV
    return tuple(shp)


def _cols_from_blocks(g):
    return jnp.moveaxis(g, 0, 1).reshape(g.shape[1], NDEV * g.shape[2])


def _blocks_from_cols(full):
    k, n8 = full.shape
    return jnp.moveaxis(full.reshape(k, NDEV, n8 // NDEV), 1, 0)


def kernel(x, mix_norm, ffn_norm, ffn_w_gu, ffn_w_down, conv_w_in, conv_w_dw, conv_w_out, fox_w_in, fox_b_f, fox_q_gain, fox_k_gain, fox_w_out, ssd_w_in, ssd_conv_w, ssd_conv_b, ssd_dt_bias, ssd_a_log, ssd_d, ssd_norm_w, ssd_w_out, loss_target, m_mix_norm, m_ffn_norm, m_ffn_w_gu, m_ffn_w_down, m_conv_w_in, m_conv_w_dw, m_conv_w_out, m_fox_w_in, m_fox_b_f, m_fox_q_gain, m_fox_k_gain, m_fox_w_out, m_ssd_w_in, m_ssd_conv_w, m_ssd_conv_b, m_ssd_dt_bias, m_ssd_a_log, m_ssd_d, m_ssd_norm_w, m_ssd_w_out, v_mix_norm, v_ffn_norm, v_ffn_w_gu, v_ffn_w_down, v_conv_w_in, v_conv_w_dw, v_conv_w_out, v_fox_w_in, v_fox_b_f, v_fox_q_gain, v_fox_k_gain, v_fox_w_out, v_ssd_w_in, v_ssd_conv_w, v_ssd_conv_b, v_ssd_dt_bias, v_ssd_a_log, v_ssd_d, v_ssd_norm_w, v_ssd_w_out):
    local = dict(mix_norm=mix_norm, ffn_norm=ffn_norm, ffn_w_gu=ffn_w_gu, ffn_w_down=ffn_w_down, conv_w_in=conv_w_in,
                 conv_w_dw=conv_w_dw, conv_w_out=conv_w_out, fox_w_in=fox_w_in, fox_b_f=fox_b_f, fox_q_gain=fox_q_gain,
                 fox_k_gain=fox_k_gain, fox_w_out=fox_w_out, ssd_w_in=ssd_w_in, ssd_conv_w=ssd_conv_w, ssd_conv_b=ssd_conv_b,
                 ssd_dt_bias=ssd_dt_bias, ssd_a_log=ssd_a_log, ssd_d=ssd_d, ssd_norm_w=ssd_norm_w, ssd_w_out=ssd_w_out)
    mom = dict(zip(_NAMES, [m_mix_norm, m_ffn_norm, m_ffn_w_gu, m_ffn_w_down, m_conv_w_in, m_conv_w_dw, m_conv_w_out, m_fox_w_in,
                            m_fox_b_f, m_fox_q_gain, m_fox_k_gain, m_fox_w_out, m_ssd_w_in, m_ssd_conv_w, m_ssd_conv_b,
                            m_ssd_dt_bias, m_ssd_a_log, m_ssd_d, m_ssd_norm_w, m_ssd_w_out]))
    var = dict(zip(_NAMES, [v_mix_norm, v_ffn_norm, v_ffn_w_gu, v_ffn_w_down, v_conv_w_in, v_conv_w_dw, v_conv_w_out, v_fox_w_in,
                            v_fox_b_f, v_fox_q_gain, v_fox_k_gain, v_fox_w_out, v_ssd_w_in, v_ssd_conv_w, v_ssd_conv_b,
                            v_ssd_dt_bias, v_ssd_a_log, v_ssd_d, v_ssd_norm_w, v_ssd_w_out]))

    shard = {k: local[k].astype(BF16) for k in _MATRICES}
    vec_pack = _to_rows(jnp.concatenate([local[k].reshape(-1) for k in _VECTORS]))
    first = _exchange([shard["ffn_w_gu"][0:1], shard["ffn_w_down"][0:1], shard["conv_w_in"][0:1], shard["conv_w_out"][0:1],
                       vec_pack], "gather_first", True)
    gvec = first[4].reshape(NDEV, -1)
    full = {k: local[k] for k in _REPLICATED}
    off = 0
    for k, axis in _VECTORS.items():
        n = local[k].size
        blk = jnp.moveaxis(gvec[:, off:off + n].reshape((NDEV,) + local[k].shape), 0, axis)
        full[k] = blk.reshape(_full_shape(local[k].shape, axis))
        off += n
    full["ssd_conv_w"] = full["ssd_conv_w"][0]
    full["ffn_w_gu"] = [first[0][:, 0]]
    full["ffn_w_down"] = [first[1][:, 0].reshape(4, FF_BLOCK, D_MODEL)]
    full["conv_w_in"] = [first[2][:, 0]]
    full["conv_w_out"] = [first[3][:, 0].reshape(D_MODEL, D_MODEL)]

    def finish_fox(w, got):
        w = dict(w)
        w["fox_w_in"] = jnp.pad(_cols_from_blocks(got[0][:, 0]), ((0, 0), (0, FOX_IN_PAD - FOX_IN)))
        w["fox_w_out"] = got[1].reshape(D_MODEL, D_MODEL)
        return w

    rest = [shard["ffn_w_gu"][1:], shard["ffn_w_down"][1:], shard["conv_w_in"][1:], shard["conv_w_out"][1:],
            shard["ssd_w_in"], shard["ssd_w_out"]]

    def finish(w, got):
        w = dict(w)
        w["ffn_w_gu"] = w["ffn_w_gu"] + [got[0][:, i] for i in range(DEPTH - 1)]
        w["ffn_w_down"] = w["ffn_w_down"] + [got[1][:, i].reshape(4, FF_BLOCK, D_MODEL) for i in range(DEPTH - 1)]
        w["conv_w_in"] = w["conv_w_in"] + [got[2][:, 0]]
        w["conv_w_out"] = w["conv_w_out"] + [got[3][:, 0].reshape(D_MODEL, D_MODEL)]
        w["ssd_w_in"] = jnp.pad(_cols_from_blocks(got[4][:, 0]), ((0, 0), (0, SSM_IN_PAD - SSM_IN)))
        w["ssd_w_out"] = got[5].reshape(SSM_INNER, D_MODEL)
        return w

    def early_slabs(g):
        return ([g["ffn_w_gu"][i] for i in range(1, DEPTH)]
                + [g["ffn_w_down"][i].reshape(NDEV, D_FF // NDEV, D_MODEL) for i in range(1, DEPTH)]
                + [g["conv_w_in"][1], g["conv_w_out"][1].reshape(NDEV, D_MODEL // NDEV, D_MODEL),
                   _blocks_from_cols(g["ssd_w_in"]), g["ssd_w_out"].reshape(NDEV, SSM_INNER // NDEV, D_MODEL)])

    def fox_slabs(g):
        return [_blocks_from_cols(g["fox_w_in"]), g["fox_w_out"].reshape(NDEV, D_MODEL // NDEV, D_MODEL)]

    loss_part, dx, grads, early, mid = _local_step(
        x[0], loss_target[0], full, ([shard["fox_w_in"], shard["fox_w_out"]], finish_fox), (rest, finish), early_slabs, fox_slabs)

    late = _exchange([grads["ffn_w_gu"][0], grads["ffn_w_down"][0].reshape(NDEV, D_FF // NDEV, D_MODEL), grads["conv_w_in"][0],
                      grads["conv_w_out"][0].reshape(NDEV, D_MODEL // NDEV, D_MODEL)], "scatter_last", False)
    se = [_sum_slabs(r, f"sum_early_{n}") for n, r in enumerate(early)]
    sm = [_sum_slabs(r, f"sum_mid_{n}") for n, r in enumerate(mid)]
    sl = [_sum_slabs(r, f"sum_late_{n}") for n, r in enumerate(late)]
    shard_grad = {
        "ffn_w_gu": jnp.stack([sl[0]] + se[0:3]), "ffn_w_down": jnp.stack([sl[1]] + se[3:6]),
        "conv_w_in": jnp.stack([sl[2], se[6]]), "conv_w_out": jnp.stack([sl[3], se[7]]),
        "fox_w_in": sm[0][None], "fox_w_out": sm[1][None], "ssd_w_in": se[8][None], "ssd_w_out": se[9][None]}

    small_names = _REPLICATED + list(_VECTORS)
    small = [jnp.reshape(loss_part, (1,))] + [grads[k].reshape(-1) for k in small_names]
    total = _all_sum_small(_to_rows(jnp.concatenate(small)), "sum_small").reshape(-1)
    loss = total[0]
    off = 1
    me = _mesh_position()
    for k, part in zip(small_names, small[1:]):
        gk = total[off:off + part.shape[0]]
        off += part.shape[0]
        if k in _VECTORS:
            axis = _VECTORS[k]
            shp = local[k].shape
            gfull = gk.reshape(shp[:axis] + (NDEV, shp[axis]) + shp[axis + 1:])
            shard_grad[k] = lax.dynamic_index_in_dim(gfull, me, axis, keepdims=False)
        else:
            shard_grad[k] = gk.reshape(local[k].shape)

    deltas, new_m, new_v = {}, {}, {}
    for k in _NAMES:
        deltas[k], new_m[k], new_v[k] = _adamw(local[k], shard_grad[k], mom[k], var[k], f"adamw_{k}")
    return (loss, dx[None], *[shard_grad[k] for k in _NAMES], *[deltas[k] for k in _NAMES],
            *[new_m[k] for k in _NAMES], *[new_v[k] for k in _NAMES])
```

```python
import numpy as np

import jax
import jax.numpy as jnp
from jax import lax
from jax.experimental import pallas as pl
from jax.experimental.pallas import tpu as pltpu

F32 = jnp.float32
BF16 = jnp.bfloat16
HI = lax.Precision.HIGHEST

NDEV = 8
D_MODEL = 1024
DEPTH = 4
D_FF = 2816
FF_BLOCK = 2 * D_FF // NDEV
RMS_EPS = 1e-6
HEAD_DIM = 64
ATTN_HEADS = 16
FOX_IN = 3 * D_MODEL + ATTN_HEADS
FOX_IN_PAD = 3200
SSM_INNER = 2048
SSM_HEADS = 32
SSM_GROUPS = 8
SSM_STATE = 128
SSM_CHUNK = 128
SSM_CONV_DIM = 4096
SSM_IN = SSM_INNER + SSM_CONV_DIM + SSM_HEADS
SSM_IN_PAD = 6272
LANES = 128
V7X_VMEM_BYTES = 64 * 1024 * 1024
VMEM_LIMIT_BYTES = (V7X_VMEM_BYTES * 3) // 4
ATTN_BWD_VMEM_BYTES = (V7X_VMEM_BYTES * 7) // 8
LOG2E = 1.4426950408889634
LN2 = 0.6931471805599453
ATTN_TILE = 1024
ATTN_ROWS = 32

ADAM_LR = 0.001
ADAM_B1 = 0.9
ADAM_B2 = 0.999
ADAM_EPS = 1e-08
ADAM_WD = 0.01
ADAM_STEP = 10

_TILE_CANDIDATES = (1024, 1408, 896, 768, 640, 512, 384, 256, 128)


def _pick_tile(n):
    for c in _TILE_CANDIDATES:
        if n % c == 0:
            return c
    raise ValueError(f"no tile for {n}")


def _params(ngrid):
    return pltpu.CompilerParams(dimension_semantics=("arbitrary",) * ngrid, vmem_limit_bytes=VMEM_LIMIT_BYTES)


def _pc(body, name, grid, in_specs, out_specs, out_shape, scratch=(), hosted=None):
    if hosted is None:
        return pl.pallas_call(
            body, name=name, grid=grid, in_specs=in_specs, out_specs=out_specs, out_shape=out_shape,
            scratch_shapes=list(scratch), compiler_params=_params(len(grid)))
    arrays, gather = hosted
    single = not isinstance(out_shape, (list, tuple))
    outs = [out_shape] if single else list(out_shape)
    ospecs = [out_specs] if single else list(out_specs)
    na, n_in, n_out, n_scr = len(arrays), len(in_specs), len(outs), len(scratch)
    pick, xouts, sems = _exchange_parts(arrays, gather)

    def run(*refs):
        ins, srcs = refs[:n_in], refs[n_in:n_in + na]
        res, dsts = refs[n_in + na:n_in + na + n_out], refs[n_in + na + n_out:n_in + 2 * na + n_out]
        scr, xsems = refs[n_in + 2 * na + n_out:n_in + 2 * na + n_out + n_scr], refs[n_in + 2 * na + n_out + n_scr:]
        first = pl.program_id(0) == 0
        last = pl.program_id(0) == grid[0] - 1
        for d in range(1, len(grid)):
            first = jnp.logical_and(first, pl.program_id(d) == 0)
            last = jnp.logical_and(last, pl.program_id(d) == grid[d] - 1)

        @pl.when(first)
        def _():
            _exchange_start(_exchange_copies(pick(srcs), dsts, *xsems))

        body(*ins, *res, *scr)

        @pl.when(last)
        def _():
            _exchange_wait(_exchange_copies(pick(srcs), dsts, *xsems))

    hbm = pl.BlockSpec(memory_space=pl.ANY)
    call = pl.pallas_call(
        run, name=name, grid=grid, in_specs=list(in_specs) + [hbm] * na, out_specs=ospecs + [hbm] * na,
        out_shape=outs + xouts, scratch_shapes=list(scratch) + sems, compiler_params=_params(len(grid)))
    return lambda *args: call(*args, *arrays)


def _dot(a, b, ca, cb, prec=None):
    return lax.dot_general(a, b, (((ca,), (cb,)), ((), ())), preferred_element_type=F32, precision=prec)


def _sds(shape, dtype=F32):
    return jax.ShapeDtypeStruct(shape, dtype)


def _row_tile(s, want=256):
    return want if s % want == 0 else s


def _sigmoid(x):
    return 1.0 / (1.0 + jnp.exp(-x))


def _softplus(x):
    return jnp.maximum(x, 0.0) + jnp.log(1.0 + jnp.exp(-jnp.abs(x)))


def _mm_spec(a, b, name, grid, a_spec, b_spec, o_spec, out, ca, cb, acc_shape, drop=(0, 0, 0), res=None, r_spec=None,
             norm_w=None, hosted=None):
    nk = grid[2]
    da, db, do_ = drop
    has_res = res is not None
    has_norm = norm_w is not None

    def body(*refs):
        refs = list(refs)
        a_ref, b_ref = refs[:2]
        r_ref = refs[2] if has_res else None
        w_ref = refs[2 + has_res] if has_norm else None
        o_ref = refs[2 + has_res + has_norm]
        h_ref = refs[3 + has_res + has_norm] if has_norm else None
        acc_ref = refs[-1]
        k = pl.program_id(2)

        @pl.when(k == 0)
        def _():
            acc_ref[...] = jnp.zeros_like(acc_ref)

        av = a_ref[(0,) * da] if da else a_ref[...]
        bv = b_ref[(0,) * db] if db else b_ref[...]
        acc_ref[...] += _dot(av.astype(BF16), bv.astype(BF16), ca, cb)

        @pl.when(k == nk - 1)
        def _():
            val = acc_ref[...]
            if has_res:
                val = val + r_ref[...]
            if do_:
                o_ref[(0,) * do_] = val.astype(out.dtype)
            else:
                o_ref[...] = val.astype(out.dtype)
            if has_norm:
                r = lax.rsqrt(jnp.mean(val * val, axis=-1, keepdims=True) + RMS_EPS)
                h_ref[...] = ((val * r) * w_ref[...]).astype(BF16)

    in_specs = [a_spec, b_spec] + ([r_spec] if has_res else [])
    args = (a, b) + ((res,) if has_res else ())
    out_specs, outs = o_spec, out
    if has_norm:
        assert acc_shape[1] == norm_w.shape[1] == out.shape[-1]
        in_specs.append(pl.BlockSpec((1, acc_shape[1]), lambda i, j, k: (0, 0)))
        args += (norm_w,)
        out_specs, outs = [o_spec, o_spec], [out, _sds(out.shape, BF16)]
    return _pc(body, name, grid, in_specs, out_specs, outs, [pltpu.VMEM(acc_shape, F32)], hosted=hosted)(*args)


def _mm(a, b, mode, name, out_dtype=F32, res=None, norm_w=None, hosted=None):
    if mode == "tn":
        r, m = a.shape
        n = b.shape[1]
        tm, tn, tk = _pick_tile(m), _pick_tile(n), _pick_tile(r)
        grid = (m // tm, n // tn, r // tk)
        a_spec = pl.BlockSpec((tk, tm), lambda i, j, k: (k, i))
        b_spec = pl.BlockSpec((tk, tn), lambda i, j, k: (k, j))
        ca, cb = 0, 0
    else:
        m, kd = a.shape
        n = b.shape[1] if mode == "nn" else b.shape[0]
        tm, tn, tk = _pick_tile(m), _pick_tile(n), _pick_tile(kd)
        grid = (m // tm, n // tn, kd // tk)
        a_spec = pl.BlockSpec((tm, tk), lambda i, j, k: (i, k))
        if mode == "nn":
            b_spec = pl.BlockSpec((tk, tn), lambda i, j, k: (k, j))
            ca, cb = 1, 0
        else:
            b_spec = pl.BlockSpec((tn, tk), lambda i, j, k: (j, k))
            ca, cb = 1, 1
    o_spec = pl.BlockSpec((tm, tn), lambda i, j, k: (i, j))
    return _mm_spec(a, b, name, grid, a_spec, b_spec, o_spec, _sds((m, n), out_dtype), ca, cb, (tm, tn), res=res, r_spec=o_spec,
                    norm_w=norm_w, hosted=hosted)


def _rms_fwd(x, w, name):
    s, d = x.shape
    ts = _row_tile(s)

    def body(x_ref, w_ref, o_ref):
        xv = x_ref[...]
        r = lax.rsqrt(jnp.mean(xv * xv, axis=-1, keepdims=True) + RMS_EPS)
        o_ref[...] = ((xv * r) * w_ref[...]).astype(BF16)

    row = pl.BlockSpec((ts, d), lambda i: (i, 0))
    return _pc(body, name, (s // ts,), [row, pl.BlockSpec((1, d), lambda i: (0, 0))], row, _sds((s, d), BF16))(x, w)


def _mm_dnorm(a, b, name, nk, a_spec, b_spec, ca, cb, drop, x, w, dres, hosted=None):
    s, d = x.shape
    tm = _pick_tile(s)
    da, db = drop

    def body(a_ref, b_ref, x_ref, w_ref, r_ref, dx_ref, dw_ref, acc_ref):
        i = pl.program_id(0)
        k = pl.program_id(2)

        @pl.when(k == 0)
        def _():
            acc_ref[...] = jnp.zeros_like(acc_ref)

        av = a_ref[(0,) * da] if da else a_ref[...]
        bv = b_ref[(0,) * db] if db else b_ref[...]
        acc_ref[...] += _dot(av.astype(BF16), bv.astype(BF16), ca, cb)

        @pl.when(k == nk - 1)
        def _():
            dhv = acc_ref[...]
            xv = x_ref[...]
            r = lax.rsqrt(jnp.mean(xv * xv, axis=-1, keepdims=True) + RMS_EPS)
            xhat = xv * r
            g = dhv * w_ref[...]
            dx_ref[...] = r_ref[...] + r * (g - xhat * jnp.mean(g * xhat, axis=-1, keepdims=True))
            part = jnp.sum(dhv * xhat, axis=0, keepdims=True)

            @pl.when(i == 0)
            def _():
                dw_ref[...] = part

            @pl.when(i > 0)
            def _():
                dw_ref[...] += part

    row = pl.BlockSpec((tm, d), lambda i, j, k: (i, 0))
    vec = pl.BlockSpec((1, d), lambda i, j, k: (0, 0))
    return list(_pc(body, name, (s // tm, 1, nk), [a_spec, b_spec, row, vec, row], [row, vec], [_sds((s, d)), _sds((1, d))],
                    [pltpu.VMEM((tm, d), F32)], hosted=hosted)(a, b, x, w, dres))


def _mm_dnorm_nt(dproj, w_in, name, x, w, dres, hosted=None):
    tm = _pick_tile(x.shape[0])
    tk = _pick_tile(dproj.shape[1])
    return _mm_dnorm(dproj, w_in, name, dproj.shape[1] // tk, pl.BlockSpec((tm, tk), lambda i, j, k: (i, k)),
                     pl.BlockSpec((D_MODEL, tk), lambda i, j, k: (0, k)), 1, 1, (0, 0), x, w, dres, hosted=hosted)


def _ffn_gate_up(h, w_gu, name, hosted=None):
    s = h.shape[0]
    tm = _pick_tile(s)

    def body(h_ref, wg_ref, wu_ref, gu_ref, a_ref):
        hv = h_ref[...]
        g = _dot(hv, wg_ref[0], 1, 0)
        u = _dot(hv, wu_ref[0], 1, 0)
        gu_ref[0, 0] = g.astype(BF16)
        gu_ref[0, 1] = u.astype(BF16)
        a_ref[0] = (g * _sigmoid(g) * u).astype(BF16)

    wblk = lambda off: pl.BlockSpec((1, D_MODEL, FF_BLOCK), lambda i, k: (k + off, 0, 0))
    return _pc(body, name, (s // tm, 4), [pl.BlockSpec((tm, D_MODEL), lambda i, k: (i, 0)), wblk(0), wblk(4)],
               [pl.BlockSpec((1, 2, tm, FF_BLOCK), lambda i, k: (k, 0, i, 0)), pl.BlockSpec((1, tm, FF_BLOCK), lambda i, k: (k, i, 0))],
               [_sds((4, 2, s, FF_BLOCK), BF16), _sds((4, s, FF_BLOCK), BF16)], hosted=hosted)(h, w_gu, w_gu)


def _ffn_dgate_up(dy, w_down, gu, name):
    s = dy.shape[0]
    tm = _pick_tile(s)

    def body(dy_ref, w_ref, gu_ref, o_ref):
        dav = _dot(dy_ref[...].astype(BF16), w_ref[0], 1, 1)
        g = gu_ref[0, 0].astype(F32)
        u = gu_ref[0, 1].astype(F32)
        sg = _sigmoid(g)
        o_ref[0, 0] = (dav * u * (sg * (1.0 + g * (1.0 - sg)))).astype(BF16)
        o_ref[0, 1] = (dav * (g * sg)).astype(BF16)

    pair = pl.BlockSpec((1, 2, tm, FF_BLOCK), lambda i, k: (k, 0, i, 0))
    return _pc(body, name, (s // tm, 4),
               [pl.BlockSpec((tm, D_MODEL), lambda i, k: (i, 0)), pl.BlockSpec((1, FF_BLOCK, D_MODEL), lambda i, k: (k, 0, 0)), pair],
               pair, _sds((4, 2, s, FF_BLOCK), BF16))(dy, w_down, gu)


def _ffn_fwd(x, h, w_gu, w_down, tag, next_norm=None, hosted=None):
    s = x.shape[0]
    tm = _pick_tile(s)
    gu, a, *got = _ffn_gate_up(h, w_gu, f"ffn_gu_{tag}", hosted)
    xspec = pl.BlockSpec((tm, D_MODEL), lambda i, j, k: (i, 0))
    y = _mm_spec(a, w_down, f"ffn_down_{tag}", (s // tm, 1, 4),
                 pl.BlockSpec((1, tm, FF_BLOCK), lambda i, j, k: (k, i, 0)),
                 pl.BlockSpec((1, FF_BLOCK, D_MODEL), lambda i, j, k: (k, 0, 0)),
                 xspec, _sds((s, D_MODEL)), 1, 0, (tm, D_MODEL), drop=(1, 1, 0), res=x, r_spec=xspec, norm_w=next_norm)
    y, h_next = y if next_norm is not None else (y, None)
    return y, h_next, (x, h, gu, a), got


def _ffn_bwd(dy, saved, norm_w, w_gu, w_down, tag, hosted=None):
    x, h, gu, a = saved
    s = x.shape[0]
    tm = _pick_tile(s)
    g_down = _mm_spec(a, dy, f"ffn_gdown_{tag}", (4, 1, s // tm),
                      pl.BlockSpec((1, tm, FF_BLOCK), lambda i, j, k: (i, k, 0)),
                      pl.BlockSpec((tm, D_MODEL), lambda i, j, k: (k, 0)),
                      pl.BlockSpec((1, FF_BLOCK, D_MODEL), lambda i, j, k: (i, 0, 0)),
                      _sds((4, FF_BLOCK, D_MODEL), BF16), 0, 0, (FF_BLOCK, D_MODEL), drop=(1, 0, 1))
    dgu = _ffn_dgate_up(dy, w_down, gu, f"ffn_dgu_{tag}")
    g_gu = _mm_spec(h, dgu, f"ffn_ggu_{tag}", (NDEV, 1, s // tm),
                    pl.BlockSpec((tm, D_MODEL), lambda i, j, k: (k, 0)),
                    pl.BlockSpec((1, 1, tm, FF_BLOCK), lambda i, j, k: (i % 4, i // 4, k, 0)),
                    pl.BlockSpec((1, D_MODEL, FF_BLOCK), lambda i, j, k: (i, 0, 0)),
                    _sds((NDEV, D_MODEL, FF_BLOCK), BF16), 0, 0, (D_MODEL, FF_BLOCK), drop=(0, 2, 1))
    dx, g_norm, *got = _mm_dnorm(dgu, w_gu, f"ffn_dh_{tag}", NDEV,
                                 pl.BlockSpec((1, 1, tm, FF_BLOCK), lambda i, j, k: (k % 4, k // 4, i, 0)),
                                 pl.BlockSpec((1, D_MODEL, FF_BLOCK), lambda i, j, k: (k, 0, 0)), 1, 1, (2, 1), x, norm_w, dy,
                                 hosted=hosted)
    return dx, g_norm, g_gu, g_down, got


def _prev_rows(cur, halo, j, first):
    rid = lax.broadcasted_iota(jnp.int32, cur.shape, 0)
    hid = lax.broadcasted_iota(jnp.int32, halo.shape, 0)
    out = pltpu.roll(cur, j, 0)
    for t in range(j):
        row = jnp.sum(jnp.where(hid == 8 - j + t, halo, 0.0), axis=0, keepdims=True)
        row = jnp.where(first, 0.0, row)
        out = jnp.where(rid == t, row, out)
    return out


def _next_rows(cur, halo, j, last):
    ts = cur.shape[0]
    rid = lax.broadcasted_iota(jnp.int32, cur.shape, 0)
    hid = lax.broadcasted_iota(jnp.int32, halo.shape, 0)
    out = pltpu.roll(cur, ts - j, 0)
    for t in range(j):
        row = jnp.sum(jnp.where(hid == t, halo, 0.0), axis=0, keepdims=True)
        row = jnp.where(last, 0.0, row)
        out = jnp.where(rid == ts - j + t, row, out)
    return out


def _halo_specs(ts, s, width, col):
    per = ts // 8
    nblk = s // 8
    prev = pl.BlockSpec((8, width), lambda i: (jnp.maximum(i * per - 1, 0), col))
    nxt = pl.BlockSpec((8, width), lambda i: (jnp.minimum((i + 1) * per, nblk - 1), col))
    return prev, nxt


def _cgate_fwd(p, w_dw, name):
    s = p.shape[0]
    d = D_MODEL
    ts = _row_tile(s)
    prev, _ = _halo_specs(ts, s, 3 * d, 0)

    def body(p_ref, h_ref, w_ref, z_ref):
        first = pl.program_id(0) == 0
        b = p_ref[:, :d]
        cv = p_ref[:, d:2 * d] * p_ref[:, 2 * d:]
        hcv = h_ref[:, d:2 * d] * h_ref[:, 2 * d:]
        u = w_ref[2:3, :] * cv + w_ref[1:2, :] * _prev_rows(cv, hcv, 1, first) + w_ref[0:1, :] * _prev_rows(cv, hcv, 2, first)
        z_ref[...] = (b * u).astype(BF16)

    return _pc(body, name, (s // ts,),
               [pl.BlockSpec((ts, 3 * d), lambda i: (i, 0)), prev, pl.BlockSpec((3, d), lambda i: (0, 0))],
               pl.BlockSpec((ts, d), lambda i: (i, 0)), _sds((s, d), BF16))(p, p, w_dw)


def _cgate_bwd(p, dz, w_dw, name):
    s = p.shape[0]
    d = D_MODEL
    ts = _row_tile(s)
    nt = s // ts
    p_prev, p_next = _halo_specs(ts, s, 3 * d, 0)
    _, dz_next = _halo_specs(ts, s, d, 0)

    def body(p_ref, hp_ref, hn_ref, dz_ref, dzn_ref, w_ref, dp_ref, dw_ref):
        i = pl.program_id(0)
        first = i == 0
        last = i == nt - 1
        b = p_ref[:, :d]
        c = p_ref[:, d:2 * d]
        v = p_ref[:, 2 * d:]
        cv = c * v
        hcv = hp_ref[:, d:2 * d] * hp_ref[:, 2 * d:]
        cv1 = _prev_rows(cv, hcv, 1, first)
        cv2 = _prev_rows(cv, hcv, 2, first)
        w0, w1, w2 = w_ref[0:1, :], w_ref[1:2, :], w_ref[2:3, :]
        u = w2 * cv + w1 * cv1 + w0 * cv2
        dzv = dz_ref[...]
        du = dzv * b
        dun = dzn_ref[...] * hn_ref[:, :d]
        dcv = w2 * du + w1 * _next_rows(du, dun, 1, last) + w0 * _next_rows(du, dun, 2, last)
        dp_ref[:, :d] = (dzv * u).astype(BF16)
        dp_ref[:, d:2 * d] = (dcv * v).astype(BF16)
        dp_ref[:, 2 * d:] = (dcv * c).astype(BF16)

        @pl.when(first)
        def _():
            dw_ref[...] = jnp.zeros_like(dw_ref)

        dw_ref[0:1, :] += jnp.sum(du * cv2, axis=0, keepdims=True)
        dw_ref[1:2, :] += jnp.sum(du * cv1, axis=0, keepdims=True)
        dw_ref[2:3, :] += jnp.sum(du * cv, axis=0, keepdims=True)

    wide = pl.BlockSpec((ts, 3 * d), lambda i: (i, 0))
    wspec = pl.BlockSpec((3, d), lambda i: (0, 0))
    return _pc(body, name, (nt,),
               [wide, p_prev, p_next, pl.BlockSpec((ts, d), lambda i: (i, 0)), dz_next, wspec],
               [wide, wspec], [_sds((s, 3 * d), BF16), _sds((3, d))])(p, p, p, dz, dz, w_dw)


def _conv_fwd(x, h, w_in, w_dw, w_out, tag, next_norm, hosted=None):
    wn = _cols_from_blocks(w_in)
    p = _mm(h, wn, "nn", f"conv_in_{tag}", hosted=hosted)
    p, got = (p[0], list(p[1:])) if hosted is not None else (p, [])
    z = _cgate_fwd(p, w_dw, f"conv_gate_{tag}")
    y, h_next = _mm(z, w_out, "nn", f"conv_out_{tag}", res=x, norm_w=next_norm)
    return y, h_next, (x, h, p, z, wn), got


def _conv_bwd(dy, saved, norm_w, w_in, w_dw, w_out, tag, hosted_fn=None):
    x, h, p, z, wn = saved
    dz = _mm(dy, w_out, "nt", f"conv_dz_{tag}")
    g_out = _mm(z, dy, "tn", f"conv_gout_{tag}", out_dtype=BF16)
    dp, g_dw = _cgate_bwd(p, dz, w_dw, f"conv_dgate_{tag}")
    g_in = _blocks_from_cols(_mm(h, dp, "tn", f"conv_gin_{tag}", out_dtype=BF16))
    hosted = hosted_fn(g_in, g_out) if hosted_fn is not None else None
    dx, g_norm, *got = _mm_dnorm_nt(dp, wn, f"conv_dh_{tag}", x, norm_w, dy, hosted=hosted)
    return dx, g_norm, g_in, g_dw, g_out, got


def _tri(lower):
    r = lax.broadcasted_iota(jnp.int32, (LANES, LANES), 0)
    c = lax.broadcasted_iota(jnp.int32, (LANES, LANES), 1)
    return jnp.where((r >= c) if lower else (r <= c), 1.0, 0.0).astype(F32)


def _cumsum_rows(v, reverse, name):
    s = v.shape[0]
    n = s // LANES
    idx = (lambda i: (n - 1 - i, 0)) if reverse else (lambda i: (i, 0))

    def body(v_ref, o_ref, carry_ref):
        @pl.when(pl.program_id(0) == 0)
        def _():
            carry_ref[...] = jnp.zeros_like(carry_ref)

        blk = v_ref[...]
        o_ref[...] = _dot(_tri(not reverse), blk, 1, 0, HI) + carry_ref[0:1, :]
        carry_ref[...] += jnp.sum(blk, axis=0, keepdims=True)

    spec = pl.BlockSpec((LANES, LANES), idx)
    return _pc(body, name, (n,), [spec], spec, _sds((s, LANES)), [pltpu.VMEM((8, LANES), F32)])(v)


def _lo_mask(shape):
    return lax.broadcasted_iota(jnp.int32, shape, len(shape) - 1) < HEAD_DIM


def _half_sums(v, lo):
    sa = jnp.sum(jnp.where(lo, v, 0.0), axis=-1, keepdims=True)
    sb = jnp.sum(jnp.where(lo, 0.0, v), axis=-1, keepdims=True)
    return jnp.where(lo, sa, sb)


def _fox_prep_fwd(proj, gq, gk, name):
    s = proj.shape[0]
    ts = _row_tile(s)
    qscale = HEAD_DIM ** -0.5 * LOG2E

    def body(q_ref, k_ref, v_ref, gq_ref, gk_ref, qo_ref, ko_ref, vo_ref):
        lo = _lo_mask((ts, LANES))

        def hnorm(xv, g):
            ms = _half_sums(xv * xv, lo) * (1.0 / HEAD_DIM)
            return (xv * lax.rsqrt(ms + RMS_EPS)) * g

        for p in range(8):
            cols = slice(p * LANES, (p + 1) * LANES)
            qo_ref[:, cols] = (hnorm(q_ref[:, cols], gq_ref[...]) * qscale).astype(BF16)
            ko_ref[:, cols] = hnorm(k_ref[:, cols], gk_ref[...]).astype(BF16)
        vo_ref[...] = v_ref[...].astype(BF16)

    def wide(blk):
        return pl.BlockSpec((ts, D_MODEL), lambda i: (i, blk))

    gspec = pl.BlockSpec((1, LANES), lambda i: (0, 0))
    out = _sds((s, D_MODEL), BF16)
    return _pc(body, name, (s // ts,), [wide(0), wide(1), wide(2), gspec, gspec], [wide(0)] * 3, [out] * 3)(
        proj, proj, proj, gq, gk)


def _fox_logf(proj, bf, name):
    s = proj.shape[0]
    ts = _row_tile(s, 512)

    def body(f_ref, b_ref, o_ref):
        z = f_ref[...] + b_ref[...]
        lf = jnp.minimum(z, 0.0) - jnp.log(1.0 + jnp.exp(-jnp.abs(z)))
        real = lax.broadcasted_iota(jnp.int32, (ts, LANES), 1) < ATTN_HEADS
        o_ref[...] = jnp.where(real, lf, 0.0)

    return _pc(body, name, (s // ts,), [pl.BlockSpec((ts, LANES), lambda i: (i, 24)), pl.BlockSpec((1, LANES), lambda i: (0, 0))],
               pl.BlockSpec((ts, LANES), lambda i: (i, 0)), _sds((s, LANES)))(proj, bf)


def _fox_dlogf(proj, bf, dlf, name):
    s = proj.shape[0]
    ts = _row_tile(s, 512)

    def body(f_ref, b_ref, d_ref, o_ref, db_ref):
        z = f_ref[...] + b_ref[...]
        real = lax.broadcasted_iota(jnp.int32, (ts, LANES), 1) < ATTN_HEADS
        g = jnp.where(real, d_ref[...] * _sigmoid(-z), 0.0)
        o_ref[...] = g.astype(BF16)

        @pl.when(pl.program_id(0) == 0)
        def _():
            db_ref[...] = jnp.zeros_like(db_ref)

        db_ref[...] += jnp.sum(g, axis=0, keepdims=True)

    vec = pl.BlockSpec((1, LANES), lambda i: (0, 0))
    row = pl.BlockSpec((ts, LANES), lambda i: (i, 0))
    return _pc(body, name, (s // ts,), [pl.BlockSpec((ts, LANES), lambda i: (i, 24)), vec, row], [row, vec],
               [_sds((s, LANES), BF16), _sds((1, LANES))])(proj, bf, dlf)


def _decay_terms(cum):
    s = cum.shape[0]
    c2 = cum * LOG2E
    hi = lax.reduce_precision(c2, 8, 7)
    mid = lax.reduce_precision(c2 - hi, 8, 7)
    low = lax.reduce_precision(c2 - hi - mid, 8, 7)
    one = jnp.ones_like(hi)

    def place(terms):
        tt = jnp.stack(terms, axis=-1).astype(BF16).reshape(s, 8, 2, 6)
        z = jnp.zeros((s, 8, HEAD_DIM - 6), BF16)
        return jnp.concatenate([tt[:, :, 1], z, tt[:, :, 0], z], axis=-1).reshape(s, D_MODEL)

    return place([hi, mid, low, one, one, one]), place([one, one, one, -hi, -mid, -low])


def _attn_tiles(s):
    t = s
    for cand in (ATTN_TILE, ATTN_TILE // 2):
        if s % cand == 0:
            t = cand
            break
    return t, s // t


def _tri_steps(n, by_key):
    if by_key:
        pairs = [(q, k) for k in range(n) for q in range(k, n)]
    else:
        pairs = [(q, k) for q in range(n) for k in range(q + 1)]
    arr = np.asarray(pairs, np.int32)
    return jnp.asarray(arr[:, 0]), jnp.asarray(arr[:, 1])


def _attn_call(body, name, s, by_key, inputs, in_kinds, out_kinds, out_shapes, scratch, hosted=None, vmem=VMEM_LIMIT_BYTES):
    t, n = _attn_tiles(s)
    qi_arr, ki_arr = _tri_steps(n, by_key)
    nsteps = int(qi_arr.shape[0])
    specs = {
        "q": pl.BlockSpec((t, LANES), lambda p, i, qi, ki: (qi[i], p)),
        "k": pl.BlockSpec((t, LANES), lambda p, i, qi, ki: (ki[i], p)),
        "r": pl.BlockSpec((1, 2, t), lambda p, i, qi, ki: (p, 0, qi[i])),
        "m": pl.BlockSpec((1, t, t), lambda p, i, qi, ki: (jnp.where(qi[i] == ki[i], 1, 0), 0, 0)),
        "Q": pl.BlockSpec((1, LANES, s), lambda p, i, qi, ki: (p, 0, 0)),
        "R": pl.BlockSpec((1, 2, s), lambda p, i, qi, ki: (p, 0, 0)),
    }
    in_specs = [specs[c] for c in in_kinds]
    out_specs = [specs[c] for c in out_kinds]
    out_shapes, scratch, inputs = list(out_shapes), list(scratch), list(inputs)
    run = body
    if hosted is not None:
        arrays, gather = hosted
        na, n_in, n_out, n_scr = len(arrays), len(inputs), len(out_kinds), len(scratch)
        pick, xouts, sems = _exchange_parts(arrays, gather)

        def run(qi_ref, ki_ref, *refs):
            ins, srcs = refs[:n_in], refs[n_in:n_in + na]
            outs, dsts = refs[n_in + na:n_in + na + n_out], refs[n_in + na + n_out:n_in + 2 * na + n_out]
            scr, xsems = refs[n_in + 2 * na + n_out:n_in + 2 * na + n_out + n_scr], refs[n_in + 2 * na + n_out + n_scr:]
            p = pl.program_id(0)
            i = pl.program_id(1)

            @pl.when(jnp.logical_and(p == 0, i == 0))
            def _():
                _exchange_start(_exchange_copies(pick(srcs), dsts, *xsems))

            body(qi_ref, ki_ref, *ins, *outs, *scr)

            @pl.when(jnp.logical_and(p == 7, i == nsteps - 1))
            def _():
                _exchange_wait(_exchange_copies(pick(srcs), dsts, *xsems))

        hbm = pl.BlockSpec(memory_space=pl.ANY)
        in_specs += [hbm] * na
        out_specs += [hbm] * na
        out_shapes += xouts
        scratch += sems
        inputs += list(arrays)
    grid_spec = pltpu.PrefetchScalarGridSpec(
        num_scalar_prefetch=2, grid=(8, nsteps), in_specs=in_specs, out_specs=out_specs, scratch_shapes=scratch)
    params = pltpu.CompilerParams(dimension_semantics=("arbitrary", "arbitrary"), vmem_limit_bytes=vmem)
    return pl.pallas_call(run, name=name, grid_spec=grid_spec, out_shape=out_shapes, compiler_params=params)(
        qi_arr, ki_arr, *inputs)


def _biased_kq(q2, k2, aq, ak, lo):
    sa = _dot(jnp.where(lo, k2, ak), jnp.where(lo, q2, aq), 1, 1)
    sb = _dot(jnp.where(lo, ak, k2), jnp.where(lo, aq, q2), 1, 1)
    return sa, sb


def _causal_bias(s):
    t, _ = _attn_tiles(s)
    kid = lax.broadcasted_iota(jnp.int32, (t, t), 0)
    qid = lax.broadcasted_iota(jnp.int32, (t, t), 1)
    return jnp.stack([jnp.zeros((t, t), BF16), jnp.where(kid > qid, -jnp.inf, 0.0).astype(BF16)])


def _fold8(v, op):
    return op(v.reshape(v.shape[0] // 8, 8, v.shape[1]), axis=0)


def _chunk(ref, mask_ref, hd, r):
    rows = slice(r * ATTN_ROWS, (r + 1) * ATTN_ROWS)
    return rows, ref[hd, rows, :] + mask_ref[0, rows, :].astype(F32)


def _flash_fwd(qs, kn, vb, augq, augk, cmask, name, hosted=None):
    s = qs.shape[0]
    t, n = _attn_tiles(s)
    nch = t // ATTN_ROWS

    def body(qi_ref, ki_ref, q_ref, k_ref, v_ref, aq_ref, ak_ref, mk_ref, o_ref, lse_ref, s_ref, p_ref, m_ref, l_ref, acc_ref):
        i = pl.program_id(1)
        qi = qi_ref[i]
        ki = ki_ref[i]

        @pl.when(ki == 0)
        def _():
            m_ref[...] = jnp.full_like(m_ref, -jnp.inf)
            l_ref[...] = jnp.zeros_like(l_ref)
            acc_ref[...] = jnp.zeros_like(acc_ref)

        lo = _lo_mask((t, LANES))
        rowlo = lax.broadcasted_iota(jnp.int32, (LANES, t), 0) < HEAD_DIM
        v2 = v_ref[...]
        sa, sb = _biased_kq(q_ref[...], k_ref[...], aq_ref[...], ak_ref[...], lo)
        s_ref[0] = sa
        s_ref[1] = sb
        alphas, pvs = [], []
        for hd in range(2):
            mx = jnp.full((8, t), -jnp.inf, F32)
            for r in range(nch):
                _, sc = _chunk(s_ref, mk_ref, hd, r)
                mx = jnp.maximum(mx, _fold8(sc, jnp.max))
            m_prev = m_ref[hd:hd + 1, :]
            m_new = jnp.maximum(m_prev, jnp.max(mx, axis=0, keepdims=True))
            ls = jnp.zeros((8, t), F32)
            for r in range(nch):
                rows, sc = _chunk(s_ref, mk_ref, hd, r)
                pm = jnp.exp2(sc - m_new)
                ls = ls + _fold8(pm, jnp.sum)
                p_ref[hd, rows, :] = pm.astype(BF16)
            alpha = jnp.exp2(m_prev - m_new)
            l_ref[hd:hd + 1, :] = alpha * l_ref[hd:hd + 1, :] + jnp.sum(ls, axis=0, keepdims=True)
            m_ref[hd:hd + 1, :] = m_new
            alphas.append(alpha)
            pvs.append(_dot(v2, p_ref[hd], 0, 0))
        acc_ref[...] = jnp.where(rowlo, alphas[0], alphas[1]) * acc_ref[...] + jnp.where(rowlo, pvs[0], pvs[1])

        @pl.when(ki == qi)
        def _():
            o_ref[...] = (acc_ref[...] / jnp.where(rowlo, l_ref[0:1, :], l_ref[1:2, :])).T
            lse_ref[0] = m_ref[0:2, :] + jnp.log2(l_ref[0:2, :])

    stat = pltpu.VMEM((8, t), F32)
    return _attn_call(body, name, s, False, (qs, kn, vb, augq, augk, cmask), "qkkqkm", "qr",
                      [_sds((s, D_MODEL)), _sds((8, 2, s))],
                      [pltpu.VMEM((2, t, t), F32), pltpu.VMEM((2, t, t), BF16), stat, stat, pltpu.VMEM((LANES, t), F32)],
                      hosted=hosted)


def _fox_delta(do, o, name):
    s = do.shape[0]
    ts = _row_tile(s)

    def body(do_ref, o_ref, d_ref):
        lo = _lo_mask((ts, LANES))
        for p in range(8):
            cols = slice(p * LANES, (p + 1) * LANES)
            d_ref[:, cols] = _half_sums(do_ref[:, cols] * o_ref[:, cols], lo)

    spec = pl.BlockSpec((ts, D_MODEL), lambda i: (i, 0))
    return _pc(body, name, (s // ts,), [spec, spec], spec, _sds((s, D_MODEL)))(do, o)


def _bwd_tile(q_ref, k_ref, v_ref, aq_ref, ak_ref, do_ref, s_ref, dp_ref, lo):
    do2 = do_ref[...].astype(BF16)
    zero = jnp.zeros_like(do2)
    v2 = v_ref[...]
    sa, sb = _biased_kq(q_ref[...], k_ref[...], aq_ref[...], ak_ref[...], lo)
    s_ref[0] = sa
    s_ref[1] = sb
    dp_ref[0] = _dot(v2, jnp.where(lo, do2, zero), 1, 1)
    dp_ref[1] = _dot(v2, jnp.where(lo, zero, do2), 1, 1)
    return do2


def _bwd_chunk(s_ref, dp_ref, mk_ref, lse_ref, dl_ref, hd, r):
    rows, sc = _chunk(s_ref, mk_ref, hd, r)
    pm = jnp.exp2(sc - lse_ref[0, hd:hd + 1, :])
    ds = pm * (dp_ref[hd, rows, :] - dl_ref[0, hd:hd + 1, :])
    return rows, pm, ds


def _flash_bwd(qs, kn, vb, augq, augk, cmask, do, lse, delta, name, hosted=None):
    s = qs.shape[0]
    t, n = _attn_tiles(s)
    nch = t // ATTN_ROWS

    def body(qi_ref, ki_ref, q_ref, k_ref, v_ref, aq_ref, ak_ref, mk_ref, do_ref, lse_ref, dl_ref,
             dk_ref, dv_ref, dc_ref, dq_ref, dcq_ref, s_ref, dp_ref, p_ref, ds_ref, dka_ref, dva_ref, dca_ref):
        i = pl.program_id(1)
        qi = qi_ref[i]
        ki = ki_ref[i]

        @pl.when(i == 0)
        def _():
            dq_ref[...] = jnp.zeros_like(dq_ref)
            dcq_ref[...] = jnp.zeros_like(dcq_ref)

        @pl.when(qi == ki)
        def _():
            dka_ref[...] = jnp.zeros_like(dka_ref)
            dva_ref[...] = jnp.zeros_like(dva_ref)
            dca_ref[...] = jnp.zeros_like(dca_ref)

        lo = _lo_mask((t, LANES))
        rowlo = lax.broadcasted_iota(jnp.int32, (LANES, t), 0) < HEAD_DIM
        do2 = _bwd_tile(q_ref, k_ref, v_ref, aq_ref, ak_ref, do_ref, s_ref, dp_ref, lo)
        q2 = q_ref[...]
        k2 = k_ref[...]
        qcols = pl.ds(pl.multiple_of(qi * t, t), t)
        dvs, dks, dqs = [], [], []
        for hd in range(2):
            rs = jnp.zeros((8, t), F32)
            for r in range(nch):
                rows, pm, ds = _bwd_chunk(s_ref, dp_ref, mk_ref, lse_ref, dl_ref, hd, r)
                rs = rs + _fold8(ds, jnp.sum)
                part = ds[:, 0:LANES]
                for c in range(1, t // LANES):
                    part = part + ds[:, c * LANES:(c + 1) * LANES]
                dca_ref[hd, rows, :] += part
                p_ref[hd, rows, :] = pm.astype(BF16)
                ds_ref[hd, rows, :] = ds.astype(BF16)
            dcq_ref[0, hd:hd + 1, qcols] += jnp.sum(rs, axis=0, keepdims=True)
            dvs.append(_dot(p_ref[hd], do2, 1, 0))
            dks.append(_dot(ds_ref[hd], q2, 1, 0))
            dqs.append(_dot(k2, ds_ref[hd], 0, 0))
        dva_ref[...] += jnp.where(lo, dvs[0], dvs[1])
        dka_ref[...] += jnp.where(lo, dks[0], dks[1])
        dq_ref[0, :, qcols] += jnp.where(rowlo, dqs[0], dqs[1])

        @pl.when(qi == n - 1)
        def _():
            dk_ref[...] = dka_ref[...] * LN2
            dv_ref[...] = dva_ref[...]
            dc_ref[...] = -jnp.where(lo, jnp.sum(dca_ref[0], axis=-1, keepdims=True), jnp.sum(dca_ref[1], axis=-1, keepdims=True))

    out = _sds((s, D_MODEL))
    return _attn_call(body, name, s, True, (qs, kn, vb, augq, augk, cmask, do, lse, delta), "qkkqkmqrr", "kkkQR",
                      [out, out, out, _sds((8, LANES, s)), _sds((8, 2, s))],
                      [pltpu.VMEM((2, t, t), F32), pltpu.VMEM((2, t, t), F32), pltpu.VMEM((2, t, t), BF16),
                       pltpu.VMEM((2, t, t), BF16), pltpu.VMEM((t, LANES), F32), pltpu.VMEM((t, LANES), F32),
                       pltpu.VMEM((2, t, LANES), F32)], hosted=hosted, vmem=ATTN_BWD_VMEM_BYTES)


def _fox_prep_bwd(proj, dqs, dk, dv, gq, gk, name):
    s = proj.shape[0]
    ts = _row_tile(s)
    scale = HEAD_DIM ** -0.5

    def body(q_ref, k_ref, dq_ref, dk_ref, dv_ref, gq_ref, gk_ref, oq_ref, ok_ref, ov_ref, dgq_ref, dgk_ref):
        lo = _lo_mask((ts, LANES))

        @pl.when(pl.program_id(0) == 0)
        def _():
            dgq_ref[...] = jnp.zeros_like(dgq_ref)
            dgk_ref[...] = jnp.zeros_like(dgk_ref)

        def back(xv, dout, g):
            r = lax.rsqrt(_half_sums(xv * xv, lo) * (1.0 / HEAD_DIM) + RMS_EPS)
            y = xv * r
            dy = dout * g
            dx = r * (dy - y * (_half_sums(dy * y, lo) * (1.0 / HEAD_DIM)))
            return dx, jnp.sum(dout * y, axis=0, keepdims=True)

        for p in range(8):
            cols = slice(p * LANES, (p + 1) * LANES)
            dxq, dgq = back(q_ref[:, cols], dq_ref[p].T * scale, gq_ref[...])
            dxk, dgk = back(k_ref[:, cols], dk_ref[:, cols], gk_ref[...])
            oq_ref[:, cols] = dxq.astype(BF16)
            ok_ref[:, cols] = dxk.astype(BF16)
            dgq_ref[...] += dgq
            dgk_ref[...] += dgk
        ov_ref[...] = dv_ref[...].astype(BF16)

    def wide(blk):
        return pl.BlockSpec((ts, D_MODEL), lambda i: (i, blk))

    gspec = pl.BlockSpec((1, LANES), lambda i: (0, 0))
    out = _sds((s, D_MODEL), BF16)
    dqt = pl.BlockSpec((8, LANES, ts), lambda i: (0, 0, i))
    return _pc(body, name, (s // ts,), [wide(0), wide(1), dqt, wide(0), wide(0), gspec, gspec],
               [wide(0)] * 3 + [gspec] * 2, [out] * 3 + [_sds((1, LANES))] * 2)(proj, proj, dqs, dk, dv, gq, gk)


def _fox_fwd(x, h, w_in, b_f, q_gain, k_gain, w_out, next_norm, hosted=None):
    proj = _mm(h, w_in, "nn", "fox_in")
    gq = jnp.tile(q_gain, (1, 2))
    gk = jnp.tile(k_gain, (1, 2))
    bf = jnp.pad(b_f, ((0, 0), (0, LANES - ATTN_HEADS)))
    qs, kn, vb = _fox_prep_fwd(proj, gq, gk, "fox_prep")
    cum = _cumsum_rows(_fox_logf(proj, bf, "fox_logf"), False, "fox_cum")[:, :ATTN_HEADS]
    augq, augk = _decay_terms(cum)
    cmask = _causal_bias(x.shape[0])
    o, lse, *got = _flash_fwd(qs, kn, vb, augq, augk, cmask, "fox_attn", hosted=hosted)
    y, h_next = _mm(o, w_out, "nn", "fox_out", res=x, norm_w=next_norm)
    return y, h_next, (x, h, proj, gq, gk, bf, qs, kn, vb, augq, augk, cmask, o, lse), got


def _fox_bwd(dy, saved, norm_w, w_in, w_out, hosted=None):
    x, h, proj, gq, gk, bf, qs, kn, vb, augq, augk, cmask, o, lse = saved
    s = x.shape[0]
    do = _mm(dy, w_out, "nt", "fox_do")
    g_out = _mm(o, dy, "tn", "fox_gout", out_dtype=BF16)
    delta = _fox_delta(do, o, "fox_delta")[:, ::HEAD_DIM].T.reshape(8, 2, s)
    dk, dv, dck, dqs, dcq, *got = _flash_bwd(qs, kn, vb, augq, augk, cmask, do, lse, delta, "fox_dattn", hosted=hosted)
    dcum = jnp.pad(dcq.reshape(ATTN_HEADS, s).T + dck[:, ::HEAD_DIM], ((0, 0), (0, LANES - ATTN_HEADS)))
    dlf = _cumsum_rows(dcum, True, "fox_dcum")
    dfl, g_bf = _fox_dlogf(proj, bf, dlf, "fox_dlogf")
    dq_o, dk_o, dv_o, g_gq, g_gk = _fox_prep_bwd(proj, dqs, dk, dv, gq, gk, "fox_dprep")
    dproj = jnp.concatenate([dq_o, dk_o, dv_o, dfl], axis=1)
    g_in = _mm(h, dproj, "tn", "fox_gin", out_dtype=BF16)
    dx, g_norm = _mm_dnorm_nt(dproj, w_in, "fox_dh", x, norm_w, dy)
    g_q = g_gq[:, :HEAD_DIM] + g_gq[:, HEAD_DIM:]
    g_k = g_gk[:, :HEAD_DIM] + g_gk[:, HEAD_DIM:]
    return dx, g_norm, g_in[:, :FOX_IN], g_bf[:, :ATTN_HEADS], g_q, g_k, g_out, got


def _ssd_conv_fwd(proj, cw, cb, name):
    s = proj.shape[0]
    ts = _row_tile(s)
    w = 1024
    per = ts // 8

    def body(p_ref, h_ref, w_ref, b_ref, o_ref):
        first = pl.program_id(0) == 0
        cur = p_ref[...]
        halo = h_ref[...]
        u = w_ref[3:4, :] * cur + b_ref[...]
        for j in range(1, 4):
            u = u + w_ref[3 - j:4 - j, :] * _prev_rows(cur, halo, j, first)
        o_ref[...] = u * _sigmoid(u)

    return _pc(body, name, (s // ts, 4),
               [pl.BlockSpec((ts, w), lambda i, j: (i, 2 + j)),
                pl.BlockSpec((8, w), lambda i, j: (jnp.maximum(i * per - 1, 0), 2 + j)),
                pl.BlockSpec((4, w), lambda i, j: (0, j)), pl.BlockSpec((1, w), lambda i, j: (0, j))],
               pl.BlockSpec((ts, w), lambda i, j: (i, j)), _sds((s, SSM_CONV_DIM)))(proj, proj, cw, cb)


def _ssd_conv_bwd(proj, dxbc, cw, cb, name):
    s = proj.shape[0]
    ts = _row_tile(s)
    nt = s // ts
    w = 1024
    per = ts // 8
    nblk = s // 8

    def body(p_ref, hp_ref, hn_ref, d_ref, dn_ref, w_ref, b_ref, o_ref, dw_ref, db_ref):
        i = pl.program_id(1)
        first = i == 0
        last = i == nt - 1
        cur = p_ref[...]
        prev = [cur] + [_prev_rows(cur, hp_ref[...], j, first) for j in range(1, 4)]
        nxt = hn_ref[...]
        tail = cur[ts - 8:, :]
        u = b_ref[...]
        un = b_ref[...]
        for j in range(4):
            u = u + w_ref[3 - j:4 - j, :] * prev[j]
            un = un + w_ref[3 - j:4 - j, :] * (nxt if j == 0 else _prev_rows(nxt, tail, j, False))
        sg = _sigmoid(u)
        g = d_ref[...] * (sg * (1.0 + u * (1.0 - sg)))
        sn = _sigmoid(un)
        gn = dn_ref[...] * (sn * (1.0 + un * (1.0 - sn)))

        @pl.when(first)
        def _():
            dw_ref[...] = jnp.zeros_like(dw_ref)
            db_ref[...] = jnp.zeros_like(db_ref)

        dpre = w_ref[3:4, :] * g
        for j in range(1, 4):
            dpre = dpre + w_ref[3 - j:4 - j, :] * _next_rows(g, gn, j, last)
        for j in range(4):
            dw_ref[3 - j:4 - j, :] += jnp.sum(g * prev[j], axis=0, keepdims=True)
        db_ref[...] += jnp.sum(g, axis=0, keepdims=True)
        o_ref[...] = dpre.astype(BF16)

    tile = pl.BlockSpec((ts, w), lambda j, i: (i, j))
    wspec = pl.BlockSpec((4, w), lambda j, i: (0, j))
    vec = pl.BlockSpec((1, w), lambda j, i: (0, j))
    nxt_blk = lambda off: pl.BlockSpec((8, w), lambda j, i: (jnp.minimum((i + 1) * per, nblk - 1), off + j))
    return _pc(body, name, (4, nt),
               [pl.BlockSpec((ts, w), lambda j, i: (i, 2 + j)),
                pl.BlockSpec((8, w), lambda j, i: (jnp.maximum(i * per - 1, 0), 2 + j)), nxt_blk(2),
                tile, nxt_blk(0), wspec, vec],
               [tile, wspec, vec], [_sds((s, SSM_CONV_DIM), BF16), _sds((4, SSM_CONV_DIM)), _sds((1, SSM_CONV_DIM))])(
                   proj, proj, proj, dxbc, dxbc, cw, cb)


def _ssd_dt_fwd(proj, bias, a_neg, name):
    s = proj.shape[0]
    n = s // SSM_CHUNK

    def body(r_ref, b_ref, a_ref, dt_ref, ac_ref):
        real = lax.broadcasted_iota(jnp.int32, (SSM_CHUNK, LANES), 1) < SSM_HEADS
        dt = jnp.where(real, _softplus(r_ref[...] + b_ref[...]), 0.0)
        dt_ref[...] = dt
        ac_ref[...] = _dot(_tri(True), dt * a_ref[...], 1, 0, HI)

    vec = pl.BlockSpec((1, LANES), lambda c: (0, 0))
    row = pl.BlockSpec((SSM_CHUNK, LANES), lambda c: (c, 0))
    return _pc(body, name, (n,), [pl.BlockSpec((SSM_CHUNK, LANES), lambda c: (c, 48)), vec, vec], [row, row],
               [_sds((s, LANES)), _sds((s, LANES))])(proj, bias, a_neg)


def _ssd_dt_bwd(proj, bias, ddt, name):
    s = proj.shape[0]
    ts = _row_tile(s, 512)

    def body(r_ref, b_ref, d_ref, o_ref, db_ref):
        real = lax.broadcasted_iota(jnp.int32, (ts, LANES), 1) < SSM_HEADS
        g = jnp.where(real, d_ref[...] * _sigmoid(r_ref[...] + b_ref[...]), 0.0)
        o_ref[...] = g.astype(BF16)

        @pl.when(pl.program_id(0) == 0)
        def _():
            db_ref[...] = jnp.zeros_like(db_ref)

        db_ref[...] += jnp.sum(g, axis=0, keepdims=True)

    vec = pl.BlockSpec((1, LANES), lambda i: (0, 0))
    row = pl.BlockSpec((ts, LANES), lambda i: (i, 0))
    return _pc(body, name, (s // ts,), [pl.BlockSpec((ts, LANES), lambda i: (i, 48)), vec, row], [row, vec],
               [_sds((s, LANES), BF16), _sds((1, LANES))])(proj, bias, ddt)


def _pair_cols(cols, k0, lo):
    return jnp.where(lo, cols[:, k0:k0 + 1], cols[:, k0 + 1:k0 + 2])


def _last_lane(row):
    lane = lax.broadcasted_iota(jnp.int32, row.shape, 1)
    return jnp.sum(jnp.where(lane == SSM_CHUNK - 1, row, 0.0), axis=-1, keepdims=True)


SSD_FWD_GROUPS = 2
SSD_BWD_GROUPS = 1


def _ssd_specs(nc, rev, n):
    cc = (lambda c: nc - 1 - c) if rev else (lambda c: c)
    nb = SSM_INNER // (LANES * n)
    return dict(
        x=pl.BlockSpec((SSM_CHUNK, 256 * n), lambda g, c: (cc(c), g)),
        b=pl.BlockSpec((SSM_CHUNK, LANES * n), lambda g, c: (cc(c), nb + g)),
        c=pl.BlockSpec((SSM_CHUNK, LANES * n), lambda g, c: (cc(c), nb + SSM_GROUPS // n + g)),
        col=pl.BlockSpec((n, SSM_CHUNK, 4), lambda g, c: (g, cc(c), 0)),
        row=pl.BlockSpec((n, 4, SSM_CHUNK), lambda g, c: (g, 0, cc(c))),
        grp=pl.BlockSpec((n, 1, 256), lambda g, c: (g, 0, 0)),
        grow=pl.BlockSpec((n, 4, LANES), lambda g, c: (g, 0, 0)),
        hs=pl.BlockSpec((1, n, 256, SSM_STATE), lambda g, c: (cc(c), g, 0, 0)),
        bc=pl.BlockSpec((SSM_CHUNK, LANES * n), lambda g, c: (cc(c), g)),
    )


def _ssd_scan_fwd(xbc, dtc, acol, drow, arow, dskip, name):
    s = xbc.shape[0]
    nc = s // SSM_CHUNK
    n = SSD_FWD_GROUPS
    sp = _ssd_specs(nc, False, n)
    L = SSM_CHUNK

    def body(x_ref, b_ref, c_ref, dtc_ref, ac_ref, dr_ref, ar_ref, dk_ref, y_ref, hs_ref, h_ref):
        @pl.when(pl.program_id(1) == 0)
        def _():
            h_ref[...] = jnp.zeros_like(h_ref)

        for gi in range(n):
            group(gi, x_ref, b_ref, c_ref, dtc_ref, ac_ref, dr_ref, ar_ref, dk_ref, y_ref, hs_ref, h_ref)

    def group(gi, x_ref, b_ref, c_ref, dtc_ref, ac_ref, dr_ref, ar_ref, dk_ref, y_ref, hs_ref, h_ref):
        x0 = gi * 256
        bb = b_ref[:, gi * LANES:(gi + 1) * LANES].astype(BF16)
        cb = c_ref[:, gi * LANES:(gi + 1) * LANES].astype(BF16)
        gm = _dot(cb, bb, 1, 1)
        dtc = dtc_ref[gi]
        ac = ac_ref[gi]
        dr = dr_ref[gi]
        ar = ar_ref[gi]
        dsk = dk_ref[gi]
        hs_ref[0, gi] = h_ref[gi]
        tril = lax.broadcasted_iota(jnp.int32, (L, L), 0) >= lax.broadcasted_iota(jnp.int32, (L, L), 1)
        lo = _lo_mask((L, LANES))
        rowlo = lax.broadcasted_iota(jnp.int32, (L, LANES), 0) < HEAD_DIM
        for pr in range(2):
            k0 = 2 * pr
            xp = x_ref[:, x0 + pr * LANES:x0 + (pr + 1) * LANES]
            xpb = xp.astype(BF16)
            hp = h_ref[gi, pr * LANES:(pr + 1) * LANES, :]
            yd, al = [], []
            for k in (k0, k0 + 1):
                seg = ac[:, k:k + 1] - ar[k:k + 1, :]
                wk = gm * jnp.exp(jnp.where(tril, seg, -jnp.inf)) * dr[k:k + 1, :]
                yd.append(_dot(wk.astype(BF16), xpb, 1, 0))
                al.append(_last_lane(ar[k:k + 1, :]))
            e = jnp.exp(_pair_cols(ac, k0, lo))
            yo = _dot(cb, hp.astype(BF16), 1, 1) * e
            y_ref[:, x0 + pr * LANES:x0 + (pr + 1) * LANES] = (
                jnp.where(lo, yd[0], yd[1]) + yo + dsk[:, pr * LANES:(pr + 1) * LANES] * xp)
            wp = jnp.where(lo, jnp.exp(al[0] - ac[:, k0:k0 + 1]) * dtc[:, k0:k0 + 1],
                           jnp.exp(al[1] - ac[:, k0 + 1:k0 + 2]) * dtc[:, k0 + 1:k0 + 2])
            st = _dot((xp * wp).astype(BF16), bb, 0, 0)
            dec = jnp.where(rowlo, jnp.exp(al[0]), jnp.exp(al[1]))
            h_ref[gi, pr * LANES:(pr + 1) * LANES, :] = dec * hp + st

    return _pc(body, name, (SSM_GROUPS // n, nc),
               [sp["x"], sp["b"], sp["c"], sp["col"], sp["col"], sp["row"], sp["row"], sp["grp"]],
               [sp["x"], sp["hs"]], [_sds((s, SSM_INNER)), _sds((nc, SSM_GROUPS, 256, SSM_STATE))],
               [pltpu.VMEM((n, 256, SSM_STATE), F32)])(xbc, xbc, xbc, dtc, acol, drow, arow, dskip)


def _ssd_scan_bwd(xbc, dtc, acol, drow, arow, dskip, agrp, hs, dy, name):
    s = xbc.shape[0]
    nc = s // SSM_CHUNK
    n = SSD_BWD_GROUPS
    sp = _ssd_specs(nc, True, n)
    L = SSM_CHUNK

    def body(x_ref, b_ref, c_ref, dtc_ref, ac_ref, dr_ref, ar_ref, dk_ref, ag_ref, hs_ref, dy_ref,
             dx_ref, db_ref, dc_ref, ddt_ref, da_ref, dd_ref, dh_ref):
        @pl.when(pl.program_id(1) == 0)
        def _():
            dh_ref[...] = jnp.zeros_like(dh_ref)
            da_ref[...] = jnp.zeros_like(da_ref)
            dd_ref[...] = jnp.zeros_like(dd_ref)

        for gi in range(n):
            group(gi, x_ref, b_ref, c_ref, dtc_ref, ac_ref, dr_ref, ar_ref, dk_ref, ag_ref, hs_ref, dy_ref,
                  dx_ref, db_ref, dc_ref, ddt_ref, da_ref, dd_ref, dh_ref)

    def group(gi, x_ref, b_ref, c_ref, dtc_ref, ac_ref, dr_ref, ar_ref, dk_ref, ag_ref, hs_ref, dy_ref,
              dx_ref, db_ref, dc_ref, ddt_ref, da_ref, dd_ref, dh_ref):
        x0 = gi * 256
        bcols = slice(gi * LANES, (gi + 1) * LANES)
        bb = b_ref[:, bcols].astype(BF16)
        cb = c_ref[:, bcols].astype(BF16)
        gm = _dot(cb, bb, 1, 1)
        dtc = dtc_ref[gi]
        ac = ac_ref[gi]
        dr = dr_ref[gi]
        ar = ar_ref[gi]
        dsk = dk_ref[gi]
        ag = ag_ref[gi]
        tril = lax.broadcasted_iota(jnp.int32, (L, L), 0) >= lax.broadcasted_iota(jnp.int32, (L, L), 1)
        lo = _lo_mask((L, LANES))
        nlo = jnp.logical_not(lo)
        rowlo = lax.broadcasted_iota(jnp.int32, (L, LANES), 0) < HEAD_DIM
        lane = lax.broadcasted_iota(jnp.int32, (L, LANES), 1)
        lane_row = lax.broadcasted_iota(jnp.int32, (1, LANES), 1)
        dgm = jnp.zeros((L, L), F32)
        dcm = jnp.zeros((L, SSM_STATE), F32)
        dbm = jnp.zeros((L, SSM_STATE), F32)
        cols = jnp.zeros((L, LANES), F32)
        rows_ddt, rows_q, al_all, dcd_all = [], [], [], []
        for pr in range(2):
            k0 = 2 * pr
            xcols = slice(x0 + pr * LANES, x0 + (pr + 1) * LANES)
            xp = x_ref[:, xcols]
            xpb = xp.astype(BF16)
            dyp = dy_ref[:, xcols]
            dypb = dyp.astype(BF16)
            zero = jnp.zeros_like(dypb)
            hp = hs_ref[0, gi, pr * LANES:(pr + 1) * LANES, :]
            hpb = hp.astype(BF16)
            dst = dh_ref[gi, pr * LANES:(pr + 1) * LANES, :]
            dstb = dst.astype(BF16)
            dxd, al = [], []
            for k in (k0, k0 + 1):
                sel = lo if k == k0 else nlo
                seg = ac[:, k:k + 1] - ar[k:k + 1, :]
                lam = jnp.exp(jnp.where(tril, seg, -jnp.inf))
                wk = gm * lam * dr[k:k + 1, :]
                dwk = _dot(jnp.where(sel, dypb, zero), xpb, 1, 1)
                mk = dwk * gm * lam
                qk = mk * dr[k:k + 1, :]
                dgm = dgm + dwk * lam * dr[k:k + 1, :]
                rows_ddt.append(jnp.sum(mk, axis=0, keepdims=True))
                rows_q.append(jnp.sum(qk, axis=0, keepdims=True))
                cols = jnp.where(lane == k, jnp.sum(qk, axis=-1, keepdims=True), cols)
                dxd.append(_dot(wk.astype(BF16), dypb, 0, 0))
                al.append(_last_lane(ar[k:k + 1, :]))
            al_all += al
            dxp = jnp.where(lo, dxd[0], dxd[1])
            e = jnp.exp(_pair_cols(ac, k0, lo))
            dye = dyp * e
            dyeb = dye.astype(BF16)
            dcm = dcm + _dot(dyeb, hpb, 1, 0)
            dh_yoff = _dot(dyeb, cb, 0, 0)
            tq = dye * _dot(cb, hpb, 1, 1)
            cols = jnp.where(lane == 4 + k0, jnp.sum(jnp.where(lo, tq, 0.0), axis=-1, keepdims=True), cols)
            cols = jnp.where(lane == 5 + k0, jnp.sum(jnp.where(lo, 0.0, tq), axis=-1, keepdims=True), cols)
            wp = jnp.where(lo, jnp.exp(al[0] - ac[:, k0:k0 + 1]) * dtc[:, k0:k0 + 1],
                           jnp.exp(al[1] - ac[:, k0 + 1:k0 + 2]) * dtc[:, k0 + 1:k0 + 2])
            dxw = _dot(bb, dstb, 1, 1)
            dxp = dxp + dxw * wp
            tw = xp * dxw
            cols = jnp.where(lane == 8 + k0, jnp.sum(jnp.where(lo, tw, 0.0), axis=-1, keepdims=True), cols)
            cols = jnp.where(lane == 9 + k0, jnp.sum(jnp.where(lo, 0.0, tw), axis=-1, keepdims=True), cols)
            dbm = dbm + _dot((xp * wp).astype(BF16), dstb, 1, 0)
            dsl = dsk[:, pr * LANES:(pr + 1) * LANES]
            dx_ref[:, xcols] = dxp + dsl * dyp
            dd_ref[gi, :, pr * LANES:(pr + 1) * LANES] += jnp.sum(dyp * xp, axis=0, keepdims=True)
            prod = dst * hp
            dcd_all.append(jnp.sum(jnp.sum(jnp.where(rowlo, prod, 0.0), axis=-1, keepdims=True), axis=0, keepdims=True))
            dcd_all.append(jnp.sum(jnp.sum(jnp.where(rowlo, 0.0, prod), axis=-1, keepdims=True), axis=0, keepdims=True))
            dec = jnp.where(rowlo, jnp.exp(al[0]), jnp.exp(al[1]))
            dh_ref[gi, pr * LANES:(pr + 1) * LANES, :] = dec * dst + dh_yoff
        dgb = dgm.astype(BF16)
        dc_ref[:, bcols] = dcm + _dot(dgb, bb, 1, 0)
        db_ref[:, bcols] = dbm + _dot(dgb, cb, 0, 0)
        colt = cols.T
        sub8 = lax.broadcasted_iota(jnp.int32, (8, LANES), 0)
        da_rows = jnp.zeros((8, LANES), F32)
        ddt_part = []
        for k in range(4):
            rs = colt[k:k + 1, :]
            uo = colt[4 + k:5 + k, :]
            dwl = colt[8 + k:9 + k, :]
            es = jnp.exp(al_all[k] - ar[k:k + 1, :])
            wrow = es * dr[k:k + 1, :]
            dwl_w = dwl * wrow
            da_k = rs - rows_q[k] + uo - dwl_w
            tail = jnp.sum(dwl_w, axis=-1, keepdims=True) + jnp.exp(al_all[k]) * dcd_all[k]
            da_k = da_k + jnp.where(lane_row == L - 1, tail, 0.0)
            da_rows = jnp.where(sub8 == k, da_k, da_rows)
            ddt_part.append(rows_ddt[k] + dwl * es)
        dda = _dot(da_rows, _tri(True), 1, 0, HI)
        for k in range(4):
            dda_k = dda[k:k + 1, :]
            ddt_ref[gi, k:k + 1, :] = ddt_part[k] + dda_k * ag[k:k + 1, :]
            da_ref[gi, k:k + 1, :] += dda_k * dr[k:k + 1, :] * ag[k:k + 1, :]

    return _pc(body, name, (SSM_GROUPS // n, nc),
               [sp["x"], sp["b"], sp["c"], sp["col"], sp["col"], sp["row"], sp["row"], sp["grp"], sp["grow"], sp["hs"], sp["x"]],
               [sp["x"], sp["bc"], sp["bc"], sp["row"], sp["grow"], sp["grp"]],
               [_sds((s, SSM_INNER)), _sds((s, 1024)), _sds((s, 1024)), _sds((SSM_GROUPS, 4, s)),
                _sds((SSM_GROUPS, 4, LANES)), _sds((SSM_GROUPS, 1, 256))],
               [pltpu.VMEM((n, 256, SSM_STATE), F32)])(xbc, xbc, xbc, dtc, acol, drow, arow, dskip, agrp, hs, dy)


def _gnorm_fwd(y, proj, nw, name):
    s = y.shape[0]
    ts = _row_tile(s)
    gw = SSM_INNER // SSM_GROUPS

    def body(y_ref, z_ref, w_ref, o_ref):
        for g in range(SSM_GROUPS):
            sl = slice(g * gw, (g + 1) * gw)
            z = z_ref[:, sl]
            y2 = y_ref[:, sl] * (z * _sigmoid(z))
            r = lax.rsqrt(jnp.mean(y2 * y2, axis=-1, keepdims=True) + RMS_EPS)
            o_ref[:, sl] = ((y2 * r) * w_ref[:, sl]).astype(BF16)

    row = pl.BlockSpec((ts, SSM_INNER), lambda i: (i, 0))
    return _pc(body, name, (s // ts,), [row, row, pl.BlockSpec((1, SSM_INNER), lambda i: (0, 0))], row,
               _sds((s, SSM_INNER), BF16))(y, proj, nw)


def _gnorm_bwd(y, proj, nw, dyn, name):
    s = y.shape[0]
    ts = _row_tile(s)
    gw = SSM_INNER // SSM_GROUPS

    def body(y_ref, z_ref, w_ref, d_ref, dy_ref, dz_ref, dw_ref):
        @pl.when(pl.program_id(0) == 0)
        def _():
            dw_ref[...] = jnp.zeros_like(dw_ref)

        for g in range(SSM_GROUPS):
            sl = slice(g * gw, (g + 1) * gw)
            z = z_ref[:, sl]
            yv = y_ref[:, sl]
            sg = _sigmoid(z)
            sz = z * sg
            y2 = yv * sz
            r = lax.rsqrt(jnp.mean(y2 * y2, axis=-1, keepdims=True) + RMS_EPS)
            yn = y2 * r
            dout = d_ref[:, sl]
            dyg = dout * w_ref[:, sl]
            dy2 = r * (dyg - yn * jnp.mean(dyg * yn, axis=-1, keepdims=True))
            dy_ref[:, sl] = dy2 * sz
            dz_ref[:, sl] = (dy2 * yv * (sg * (1.0 + z * (1.0 - sg)))).astype(BF16)
            dw_ref[:, sl] += jnp.sum(dout * yn, axis=0, keepdims=True)

    row = pl.BlockSpec((ts, SSM_INNER), lambda i: (i, 0))
    vec = pl.BlockSpec((1, SSM_INNER), lambda i: (0, 0))
    return _pc(body, name, (s // ts,), [row, row, vec, row], [row, row, vec],
               [_sds((s, SSM_INNER)), _sds((s, SSM_INNER), BF16), _sds((1, SSM_INNER))])(y, proj, nw, dyn)


def _head_layouts(v, s):
    return v.reshape(s, SSM_GROUPS, 4).transpose(1, 0, 2), v.T.reshape(SSM_GROUPS, 4, s)


def _ssd_fwd(x, h, w_in, conv_w, conv_b, dt_bias, a_log, d_skip, gnorm_w, w_out, next_norm):
    s = x.shape[0]
    proj = _mm(h, w_in, "nn", "ssd_in")
    xbc = _ssd_conv_fwd(proj, conv_w, conv_b, "ssd_conv")
    pad = ((0, 0), (0, LANES - SSM_HEADS))
    a_neg = -jnp.exp(a_log)
    bias = jnp.pad(dt_bias, pad)
    dt, acum = _ssd_dt_fwd(proj, bias, jnp.pad(a_neg, pad), "ssd_dt")
    dtc, drow = _head_layouts(dt[:, :SSM_HEADS], s)
    acol, arow = _head_layouts(acum[:, :SSM_HEADS], s)
    dskip = jnp.repeat(d_skip.reshape(SSM_GROUPS, 1, 4), HEAD_DIM, axis=2)
    y, hs = _ssd_scan_fwd(xbc, dtc, acol, drow, arow, dskip, "ssd_scan")
    yn = _gnorm_fwd(y, proj, gnorm_w, "ssd_gnorm")
    out, h_next = _mm(yn, w_out, "nn", "ssd_out", res=x, norm_w=next_norm)
    return out, h_next, (x, h, proj, xbc, bias, a_neg, dtc, acol, drow, arow, dskip, y, hs, yn)


def _ssd_bwd(dout, saved, norm_w, w_in, conv_w, conv_b, gnorm_w, w_out):
    x, h, proj, xbc, bias, a_neg, dtc, acol, drow, arow, dskip, y, hs, yn = saved
    s = x.shape[0]
    dyn = _mm(dout, w_out, "nt", "ssd_dyn")
    g_out = _mm(yn, dout, "tn", "ssd_gout", out_dtype=BF16)
    dy, dz, g_gnorm = _gnorm_bwd(y, proj, gnorm_w, dyn, "ssd_dgnorm")
    agrp = jnp.broadcast_to(a_neg.reshape(SSM_GROUPS, 4, 1), (SSM_GROUPS, 4, LANES))
    dxs, db, dc, ddt_row, da_acc, dd_acc = _ssd_scan_bwd(xbc, dtc, acol, drow, arow, dskip, agrp, hs, dy, "ssd_dscan")
    dxbc = jnp.concatenate([dxs, db, dc], axis=1)
    dpre, g_cw, g_cb = _ssd_conv_bwd(proj, dxbc, conv_w, conv_b, "ssd_dconv")
    ddt = jnp.pad(ddt_row.reshape(SSM_HEADS, s).T, ((0, 0), (0, LANES - SSM_HEADS)))
    ddtraw, g_dtb = _ssd_dt_bwd(proj, bias, ddt, "ssd_ddt")
    dproj = jnp.concatenate([dz, dpre, ddtraw], axis=1)
    g_in = _mm(h, dproj, "tn", "ssd_gin", out_dtype=BF16)
    dx, g_norm = _mm_dnorm_nt(dproj, w_in, "ssd_dh", x, norm_w, dout)
    g_alog = jnp.sum(da_acc, axis=-1).reshape(1, SSM_HEADS)
    g_d = jnp.sum(dd_acc.reshape(SSM_GROUPS, 4, HEAD_DIM), axis=-1).reshape(1, SSM_HEADS)
    return dx, g_norm, g_in[:, :SSM_IN], g_cw, g_cb, g_dtb[:, :SSM_HEADS], g_alog, g_d, g_gnorm, g_out


def _loss_head(y, target, name):
    s, d = y.shape
    ts = _row_tile(s)

    def body(y_ref, t_ref, dy_ref, l_ref):
        @pl.when(pl.program_id(0) == 0)
        def _():
            l_ref[...] = jnp.zeros_like(l_ref)

        e = y_ref[...] - t_ref[...]
        dy_ref[...] = e * (1.0 / d)
        part = jnp.sum(jnp.sum(e * e, axis=-1, keepdims=True), axis=0, keepdims=True) * (0.5 / d)
        l_ref[...] += jnp.broadcast_to(part, l_ref.shape)

    row = pl.BlockSpec((ts, d), lambda i: (i, 0))
    dy, lacc = _pc(body, name, (s // ts,), [row, row], [row, pl.BlockSpec((8, LANES), lambda i: (0, 0))],
                   [_sds((s, d)), _sds((8, LANES))])(y, target)
    return lacc[0, 0], dy


def _local_step(x, target, w, gather_ffn0=None, gather_fox=None, gather_rest=None, scatter_first=None, scatter_fox=None,
                scatter_last=None):
    saved = []
    received, received_fox, received_last = None, None, None
    h = _rms_fwd(x, w["mix_norm"][0:1], "first_norm")
    for i in range(DEPTH):
        kind, j = i % 3, i // 3
        fn = w["ffn_norm"][i:i + 1]
        if kind == 0:
            hosted = (gather_ffn0[0], True) if (i == 0 and gather_ffn0 is not None) else None
            x, h, sv, got = _conv_fwd(x, h, w["conv_w_in"][j], w["conv_w_dw"][j], w["conv_w_out"][j], str(i), fn, hosted)
            if hosted is not None:
                w = gather_ffn0[1](w, got)
        elif kind == 1:
            hosted = None if gather_rest is None else (gather_rest[0], True)
            x, h, sv, got = _fox_fwd(x, h, w["fox_w_in"], w["fox_b_f"], w["fox_q_gain"], w["fox_k_gain"], w["fox_w_out"], fn, hosted)
            if gather_rest is not None:
                w = gather_rest[1](w, got)
        else:
            x, h, sv = _ssd_fwd(x, h, w["ssd_w_in"], w["ssd_conv_w"], w["ssd_conv_b"], w["ssd_dt_bias"],
                                w["ssd_a_log"], w["ssd_d"], w["ssd_norm_w"], w["ssd_w_out"], fn)
        hosted = (gather_fox[0], True) if (i == 0 and gather_fox is not None) else None
        nxt = w["mix_norm"][i + 1:i + 2] if i + 1 < DEPTH else None
        x, h, sf, got = _ffn_fwd(x, h, w["ffn_w_gu"][i], w["ffn_w_down"][i], str(i), nxt, hosted)
        if hosted is not None:
            w = gather_fox[1](w, got)
        saved.append((sv, sf))
    loss, dx = _loss_head(x, target, "loss_head")
    g = {k: [None] * n for k, n in (("mix_norm", DEPTH), ("ffn_norm", DEPTH), ("ffn_w_gu", DEPTH), ("ffn_w_down", DEPTH),
                                    ("conv_w_in", 2), ("conv_w_dw", 2), ("conv_w_out", 2))}
    for i in reversed(range(DEPTH)):
        kind, j = i % 3, i // 3
        sv, sf = saved[i]
        hosted = (scatter_fox(g), False) if (i == 0 and scatter_fox is not None) else None
        dx, g["ffn_norm"][i], g["ffn_w_gu"][i], g["ffn_w_down"][i], got = _ffn_bwd(
            dx, sf, w["ffn_norm"][i:i + 1], w["ffn_w_gu"][i], w["ffn_w_down"][i], str(i), hosted)
        if hosted is not None:
            received_fox = got
        mn = w["mix_norm"][i:i + 1]
        if kind == 0:
            hosted_fn = scatter_last(g) if (i == 0 and scatter_last is not None) else None
            dx, g["mix_norm"][i], g["conv_w_in"][j], g["conv_w_dw"][j], g["conv_w_out"][j], got = _conv_bwd(
                dx, sv, mn, w["conv_w_in"][j], w["conv_w_dw"][j], w["conv_w_out"][j], str(i), hosted_fn)
            if hosted_fn is not None:
                received_last = got
        elif kind == 1:
            hosted = None if scatter_first is None else (scatter_first(g), False)
            (dx, g["mix_norm"][i], g["fox_w_in"], g["fox_b_f"], g["fox_q_gain"], g["fox_k_gain"],
             g["fox_w_out"], received) = _fox_bwd(dx, sv, mn, w["fox_w_in"], w["fox_w_out"], hosted)
        else:
            (dx, g["mix_norm"][i], g["ssd_w_in"], g["ssd_conv_w"], g["ssd_conv_b"], g["ssd_dt_bias"], g["ssd_a_log"],
             g["ssd_d"], g["ssd_norm_w"], g["ssd_w_out"]) = _ssd_bwd(
                 dx, sv, mn, w["ssd_w_in"], w["ssd_conv_w"], w["ssd_conv_b"], w["ssd_norm_w"], w["ssd_w_out"])
    g["mix_norm"] = jnp.concatenate(g["mix_norm"], axis=0)
    g["ffn_norm"] = jnp.concatenate(g["ffn_norm"], axis=0)
    g["conv_w_dw"] = jnp.stack(g["conv_w_dw"], axis=0)
    g["ssd_conv_w"] = g["ssd_conv_w"][None]
    return loss, dx, g, received, received_fox, received_last


def _mesh_position():
    return lax.axis_index("x") * 4 + lax.axis_index("y") * 2 + lax.axis_index("c")


def _device_of(t):
    return (lax.shift_right_logical(t, 2), lax.bitwise_and(lax.shift_right_logical(t, 1), 1), lax.bitwise_and(t, 1))


def _exchange_copies(srcs_of, out_refs, send_sems, recv_sems, local_sems):
    me = _mesh_position()
    na = len(out_refs)
    locals_ = [pltpu.make_async_copy(srcs_of(a, me), out_refs[a].at[me], local_sems.at[a]) for a in range(na)]
    sends, arrivals = [], []
    for j in range(1, NDEV):
        t = lax.rem(me + j, NDEV)
        frm = lax.rem(me + NDEV - j, NDEV)
        for a in range(na):
            sends.append(pltpu.make_async_remote_copy(
                src_ref=srcs_of(a, t), dst_ref=out_refs[a].at[me], send_sem=send_sems.at[a, j - 1],
                recv_sem=recv_sems.at[a, j - 1], device_id=_device_of(t), device_id_type=pl.DeviceIdType.MESH))
            arrivals.append(pltpu.make_async_remote_copy(
                src_ref=srcs_of(a, me), dst_ref=out_refs[a].at[frm], send_sem=send_sems.at[a, j - 1],
                recv_sem=recv_sems.at[a, j - 1], device_id=_device_of(frm), device_id_type=pl.DeviceIdType.MESH))
    return locals_, sends, arrivals


def _exchange_start(copies):
    locals_, sends, _ = copies
    for cp in locals_ + sends:
        cp.start()


def _exchange_wait(copies):
    locals_, sends, arrivals = copies
    for cp in sends:
        cp.wait_send()
    for cp in arrivals:
        cp.wait_recv()
    for cp in locals_:
        cp.wait()


def _exchange_run(srcs_of, out_refs, send_sems, recv_sems, local_sems):
    copies = _exchange_copies(srcs_of, out_refs, send_sems, recv_sems, local_sems)
    _exchange_start(copies)
    _exchange_wait(copies)


def _exchange_parts(arrays, gather):
    na = len(arrays)
    outs = [_sds(((NDEV,) + a.shape) if gather else a.shape, a.dtype) for a in arrays]
    sems = [pltpu.SemaphoreType.DMA((na, NDEV - 1)), pltpu.SemaphoreType.DMA((na, NDEV - 1)), pltpu.SemaphoreType.DMA((na,))]
    pick = (lambda srcs: (lambda a, t: srcs[a])) if gather else (lambda srcs: (lambda a, t: srcs[a].at[t]))
    return pick, outs, sems


def _exchange(arrays, name, gather):
    na = len(arrays)
    pick, outs, sems = _exchange_parts(arrays, gather)

    def body(*refs):
        _exchange_run(pick(refs[:na]), refs[na:2 * na], *refs[2 * na:])

    hbm = pl.BlockSpec(memory_space=pl.ANY)
    return pl.pallas_call(body, name=name, in_specs=[hbm] * na, out_specs=[hbm] * na, out_shape=outs, scratch_shapes=sems)(*arrays)


def _all_sum_small(pack, name):
    def body(src_ref, out_ref, buf_ref, send_sems, recv_sems, local_sems):
        _exchange_run(lambda a, t: src_ref, [buf_ref], send_sems, recv_sems, local_sems)
        acc = buf_ref[0]
        for d in range(1, NDEV):
            acc = acc + buf_ref[d]
        out_ref[...] = acc

    vmem = pl.BlockSpec(memory_space=pltpu.VMEM)
    return pl.pallas_call(
        body, name=name, in_specs=[vmem], out_specs=vmem, out_shape=_sds(pack.shape, pack.dtype),
        scratch_shapes=[pltpu.VMEM((NDEV,) + pack.shape, pack.dtype), pltpu.SemaphoreType.DMA((1, NDEV - 1)),
                        pltpu.SemaphoreType.DMA((1, NDEV - 1)), pltpu.SemaphoreType.DMA((1,))])(pack)


def _sum_slabs(slabs, name):
    _, r, c = slabs.shape
    tr = r
    for cand in (256, 352):
        if r % cand == 0:
            tr = cand
            break

    def body(s_ref, o_ref):
        acc = s_ref[0].astype(F32)
        for d in range(1, NDEV):
            acc = acc + s_ref[d].astype(F32)
        o_ref[...] = acc

    return _pc(body, name, (r // tr,), [pl.BlockSpec((NDEV, tr, c), lambda i: (0, i, 0))],
               pl.BlockSpec((tr, c), lambda i: (i, 0)), _sds((r, c)))(slabs)


def _adamw(wt, g, m, v, name):
    shape = wt.shape
    w2, g2, m2, v2 = (a.reshape(-1, shape[-1]) for a in (wt, g, m, v))
    r, c = w2.shape
    tr = r
    for cand in (512, 352, 256):
        if r % cand == 0:
            tr = cand
            break
    c1 = 1.0 - ADAM_B1 ** ADAM_STEP
    c2 = 1.0 - ADAM_B2 ** ADAM_STEP

    def body(w_ref, g_ref, m_ref, v_ref, d_ref, mo_ref, vo_ref):
        gv = g_ref[...]
        mn = ADAM_B1 * m_ref[...] + (1.0 - ADAM_B1) * gv
        vn = ADAM_B2 * v_ref[...] + (1.0 - ADAM_B2) * (gv * gv)
        mo_ref[...] = mn
        vo_ref[...] = vn
        d_ref[...] = -ADAM_LR * ((mn / c1) / (jnp.sqrt(vn / c2) + ADAM_EPS) + ADAM_WD * w_ref[...])

    spec = pl.BlockSpec((tr, c), lambda i: (i, 0))
    outs = _pc(body, name, (r // tr,), [spec] * 4, [spec] * 3, [_sds((r, c))] * 3)(w2, g2, m2, v2)
    return tuple(o.reshape(shape) for o in outs)


_NAMES = ["mix_norm", "ffn_norm", "ffn_w_gu", "ffn_w_down", "conv_w_in", "conv_w_dw", "conv_w_out", "fox_w_in", "fox_b_f",
          "fox_q_gain", "fox_k_gain", "fox_w_out", "ssd_w_in", "ssd_conv_w", "ssd_conv_b", "ssd_dt_bias", "ssd_a_log",
          "ssd_d", "ssd_norm_w", "ssd_w_out"]
_MATRICES = ["ffn_w_gu", "ffn_w_down", "conv_w_in", "conv_w_out", "fox_w_in", "fox_w_out", "ssd_w_in", "ssd_w_out"]
_VECTORS = {"conv_w_dw": 2, "ssd_conv_w": 2, "ssd_conv_b": 1, "ssd_norm_w": 1}
_REPLICATED = ["mix_norm", "ffn_norm", "fox_b_f", "fox_q_gain", "fox_k_gain", "ssd_dt_bias", "ssd_a_log", "ssd_d"]


def _to_rows(flat):
    n = flat.shape[0]
    rows = -(-n // (8 * D_MODEL)) * 8
    return jnp.pad(flat, (0, rows * D_MODEL - n)).reshape(rows, D_MODEL)


def _full_shape(local_shape, axis):
    shp = list(local_shape)
    shp[axis] *= NDEV
    return tuple(shp)


def _cols_from_blocks(g):
    return jnp.moveaxis(g, 0, 1).reshape(g.shape[1], NDEV * g.shape[2])


def _blocks_from_cols(full):
    k, n8 = full.shape
    return jnp.moveaxis(full.reshape(k, NDEV, n8 // NDEV), 1, 0)


def kernel(x, mix_norm, ffn_norm, ffn_w_gu, ffn_w_down, conv_w_in, conv_w_dw, conv_w_out, fox_w_in, fox_b_f, fox_q_gain, fox_k_gain, fox_w_out, ssd_w_in, ssd_conv_w, ssd_conv_b, ssd_dt_bias, ssd_a_log, ssd_d, ssd_norm_w, ssd_w_out, loss_target, m_mix_norm, m_ffn_norm, m_ffn_w_gu, m_ffn_w_down, m_conv_w_in, m_conv_w_dw, m_conv_w_out, m_fox_w_in, m_fox_b_f, m_fox_q_gain, m_fox_k_gain, m_fox_w_out, m_ssd_w_in, m_ssd_conv_w, m_ssd_conv_b, m_ssd_dt_bias, m_ssd_a_log, m_ssd_d, m_ssd_norm_w, m_ssd_w_out, v_mix_norm, v_ffn_norm, v_ffn_w_gu, v_ffn_w_down, v_conv_w_in, v_conv_w_dw, v_conv_w_out, v_fox_w_in, v_fox_b_f, v_fox_q_gain, v_fox_k_gain, v_fox_w_out, v_ssd_w_in, v_ssd_conv_w, v_ssd_conv_b, v_ssd_dt_bias, v_ssd_a_log, v_ssd_d, v_ssd_norm_w, v_ssd_w_out):
    local = dict(mix_norm=mix_norm, ffn_norm=ffn_norm, ffn_w_gu=ffn_w_gu, ffn_w_down=ffn_w_down, conv_w_in=conv_w_in,
                 conv_w_dw=conv_w_dw, conv_w_out=conv_w_out, fox_w_in=fox_w_in, fox_b_f=fox_b_f, fox_q_gain=fox_q_gain,
                 fox_k_gain=fox_k_gain, fox_w_out=fox_w_out, ssd_w_in=ssd_w_in, ssd_conv_w=ssd_conv_w, ssd_conv_b=ssd_conv_b,
                 ssd_dt_bias=ssd_dt_bias, ssd_a_log=ssd_a_log, ssd_d=ssd_d, ssd_norm_w=ssd_norm_w, ssd_w_out=ssd_w_out)
    mom = dict(zip(_NAMES, [m_mix_norm, m_ffn_norm, m_ffn_w_gu, m_ffn_w_down, m_conv_w_in, m_conv_w_dw, m_conv_w_out, m_fox_w_in,
                            m_fox_b_f, m_fox_q_gain, m_fox_k_gain, m_fox_w_out, m_ssd_w_in, m_ssd_conv_w, m_ssd_conv_b,
                            m_ssd_dt_bias, m_ssd_a_log, m_ssd_d, m_ssd_norm_w, m_ssd_w_out]))
    var = dict(zip(_NAMES, [v_mix_norm, v_ffn_norm, v_ffn_w_gu, v_ffn_w_down, v_conv_w_in, v_conv_w_dw, v_conv_w_out, v_fox_w_in,
                            v_fox_b_f, v_fox_q_gain, v_fox_k_gain, v_fox_w_out, v_ssd_w_in, v_ssd_conv_w, v_ssd_conv_b,
                            v_ssd_dt_bias, v_ssd_a_log, v_ssd_d, v_ssd_norm_w, v_ssd_w_out]))

    shard = {k: local[k].astype(BF16) for k in _MATRICES}
    vec_pack = _to_rows(jnp.concatenate([local[k].reshape(-1) for k in _VECTORS]))
    first = _exchange([shard["conv_w_in"][0:1], shard["conv_w_out"][0:1], vec_pack], "gather_first", True)
    gvec = first[2].reshape(NDEV, -1)
    full = {k: local[k] for k in _REPLICATED}
    off = 0
    for k, axis in _VECTORS.items():
        n = local[k].size
        blk = jnp.moveaxis(gvec[:, off:off + n].reshape((NDEV,) + local[k].shape), 0, axis)
        full[k] = blk.reshape(_full_shape(local[k].shape, axis))
        off += n
    full["ssd_conv_w"] = full["ssd_conv_w"][0]
    full["conv_w_in"] = [first[0][:, 0]]
    full["conv_w_out"] = [first[1][:, 0].reshape(D_MODEL, D_MODEL)]

    def finish_ffn0(w, got):
        w = dict(w)
        w["ffn_w_gu"] = [got[0][:, 0]]
        w["ffn_w_down"] = [got[1][:, 0].reshape(4, FF_BLOCK, D_MODEL)]
        return w

    def finish_fox(w, got):
        w = dict(w)
        w["fox_w_in"] = jnp.pad(_cols_from_blocks(got[0][:, 0]), ((0, 0), (0, FOX_IN_PAD - FOX_IN)))
        w["fox_w_out"] = got[1].reshape(D_MODEL, D_MODEL)
        return w

    rest = [shard["ffn_w_gu"][1:], shard["ffn_w_down"][1:], shard["conv_w_in"][1:], shard["conv_w_out"][1:],
            shard["ssd_w_in"], shard["ssd_w_out"]]

    def finish(w, got):
        w = dict(w)
        w["ffn_w_gu"] = w["ffn_w_gu"] + [got[0][:, i] for i in range(DEPTH - 1)]
        w["ffn_w_down"] = w["ffn_w_down"] + [got[1][:, i].reshape(4, FF_BLOCK, D_MODEL) for i in range(DEPTH - 1)]
        w["conv_w_in"] = w["conv_w_in"] + [got[2][:, 0]]
        w["conv_w_out"] = w["conv_w_out"] + [got[3][:, 0].reshape(D_MODEL, D_MODEL)]
        w["ssd_w_in"] = jnp.pad(_cols_from_blocks(got[4][:, 0]), ((0, 0), (0, SSM_IN_PAD - SSM_IN)))
        w["ssd_w_out"] = got[5].reshape(SSM_INNER, D_MODEL)
        return w

    def early_slabs(g):
        return ([g["ffn_w_gu"][i] for i in range(1, DEPTH)]
                + [g["ffn_w_down"][i].reshape(NDEV, D_FF // NDEV, D_MODEL) for i in range(1, DEPTH)]
                + [g["conv_w_in"][1], g["conv_w_out"][1].reshape(NDEV, D_MODEL // NDEV, D_MODEL),
                   _blocks_from_cols(g["ssd_w_in"]), g["ssd_w_out"].reshape(NDEV, SSM_INNER // NDEV, D_MODEL)])

    def fox_slabs(g):
        return [_blocks_from_cols(g["fox_w_in"]), g["fox_w_out"].reshape(NDEV, D_MODEL // NDEV, D_MODEL)]

    def last_slabs(g):
        def with_conv(g_in, g_out):
            return ([g["ffn_w_gu"][0], g["ffn_w_down"][0].reshape(NDEV, D_FF // NDEV, D_MODEL), g_in,
                     g_out.reshape(NDEV, D_MODEL // NDEV, D_MODEL)], False)
        return with_conv

    loss_part, dx, grads, early, mid, late = _local_step(
        x[0], loss_target[0], full, ([shard["ffn_w_gu"][0:1], shard["ffn_w_down"][0:1]], finish_ffn0),
        ([shard["fox_w_in"], shard["fox_w_out"]], finish_fox), (rest, finish), early_slabs, fox_slabs, last_slabs)

    se = [_sum_slabs(r, f"sum_early_{n}") for n, r in enumerate(early)]
    sm = [_sum_slabs(r, f"sum_mid_{n}") for n, r in enumerate(mid)]
    sl = [_sum_slabs(r, f"sum_late_{n}") for n, r in enumerate(late)]
    shard_grad = {
        "ffn_w_gu": jnp.stack([sl[0]] + se[0:3]), "ffn_w_down": jnp.stack([sl[1]] + se[3:6]),
        "conv_w_in": jnp.stack([sl[2], se[6]]), "conv_w_out": jnp.stack([sl[3], se[7]]),
        "fox_w_in": sm[0][None], "fox_w_out": sm[1][None], "ssd_w_in": se[8][None], "ssd_w_out": se[9][None]}

    small_names = _REPLICATED + list(_VECTORS)
    small = [jnp.reshape(loss_part, (1,))] + [grads[k].reshape(-1) for k in small_names]
    total = _all_sum_small(_to_rows(jnp.concatenate(small)), "sum_small").reshape(-1)
    loss = total[0]
    off = 1
    me = _mesh_position()
    for k, part in zip(small_names, small[1:]):
        gk = total[off:off + part.shape[0]]
        off += part.shape[0]
        if k in _VECTORS:
            axis = _VECTORS[k]
            shp = local[k].shape
            gfull = gk.reshape(shp[:axis] + (NDEV, shp[axis]) + shp[axis + 1:])
            shard_grad[k] = lax.dynamic_index_in_dim(gfull, me, axis, keepdims=False)
        else:
            shard_grad[k] = gk.reshape(local[k].shape)

    deltas, new_m, new_v = {}, {}, {}
    for k in _NAMES:
        deltas[k], new_m[k], new_v[k] = _adamw(local[k], shard_grad[k], mom[k], var[k], f"adamw_{k}")
    return (loss, dx[None], *[shard_grad[k] for k in _NAMES], *[deltas[k] for k in _NAMES],
            *[new_m[k] for k in _NAMES], *[new_v[k] for k in _NAMES])
```

```python
import numpy as np

import jax
import jax.numpy as jnp
from jax import lax
from jax.experimental import pallas as pl
from jax.experimental.pallas import tpu as pltpu

F32 = jnp.float32
BF16 = jnp.bfloat16
HI = lax.Precision.HIGHEST

NDEV = 8
D_MODEL = 1024
DEPTH = 4
D_FF = 2816
FF_BLOCK = 2 * D_FF // NDEV
RMS_EPS = 1e-6
HEAD_DIM = 64
ATTN_HEADS = 16
FOX_IN = 3 * D_MODEL + ATTN_HEADS
FOX_IN_PAD = 3200
SSM_INNER = 2048
SSM_HEADS = 32
SSM_GROUPS = 8
SSM_STATE = 128
SSM_CHUNK = 128
SSM_CONV_DIM = 4096
SSM_IN = SSM_INNER + SSM_CONV_DIM + SSM_HEADS
SSM_IN_PAD = 6272
LANES = 128
V7X_VMEM_BYTES = 64 * 1024 * 1024
VMEM_LIMIT_BYTES = (V7X_VMEM_BYTES * 3) // 4
ATTN_BWD_VMEM_BYTES = (V7X_VMEM_BYTES * 7) // 8
LOG2E = 1.4426950408889634
LN2 = 0.6931471805599453
ATTN_TILE = 1024
ATTN_ROWS = 32

ADAM_LR = 0.001
ADAM_B1 = 0.9
ADAM_B2 = 0.999
ADAM_EPS = 1e-08
ADAM_WD = 0.01
ADAM_STEP = 10

_TILE_CANDIDATES = (1024, 1408, 896, 768, 640, 512, 384, 256, 128)


def _pick_tile(n):
    for c in _TILE_CANDIDATES:
        if n % c == 0:
            return c
    raise ValueError(f"no tile for {n}")


def _params(ngrid):
    return pltpu.CompilerParams(dimension_semantics=("arbitrary",) * ngrid, vmem_limit_bytes=VMEM_LIMIT_BYTES)


def _pc(body, name, grid, in_specs, out_specs, out_shape, scratch=(), hosted=None):
    if hosted is None:
        return pl.pallas_call(
            body, name=name, grid=grid, in_specs=in_specs, out_specs=out_specs, out_shape=out_shape,
            scratch_shapes=list(scratch), compiler_params=_params(len(grid)))
    arrays, gather = hosted
    single = not isinstance(out_shape, (list, tuple))
    outs = [out_shape] if single else list(out_shape)
    ospecs = [out_specs] if single else list(out_specs)
    na, n_in, n_out, n_scr = len(arrays), len(in_specs), len(outs), len(scratch)
    pick, xouts, sems = _exchange_parts(arrays, gather)

    def run(*refs):
        ins, srcs = refs[:n_in], refs[n_in:n_in + na]
        res, dsts = refs[n_in + na:n_in + na + n_out], refs[n_in + na + n_out:n_in + 2 * na + n_out]
        scr, xsems = refs[n_in + 2 * na + n_out:n_in + 2 * na + n_out + n_scr], refs[n_in + 2 * na + n_out + n_scr:]
        first = pl.program_id(0) == 0
        last = pl.program_id(0) == grid[0] - 1
        for d in range(1, len(grid)):
            first = jnp.logical_and(first, pl.program_id(d) == 0)
            last = jnp.logical_and(last, pl.program_id(d) == grid[d] - 1)

        @pl.when(first)
        def _():
            _exchange_start(_exchange_copies(pick(srcs), dsts, *xsems))

        body(*ins, *res, *scr)

        @pl.when(last)
        def _():
            _exchange_wait(_exchange_copies(pick(srcs), dsts, *xsems))

    hbm = pl.BlockSpec(memory_space=pl.ANY)
    call = pl.pallas_call(
        run, name=name, grid=grid, in_specs=list(in_specs) + [hbm] * na, out_specs=ospecs + [hbm] * na,
        out_shape=outs + xouts, scratch_shapes=list(scratch) + sems, compiler_params=_params(len(grid)))
    return lambda *args: call(*args, *arrays)


def _dot(a, b, ca, cb, prec=None):
    return lax.dot_general(a, b, (((ca,), (cb,)), ((), ())), preferred_element_type=F32, precision=prec)


def _sds(shape, dtype=F32):
    return jax.ShapeDtypeStruct(shape, dtype)


def _row_tile(s, want=256):
    return want if s % want == 0 else s


def _sigmoid(x):
    return 1.0 / (1.0 + jnp.exp(-x))


def _softplus(x):
    return jnp.maximum(x, 0.0) + jnp.log(1.0 + jnp.exp(-jnp.abs(x)))


def _mm_spec(a, b, name, grid, a_spec, b_spec, o_spec, out, ca, cb, acc_shape, drop=(0, 0, 0), res=None, r_spec=None,
             norm_w=None, hosted=None):
    nk = grid[2]
    da, db, do_ = drop
    has_res = res is not None
    has_norm = norm_w is not None

    def body(*refs):
        refs = list(refs)
        a_ref, b_ref = refs[:2]
        r_ref = refs[2] if has_res else None
        w_ref = refs[2 + has_res] if has_norm else None
        o_ref = refs[2 + has_res + has_norm]
        h_ref = refs[3 + has_res + has_norm] if has_norm else None
        acc_ref = refs[-1]
        k = pl.program_id(2)

        @pl.when(k == 0)
        def _():
            acc_ref[...] = jnp.zeros_like(acc_ref)

        av = a_ref[(0,) * da] if da else a_ref[...]
        bv = b_ref[(0,) * db] if db else b_ref[...]
        acc_ref[...] += _dot(av.astype(BF16), bv.astype(BF16), ca, cb)

        @pl.when(k == nk - 1)
        def _():
            val = acc_ref[...]
            if has_res:
                val = val + r_ref[...]
            if do_:
                o_ref[(0,) * do_] = val.astype(out.dtype)
            else:
                o_ref[...] = val.astype(out.dtype)
            if has_norm:
                r = lax.rsqrt(jnp.mean(val * val, axis=-1, keepdims=True) + RMS_EPS)
                h_ref[...] = ((val * r) * w_ref[...]).astype(BF16)

    in_specs = [a_spec, b_spec] + ([r_spec] if has_res else [])
    args = (a, b) + ((res,) if has_res else ())
    out_specs, outs = o_spec, out
    if has_norm:
        assert acc_shape[1] == norm_w.shape[1] == out.shape[-1]
        in_specs.append(pl.BlockSpec((1, acc_shape[1]), lambda i, j, k: (0, 0)))
        args += (norm_w,)
        out_specs, outs = [o_spec, o_spec], [out, _sds(out.shape, BF16)]
    return _pc(body, name, grid, in_specs, out_specs, outs, [pltpu.VMEM(acc_shape, F32)], hosted=hosted)(*args)


def _mm(a, b, mode, name, out_dtype=F32, res=None, norm_w=None, hosted=None):
    if mode == "tn":
        r, m = a.shape
        n = b.shape[1]
        tm, tn, tk = _pick_tile(m), _pick_tile(n), _pick_tile(r)
        grid = (m // tm, n // tn, r // tk)
        a_spec = pl.BlockSpec((tk, tm), lambda i, j, k: (k, i))
        b_spec = pl.BlockSpec((tk, tn), lambda i, j, k: (k, j))
        ca, cb = 0, 0
    else:
        m, kd = a.shape
        n = b.shape[1] if mode == "nn" else b.shape[0]
        tm, tn, tk = _pick_tile(m), _pick_tile(n), _pick_tile(kd)
        grid = (m // tm, n // tn, kd // tk)
        a_spec = pl.BlockSpec((tm, tk), lambda i, j, k: (i, k))
        if mode == "nn":
            b_spec = pl.BlockSpec((tk, tn), lambda i, j, k: (k, j))
            ca, cb = 1, 0
        else:
            b_spec = pl.BlockSpec((tn, tk), lambda i, j, k: (j, k))
            ca, cb = 1, 1
    o_spec = pl.BlockSpec((tm, tn), lambda i, j, k: (i, j))
    return _mm_spec(a, b, name, grid, a_spec, b_spec, o_spec, _sds((m, n), out_dtype), ca, cb, (tm, tn), res=res, r_spec=o_spec,
                    norm_w=norm_w, hosted=hosted)


def _rms_fwd(x, w, name):
    s, d = x.shape
    ts = _row_tile(s)

    def body(x_ref, w_ref, o_ref):
        xv = x_ref[...]
        r = lax.rsqrt(jnp.mean(xv * xv, axis=-1, keepdims=True) + RMS_EPS)
        o_ref[...] = ((xv * r) * w_ref[...]).astype(BF16)

    row = pl.BlockSpec((ts, d), lambda i: (i, 0))
    return _pc(body, name, (s // ts,), [row, pl.BlockSpec((1, d), lambda i: (0, 0))], row, _sds((s, d), BF16))(x, w)


def _mm_dnorm(a, b, name, nk, a_spec, b_spec, ca, cb, drop, x, w, dres, hosted=None):
    s, d = x.shape
    tm = _pick_tile(s)
    da, db = drop

    def body(a_ref, b_ref, x_ref, w_ref, r_ref, dx_ref, dw_ref, acc_ref):
        i = pl.program_id(0)
        k = pl.program_id(2)

        @pl.when(k == 0)
        def _():
            acc_ref[...] = jnp.zeros_like(acc_ref)

        av = a_ref[(0,) * da] if da else a_ref[...]
        bv = b_ref[(0,) * db] if db else b_ref[...]
        acc_ref[...] += _dot(av.astype(BF16), bv.astype(BF16), ca, cb)

        @pl.when(k == nk - 1)
        def _():
            dhv = acc_ref[...]
            xv = x_ref[...]
            r = lax.rsqrt(jnp.mean(xv * xv, axis=-1, keepdims=True) + RMS_EPS)
            xhat = xv * r
            g = dhv * w_ref[...]
            dx_ref[...] = r_ref[...] + r * (g - xhat * jnp.mean(g * xhat, axis=-1, keepdims=True))
            part = jnp.sum(dhv * xhat, axis=0, keepdims=True)

            @pl.when(i == 0)
            def _():
                dw_ref[...] = part

            @pl.when(i > 0)
            def _():
                dw_ref[...] += part

    row = pl.BlockSpec((tm, d), lambda i, j, k: (i, 0))
    vec = pl.BlockSpec((1, d), lambda i, j, k: (0, 0))
    return list(_pc(body, name, (s // tm, 1, nk), [a_spec, b_spec, row, vec, row], [row, vec], [_sds((s, d)), _sds((1, d))],
                    [pltpu.VMEM((tm, d), F32)], hosted=hosted)(a, b, x, w, dres))


def _mm_dnorm_nt(dproj, w_in, name, x, w, dres, hosted=None):
    tm = _pick_tile(x.shape[0])
    tk = _pick_tile(dproj.shape[1])
    return _mm_dnorm(dproj, w_in, name, dproj.shape[1] // tk, pl.BlockSpec((tm, tk), lambda i, j, k: (i, k)),
                     pl.BlockSpec((D_MODEL, tk), lambda i, j, k: (0, k)), 1, 1, (0, 0), x, w, dres, hosted=hosted)


def _ffn_gate_up(h, w_gu, name, hosted=None):
    s = h.shape[0]
    tm = _pick_tile(s)

    def body(h_ref, wg_ref, wu_ref, gu_ref, a_ref):
        hv = h_ref[...]
        g = _dot(hv, wg_ref[0], 1, 0)
        u = _dot(hv, wu_ref[0], 1, 0)
        gu_ref[0, 0] = g.astype(BF16)
        gu_ref[0, 1] = u.astype(BF16)
        a_ref[0] = (g * _sigmoid(g) * u).astype(BF16)

    wblk = lambda off: pl.BlockSpec((1, D_MODEL, FF_BLOCK), lambda i, k: (k + off, 0, 0))
    return _pc(body, name, (s // tm, 4), [pl.BlockSpec((tm, D_MODEL), lambda i, k: (i, 0)), wblk(0), wblk(4)],
               [pl.BlockSpec((1, 2, tm, FF_BLOCK), lambda i, k: (k, 0, i, 0)), pl.BlockSpec((1, tm, FF_BLOCK), lambda i, k: (k, i, 0))],
               [_sds((4, 2, s, FF_BLOCK), BF16), _sds((4, s, FF_BLOCK), BF16)], hosted=hosted)(h, w_gu, w_gu)


def _stage_gather(stages, stage):
    return (stages[stage][0], True) if stages and stage in stages else None


def _stage_arrived(stages, stage, got, default=None):
    late = stages[stage][1](got) if stages and stage in stages else None
    return default if late is None else late


def _stage_slabs(hosted_fn, stage, **new):
    arrays = hosted_fn(stage, **new) if hosted_fn is not None else None
    return None if arrays is None else (arrays, False)


def _ffn_dgate_up(dy, w_down, gu, name, hosted=None):
    s = dy.shape[0]
    tm = _pick_tile(s)

    def body(dy_ref, w_ref, gu_ref, o_ref):
        dav = _dot(dy_ref[...].astype(BF16), w_ref[0], 1, 1)
        g = gu_ref[0, 0].astype(F32)
        u = gu_ref[0, 1].astype(F32)
        sg = _sigmoid(g)
        o_ref[0, 0] = (dav * u * (sg * (1.0 + g * (1.0 - sg)))).astype(BF16)
        o_ref[0, 1] = (dav * (g * sg)).astype(BF16)

    pair = pl.BlockSpec((1, 2, tm, FF_BLOCK), lambda i, k: (k, 0, i, 0))
    return _pc(body, name, (s // tm, 4),
               [pl.BlockSpec((tm, D_MODEL), lambda i, k: (i, 0)), pl.BlockSpec((1, FF_BLOCK, D_MODEL), lambda i, k: (k, 0, 0)), pair],
               pair, _sds((4, 2, s, FF_BLOCK), BF16), hosted=hosted)(dy, w_down, gu)


def _ffn_fwd(x, h, w_gu, w_down, tag, next_norm=None, stages=None):
    s = x.shape[0]
    tm = _pick_tile(s)
    gu, a, *got = _ffn_gate_up(h, w_gu, f"ffn_gu_{tag}", _stage_gather(stages, "ffn_gu"))
    w_down = _stage_arrived(stages, "ffn_gu", got, w_down)
    xspec = pl.BlockSpec((tm, D_MODEL), lambda i, j, k: (i, 0))
    hosted = _stage_gather(stages, "ffn_down")
    y = _mm_spec(a, w_down, f"ffn_down_{tag}", (s // tm, 1, 4),
                 pl.BlockSpec((1, tm, FF_BLOCK), lambda i, j, k: (k, i, 0)),
                 pl.BlockSpec((1, FF_BLOCK, D_MODEL), lambda i, j, k: (k, 0, 0)),
                 xspec, _sds((s, D_MODEL)), 1, 0, (tm, D_MODEL), drop=(1, 1, 0), res=x, r_spec=xspec, norm_w=next_norm,
                 hosted=hosted)
    n_own = 2 if next_norm is not None else 1
    own = list(y[:n_own]) if (hosted is not None or next_norm is not None) else [y]
    if hosted is not None:
        _stage_arrived(stages, "ffn_down", list(y[n_own:]))
    y, h_next = own if next_norm is not None else (own[0], None)
    return y, h_next, (x, h, gu, a)


def _ffn_bwd(dy, saved, norm_w, w_gu, w_down, tag, hosted_fn=None):
    x, h, gu, a = saved
    s = x.shape[0]
    tm = _pick_tile(s)
    got = {}
    hosted = _stage_slabs(hosted_fn, "ffn_gdown")
    g_down = _mm_spec(a, dy, f"ffn_gdown_{tag}", (4, 1, s // tm),
                      pl.BlockSpec((1, tm, FF_BLOCK), lambda i, j, k: (i, k, 0)),
                      pl.BlockSpec((tm, D_MODEL), lambda i, j, k: (k, 0)),
                      pl.BlockSpec((1, FF_BLOCK, D_MODEL), lambda i, j, k: (i, 0, 0)),
                      _sds((4, FF_BLOCK, D_MODEL), BF16), 0, 0, (FF_BLOCK, D_MODEL), drop=(1, 0, 1), hosted=hosted)
    if hosted is not None:
        g_down, *got["ffn_gdown"] = g_down
    hosted = _stage_slabs(hosted_fn, "ffn_dgu", g_down=g_down)
    dgu = _ffn_dgate_up(dy, w_down, gu, f"ffn_dgu_{tag}", hosted)
    if hosted is not None:
        dgu, *got["ffn_dgu"] = dgu
    g_gu = _mm_spec(h, dgu, f"ffn_ggu_{tag}", (NDEV, 1, s // tm),
                    pl.BlockSpec((tm, D_MODEL), lambda i, j, k: (k, 0)),
                    pl.BlockSpec((1, 1, tm, FF_BLOCK), lambda i, j, k: (i % 4, i // 4, k, 0)),
                    pl.BlockSpec((1, D_MODEL, FF_BLOCK), lambda i, j, k: (i, 0, 0)),
                    _sds((NDEV, D_MODEL, FF_BLOCK), BF16), 0, 0, (D_MODEL, FF_BLOCK), drop=(0, 2, 1))
    hosted = _stage_slabs(hosted_fn, "ffn_dh", g_gu=g_gu)
    dx, g_norm, *arrived = _mm_dnorm(dgu, w_gu, f"ffn_dh_{tag}", NDEV,
                                     pl.BlockSpec((1, 1, tm, FF_BLOCK), lambda i, j, k: (k % 4, k // 4, i, 0)),
                                     pl.BlockSpec((1, D_MODEL, FF_BLOCK), lambda i, j, k: (k, 0, 0)), 1, 1, (2, 1), x, norm_w, dy,
                                     hosted=hosted)
    if hosted is not None:
        got["ffn_dh"] = arrived
    return dx, g_norm, g_gu, g_down, got


def _prev_rows(cur, halo, j, first):
    rid = lax.broadcasted_iota(jnp.int32, cur.shape, 0)
    hid = lax.broadcasted_iota(jnp.int32, halo.shape, 0)
    out = pltpu.roll(cur, j, 0)
    for t in range(j):
        row = jnp.sum(jnp.where(hid == 8 - j + t, halo, 0.0), axis=0, keepdims=True)
        row = jnp.where(first, 0.0, row)
        out = jnp.where(rid == t, row, out)
    return out


def _next_rows(cur, halo, j, last):
    ts = cur.shape[0]
    rid = lax.broadcasted_iota(jnp.int32, cur.shape, 0)
    hid = lax.broadcasted_iota(jnp.int32, halo.shape, 0)
    out = pltpu.roll(cur, ts - j, 0)
    for t in range(j):
        row = jnp.sum(jnp.where(hid == t, halo, 0.0), axis=0, keepdims=True)
        row = jnp.where(last, 0.0, row)
        out = jnp.where(rid == ts - j + t, row, out)
    return out


def _halo_specs(ts, s, width, col):
    per = ts // 8
    nblk = s // 8
    prev = pl.BlockSpec((8, width), lambda i: (jnp.maximum(i * per - 1, 0), col))
    nxt = pl.BlockSpec((8, width), lambda i: (jnp.minimum((i + 1) * per, nblk - 1), col))
    return prev, nxt


def _cgate_fwd(p, w_dw, name, hosted=None):
    s = p.shape[0]
    d = D_MODEL
    ts = _row_tile(s)
    prev, _ = _halo_specs(ts, s, 3 * d, 0)

    def body(p_ref, h_ref, w_ref, z_ref):
        first = pl.program_id(0) == 0
        b = p_ref[:, :d]
        cv = p_ref[:, d:2 * d] * p_ref[:, 2 * d:]
        hcv = h_ref[:, d:2 * d] * h_ref[:, 2 * d:]
        u = w_ref[2:3, :] * cv + w_ref[1:2, :] * _prev_rows(cv, hcv, 1, first) + w_ref[0:1, :] * _prev_rows(cv, hcv, 2, first)
        z_ref[...] = (b * u).astype(BF16)

    return _pc(body, name, (s // ts,),
               [pl.BlockSpec((ts, 3 * d), lambda i: (i, 0)), prev, pl.BlockSpec((3, d), lambda i: (0, 0))],
               pl.BlockSpec((ts, d), lambda i: (i, 0)), _sds((s, d), BF16), hosted=hosted)(p, p, w_dw)


def _cgate_bwd(p, dz, w_dw, name):
    s = p.shape[0]
    d = D_MODEL
    ts = _row_tile(s)
    nt = s // ts
    p_prev, p_next = _halo_specs(ts, s, 3 * d, 0)
    _, dz_next = _halo_specs(ts, s, d, 0)

    def body(p_ref, hp_ref, hn_ref, dz_ref, dzn_ref, w_ref, dp_ref, dw_ref):
        i = pl.program_id(0)
        first = i == 0
        last = i == nt - 1
        b = p_ref[:, :d]
        c = p_ref[:, d:2 * d]
        v = p_ref[:, 2 * d:]
        cv = c * v
        hcv = hp_ref[:, d:2 * d] * hp_ref[:, 2 * d:]
        cv1 = _prev_rows(cv, hcv, 1, first)
        cv2 = _prev_rows(cv, hcv, 2, first)
        w0, w1, w2 = w_ref[0:1, :], w_ref[1:2, :], w_ref[2:3, :]
        u = w2 * cv + w1 * cv1 + w0 * cv2
        dzv = dz_ref[...]
        du = dzv * b
        dun = dzn_ref[...] * hn_ref[:, :d]
        dcv = w2 * du + w1 * _next_rows(du, dun, 1, last) + w0 * _next_rows(du, dun, 2, last)
        dp_ref[:, :d] = (dzv * u).astype(BF16)
        dp_ref[:, d:2 * d] = (dcv * v).astype(BF16)
        dp_ref[:, 2 * d:] = (dcv * c).astype(BF16)

        @pl.when(first)
        def _():
            dw_ref[...] = jnp.zeros_like(dw_ref)

        dw_ref[0:1, :] += jnp.sum(du * cv2, axis=0, keepdims=True)
        dw_ref[1:2, :] += jnp.sum(du * cv1, axis=0, keepdims=True)
        dw_ref[2:3, :] += jnp.sum(du * cv, axis=0, keepdims=True)

    wide = pl.BlockSpec((ts, 3 * d), lambda i: (i, 0))
    wspec = pl.BlockSpec((3, d), lambda i: (0, 0))
    return _pc(body, name, (nt,),
               [wide, p_prev, p_next, pl.BlockSpec((ts, d), lambda i: (i, 0)), dz_next, wspec],
               [wide, wspec], [_sds((s, 3 * d), BF16), _sds((3, d))])(p, p, p, dz, dz, w_dw)


def _conv_fwd(x, h, w_in, w_dw, w_out, tag, next_norm, stages=None):
    wn = _cols_from_blocks(w_in)
    hosted = _stage_gather(stages, "conv_in")
    p = _mm(h, wn, "nn", f"conv_in_{tag}", hosted=hosted)
    if hosted is not None:
        p, *got = p
        _stage_arrived(stages, "conv_in", got)
    hosted = _stage_gather(stages, "conv_gate")
    z = _cgate_fwd(p, w_dw, f"conv_gate_{tag}", hosted)
    if hosted is not None:
        z, *got = z
        w_out = _stage_arrived(stages, "conv_gate", got, w_out)
    y, h_next = _mm(z, w_out, "nn", f"conv_out_{tag}", res=x, norm_w=next_norm)
    return y, h_next, (x, h, p, z, wn)


def _conv_bwd(dy, saved, norm_w, w_in, w_dw, w_out, tag, hosted_fn=None):
    x, h, p, z, wn = saved
    dz = _mm(dy, w_out, "nt", f"conv_dz_{tag}")
    g_out = _mm(z, dy, "tn", f"conv_gout_{tag}", out_dtype=BF16)
    dp, g_dw = _cgate_bwd(p, dz, w_dw, f"conv_dgate_{tag}")
    g_in = _blocks_from_cols(_mm(h, dp, "tn", f"conv_gin_{tag}", out_dtype=BF16))
    hosted = _stage_slabs(hosted_fn, "conv_dh", g_in=g_in, g_out=g_out)
    dx, g_norm, *got = _mm_dnorm_nt(dp, wn, f"conv_dh_{tag}", x, norm_w, dy, hosted=hosted)
    return dx, g_norm, g_in, g_dw, g_out, ({"conv_dh": got} if hosted is not None else {})


def _tri(lower):
    r = lax.broadcasted_iota(jnp.int32, (LANES, LANES), 0)
    c = lax.broadcasted_iota(jnp.int32, (LANES, LANES), 1)
    return jnp.where((r >= c) if lower else (r <= c), 1.0, 0.0).astype(F32)


def _cumsum_rows(v, reverse, name):
    s = v.shape[0]
    n = s // LANES
    idx = (lambda i: (n - 1 - i, 0)) if reverse else (lambda i: (i, 0))

    def body(v_ref, o_ref, carry_ref):
        @pl.when(pl.program_id(0) == 0)
        def _():
            carry_ref[...] = jnp.zeros_like(carry_ref)

        blk = v_ref[...]
        o_ref[...] = _dot(_tri(not reverse), blk, 1, 0, HI) + carry_ref[0:1, :]
        carry_ref[...] += jnp.sum(blk, axis=0, keepdims=True)

    spec = pl.BlockSpec((LANES, LANES), idx)
    return _pc(body, name, (n,), [spec], spec, _sds((s, LANES)), [pltpu.VMEM((8, LANES), F32)])(v)


def _lo_mask(shape):
    return lax.broadcasted_iota(jnp.int32, shape, len(shape) - 1) < HEAD_DIM


def _half_sums(v, lo):
    sa = jnp.sum(jnp.where(lo, v, 0.0), axis=-1, keepdims=True)
    sb = jnp.sum(jnp.where(lo, 0.0, v), axis=-1, keepdims=True)
    return jnp.where(lo, sa, sb)


def _fox_prep_fwd(proj, gq, gk, name):
    s = proj.shape[0]
    ts = _row_tile(s)
    qscale = HEAD_DIM ** -0.5 * LOG2E

    def body(q_ref, k_ref, v_ref, gq_ref, gk_ref, qo_ref, ko_ref, vo_ref):
        lo = _lo_mask((ts, LANES))

        def hnorm(xv, g):
            ms = _half_sums(xv * xv, lo) * (1.0 / HEAD_DIM)
            return (xv * lax.rsqrt(ms + RMS_EPS)) * g

        for p in range(8):
            cols = slice(p * LANES, (p + 1) * LANES)
            qo_ref[:, cols] = (hnorm(q_ref[:, cols], gq_ref[...]) * qscale).astype(BF16)
            ko_ref[:, cols] = hnorm(k_ref[:, cols], gk_ref[...]).astype(BF16)
        vo_ref[...] = v_ref[...].astype(BF16)

    def wide(blk):
        return pl.BlockSpec((ts, D_MODEL), lambda i: (i, blk))

    gspec = pl.BlockSpec((1, LANES), lambda i: (0, 0))
    out = _sds((s, D_MODEL), BF16)
    return _pc(body, name, (s // ts,), [wide(0), wide(1), wide(2), gspec, gspec], [wide(0)] * 3, [out] * 3)(
        proj, proj, proj, gq, gk)


def _fox_logf(proj, bf, name):
    s = proj.shape[0]
    ts = _row_tile(s, 512)

    def body(f_ref, b_ref, o_ref):
        z = f_ref[...] + b_ref[...]
        lf = jnp.minimum(z, 0.0) - jnp.log(1.0 + jnp.exp(-jnp.abs(z)))
        real = lax.broadcasted_iota(jnp.int32, (ts, LANES), 1) < ATTN_HEADS
        o_ref[...] = jnp.where(real, lf, 0.0)

    return _pc(body, name, (s // ts,), [pl.BlockSpec((ts, LANES), lambda i: (i, 24)), pl.BlockSpec((1, LANES), lambda i: (0, 0))],
               pl.BlockSpec((ts, LANES), lambda i: (i, 0)), _sds((s, LANES)))(proj, bf)


def _fox_dlogf(proj, bf, dlf, name):
    s = proj.shape[0]
    ts = _row_tile(s, 512)

    def body(f_ref, b_ref, d_ref, o_ref, db_ref):
        z = f_ref[...] + b_ref[...]
        real = lax.broadcasted_iota(jnp.int32, (ts, LANES), 1) < ATTN_HEADS
        g = jnp.where(real, d_ref[...] * _sigmoid(-z), 0.0)
        o_ref[...] = g.astype(BF16)

        @pl.when(pl.program_id(0) == 0)
        def _():
            db_ref[...] = jnp.zeros_like(db_ref)

        db_ref[...] += jnp.sum(g, axis=0, keepdims=True)

    vec = pl.BlockSpec((1, LANES), lambda i: (0, 0))
    row = pl.BlockSpec((ts, LANES), lambda i: (i, 0))
    return _pc(body, name, (s // ts,), [pl.BlockSpec((ts, LANES), lambda i: (i, 24)), vec, row], [row, vec],
               [_sds((s, LANES), BF16), _sds((1, LANES))])(proj, bf, dlf)


def _decay_terms(cum):
    s = cum.shape[0]
    c2 = cum * LOG2E
    hi = lax.reduce_precision(c2, 8, 7)
    mid = lax.reduce_precision(c2 - hi, 8, 7)
    low = lax.reduce_precision(c2 - hi - mid, 8, 7)
    one = jnp.ones_like(hi)

    def place(terms):
        tt = jnp.stack(terms, axis=-1).astype(BF16).reshape(s, 8, 2, 6)
        z = jnp.zeros((s, 8, HEAD_DIM - 6), BF16)
        return jnp.concatenate([tt[:, :, 1], z, tt[:, :, 0], z], axis=-1).reshape(s, D_MODEL)

    return place([hi, mid, low, one, one, one]), place([one, one, one, -hi, -mid, -low])


def _attn_tiles(s):
    t = s
    for cand in (ATTN_TILE, ATTN_TILE // 2):
        if s % cand == 0:
            t = cand
            break
    return t, s // t


def _tri_steps(n, by_key):
    if by_key:
        pairs = [(q, k) for k in range(n) for q in range(k, n)]
    else:
        pairs = [(q, k) for q in range(n) for k in range(q + 1)]
    arr = np.asarray(pairs, np.int32)
    return jnp.asarray(arr[:, 0]), jnp.asarray(arr[:, 1])


def _attn_call(body, name, s, by_key, inputs, in_kinds, out_kinds, out_shapes, scratch, hosted=None, vmem=VMEM_LIMIT_BYTES):
    t, n = _attn_tiles(s)
    qi_arr, ki_arr = _tri_steps(n, by_key)
    nsteps = int(qi_arr.shape[0])
    specs = {
        "q": pl.BlockSpec((t, LANES), lambda p, i, qi, ki: (qi[i], p)),
        "k": pl.BlockSpec((t, LANES), lambda p, i, qi, ki: (ki[i], p)),
        "r": pl.BlockSpec((1, 2, t), lambda p, i, qi, ki: (p, 0, qi[i])),
        "m": pl.BlockSpec((1, t, t), lambda p, i, qi, ki: (jnp.where(qi[i] == ki[i], 1, 0), 0, 0)),
        "Q": pl.BlockSpec((1, LANES, s), lambda p, i, qi, ki: (p, 0, 0)),
        "R": pl.BlockSpec((1, 2, s), lambda p, i, qi, ki: (p, 0, 0)),
    }
    in_specs = [specs[c] for c in in_kinds]
    out_specs = [specs[c] for c in out_kinds]
    out_shapes, scratch, inputs = list(out_shapes), list(scratch), list(inputs)
    run = body
    if hosted is not None:
        arrays, gather = hosted
        na, n_in, n_out, n_scr = len(arrays), len(inputs), len(out_kinds), len(scratch)
        pick, xouts, sems = _exchange_parts(arrays, gather)

        def run(qi_ref, ki_ref, *refs):
            ins, srcs = refs[:n_in], refs[n_in:n_in + na]
            outs, dsts = refs[n_in + na:n_in + na + n_out], refs[n_in + na + n_out:n_in + 2 * na + n_out]
            scr, xsems = refs[n_in + 2 * na + n_out:n_in + 2 * na + n_out + n_scr], refs[n_in + 2 * na + n_out + n_scr:]
            p = pl.program_id(0)
            i = pl.program_id(1)

            @pl.when(jnp.logical_and(p == 0, i == 0))
            def _():
                _exchange_start(_exchange_copies(pick(srcs), dsts, *xsems))

            body(qi_ref, ki_ref, *ins, *outs, *scr)

            @pl.when(jnp.logical_and(p == 7, i == nsteps - 1))
            def _():
                _exchange_wait(_exchange_copies(pick(srcs), dsts, *xsems))

        hbm = pl.BlockSpec(memory_space=pl.ANY)
        in_specs += [hbm] * na
        out_specs += [hbm] * na
        out_shapes += xouts
        scratch += sems
        inputs += list(arrays)
    grid_spec = pltpu.PrefetchScalarGridSpec(
        num_scalar_prefetch=2, grid=(8, nsteps), in_specs=in_specs, out_specs=out_specs, scratch_shapes=scratch)
    params = pltpu.CompilerParams(dimension_semantics=("arbitrary", "arbitrary"), vmem_limit_bytes=vmem)
    return pl.pallas_call(run, name=name, grid_spec=grid_spec, out_shape=out_shapes, compiler_params=params)(
        qi_arr, ki_arr, *inputs)


def _biased_kq(q2, k2, aq, ak, lo):
    sa = _dot(jnp.where(lo, k2, ak), jnp.where(lo, q2, aq), 1, 1)
    sb = _dot(jnp.where(lo, ak, k2), jnp.where(lo, aq, q2), 1, 1)
    return sa, sb


def _causal_bias(s):
    t, _ = _attn_tiles(s)
    kid = lax.broadcasted_iota(jnp.int32, (t, t), 0)
    qid = lax.broadcasted_iota(jnp.int32, (t, t), 1)
    return jnp.stack([jnp.zeros((t, t), BF16), jnp.where(kid > qid, -jnp.inf, 0.0).astype(BF16)])


def _fold8(v, op):
    return op(v.reshape(v.shape[0] // 8, 8, v.shape[1]), axis=0)


def _chunk(ref, mask_ref, hd, r):
    rows = slice(r * ATTN_ROWS, (r + 1) * ATTN_ROWS)
    return rows, ref[hd, rows, :] + mask_ref[0, rows, :].astype(F32)


def _flash_fwd(qs, kn, vb, augq, augk, cmask, name, hosted=None):
    s = qs.shape[0]
    t, n = _attn_tiles(s)
    nch = t // ATTN_ROWS

    def body(qi_ref, ki_ref, q_ref, k_ref, v_ref, aq_ref, ak_ref, mk_ref, o_ref, lse_ref, s_ref, p_ref, m_ref, l_ref, acc_ref):
        i = pl.program_id(1)
        qi = qi_ref[i]
        ki = ki_ref[i]

        @pl.when(ki == 0)
        def _():
            m_ref[...] = jnp.full_like(m_ref, -jnp.inf)
            l_ref[...] = jnp.zeros_like(l_ref)
            acc_ref[...] = jnp.zeros_like(acc_ref)

        lo = _lo_mask((t, LANES))
        rowlo = lax.broadcasted_iota(jnp.int32, (LANES, t), 0) < HEAD_DIM
        v2 = v_ref[...]
        sa, sb = _biased_kq(q_ref[...], k_ref[...], aq_ref[...], ak_ref[...], lo)
        s_ref[0] = sa
        s_ref[1] = sb
        alphas, pvs = [], []
        for hd in range(2):
            mx = jnp.full((8, t), -jnp.inf, F32)
            for r in range(nch):
                _, sc = _chunk(s_ref, mk_ref, hd, r)
                mx = jnp.maximum(mx, _fold8(sc, jnp.max))
            m_prev = m_ref[hd:hd + 1, :]
            m_new = jnp.maximum(m_prev, jnp.max(mx, axis=0, keepdims=True))
            ls = jnp.zeros((8, t), F32)
            for r in range(nch):
                rows, sc = _chunk(s_ref, mk_ref, hd, r)
                pm = jnp.exp2(sc - m_new)
                ls = ls + _fold8(pm, jnp.sum)
                p_ref[hd, rows, :] = pm.astype(BF16)
            alpha = jnp.exp2(m_prev - m_new)
            l_ref[hd:hd + 1, :] = alpha * l_ref[hd:hd + 1, :] + jnp.sum(ls, axis=0, keepdims=True)
            m_ref[hd:hd + 1, :] = m_new
            alphas.append(alpha)
            pvs.append(_dot(v2, p_ref[hd], 0, 0))
        acc_ref[...] = jnp.where(rowlo, alphas[0], alphas[1]) * acc_ref[...] + jnp.where(rowlo, pvs[0], pvs[1])

        @pl.when(ki == qi)
        def _():
            o_ref[...] = (acc_ref[...] / jnp.where(rowlo, l_ref[0:1, :], l_ref[1:2, :])).T
            lse_ref[0] = m_ref[0:2, :] + jnp.log2(l_ref[0:2, :])

    stat = pltpu.VMEM((8, t), F32)
    return _attn_call(body, name, s, False, (qs, kn, vb, augq, augk, cmask), "qkkqkm", "qr",
                      [_sds((s, D_MODEL)), _sds((8, 2, s))],
                      [pltpu.VMEM((2, t, t), F32), pltpu.VMEM((2, t, t), BF16), stat, stat, pltpu.VMEM((LANES, t), F32)],
                      hosted=hosted)


def _fox_delta(do, o, name):
    s = do.shape[0]
    ts = _row_tile(s)

    def body(do_ref, o_ref, d_ref):
        lo = _lo_mask((ts, LANES))
        for p in range(8):
            cols = slice(p * LANES, (p + 1) * LANES)
            d_ref[:, cols] = _half_sums(do_ref[:, cols] * o_ref[:, cols], lo)

    spec = pl.BlockSpec((ts, D_MODEL), lambda i: (i, 0))
    return _pc(body, name, (s // ts,), [spec, spec], spec, _sds((s, D_MODEL)))(do, o)


def _bwd_tile(q_ref, k_ref, v_ref, aq_ref, ak_ref, do_ref, s_ref, dp_ref, lo):
    do2 = do_ref[...].astype(BF16)
    zero = jnp.zeros_like(do2)
    v2 = v_ref[...]
    sa, sb = _biased_kq(q_ref[...], k_ref[...], aq_ref[...], ak_ref[...], lo)
    s_ref[0] = sa
    s_ref[1] = sb
    dp_ref[0] = _dot(v2, jnp.where(lo, do2, zero), 1, 1)
    dp_ref[1] = _dot(v2, jnp.where(lo, zero, do2), 1, 1)
    return do2


def _bwd_chunk(s_ref, dp_ref, mk_ref, lse_ref, dl_ref, hd, r):
    rows, sc = _chunk(s_ref, mk_ref, hd, r)
    pm = jnp.exp2(sc - lse_ref[0, hd:hd + 1, :])
    ds = pm * (dp_ref[hd, rows, :] - dl_ref[0, hd:hd + 1, :])
    return rows, pm, ds


def _flash_bwd(qs, kn, vb, augq, augk, cmask, do, lse, delta, name, hosted=None):
    s = qs.shape[0]
    t, n = _attn_tiles(s)
    nch = t // ATTN_ROWS

    def body(qi_ref, ki_ref, q_ref, k_ref, v_ref, aq_ref, ak_ref, mk_ref, do_ref, lse_ref, dl_ref,
             dk_ref, dv_ref, dc_ref, dq_ref, dcq_ref, s_ref, dp_ref, p_ref, ds_ref, dka_ref, dva_ref, dca_ref):
        i = pl.program_id(1)
        qi = qi_ref[i]
        ki = ki_ref[i]

        @pl.when(i == 0)
        def _():
            dq_ref[...] = jnp.zeros_like(dq_ref)
            dcq_ref[...] = jnp.zeros_like(dcq_ref)

        @pl.when(qi == ki)
        def _():
            dka_ref[...] = jnp.zeros_like(dka_ref)
            dva_ref[...] = jnp.zeros_like(dva_ref)
            dca_ref[...] = jnp.zeros_like(dca_ref)

        lo = _lo_mask((t, LANES))
        rowlo = lax.broadcasted_iota(jnp.int32, (LANES, t), 0) < HEAD_DIM
        do2 = _bwd_tile(q_ref, k_ref, v_ref, aq_ref, ak_ref, do_ref, s_ref, dp_ref, lo)
        q2 = q_ref[...]
        k2 = k_ref[...]
        qcols = pl.ds(pl.multiple_of(qi * t, t), t)
        dvs, dks, dqs = [], [], []
        for hd in range(2):
            rs = jnp.zeros((8, t), F32)
            for r in range(nch):
                rows, pm, ds = _bwd_chunk(s_ref, dp_ref, mk_ref, lse_ref, dl_ref, hd, r)
                rs = rs + _fold8(ds, jnp.sum)
                part = ds[:, 0:LANES]
                for c in range(1, t // LANES):
                    part = part + ds[:, c * LANES:(c + 1) * LANES]
                dca_ref[hd, rows, :] += part
                p_ref[hd, rows, :] = pm.astype(BF16)
                ds_ref[hd, rows, :] = ds.astype(BF16)
            dcq_ref[0, hd:hd + 1, qcols] += jnp.sum(rs, axis=0, keepdims=True)
            dvs.append(_dot(p_ref[hd], do2, 1, 0))
            dks.append(_dot(ds_ref[hd], q2, 1, 0))
            dqs.append(_dot(k2, ds_ref[hd], 0, 0))
        dva_ref[...] += jnp.where(lo, dvs[0], dvs[1])
        dka_ref[...] += jnp.where(lo, dks[0], dks[1])
        dq_ref[0, :, qcols] += jnp.where(rowlo, dqs[0], dqs[1])

        @pl.when(qi == n - 1)
        def _():
            dk_ref[...] = dka_ref[...] * LN2
            dv_ref[...] = dva_ref[...]
            dc_ref[...] = -jnp.where(lo, jnp.sum(dca_ref[0], axis=-1, keepdims=True), jnp.sum(dca_ref[1], axis=-1, keepdims=True))

    out = _sds((s, D_MODEL))
    return _attn_call(body, name, s, True, (qs, kn, vb, augq, augk, cmask, do, lse, delta), "qkkqkmqrr", "kkkQR",
                      [out, out, out, _sds((8, LANES, s)), _sds((8, 2, s))],
                      [pltpu.VMEM((2, t, t), F32), pltpu.VMEM((2, t, t), F32), pltpu.VMEM((2, t, t), BF16),
                       pltpu.VMEM((2, t, t), BF16), pltpu.VMEM((t, LANES), F32), pltpu.VMEM((t, LANES), F32),
                       pltpu.VMEM((2, t, LANES), F32)], hosted=hosted, vmem=ATTN_BWD_VMEM_BYTES)


def _fox_prep_bwd(proj, dqs, dk, dv, gq, gk, name):
    s = proj.shape[0]
    ts = _row_tile(s)
    scale = HEAD_DIM ** -0.5

    def body(q_ref, k_ref, dq_ref, dk_ref, dv_ref, gq_ref, gk_ref, oq_ref, ok_ref, ov_ref, dgq_ref, dgk_ref):
        lo = _lo_mask((ts, LANES))

        @pl.when(pl.program_id(0) == 0)
        def _():
            dgq_ref[...] = jnp.zeros_like(dgq_ref)
            dgk_ref[...] = jnp.zeros_like(dgk_ref)

        def back(xv, dout, g):
            r = lax.rsqrt(_half_sums(xv * xv, lo) * (1.0 / HEAD_DIM) + RMS_EPS)
            y = xv * r
            dy = dout * g
            dx = r * (dy - y * (_half_sums(dy * y, lo) * (1.0 / HEAD_DIM)))
            return dx, jnp.sum(dout * y, axis=0, keepdims=True)

        for p in range(8):
            cols = slice(p * LANES, (p + 1) * LANES)
            dxq, dgq = back(q_ref[:, cols], dq_ref[p].T * scale, gq_ref[...])
            dxk, dgk = back(k_ref[:, cols], dk_ref[:, cols], gk_ref[...])
            oq_ref[:, cols] = dxq.astype(BF16)
            ok_ref[:, cols] = dxk.astype(BF16)
            dgq_ref[...] += dgq
            dgk_ref[...] += dgk
        ov_ref[...] = dv_ref[...].astype(BF16)

    def wide(blk):
        return pl.BlockSpec((ts, D_MODEL), lambda i: (i, blk))

    gspec = pl.BlockSpec((1, LANES), lambda i: (0, 0))
    out = _sds((s, D_MODEL), BF16)
    dqt = pl.BlockSpec((8, LANES, ts), lambda i: (0, 0, i))
    return _pc(body, name, (s // ts,), [wide(0), wide(1), dqt, wide(0), wide(0), gspec, gspec],
               [wide(0)] * 3 + [gspec] * 2, [out] * 3 + [_sds((1, LANES))] * 2)(proj, proj, dqs, dk, dv, gq, gk)


def _fox_fwd(x, h, w_in, b_f, q_gain, k_gain, w_out, next_norm, hosted=None):
    proj = _mm(h, w_in, "nn", "fox_in")
    gq = jnp.tile(q_gain, (1, 2))
    gk = jnp.tile(k_gain, (1, 2))
    bf = jnp.pad(b_f, ((0, 0), (0, LANES - ATTN_HEADS)))
    qs, kn, vb = _fox_prep_fwd(proj, gq, gk, "fox_prep")
    cum = _cumsum_rows(_fox_logf(proj, bf, "fox_logf"), False, "fox_cum")[:, :ATTN_HEADS]
    augq, augk = _decay_terms(cum)
    cmask = _causal_bias(x.shape[0])
    o, lse, *got = _flash_fwd(qs, kn, vb, augq, augk, cmask, "fox_attn", hosted=hosted)
    y, h_next = _mm(o, w_out, "nn", "fox_out", res=x, norm_w=next_norm)
    return y, h_next, (x, h, proj, gq, gk, bf, qs, kn, vb, augq, augk, cmask, o, lse), got


def _fox_bwd(dy, saved, norm_w, w_in, w_out, hosted=None):
    x, h, proj, gq, gk, bf, qs, kn, vb, augq, augk, cmask, o, lse = saved
    s = x.shape[0]
    do = _mm(dy, w_out, "nt", "fox_do")
    g_out = _mm(o, dy, "tn", "fox_gout", out_dtype=BF16)
    delta = _fox_delta(do, o, "fox_delta")[:, ::HEAD_DIM].T.reshape(8, 2, s)
    dk, dv, dck, dqs, dcq, *got = _flash_bwd(qs, kn, vb, augq, augk, cmask, do, lse, delta, "fox_dattn", hosted=hosted)
    dcum = jnp.pad(dcq.reshape(ATTN_HEADS, s).T + dck[:, ::HEAD_DIM], ((0, 0), (0, LANES - ATTN_HEADS)))
    dlf = _cumsum_rows(dcum, True, "fox_dcum")
    dfl, g_bf = _fox_dlogf(proj, bf, dlf, "fox_dlogf")
    dq_o, dk_o, dv_o, g_gq, g_gk = _fox_prep_bwd(proj, dqs, dk, dv, gq, gk, "fox_dprep")
    dproj = jnp.concatenate([dq_o, dk_o, dv_o, dfl], axis=1)
    g_in = _mm(h, dproj, "tn", "fox_gin", out_dtype=BF16)
    dx, g_norm = _mm_dnorm_nt(dproj, w_in, "fox_dh", x, norm_w, dy)
    g_q = g_gq[:, :HEAD_DIM] + g_gq[:, HEAD_DIM:]
    g_k = g_gk[:, :HEAD_DIM] + g_gk[:, HEAD_DIM:]
    return dx, g_norm, g_in[:, :FOX_IN], g_bf[:, :ATTN_HEADS], g_q, g_k, g_out, got


def _ssd_conv_fwd(proj, cw, cb, name):
    s = proj.shape[0]
    ts = _row_tile(s)
    w = 1024
    per = ts // 8

    def body(p_ref, h_ref, w_ref, b_ref, o_ref):
        first = pl.program_id(0) == 0
        cur = p_ref[...]
        halo = h_ref[...]
        u = w_ref[3:4, :] * cur + b_ref[...]
        for j in range(1, 4):
            u = u + w_ref[3 - j:4 - j, :] * _prev_rows(cur, halo, j, first)
        o_ref[...] = u * _sigmoid(u)

    return _pc(body, name, (s // ts, 4),
               [pl.BlockSpec((ts, w), lambda i, j: (i, 2 + j)),
                pl.BlockSpec((8, w), lambda i, j: (jnp.maximum(i * per - 1, 0), 2 + j)),
                pl.BlockSpec((4, w), lambda i, j: (0, j)), pl.BlockSpec((1, w), lambda i, j: (0, j))],
               pl.BlockSpec((ts, w), lambda i, j: (i, j)), _sds((s, SSM_CONV_DIM)))(proj, proj, cw, cb)


def _ssd_conv_bwd(proj, dxbc, cw, cb, name):
    s = proj.shape[0]
    ts = _row_tile(s)
    nt = s // ts
    w = 1024
    per = ts // 8
    nblk = s // 8

    def body(p_ref, hp_ref, hn_ref, d_ref, dn_ref, w_ref, b_ref, o_ref, dw_ref, db_ref):
        i = pl.program_id(1)
        first = i == 0
        last = i == nt - 1
        cur = p_ref[...]
        prev = [cur] + [_prev_rows(cur, hp_ref[...], j, first) for j in range(1, 4)]
        nxt = hn_ref[...]
        tail = cur[ts - 8:, :]
        u = b_ref[...]
        un = b_ref[...]
        for j in range(4):
            u = u + w_ref[3 - j:4 - j, :] * prev[j]
            un = un + w_ref[3 - j:4 - j, :] * (nxt if j == 0 else _prev_rows(nxt, tail, j, False))
        sg = _sigmoid(u)
        g = d_ref[...] * (sg * (1.0 + u * (1.0 - sg)))
        sn = _sigmoid(un)
        gn = dn_ref[...] * (sn * (1.0 + un * (1.0 - sn)))

        @pl.when(first)
        def _():
            dw_ref[...] = jnp.zeros_like(dw_ref)
            db_ref[...] = jnp.zeros_like(db_ref)

        dpre = w_ref[3:4, :] * g
        for j in range(1, 4):
            dpre = dpre + w_ref[3 - j:4 - j, :] * _next_rows(g, gn, j, last)
        for j in range(4):
            dw_ref[3 - j:4 - j, :] += jnp.sum(g * prev[j], axis=0, keepdims=True)
        db_ref[...] += jnp.sum(g, axis=0, keepdims=True)
        o_ref[...] = dpre.astype(BF16)

    tile = pl.BlockSpec((ts, w), lambda j, i: (i, j))
    wspec = pl.BlockSpec((4, w), lambda j, i: (0, j))
    vec = pl.BlockSpec((1, w), lambda j, i: (0, j))
    nxt_blk = lambda off: pl.BlockSpec((8, w), lambda j, i: (jnp.minimum((i + 1) * per, nblk - 1), off + j))
    return _pc(body, name, (4, nt),
               [pl.BlockSpec((ts, w), lambda j, i: (i, 2 + j)),
                pl.BlockSpec((8, w), lambda j, i: (jnp.maximum(i * per - 1, 0), 2 + j)), nxt_blk(2),
                tile, nxt_blk(0), wspec, vec],
               [tile, wspec, vec], [_sds((s, SSM_CONV_DIM), BF16), _sds((4, SSM_CONV_DIM)), _sds((1, SSM_CONV_DIM))])(
                   proj, proj, proj, dxbc, dxbc, cw, cb)


def _ssd_dt_fwd(proj, bias, a_neg, name):
    s = proj.shape[0]
    n = s // SSM_CHUNK

    def body(r_ref, b_ref, a_ref, dt_ref, ac_ref):
        real = lax.broadcasted_iota(jnp.int32, (SSM_CHUNK, LANES), 1) < SSM_HEADS
        dt = jnp.where(real, _softplus(r_ref[...] + b_ref[...]), 0.0)
        dt_ref[...] = dt
        ac_ref[...] = _dot(_tri(True), dt * a_ref[...], 1, 0, HI)

    vec = pl.BlockSpec((1, LANES), lambda c: (0, 0))
    row = pl.BlockSpec((SSM_CHUNK, LANES), lambda c: (c, 0))
    return _pc(body, name, (n,), [pl.BlockSpec((SSM_CHUNK, LANES), lambda c: (c, 48)), vec, vec], [row, row],
               [_sds((s, LANES)), _sds((s, LANES))])(proj, bias, a_neg)


def _ssd_dt_bwd(proj, bias, ddt, name):
    s = proj.shape[0]
    ts = _row_tile(s, 512)

    def body(r_ref, b_ref, d_ref, o_ref, db_ref):
        real = lax.broadcasted_iota(jnp.int32, (ts, LANES), 1) < SSM_HEADS
        g = jnp.where(real, d_ref[...] * _sigmoid(r_ref[...] + b_ref[...]), 0.0)
        o_ref[...] = g.astype(BF16)

        @pl.when(pl.program_id(0) == 0)
        def _():
            db_ref[...] = jnp.zeros_like(db_ref)

        db_ref[...] += jnp.sum(g, axis=0, keepdims=True)

    vec = pl.BlockSpec((1, LANES), lambda i: (0, 0))
    row = pl.BlockSpec((ts, LANES), lambda i: (i, 0))
    return _pc(body, name, (s // ts,), [pl.BlockSpec((ts, LANES), lambda i: (i, 48)), vec, row], [row, vec],
               [_sds((s, LANES), BF16), _sds((1, LANES))])(proj, bias, ddt)


def _pair_cols(cols, k0, lo):
    return jnp.where(lo, cols[:, k0:k0 + 1], cols[:, k0 + 1:k0 + 2])


def _last_lane(row):
    lane = lax.broadcasted_iota(jnp.int32, row.shape, 1)
    return jnp.sum(jnp.where(lane == SSM_CHUNK - 1, row, 0.0), axis=-1, keepdims=True)


SSD_FWD_GROUPS = 2
SSD_BWD_GROUPS = 1


def _ssd_specs(nc, rev, n):
    cc = (lambda c: nc - 1 - c) if rev else (lambda c: c)
    nb = SSM_INNER // (LANES * n)
    return dict(
        x=pl.BlockSpec((SSM_CHUNK, 256 * n), lambda g, c: (cc(c), g)),
        b=pl.BlockSpec((SSM_CHUNK, LANES * n), lambda g, c: (cc(c), nb + g)),
        c=pl.BlockSpec((SSM_CHUNK, LANES * n), lambda g, c: (cc(c), nb + SSM_GROUPS // n + g)),
        col=pl.BlockSpec((n, SSM_CHUNK, 4), lambda g, c: (g, cc(c), 0)),
        row=pl.BlockSpec((n, 4, SSM_CHUNK), lambda g, c: (g, 0, cc(c))),
        grp=pl.BlockSpec((n, 1, 256), lambda g, c: (g, 0, 0)),
        grow=pl.BlockSpec((n, 4, LANES), lambda g, c: (g, 0, 0)),
        hs=pl.BlockSpec((1, n, 256, SSM_STATE), lambda g, c: (cc(c), g, 0, 0)),
        bc=pl.BlockSpec((SSM_CHUNK, LANES * n), lambda g, c: (cc(c), g)),
    )


def _ssd_scan_fwd(xbc, dtc, acol, drow, arow, dskip, name):
    s = xbc.shape[0]
    nc = s // SSM_CHUNK
    n = SSD_FWD_GROUPS
    sp = _ssd_specs(nc, False, n)
    L = SSM_CHUNK

    def body(x_ref, b_ref, c_ref, dtc_ref, ac_ref, dr_ref, ar_ref, dk_ref, y_ref, hs_ref, h_ref):
        @pl.when(pl.program_id(1) == 0)
        def _():
            h_ref[...] = jnp.zeros_like(h_ref)

        for gi in range(n):
            group(gi, x_ref, b_ref, c_ref, dtc_ref, ac_ref, dr_ref, ar_ref, dk_ref, y_ref, hs_ref, h_ref)

    def group(gi, x_ref, b_ref, c_ref, dtc_ref, ac_ref, dr_ref, ar_ref, dk_ref, y_ref, hs_ref, h_ref):
        x0 = gi * 256
        bb = b_ref[:, gi * LANES:(gi + 1) * LANES].astype(BF16)
        cb = c_ref[:, gi * LANES:(gi + 1) * LANES].astype(BF16)
        gm = _dot(cb, bb, 1, 1)
        dtc = dtc_ref[gi]
        ac = ac_ref[gi]
        dr = dr_ref[gi]
        ar = ar_ref[gi]
        dsk = dk_ref[gi]
        hs_ref[0, gi] = h_ref[gi]
        tril = lax.broadcasted_iota(jnp.int32, (L, L), 0) >= lax.broadcasted_iota(jnp.int32, (L, L), 1)
        lo = _lo_mask((L, LANES))
        rowlo = lax.broadcasted_iota(jnp.int32, (L, LANES), 0) < HEAD_DIM
        for pr in range(2):
            k0 = 2 * pr
            xp = x_ref[:, x0 + pr * LANES:x0 + (pr + 1) * LANES]
            xpb = xp.astype(BF16)
            hp = h_ref[gi, pr * LANES:(pr + 1) * LANES, :]
            yd, al = [], []
            for k in (k0, k0 + 1):
                seg = ac[:, k:k + 1] - ar[k:k + 1, :]
                wk = gm * jnp.exp(jnp.where(tril, seg, -jnp.inf)) * dr[k:k + 1, :]
                yd.append(_dot(wk.astype(BF16), xpb, 1, 0))
                al.append(_last_lane(ar[k:k + 1, :]))
            e = jnp.exp(_pair_cols(ac, k0, lo))
            yo = _dot(cb, hp.astype(BF16), 1, 1) * e
            y_ref[:, x0 + pr * LANES:x0 + (pr + 1) * LANES] = (
                jnp.where(lo, yd[0], yd[1]) + yo + dsk[:, pr * LANES:(pr + 1) * LANES] * xp)
            wp = jnp.where(lo, jnp.exp(al[0] - ac[:, k0:k0 + 1]) * dtc[:, k0:k0 + 1],
                           jnp.exp(al[1] - ac[:, k0 + 1:k0 + 2]) * dtc[:, k0 + 1:k0 + 2])
            st = _dot((xp * wp).astype(BF16), bb, 0, 0)
            dec = jnp.where(rowlo, jnp.exp(al[0]), jnp.exp(al[1]))
            h_ref[gi, pr * LANES:(pr + 1) * LANES, :] = dec * hp + st

    return _pc(body, name, (SSM_GROUPS // n, nc),
               [sp["x"], sp["b"], sp["c"], sp["col"], sp["col"], sp["row"], sp["row"], sp["grp"]],
               [sp["x"], sp["hs"]], [_sds((s, SSM_INNER)), _sds((nc, SSM_GROUPS, 256, SSM_STATE))],
               [pltpu.VMEM((n, 256, SSM_STATE), F32)])(xbc, xbc, xbc, dtc, acol, drow, arow, dskip)


def _ssd_scan_bwd(xbc, dtc, acol, drow, arow, dskip, agrp, hs, dy, name):
    s = xbc.shape[0]
    nc = s // SSM_CHUNK
    n = SSD_BWD_GROUPS
    sp = _ssd_specs(nc, True, n)
    L = SSM_CHUNK

    def body(x_ref, b_ref, c_ref, dtc_ref, ac_ref, dr_ref, ar_ref, dk_ref, ag_ref, hs_ref, dy_ref,
             dx_ref, db_ref, dc_ref, ddt_ref, da_ref, dd_ref, dh_ref):
        @pl.when(pl.program_id(1) == 0)
        def _():
            dh_ref[...] = jnp.zeros_like(dh_ref)
            da_ref[...] = jnp.zeros_like(da_ref)
            dd_ref[...] = jnp.zeros_like(dd_ref)

        for gi in range(n):
            group(gi, x_ref, b_ref, c_ref, dtc_ref, ac_ref, dr_ref, ar_ref, dk_ref, ag_ref, hs_ref, dy_ref,
                  dx_ref, db_ref, dc_ref, ddt_ref, da_ref, dd_ref, dh_ref)

    def group(gi, x_ref, b_ref, c_ref, dtc_ref, ac_ref, dr_ref, ar_ref, dk_ref, ag_ref, hs_ref, dy_ref,
              dx_ref, db_ref, dc_ref, ddt_ref, da_ref, dd_ref, dh_ref):
        x0 = gi * 256
        bcols = slice(gi * LANES, (gi + 1) * LANES)
        bb = b_ref[:, bcols].astype(BF16)
        cb = c_ref[:, bcols].astype(BF16)
        gm = _dot(cb, bb, 1, 1)
        dtc = dtc_ref[gi]
        ac = ac_ref[gi]
        dr = dr_ref[gi]
        ar = ar_ref[gi]
        dsk = dk_ref[gi]
        ag = ag_ref[gi]
        tril = lax.broadcasted_iota(jnp.int32, (L, L), 0) >= lax.broadcasted_iota(jnp.int32, (L, L), 1)
        lo = _lo_mask((L, LANES))
        nlo = jnp.logical_not(lo)
        rowlo = lax.broadcasted_iota(jnp.int32, (L, LANES), 0) < HEAD_DIM
        lane = lax.broadcasted_iota(jnp.int32, (L, LANES), 1)
        lane_row = lax.broadcasted_iota(jnp.int32, (1, LANES), 1)
        dgm = jnp.zeros((L, L), F32)
        dcm = jnp.zeros((L, SSM_STATE), F32)
        dbm = jnp.zeros((L, SSM_STATE), F32)
        cols = jnp.zeros((L, LANES), F32)
        rows_ddt, rows_q, al_all, dcd_all = [], [], [], []
        for pr in range(2):
            k0 = 2 * pr
            xcols = slice(x0 + pr * LANES, x0 + (pr + 1) * LANES)
            xp = x_ref[:, xcols]
            xpb = xp.astype(BF16)
            dyp = dy_ref[:, xcols]
            dypb = dyp.astype(BF16)
            zero = jnp.zeros_like(dypb)
            hp = hs_ref[0, gi, pr * LANES:(pr + 1) * LANES, :]
            hpb = hp.astype(BF16)
            dst = dh_ref[gi, pr * LANES:(pr + 1) * LANES, :]
            dstb = dst.astype(BF16)
            dxd, al = [], []
            for k in (k0, k0 + 1):
                sel = lo if k == k0 else nlo
                seg = ac[:, k:k + 1] - ar[k:k + 1, :]
                lam = jnp.exp(jnp.where(tril, seg, -jnp.inf))
                wk = gm * lam * dr[k:k + 1, :]
                dwk = _dot(jnp.where(sel, dypb, zero), xpb, 1, 1)
                mk = dwk * gm * lam
                qk = mk * dr[k:k + 1, :]
                dgm = dgm + dwk * lam * dr[k:k + 1, :]
                rows_ddt.append(jnp.sum(mk, axis=0, keepdims=True))
                rows_q.append(jnp.sum(qk, axis=0, keepdims=True))
                cols = jnp.where(lane == k, jnp.sum(qk, axis=-1, keepdims=True), cols)
                dxd.append(_dot(wk.astype(BF16), dypb, 0, 0))
                al.append(_last_lane(ar[k:k + 1, :]))
            al_all += al
            dxp = jnp.where(lo, dxd[0], dxd[1])
            e = jnp.exp(_pair_cols(ac, k0, lo))
            dye = dyp * e
            dyeb = dye.astype(BF16)
            dcm = dcm + _dot(dyeb, hpb, 1, 0)
            dh_yoff = _dot(dyeb, cb, 0, 0)
            tq = dye * _dot(cb, hpb, 1, 1)
            cols = jnp.where(lane == 4 + k0, jnp.sum(jnp.where(lo, tq, 0.0), axis=-1, keepdims=True), cols)
            cols = jnp.where(lane == 5 + k0, jnp.sum(jnp.where(lo, 0.0, tq), axis=-1, keepdims=True), cols)
            wp = jnp.where(lo, jnp.exp(al[0] - ac[:, k0:k0 + 1]) * dtc[:, k0:k0 + 1],
                           jnp.exp(al[1] - ac[:, k0 + 1:k0 + 2]) * dtc[:, k0 + 1:k0 + 2])
            dxw = _dot(bb, dstb, 1, 1)
            dxp = dxp + dxw * wp
            tw = xp * dxw
            cols = jnp.where(lane == 8 + k0, jnp.sum(jnp.where(lo, tw, 0.0), axis=-1, keepdims=True), cols)
            cols = jnp.where(lane == 9 + k0, jnp.sum(jnp.where(lo, 0.0, tw), axis=-1, keepdims=True), cols)
            dbm = dbm + _dot((xp * wp).astype(BF16), dstb, 1, 0)
            dsl = dsk[:, pr * LANES:(pr + 1) * LANES]
            dx_ref[:, xcols] = dxp + dsl * dyp
            dd_ref[gi, :, pr * LANES:(pr + 1) * LANES] += jnp.sum(dyp * xp, axis=0, keepdims=True)
            prod = dst * hp
            dcd_all.append(jnp.sum(jnp.sum(jnp.where(rowlo, prod, 0.0), axis=-1, keepdims=True), axis=0, keepdims=True))
            dcd_all.append(jnp.sum(jnp.sum(jnp.where(rowlo, 0.0, prod), axis=-1, keepdims=True), axis=0, keepdims=True))
            dec = jnp.where(rowlo, jnp.exp(al[0]), jnp.exp(al[1]))
            dh_ref[gi, pr * LANES:(pr + 1) * LANES, :] = dec * dst + dh_yoff
        dgb = dgm.astype(BF16)
        dc_ref[:, bcols] = dcm + _dot(dgb, bb, 1, 0)
        db_ref[:, bcols] = dbm + _dot(dgb, cb, 0, 0)
        colt = cols.T
        sub8 = lax.broadcasted_iota(jnp.int32, (8, LANES), 0)
        da_rows = jnp.zeros((8, LANES), F32)
        ddt_part = []
        for k in range(4):
            rs = colt[k:k + 1, :]
            uo = colt[4 + k:5 + k, :]
            dwl = colt[8 + k:9 + k, :]
            es = jnp.exp(al_all[k] - ar[k:k + 1, :])
            wrow = es * dr[k:k + 1, :]
            dwl_w = dwl * wrow
            da_k = rs - rows_q[k] + uo - dwl_w
            tail = jnp.sum(dwl_w, axis=-1, keepdims=True) + jnp.exp(al_all[k]) * dcd_all[k]
            da_k = da_k + jnp.where(lane_row == L - 1, tail, 0.0)
            da_rows = jnp.where(sub8 == k, da_k, da_rows)
            ddt_part.append(rows_ddt[k] + dwl * es)
        dda = _dot(da_rows, _tri(True), 1, 0, HI)
        for k in range(4):
            dda_k = dda[k:k + 1, :]
            ddt_ref[gi, k:k + 1, :] = ddt_part[k] + dda_k * ag[k:k + 1, :]
            da_ref[gi, k:k + 1, :] += dda_k * dr[k:k + 1, :] * ag[k:k + 1, :]

    return _pc(body, name, (SSM_GROUPS // n, nc),
               [sp["x"], sp["b"], sp["c"], sp["col"], sp["col"], sp["row"], sp["row"], sp["grp"], sp["grow"], sp["hs"], sp["x"]],
               [sp["x"], sp["bc"], sp["bc"], sp["row"], sp["grow"], sp["grp"]],
               [_sds((s, SSM_INNER)), _sds((s, 1024)), _sds((s, 1024)), _sds((SSM_GROUPS, 4, s)),
                _sds((SSM_GROUPS, 4, LANES)), _sds((SSM_GROUPS, 1, 256))],
               [pltpu.VMEM((n, 256, SSM_STATE), F32)])(xbc, xbc, xbc, dtc, acol, drow, arow, dskip, agrp, hs, dy)


def _gnorm_fwd(y, proj, nw, name):
    s = y.shape[0]
    ts = _row_tile(s)
    gw = SSM_INNER // SSM_GROUPS

    def body(y_ref, z_ref, w_ref, o_ref):
        for g in range(SSM_GROUPS):
            sl = slice(g * gw, (g + 1) * gw)
            z = z_ref[:, sl]
            y2 = y_ref[:, sl] * (z * _sigmoid(z))
            r = lax.rsqrt(jnp.mean(y2 * y2, axis=-1, keepdims=True) + RMS_EPS)
            o_ref[:, sl] = ((y2 * r) * w_ref[:, sl]).astype(BF16)

    row = pl.BlockSpec((ts, SSM_INNER), lambda i: (i, 0))
    return _pc(body, name, (s // ts,), [row, row, pl.BlockSpec((1, SSM_INNER), lambda i: (0, 0))], row,
               _sds((s, SSM_INNER), BF16))(y, proj, nw)


def _gnorm_bwd(y, proj, nw, dyn, name):
    s = y.shape[0]
    ts = _row_tile(s)
    gw = SSM_INNER // SSM_GROUPS

    def body(y_ref, z_ref, w_ref, d_ref, dy_ref, dz_ref, dw_ref):
        @pl.when(pl.program_id(0) == 0)
        def _():
            dw_ref[...] = jnp.zeros_like(dw_ref)

        for g in range(SSM_GROUPS):
            sl = slice(g * gw, (g + 1) * gw)
            z = z_ref[:, sl]
            yv = y_ref[:, sl]
            sg = _sigmoid(z)
            sz = z * sg
            y2 = yv * sz
            r = lax.rsqrt(jnp.mean(y2 * y2, axis=-1, keepdims=True) + RMS_EPS)
            yn = y2 * r
            dout = d_ref[:, sl]
            dyg = dout * w_ref[:, sl]
            dy2 = r * (dyg - yn * jnp.mean(dyg * yn, axis=-1, keepdims=True))
            dy_ref[:, sl] = dy2 * sz
            dz_ref[:, sl] = (dy2 * yv * (sg * (1.0 + z * (1.0 - sg)))).astype(BF16)
            dw_ref[:, sl] += jnp.sum(dout * yn, axis=0, keepdims=True)

    row = pl.BlockSpec((ts, SSM_INNER), lambda i: (i, 0))
    vec = pl.BlockSpec((1, SSM_INNER), lambda i: (0, 0))
    return _pc(body, name, (s // ts,), [row, row, vec, row], [row, row, vec],
               [_sds((s, SSM_INNER)), _sds((s, SSM_INNER), BF16), _sds((1, SSM_INNER))])(y, proj, nw, dyn)


def _head_layouts(v, s):
    return v.reshape(s, SSM_GROUPS, 4).transpose(1, 0, 2), v.T.reshape(SSM_GROUPS, 4, s)


def _ssd_fwd(x, h, w_in, conv_w, conv_b, dt_bias, a_log, d_skip, gnorm_w, w_out, next_norm):
    s = x.shape[0]
    proj = _mm(h, w_in, "nn", "ssd_in")
    xbc = _ssd_conv_fwd(proj, conv_w, conv_b, "ssd_conv")
    pad = ((0, 0), (0, LANES - SSM_HEADS))
    a_neg = -jnp.exp(a_log)
    bias = jnp.pad(dt_bias, pad)
    dt, acum = _ssd_dt_fwd(proj, bias, jnp.pad(a_neg, pad), "ssd_dt")
    dtc, drow = _head_layouts(dt[:, :SSM_HEADS], s)
    acol, arow = _head_layouts(acum[:, :SSM_HEADS], s)
    dskip = jnp.repeat(d_skip.reshape(SSM_GROUPS, 1, 4), HEAD_DIM, axis=2)
    y, hs = _ssd_scan_fwd(xbc, dtc, acol, drow, arow, dskip, "ssd_scan")
    yn = _gnorm_fwd(y, proj, gnorm_w, "ssd_gnorm")
    out, h_next = _mm(yn, w_out, "nn", "ssd_out", res=x, norm_w=next_norm)
    return out, h_next, (x, h, proj, xbc, bias, a_neg, dtc, acol, drow, arow, dskip, y, hs, yn)


def _ssd_bwd(dout, saved, norm_w, w_in, conv_w, conv_b, gnorm_w, w_out):
    x, h, proj, xbc, bias, a_neg, dtc, acol, drow, arow, dskip, y, hs, yn = saved
    s = x.shape[0]
    dyn = _mm(dout, w_out, "nt", "ssd_dyn")
    g_out = _mm(yn, dout, "tn", "ssd_gout", out_dtype=BF16)
    dy, dz, g_gnorm = _gnorm_bwd(y, proj, gnorm_w, dyn, "ssd_dgnorm")
    agrp = jnp.broadcast_to(a_neg.reshape(SSM_GROUPS, 4, 1), (SSM_GROUPS, 4, LANES))
    dxs, db, dc, ddt_row, da_acc, dd_acc = _ssd_scan_bwd(xbc, dtc, acol, drow, arow, dskip, agrp, hs, dy, "ssd_dscan")
    dxbc = jnp.concatenate([dxs, db, dc], axis=1)
    dpre, g_cw, g_cb = _ssd_conv_bwd(proj, dxbc, conv_w, conv_b, "ssd_dconv")
    ddt = jnp.pad(ddt_row.reshape(SSM_HEADS, s).T, ((0, 0), (0, LANES - SSM_HEADS)))
    ddtraw, g_dtb = _ssd_dt_bwd(proj, bias, ddt, "ssd_ddt")
    dproj = jnp.concatenate([dz, dpre, ddtraw], axis=1)
    g_in = _mm(h, dproj, "tn", "ssd_gin", out_dtype=BF16)
    dx, g_norm = _mm_dnorm_nt(dproj, w_in, "ssd_dh", x, norm_w, dout)
    g_alog = jnp.sum(da_acc, axis=-1).reshape(1, SSM_HEADS)
    g_d = jnp.sum(dd_acc.reshape(SSM_GROUPS, 4, HEAD_DIM), axis=-1).reshape(1, SSM_HEADS)
    return dx, g_norm, g_in[:, :SSM_IN], g_cw, g_cb, g_dtb[:, :SSM_HEADS], g_alog, g_d, g_gnorm, g_out


def _loss_head(y, target, name):
    s, d = y.shape
    ts = _row_tile(s)

    def body(y_ref, t_ref, dy_ref, l_ref):
        @pl.when(pl.program_id(0) == 0)
        def _():
            l_ref[...] = jnp.zeros_like(l_ref)

        e = y_ref[...] - t_ref[...]
        dy_ref[...] = e * (1.0 / d)
        part = jnp.sum(jnp.sum(e * e, axis=-1, keepdims=True), axis=0, keepdims=True) * (0.5 / d)
        l_ref[...] += jnp.broadcast_to(part, l_ref.shape)

    row = pl.BlockSpec((ts, d), lambda i: (i, 0))
    dy, lacc = _pc(body, name, (s // ts,), [row, row], [row, pl.BlockSpec((8, LANES), lambda i: (0, 0))],
                   [_sds((s, d)), _sds((8, LANES))])(y, target)
    return lacc[0, 0], dy


def _local_step(x, target, w, gather_layer0=None, gather_rest=None, scatter_first=None, scatter_layer0=None):
    saved = []
    received, received_layer0 = None, {}

    def layer0_stages():
        def entry(stage):
            shards, finish = gather_layer0[stage]

            def on_arrival(got):
                nonlocal w
                w = finish(w, got)
                return {"conv_gate": lambda: w["conv_w_out"][0], "ffn_gu": lambda: w["ffn_w_down"][0]}.get(stage, lambda: None)()
            return shards, on_arrival
        return {stage: entry(stage) for stage in gather_layer0}

    at = lambda weights, n: weights[n] if n < len(weights) else None
    h = _rms_fwd(x, w["mix_norm"][0:1], "first_norm")
    for i in range(DEPTH):
        kind, j = i % 3, i // 3
        fn = w["ffn_norm"][i:i + 1]
        stages = layer0_stages() if (i == 0 and gather_layer0 is not None) else None
        if kind == 0:
            x, h, sv = _conv_fwd(x, h, w["conv_w_in"][j], w["conv_w_dw"][j], at(w["conv_w_out"], j), str(i), fn, stages)
        elif kind == 1:
            hosted = None if gather_rest is None else (gather_rest[0], True)
            x, h, sv, got = _fox_fwd(x, h, w["fox_w_in"], w["fox_b_f"], w["fox_q_gain"], w["fox_k_gain"], w["fox_w_out"], fn, hosted)
            if gather_rest is not None:
                w = gather_rest[1](w, got)
        else:
            x, h, sv = _ssd_fwd(x, h, w["ssd_w_in"], w["ssd_conv_w"], w["ssd_conv_b"], w["ssd_dt_bias"],
                                w["ssd_a_log"], w["ssd_d"], w["ssd_norm_w"], w["ssd_w_out"], fn)
        nxt = w["mix_norm"][i + 1:i + 2] if i + 1 < DEPTH else None
        x, h, sf = _ffn_fwd(x, h, w["ffn_w_gu"][i], at(w["ffn_w_down"], i), str(i), nxt, stages)
        saved.append((sv, sf))
    loss, dx = _loss_head(x, target, "loss_head")
    g = {k: [None] * n for k, n in (("mix_norm", DEPTH), ("ffn_norm", DEPTH), ("ffn_w_gu", DEPTH), ("ffn_w_down", DEPTH),
                                    ("conv_w_in", 2), ("conv_w_dw", 2), ("conv_w_out", 2))}
    for i in reversed(range(DEPTH)):
        kind, j = i % 3, i // 3
        sv, sf = saved[i]
        hosted_fn = None
        if i == 0 and scatter_layer0 is not None:
            hosted_fn = lambda stage, **new: scatter_layer0(stage, g, **new)
        dx, g["ffn_norm"][i], g["ffn_w_gu"][i], g["ffn_w_down"][i], got = _ffn_bwd(
            dx, sf, w["ffn_norm"][i:i + 1], w["ffn_w_gu"][i], w["ffn_w_down"][i], str(i), hosted_fn)
        received_layer0.update(got)
        mn = w["mix_norm"][i:i + 1]
        if kind == 0:
            dx, g["mix_norm"][i], g["conv_w_in"][j], g["conv_w_dw"][j], g["conv_w_out"][j], got = _conv_bwd(
                dx, sv, mn, w["conv_w_in"][j], w["conv_w_dw"][j], w["conv_w_out"][j], str(i), hosted_fn)
            received_layer0.update(got)
        elif kind == 1:
            hosted = None if scatter_first is None else (scatter_first(g), False)
            (dx, g["mix_norm"][i], g["fox_w_in"], g["fox_b_f"], g["fox_q_gain"], g["fox_k_gain"],
             g["fox_w_out"], received) = _fox_bwd(dx, sv, mn, w["fox_w_in"], w["fox_w_out"], hosted)
        else:
            (dx, g["mix_norm"][i], g["ssd_w_in"], g["ssd_conv_w"], g["ssd_conv_b"], g["ssd_dt_bias"], g["ssd_a_log"],
             g["ssd_d"], g["ssd_norm_w"], g["ssd_w_out"]) = _ssd_bwd(
                 dx, sv, mn, w["ssd_w_in"], w["ssd_conv_w"], w["ssd_conv_b"], w["ssd_norm_w"], w["ssd_w_out"])
    g["mix_norm"] = jnp.concatenate(g["mix_norm"], axis=0)
    g["ffn_norm"] = jnp.concatenate(g["ffn_norm"], axis=0)
    g["conv_w_dw"] = jnp.stack(g["conv_w_dw"], axis=0)
    g["ssd_conv_w"] = g["ssd_conv_w"][None]
    return loss, dx, g, received, received_layer0


def _mesh_position():
    return lax.axis_index("x") * 4 + lax.axis_index("y") * 2 + lax.axis_index("c")


def _device_of(t):
    return (lax.shift_right_logical(t, 2), lax.bitwise_and(lax.shift_right_logical(t, 1), 1), lax.bitwise_and(t, 1))


def _exchange_copies(srcs_of, out_refs, send_sems, recv_sems, local_sems):
    me = _mesh_position()
    na = len(out_refs)
    locals_ = [pltpu.make_async_copy(srcs_of(a, me), out_refs[a].at[me], local_sems.at[a]) for a in range(na)]
    sends, arrivals = [], []
    for j in range(1, NDEV):
        t = lax.rem(me + j, NDEV)
        frm = lax.rem(me + NDEV - j, NDEV)
        for a in range(na):
            sends.append(pltpu.make_async_remote_copy(
                src_ref=srcs_of(a, t), dst_ref=out_refs[a].at[me], send_sem=send_sems.at[a, j - 1],
                recv_sem=recv_sems.at[a, j - 1], device_id=_device_of(t), device_id_type=pl.DeviceIdType.MESH))
            arrivals.append(pltpu.make_async_remote_copy(
                src_ref=srcs_of(a, me), dst_ref=out_refs[a].at[frm], send_sem=send_sems.at[a, j - 1],
                recv_sem=recv_sems.at[a, j - 1], device_id=_device_of(frm), device_id_type=pl.DeviceIdType.MESH))
    return locals_, sends, arrivals


def _exchange_start(copies):
    locals_, sends, _ = copies
    for cp in locals_ + sends:
        cp.start()


def _exchange_wait(copies):
    locals_, sends, arrivals = copies
    for cp in sends:
        cp.wait_send()
    for cp in arrivals:
        cp.wait_recv()
    for cp in locals_:
        cp.wait()


def _exchange_run(srcs_of, out_refs, send_sems, recv_sems, local_sems):
    copies = _exchange_copies(srcs_of, out_refs, send_sems, recv_sems, local_sems)
    _exchange_start(copies)
    _exchange_wait(copies)


def _exchange_parts(arrays, gather):
    na = len(arrays)
    outs = [_sds(((NDEV,) + a.shape) if gather else a.shape, a.dtype) for a in arrays]
    sems = [pltpu.SemaphoreType.DMA((na, NDEV - 1)), pltpu.SemaphoreType.DMA((na, NDEV - 1)), pltpu.SemaphoreType.DMA((na,))]
    pick = (lambda srcs: (lambda a, t: srcs[a])) if gather else (lambda srcs: (lambda a, t: srcs[a].at[t]))
    return pick, outs, sems


def _exchange(arrays, name, gather):
    na = len(arrays)
    pick, outs, sems = _exchange_parts(arrays, gather)

    def body(*refs):
        _exchange_run(pick(refs[:na]), refs[na:2 * na], *refs[2 * na:])

    hbm = pl.BlockSpec(memory_space=pl.ANY)
    return pl.pallas_call(body, name=name, in_specs=[hbm] * na, out_specs=[hbm] * na, out_shape=outs, scratch_shapes=sems)(*arrays)


def _all_sum_small(pack, name):
    def body(src_ref, out_ref, buf_ref, send_sems, recv_sems, local_sems):
        _exchange_run(lambda a, t: src_ref, [buf_ref], send_sems, recv_sems, local_sems)
        acc = buf_ref[0]
        for d in range(1, NDEV):
            acc = acc + buf_ref[d]
        out_ref[...] = acc

    vmem = pl.BlockSpec(memory_space=pltpu.VMEM)
    return pl.pallas_call(
        body, name=name, in_specs=[vmem], out_specs=vmem, out_shape=_sds(pack.shape, pack.dtype),
        scratch_shapes=[pltpu.VMEM((NDEV,) + pack.shape, pack.dtype), pltpu.SemaphoreType.DMA((1, NDEV - 1)),
                        pltpu.SemaphoreType.DMA((1, NDEV - 1)), pltpu.SemaphoreType.DMA((1,))])(pack)


def _sum_slabs(slabs, name):
    _, r, c = slabs.shape
    tr = r
    for cand in (256, 352):
        if r % cand == 0:
            tr = cand
            break

    def body(s_ref, o_ref):
        acc = s_ref[0].astype(F32)
        for d in range(1, NDEV):
            acc = acc + s_ref[d].astype(F32)
        o_ref[...] = acc

    return _pc(body, name, (r // tr,), [pl.BlockSpec((NDEV, tr, c), lambda i: (0, i, 0))],
               pl.BlockSpec((tr, c), lambda i: (i, 0)), _sds((r, c)))(slabs)


def _adamw(wt, g, m, v, name):
    shape = wt.shape
    w2, g2, m2, v2 = (a.reshape(-1, shape[-1]) for a in (wt, g, m, v))
    r, c = w2.shape
    tr = r
    for cand in (512, 352, 256):
        if r % cand == 0:
            tr = cand
            break
    c1 = 1.0 - ADAM_B1 ** ADAM_STEP
    c2 = 1.0 - ADAM_B2 ** ADAM_STEP

    def body(w_ref, g_ref, m_ref, v_ref, d_ref, mo_ref, vo_ref):
        gv = g_ref[...]
        mn = ADAM_B1 * m_ref[...] + (1.0 - ADAM_B1) * gv
        vn = ADAM_B2 * v_ref[...] + (1.0 - ADAM_B2) * (gv * gv)
        mo_ref[...] = mn
        vo_ref[...] = vn
        d_ref[...] = -ADAM_LR * ((mn / c1) / (jnp.sqrt(vn / c2) + ADAM_EPS) + ADAM_WD * w_ref[...])

    spec = pl.BlockSpec((tr, c), lambda i: (i, 0))
    outs = _pc(body, name, (r // tr,), [spec] * 4, [spec] * 3, [_sds((r, c))] * 3)(w2, g2, m2, v2)
    return tuple(o.reshape(shape) for o in outs)


_NAMES = ["mix_norm", "ffn_norm", "ffn_w_gu", "ffn_w_down", "conv_w_in", "conv_w_dw", "conv_w_out", "fox_w_in", "fox_b_f",
          "fox_q_gain", "fox_k_gain", "fox_w_out", "ssd_w_in", "ssd_conv_w", "ssd_conv_b", "ssd_dt_bias", "ssd_a_log",
          "ssd_d", "ssd_norm_w", "ssd_w_out"]
_MATRICES = ["ffn_w_gu", "ffn_w_down", "conv_w_in", "conv_w_out", "fox_w_in", "fox_w_out", "ssd_w_in", "ssd_w_out"]
_VECTORS = {"conv_w_dw": 2, "ssd_conv_w": 2, "ssd_conv_b": 1, "ssd_norm_w": 1}
_REPLICATED = ["mix_norm", "ffn_norm", "fox_b_f", "fox_q_gain", "fox_k_gain", "ssd_dt_bias", "ssd_a_log", "ssd_d"]


def _to_rows(flat):
    n = flat.shape[0]
    rows = -(-n // (8 * D_MODEL)) * 8
    return jnp.pad(flat, (0, rows * D_MODEL - n)).reshape(rows, D_MODEL)


def _full_shape(local_shape, axis):
    shp = list(local_shape)
    shp[axis] *= NDEV
    return tuple(shp)


def _cols_from_blocks(g):
    return jnp.moveaxis(g, 0, 1).reshape(g.shape[1], NDEV * g.shape[2])


def _blocks_from_cols(full):
    k, n8 = full.shape
    return jnp.moveaxis(full.reshape(k, NDEV, n8 // NDEV), 1, 0)


def kernel(x, mix_norm, ffn_norm, ffn_w_gu, ffn_w_down, conv_w_in, conv_w_dw, conv_w_out, fox_w_in, fox_b_f, fox_q_gain, fox_k_gain, fox_w_out, ssd_w_in, ssd_conv_w, ssd_conv_b, ssd_dt_bias, ssd_a_log, ssd_d, ssd_norm_w, ssd_w_out, loss_target, m_mix_norm, m_ffn_norm, m_ffn_w_gu, m_ffn_w_down, m_conv_w_in, m_conv_w_dw, m_conv_w_out, m_fox_w_in, m_fox_b_f, m_fox_q_gain, m_fox_k_gain, m_fox_w_out, m_ssd_w_in, m_ssd_conv_w, m_ssd_conv_b, m_ssd_dt_bias, m_ssd_a_log, m_ssd_d, m_ssd_norm_w, m_ssd_w_out, v_mix_norm, v_ffn_norm, v_ffn_w_gu, v_ffn_w_down, v_conv_w_in, v_conv_w_dw, v_conv_w_out, v_fox_w_in, v_fox_b_f, v_fox_q_gain, v_fox_k_gain, v_fox_w_out, v_ssd_w_in, v_ssd_conv_w, v_ssd_conv_b, v_ssd_dt_bias, v_ssd_a_log, v_ssd_d, v_ssd_norm_w, v_ssd_w_out):
    local = dict(mix_norm=mix_norm, ffn_norm=ffn_norm, ffn_w_gu=ffn_w_gu, ffn_w_down=ffn_w_down, conv_w_in=conv_w_in,
                 conv_w_dw=conv_w_dw, conv_w_out=conv_w_out, fox_w_in=fox_w_in, fox_b_f=fox_b_f, fox_q_gain=fox_q_gain,
                 fox_k_gain=fox_k_gain, fox_w_out=fox_w_out, ssd_w_in=ssd_w_in, ssd_conv_w=ssd_conv_w, ssd_conv_b=ssd_conv_b,
                 ssd_dt_bias=ssd_dt_bias, ssd_a_log=ssd_a_log, ssd_d=ssd_d, ssd_norm_w=ssd_norm_w, ssd_w_out=ssd_w_out)
    mom = dict(zip(_NAMES, [m_mix_norm, m_ffn_norm, m_ffn_w_gu, m_ffn_w_down, m_conv_w_in, m_conv_w_dw, m_conv_w_out, m_fox_w_in,
                            m_fox_b_f, m_fox_q_gain, m_fox_k_gain, m_fox_w_out, m_ssd_w_in, m_ssd_conv_w, m_ssd_conv_b,
                            m_ssd_dt_bias, m_ssd_a_log, m_ssd_d, m_ssd_norm_w, m_ssd_w_out]))
    var = dict(zip(_NAMES, [v_mix_norm, v_ffn_norm, v_ffn_w_gu, v_ffn_w_down, v_conv_w_in, v_conv_w_dw, v_conv_w_out, v_fox_w_in,
                            v_fox_b_f, v_fox_q_gain, v_fox_k_gain, v_fox_w_out, v_ssd_w_in, v_ssd_conv_w, v_ssd_conv_b,
                            v_ssd_dt_bias, v_ssd_a_log, v_ssd_d, v_ssd_norm_w, v_ssd_w_out]))

    shard = {k: local[k].astype(BF16) for k in _MATRICES}
    vec_pack = _to_rows(jnp.concatenate([local[k].reshape(-1) for k in _VECTORS]))
    first = _exchange([shard["conv_w_in"][0:1], vec_pack], "gather_first", True)
    gvec = first[1].reshape(NDEV, -1)
    full = {k: local[k] for k in _REPLICATED}
    off = 0
    for k, axis in _VECTORS.items():
        n = local[k].size
        blk = jnp.moveaxis(gvec[:, off:off + n].reshape((NDEV,) + local[k].shape), 0, axis)
        full[k] = blk.reshape(_full_shape(local[k].shape, axis))
        off += n
    full["ssd_conv_w"] = full["ssd_conv_w"][0]
    full["conv_w_in"] = [first[0][:, 0]]
    full["conv_w_out"], full["ffn_w_gu"], full["ffn_w_down"] = [], [], []

    def finish_gu0(w, got):
        return dict(w, ffn_w_gu=[got[0][:, 0]])

    def finish_out0(w, got):
        return dict(w, conv_w_out=[got[0][:, 0].reshape(D_MODEL, D_MODEL)])

    def finish_down0(w, got):
        return dict(w, ffn_w_down=[got[0][:, 0].reshape(4, FF_BLOCK, D_MODEL)])

    def finish_fox(w, got):
        return dict(w, fox_w_in=jnp.pad(_cols_from_blocks(got[0][:, 0]), ((0, 0), (0, FOX_IN_PAD - FOX_IN))),
                    fox_w_out=got[1].reshape(D_MODEL, D_MODEL))

    layer0 = {"conv_in": ([shard["ffn_w_gu"][0:1]], finish_gu0), "conv_gate": ([shard["conv_w_out"][0:1]], finish_out0),
              "ffn_gu": ([shard["ffn_w_down"][0:1]], finish_down0), "ffn_down": ([shard["fox_w_in"], shard["fox_w_out"]], finish_fox)}

    rest = [shard["ffn_w_gu"][1:], shard["ffn_w_down"][1:], shard["conv_w_in"][1:], shard["conv_w_out"][1:],
            shard["ssd_w_in"], shard["ssd_w_out"]]

    def finish(w, got):
        w = dict(w)
        w["ffn_w_gu"] = w["ffn_w_gu"] + [got[0][:, i] for i in range(DEPTH - 1)]
        w["ffn_w_down"] = w["ffn_w_down"] + [got[1][:, i].reshape(4, FF_BLOCK, D_MODEL) for i in range(DEPTH - 1)]
        w["conv_w_in"] = w["conv_w_in"] + [got[2][:, 0]]
        w["conv_w_out"] = w["conv_w_out"] + [got[3][:, 0].reshape(D_MODEL, D_MODEL)]
        w["ssd_w_in"] = jnp.pad(_cols_from_blocks(got[4][:, 0]), ((0, 0), (0, SSM_IN_PAD - SSM_IN)))
        w["ssd_w_out"] = got[5].reshape(SSM_INNER, D_MODEL)
        return w

    def early_slabs(g):
        return ([g["ffn_w_gu"][i] for i in range(1, DEPTH)]
                + [g["ffn_w_down"][i].reshape(NDEV, D_FF // NDEV, D_MODEL) for i in range(1, DEPTH)]
                + [g["conv_w_in"][1], g["conv_w_out"][1].reshape(NDEV, D_MODEL // NDEV, D_MODEL),
                   _blocks_from_cols(g["ssd_w_in"]), g["ssd_w_out"].reshape(NDEV, SSM_INNER // NDEV, D_MODEL)])

    def layer0_slabs(stage, g, g_down=None, g_gu=None, g_in=None, g_out=None):
        if stage == "ffn_gdown":
            return [_blocks_from_cols(g["fox_w_in"]), g["fox_w_out"].reshape(NDEV, D_MODEL // NDEV, D_MODEL)]
        if stage == "ffn_dgu":
            return [g_down.reshape(NDEV, D_FF // NDEV, D_MODEL)]
        if stage == "ffn_dh":
            return [g_gu]
        if stage == "conv_dh":
            return [g_in, g_out.reshape(NDEV, D_MODEL // NDEV, D_MODEL)]
        return None

    loss_part, dx, grads, early, late = _local_step(x[0], loss_target[0], full, layer0, (rest, finish), early_slabs, layer0_slabs)

    se = [_sum_slabs(r, f"sum_early_{n}") for n, r in enumerate(early)]
    sl = {stage: [_sum_slabs(r, f"sum_{stage}_{n}") for n, r in enumerate(rs)] for stage, rs in late.items()}
    shard_grad = {
        "ffn_w_gu": jnp.stack(sl["ffn_dh"] + se[0:3]), "ffn_w_down": jnp.stack(sl["ffn_dgu"] + se[3:6]),
        "conv_w_in": jnp.stack([sl["conv_dh"][0], se[6]]), "conv_w_out": jnp.stack([sl["conv_dh"][1], se[7]]),
        "fox_w_in": sl["ffn_gdown"][0][None], "fox_w_out": sl["ffn_gdown"][1][None],
        "ssd_w_in": se[8][None], "ssd_w_out": se[9][None]}

    small_names = _REPLICATED + list(_VECTORS)
    small = [jnp.reshape(loss_part, (1,))] + [grads[k].reshape(-1) for k in small_names]
    total = _all_sum_small(_to_rows(jnp.concatenate(small)), "sum_small").reshape(-1)
    loss = total[0]
    off = 1
    me = _mesh_position()
    for k, part in zip(small_names, small[1:]):
        gk = total[off:off + part.shape[0]]
        off += part.shape[0]
        if k in _VECTORS:
            axis = _VECTORS[k]
            shp = local[k].shape
            gfull = gk.reshape(shp[:axis] + (NDEV, shp[axis]) + shp[axis + 1:])
            shard_grad[k] = lax.dynamic_index_in_dim(gfull, me, axis, keepdims=False)
        else:
            shard_grad[k] = gk.reshape(local[k].shape)

    deltas, new_m, new_v = {}, {}, {}
    for k in _NAMES:
        deltas[k], new_m[k], new_v[k] = _adamw(local[k], shard_grad[k], mom[k], var[k], f"adamw_{k}")
    return (loss, dx[None], *[shard_grad[k] for k in _NAMES], *[deltas[k] for k in _NAMES],
            *[new_m[k] for k in _NAMES], *[new_v[k] for k in _NAMES])
```

```python
import numpy as np

import jax
import jax.numpy as jnp
from jax import lax
from jax.experimental import pallas as pl
from jax.experimental.pallas import tpu as pltpu

F32 = jnp.float32
BF16 = jnp.bfloat16
HI = lax.Precision.HIGHEST

NDEV = 8
D_MODEL = 1024
DEPTH = 4
D_FF = 2816
FF_BLOCK = 2 * D_FF // NDEV
RMS_EPS = 1e-6
HEAD_DIM = 64
ATTN_HEADS = 16
FOX_IN = 3 * D_MODEL + ATTN_HEADS
FOX_IN_PAD = 3200
SSM_INNER = 2048
SSM_HEADS = 32
SSM_GROUPS = 8
SSM_STATE = 128
SSM_CHUNK = 128
SSM_CONV_DIM = 4096
SSM_IN = SSM_INNER + SSM_CONV_DIM + SSM_HEADS
SSM_IN_PAD = 6272
LANES = 128
V7X_VMEM_BYTES = 64 * 1024 * 1024
VMEM_LIMIT_BYTES = (V7X_VMEM_BYTES * 3) // 4
ATTN_BWD_VMEM_BYTES = (V7X_VMEM_BYTES * 7) // 8
LOG2E = 1.4426950408889634
LN2 = 0.6931471805599453
ATTN_TILE = 1024
ATTN_ROWS = 32

ADAM_LR = 0.001
ADAM_B1 = 0.9
ADAM_B2 = 0.999
ADAM_EPS = 1e-08
ADAM_WD = 0.01
ADAM_STEP = 10

_TILE_CANDIDATES = (1024, 1408, 896, 768, 640, 512, 384, 256, 128)


def _pick_tile(n):
    for c in _TILE_CANDIDATES:
        if n % c == 0:
            return c
    raise ValueError(f"no tile for {n}")


def _params(ngrid):
    return pltpu.CompilerParams(dimension_semantics=("arbitrary",) * ngrid, vmem_limit_bytes=VMEM_LIMIT_BYTES)


def _pc(body, name, grid, in_specs, out_specs, out_shape, scratch=(), hosted=None):
    if hosted is None:
        return pl.pallas_call(
            body, name=name, grid=grid, in_specs=in_specs, out_specs=out_specs, out_shape=out_shape,
            scratch_shapes=list(scratch), compiler_params=_params(len(grid)))
    arrays, gather = hosted
    single = not isinstance(out_shape, (list, tuple))
    outs = [out_shape] if single else list(out_shape)
    ospecs = [out_specs] if single else list(out_specs)
    na, n_in, n_out, n_scr = len(arrays), len(in_specs), len(outs), len(scratch)
    pick, xouts, sems = _exchange_parts(arrays, gather)

    def run(*refs):
        ins, srcs = refs[:n_in], refs[n_in:n_in + na]
        res, dsts = refs[n_in + na:n_in + na + n_out], refs[n_in + na + n_out:n_in + 2 * na + n_out]
        scr, xsems = refs[n_in + 2 * na + n_out:n_in + 2 * na + n_out + n_scr], refs[n_in + 2 * na + n_out + n_scr:]
        first = pl.program_id(0) == 0
        last = pl.program_id(0) == grid[0] - 1
        for d in range(1, len(grid)):
            first = jnp.logical_and(first, pl.program_id(d) == 0)
            last = jnp.logical_and(last, pl.program_id(d) == grid[d] - 1)

        @pl.when(first)
        def _():
            _exchange_start(_exchange_copies(pick(srcs), dsts, *xsems))

        body(*ins, *res, *scr)

        @pl.when(last)
        def _():
            _exchange_wait(_exchange_copies(pick(srcs), dsts, *xsems))

    hbm = pl.BlockSpec(memory_space=pl.ANY)
    call = pl.pallas_call(
        run, name=name, grid=grid, in_specs=list(in_specs) + [hbm] * na, out_specs=ospecs + [hbm] * na,
        out_shape=outs + xouts, scratch_shapes=list(scratch) + sems, compiler_params=_params(len(grid)))
    return lambda *args: call(*args, *arrays)


def _dot(a, b, ca, cb, prec=None):
    return lax.dot_general(a, b, (((ca,), (cb,)), ((), ())), preferred_element_type=F32, precision=prec)


def _sds(shape, dtype=F32):
    return jax.ShapeDtypeStruct(shape, dtype)


def _row_tile(s, want=256):
    return want if s % want == 0 else s


def _sigmoid(x):
    return 1.0 / (1.0 + jnp.exp(-x))


def _softplus(x):
    return jnp.maximum(x, 0.0) + jnp.log(1.0 + jnp.exp(-jnp.abs(x)))


def _mm_spec(a, b, name, grid, a_spec, b_spec, o_spec, out, ca, cb, acc_shape, drop=(0, 0, 0), res=None, r_spec=None,
             norm_w=None, hosted=None):
    nk = grid[2]
    da, db, do_ = drop
    has_res = res is not None
    has_norm = norm_w is not None

    def body(*refs):
        refs = list(refs)
        a_ref, b_ref = refs[:2]
        r_ref = refs[2] if has_res else None
        w_ref = refs[2 + has_res] if has_norm else None
        o_ref = refs[2 + has_res + has_norm]
        h_ref = refs[3 + has_res + has_norm] if has_norm else None
        acc_ref = refs[-1]
        k = pl.program_id(2)

        @pl.when(k == 0)
        def _():
            acc_ref[...] = jnp.zeros_like(acc_ref)

        av = a_ref[(0,) * da] if da else a_ref[...]
        bv = b_ref[(0,) * db] if db else b_ref[...]
        acc_ref[...] += _dot(av.astype(BF16), bv.astype(BF16), ca, cb)

        @pl.when(k == nk - 1)
        def _():
            val = acc_ref[...]
            if has_res:
                val = val + r_ref[...]
            if do_:
                o_ref[(0,) * do_] = val.astype(out.dtype)
            else:
                o_ref[...] = val.astype(out.dtype)
            if has_norm:
                r = lax.rsqrt(jnp.mean(val * val, axis=-1, keepdims=True) + RMS_EPS)
                h_ref[...] = ((val * r) * w_ref[...]).astype(BF16)

    in_specs = [a_spec, b_spec] + ([r_spec] if has_res else [])
    args = (a, b) + ((res,) if has_res else ())
    out_specs, outs = o_spec, out
    if has_norm:
        assert acc_shape[1] == norm_w.shape[1] == out.shape[-1]
        in_specs.append(pl.BlockSpec((1, acc_shape[1]), lambda i, j, k: (0, 0)))
        args += (norm_w,)
        out_specs, outs = [o_spec, o_spec], [out, _sds(out.shape, BF16)]
    return _pc(body, name, grid, in_specs, out_specs, outs, [pltpu.VMEM(acc_shape, F32)], hosted=hosted)(*args)


def _mm(a, b, mode, name, out_dtype=F32, res=None, norm_w=None, hosted=None):
    if mode == "tn":
        r, m = a.shape
        n = b.shape[1]
        tm, tn, tk = _pick_tile(m), _pick_tile(n), _pick_tile(r)
        grid = (m // tm, n // tn, r // tk)
        a_spec = pl.BlockSpec((tk, tm), lambda i, j, k: (k, i))
        b_spec = pl.BlockSpec((tk, tn), lambda i, j, k: (k, j))
        ca, cb = 0, 0
    else:
        m, kd = a.shape
        n = b.shape[1] if mode == "nn" else b.shape[0]
        tm, tn, tk = _pick_tile(m), _pick_tile(n), _pick_tile(kd)
        grid = (m // tm, n // tn, kd // tk)
        a_spec = pl.BlockSpec((tm, tk), lambda i, j, k: (i, k))
        if mode == "nn":
            b_spec = pl.BlockSpec((tk, tn), lambda i, j, k: (k, j))
            ca, cb = 1, 0
        else:
            b_spec = pl.BlockSpec((tn, tk), lambda i, j, k: (j, k))
            ca, cb = 1, 1
    o_spec = pl.BlockSpec((tm, tn), lambda i, j, k: (i, j))
    return _mm_spec(a, b, name, grid, a_spec, b_spec, o_spec, _sds((m, n), out_dtype), ca, cb, (tm, tn), res=res, r_spec=o_spec,
                    norm_w=norm_w, hosted=hosted)


def _rms_fwd(x, w, name):
    s, d = x.shape
    ts = _row_tile(s)

    def body(x_ref, w_ref, o_ref):
        xv = x_ref[...]
        r = lax.rsqrt(jnp.mean(xv * xv, axis=-1, keepdims=True) + RMS_EPS)
        o_ref[...] = ((xv * r) * w_ref[...]).astype(BF16)

    row = pl.BlockSpec((ts, d), lambda i: (i, 0))
    return _pc(body, name, (s // ts,), [row, pl.BlockSpec((1, d), lambda i: (0, 0))], row, _sds((s, d), BF16))(x, w)


def _mm_dnorm(a, b, name, nk, a_spec, b_spec, ca, cb, drop, x, w, dres, hosted=None):
    s, d = x.shape
    tm = _pick_tile(s)
    da, db = drop

    def body(a_ref, b_ref, x_ref, w_ref, r_ref, dx_ref, dw_ref, acc_ref):
        i = pl.program_id(0)
        k = pl.program_id(2)

        @pl.when(k == 0)
        def _():
            acc_ref[...] = jnp.zeros_like(acc_ref)

        av = a_ref[(0,) * da] if da else a_ref[...]
        bv = b_ref[(0,) * db] if db else b_ref[...]
        acc_ref[...] += _dot(av.astype(BF16), bv.astype(BF16), ca, cb)

        @pl.when(k == nk - 1)
        def _():
            dhv = acc_ref[...]
            xv = x_ref[...]
            r = lax.rsqrt(jnp.mean(xv * xv, axis=-1, keepdims=True) + RMS_EPS)
            xhat = xv * r
            g = dhv * w_ref[...]
            dx_ref[...] = r_ref[...] + r * (g - xhat * jnp.mean(g * xhat, axis=-1, keepdims=True))
            part = jnp.sum(dhv * xhat, axis=0, keepdims=True)

            @pl.when(i == 0)
            def _():
                dw_ref[...] = part

            @pl.when(i > 0)
            def _():
                dw_ref[...] += part

    row = pl.BlockSpec((tm, d), lambda i, j, k: (i, 0))
    vec = pl.BlockSpec((1, d), lambda i, j, k: (0, 0))
    return list(_pc(body, name, (s // tm, 1, nk), [a_spec, b_spec, row, vec, row], [row, vec], [_sds((s, d)), _sds((1, d))],
                    [pltpu.VMEM((tm, d), F32)], hosted=hosted)(a, b, x, w, dres))


def _mm_dnorm_nt(dproj, w_in, name, x, w, dres, hosted=None):
    tm = _pick_tile(x.shape[0])
    tk = _pick_tile(dproj.shape[1])
    return _mm_dnorm(dproj, w_in, name, dproj.shape[1] // tk, pl.BlockSpec((tm, tk), lambda i, j, k: (i, k)),
                     pl.BlockSpec((D_MODEL, tk), lambda i, j, k: (0, k)), 1, 1, (0, 0), x, w, dres, hosted=hosted)


def _ffn_gate_up(h, w_gu, name, hosted=None):
    s = h.shape[0]
    tm = _pick_tile(s)

    def body(h_ref, wg_ref, wu_ref, gu_ref, a_ref):
        hv = h_ref[...]
        g = _dot(hv, wg_ref[0], 1, 0)
        u = _dot(hv, wu_ref[0], 1, 0)
        gu_ref[0, 0] = g.astype(BF16)
        gu_ref[0, 1] = u.astype(BF16)
        a_ref[0] = (g * _sigmoid(g) * u).astype(BF16)

    wblk = lambda off: pl.BlockSpec((1, D_MODEL, FF_BLOCK), lambda i, k: (k + off, 0, 0))
    return _pc(body, name, (s // tm, 4), [pl.BlockSpec((tm, D_MODEL), lambda i, k: (i, 0)), wblk(0), wblk(4)],
               [pl.BlockSpec((1, 2, tm, FF_BLOCK), lambda i, k: (k, 0, i, 0)), pl.BlockSpec((1, tm, FF_BLOCK), lambda i, k: (k, i, 0))],
               [_sds((4, 2, s, FF_BLOCK), BF16), _sds((4, s, FF_BLOCK), BF16)], hosted=hosted)(h, w_gu, w_gu)


def _stage_gather(stages, stage):
    return (stages[stage][0], True) if stages and stage in stages else None


def _stage_arrived(stages, stage, got, default=None):
    late = stages[stage][1](got) if stages and stage in stages else None
    return default if late is None else late


def _stage_slabs(hosted_fn, stage, **new):
    arrays = hosted_fn(stage, **new) if hosted_fn is not None else None
    return None if arrays is None else (arrays, False)


def _ffn_dgate_up(dy, w_down, gu, name, hosted=None):
    s = dy.shape[0]
    tm = _pick_tile(s)

    def body(dy_ref, w_ref, gu_ref, o_ref):
        dav = _dot(dy_ref[...].astype(BF16), w_ref[0], 1, 1)
        g = gu_ref[0, 0].astype(F32)
        u = gu_ref[0, 1].astype(F32)
        sg = _sigmoid(g)
        o_ref[0, 0] = (dav * u * (sg * (1.0 + g * (1.0 - sg)))).astype(BF16)
        o_ref[0, 1] = (dav * (g * sg)).astype(BF16)

    pair = pl.BlockSpec((1, 2, tm, FF_BLOCK), lambda i, k: (k, 0, i, 0))
    return _pc(body, name, (s // tm, 4),
               [pl.BlockSpec((tm, D_MODEL), lambda i, k: (i, 0)), pl.BlockSpec((1, FF_BLOCK, D_MODEL), lambda i, k: (k, 0, 0)), pair],
               pair, _sds((4, 2, s, FF_BLOCK), BF16), hosted=hosted)(dy, w_down, gu)


def _ffn_fwd(x, h, w_gu, w_down, tag, next_norm=None, stages=None):
    s = x.shape[0]
    tm = _pick_tile(s)
    gu, a, *got = _ffn_gate_up(h, w_gu, f"ffn_gu_{tag}", _stage_gather(stages, "ffn_gu"))
    w_down = _stage_arrived(stages, "ffn_gu", got, w_down)
    xspec = pl.BlockSpec((tm, D_MODEL), lambda i, j, k: (i, 0))
    hosted = _stage_gather(stages, "ffn_down")
    y = _mm_spec(a, w_down, f"ffn_down_{tag}", (s // tm, 1, 4),
                 pl.BlockSpec((1, tm, FF_BLOCK), lambda i, j, k: (k, i, 0)),
                 pl.BlockSpec((1, FF_BLOCK, D_MODEL), lambda i, j, k: (k, 0, 0)),
                 xspec, _sds((s, D_MODEL)), 1, 0, (tm, D_MODEL), drop=(1, 1, 0), res=x, r_spec=xspec, norm_w=next_norm,
                 hosted=hosted)
    n_own = 2 if next_norm is not None else 1
    own = list(y[:n_own]) if (hosted is not None or next_norm is not None) else [y]
    if hosted is not None:
        _stage_arrived(stages, "ffn_down", list(y[n_own:]))
    y, h_next = own if next_norm is not None else (own[0], None)
    return y, h_next, (x, h, gu, a)


def _ffn_bwd(dy, saved, norm_w, w_gu, w_down, tag, hosted_fn=None):
    x, h, gu, a = saved
    s = x.shape[0]
    tm = _pick_tile(s)
    got = {}
    hosted = _stage_slabs(hosted_fn, "ffn_gdown")
    g_down = _mm_spec(a, dy, f"ffn_gdown_{tag}", (4, 1, s // tm),
                      pl.BlockSpec((1, tm, FF_BLOCK), lambda i, j, k: (i, k, 0)),
                      pl.BlockSpec((tm, D_MODEL), lambda i, j, k: (k, 0)),
                      pl.BlockSpec((1, FF_BLOCK, D_MODEL), lambda i, j, k: (i, 0, 0)),
                      _sds((4, FF_BLOCK, D_MODEL), BF16), 0, 0, (FF_BLOCK, D_MODEL), drop=(1, 0, 1), hosted=hosted)
    if hosted is not None:
        g_down, *got["ffn_gdown"] = g_down
    hosted = _stage_slabs(hosted_fn, "ffn_dgu", g_down=g_down)
    dgu = _ffn_dgate_up(dy, w_down, gu, f"ffn_dgu_{tag}", hosted)
    if hosted is not None:
        dgu, *got["ffn_dgu"] = dgu
    g_gu = _mm_spec(h, dgu, f"ffn_ggu_{tag}", (NDEV, 1, s // tm),
                    pl.BlockSpec((tm, D_MODEL), lambda i, j, k: (k, 0)),
                    pl.BlockSpec((1, 1, tm, FF_BLOCK), lambda i, j, k: (i % 4, i // 4, k, 0)),
                    pl.BlockSpec((1, D_MODEL, FF_BLOCK), lambda i, j, k: (i, 0, 0)),
                    _sds((NDEV, D_MODEL, FF_BLOCK), BF16), 0, 0, (D_MODEL, FF_BLOCK), drop=(0, 2, 1))
    hosted = _stage_slabs(hosted_fn, "ffn_dh", g_gu=g_gu)
    dx, g_norm, *arrived = _mm_dnorm(dgu, w_gu, f"ffn_dh_{tag}", NDEV,
                                     pl.BlockSpec((1, 1, tm, FF_BLOCK), lambda i, j, k: (k % 4, k // 4, i, 0)),
                                     pl.BlockSpec((1, D_MODEL, FF_BLOCK), lambda i, j, k: (k, 0, 0)), 1, 1, (2, 1), x, norm_w, dy,
                                     hosted=hosted)
    if hosted is not None:
        got["ffn_dh"] = arrived
    return dx, g_norm, g_gu, g_down, got


def _prev_rows(cur, halo, j, first):
    rid = lax.broadcasted_iota(jnp.int32, cur.shape, 0)
    hid = lax.broadcasted_iota(jnp.int32, halo.shape, 0)
    out = pltpu.roll(cur, j, 0)
    for t in range(j):
        row = jnp.sum(jnp.where(hid == 8 - j + t, halo, 0.0), axis=0, keepdims=True)
        row = jnp.where(first, 0.0, row)
        out = jnp.where(rid == t, row, out)
    return out


def _next_rows(cur, halo, j, last):
    ts = cur.shape[0]
    rid = lax.broadcasted_iota(jnp.int32, cur.shape, 0)
    hid = lax.broadcasted_iota(jnp.int32, halo.shape, 0)
    out = pltpu.roll(cur, ts - j, 0)
    for t in range(j):
        row = jnp.sum(jnp.where(hid == t, halo, 0.0), axis=0, keepdims=True)
        row = jnp.where(last, 0.0, row)
        out = jnp.where(rid == ts - j + t, row, out)
    return out


def _halo_specs(ts, s, width, col):
    per = ts // 8
    nblk = s // 8
    prev = pl.BlockSpec((8, width), lambda i: (jnp.maximum(i * per - 1, 0), col))
    nxt = pl.BlockSpec((8, width), lambda i: (jnp.minimum((i + 1) * per, nblk - 1), col))
    return prev, nxt


def _cgate_fwd(p, w_dw, name, hosted=None):
    s = p.shape[0]
    d = D_MODEL
    ts = _row_tile(s)
    prev, _ = _halo_specs(ts, s, 3 * d, 0)

    def body(p_ref, h_ref, w_ref, z_ref):
        first = pl.program_id(0) == 0
        b = p_ref[:, :d]
        cv = p_ref[:, d:2 * d] * p_ref[:, 2 * d:]
        hcv = h_ref[:, d:2 * d] * h_ref[:, 2 * d:]
        u = w_ref[2:3, :] * cv + w_ref[1:2, :] * _prev_rows(cv, hcv, 1, first) + w_ref[0:1, :] * _prev_rows(cv, hcv, 2, first)
        z_ref[...] = (b * u).astype(BF16)

    return _pc(body, name, (s // ts,),
               [pl.BlockSpec((ts, 3 * d), lambda i: (i, 0)), prev, pl.BlockSpec((3, d), lambda i: (0, 0))],
               pl.BlockSpec((ts, d), lambda i: (i, 0)), _sds((s, d), BF16), hosted=hosted)(p, p, w_dw)


def _cgate_bwd(p, dz, w_dw, name):
    s = p.shape[0]
    d = D_MODEL
    ts = _row_tile(s)
    nt = s // ts
    p_prev, p_next = _halo_specs(ts, s, 3 * d, 0)
    _, dz_next = _halo_specs(ts, s, d, 0)

    def body(p_ref, hp_ref, hn_ref, dz_ref, dzn_ref, w_ref, dp_ref, dw_ref):
        i = pl.program_id(0)
        first = i == 0
        last = i == nt - 1
        b = p_ref[:, :d]
        c = p_ref[:, d:2 * d]
        v = p_ref[:, 2 * d:]
        cv = c * v
        hcv = hp_ref[:, d:2 * d] * hp_ref[:, 2 * d:]
        cv1 = _prev_rows(cv, hcv, 1, first)
        cv2 = _prev_rows(cv, hcv, 2, first)
        w0, w1, w2 = w_ref[0:1, :], w_ref[1:2, :], w_ref[2:3, :]
        u = w2 * cv + w1 * cv1 + w0 * cv2
        dzv = dz_ref[...]
        du = dzv * b
        dun = dzn_ref[...] * hn_ref[:, :d]
        dcv = w2 * du + w1 * _next_rows(du, dun, 1, last) + w0 * _next_rows(du, dun, 2, last)
        dp_ref[:, :d] = (dzv * u).astype(BF16)
        dp_ref[:, d:2 * d] = (dcv * v).astype(BF16)
        dp_ref[:, 2 * d:] = (dcv * c).astype(BF16)

        @pl.when(first)
        def _():
            dw_ref[...] = jnp.zeros_like(dw_ref)

        dw_ref[0:1, :] += jnp.sum(du * cv2, axis=0, keepdims=True)
        dw_ref[1:2, :] += jnp.sum(du * cv1, axis=0, keepdims=True)
        dw_ref[2:3, :] += jnp.sum(du * cv, axis=0, keepdims=True)

    wide = pl.BlockSpec((ts, 3 * d), lambda i: (i, 0))
    wspec = pl.BlockSpec((3, d), lambda i: (0, 0))
    return _pc(body, name, (nt,),
               [wide, p_prev, p_next, pl.BlockSpec((ts, d), lambda i: (i, 0)), dz_next, wspec],
               [wide, wspec], [_sds((s, 3 * d), BF16), _sds((3, d))])(p, p, p, dz, dz, w_dw)


def _conv_fwd(x, h, w_in, w_dw, w_out, tag, next_norm, stages=None):
    wn = _cols_from_blocks(w_in)
    hosted = _stage_gather(stages, "conv_in")
    p = _mm(h, wn, "nn", f"conv_in_{tag}", hosted=hosted)
    if hosted is not None:
        p, *got = p
        _stage_arrived(stages, "conv_in", got)
    hosted = _stage_gather(stages, "conv_gate")
    z = _cgate_fwd(p, w_dw, f"conv_gate_{tag}", hosted)
    if hosted is not None:
        z, *got = z
        w_out = _stage_arrived(stages, "conv_gate", got, w_out)
    y, h_next = _mm(z, w_out, "nn", f"conv_out_{tag}", res=x, norm_w=next_norm)
    return y, h_next, (x, h, p, z, wn)


def _conv_bwd(dy, saved, norm_w, w_in, w_dw, w_out, tag, hosted_fn=None):
    x, h, p, z, wn = saved
    dz = _mm(dy, w_out, "nt", f"conv_dz_{tag}")
    g_out = _mm(z, dy, "tn", f"conv_gout_{tag}", out_dtype=BF16)
    dp, g_dw = _cgate_bwd(p, dz, w_dw, f"conv_dgate_{tag}")
    g_in = _blocks_from_cols(_mm(h, dp, "tn", f"conv_gin_{tag}", out_dtype=BF16))
    hosted = _stage_slabs(hosted_fn, "conv_dh", g_in=g_in, g_out=g_out)
    dx, g_norm, *got = _mm_dnorm_nt(dp, wn, f"conv_dh_{tag}", x, norm_w, dy, hosted=hosted)
    return dx, g_norm, g_in, g_dw, g_out, ({"conv_dh": got} if hosted is not None else {})


def _tri(lower):
    r = lax.broadcasted_iota(jnp.int32, (LANES, LANES), 0)
    c = lax.broadcasted_iota(jnp.int32, (LANES, LANES), 1)
    return jnp.where((r >= c) if lower else (r <= c), 1.0, 0.0).astype(F32)


def _cumsum_rows(v, reverse, name):
    s = v.shape[0]
    n = s // LANES
    idx = (lambda i: (n - 1 - i, 0)) if reverse else (lambda i: (i, 0))

    def body(v_ref, o_ref, carry_ref):
        @pl.when(pl.program_id(0) == 0)
        def _():
            carry_ref[...] = jnp.zeros_like(carry_ref)

        blk = v_ref[...]
        o_ref[...] = _dot(_tri(not reverse), blk, 1, 0, HI) + carry_ref[0:1, :]
        carry_ref[...] += jnp.sum(blk, axis=0, keepdims=True)

    spec = pl.BlockSpec((LANES, LANES), idx)
    return _pc(body, name, (n,), [spec], spec, _sds((s, LANES)), [pltpu.VMEM((8, LANES), F32)])(v)


def _lo_mask(shape):
    return lax.broadcasted_iota(jnp.int32, shape, len(shape) - 1) < HEAD_DIM


def _half_sums(v, lo):
    sa = jnp.sum(jnp.where(lo, v, 0.0), axis=-1, keepdims=True)
    sb = jnp.sum(jnp.where(lo, 0.0, v), axis=-1, keepdims=True)
    return jnp.where(lo, sa, sb)


def _fox_prep_fwd(proj, gq, gk, name):
    s = proj.shape[0]
    ts = _row_tile(s)
    qscale = HEAD_DIM ** -0.5 * LOG2E

    def body(q_ref, k_ref, v_ref, gq_ref, gk_ref, qo_ref, ko_ref, vo_ref):
        lo = _lo_mask((ts, LANES))

        def hnorm(xv, g):
            ms = _half_sums(xv * xv, lo) * (1.0 / HEAD_DIM)
            return (xv * lax.rsqrt(ms + RMS_EPS)) * g

        for p in range(8):
            cols = slice(p * LANES, (p + 1) * LANES)
            qo_ref[:, cols] = (hnorm(q_ref[:, cols], gq_ref[...]) * qscale).astype(BF16)
            ko_ref[:, cols] = hnorm(k_ref[:, cols], gk_ref[...]).astype(BF16)
        vo_ref[...] = v_ref[...].astype(BF16)

    def wide(blk):
        return pl.BlockSpec((ts, D_MODEL), lambda i: (i, blk))

    gspec = pl.BlockSpec((1, LANES), lambda i: (0, 0))
    out = _sds((s, D_MODEL), BF16)
    return _pc(body, name, (s // ts,), [wide(0), wide(1), wide(2), gspec, gspec], [wide(0)] * 3, [out] * 3)(
        proj, proj, proj, gq, gk)


def _fox_logf(proj, bf, name):
    s = proj.shape[0]
    ts = _row_tile(s, 512)

    def body(f_ref, b_ref, o_ref):
        z = f_ref[...] + b_ref[...]
        lf = jnp.minimum(z, 0.0) - jnp.log(1.0 + jnp.exp(-jnp.abs(z)))
        real = lax.broadcasted_iota(jnp.int32, (ts, LANES), 1) < ATTN_HEADS
        o_ref[...] = jnp.where(real, lf, 0.0)

    return _pc(body, name, (s // ts,), [pl.BlockSpec((ts, LANES), lambda i: (i, 24)), pl.BlockSpec((1, LANES), lambda i: (0, 0))],
               pl.BlockSpec((ts, LANES), lambda i: (i, 0)), _sds((s, LANES)))(proj, bf)


def _fox_dlogf(proj, bf, dlf, name):
    s = proj.shape[0]
    ts = _row_tile(s, 512)

    def body(f_ref, b_ref, d_ref, o_ref, db_ref):
        z = f_ref[...] + b_ref[...]
        real = lax.broadcasted_iota(jnp.int32, (ts, LANES), 1) < ATTN_HEADS
        g = jnp.where(real, d_ref[...] * _sigmoid(-z), 0.0)
        o_ref[...] = g.astype(BF16)

        @pl.when(pl.program_id(0) == 0)
        def _():
            db_ref[...] = jnp.zeros_like(db_ref)

        db_ref[...] += jnp.sum(g, axis=0, keepdims=True)

    vec = pl.BlockSpec((1, LANES), lambda i: (0, 0))
    row = pl.BlockSpec((ts, LANES), lambda i: (i, 0))
    return _pc(body, name, (s // ts,), [pl.BlockSpec((ts, LANES), lambda i: (i, 24)), vec, row], [row, vec],
               [_sds((s, LANES), BF16), _sds((1, LANES))])(proj, bf, dlf)


def _decay_terms(cum):
    s = cum.shape[0]
    c2 = cum * LOG2E
    hi = lax.reduce_precision(c2, 8, 7)
    mid = lax.reduce_precision(c2 - hi, 8, 7)
    low = lax.reduce_precision(c2 - hi - mid, 8, 7)
    one = jnp.ones_like(hi)

    def place(terms):
        tt = jnp.stack(terms, axis=-1).astype(BF16).reshape(s, 8, 2, 6)
        z = jnp.zeros((s, 8, HEAD_DIM - 6), BF16)
        return jnp.concatenate([tt[:, :, 1], z, tt[:, :, 0], z], axis=-1).reshape(s, D_MODEL)

    return place([hi, mid, low, one, one, one]), place([one, one, one, -hi, -mid, -low])


def _attn_tiles(s):
    t = s
    for cand in (ATTN_TILE, ATTN_TILE // 2):
        if s % cand == 0:
            t = cand
            break
    return t, s // t


def _tri_steps(n, by_key):
    if by_key:
        pairs = [(q, k) for k in range(n) for q in range(k, n)]
    else:
        pairs = [(q, k) for q in range(n) for k in range(q + 1)]
    arr = np.asarray(pairs, np.int32)
    return jnp.asarray(arr[:, 0]), jnp.asarray(arr[:, 1])


def _attn_call(body, name, s, by_key, inputs, in_kinds, out_kinds, out_shapes, scratch, hosted=None, vmem=VMEM_LIMIT_BYTES):
    t, n = _attn_tiles(s)
    qi_arr, ki_arr = _tri_steps(n, by_key)
    nsteps = int(qi_arr.shape[0])
    specs = {
        "q": pl.BlockSpec((t, LANES), lambda p, i, qi, ki: (qi[i], p)),
        "k": pl.BlockSpec((t, LANES), lambda p, i, qi, ki: (ki[i], p)),
        "r": pl.BlockSpec((1, 2, t), lambda p, i, qi, ki: (p, 0, qi[i])),
        "m": pl.BlockSpec((1, t, t), lambda p, i, qi, ki: (jnp.where(qi[i] == ki[i], 1, 0), 0, 0)),
        "Q": pl.BlockSpec((1, LANES, s), lambda p, i, qi, ki: (p, 0, 0)),
        "R": pl.BlockSpec((1, 2, s), lambda p, i, qi, ki: (p, 0, 0)),
    }
    in_specs = [specs[c] for c in in_kinds]
    out_specs = [specs[c] for c in out_kinds]
    out_shapes, scratch, inputs = list(out_shapes), list(scratch), list(inputs)
    run = body
    if hosted is not None:
        arrays, gather = hosted
        na, n_in, n_out, n_scr = len(arrays), len(inputs), len(out_kinds), len(scratch)
        pick, xouts, sems = _exchange_parts(arrays, gather)

        def run(qi_ref, ki_ref, *refs):
            ins, srcs = refs[:n_in], refs[n_in:n_in + na]
            outs, dsts = refs[n_in + na:n_in + na + n_out], refs[n_in + na + n_out:n_in + 2 * na + n_out]
            scr, xsems = refs[n_in + 2 * na + n_out:n_in + 2 * na + n_out + n_scr], refs[n_in + 2 * na + n_out + n_scr:]
            p = pl.program_id(0)
            i = pl.program_id(1)

            @pl.when(jnp.logical_and(p == 0, i == 0))
            def _():
                _exchange_start(_exchange_copies(pick(srcs), dsts, *xsems))

            body(qi_ref, ki_ref, *ins, *outs, *scr)

            @pl.when(jnp.logical_and(p == 7, i == nsteps - 1))
            def _():
                _exchange_wait(_exchange_copies(pick(srcs), dsts, *xsems))

        hbm = pl.BlockSpec(memory_space=pl.ANY)
        in_specs += [hbm] * na
        out_specs += [hbm] * na
        out_shapes += xouts
        scratch += sems
        inputs += list(arrays)
    grid_spec = pltpu.PrefetchScalarGridSpec(
        num_scalar_prefetch=2, grid=(8, nsteps), in_specs=in_specs, out_specs=out_specs, scratch_shapes=scratch)
    params = pltpu.CompilerParams(dimension_semantics=("arbitrary", "arbitrary"), vmem_limit_bytes=vmem)
    return pl.pallas_call(run, name=name, grid_spec=grid_spec, out_shape=out_shapes, compiler_params=params)(
        qi_arr, ki_arr, *inputs)


def _biased_kq(q2, k2, aq, ak, lo):
    sa = _dot(jnp.where(lo, k2, ak), jnp.where(lo, q2, aq), 1, 1)
    sb = _dot(jnp.where(lo, ak, k2), jnp.where(lo, aq, q2), 1, 1)
    return sa, sb


def _causal_bias(s):
    t, _ = _attn_tiles(s)
    kid = lax.broadcasted_iota(jnp.int32, (t, t), 0)
    qid = lax.broadcasted_iota(jnp.int32, (t, t), 1)
    return jnp.stack([jnp.zeros((t, t), BF16), jnp.where(kid > qid, -jnp.inf, 0.0).astype(BF16)])


def _fold8(v, op):
    return op(v.reshape(v.shape[0] // 8, 8, v.shape[1]), axis=0)


def _chunk(ref, mask_ref, hd, r):
    rows = slice(r * ATTN_ROWS, (r + 1) * ATTN_ROWS)
    return rows, ref[hd, rows, :] + mask_ref[0, rows, :].astype(F32)


def _flash_fwd(qs, kn, vb, augq, augk, cmask, name, hosted=None):
    s = qs.shape[0]
    t, n = _attn_tiles(s)
    nch = t // ATTN_ROWS

    def body(qi_ref, ki_ref, q_ref, k_ref, v_ref, aq_ref, ak_ref, mk_ref, o_ref, lse_ref, s_ref, p_ref, m_ref, l_ref, acc_ref):
        i = pl.program_id(1)
        qi = qi_ref[i]
        ki = ki_ref[i]

        @pl.when(ki == 0)
        def _():
            m_ref[...] = jnp.full_like(m_ref, -jnp.inf)
            l_ref[...] = jnp.zeros_like(l_ref)
            acc_ref[...] = jnp.zeros_like(acc_ref)

        lo = _lo_mask((t, LANES))
        rowlo = lax.broadcasted_iota(jnp.int32, (LANES, t), 0) < HEAD_DIM
        v2 = v_ref[...]
        sa, sb = _biased_kq(q_ref[...], k_ref[...], aq_ref[...], ak_ref[...], lo)
        s_ref[0] = sa
        s_ref[1] = sb
        alphas, pvs = [], []
        for hd in range(2):
            mx = jnp.full((8, t), -jnp.inf, F32)
            for r in range(nch):
                _, sc = _chunk(s_ref, mk_ref, hd, r)
                mx = jnp.maximum(mx, _fold8(sc, jnp.max))
            m_prev = m_ref[hd:hd + 1, :]
            m_new = jnp.maximum(m_prev, jnp.max(mx, axis=0, keepdims=True))
            ls = jnp.zeros((8, t), F32)
            for r in range(nch):
                rows, sc = _chunk(s_ref, mk_ref, hd, r)
                pm = jnp.exp2(sc - m_new)
                ls = ls + _fold8(pm, jnp.sum)
                p_ref[hd, rows, :] = pm.astype(BF16)
            alpha = jnp.exp2(m_prev - m_new)
            l_ref[hd:hd + 1, :] = alpha * l_ref[hd:hd + 1, :] + jnp.sum(ls, axis=0, keepdims=True)
            m_ref[hd:hd + 1, :] = m_new
            alphas.append(alpha)
            pvs.append(_dot(v2, p_ref[hd], 0, 0))
        acc_ref[...] = jnp.where(rowlo, alphas[0], alphas[1]) * acc_ref[...] + jnp.where(rowlo, pvs[0], pvs[1])

        @pl.when(ki == qi)
        def _():
            o_ref[...] = (acc_ref[...] / jnp.where(rowlo, l_ref[0:1, :], l_ref[1:2, :])).T
            lse_ref[0] = m_ref[0:2, :] + jnp.log2(l_ref[0:2, :])

    stat = pltpu.VMEM((8, t), F32)
    return _attn_call(body, name, s, False, (qs, kn, vb, augq, augk, cmask), "qkkqkm", "qr",
                      [_sds((s, D_MODEL)), _sds((8, 2, s))],
                      [pltpu.VMEM((2, t, t), F32), pltpu.VMEM((2, t, t), BF16), stat, stat, pltpu.VMEM((LANES, t), F32)],
                      hosted=hosted)


def _fox_delta(do, o, name):
    s = do.shape[0]
    ts = _row_tile(s)

    def body(do_ref, o_ref, d_ref):
        lo = _lo_mask((ts, LANES))
        for p in range(8):
            cols = slice(p * LANES, (p + 1) * LANES)
            d_ref[:, cols] = _half_sums(do_ref[:, cols] * o_ref[:, cols], lo)

    spec = pl.BlockSpec((ts, D_MODEL), lambda i: (i, 0))
    return _pc(body, name, (s // ts,), [spec, spec], spec, _sds((s, D_MODEL)))(do, o)


def _bwd_tile(q_ref, k_ref, v_ref, aq_ref, ak_ref, do_ref, s_ref, dp_ref, lo):
    do2 = do_ref[...].astype(BF16)
    zero = jnp.zeros_like(do2)
    v2 = v_ref[...]
    sa, sb = _biased_kq(q_ref[...], k_ref[...], aq_ref[...], ak_ref[...], lo)
    s_ref[0] = sa
    s_ref[1] = sb
    dp_ref[0] = _dot(v2, jnp.where(lo, do2, zero), 1, 1)
    dp_ref[1] = _dot(v2, jnp.where(lo, zero, do2), 1, 1)
    return do2


def _bwd_chunk(s_ref, dp_ref, mk_ref, lse_ref, dl_ref, hd, r):
    rows, sc = _chunk(s_ref, mk_ref, hd, r)
    pm = jnp.exp2(sc - lse_ref[0, hd:hd + 1, :])
    ds = pm * (dp_ref[hd, rows, :] - dl_ref[0, hd:hd + 1, :])
    return rows, pm, ds


def _flash_bwd(qs, kn, vb, augq, augk, cmask, do, lse, delta, name, hosted=None):
    s = qs.shape[0]
    t, n = _attn_tiles(s)
    nch = t // ATTN_ROWS

    def body(qi_ref, ki_ref, q_ref, k_ref, v_ref, aq_ref, ak_ref, mk_ref, do_ref, lse_ref, dl_ref,
             dk_ref, dv_ref, dc_ref, dq_ref, dcq_ref, s_ref, dp_ref, p_ref, ds_ref, dka_ref, dva_ref, dca_ref):
        i = pl.program_id(1)
        qi = qi_ref[i]
        ki = ki_ref[i]

        @pl.when(i == 0)
        def _():
            dq_ref[...] = jnp.zeros_like(dq_ref)
            dcq_ref[...] = jnp.zeros_like(dcq_ref)

        @pl.when(qi == ki)
        def _():
            dka_ref[...] = jnp.zeros_like(dka_ref)
            dva_ref[...] = jnp.zeros_like(dva_ref)
            dca_ref[...] = jnp.zeros_like(dca_ref)

        lo = _lo_mask((t, LANES))
        rowlo = lax.broadcasted_iota(jnp.int32, (LANES, t), 0) < HEAD_DIM
        do2 = _bwd_tile(q_ref, k_ref, v_ref, aq_ref, ak_ref, do_ref, s_ref, dp_ref, lo)
        q2 = q_ref[...]
        k2 = k_ref[...]
        qcols = pl.ds(pl.multiple_of(qi * t, t), t)
        dvs, dks, dqs = [], [], []
        for hd in range(2):
            rs = jnp.zeros((8, t), F32)
            for r in range(nch):
                rows, pm, ds = _bwd_chunk(s_ref, dp_ref, mk_ref, lse_ref, dl_ref, hd, r)
                rs = rs + _fold8(ds, jnp.sum)
                part = ds[:, 0:LANES]
                for c in range(1, t // LANES):
                    part = part + ds[:, c * LANES:(c + 1) * LANES]
                dca_ref[hd, rows, :] += part
                p_ref[hd, rows, :] = pm.astype(BF16)
                ds_ref[hd, rows, :] = ds.astype(BF16)
            dcq_ref[0, hd:hd + 1, qcols] += jnp.sum(rs, axis=0, keepdims=True)
            dvs.append(_dot(p_ref[hd], do2, 1, 0))
            dks.append(_dot(ds_ref[hd], q2, 1, 0))
            dqs.append(_dot(k2, ds_ref[hd], 0, 0))
        dva_ref[...] += jnp.where(lo, dvs[0], dvs[1])
        dka_ref[...] += jnp.where(lo, dks[0], dks[1])
        dq_ref[0, :, qcols] += jnp.where(rowlo, dqs[0], dqs[1])

        @pl.when(qi == n - 1)
        def _():
            dk_ref[...] = dka_ref[...] * LN2
            dv_ref[...] = dva_ref[...]
            dc_ref[...] = -jnp.where(lo, jnp.sum(dca_ref[0], axis=-1, keepdims=True), jnp.sum(dca_ref[1], axis=-1, keepdims=True))

    out = _sds((s, D_MODEL))
    return _attn_call(body, name, s, True, (qs, kn, vb, augq, augk, cmask, do, lse, delta), "qkkqkmqrr", "kkkQR",
                      [out, out, out, _sds((8, LANES, s)), _sds((8, 2, s))],
                      [pltpu.VMEM((2, t, t), F32), pltpu.VMEM((2, t, t), F32), pltpu.VMEM((2, t, t), BF16),
                       pltpu.VMEM((2, t, t), BF16), pltpu.VMEM((t, LANES), F32), pltpu.VMEM((t, LANES), F32),
                       pltpu.VMEM((2, t, LANES), F32)], hosted=hosted, vmem=ATTN_BWD_VMEM_BYTES)


def _fox_prep_bwd(proj, dqs, dk, dv, gq, gk, name):
    s = proj.shape[0]
    ts = _row_tile(s)
    scale = HEAD_DIM ** -0.5

    def body(q_ref, k_ref, dq_ref, dk_ref, dv_ref, gq_ref, gk_ref, oq_ref, ok_ref, ov_ref, dgq_ref, dgk_ref):
        lo = _lo_mask((ts, LANES))

        @pl.when(pl.program_id(0) == 0)
        def _():
            dgq_ref[...] = jnp.zeros_like(dgq_ref)
            dgk_ref[...] = jnp.zeros_like(dgk_ref)

        def back(xv, dout, g):
            r = lax.rsqrt(_half_sums(xv * xv, lo) * (1.0 / HEAD_DIM) + RMS_EPS)
            y = xv * r
            dy = dout * g
            dx = r * (dy - y * (_half_sums(dy * y, lo) * (1.0 / HEAD_DIM)))
            return dx, jnp.sum(dout * y, axis=0, keepdims=True)

        for p in range(8):
            cols = slice(p * LANES, (p + 1) * LANES)
            dxq, dgq = back(q_ref[:, cols], dq_ref[p].T * scale, gq_ref[...])
            dxk, dgk = back(k_ref[:, cols], dk_ref[:, cols], gk_ref[...])
            oq_ref[:, cols] = dxq.astype(BF16)
            ok_ref[:, cols] = dxk.astype(BF16)
            dgq_ref[...] += dgq
            dgk_ref[...] += dgk
        ov_ref[...] = dv_ref[...].astype(BF16)

    def wide(blk):
        return pl.BlockSpec((ts, D_MODEL), lambda i: (i, blk))

    gspec = pl.BlockSpec((1, LANES), lambda i: (0, 0))
    out = _sds((s, D_MODEL), BF16)
    dqt = pl.BlockSpec((8, LANES, ts), lambda i: (0, 0, i))
    return _pc(body, name, (s // ts,), [wide(0), wide(1), dqt, wide(0), wide(0), gspec, gspec],
               [wide(0)] * 3 + [gspec] * 2, [out] * 3 + [_sds((1, LANES))] * 2)(proj, proj, dqs, dk, dv, gq, gk)


def _fox_fwd(x, h, w_in, b_f, q_gain, k_gain, w_out, next_norm, hosted=None):
    proj = _mm(h, w_in, "nn", "fox_in")
    gq = jnp.tile(q_gain, (1, 2))
    gk = jnp.tile(k_gain, (1, 2))
    bf = jnp.pad(b_f, ((0, 0), (0, LANES - ATTN_HEADS)))
    qs, kn, vb = _fox_prep_fwd(proj, gq, gk, "fox_prep")
    cum = _cumsum_rows(_fox_logf(proj, bf, "fox_logf"), False, "fox_cum")[:, :ATTN_HEADS]
    augq, augk = _decay_terms(cum)
    cmask = _causal_bias(x.shape[0])
    o, lse, *got = _flash_fwd(qs, kn, vb, augq, augk, cmask, "fox_attn", hosted=hosted)
    y, h_next = _mm(o, w_out, "nn", "fox_out", res=x, norm_w=next_norm)
    return y, h_next, (x, h, proj, gq, gk, bf, qs, kn, vb, augq, augk, cmask, o, lse), got


def _fox_bwd(dy, saved, norm_w, w_in, w_out, hosted=None):
    x, h, proj, gq, gk, bf, qs, kn, vb, augq, augk, cmask, o, lse = saved
    s = x.shape[0]
    do = _mm(dy, w_out, "nt", "fox_do")
    g_out = _mm(o, dy, "tn", "fox_gout", out_dtype=BF16)
    delta = _fox_delta(do, o, "fox_delta")[:, ::HEAD_DIM].T.reshape(8, 2, s)
    dk, dv, dck, dqs, dcq, *got = _flash_bwd(qs, kn, vb, augq, augk, cmask, do, lse, delta, "fox_dattn", hosted=hosted)
    dcum = jnp.pad(dcq.reshape(ATTN_HEADS, s).T + dck[:, ::HEAD_DIM], ((0, 0), (0, LANES - ATTN_HEADS)))
    dlf = _cumsum_rows(dcum, True, "fox_dcum")
    dfl, g_bf = _fox_dlogf(proj, bf, dlf, "fox_dlogf")
    dq_o, dk_o, dv_o, g_gq, g_gk = _fox_prep_bwd(proj, dqs, dk, dv, gq, gk, "fox_dprep")
    dproj = jnp.concatenate([dq_o, dk_o, dv_o, dfl], axis=1)
    g_in = _mm(h, dproj, "tn", "fox_gin", out_dtype=BF16)
    dx, g_norm = _mm_dnorm_nt(dproj, w_in, "fox_dh", x, norm_w, dy)
    g_q = g_gq[:, :HEAD_DIM] + g_gq[:, HEAD_DIM:]
    g_k = g_gk[:, :HEAD_DIM] + g_gk[:, HEAD_DIM:]
    return dx, g_norm, g_in[:, :FOX_IN], g_bf[:, :ATTN_HEADS], g_q, g_k, g_out, got


def _ssd_conv_fwd(proj, cw, cb, name):
    s = proj.shape[0]
    ts = _row_tile(s)
    w = 1024
    per = ts // 8

    def body(p_ref, h_ref, w_ref, b_ref, o_ref):
        first = pl.program_id(0) == 0
        cur = p_ref[...]
        halo = h_ref[...]
        u = w_ref[3:4, :] * cur + b_ref[...]
        for j in range(1, 4):
            u = u + w_ref[3 - j:4 - j, :] * _prev_rows(cur, halo, j, first)
        o_ref[...] = u * _sigmoid(u)

    return _pc(body, name, (s // ts, 4),
               [pl.BlockSpec((ts, w), lambda i, j: (i, 2 + j)),
                pl.BlockSpec((8, w), lambda i, j: (jnp.maximum(i * per - 1, 0), 2 + j)),
                pl.BlockSpec((4, w), lambda i, j: (0, j)), pl.BlockSpec((1, w), lambda i, j: (0, j))],
               pl.BlockSpec((ts, w), lambda i, j: (i, j)), _sds((s, SSM_CONV_DIM)))(proj, proj, cw, cb)


def _ssd_conv_bwd(proj, d, first_col, cw, cb, name):
    s = proj.shape[0]
    ts = _row_tile(s)
    nt = s // ts
    w = 1024
    ncol = d.shape[1] // w
    per = ts // 8
    nblk = s // 8

    def body(p_ref, hp_ref, hn_ref, d_ref, dn_ref, w_ref, b_ref, o_ref, dw_ref, db_ref):
        i = pl.program_id(1)
        first = i == 0
        last = i == nt - 1
        cur = p_ref[...]
        prev = [cur] + [_prev_rows(cur, hp_ref[...], j, first) for j in range(1, 4)]
        nxt = hn_ref[...]
        tail = cur[ts - 8:, :]
        u = b_ref[...]
        un = b_ref[...]
        for j in range(4):
            u = u + w_ref[3 - j:4 - j, :] * prev[j]
            un = un + w_ref[3 - j:4 - j, :] * (nxt if j == 0 else _prev_rows(nxt, tail, j, False))
        sg = _sigmoid(u)
        g = d_ref[...] * (sg * (1.0 + u * (1.0 - sg)))
        sn = _sigmoid(un)
        gn = dn_ref[...] * (sn * (1.0 + un * (1.0 - sn)))

        @pl.when(first)
        def _():
            dw_ref[...] = jnp.zeros_like(dw_ref)
            db_ref[...] = jnp.zeros_like(db_ref)

        dpre = w_ref[3:4, :] * g
        for j in range(1, 4):
            dpre = dpre + w_ref[3 - j:4 - j, :] * _next_rows(g, gn, j, last)
        for j in range(4):
            dw_ref[3 - j:4 - j, :] += jnp.sum(g * prev[j], axis=0, keepdims=True)
        db_ref[...] += jnp.sum(g, axis=0, keepdims=True)
        o_ref[...] = dpre.astype(BF16)

    tile = pl.BlockSpec((ts, w), lambda j, i: (i, j))
    wspec = lambda off: pl.BlockSpec((4, w), lambda j, i: (0, off + j))
    vec = lambda off: pl.BlockSpec((1, w), lambda j, i: (0, off + j))
    nxt_blk = lambda off: pl.BlockSpec((8, w), lambda j, i: (jnp.minimum((i + 1) * per, nblk - 1), off + j))
    in_proj = 2 + first_col
    return _pc(body, name, (ncol, nt),
               [pl.BlockSpec((ts, w), lambda j, i: (i, in_proj + j)),
                pl.BlockSpec((8, w), lambda j, i: (jnp.maximum(i * per - 1, 0), in_proj + j)), nxt_blk(in_proj),
                tile, nxt_blk(0), wspec(first_col), vec(first_col)],
               [tile, wspec(0), vec(0)], [_sds((s, ncol * w), BF16), _sds((4, ncol * w)), _sds((1, ncol * w))])(
                   proj, proj, proj, d, d, cw, cb)


def _ssd_dt_fwd(proj, bias, a_neg, name):
    s = proj.shape[0]
    n = s // SSM_CHUNK

    def body(r_ref, b_ref, a_ref, dt_ref, ac_ref):
        real = lax.broadcasted_iota(jnp.int32, (SSM_CHUNK, LANES), 1) < SSM_HEADS
        dt = jnp.where(real, _softplus(r_ref[...] + b_ref[...]), 0.0)
        dt_ref[...] = dt
        ac_ref[...] = _dot(_tri(True), dt * a_ref[...], 1, 0, HI)

    vec = pl.BlockSpec((1, LANES), lambda c: (0, 0))
    row = pl.BlockSpec((SSM_CHUNK, LANES), lambda c: (c, 0))
    return _pc(body, name, (n,), [pl.BlockSpec((SSM_CHUNK, LANES), lambda c: (c, 48)), vec, vec], [row, row],
               [_sds((s, LANES)), _sds((s, LANES))])(proj, bias, a_neg)


def _ssd_dt_bwd(proj, bias, ddt, name):
    s = proj.shape[0]
    ts = _row_tile(s, 512)

    def body(r_ref, b_ref, d_ref, o_ref, db_ref):
        real = lax.broadcasted_iota(jnp.int32, (ts, LANES), 1) < SSM_HEADS
        g = jnp.where(real, d_ref[...] * _sigmoid(r_ref[...] + b_ref[...]), 0.0)
        o_ref[...] = g.astype(BF16)

        @pl.when(pl.program_id(0) == 0)
        def _():
            db_ref[...] = jnp.zeros_like(db_ref)

        db_ref[...] += jnp.sum(g, axis=0, keepdims=True)

    vec = pl.BlockSpec((1, LANES), lambda i: (0, 0))
    row = pl.BlockSpec((ts, LANES), lambda i: (i, 0))
    return _pc(body, name, (s // ts,), [pl.BlockSpec((ts, LANES), lambda i: (i, 48)), vec, row], [row, vec],
               [_sds((s, LANES), BF16), _sds((1, LANES))])(proj, bias, ddt)


def _pair_cols(cols, k0, lo):
    return jnp.where(lo, cols[:, k0:k0 + 1], cols[:, k0 + 1:k0 + 2])


def _last_lane(row):
    lane = lax.broadcasted_iota(jnp.int32, row.shape, 1)
    return jnp.sum(jnp.where(lane == SSM_CHUNK - 1, row, 0.0), axis=-1, keepdims=True)


SSD_FWD_GROUPS = 2
SSD_BWD_GROUPS = 1


def _ssd_specs(nc, rev, n):
    cc = (lambda c: nc - 1 - c) if rev else (lambda c: c)
    nb = SSM_INNER // (LANES * n)
    return dict(
        x=pl.BlockSpec((SSM_CHUNK, 256 * n), lambda g, c: (cc(c), g)),
        b=pl.BlockSpec((SSM_CHUNK, LANES * n), lambda g, c: (cc(c), nb + g)),
        c=pl.BlockSpec((SSM_CHUNK, LANES * n), lambda g, c: (cc(c), nb + SSM_GROUPS // n + g)),
        col=pl.BlockSpec((n, SSM_CHUNK, 4), lambda g, c: (g, cc(c), 0)),
        row=pl.BlockSpec((n, 4, SSM_CHUNK), lambda g, c: (g, 0, cc(c))),
        grp=pl.BlockSpec((n, 1, 256), lambda g, c: (g, 0, 0)),
        grow=pl.BlockSpec((n, 4, LANES), lambda g, c: (g, 0, 0)),
        hs=pl.BlockSpec((1, n, 256, SSM_STATE), lambda g, c: (cc(c), g, 0, 0)),
        bc=pl.BlockSpec((SSM_CHUNK, LANES * n), lambda g, c: (cc(c), g)),
    )


def _ssd_scan_fwd(xbc, dtc, acol, drow, arow, dskip, name):
    s = xbc.shape[0]
    nc = s // SSM_CHUNK
    n = SSD_FWD_GROUPS
    sp = _ssd_specs(nc, False, n)
    L = SSM_CHUNK

    def body(x_ref, b_ref, c_ref, dtc_ref, ac_ref, dr_ref, ar_ref, dk_ref, y_ref, hs_ref, h_ref):
        @pl.when(pl.program_id(1) == 0)
        def _():
            h_ref[...] = jnp.zeros_like(h_ref)

        for gi in range(n):
            group(gi, x_ref, b_ref, c_ref, dtc_ref, ac_ref, dr_ref, ar_ref, dk_ref, y_ref, hs_ref, h_ref)

    def group(gi, x_ref, b_ref, c_ref, dtc_ref, ac_ref, dr_ref, ar_ref, dk_ref, y_ref, hs_ref, h_ref):
        x0 = gi * 256
        bb = b_ref[:, gi * LANES:(gi + 1) * LANES].astype(BF16)
        cb = c_ref[:, gi * LANES:(gi + 1) * LANES].astype(BF16)
        gm = _dot(cb, bb, 1, 1)
        dtc = dtc_ref[gi]
        ac = ac_ref[gi]
        dr = dr_ref[gi]
        ar = ar_ref[gi]
        dsk = dk_ref[gi]
        hs_ref[0, gi] = h_ref[gi]
        tril = lax.broadcasted_iota(jnp.int32, (L, L), 0) >= lax.broadcasted_iota(jnp.int32, (L, L), 1)
        lo = _lo_mask((L, LANES))
        rowlo = lax.broadcasted_iota(jnp.int32, (L, LANES), 0) < HEAD_DIM
        for pr in range(2):
            k0 = 2 * pr
            xp = x_ref[:, x0 + pr * LANES:x0 + (pr + 1) * LANES]
            xpb = xp.astype(BF16)
            hp = h_ref[gi, pr * LANES:(pr + 1) * LANES, :]
            yd, al = [], []
            for k in (k0, k0 + 1):
                seg = ac[:, k:k + 1] - ar[k:k + 1, :]
                wk = gm * jnp.exp(jnp.where(tril, seg, -jnp.inf)) * dr[k:k + 1, :]
                yd.append(_dot(wk.astype(BF16), xpb, 1, 0))
                al.append(_last_lane(ar[k:k + 1, :]))
            e = jnp.exp(_pair_cols(ac, k0, lo))
            yo = _dot(cb, hp.astype(BF16), 1, 1) * e
            y_ref[:, x0 + pr * LANES:x0 + (pr + 1) * LANES] = (
                jnp.where(lo, yd[0], yd[1]) + yo + dsk[:, pr * LANES:(pr + 1) * LANES] * xp)
            wp = jnp.where(lo, jnp.exp(al[0] - ac[:, k0:k0 + 1]) * dtc[:, k0:k0 + 1],
                           jnp.exp(al[1] - ac[:, k0 + 1:k0 + 2]) * dtc[:, k0 + 1:k0 + 2])
            st = _dot((xp * wp).astype(BF16), bb, 0, 0)
            dec = jnp.where(rowlo, jnp.exp(al[0]), jnp.exp(al[1]))
            h_ref[gi, pr * LANES:(pr + 1) * LANES, :] = dec * hp + st

    return _pc(body, name, (SSM_GROUPS // n, nc),
               [sp["x"], sp["b"], sp["c"], sp["col"], sp["col"], sp["row"], sp["row"], sp["grp"]],
               [sp["x"], sp["hs"]], [_sds((s, SSM_INNER)), _sds((nc, SSM_GROUPS, 256, SSM_STATE))],
               [pltpu.VMEM((n, 256, SSM_STATE), F32)])(xbc, xbc, xbc, dtc, acol, drow, arow, dskip)


def _ssd_scan_bwd(xbc, dtc, acol, drow, arow, dskip, agrp, hs, dy, name):
    s = xbc.shape[0]
    nc = s // SSM_CHUNK
    n = SSD_BWD_GROUPS
    sp = _ssd_specs(nc, True, n)
    L = SSM_CHUNK

    def body(x_ref, b_ref, c_ref, dtc_ref, ac_ref, dr_ref, ar_ref, dk_ref, ag_ref, hs_ref, dy_ref,
             dx_ref, db_ref, dc_ref, ddt_ref, da_ref, dd_ref, dh_ref):
        @pl.when(pl.program_id(1) == 0)
        def _():
            dh_ref[...] = jnp.zeros_like(dh_ref)
            da_ref[...] = jnp.zeros_like(da_ref)
            dd_ref[...] = jnp.zeros_like(dd_ref)

        for gi in range(n):
            group(gi, x_ref, b_ref, c_ref, dtc_ref, ac_ref, dr_ref, ar_ref, dk_ref, ag_ref, hs_ref, dy_ref,
                  dx_ref, db_ref, dc_ref, ddt_ref, da_ref, dd_ref, dh_ref)

    def group(gi, x_ref, b_ref, c_ref, dtc_ref, ac_ref, dr_ref, ar_ref, dk_ref, ag_ref, hs_ref, dy_ref,
              dx_ref, db_ref, dc_ref, ddt_ref, da_ref, dd_ref, dh_ref):
        x0 = gi * 256
        bcols = slice(gi * LANES, (gi + 1) * LANES)
        bb = b_ref[:, bcols].astype(BF16)
        cb = c_ref[:, bcols].astype(BF16)
        gm = _dot(cb, bb, 1, 1)
        dtc = dtc_ref[gi]
        ac = ac_ref[gi]
        dr = dr_ref[gi]
        ar = ar_ref[gi]
        dsk = dk_ref[gi]
        ag = ag_ref[gi]
        tril = lax.broadcasted_iota(jnp.int32, (L, L), 0) >= lax.broadcasted_iota(jnp.int32, (L, L), 1)
        lo = _lo_mask((L, LANES))
        nlo = jnp.logical_not(lo)
        rowlo = lax.broadcasted_iota(jnp.int32, (L, LANES), 0) < HEAD_DIM
        lane = lax.broadcasted_iota(jnp.int32, (L, LANES), 1)
        lane_row = lax.broadcasted_iota(jnp.int32, (1, LANES), 1)
        dgm = jnp.zeros((L, L), F32)
        dcm = jnp.zeros((L, SSM_STATE), F32)
        dbm = jnp.zeros((L, SSM_STATE), F32)
        cols = jnp.zeros((L, LANES), F32)
        rows_ddt, rows_q, al_all, dcd_all = [], [], [], []
        for pr in range(2):
            k0 = 2 * pr
            xcols = slice(x0 + pr * LANES, x0 + (pr + 1) * LANES)
            xp = x_ref[:, xcols]
            xpb = xp.astype(BF16)
            dyp = dy_ref[:, xcols]
            dypb = dyp.astype(BF16)
            zero = jnp.zeros_like(dypb)
            hp = hs_ref[0, gi, pr * LANES:(pr + 1) * LANES, :]
            hpb = hp.astype(BF16)
            dst = dh_ref[gi, pr * LANES:(pr + 1) * LANES, :]
            dstb = dst.astype(BF16)
            dxd, al = [], []
            for k in (k0, k0 + 1):
                sel = lo if k == k0 else nlo
                seg = ac[:, k:k + 1] - ar[k:k + 1, :]
                lam = jnp.exp(jnp.where(tril, seg, -jnp.inf))
                wk = gm * lam * dr[k:k + 1, :]
                dwk = _dot(jnp.where(sel, dypb, zero), xpb, 1, 1)
                mk = dwk * gm * lam
                qk = mk * dr[k:k + 1, :]
                dgm = dgm + dwk * lam * dr[k:k + 1, :]
                rows_ddt.append(jnp.sum(mk, axis=0, keepdims=True))
                rows_q.append(jnp.sum(qk, axis=0, keepdims=True))
                cols = jnp.where(lane == k, jnp.sum(qk, axis=-1, keepdims=True), cols)
                dxd.append(_dot(wk.astype(BF16), dypb, 0, 0))
                al.append(_last_lane(ar[k:k + 1, :]))
            al_all += al
            dxp = jnp.where(lo, dxd[0], dxd[1])
            e = jnp.exp(_pair_cols(ac, k0, lo))
            dye = dyp * e
            dyeb = dye.astype(BF16)
            dcm = dcm + _dot(dyeb, hpb, 1, 0)
            dh_yoff = _dot(dyeb, cb, 0, 0)
            tq = dye * _dot(cb, hpb, 1, 1)
            cols = jnp.where(lane == 4 + k0, jnp.sum(jnp.where(lo, tq, 0.0), axis=-1, keepdims=True), cols)
            cols = jnp.where(lane == 5 + k0, jnp.sum(jnp.where(lo, 0.0, tq), axis=-1, keepdims=True), cols)
            wp = jnp.where(lo, jnp.exp(al[0] - ac[:, k0:k0 + 1]) * dtc[:, k0:k0 + 1],
                           jnp.exp(al[1] - ac[:, k0 + 1:k0 + 2]) * dtc[:, k0 + 1:k0 + 2])
            dxw = _dot(bb, dstb, 1, 1)
            dxp = dxp + dxw * wp
            tw = xp * dxw
            cols = jnp.where(lane == 8 + k0, jnp.sum(jnp.where(lo, tw, 0.0), axis=-1, keepdims=True), cols)
            cols = jnp.where(lane == 9 + k0, jnp.sum(jnp.where(lo, 0.0, tw), axis=-1, keepdims=True), cols)
            dbm = dbm + _dot((xp * wp).astype(BF16), dstb, 1, 0)
            dsl = dsk[:, pr * LANES:(pr + 1) * LANES]
            dx_ref[:, xcols] = dxp + dsl * dyp
            dd_ref[gi, :, pr * LANES:(pr + 1) * LANES] += jnp.sum(dyp * xp, axis=0, keepdims=True)
            prod = dst * hp
            dcd_all.append(jnp.sum(jnp.sum(jnp.where(rowlo, prod, 0.0), axis=-1, keepdims=True), axis=0, keepdims=True))
            dcd_all.append(jnp.sum(jnp.sum(jnp.where(rowlo, 0.0, prod), axis=-1, keepdims=True), axis=0, keepdims=True))
            dec = jnp.where(rowlo, jnp.exp(al[0]), jnp.exp(al[1]))
            dh_ref[gi, pr * LANES:(pr + 1) * LANES, :] = dec * dst + dh_yoff
        dgb = dgm.astype(BF16)
        dc_ref[:, bcols] = dcm + _dot(dgb, bb, 1, 0)
        db_ref[:, bcols] = dbm + _dot(dgb, cb, 0, 0)
        colt = cols.T
        sub8 = lax.broadcasted_iota(jnp.int32, (8, LANES), 0)
        da_rows = jnp.zeros((8, LANES), F32)
        ddt_part = []
        for k in range(4):
            rs = colt[k:k + 1, :]
            uo = colt[4 + k:5 + k, :]
            dwl = colt[8 + k:9 + k, :]
            es = jnp.exp(al_all[k] - ar[k:k + 1, :])
            wrow = es * dr[k:k + 1, :]
            dwl_w = dwl * wrow
            da_k = rs - rows_q[k] + uo - dwl_w
            tail = jnp.sum(dwl_w, axis=-1, keepdims=True) + jnp.exp(al_all[k]) * dcd_all[k]
            da_k = da_k + jnp.where(lane_row == L - 1, tail, 0.0)
            da_rows = jnp.where(sub8 == k, da_k, da_rows)
            ddt_part.append(rows_ddt[k] + dwl * es)
        dda = _dot(da_rows, _tri(True), 1, 0, HI)
        for k in range(4):
            dda_k = dda[k:k + 1, :]
            ddt_ref[gi, k:k + 1, :] = ddt_part[k] + dda_k * ag[k:k + 1, :]
            da_ref[gi, k:k + 1, :] += dda_k * dr[k:k + 1, :] * ag[k:k + 1, :]

    return _pc(body, name, (SSM_GROUPS // n, nc),
               [sp["x"], sp["b"], sp["c"], sp["col"], sp["col"], sp["row"], sp["row"], sp["grp"], sp["grow"], sp["hs"], sp["x"]],
               [sp["x"], sp["bc"], sp["bc"], sp["row"], sp["grow"], sp["grp"]],
               [_sds((s, SSM_INNER)), _sds((s, 1024)), _sds((s, 1024)), _sds((SSM_GROUPS, 4, s)),
                _sds((SSM_GROUPS, 4, LANES)), _sds((SSM_GROUPS, 1, 256))],
               [pltpu.VMEM((n, 256, SSM_STATE), F32)])(xbc, xbc, xbc, dtc, acol, drow, arow, dskip, agrp, hs, dy)


def _gnorm_fwd(y, proj, nw, name):
    s = y.shape[0]
    ts = _row_tile(s)
    gw = SSM_INNER // SSM_GROUPS

    def body(y_ref, z_ref, w_ref, o_ref):
        for g in range(SSM_GROUPS):
            sl = slice(g * gw, (g + 1) * gw)
            z = z_ref[:, sl]
            y2 = y_ref[:, sl] * (z * _sigmoid(z))
            r = lax.rsqrt(jnp.mean(y2 * y2, axis=-1, keepdims=True) + RMS_EPS)
            o_ref[:, sl] = ((y2 * r) * w_ref[:, sl]).astype(BF16)

    row = pl.BlockSpec((ts, SSM_INNER), lambda i: (i, 0))
    return _pc(body, name, (s // ts,), [row, row, pl.BlockSpec((1, SSM_INNER), lambda i: (0, 0))], row,
               _sds((s, SSM_INNER), BF16))(y, proj, nw)


def _gnorm_bwd(y, proj, nw, dyn, name):
    s = y.shape[0]
    ts = _row_tile(s)
    gw = SSM_INNER // SSM_GROUPS

    def body(y_ref, z_ref, w_ref, d_ref, dy_ref, dz_ref, dw_ref):
        @pl.when(pl.program_id(0) == 0)
        def _():
            dw_ref[...] = jnp.zeros_like(dw_ref)

        for g in range(SSM_GROUPS):
            sl = slice(g * gw, (g + 1) * gw)
            z = z_ref[:, sl]
            yv = y_ref[:, sl]
            sg = _sigmoid(z)
            sz = z * sg
            y2 = yv * sz
            r = lax.rsqrt(jnp.mean(y2 * y2, axis=-1, keepdims=True) + RMS_EPS)
            yn = y2 * r
            dout = d_ref[:, sl]
            dyg = dout * w_ref[:, sl]
            dy2 = r * (dyg - yn * jnp.mean(dyg * yn, axis=-1, keepdims=True))
            dy_ref[:, sl] = dy2 * sz
            dz_ref[:, sl] = (dy2 * yv * (sg * (1.0 + z * (1.0 - sg)))).astype(BF16)
            dw_ref[:, sl] += jnp.sum(dout * yn, axis=0, keepdims=True)

    row = pl.BlockSpec((ts, SSM_INNER), lambda i: (i, 0))
    vec = pl.BlockSpec((1, SSM_INNER), lambda i: (0, 0))
    return _pc(body, name, (s // ts,), [row, row, vec, row], [row, row, vec],
               [_sds((s, SSM_INNER)), _sds((s, SSM_INNER), BF16), _sds((1, SSM_INNER))])(y, proj, nw, dyn)


def _head_layouts(v, s):
    return v.reshape(s, SSM_GROUPS, 4).transpose(1, 0, 2), v.T.reshape(SSM_GROUPS, 4, s)


def _ssd_fwd(x, h, w_in, conv_w, conv_b, dt_bias, a_log, d_skip, gnorm_w, w_out, next_norm):
    s = x.shape[0]
    proj = _mm(h, w_in, "nn", "ssd_in")
    xbc = _ssd_conv_fwd(proj, conv_w, conv_b, "ssd_conv")
    pad = ((0, 0), (0, LANES - SSM_HEADS))
    a_neg = -jnp.exp(a_log)
    bias = jnp.pad(dt_bias, pad)
    dt, acum = _ssd_dt_fwd(proj, bias, jnp.pad(a_neg, pad), "ssd_dt")
    dtc, drow = _head_layouts(dt[:, :SSM_HEADS], s)
    acol, arow = _head_layouts(acum[:, :SSM_HEADS], s)
    dskip = jnp.repeat(d_skip.reshape(SSM_GROUPS, 1, 4), HEAD_DIM, axis=2)
    y, hs = _ssd_scan_fwd(xbc, dtc, acol, drow, arow, dskip, "ssd_scan")
    yn = _gnorm_fwd(y, proj, gnorm_w, "ssd_gnorm")
    out, h_next = _mm(yn, w_out, "nn", "ssd_out", res=x, norm_w=next_norm)
    return out, h_next, (x, h, proj, xbc, bias, a_neg, dtc, acol, drow, arow, dskip, y, hs, yn)


def _ssd_bwd(dout, saved, norm_w, w_in, conv_w, conv_b, gnorm_w, w_out):
    x, h, proj, xbc, bias, a_neg, dtc, acol, drow, arow, dskip, y, hs, yn = saved
    s = x.shape[0]
    dyn = _mm(dout, w_out, "nt", "ssd_dyn")
    g_out = _mm(yn, dout, "tn", "ssd_gout", out_dtype=BF16)
    dy, dz, g_gnorm = _gnorm_bwd(y, proj, gnorm_w, dyn, "ssd_dgnorm")
    agrp = jnp.broadcast_to(a_neg.reshape(SSM_GROUPS, 4, 1), (SSM_GROUPS, 4, LANES))
    dxs, db, dc, ddt_row, da_acc, dd_acc = _ssd_scan_bwd(xbc, dtc, acol, drow, arow, dskip, agrp, hs, dy, "ssd_dscan")
    parts = [_ssd_conv_bwd(proj, d, col, conv_w, conv_b, f"ssd_dconv_{tag}") for d, col, tag in ((dxs, 0, "x"), (db, 2, "b"), (dc, 3, "c"))]
    g_cw = jnp.concatenate([p[1] for p in parts], axis=1)
    g_cb = jnp.concatenate([p[2] for p in parts], axis=1)
    ddt = jnp.pad(ddt_row.reshape(SSM_HEADS, s).T, ((0, 0), (0, LANES - SSM_HEADS)))
    ddtraw, g_dtb = _ssd_dt_bwd(proj, bias, ddt, "ssd_ddt")
    dproj = jnp.concatenate([dz] + [p[0] for p in parts] + [ddtraw], axis=1)
    g_in = _mm(h, dproj, "tn", "ssd_gin", out_dtype=BF16)
    dx, g_norm = _mm_dnorm_nt(dproj, w_in, "ssd_dh", x, norm_w, dout)
    g_alog = jnp.sum(da_acc, axis=-1).reshape(1, SSM_HEADS)
    g_d = jnp.sum(dd_acc.reshape(SSM_GROUPS, 4, HEAD_DIM), axis=-1).reshape(1, SSM_HEADS)
    return dx, g_norm, g_in[:, :SSM_IN], g_cw, g_cb, g_dtb[:, :SSM_HEADS], g_alog, g_d, g_gnorm, g_out


def _loss_head(y, target, name):
    s, d = y.shape
    ts = _row_tile(s)

    def body(y_ref, t_ref, dy_ref, l_ref):
        @pl.when(pl.program_id(0) == 0)
        def _():
            l_ref[...] = jnp.zeros_like(l_ref)

        e = y_ref[...] - t_ref[...]
        dy_ref[...] = e * (1.0 / d)
        part = jnp.sum(jnp.sum(e * e, axis=-1, keepdims=True), axis=0, keepdims=True) * (0.5 / d)
        l_ref[...] += jnp.broadcast_to(part, l_ref.shape)

    row = pl.BlockSpec((ts, d), lambda i: (i, 0))
    dy, lacc = _pc(body, name, (s // ts,), [row, row], [row, pl.BlockSpec((8, LANES), lambda i: (0, 0))],
                   [_sds((s, d)), _sds((8, LANES))])(y, target)
    return lacc[0, 0], dy


def _local_step(x, target, w, gather_layer0=None, gather_rest=None, scatter_first=None, scatter_layer0=None):
    saved = []
    received, received_layer0 = None, {}

    def layer0_stages():
        def entry(stage):
            shards, finish = gather_layer0[stage]

            def on_arrival(got):
                nonlocal w
                w = finish(w, got)
                return {"conv_gate": lambda: w["conv_w_out"][0], "ffn_gu": lambda: w["ffn_w_down"][0]}.get(stage, lambda: None)()
            return shards, on_arrival
        return {stage: entry(stage) for stage in gather_layer0}

    at = lambda weights, n: weights[n] if n < len(weights) else None
    h = _rms_fwd(x, w["mix_norm"][0:1], "first_norm")
    for i in range(DEPTH):
        kind, j = i % 3, i // 3
        fn = w["ffn_norm"][i:i + 1]
        stages = layer0_stages() if (i == 0 and gather_layer0 is not None) else None
        if kind == 0:
            x, h, sv = _conv_fwd(x, h, w["conv_w_in"][j], w["conv_w_dw"][j], at(w["conv_w_out"], j), str(i), fn, stages)
        elif kind == 1:
            hosted = None if gather_rest is None else (gather_rest[0], True)
            x, h, sv, got = _fox_fwd(x, h, w["fox_w_in"], w["fox_b_f"], w["fox_q_gain"], w["fox_k_gain"], w["fox_w_out"], fn, hosted)
            if gather_rest is not None:
                w = gather_rest[1](w, got)
        else:
            x, h, sv = _ssd_fwd(x, h, w["ssd_w_in"], w["ssd_conv_w"], w["ssd_conv_b"], w["ssd_dt_bias"],
                                w["ssd_a_log"], w["ssd_d"], w["ssd_norm_w"], w["ssd_w_out"], fn)
        nxt = w["mix_norm"][i + 1:i + 2] if i + 1 < DEPTH else None
        x, h, sf = _ffn_fwd(x, h, w["ffn_w_gu"][i], at(w["ffn_w_down"], i), str(i), nxt, stages)
        saved.append((sv, sf))
    loss, dx = _loss_head(x, target, "loss_head")
    g = {k: [None] * n for k, n in (("mix_norm", DEPTH), ("ffn_norm", DEPTH), ("ffn_w_gu", DEPTH), ("ffn_w_down", DEPTH),
                                    ("conv_w_in", 2), ("conv_w_dw", 2), ("conv_w_out", 2))}
    for i in reversed(range(DEPTH)):
        kind, j = i % 3, i // 3
        sv, sf = saved[i]
        hosted_fn = None
        if i == 0 and scatter_layer0 is not None:
            hosted_fn = lambda stage, **new: scatter_layer0(stage, g, **new)
        dx, g["ffn_norm"][i], g["ffn_w_gu"][i], g["ffn_w_down"][i], got = _ffn_bwd(
            dx, sf, w["ffn_norm"][i:i + 1], w["ffn_w_gu"][i], w["ffn_w_down"][i], str(i), hosted_fn)
        received_layer0.update(got)
        mn = w["mix_norm"][i:i + 1]
        if kind == 0:
            dx, g["mix_norm"][i], g["conv_w_in"][j], g["conv_w_dw"][j], g["conv_w_out"][j], got = _conv_bwd(
                dx, sv, mn, w["conv_w_in"][j], w["conv_w_dw"][j], w["conv_w_out"][j], str(i), hosted_fn)
            received_layer0.update(got)
        elif kind == 1:
            hosted = None if scatter_first is None else (scatter_first(g), False)
            (dx, g["mix_norm"][i], g["fox_w_in"], g["fox_b_f"], g["fox_q_gain"], g["fox_k_gain"],
             g["fox_w_out"], received) = _fox_bwd(dx, sv, mn, w["fox_w_in"], w["fox_w_out"], hosted)
        else:
            (dx, g["mix_norm"][i], g["ssd_w_in"], g["ssd_conv_w"], g["ssd_conv_b"], g["ssd_dt_bias"], g["ssd_a_log"],
             g["ssd_d"], g["ssd_norm_w"], g["ssd_w_out"]) = _ssd_bwd(
                 dx, sv, mn, w["ssd_w_in"], w["ssd_conv_w"], w["ssd_conv_b"], w["ssd_norm_w"], w["ssd_w_out"])
    g["mix_norm"] = jnp.concatenate(g["mix_norm"], axis=0)
    g["ffn_norm"] = jnp.concatenate(g["ffn_norm"], axis=0)
    g["conv_w_dw"] = jnp.stack(g["conv_w_dw"], axis=0)
    g["ssd_conv_w"] = g["ssd_conv_w"][None]
    return loss, dx, g, received, received_layer0


def _mesh_position():
    return lax.axis_index("x") * 4 + lax.axis_index("y") * 2 + lax.axis_index("c")


def _device_of(t):
    return (lax.shift_right_logical(t, 2), lax.bitwise_and(lax.shift_right_logical(t, 1), 1), lax.bitwise_and(t, 1))


def _exchange_copies(srcs_of, out_refs, send_sems, recv_sems, local_sems):
    me = _mesh_position()
    na = len(out_refs)
    locals_ = [pltpu.make_async_copy(srcs_of(a, me), out_refs[a].at[me], local_sems.at[a]) for a in range(na)]
    sends, arrivals = [], []
    for j in range(1, NDEV):
        t = lax.rem(me + j, NDEV)
        frm = lax.rem(me + NDEV - j, NDEV)
        for a in range(na):
            sends.append(pltpu.make_async_remote_copy(
                src_ref=srcs_of(a, t), dst_ref=out_refs[a].at[me], send_sem=send_sems.at[a, j - 1],
                recv_sem=recv_sems.at[a, j - 1], device_id=_device_of(t), device_id_type=pl.DeviceIdType.MESH))
            arrivals.append(pltpu.make_async_remote_copy(
                src_ref=srcs_of(a, me), dst_ref=out_refs[a].at[frm], send_sem=send_sems.at[a, j - 1],
                recv_sem=recv_sems.at[a, j - 1], device_id=_device_of(frm), device_id_type=pl.DeviceIdType.MESH))
    return locals_, sends, arrivals


def _exchange_start(copies):
    locals_, sends, _ = copies
    for cp in locals_ + sends:
        cp.start()


def _exchange_wait(copies):
    locals_, sends, arrivals = copies
    for cp in sends:
        cp.wait_send()
    for cp in arrivals:
        cp.wait_recv()
    for cp in locals_:
        cp.wait()


def _exchange_run(srcs_of, out_refs, send_sems, recv_sems, local_sems):
    copies = _exchange_copies(srcs_of, out_refs, send_sems, recv_sems, local_sems)
    _exchange_start(copies)
    _exchange_wait(copies)


def _exchange_parts(arrays, gather):
    na = len(arrays)
    outs = [_sds(((NDEV,) + a.shape) if gather else a.shape, a.dtype) for a in arrays]
    sems = [pltpu.SemaphoreType.DMA((na, NDEV - 1)), pltpu.SemaphoreType.DMA((na, NDEV - 1)), pltpu.SemaphoreType.DMA((na,))]
    pick = (lambda srcs: (lambda a, t: srcs[a])) if gather else (lambda srcs: (lambda a, t: srcs[a].at[t]))
    return pick, outs, sems


def _exchange(arrays, name, gather):
    na = len(arrays)
    pick, outs, sems = _exchange_parts(arrays, gather)

    def body(*refs):
        _exchange_run(pick(refs[:na]), refs[na:2 * na], *refs[2 * na:])

    hbm = pl.BlockSpec(memory_space=pl.ANY)
    return pl.pallas_call(body, name=name, in_specs=[hbm] * na, out_specs=[hbm] * na, out_shape=outs, scratch_shapes=sems)(*arrays)


def _all_sum_small(pack, name):
    def body(src_ref, out_ref, buf_ref, send_sems, recv_sems, local_sems):
        _exchange_run(lambda a, t: src_ref, [buf_ref], send_sems, recv_sems, local_sems)
        acc = buf_ref[0]
        for d in range(1, NDEV):
            acc = acc + buf_ref[d]
        out_ref[...] = acc

    vmem = pl.BlockSpec(memory_space=pltpu.VMEM)
    return pl.pallas_call(
        body, name=name, in_specs=[vmem], out_specs=vmem, out_shape=_sds(pack.shape, pack.dtype),
        scratch_shapes=[pltpu.VMEM((NDEV,) + pack.shape, pack.dtype), pltpu.SemaphoreType.DMA((1, NDEV - 1)),
                        pltpu.SemaphoreType.DMA((1, NDEV - 1)), pltpu.SemaphoreType.DMA((1,))])(pack)


def _sum_slabs(slabs, name):
    _, r, c = slabs.shape
    tr = r
    for cand in (256, 352):
        if r % cand == 0:
            tr = cand
            break

    def body(s_ref, o_ref):
        acc = s_ref[0].astype(F32)
        for d in range(1, NDEV):
            acc = acc + s_ref[d].astype(F32)
        o_ref[...] = acc

    return _pc(body, name, (r // tr,), [pl.BlockSpec((NDEV, tr, c), lambda i: (0, i, 0))],
               pl.BlockSpec((tr, c), lambda i: (i, 0)), _sds((r, c)))(slabs)


def _adamw(wt, g, m, v, name):
    shape = wt.shape
    w2, g2, m2, v2 = (a.reshape(-1, shape[-1]) for a in (wt, g, m, v))
    r, c = w2.shape
    tr = r
    for cand in (512, 352, 256):
        if r % cand == 0:
            tr = cand
            break
    c1 = 1.0 - ADAM_B1 ** ADAM_STEP
    c2 = 1.0 - ADAM_B2 ** ADAM_STEP

    def body(w_ref, g_ref, m_ref, v_ref, d_ref, mo_ref, vo_ref):
        gv = g_ref[...]
        mn = ADAM_B1 * m_ref[...] + (1.0 - ADAM_B1) * gv
        vn = ADAM_B2 * v_ref[...] + (1.0 - ADAM_B2) * (gv * gv)
        mo_ref[...] = mn
        vo_ref[...] = vn
        d_ref[...] = -ADAM_LR * ((mn / c1) / (jnp.sqrt(vn / c2) + ADAM_EPS) + ADAM_WD * w_ref[...])

    spec = pl.BlockSpec((tr, c), lambda i: (i, 0))
    outs = _pc(body, name, (r // tr,), [spec] * 4, [spec] * 3, [_sds((r, c))] * 3)(w2, g2, m2, v2)
    return tuple(o.reshape(shape) for o in outs)


_NAMES = ["mix_norm", "ffn_norm", "ffn_w_gu", "ffn_w_down", "conv_w_in", "conv_w_dw", "conv_w_out", "fox_w_in", "fox_b_f",
          "fox_q_gain", "fox_k_gain", "fox_w_out", "ssd_w_in", "ssd_conv_w", "ssd_conv_b", "ssd_dt_bias", "ssd_a_log",
          "ssd_d", "ssd_norm_w", "ssd_w_out"]
_MATRICES = ["ffn_w_gu", "ffn_w_down", "conv_w_in", "conv_w_out", "fox_w_in", "fox_w_out", "ssd_w_in", "ssd_w_out"]
_VECTORS = {"conv_w_dw": 2, "ssd_conv_w": 2, "ssd_conv_b": 1, "ssd_norm_w": 1}
_REPLICATED = ["mix_norm", "ffn_norm", "fox_b_f", "fox_q_gain", "fox_k_gain", "ssd_dt_bias", "ssd_a_log", "ssd_d"]


def _to_rows(flat):
    n = flat.shape[0]
    rows = -(-n // (8 * D_MODEL)) * 8
    return jnp.pad(flat, (0, rows * D_MODEL - n)).reshape(rows, D_MODEL)


def _full_shape(local_shape, axis):
    shp = list(local_shape)
    shp[axis] *= NDEV
    return tuple(shp)


def _cols_from_blocks(g):
    return jnp.moveaxis(g, 0, 1).reshape(g.shape[1], NDEV * g.shape[2])


def _blocks_from_cols(full):
    k, n8 = full.shape
    return jnp.moveaxis(full.reshape(k, NDEV, n8 // NDEV), 1, 0)


def kernel(x, mix_norm, ffn_norm, ffn_w_gu, ffn_w_down, conv_w_in, conv_w_dw, conv_w_out, fox_w_in, fox_b_f, fox_q_gain, fox_k_gain, fox_w_out, ssd_w_in, ssd_conv_w, ssd_conv_b, ssd_dt_bias, ssd_a_log, ssd_d, ssd_norm_w, ssd_w_out, loss_target, m_mix_norm, m_ffn_norm, m_ffn_w_gu, m_ffn_w_down, m_conv_w_in, m_conv_w_dw, m_conv_w_out, m_fox_w_in, m_fox_b_f, m_fox_q_gain, m_fox_k_gain, m_fox_w_out, m_ssd_w_in, m_ssd_conv_w, m_ssd_conv_b, m_ssd_dt_bias, m_ssd_a_log, m_ssd_d, m_ssd_norm_w, m_ssd_w_out, v_mix_norm, v_ffn_norm, v_ffn_w_gu, v_ffn_w_down, v_conv_w_in, v_conv_w_dw, v_conv_w_out, v_fox_w_in, v_fox_b_f, v_fox_q_gain, v_fox_k_gain, v_fox_w_out, v_ssd_w_in, v_ssd_conv_w, v_ssd_conv_b, v_ssd_dt_bias, v_ssd_a_log, v_ssd_d, v_ssd_norm_w, v_ssd_w_out):
    local = dict(mix_norm=mix_norm, ffn_norm=ffn_norm, ffn_w_gu=ffn_w_gu, ffn_w_down=ffn_w_down, conv_w_in=conv_w_in,
                 conv_w_dw=conv_w_dw, conv_w_out=conv_w_out, fox_w_in=fox_w_in, fox_b_f=fox_b_f, fox_q_gain=fox_q_gain,
                 fox_k_gain=fox_k_gain, fox_w_out=fox_w_out, ssd_w_in=ssd_w_in, ssd_conv_w=ssd_conv_w, ssd_conv_b=ssd_conv_b,
                 ssd_dt_bias=ssd_dt_bias, ssd_a_log=ssd_a_log, ssd_d=ssd_d, ssd_norm_w=ssd_norm_w, ssd_w_out=ssd_w_out)
    mom = dict(zip(_NAMES, [m_mix_norm, m_ffn_norm, m_ffn_w_gu, m_ffn_w_down, m_conv_w_in, m_conv_w_dw, m_conv_w_out, m_fox_w_in,
                            m_fox_b_f, m_fox_q_gain, m_fox_k_gain, m_fox_w_out, m_ssd_w_in, m_ssd_conv_w, m_ssd_conv_b,
                            m_ssd_dt_bias, m_ssd_a_log, m_ssd_d, m_ssd_norm_w, m_ssd_w_out]))
    var = dict(zip(_NAMES, [v_mix_norm, v_ffn_norm, v_ffn_w_gu, v_ffn_w_down, v_conv_w_in, v_conv_w_dw, v_conv_w_out, v_fox_w_in,
                            v_fox_b_f, v_fox_q_gain, v_fox_k_gain, v_fox_w_out, v_ssd_w_in, v_ssd_conv_w, v_ssd_conv_b,
                            v_ssd_dt_bias, v_ssd_a_log, v_ssd_d, v_ssd_norm_w, v_ssd_w_out]))

    shard = {k: local[k].astype(BF16) for k in _MATRICES}
    vec_pack = _to_rows(jnp.concatenate([local[k].reshape(-1) for k in _VECTORS]))
    first = _exchange([shard["conv_w_in"][0:1], vec_pack], "gather_first", True)
    gvec = first[1].reshape(NDEV, -1)
    full = {k: local[k] for k in _REPLICATED}
    off = 0
    for k, axis in _VECTORS.items():
        n = local[k].size
        blk = jnp.moveaxis(gvec[:, off:off + n].reshape((NDEV,) + local[k].shape), 0, axis)
        full[k] = blk.reshape(_full_shape(local[k].shape, axis))
        off += n
    full["ssd_conv_w"] = full["ssd_conv_w"][0]
    full["conv_w_in"] = [first[0][:, 0]]
    full["conv_w_out"], full["ffn_w_gu"], full["ffn_w_down"] = [], [], []

    def finish_gu0(w, got):
        return dict(w, ffn_w_gu=[got[0][:, 0]])

    def finish_out0(w, got):
        return dict(w, conv_w_out=[got[0][:, 0].reshape(D_MODEL, D_MODEL)])

    def finish_down0(w, got):
        return dict(w, ffn_w_down=[got[0][:, 0].reshape(4, FF_BLOCK, D_MODEL)], fox_w_out=got[1].reshape(D_MODEL, D_MODEL))

    def finish_fox(w, got):
        return dict(w, fox_w_in=jnp.pad(_cols_from_blocks(got[0][:, 0]), ((0, 0), (0, FOX_IN_PAD - FOX_IN))))

    layer0 = {"conv_in": ([shard["ffn_w_gu"][0:1]], finish_gu0), "conv_gate": ([shard["conv_w_out"][0:1]], finish_out0),
              "ffn_gu": ([shard["ffn_w_down"][0:1], shard["fox_w_out"]], finish_down0),
              "ffn_down": ([shard["fox_w_in"]], finish_fox)}

    rest = [shard["ffn_w_gu"][1:], shard["ffn_w_down"][1:], shard["conv_w_in"][1:], shard["conv_w_out"][1:],
            shard["ssd_w_in"], shard["ssd_w_out"]]

    def finish(w, got):
        w = dict(w)
        w["ffn_w_gu"] = w["ffn_w_gu"] + [got[0][:, i] for i in range(DEPTH - 1)]
        w["ffn_w_down"] = w["ffn_w_down"] + [got[1][:, i].reshape(4, FF_BLOCK, D_MODEL) for i in range(DEPTH - 1)]
        w["conv_w_in"] = w["conv_w_in"] + [got[2][:, 0]]
        w["conv_w_out"] = w["conv_w_out"] + [got[3][:, 0].reshape(D_MODEL, D_MODEL)]
        w["ssd_w_in"] = jnp.pad(_cols_from_blocks(got[4][:, 0]), ((0, 0), (0, SSM_IN_PAD - SSM_IN)))
        w["ssd_w_out"] = got[5].reshape(SSM_INNER, D_MODEL)
        return w

    def early_slabs(g):
        return ([g["ffn_w_gu"][i] for i in range(1, DEPTH)]
                + [g["ffn_w_down"][i].reshape(NDEV, D_FF // NDEV, D_MODEL) for i in range(1, DEPTH)]
                + [g["conv_w_in"][1], g["conv_w_out"][1].reshape(NDEV, D_MODEL // NDEV, D_MODEL),
                   _blocks_from_cols(g["ssd_w_in"]), g["ssd_w_out"].reshape(NDEV, SSM_INNER // NDEV, D_MODEL)])

    def layer0_slabs(stage, g, g_down=None, g_gu=None, g_in=None, g_out=None):
        if stage == "ffn_gdown":
            return [_blocks_from_cols(g["fox_w_in"])]
        if stage == "ffn_dgu":
            return [g_down.reshape(NDEV, D_FF // NDEV, D_MODEL), g["fox_w_out"].reshape(NDEV, D_MODEL // NDEV, D_MODEL)]
        if stage == "ffn_dh":
            return [g_gu]
        if stage == "conv_dh":
            return [g_in, g_out.reshape(NDEV, D_MODEL // NDEV, D_MODEL)]
        return None

    loss_part, dx, grads, early, late = _local_step(x[0], loss_target[0], full, layer0, (rest, finish), early_slabs, layer0_slabs)

    se = [_sum_slabs(r, f"sum_early_{n}") for n, r in enumerate(early)]
    sl = {stage: [_sum_slabs(r, f"sum_{stage}_{n}") for n, r in enumerate(rs)] for stage, rs in late.items()}
    shard_grad = {
        "ffn_w_gu": jnp.stack(sl["ffn_dh"] + se[0:3]), "ffn_w_down": jnp.stack(sl["ffn_dgu"][0:1] + se[3:6]),
        "conv_w_in": jnp.stack([sl["conv_dh"][0], se[6]]), "conv_w_out": jnp.stack([sl["conv_dh"][1], se[7]]),
        "fox_w_in": sl["ffn_gdown"][0][None], "fox_w_out": sl["ffn_dgu"][1][None],
        "ssd_w_in": se[8][None], "ssd_w_out": se[9][None]}

    small_names = _REPLICATED + list(_VECTORS)
    small = [jnp.reshape(loss_part, (1,))] + [grads[k].reshape(-1) for k in small_names]
    total = _all_sum_small(_to_rows(jnp.concatenate(small)), "sum_small").reshape(-1)
    loss = total[0]
    off = 1
    me = _mesh_position()
    for k, part in zip(small_names, small[1:]):
        gk = total[off:off + part.shape[0]]
        off += part.shape[0]
        if k in _VECTORS:
            axis = _VECTORS[k]
            shp = local[k].shape
            gfull = gk.reshape(shp[:axis] + (NDEV, shp[axis]) + shp[axis + 1:])
            shard_grad[k] = lax.dynamic_index_in_dim(gfull, me, axis, keepdims=False)
        else:
            shard_grad[k] = gk.reshape(local[k].shape)

    deltas, new_m, new_v = {}, {}, {}
    for k in _NAMES:
        deltas[k], new_m[k], new_v[k] = _adamw(local[k], shard_grad[k], mom[k], var[k], f"adamw_{k}")
    return (loss, dx[None], *[shard_grad[k] for k in _NAMES], *[deltas[k] for k in _NAMES],
            *[new_m[k] for k in _NAMES], *[new_v[k] for k in _NAMES])
```

```python
import numpy as np

import jax
import jax.numpy as jnp
from jax import lax
from jax.experimental import pallas as pl
from jax.experimental.pallas import tpu as pltpu

F32 = jnp.float32
BF16 = jnp.bfloat16
HI = lax.Precision.HIGHEST

NDEV = 8
D_MODEL = 1024
DEPTH = 4
D_FF = 2816
FF_BLOCK = 2 * D_FF // NDEV
RMS_EPS = 1e-6
HEAD_DIM = 64
ATTN_HEADS = 16
FOX_IN = 3 * D_MODEL + ATTN_HEADS
FOX_IN_PAD = 3200
SSM_INNER = 2048
SSM_HEADS = 32
SSM_GROUPS = 8
SSM_STATE = 128
SSM_CHUNK = 128
SSM_CONV_DIM = 4096
SSM_IN = SSM_INNER + SSM_CONV_DIM + SSM_HEADS
SSM_IN_PAD = 6272
LANES = 128
V7X_VMEM_BYTES = 64 * 1024 * 1024
VMEM_LIMIT_BYTES = (V7X_VMEM_BYTES * 3) // 4
ATTN_BWD_VMEM_BYTES = (V7X_VMEM_BYTES * 7) // 8
LOG2E = 1.4426950408889634
LN2 = 0.6931471805599453
ATTN_TILE = 1024
ATTN_ROWS = 32
CUMSUM_ROWS = 512

ADAM_LR = 0.001
ADAM_B1 = 0.9
ADAM_B2 = 0.999
ADAM_EPS = 1e-08
ADAM_WD = 0.01
ADAM_STEP = 10

_TILE_CANDIDATES = (1024, 1408, 896, 768, 640, 512, 384, 256, 128)


def _pick_tile(n):
    for c in _TILE_CANDIDATES:
        if n % c == 0:
            return c
    raise ValueError(f"no tile for {n}")


def _params(ngrid):
    return pltpu.CompilerParams(dimension_semantics=("arbitrary",) * ngrid, vmem_limit_bytes=VMEM_LIMIT_BYTES)


def _pc(body, name, grid, in_specs, out_specs, out_shape, scratch=(), hosted=None):
    if hosted is None:
        return pl.pallas_call(
            body, name=name, grid=grid, in_specs=in_specs, out_specs=out_specs, out_shape=out_shape,
            scratch_shapes=list(scratch), compiler_params=_params(len(grid)))
    arrays, gather = hosted
    single = not isinstance(out_shape, (list, tuple))
    outs = [out_shape] if single else list(out_shape)
    ospecs = [out_specs] if single else list(out_specs)
    na, n_in, n_out, n_scr = len(arrays), len(in_specs), len(outs), len(scratch)
    pick, xouts, sems = _exchange_parts(arrays, gather)

    def run(*refs):
        ins, srcs = refs[:n_in], refs[n_in:n_in + na]
        res, dsts = refs[n_in + na:n_in + na + n_out], refs[n_in + na + n_out:n_in + 2 * na + n_out]
        scr, xsems = refs[n_in + 2 * na + n_out:n_in + 2 * na + n_out + n_scr], refs[n_in + 2 * na + n_out + n_scr:]
        first = pl.program_id(0) == 0
        last = pl.program_id(0) == grid[0] - 1
        for d in range(1, len(grid)):
            first = jnp.logical_and(first, pl.program_id(d) == 0)
            last = jnp.logical_and(last, pl.program_id(d) == grid[d] - 1)

        @pl.when(first)
        def _():
            _exchange_start(_exchange_copies(pick(srcs), dsts, *xsems))

        body(*ins, *res, *scr)

        @pl.when(last)
        def _():
            _exchange_wait(_exchange_copies(pick(srcs), dsts, *xsems))

    hbm = pl.BlockSpec(memory_space=pl.ANY)
    call = pl.pallas_call(
        run, name=name, grid=grid, in_specs=list(in_specs) + [hbm] * na, out_specs=ospecs + [hbm] * na,
        out_shape=outs + xouts, scratch_shapes=list(scratch) + sems, compiler_params=_params(len(grid)))
    return lambda *args: call(*args, *arrays)


def _dot(a, b, ca, cb, prec=None):
    return lax.dot_general(a, b, (((ca,), (cb,)), ((), ())), preferred_element_type=F32, precision=prec)


def _sds(shape, dtype=F32):
    return jax.ShapeDtypeStruct(shape, dtype)


def _row_tile(s, want=256):
    return want if s % want == 0 else s


def _sigmoid(x):
    return 1.0 / (1.0 + jnp.exp(-x))


def _softplus(x):
    return jnp.maximum(x, 0.0) + jnp.log(1.0 + jnp.exp(-jnp.abs(x)))


def _mm_spec(a, b, name, grid, a_spec, b_spec, o_spec, out, ca, cb, acc_shape, drop=(0, 0, 0), res=None, r_spec=None,
             norm_w=None, hosted=None):
    nk = grid[2]
    da, db, do_ = drop
    has_res = res is not None
    has_norm = norm_w is not None

    def body(*refs):
        refs = list(refs)
        a_ref, b_ref = refs[:2]
        r_ref = refs[2] if has_res else None
        w_ref = refs[2 + has_res] if has_norm else None
        o_ref = refs[2 + has_res + has_norm]
        h_ref = refs[3 + has_res + has_norm] if has_norm else None
        acc_ref = refs[-1]
        k = pl.program_id(2)

        @pl.when(k == 0)
        def _():
            acc_ref[...] = jnp.zeros_like(acc_ref)

        av = a_ref[(0,) * da] if da else a_ref[...]
        bv = b_ref[(0,) * db] if db else b_ref[...]
        acc_ref[...] += _dot(av.astype(BF16), bv.astype(BF16), ca, cb)

        @pl.when(k == nk - 1)
        def _():
            val = acc_ref[...]
            if has_res:
                val = val + r_ref[...]
            if do_:
                o_ref[(0,) * do_] = val.astype(out.dtype)
            else:
                o_ref[...] = val.astype(out.dtype)
            if has_norm:
                r = lax.rsqrt(jnp.mean(val * val, axis=-1, keepdims=True) + RMS_EPS)
                h_ref[...] = ((val * r) * w_ref[...]).astype(BF16)

    in_specs = [a_spec, b_spec] + ([r_spec] if has_res else [])
    args = (a, b) + ((res,) if has_res else ())
    out_specs, outs = o_spec, out
    if has_norm:
        assert acc_shape[1] == norm_w.shape[1] == out.shape[-1]
        in_specs.append(pl.BlockSpec((1, acc_shape[1]), lambda i, j, k: (0, 0)))
        args += (norm_w,)
        out_specs, outs = [o_spec, o_spec], [out, _sds(out.shape, BF16)]
    return _pc(body, name, grid, in_specs, out_specs, outs, [pltpu.VMEM(acc_shape, F32)], hosted=hosted)(*args)


def _mm(a, b, mode, name, out_dtype=F32, res=None, norm_w=None, hosted=None):
    if mode == "tn":
        r, m = a.shape
        n = b.shape[1]
        tm, tn, tk = _pick_tile(m), _pick_tile(n), _pick_tile(r)
        grid = (m // tm, n // tn, r // tk)
        a_spec = pl.BlockSpec((tk, tm), lambda i, j, k: (k, i))
        b_spec = pl.BlockSpec((tk, tn), lambda i, j, k: (k, j))
        ca, cb = 0, 0
    else:
        m, kd = a.shape
        n = b.shape[1] if mode == "nn" else b.shape[0]
        tm, tn, tk = _pick_tile(m), _pick_tile(n), _pick_tile(kd)
        grid = (m // tm, n // tn, kd // tk)
        a_spec = pl.BlockSpec((tm, tk), lambda i, j, k: (i, k))
        if mode == "nn":
            b_spec = pl.BlockSpec((tk, tn), lambda i, j, k: (k, j))
            ca, cb = 1, 0
        else:
            b_spec = pl.BlockSpec((tn, tk), lambda i, j, k: (j, k))
            ca, cb = 1, 1
    o_spec = pl.BlockSpec((tm, tn), lambda i, j, k: (i, j))
    return _mm_spec(a, b, name, grid, a_spec, b_spec, o_spec, _sds((m, n), out_dtype), ca, cb, (tm, tn), res=res, r_spec=o_spec,
                    norm_w=norm_w, hosted=hosted)


def _rms_fwd(x, w, name):
    s, d = x.shape
    ts = _row_tile(s)

    def body(x_ref, w_ref, o_ref):
        xv = x_ref[...]
        r = lax.rsqrt(jnp.mean(xv * xv, axis=-1, keepdims=True) + RMS_EPS)
        o_ref[...] = ((xv * r) * w_ref[...]).astype(BF16)

    row = pl.BlockSpec((ts, d), lambda i: (i, 0))
    return _pc(body, name, (s // ts,), [row, pl.BlockSpec((1, d), lambda i: (0, 0))], row, _sds((s, d), BF16))(x, w)


def _mm_dnorm(a, b, name, nk, a_spec, b_spec, ca, cb, drop, x, w, dres, hosted=None):
    s, d = x.shape
    tm = _pick_tile(s)
    da, db = drop

    def body(a_ref, b_ref, x_ref, w_ref, r_ref, dx_ref, dw_ref, acc_ref):
        i = pl.program_id(0)
        k = pl.program_id(2)

        @pl.when(k == 0)
        def _():
            acc_ref[...] = jnp.zeros_like(acc_ref)

        av = a_ref[(0,) * da] if da else a_ref[...]
        bv = b_ref[(0,) * db] if db else b_ref[...]
        acc_ref[...] += _dot(av.astype(BF16), bv.astype(BF16), ca, cb)

        @pl.when(k == nk - 1)
        def _():
            dhv = acc_ref[...]
            xv = x_ref[...]
            r = lax.rsqrt(jnp.mean(xv * xv, axis=-1, keepdims=True) + RMS_EPS)
            xhat = xv * r
            g = dhv * w_ref[...]
            dx_ref[...] = r_ref[...] + r * (g - xhat * jnp.mean(g * xhat, axis=-1, keepdims=True))
            part = jnp.sum(dhv * xhat, axis=0, keepdims=True)

            @pl.when(i == 0)
            def _():
                dw_ref[...] = part

            @pl.when(i > 0)
            def _():
                dw_ref[...] += part

    row = pl.BlockSpec((tm, d), lambda i, j, k: (i, 0))
    vec = pl.BlockSpec((1, d), lambda i, j, k: (0, 0))
    return list(_pc(body, name, (s // tm, 1, nk), [a_spec, b_spec, row, vec, row], [row, vec], [_sds((s, d)), _sds((1, d))],
                    [pltpu.VMEM((tm, d), F32)], hosted=hosted)(a, b, x, w, dres))


def _mm_dnorm_nt(dproj, w_in, name, x, w, dres, hosted=None):
    tm = _pick_tile(x.shape[0])
    tk = _pick_tile(dproj.shape[1])
    return _mm_dnorm(dproj, w_in, name, dproj.shape[1] // tk, pl.BlockSpec((tm, tk), lambda i, j, k: (i, k)),
                     pl.BlockSpec((D_MODEL, tk), lambda i, j, k: (0, k)), 1, 1, (0, 0), x, w, dres, hosted=hosted)


def _ffn_gate_up(h, w_gu, name, hosted=None):
    s = h.shape[0]
    tm = _pick_tile(s)

    def body(h_ref, wg_ref, wu_ref, gu_ref, a_ref):
        hv = h_ref[...]
        g = _dot(hv, wg_ref[0], 1, 0)
        u = _dot(hv, wu_ref[0], 1, 0)
        gu_ref[0, 0] = g.astype(BF16)
        gu_ref[0, 1] = u.astype(BF16)
        a_ref[0] = (g * _sigmoid(g) * u).astype(BF16)

    wblk = lambda off: pl.BlockSpec((1, D_MODEL, FF_BLOCK), lambda i, k: (k + off, 0, 0))
    return _pc(body, name, (s // tm, 4), [pl.BlockSpec((tm, D_MODEL), lambda i, k: (i, 0)), wblk(0), wblk(4)],
               [pl.BlockSpec((1, 2, tm, FF_BLOCK), lambda i, k: (k, 0, i, 0)), pl.BlockSpec((1, tm, FF_BLOCK), lambda i, k: (k, i, 0))],
               [_sds((4, 2, s, FF_BLOCK), BF16), _sds((4, s, FF_BLOCK), BF16)], hosted=hosted)(h, w_gu, w_gu)


def _stage_gather(stages, stage):
    return (stages[stage][0], True) if stages and stage in stages else None


def _stage_arrived(stages, stage, got, default=None):
    late = stages[stage][1](got) if stages and stage in stages else None
    return default if late is None else late


def _stage_slabs(hosted_fn, stage, **new):
    arrays = hosted_fn(stage, **new) if hosted_fn is not None else None
    return None if arrays is None else (arrays, False)


def _ffn_dgate_up(dy, w_down, gu, name, hosted=None):
    s = dy.shape[0]
    tm = _pick_tile(s)

    def body(dy_ref, w_ref, gu_ref, o_ref):
        dav = _dot(dy_ref[...].astype(BF16), w_ref[0], 1, 1)
        g = gu_ref[0, 0].astype(F32)
        u = gu_ref[0, 1].astype(F32)
        sg = _sigmoid(g)
        o_ref[0, 0] = (dav * u * (sg * (1.0 + g * (1.0 - sg)))).astype(BF16)
        o_ref[0, 1] = (dav * (g * sg)).astype(BF16)

    pair = pl.BlockSpec((1, 2, tm, FF_BLOCK), lambda i, k: (k, 0, i, 0))
    return _pc(body, name, (s // tm, 4),
               [pl.BlockSpec((tm, D_MODEL), lambda i, k: (i, 0)), pl.BlockSpec((1, FF_BLOCK, D_MODEL), lambda i, k: (k, 0, 0)), pair],
               pair, _sds((4, 2, s, FF_BLOCK), BF16), hosted=hosted)(dy, w_down, gu)


def _ffn_fwd(x, h, w_gu, w_down, tag, next_norm=None, stages=None):
    s = x.shape[0]
    tm = _pick_tile(s)
    gu, a, *got = _ffn_gate_up(h, w_gu, f"ffn_gu_{tag}", _stage_gather(stages, "ffn_gu"))
    w_down = _stage_arrived(stages, "ffn_gu", got, w_down)
    xspec = pl.BlockSpec((tm, D_MODEL), lambda i, j, k: (i, 0))
    hosted = _stage_gather(stages, "ffn_down")
    y = _mm_spec(a, w_down, f"ffn_down_{tag}", (s // tm, 1, 4),
                 pl.BlockSpec((1, tm, FF_BLOCK), lambda i, j, k: (k, i, 0)),
                 pl.BlockSpec((1, FF_BLOCK, D_MODEL), lambda i, j, k: (k, 0, 0)),
                 xspec, _sds((s, D_MODEL)), 1, 0, (tm, D_MODEL), drop=(1, 1, 0), res=x, r_spec=xspec, norm_w=next_norm,
                 hosted=hosted)
    n_own = 2 if next_norm is not None else 1
    own = list(y[:n_own]) if (hosted is not None or next_norm is not None) else [y]
    if hosted is not None:
        _stage_arrived(stages, "ffn_down", list(y[n_own:]))
    y, h_next = own if next_norm is not None else (own[0], None)
    return y, h_next, (x, h, gu, a)


def _ffn_bwd(dy, saved, norm_w, w_gu, w_down, tag, hosted_fn=None):
    x, h, gu, a = saved
    s = x.shape[0]
    tm = _pick_tile(s)
    got = {}
    hosted = _stage_slabs(hosted_fn, "ffn_gdown")
    g_down = _mm_spec(a, dy, f"ffn_gdown_{tag}", (4, 1, s // tm),
                      pl.BlockSpec((1, tm, FF_BLOCK), lambda i, j, k: (i, k, 0)),
                      pl.BlockSpec((tm, D_MODEL), lambda i, j, k: (k, 0)),
                      pl.BlockSpec((1, FF_BLOCK, D_MODEL), lambda i, j, k: (i, 0, 0)),
                      _sds((4, FF_BLOCK, D_MODEL), BF16), 0, 0, (FF_BLOCK, D_MODEL), drop=(1, 0, 1), hosted=hosted)
    if hosted is not None:
        g_down, *got["ffn_gdown"] = g_down
    hosted = _stage_slabs(hosted_fn, "ffn_dgu", g_down=g_down)
    dgu = _ffn_dgate_up(dy, w_down, gu, f"ffn_dgu_{tag}", hosted)
    if hosted is not None:
        dgu, *got["ffn_dgu"] = dgu
    g_gu = _mm_spec(h, dgu, f"ffn_ggu_{tag}", (NDEV, 1, s // tm),
                    pl.BlockSpec((tm, D_MODEL), lambda i, j, k: (k, 0)),
                    pl.BlockSpec((1, 1, tm, FF_BLOCK), lambda i, j, k: (i % 4, i // 4, k, 0)),
                    pl.BlockSpec((1, D_MODEL, FF_BLOCK), lambda i, j, k: (i, 0, 0)),
                    _sds((NDEV, D_MODEL, FF_BLOCK), BF16), 0, 0, (D_MODEL, FF_BLOCK), drop=(0, 2, 1))
    hosted = _stage_slabs(hosted_fn, "ffn_dh", g_gu=g_gu)
    dx, g_norm, *arrived = _mm_dnorm(dgu, w_gu, f"ffn_dh_{tag}", NDEV,
                                     pl.BlockSpec((1, 1, tm, FF_BLOCK), lambda i, j, k: (k % 4, k // 4, i, 0)),
                                     pl.BlockSpec((1, D_MODEL, FF_BLOCK), lambda i, j, k: (k, 0, 0)), 1, 1, (2, 1), x, norm_w, dy,
                                     hosted=hosted)
    if hosted is not None:
        got["ffn_dh"] = arrived
    return dx, g_norm, g_gu, g_down, got


def _prev_rows(cur, halo, j, first):
    rid = lax.broadcasted_iota(jnp.int32, cur.shape, 0)
    hid = lax.broadcasted_iota(jnp.int32, halo.shape, 0)
    out = pltpu.roll(cur, j, 0)
    for t in range(j):
        row = jnp.sum(jnp.where(hid == 8 - j + t, halo, 0.0), axis=0, keepdims=True)
        row = jnp.where(first, 0.0, row)
        out = jnp.where(rid == t, row, out)
    return out


def _next_rows(cur, halo, j, last):
    ts = cur.shape[0]
    rid = lax.broadcasted_iota(jnp.int32, cur.shape, 0)
    hid = lax.broadcasted_iota(jnp.int32, halo.shape, 0)
    out = pltpu.roll(cur, ts - j, 0)
    for t in range(j):
        row = jnp.sum(jnp.where(hid == t, halo, 0.0), axis=0, keepdims=True)
        row = jnp.where(last, 0.0, row)
        out = jnp.where(rid == ts - j + t, row, out)
    return out


def _halo_specs(ts, s, width, col):
    per = ts // 8
    nblk = s // 8
    prev = pl.BlockSpec((8, width), lambda i: (jnp.maximum(i * per - 1, 0), col))
    nxt = pl.BlockSpec((8, width), lambda i: (jnp.minimum((i + 1) * per, nblk - 1), col))
    return prev, nxt


def _cgate_fwd(p, w_dw, name, hosted=None):
    s = p.shape[0]
    d = D_MODEL
    ts = _row_tile(s)
    prev, _ = _halo_specs(ts, s, 3 * d, 0)

    def body(p_ref, h_ref, w_ref, z_ref):
        first = pl.program_id(0) == 0
        b = p_ref[:, :d]
        cv = p_ref[:, d:2 * d] * p_ref[:, 2 * d:]
        hcv = h_ref[:, d:2 * d] * h_ref[:, 2 * d:]
        u = w_ref[2:3, :] * cv + w_ref[1:2, :] * _prev_rows(cv, hcv, 1, first) + w_ref[0:1, :] * _prev_rows(cv, hcv, 2, first)
        z_ref[...] = (b * u).astype(BF16)

    return _pc(body, name, (s // ts,),
               [pl.BlockSpec((ts, 3 * d), lambda i: (i, 0)), prev, pl.BlockSpec((3, d), lambda i: (0, 0))],
               pl.BlockSpec((ts, d), lambda i: (i, 0)), _sds((s, d), BF16), hosted=hosted)(p, p, w_dw)


def _cgate_bwd(p, dz, w_dw, name):
    s = p.shape[0]
    d = D_MODEL
    ts = _row_tile(s)
    nt = s // ts
    p_prev, p_next = _halo_specs(ts, s, 3 * d, 0)
    _, dz_next = _halo_specs(ts, s, d, 0)

    def body(p_ref, hp_ref, hn_ref, dz_ref, dzn_ref, w_ref, dp_ref, dw_ref):
        i = pl.program_id(0)
        first = i == 0
        last = i == nt - 1
        b = p_ref[:, :d]
        c = p_ref[:, d:2 * d]
        v = p_ref[:, 2 * d:]
        cv = c * v
        hcv = hp_ref[:, d:2 * d] * hp_ref[:, 2 * d:]
        cv1 = _prev_rows(cv, hcv, 1, first)
        cv2 = _prev_rows(cv, hcv, 2, first)
        w0, w1, w2 = w_ref[0:1, :], w_ref[1:2, :], w_ref[2:3, :]
        u = w2 * cv + w1 * cv1 + w0 * cv2
        dzv = dz_ref[...]
        du = dzv * b
        dun = dzn_ref[...] * hn_ref[:, :d]
        dcv = w2 * du + w1 * _next_rows(du, dun, 1, last) + w0 * _next_rows(du, dun, 2, last)
        dp_ref[:, :d] = (dzv * u).astype(BF16)
        dp_ref[:, d:2 * d] = (dcv * v).astype(BF16)
        dp_ref[:, 2 * d:] = (dcv * c).astype(BF16)

        @pl.when(first)
        def _():
            dw_ref[...] = jnp.zeros_like(dw_ref)

        dw_ref[0:1, :] += jnp.sum(du * cv2, axis=0, keepdims=True)
        dw_ref[1:2, :] += jnp.sum(du * cv1, axis=0, keepdims=True)
        dw_ref[2:3, :] += jnp.sum(du * cv, axis=0, keepdims=True)

    wide = pl.BlockSpec((ts, 3 * d), lambda i: (i, 0))
    wspec = pl.BlockSpec((3, d), lambda i: (0, 0))
    return _pc(body, name, (nt,),
               [wide, p_prev, p_next, pl.BlockSpec((ts, d), lambda i: (i, 0)), dz_next, wspec],
               [wide, wspec], [_sds((s, 3 * d), BF16), _sds((3, d))])(p, p, p, dz, dz, w_dw)


def _conv_fwd(x, h, w_in, w_dw, w_out, tag, next_norm, stages=None):
    wn = _cols_from_blocks(w_in)
    hosted = _stage_gather(stages, "conv_in")
    p = _mm(h, wn, "nn", f"conv_in_{tag}", hosted=hosted)
    if hosted is not None:
        p, *got = p
        _stage_arrived(stages, "conv_in", got)
    hosted = _stage_gather(stages, "conv_gate")
    z = _cgate_fwd(p, w_dw, f"conv_gate_{tag}", hosted)
    if hosted is not None:
        z, *got = z
        w_out = _stage_arrived(stages, "conv_gate", got, w_out)
    y, h_next = _mm(z, w_out, "nn", f"conv_out_{tag}", res=x, norm_w=next_norm)
    return y, h_next, (x, h, p, z, wn)


def _conv_bwd(dy, saved, norm_w, w_in, w_dw, w_out, tag, hosted_fn=None):
    x, h, p, z, wn = saved
    dz = _mm(dy, w_out, "nt", f"conv_dz_{tag}")
    g_out = _mm(z, dy, "tn", f"conv_gout_{tag}", out_dtype=BF16)
    dp, g_dw = _cgate_bwd(p, dz, w_dw, f"conv_dgate_{tag}")
    g_in = _blocks_from_cols(_mm(h, dp, "tn", f"conv_gin_{tag}", out_dtype=BF16))
    hosted = _stage_slabs(hosted_fn, "conv_dh", g_in=g_in, g_out=g_out)
    dx, g_norm, *got = _mm_dnorm_nt(dp, wn, f"conv_dh_{tag}", x, norm_w, dy, hosted=hosted)
    return dx, g_norm, g_in, g_dw, g_out, ({"conv_dh": got} if hosted is not None else {})


def _tri(lower, n=LANES):
    r = lax.broadcasted_iota(jnp.int32, (n, n), 0)
    c = lax.broadcasted_iota(jnp.int32, (n, n), 1)
    return jnp.where((r >= c) if lower else (r <= c), 1.0, 0.0).astype(F32)


def _cumsum_rows(v, name):
    s = v.shape[0]
    rows = _row_tile(s, CUMSUM_ROWS)

    def body(v_ref, o_ref, carry_ref):
        @pl.when(pl.program_id(0) == 0)
        def _():
            carry_ref[...] = jnp.zeros_like(carry_ref)

        blk = v_ref[...]
        o_ref[...] = _dot(_tri(True, rows), blk, 1, 0, HI) + carry_ref[0:1, :]
        carry_ref[...] += jnp.sum(blk, axis=0, keepdims=True)

    spec = pl.BlockSpec((rows, LANES), lambda i: (i, 0))
    return _pc(body, name, (s // rows,), [spec], spec, _sds((s, LANES)), [pltpu.VMEM((8, LANES), F32)])(v)


def _fox_dcum(dck, dcq, name):
    s = dck.shape[0]
    rows = _row_tile(s, CUMSUM_ROWS)
    n = s // rows

    def body(k_ref, q_ref, o_ref, carry_ref):
        @pl.when(pl.program_id(0) == 0)
        def _():
            carry_ref[...] = jnp.zeros_like(carry_ref)

        head = lax.broadcasted_iota(jnp.int32, (ATTN_HEADS, LANES), 0)
        unit = jnp.where(head == lax.broadcasted_iota(jnp.int32, (ATTN_HEADS, LANES), 1), 1.0, 0.0).astype(F32)
        blk = _dot(q_ref[...], unit, 0, 0, HI)
        lane = lax.broadcasted_iota(jnp.int32, (rows, LANES), 1)
        for hd in range(ATTN_HEADS):
            first_lane = hd * HEAD_DIM
            pair = k_ref[:, first_lane // LANES * LANES:(first_lane // LANES + 1) * LANES]
            blk = blk + jnp.where(lane == hd, pltpu.roll(pair, (hd - first_lane) % LANES, axis=1), 0.0)
        o_ref[...] = _dot(_tri(False, rows), blk, 1, 0, HI) + carry_ref[0:1, :]
        carry_ref[...] += jnp.sum(blk, axis=0, keepdims=True)

    return _pc(body, name, (n,),
               [pl.BlockSpec((rows, D_MODEL), lambda i: (n - 1 - i, 0)), pl.BlockSpec((ATTN_HEADS, rows), lambda i: (0, n - 1 - i))],
               pl.BlockSpec((rows, LANES), lambda i: (n - 1 - i, 0)), _sds((s, LANES)), [pltpu.VMEM((8, LANES), F32)])(dck, dcq)


def _lo_mask(shape):
    return lax.broadcasted_iota(jnp.int32, shape, len(shape) - 1) < HEAD_DIM


def _half_sums(v, lo):
    sa = jnp.sum(jnp.where(lo, v, 0.0), axis=-1, keepdims=True)
    sb = jnp.sum(jnp.where(lo, 0.0, v), axis=-1, keepdims=True)
    return jnp.where(lo, sa, sb)


def _fox_prep_fwd(proj, gq, gk, name):
    s = proj.shape[0]
    ts = _row_tile(s)
    qscale = HEAD_DIM ** -0.5 * LOG2E

    def body(q_ref, k_ref, v_ref, gq_ref, gk_ref, qo_ref, ko_ref, vo_ref):
        lo = _lo_mask((ts, LANES))

        def hnorm(xv, g):
            ms = _half_sums(xv * xv, lo) * (1.0 / HEAD_DIM)
            return (xv * lax.rsqrt(ms + RMS_EPS)) * g

        for p in range(8):
            cols = slice(p * LANES, (p + 1) * LANES)
            qo_ref[:, cols] = (hnorm(q_ref[:, cols], gq_ref[...]) * qscale).astype(BF16)
            ko_ref[:, cols] = hnorm(k_ref[:, cols], gk_ref[...]).astype(BF16)
        vo_ref[...] = v_ref[...].astype(BF16)

    def wide(blk):
        return pl.BlockSpec((ts, D_MODEL), lambda i: (i, blk))

    gspec = pl.BlockSpec((1, LANES), lambda i: (0, 0))
    out = _sds((s, D_MODEL), BF16)
    return _pc(body, name, (s // ts,), [wide(0), wide(1), wide(2), gspec, gspec], [wide(0)] * 3, [out] * 3)(
        proj, proj, proj, gq, gk)


def _fox_logf(proj, bf, name):
    s = proj.shape[0]
    ts = _row_tile(s, 512)

    def body(f_ref, b_ref, o_ref):
        z = f_ref[...] + b_ref[...]
        lf = jnp.minimum(z, 0.0) - jnp.log(1.0 + jnp.exp(-jnp.abs(z)))
        real = lax.broadcasted_iota(jnp.int32, (ts, LANES), 1) < ATTN_HEADS
        o_ref[...] = jnp.where(real, lf, 0.0)

    return _pc(body, name, (s // ts,), [pl.BlockSpec((ts, LANES), lambda i: (i, 24)), pl.BlockSpec((1, LANES), lambda i: (0, 0))],
               pl.BlockSpec((ts, LANES), lambda i: (i, 0)), _sds((s, LANES)))(proj, bf)


def _fox_dlogf(proj, bf, dlf, name):
    s = proj.shape[0]
    ts = _row_tile(s, 512)

    def body(f_ref, b_ref, d_ref, o_ref, db_ref):
        z = f_ref[...] + b_ref[...]
        real = lax.broadcasted_iota(jnp.int32, (ts, LANES), 1) < ATTN_HEADS
        g = jnp.where(real, d_ref[...] * _sigmoid(-z), 0.0)
        o_ref[...] = g.astype(BF16)

        @pl.when(pl.program_id(0) == 0)
        def _():
            db_ref[...] = jnp.zeros_like(db_ref)

        db_ref[...] += jnp.sum(g, axis=0, keepdims=True)

    vec = pl.BlockSpec((1, LANES), lambda i: (0, 0))
    row = pl.BlockSpec((ts, LANES), lambda i: (i, 0))
    return _pc(body, name, (s // ts,), [pl.BlockSpec((ts, LANES), lambda i: (i, 24)), vec, row], [row, vec],
               [_sds((s, LANES), BF16), _sds((1, LANES))])(proj, bf, dlf)


def _decay_terms(cum):
    s = cum.shape[0]
    c2 = cum * LOG2E
    hi = lax.reduce_precision(c2, 8, 7)
    mid = lax.reduce_precision(c2 - hi, 8, 7)
    low = lax.reduce_precision(c2 - hi - mid, 8, 7)
    one = jnp.ones_like(hi)

    def place(terms):
        tt = jnp.stack(terms, axis=-1).astype(BF16).reshape(s, 8, 2, 6)
        z = jnp.zeros((s, 8, HEAD_DIM - 6), BF16)
        return jnp.concatenate([tt[:, :, 1], z, tt[:, :, 0], z], axis=-1).reshape(s, D_MODEL)

    return place([hi, mid, low, one, one, one]), place([one, one, one, -hi, -mid, -low])


def _attn_tiles(s):
    t = s
    for cand in (ATTN_TILE, ATTN_TILE // 2):
        if s % cand == 0:
            t = cand
            break
    return t, s // t


def _tri_steps(n, by_key):
    if by_key:
        pairs = [(q, k) for k in range(n) for q in range(k, n)]
    else:
        pairs = [(q, k) for q in range(n) for k in range(q + 1)]
    arr = np.asarray(pairs, np.int32)
    return jnp.asarray(arr[:, 0]), jnp.asarray(arr[:, 1])


def _attn_call(body, name, s, by_key, inputs, in_kinds, out_kinds, out_shapes, scratch, hosted=None, vmem=VMEM_LIMIT_BYTES):
    t, n = _attn_tiles(s)
    qi_arr, ki_arr = _tri_steps(n, by_key)
    nsteps = int(qi_arr.shape[0])
    specs = {
        "q": pl.BlockSpec((t, LANES), lambda p, i, qi, ki: (qi[i], p)),
        "k": pl.BlockSpec((t, LANES), lambda p, i, qi, ki: (ki[i], p)),
        "r": pl.BlockSpec((1, 2, t), lambda p, i, qi, ki: (p, 0, qi[i])),
        "m": pl.BlockSpec((1, t, t), lambda p, i, qi, ki: (jnp.where(qi[i] == ki[i], 1, 0), 0, 0)),
        "Q": pl.BlockSpec((1, LANES, s), lambda p, i, qi, ki: (p, 0, 0)),
        "R": pl.BlockSpec((1, 2, s), lambda p, i, qi, ki: (p, 0, 0)),
    }
    in_specs = [specs[c] for c in in_kinds]
    out_specs = [specs[c] for c in out_kinds]
    out_shapes, scratch, inputs = list(out_shapes), list(scratch), list(inputs)
    run = body
    if hosted is not None:
        arrays, gather = hosted
        na, n_in, n_out, n_scr = len(arrays), len(inputs), len(out_kinds), len(scratch)
        pick, xouts, sems = _exchange_parts(arrays, gather)

        def run(qi_ref, ki_ref, *refs):
            ins, srcs = refs[:n_in], refs[n_in:n_in + na]
            outs, dsts = refs[n_in + na:n_in + na + n_out], refs[n_in + na + n_out:n_in + 2 * na + n_out]
            scr, xsems = refs[n_in + 2 * na + n_out:n_in + 2 * na + n_out + n_scr], refs[n_in + 2 * na + n_out + n_scr:]
            p = pl.program_id(0)
            i = pl.program_id(1)

            @pl.when(jnp.logical_and(p == 0, i == 0))
            def _():
                _exchange_start(_exchange_copies(pick(srcs), dsts, *xsems))

            body(qi_ref, ki_ref, *ins, *outs, *scr)

            @pl.when(jnp.logical_and(p == 7, i == nsteps - 1))
            def _():
                _exchange_wait(_exchange_copies(pick(srcs), dsts, *xsems))

        hbm = pl.BlockSpec(memory_space=pl.ANY)
        in_specs += [hbm] * na
        out_specs += [hbm] * na
        out_shapes += xouts
        scratch += sems
        inputs += list(arrays)
    grid_spec = pltpu.PrefetchScalarGridSpec(
        num_scalar_prefetch=2, grid=(8, nsteps), in_specs=in_specs, out_specs=out_specs, scratch_shapes=scratch)
    params = pltpu.CompilerParams(dimension_semantics=("arbitrary", "arbitrary"), vmem_limit_bytes=vmem)
    return pl.pallas_call(run, name=name, grid_spec=grid_spec, out_shape=out_shapes, compiler_params=params)(
        qi_arr, ki_arr, *inputs)


def _biased_kq(q2, k2, aq, ak, lo):
    sa = _dot(jnp.where(lo, k2, ak), jnp.where(lo, q2, aq), 1, 1)
    sb = _dot(jnp.where(lo, ak, k2), jnp.where(lo, aq, q2), 1, 1)
    return sa, sb


def _causal_bias(s):
    t, _ = _attn_tiles(s)
    kid = lax.broadcasted_iota(jnp.int32, (t, t), 0)
    qid = lax.broadcasted_iota(jnp.int32, (t, t), 1)
    return jnp.stack([jnp.zeros((t, t), BF16), jnp.where(kid > qid, -jnp.inf, 0.0).astype(BF16)])


def _fold8(v, op):
    return op(v.reshape(v.shape[0] // 8, 8, v.shape[1]), axis=0)


def _chunk(ref, mask_ref, hd, r):
    rows = slice(r * ATTN_ROWS, (r + 1) * ATTN_ROWS)
    return rows, ref[hd, rows, :] + mask_ref[0, rows, :].astype(F32)


def _flash_fwd(qs, kn, vb, augq, augk, cmask, name, hosted=None):
    s = qs.shape[0]
    t, n = _attn_tiles(s)
    nch = t // ATTN_ROWS

    def body(qi_ref, ki_ref, q_ref, k_ref, v_ref, aq_ref, ak_ref, mk_ref, o_ref, lse_ref, s_ref, p_ref, m_ref, l_ref, acc_ref):
        i = pl.program_id(1)
        qi = qi_ref[i]
        ki = ki_ref[i]

        @pl.when(ki == 0)
        def _():
            m_ref[...] = jnp.full_like(m_ref, -jnp.inf)
            l_ref[...] = jnp.zeros_like(l_ref)
            acc_ref[...] = jnp.zeros_like(acc_ref)

        lo = _lo_mask((t, LANES))
        rowlo = lax.broadcasted_iota(jnp.int32, (LANES, t), 0) < HEAD_DIM
        v2 = v_ref[...]
        sa, sb = _biased_kq(q_ref[...], k_ref[...], aq_ref[...], ak_ref[...], lo)
        s_ref[0] = sa
        s_ref[1] = sb
        alphas, pvs = [], []
        for hd in range(2):
            mx = jnp.full((8, t), -jnp.inf, F32)
            for r in range(nch):
                _, sc = _chunk(s_ref, mk_ref, hd, r)
                mx = jnp.maximum(mx, _fold8(sc, jnp.max))
            m_prev = m_ref[hd:hd + 1, :]
            m_new = jnp.maximum(m_prev, jnp.max(mx, axis=0, keepdims=True))
            ls = jnp.zeros((8, t), F32)
            for r in range(nch):
                rows, sc = _chunk(s_ref, mk_ref, hd, r)
                pm = jnp.exp2(sc - m_new)
                ls = ls + _fold8(pm, jnp.sum)
                p_ref[hd, rows, :] = pm.astype(BF16)
            alpha = jnp.exp2(m_prev - m_new)
            l_ref[hd:hd + 1, :] = alpha * l_ref[hd:hd + 1, :] + jnp.sum(ls, axis=0, keepdims=True)
            m_ref[hd:hd + 1, :] = m_new
            alphas.append(alpha)
            pvs.append(_dot(v2, p_ref[hd], 0, 0))
        acc_ref[...] = jnp.where(rowlo, alphas[0], alphas[1]) * acc_ref[...] + jnp.where(rowlo, pvs[0], pvs[1])

        @pl.when(ki == qi)
        def _():
            o_ref[...] = (acc_ref[...] / jnp.where(rowlo, l_ref[0:1, :], l_ref[1:2, :])).T
            lse_ref[0] = m_ref[0:2, :] + jnp.log2(l_ref[0:2, :])

    stat = pltpu.VMEM((8, t), F32)
    return _attn_call(body, name, s, False, (qs, kn, vb, augq, augk, cmask), "qkkqkm", "qr",
                      [_sds((s, D_MODEL)), _sds((8, 2, s))],
                      [pltpu.VMEM((2, t, t), F32), pltpu.VMEM((2, t, t), BF16), stat, stat, pltpu.VMEM((LANES, t), F32)],
                      hosted=hosted)


def _fox_delta(do, o, name):
    s = do.shape[0]
    ts = _row_tile(s)

    def body(do_ref, o_ref, d_ref):
        head = lax.broadcasted_iota(jnp.int32, (ATTN_HEADS, D_MODEL), 0)
        col = lax.broadcasted_iota(jnp.int32, (ATTN_HEADS, D_MODEL), 1)
        member = jnp.where(col // HEAD_DIM == head, 1.0, 0.0).astype(F32)
        d_ref[...] = _dot(member, do_ref[...] * o_ref[...], 1, 1, HI)

    spec = pl.BlockSpec((ts, D_MODEL), lambda i: (i, 0))
    return _pc(body, name, (s // ts,), [spec, spec], pl.BlockSpec((ATTN_HEADS, ts), lambda i: (0, i)), _sds((ATTN_HEADS, s)))(do, o)


def _bwd_tile(q_ref, k_ref, v_ref, aq_ref, ak_ref, do_ref, s_ref, dp_ref, lo):
    do2 = do_ref[...].astype(BF16)
    zero = jnp.zeros_like(do2)
    v2 = v_ref[...]
    sa, sb = _biased_kq(q_ref[...], k_ref[...], aq_ref[...], ak_ref[...], lo)
    s_ref[0] = sa
    s_ref[1] = sb
    dp_ref[0] = _dot(v2, jnp.where(lo, do2, zero), 1, 1)
    dp_ref[1] = _dot(v2, jnp.where(lo, zero, do2), 1, 1)
    return do2


def _bwd_chunk(s_ref, dp_ref, mk_ref, lse_ref, dl_ref, hd, r):
    rows, sc = _chunk(s_ref, mk_ref, hd, r)
    pm = jnp.exp2(sc - lse_ref[0, hd:hd + 1, :])
    ds = pm * (dp_ref[hd, rows, :] - dl_ref[0, hd:hd + 1, :])
    return rows, pm, ds


def _flash_bwd(qs, kn, vb, augq, augk, cmask, do, lse, delta, name, hosted=None):
    s = qs.shape[0]
    t, n = _attn_tiles(s)
    nch = t // ATTN_ROWS

    def body(qi_ref, ki_ref, q_ref, k_ref, v_ref, aq_ref, ak_ref, mk_ref, do_ref, lse_ref, dl_ref,
             dk_ref, dv_ref, dc_ref, dq_ref, dcq_ref, s_ref, dp_ref, p_ref, ds_ref, dka_ref, dva_ref, dca_ref):
        i = pl.program_id(1)
        qi = qi_ref[i]
        ki = ki_ref[i]

        @pl.when(i == 0)
        def _():
            dq_ref[...] = jnp.zeros_like(dq_ref)
            dcq_ref[...] = jnp.zeros_like(dcq_ref)

        @pl.when(qi == ki)
        def _():
            dka_ref[...] = jnp.zeros_like(dka_ref)
            dva_ref[...] = jnp.zeros_like(dva_ref)
            dca_ref[...] = jnp.zeros_like(dca_ref)

        lo = _lo_mask((t, LANES))
        rowlo = lax.broadcasted_iota(jnp.int32, (LANES, t), 0) < HEAD_DIM
        do2 = _bwd_tile(q_ref, k_ref, v_ref, aq_ref, ak_ref, do_ref, s_ref, dp_ref, lo)
        q2 = q_ref[...]
        k2 = k_ref[...]
        qcols = pl.ds(pl.multiple_of(qi * t, t), t)
        dvs, dks, dqs = [], [], []
        for hd in range(2):
            rs = jnp.zeros((8, t), F32)
            for r in range(nch):
                rows, pm, ds = _bwd_chunk(s_ref, dp_ref, mk_ref, lse_ref, dl_ref, hd, r)
                rs = rs + _fold8(ds, jnp.sum)
                part = ds[:, 0:LANES]
                for c in range(1, t // LANES):
                    part = part + ds[:, c * LANES:(c + 1) * LANES]
                dca_ref[hd, rows, :] += part
                p_ref[hd, rows, :] = pm.astype(BF16)
                ds_ref[hd, rows, :] = ds.astype(BF16)
            dcq_ref[0, hd:hd + 1, qcols] += jnp.sum(rs, axis=0, keepdims=True)
            dvs.append(_dot(p_ref[hd], do2, 1, 0))
            dks.append(_dot(ds_ref[hd], q2, 1, 0))
            dqs.append(_dot(k2, ds_ref[hd], 0, 0))
        dva_ref[...] += jnp.where(lo, dvs[0], dvs[1])
        dka_ref[...] += jnp.where(lo, dks[0], dks[1])
        dq_ref[0, :, qcols] += jnp.where(rowlo, dqs[0], dqs[1])

        @pl.when(qi == n - 1)
        def _():
            dk_ref[...] = dka_ref[...] * LN2
            dv_ref[...] = dva_ref[...]
            dc_ref[...] = -jnp.where(lo, jnp.sum(dca_ref[0], axis=-1, keepdims=True), jnp.sum(dca_ref[1], axis=-1, keepdims=True))

    out = _sds((s, D_MODEL))
    return _attn_call(body, name, s, True, (qs, kn, vb, augq, augk, cmask, do, lse, delta), "qkkqkmqrr", "kkkQR",
                      [out, out, out, _sds((8, LANES, s)), _sds((8, 2, s))],
                      [pltpu.VMEM((2, t, t), F32), pltpu.VMEM((2, t, t), F32), pltpu.VMEM((2, t, t), BF16),
                       pltpu.VMEM((2, t, t), BF16), pltpu.VMEM((t, LANES), F32), pltpu.VMEM((t, LANES), F32),
                       pltpu.VMEM((2, t, LANES), F32)], hosted=hosted, vmem=ATTN_BWD_VMEM_BYTES)


def _fox_prep_bwd(proj, dqs, dk, dv, gq, gk, name):
    s = proj.shape[0]
    ts = _row_tile(s)
    scale = HEAD_DIM ** -0.5

    def body(q_ref, k_ref, dq_ref, dk_ref, dv_ref, gq_ref, gk_ref, oq_ref, ok_ref, ov_ref, dgq_ref, dgk_ref):
        lo = _lo_mask((ts, LANES))

        @pl.when(pl.program_id(0) == 0)
        def _():
            dgq_ref[...] = jnp.zeros_like(dgq_ref)
            dgk_ref[...] = jnp.zeros_like(dgk_ref)

        def back(xv, dout, g):
            r = lax.rsqrt(_half_sums(xv * xv, lo) * (1.0 / HEAD_DIM) + RMS_EPS)
            y = xv * r
            dy = dout * g
            dx = r * (dy - y * (_half_sums(dy * y, lo) * (1.0 / HEAD_DIM)))
            return dx, jnp.sum(dout * y, axis=0, keepdims=True)

        for p in range(8):
            cols = slice(p * LANES, (p + 1) * LANES)
            dxq, dgq = back(q_ref[:, cols], dq_ref[p].T * scale, gq_ref[...])
            dxk, dgk = back(k_ref[:, cols], dk_ref[:, cols], gk_ref[...])
            oq_ref[:, cols] = dxq.astype(BF16)
            ok_ref[:, cols] = dxk.astype(BF16)
            dgq_ref[...] += dgq
            dgk_ref[...] += dgk
        ov_ref[...] = dv_ref[...].astype(BF16)

    def wide(blk):
        return pl.BlockSpec((ts, D_MODEL), lambda i: (i, blk))

    gspec = pl.BlockSpec((1, LANES), lambda i: (0, 0))
    out = _sds((s, D_MODEL), BF16)
    dqt = pl.BlockSpec((8, LANES, ts), lambda i: (0, 0, i))
    return _pc(body, name, (s // ts,), [wide(0), wide(1), dqt, wide(0), wide(0), gspec, gspec],
               [wide(0)] * 3 + [gspec] * 2, [out] * 3 + [_sds((1, LANES))] * 2)(proj, proj, dqs, dk, dv, gq, gk)


def _fox_fwd(x, h, w_in, b_f, q_gain, k_gain, w_out, next_norm, hosted=None):
    proj = _mm(h, w_in, "nn", "fox_in")
    gq = jnp.tile(q_gain, (1, 2))
    gk = jnp.tile(k_gain, (1, 2))
    bf = jnp.pad(b_f, ((0, 0), (0, LANES - ATTN_HEADS)))
    qs, kn, vb = _fox_prep_fwd(proj, gq, gk, "fox_prep")
    cum = _cumsum_rows(_fox_logf(proj, bf, "fox_logf"), "fox_cum")[:, :ATTN_HEADS]
    augq, augk = _decay_terms(cum)
    cmask = _causal_bias(x.shape[0])
    o, lse, *got = _flash_fwd(qs, kn, vb, augq, augk, cmask, "fox_attn", hosted=hosted)
    y, h_next = _mm(o, w_out, "nn", "fox_out", res=x, norm_w=next_norm)
    return y, h_next, (x, h, proj, gq, gk, bf, qs, kn, vb, augq, augk, cmask, o, lse), got


def _fox_bwd(dy, saved, norm_w, w_in, w_out, hosted=None):
    x, h, proj, gq, gk, bf, qs, kn, vb, augq, augk, cmask, o, lse = saved
    s = x.shape[0]
    do = _mm(dy, w_out, "nt", "fox_do")
    g_out = _mm(o, dy, "tn", "fox_gout", out_dtype=BF16)
    delta = _fox_delta(do, o, "fox_delta").reshape(8, 2, s)
    dk, dv, dck, dqs, dcq, *got = _flash_bwd(qs, kn, vb, augq, augk, cmask, do, lse, delta, "fox_dattn", hosted=hosted)
    dlf = _fox_dcum(dck, dcq.reshape(ATTN_HEADS, s), "fox_dcum")
    dfl, g_bf = _fox_dlogf(proj, bf, dlf, "fox_dlogf")
    dq_o, dk_o, dv_o, g_gq, g_gk = _fox_prep_bwd(proj, dqs, dk, dv, gq, gk, "fox_dprep")
    dproj = jnp.concatenate([dq_o, dk_o, dv_o, dfl], axis=1)
    g_in = _mm(h, dproj, "tn", "fox_gin", out_dtype=BF16)
    dx, g_norm = _mm_dnorm_nt(dproj, w_in, "fox_dh", x, norm_w, dy)
    g_q = g_gq[:, :HEAD_DIM] + g_gq[:, HEAD_DIM:]
    g_k = g_gk[:, :HEAD_DIM] + g_gk[:, HEAD_DIM:]
    return dx, g_norm, g_in[:, :FOX_IN], g_bf[:, :ATTN_HEADS], g_q, g_k, g_out, got


def _ssd_conv_fwd(proj, cw, cb, name):
    s = proj.shape[0]
    ts = _row_tile(s)
    w = 1024
    per = ts // 8

    def body(p_ref, h_ref, w_ref, b_ref, o_ref):
        first = pl.program_id(0) == 0
        cur = p_ref[...]
        halo = h_ref[...]
        u = w_ref[3:4, :] * cur + b_ref[...]
        for j in range(1, 4):
            u = u + w_ref[3 - j:4 - j, :] * _prev_rows(cur, halo, j, first)
        o_ref[...] = u * _sigmoid(u)

    return _pc(body, name, (s // ts, 4),
               [pl.BlockSpec((ts, w), lambda i, j: (i, 2 + j)),
                pl.BlockSpec((8, w), lambda i, j: (jnp.maximum(i * per - 1, 0), 2 + j)),
                pl.BlockSpec((4, w), lambda i, j: (0, j)), pl.BlockSpec((1, w), lambda i, j: (0, j))],
               pl.BlockSpec((ts, w), lambda i, j: (i, j)), _sds((s, SSM_CONV_DIM)))(proj, proj, cw, cb)


def _ssd_conv_bwd(proj, d, first_col, cw, cb, name):
    s = proj.shape[0]
    ts = _row_tile(s)
    nt = s // ts
    w = 1024
    ncol = d.shape[1] // w
    per = ts // 8
    nblk = s // 8

    def body(p_ref, hp_ref, hn_ref, d_ref, dn_ref, w_ref, b_ref, o_ref, dw_ref, db_ref):
        i = pl.program_id(1)
        first = i == 0
        last = i == nt - 1
        cur = p_ref[...]
        prev = [cur] + [_prev_rows(cur, hp_ref[...], j, first) for j in range(1, 4)]
        nxt = hn_ref[...]
        tail = cur[ts - 8:, :]
        u = b_ref[...]
        un = b_ref[...]
        for j in range(4):
            u = u + w_ref[3 - j:4 - j, :] * prev[j]
            un = un + w_ref[3 - j:4 - j, :] * (nxt if j == 0 else _prev_rows(nxt, tail, j, False))
        sg = _sigmoid(u)
        g = d_ref[...] * (sg * (1.0 + u * (1.0 - sg)))
        sn = _sigmoid(un)
        gn = dn_ref[...] * (sn * (1.0 + un * (1.0 - sn)))

        @pl.when(first)
        def _():
            dw_ref[...] = jnp.zeros_like(dw_ref)
            db_ref[...] = jnp.zeros_like(db_ref)

        dpre = w_ref[3:4, :] * g
        for j in range(1, 4):
            dpre = dpre + w_ref[3 - j:4 - j, :] * _next_rows(g, gn, j, last)
        for j in range(4):
            dw_ref[3 - j:4 - j, :] += jnp.sum(g * prev[j], axis=0, keepdims=True)
        db_ref[...] += jnp.sum(g, axis=0, keepdims=True)
        o_ref[...] = dpre.astype(BF16)

    tile = pl.BlockSpec((ts, w), lambda j, i: (i, j))
    wspec = lambda off: pl.BlockSpec((4, w), lambda j, i: (0, off + j))
    vec = lambda off: pl.BlockSpec((1, w), lambda j, i: (0, off + j))
    nxt_blk = lambda off: pl.BlockSpec((8, w), lambda j, i: (jnp.minimum((i + 1) * per, nblk - 1), off + j))
    in_proj = 2 + first_col
    return _pc(body, name, (ncol, nt),
               [pl.BlockSpec((ts, w), lambda j, i: (i, in_proj + j)),
                pl.BlockSpec((8, w), lambda j, i: (jnp.maximum(i * per - 1, 0), in_proj + j)), nxt_blk(in_proj),
                tile, nxt_blk(0), wspec(first_col), vec(first_col)],
               [tile, wspec(0), vec(0)], [_sds((s, ncol * w), BF16), _sds((4, ncol * w)), _sds((1, ncol * w))])(
                   proj, proj, proj, d, d, cw, cb)


def _ssd_dt_fwd(proj, bias, a_neg, name):
    s = proj.shape[0]
    n = s // SSM_CHUNK

    def body(r_ref, b_ref, a_ref, dt_ref, ac_ref):
        real = lax.broadcasted_iota(jnp.int32, (SSM_CHUNK, LANES), 1) < SSM_HEADS
        dt = jnp.where(real, _softplus(r_ref[...] + b_ref[...]), 0.0)
        dt_ref[...] = dt
        ac_ref[...] = _dot(_tri(True), dt * a_ref[...], 1, 0, HI)

    vec = pl.BlockSpec((1, LANES), lambda c: (0, 0))
    row = pl.BlockSpec((SSM_CHUNK, LANES), lambda c: (c, 0))
    return _pc(body, name, (n,), [pl.BlockSpec((SSM_CHUNK, LANES), lambda c: (c, 48)), vec, vec], [row, row],
               [_sds((s, LANES)), _sds((s, LANES))])(proj, bias, a_neg)


def _ssd_dt_bwd(proj, bias, ddt, name):
    s = proj.shape[0]
    ts = _row_tile(s, 512)

    def body(r_ref, b_ref, d_ref, o_ref, db_ref):
        real = lax.broadcasted_iota(jnp.int32, (ts, LANES), 1) < SSM_HEADS
        g = jnp.where(real, d_ref[...] * _sigmoid(r_ref[...] + b_ref[...]), 0.0)
        o_ref[...] = g.astype(BF16)

        @pl.when(pl.program_id(0) == 0)
        def _():
            db_ref[...] = jnp.zeros_like(db_ref)

        db_ref[...] += jnp.sum(g, axis=0, keepdims=True)

    vec = pl.BlockSpec((1, LANES), lambda i: (0, 0))
    row = pl.BlockSpec((ts, LANES), lambda i: (i, 0))
    return _pc(body, name, (s // ts,), [pl.BlockSpec((ts, LANES), lambda i: (i, 48)), vec, row], [row, vec],
               [_sds((s, LANES), BF16), _sds((1, LANES))])(proj, bias, ddt)


def _pair_cols(cols, k0, lo):
    return jnp.where(lo, cols[:, k0:k0 + 1], cols[:, k0 + 1:k0 + 2])


def _last_lane(row):
    lane = lax.broadcasted_iota(jnp.int32, row.shape, 1)
    return jnp.sum(jnp.where(lane == SSM_CHUNK - 1, row, 0.0), axis=-1, keepdims=True)


SSD_FWD_GROUPS = 2
SSD_BWD_GROUPS = 1


def _ssd_specs(nc, rev, n):
    cc = (lambda c: nc - 1 - c) if rev else (lambda c: c)
    nb = SSM_INNER // (LANES * n)
    return dict(
        x=pl.BlockSpec((SSM_CHUNK, 256 * n), lambda g, c: (cc(c), g)),
        b=pl.BlockSpec((SSM_CHUNK, LANES * n), lambda g, c: (cc(c), nb + g)),
        c=pl.BlockSpec((SSM_CHUNK, LANES * n), lambda g, c: (cc(c), nb + SSM_GROUPS // n + g)),
        col=pl.BlockSpec((n, SSM_CHUNK, 4), lambda g, c: (g, cc(c), 0)),
        row=pl.BlockSpec((n, 4, SSM_CHUNK), lambda g, c: (g, 0, cc(c))),
        grp=pl.BlockSpec((n, 1, 256), lambda g, c: (g, 0, 0)),
        grow=pl.BlockSpec((n, 4, LANES), lambda g, c: (g, 0, 0)),
        hs=pl.BlockSpec((1, n, 256, SSM_STATE), lambda g, c: (cc(c), g, 0, 0)),
        bc=pl.BlockSpec((SSM_CHUNK, LANES * n), lambda g, c: (cc(c), g)),
    )


def _ssd_scan_fwd(xbc, dtc, acol, drow, arow, dskip, name):
    s = xbc.shape[0]
    nc = s // SSM_CHUNK
    n = SSD_FWD_GROUPS
    sp = _ssd_specs(nc, False, n)
    L = SSM_CHUNK

    def body(x_ref, b_ref, c_ref, dtc_ref, ac_ref, dr_ref, ar_ref, dk_ref, y_ref, hs_ref, h_ref):
        @pl.when(pl.program_id(1) == 0)
        def _():
            h_ref[...] = jnp.zeros_like(h_ref)

        for gi in range(n):
            group(gi, x_ref, b_ref, c_ref, dtc_ref, ac_ref, dr_ref, ar_ref, dk_ref, y_ref, hs_ref, h_ref)

    def group(gi, x_ref, b_ref, c_ref, dtc_ref, ac_ref, dr_ref, ar_ref, dk_ref, y_ref, hs_ref, h_ref):
        x0 = gi * 256
        bb = b_ref[:, gi * LANES:(gi + 1) * LANES].astype(BF16)
        cb = c_ref[:, gi * LANES:(gi + 1) * LANES].astype(BF16)
        gm = _dot(cb, bb, 1, 1)
        dtc = dtc_ref[gi]
        ac = ac_ref[gi]
        dr = dr_ref[gi]
        ar = ar_ref[gi]
        dsk = dk_ref[gi]
        hs_ref[0, gi] = h_ref[gi]
        tril = lax.broadcasted_iota(jnp.int32, (L, L), 0) >= lax.broadcasted_iota(jnp.int32, (L, L), 1)
        lo = _lo_mask((L, LANES))
        rowlo = lax.broadcasted_iota(jnp.int32, (L, LANES), 0) < HEAD_DIM
        for pr in range(2):
            k0 = 2 * pr
            xp = x_ref[:, x0 + pr * LANES:x0 + (pr + 1) * LANES]
            xpb = xp.astype(BF16)
            hp = h_ref[gi, pr * LANES:(pr + 1) * LANES, :]
            yd, al = [], []
            for k in (k0, k0 + 1):
                seg = ac[:, k:k + 1] - ar[k:k + 1, :]
                wk = gm * jnp.exp(jnp.where(tril, seg, -jnp.inf)) * dr[k:k + 1, :]
                yd.append(_dot(wk.astype(BF16), xpb, 1, 0))
                al.append(_last_lane(ar[k:k + 1, :]))
            e = jnp.exp(_pair_cols(ac, k0, lo))
            yo = _dot(cb, hp.astype(BF16), 1, 1) * e
            y_ref[:, x0 + pr * LANES:x0 + (pr + 1) * LANES] = (
                jnp.where(lo, yd[0], yd[1]) + yo + dsk[:, pr * LANES:(pr + 1) * LANES] * xp)
            wp = jnp.where(lo, jnp.exp(al[0] - ac[:, k0:k0 + 1]) * dtc[:, k0:k0 + 1],
                           jnp.exp(al[1] - ac[:, k0 + 1:k0 + 2]) * dtc[:, k0 + 1:k0 + 2])
            st = _dot((xp * wp).astype(BF16), bb, 0, 0)
            dec = jnp.where(rowlo, jnp.exp(al[0]), jnp.exp(al[1]))
            h_ref[gi, pr * LANES:(pr + 1) * LANES, :] = dec * hp + st

    return _pc(body, name, (SSM_GROUPS // n, nc),
               [sp["x"], sp["b"], sp["c"], sp["col"], sp["col"], sp["row"], sp["row"], sp["grp"]],
               [sp["x"], sp["hs"]], [_sds((s, SSM_INNER)), _sds((nc, SSM_GROUPS, 256, SSM_STATE))],
               [pltpu.VMEM((n, 256, SSM_STATE), F32)])(xbc, xbc, xbc, dtc, acol, drow, arow, dskip)


def _ssd_scan_bwd(xbc, dtc, acol, drow, arow, dskip, agrp, hs, dy, name):
    s = xbc.shape[0]
    nc = s // SSM_CHUNK
    n = SSD_BWD_GROUPS
    sp = _ssd_specs(nc, True, n)
    L = SSM_CHUNK

    def body(x_ref, b_ref, c_ref, dtc_ref, ac_ref, dr_ref, ar_ref, dk_ref, ag_ref, hs_ref, dy_ref,
             dx_ref, db_ref, dc_ref, ddt_ref, da_ref, dd_ref, dh_ref):
        @pl.when(pl.program_id(1) == 0)
        def _():
            dh_ref[...] = jnp.zeros_like(dh_ref)
            da_ref[...] = jnp.zeros_like(da_ref)
            dd_ref[...] = jnp.zeros_like(dd_ref)

        for gi in range(n):
            group(gi, x_ref, b_ref, c_ref, dtc_ref, ac_ref, dr_ref, ar_ref, dk_ref, ag_ref, hs_ref, dy_ref,
                  dx_ref, db_ref, dc_ref, ddt_ref, da_ref, dd_ref, dh_ref)

    def group(gi, x_ref, b_ref, c_ref, dtc_ref, ac_ref, dr_ref, ar_ref, dk_ref, ag_ref, hs_ref, dy_ref,
              dx_ref, db_ref, dc_ref, ddt_ref, da_ref, dd_ref, dh_ref):
        x0 = gi * 256
        bcols = slice(gi * LANES, (gi + 1) * LANES)
        bb = b_ref[:, bcols].astype(BF16)
        cb = c_ref[:, bcols].astype(BF16)
        gm = _dot(cb, bb, 1, 1)
        dtc = dtc_ref[gi]
        ac = ac_ref[gi]
        dr = dr_ref[gi]
        ar = ar_ref[gi]
        dsk = dk_ref[gi]
        ag = ag_ref[gi]
        tril = lax.broadcasted_iota(jnp.int32, (L, L), 0) >= lax.broadcasted_iota(jnp.int32, (L, L), 1)
        lo = _lo_mask((L, LANES))
        nlo = jnp.logical_not(lo)
        rowlo = lax.broadcasted_iota(jnp.int32, (L, LANES), 0) < HEAD_DIM
        lane = lax.broadcasted_iota(jnp.int32, (L, LANES), 1)
        lane_row = lax.broadcasted_iota(jnp.int32, (1, LANES), 1)
        dgm = jnp.zeros((L, L), F32)
        dcm = jnp.zeros((L, SSM_STATE), F32)
        dbm = jnp.zeros((L, SSM_STATE), F32)
        cols = jnp.zeros((L, LANES), F32)
        rows_ddt, rows_q, al_all, dcd_all = [], [], [], []
        for pr in range(2):
            k0 = 2 * pr
            xcols = slice(x0 + pr * LANES, x0 + (pr + 1) * LANES)
            xp = x_ref[:, xcols]
            xpb = xp.astype(BF16)
            dyp = dy_ref[:, xcols]
            dypb = dyp.astype(BF16)
            zero = jnp.zeros_like(dypb)
            hp = hs_ref[0, gi, pr * LANES:(pr + 1) * LANES, :]
            hpb = hp.astype(BF16)
            dst = dh_ref[gi, pr * LANES:(pr + 1) * LANES, :]
            dstb = dst.astype(BF16)
            dxd, al = [], []
            for k in (k0, k0 + 1):
                sel = lo if k == k0 else nlo
                seg = ac[:, k:k + 1] - ar[k:k + 1, :]
                lam = jnp.exp(jnp.where(tril, seg, -jnp.inf))
                wk = gm * lam * dr[k:k + 1, :]
                dwk = _dot(jnp.where(sel, dypb, zero), xpb, 1, 1)
                mk = dwk * gm * lam
                qk = mk * dr[k:k + 1, :]
                dgm = dgm + dwk * lam * dr[k:k + 1, :]
                rows_ddt.append(jnp.sum(mk, axis=0, keepdims=True))
                rows_q.append(jnp.sum(qk, axis=0, keepdims=True))
                cols = jnp.where(lane == k, jnp.sum(qk, axis=-1, keepdims=True), cols)
                dxd.append(_dot(wk.astype(BF16), dypb, 0, 0))
                al.append(_last_lane(ar[k:k + 1, :]))
            al_all += al
            dxp = jnp.where(lo, dxd[0], dxd[1])
            e = jnp.exp(_pair_cols(ac, k0, lo))
            dye = dyp * e
            dyeb = dye.astype(BF16)
            dcm = dcm + _dot(dyeb, hpb, 1, 0)
            dh_yoff = _dot(dyeb, cb, 0, 0)
            tq = dye * _dot(cb, hpb, 1, 1)
            cols = jnp.where(lane == 4 + k0, jnp.sum(jnp.where(lo, tq, 0.0), axis=-1, keepdims=True), cols)
            cols = jnp.where(lane == 5 + k0, jnp.sum(jnp.where(lo, 0.0, tq), axis=-1, keepdims=True), cols)
            wp = jnp.where(lo, jnp.exp(al[0] - ac[:, k0:k0 + 1]) * dtc[:, k0:k0 + 1],
                           jnp.exp(al[1] - ac[:, k0 + 1:k0 + 2]) * dtc[:, k0 + 1:k0 + 2])
            dxw = _dot(bb, dstb, 1, 1)
            dxp = dxp + dxw * wp
            tw = xp * dxw
            cols = jnp.where(lane == 8 + k0, jnp.sum(jnp.where(lo, tw, 0.0), axis=-1, keepdims=True), cols)
            cols = jnp.where(lane == 9 + k0, jnp.sum(jnp.where(lo, 0.0, tw), axis=-1, keepdims=True), cols)
            dbm = dbm + _dot((xp * wp).astype(BF16), dstb, 1, 0)
            dsl = dsk[:, pr * LANES:(pr + 1) * LANES]
            dx_ref[:, xcols] = dxp + dsl * dyp
            dd_ref[gi, :, pr * LANES:(pr + 1) * LANES] += jnp.sum(dyp * xp, axis=0, keepdims=True)
            prod = dst * hp
            dcd_all.append(jnp.sum(jnp.sum(jnp.where(rowlo, prod, 0.0), axis=-1, keepdims=True), axis=0, keepdims=True))
            dcd_all.append(jnp.sum(jnp.sum(jnp.where(rowlo, 0.0, prod), axis=-1, keepdims=True), axis=0, keepdims=True))
            dec = jnp.where(rowlo, jnp.exp(al[0]), jnp.exp(al[1]))
            dh_ref[gi, pr * LANES:(pr + 1) * LANES, :] = dec * dst + dh_yoff
        dgb = dgm.astype(BF16)
        dc_ref[:, bcols] = dcm + _dot(dgb, bb, 1, 0)
        db_ref[:, bcols] = dbm + _dot(dgb, cb, 0, 0)
        colt = cols.T
        sub8 = lax.broadcasted_iota(jnp.int32, (8, LANES), 0)
        da_rows = jnp.zeros((8, LANES), F32)
        ddt_part = []
        for k in range(4):
            rs = colt[k:k + 1, :]
            uo = colt[4 + k:5 + k, :]
            dwl = colt[8 + k:9 + k, :]
            es = jnp.exp(al_all[k] - ar[k:k + 1, :])
            wrow = es * dr[k:k + 1, :]
            dwl_w = dwl * wrow
            da_k = rs - rows_q[k] + uo - dwl_w
            tail = jnp.sum(dwl_w, axis=-1, keepdims=True) + jnp.exp(al_all[k]) * dcd_all[k]
            da_k = da_k + jnp.where(lane_row == L - 1, tail, 0.0)
            da_rows = jnp.where(sub8 == k, da_k, da_rows)
            ddt_part.append(rows_ddt[k] + dwl * es)
        dda = _dot(da_rows, _tri(True), 1, 0, HI)
        for k in range(4):
            dda_k = dda[k:k + 1, :]
            ddt_ref[gi, k:k + 1, :] = ddt_part[k] + dda_k * ag[k:k + 1, :]
            da_ref[gi, k:k + 1, :] += dda_k * dr[k:k + 1, :] * ag[k:k + 1, :]

    return _pc(body, name, (SSM_GROUPS // n, nc),
               [sp["x"], sp["b"], sp["c"], sp["col"], sp["col"], sp["row"], sp["row"], sp["grp"], sp["grow"], sp["hs"], sp["x"]],
               [sp["x"], sp["bc"], sp["bc"], sp["row"], sp["grow"], sp["grp"]],
               [_sds((s, SSM_INNER)), _sds((s, 1024)), _sds((s, 1024)), _sds((SSM_GROUPS, 4, s)),
                _sds((SSM_GROUPS, 4, LANES)), _sds((SSM_GROUPS, 1, 256))],
               [pltpu.VMEM((n, 256, SSM_STATE), F32)])(xbc, xbc, xbc, dtc, acol, drow, arow, dskip, agrp, hs, dy)


def _gnorm_fwd(y, proj, nw, name):
    s = y.shape[0]
    ts = _row_tile(s)
    gw = SSM_INNER // SSM_GROUPS

    def body(y_ref, z_ref, w_ref, o_ref):
        for g in range(SSM_GROUPS):
            sl = slice(g * gw, (g + 1) * gw)
            z = z_ref[:, sl]
            y2 = y_ref[:, sl] * (z * _sigmoid(z))
            r = lax.rsqrt(jnp.mean(y2 * y2, axis=-1, keepdims=True) + RMS_EPS)
            o_ref[:, sl] = ((y2 * r) * w_ref[:, sl]).astype(BF16)

    row = pl.BlockSpec((ts, SSM_INNER), lambda i: (i, 0))
    return _pc(body, name, (s // ts,), [row, row, pl.BlockSpec((1, SSM_INNER), lambda i: (0, 0))], row,
               _sds((s, SSM_INNER), BF16))(y, proj, nw)


def _gnorm_bwd(y, proj, nw, dyn, name):
    s = y.shape[0]
    ts = _row_tile(s)
    gw = SSM_INNER // SSM_GROUPS

    def body(y_ref, z_ref, w_ref, d_ref, dy_ref, dz_ref, dw_ref):
        @pl.when(pl.program_id(0) == 0)
        def _():
            dw_ref[...] = jnp.zeros_like(dw_ref)

        for g in range(SSM_GROUPS):
            sl = slice(g * gw, (g + 1) * gw)
            z = z_ref[:, sl]
            yv = y_ref[:, sl]
            sg = _sigmoid(z)
            sz = z * sg
            y2 = yv * sz
            r = lax.rsqrt(jnp.mean(y2 * y2, axis=-1, keepdims=True) + RMS_EPS)
            yn = y2 * r
            dout = d_ref[:, sl]
            dyg = dout * w_ref[:, sl]
            dy2 = r * (dyg - yn * jnp.mean(dyg * yn, axis=-1, keepdims=True))
            dy_ref[:, sl] = dy2 * sz
            dz_ref[:, sl] = (dy2 * yv * (sg * (1.0 + z * (1.0 - sg)))).astype(BF16)
            dw_ref[:, sl] += jnp.sum(dout * yn, axis=0, keepdims=True)

    row = pl.BlockSpec((ts, SSM_INNER), lambda i: (i, 0))
    vec = pl.BlockSpec((1, SSM_INNER), lambda i: (0, 0))
    return _pc(body, name, (s // ts,), [row, row, vec, row], [row, row, vec],
               [_sds((s, SSM_INNER)), _sds((s, SSM_INNER), BF16), _sds((1, SSM_INNER))])(y, proj, nw, dyn)


def _head_layouts(v, s):
    return v.reshape(s, SSM_GROUPS, 4).transpose(1, 0, 2), v.T.reshape(SSM_GROUPS, 4, s)


def _ssd_fwd(x, h, w_in, conv_w, conv_b, dt_bias, a_log, d_skip, gnorm_w, w_out, next_norm):
    s = x.shape[0]
    proj = _mm(h, w_in, "nn", "ssd_in")
    xbc = _ssd_conv_fwd(proj, conv_w, conv_b, "ssd_conv")
    pad = ((0, 0), (0, LANES - SSM_HEADS))
    a_neg = -jnp.exp(a_log)
    bias = jnp.pad(dt_bias, pad)
    dt, acum = _ssd_dt_fwd(proj, bias, jnp.pad(a_neg, pad), "ssd_dt")
    dtc, drow = _head_layouts(dt[:, :SSM_HEADS], s)
    acol, arow = _head_layouts(acum[:, :SSM_HEADS], s)
    dskip = jnp.repeat(d_skip.reshape(SSM_GROUPS, 1, 4), HEAD_DIM, axis=2)
    y, hs = _ssd_scan_fwd(xbc, dtc, acol, drow, arow, dskip, "ssd_scan")
    yn = _gnorm_fwd(y, proj, gnorm_w, "ssd_gnorm")
    out, h_next = _mm(yn, w_out, "nn", "ssd_out", res=x, norm_w=next_norm)
    return out, h_next, (x, h, proj, xbc, bias, a_neg, dtc, acol, drow, arow, dskip, y, hs, yn)


def _ssd_bwd(dout, saved, norm_w, w_in, conv_w, conv_b, gnorm_w, w_out):
    x, h, proj, xbc, bias, a_neg, dtc, acol, drow, arow, dskip, y, hs, yn = saved
    s = x.shape[0]
    dyn = _mm(dout, w_out, "nt", "ssd_dyn")
    g_out = _mm(yn, dout, "tn", "ssd_gout", out_dtype=BF16)
    dy, dz, g_gnorm = _gnorm_bwd(y, proj, gnorm_w, dyn, "ssd_dgnorm")
    agrp = jnp.broadcast_to(a_neg.reshape(SSM_GROUPS, 4, 1), (SSM_GROUPS, 4, LANES))
    dxs, db, dc, ddt_row, da_acc, dd_acc = _ssd_scan_bwd(xbc, dtc, acol, drow, arow, dskip, agrp, hs, dy, "ssd_dscan")
    parts = [_ssd_conv_bwd(proj, d, col, conv_w, conv_b, f"ssd_dconv_{tag}") for d, col, tag in ((dxs, 0, "x"), (db, 2, "b"), (dc, 3, "c"))]
    g_cw = jnp.concatenate([p[1] for p in parts], axis=1)
    g_cb = jnp.concatenate([p[2] for p in parts], axis=1)
    ddt = jnp.pad(ddt_row.reshape(SSM_HEADS, s).T, ((0, 0), (0, LANES - SSM_HEADS)))
    ddtraw, g_dtb = _ssd_dt_bwd(proj, bias, ddt, "ssd_ddt")
    dproj = jnp.concatenate([dz] + [p[0] for p in parts] + [ddtraw], axis=1)
    g_in = _mm(h, dproj, "tn", "ssd_gin", out_dtype=BF16)
    dx, g_norm = _mm_dnorm_nt(dproj, w_in, "ssd_dh", x, norm_w, dout)
    g_alog = jnp.sum(da_acc, axis=-1).reshape(1, SSM_HEADS)
    g_d = jnp.sum(dd_acc.reshape(SSM_GROUPS, 4, HEAD_DIM), axis=-1).reshape(1, SSM_HEADS)
    return dx, g_norm, g_in[:, :SSM_IN], g_cw, g_cb, g_dtb[:, :SSM_HEADS], g_alog, g_d, g_gnorm, g_out


def _loss_head(y, target, name):
    s, d = y.shape
    ts = _row_tile(s)

    def body(y_ref, t_ref, dy_ref, l_ref):
        @pl.when(pl.program_id(0) == 0)
        def _():
            l_ref[...] = jnp.zeros_like(l_ref)

        e = y_ref[...] - t_ref[...]
        dy_ref[...] = e * (1.0 / d)
        part = jnp.sum(jnp.sum(e * e, axis=-1, keepdims=True), axis=0, keepdims=True) * (0.5 / d)
        l_ref[...] += jnp.broadcast_to(part, l_ref.shape)

    row = pl.BlockSpec((ts, d), lambda i: (i, 0))
    dy, lacc = _pc(body, name, (s // ts,), [row, row], [row, pl.BlockSpec((8, LANES), lambda i: (0, 0))],
                   [_sds((s, d)), _sds((8, LANES))])(y, target)
    return lacc[0, 0], dy


def _local_step(x, target, w, gather_layer0=None, gather_rest=None, scatter_first=None, scatter_layer0=None):
    saved = []
    received, received_layer0 = None, {}

    def layer0_stages():
        def entry(stage):
            shards, finish = gather_layer0[stage]

            def on_arrival(got):
                nonlocal w
                w = finish(w, got)
                return {"conv_gate": lambda: w["conv_w_out"][0], "ffn_gu": lambda: w["ffn_w_down"][0]}.get(stage, lambda: None)()
            return shards, on_arrival
        return {stage: entry(stage) for stage in gather_layer0}

    at = lambda weights, n: weights[n] if n < len(weights) else None
    h = _rms_fwd(x, w["mix_norm"][0:1], "first_norm")
    for i in range(DEPTH):
        kind, j = i % 3, i // 3
        fn = w["ffn_norm"][i:i + 1]
        stages = layer0_stages() if (i == 0 and gather_layer0 is not None) else None
        if kind == 0:
            x, h, sv = _conv_fwd(x, h, w["conv_w_in"][j], w["conv_w_dw"][j], at(w["conv_w_out"], j), str(i), fn, stages)
        elif kind == 1:
            hosted = None if gather_rest is None else (gather_rest[0], True)
            x, h, sv, got = _fox_fwd(x, h, w["fox_w_in"], w["fox_b_f"], w["fox_q_gain"], w["fox_k_gain"], w["fox_w_out"], fn, hosted)
            if gather_rest is not None:
                w = gather_rest[1](w, got)
        else:
            x, h, sv = _ssd_fwd(x, h, w["ssd_w_in"], w["ssd_conv_w"], w["ssd_conv_b"], w["ssd_dt_bias"],
                                w["ssd_a_log"], w["ssd_d"], w["ssd_norm_w"], w["ssd_w_out"], fn)
        nxt = w["mix_norm"][i + 1:i + 2] if i + 1 < DEPTH else None
        x, h, sf = _ffn_fwd(x, h, w["ffn_w_gu"][i], at(w["ffn_w_down"], i), str(i), nxt, stages)
        saved.append((sv, sf))
    loss, dx = _loss_head(x, target, "loss_head")
    g = {k: [None] * n for k, n in (("mix_norm", DEPTH), ("ffn_norm", DEPTH), ("ffn_w_gu", DEPTH), ("ffn_w_down", DEPTH),
                                    ("conv_w_in", 2), ("conv_w_dw", 2), ("conv_w_out", 2))}
    for i in reversed(range(DEPTH)):
        kind, j = i % 3, i // 3
        sv, sf = saved[i]
        hosted_fn = None
        if i == 0 and scatter_layer0 is not None:
            hosted_fn = lambda stage, **new: scatter_layer0(stage, g, **new)
        dx, g["ffn_norm"][i], g["ffn_w_gu"][i], g["ffn_w_down"][i], got = _ffn_bwd(
            dx, sf, w["ffn_norm"][i:i + 1], w["ffn_w_gu"][i], w["ffn_w_down"][i], str(i), hosted_fn)
        received_layer0.update(got)
        mn = w["mix_norm"][i:i + 1]
        if kind == 0:
            dx, g["mix_norm"][i], g["conv_w_in"][j], g["conv_w_dw"][j], g["conv_w_out"][j], got = _conv_bwd(
                dx, sv, mn, w["conv_w_in"][j], w["conv_w_dw"][j], w["conv_w_out"][j], str(i), hosted_fn)
            received_layer0.update(got)
        elif kind == 1:
            hosted = None if scatter_first is None else (scatter_first(g), False)
            (dx, g["mix_norm"][i], g["fox_w_in"], g["fox_b_f"], g["fox_q_gain"], g["fox_k_gain"],
             g["fox_w_out"], received) = _fox_bwd(dx, sv, mn, w["fox_w_in"], w["fox_w_out"], hosted)
        else:
            (dx, g["mix_norm"][i], g["ssd_w_in"], g["ssd_conv_w"], g["ssd_conv_b"], g["ssd_dt_bias"], g["ssd_a_log"],
             g["ssd_d"], g["ssd_norm_w"], g["ssd_w_out"]) = _ssd_bwd(
                 dx, sv, mn, w["ssd_w_in"], w["ssd_conv_w"], w["ssd_conv_b"], w["ssd_norm_w"], w["ssd_w_out"])
    g["mix_norm"] = jnp.concatenate(g["mix_norm"], axis=0)
    g["ffn_norm"] = jnp.concatenate(g["ffn_norm"], axis=0)
    g["conv_w_dw"] = jnp.stack(g["conv_w_dw"], axis=0)
    g["ssd_conv_w"] = g["ssd_conv_w"][None]
    return loss, dx, g, received, received_layer0


def _mesh_position():
    return lax.axis_index("x") * 4 + lax.axis_index("y") * 2 + lax.axis_index("c")


def _device_of(t):
    return (lax.shift_right_logical(t, 2), lax.bitwise_and(lax.shift_right_logical(t, 1), 1), lax.bitwise_and(t, 1))


def _exchange_copies(srcs_of, out_refs, send_sems, recv_sems, local_sems):
    me = _mesh_position()
    na = len(out_refs)
    locals_ = [pltpu.make_async_copy(srcs_of(a, me), out_refs[a].at[me], local_sems.at[a]) for a in range(na)]
    sends, arrivals = [], []
    for j in range(1, NDEV):
        t = lax.rem(me + j, NDEV)
        frm = lax.rem(me + NDEV - j, NDEV)
        for a in range(na):
            sends.append(pltpu.make_async_remote_copy(
                src_ref=srcs_of(a, t), dst_ref=out_refs[a].at[me], send_sem=send_sems.at[a, j - 1],
                recv_sem=recv_sems.at[a, j - 1], device_id=_device_of(t), device_id_type=pl.DeviceIdType.MESH))
            arrivals.append(pltpu.make_async_remote_copy(
                src_ref=srcs_of(a, me), dst_ref=out_refs[a].at[frm], send_sem=send_sems.at[a, j - 1],
                recv_sem=recv_sems.at[a, j - 1], device_id=_device_of(frm), device_id_type=pl.DeviceIdType.MESH))
    return locals_, sends, arrivals


def _exchange_start(copies):
    locals_, sends, _ = copies
    for cp in locals_ + sends:
        cp.start()


def _exchange_wait(copies):
    locals_, sends, arrivals = copies
    for cp in sends:
        cp.wait_send()
    for cp in arrivals:
        cp.wait_recv()
    for cp in locals_:
        cp.wait()


def _exchange_run(srcs_of, out_refs, send_sems, recv_sems, local_sems):
    copies = _exchange_copies(srcs_of, out_refs, send_sems, recv_sems, local_sems)
    _exchange_start(copies)
    _exchange_wait(copies)


def _exchange_parts(arrays, gather):
    na = len(arrays)
    outs = [_sds(((NDEV,) + a.shape) if gather else a.shape, a.dtype) for a in arrays]
    sems = [pltpu.SemaphoreType.DMA((na, NDEV - 1)), pltpu.SemaphoreType.DMA((na, NDEV - 1)), pltpu.SemaphoreType.DMA((na,))]
    pick = (lambda srcs: (lambda a, t: srcs[a])) if gather else (lambda srcs: (lambda a, t: srcs[a].at[t]))
    return pick, outs, sems


def _exchange(arrays, name, gather):
    na = len(arrays)
    pick, outs, sems = _exchange_parts(arrays, gather)

    def body(*refs):
        _exchange_run(pick(refs[:na]), refs[na:2 * na], *refs[2 * na:])

    hbm = pl.BlockSpec(memory_space=pl.ANY)
    return pl.pallas_call(body, name=name, in_specs=[hbm] * na, out_specs=[hbm] * na, out_shape=outs, scratch_shapes=sems)(*arrays)


def _all_sum_small(pack, name):
    def body(src_ref, out_ref, buf_ref, send_sems, recv_sems, local_sems):
        _exchange_run(lambda a, t: src_ref, [buf_ref], send_sems, recv_sems, local_sems)
        acc = buf_ref[0]
        for d in range(1, NDEV):
            acc = acc + buf_ref[d]
        out_ref[...] = acc

    vmem = pl.BlockSpec(memory_space=pltpu.VMEM)
    return pl.pallas_call(
        body, name=name, in_specs=[vmem], out_specs=vmem, out_shape=_sds(pack.shape, pack.dtype),
        scratch_shapes=[pltpu.VMEM((NDEV,) + pack.shape, pack.dtype), pltpu.SemaphoreType.DMA((1, NDEV - 1)),
                        pltpu.SemaphoreType.DMA((1, NDEV - 1)), pltpu.SemaphoreType.DMA((1,))])(pack)


def _sum_slabs(slabs, name):
    _, r, c = slabs.shape
    tr = r
    for cand in (256, 352):
        if r % cand == 0:
            tr = cand
            break

    def body(s_ref, o_ref):
        acc = s_ref[0].astype(F32)
        for d in range(1, NDEV):
            acc = acc + s_ref[d].astype(F32)
        o_ref[...] = acc

    return _pc(body, name, (r // tr,), [pl.BlockSpec((NDEV, tr, c), lambda i: (0, i, 0))],
               pl.BlockSpec((tr, c), lambda i: (i, 0)), _sds((r, c)))(slabs)


def _adamw(wt, g, m, v, name):
    shape = wt.shape
    w2, g2, m2, v2 = (a.reshape(-1, shape[-1]) for a in (wt, g, m, v))
    r, c = w2.shape
    tr = r
    for cand in (512, 352, 256):
        if r % cand == 0:
            tr = cand
            break
    c1 = 1.0 - ADAM_B1 ** ADAM_STEP
    c2 = 1.0 - ADAM_B2 ** ADAM_STEP

    def body(w_ref, g_ref, m_ref, v_ref, d_ref, mo_ref, vo_ref):
        gv = g_ref[...]
        mn = ADAM_B1 * m_ref[...] + (1.0 - ADAM_B1) * gv
        vn = ADAM_B2 * v_ref[...] + (1.0 - ADAM_B2) * (gv * gv)
        mo_ref[...] = mn
        vo_ref[...] = vn
        d_ref[...] = -ADAM_LR * ((mn / c1) / (jnp.sqrt(vn / c2) + ADAM_EPS) + ADAM_WD * w_ref[...])

    spec = pl.BlockSpec((tr, c), lambda i: (i, 0))
    outs = _pc(body, name, (r // tr,), [spec] * 4, [spec] * 3, [_sds((r, c))] * 3)(w2, g2, m2, v2)
    return tuple(o.reshape(shape) for o in outs)


_NAMES = ["mix_norm", "ffn_norm", "ffn_w_gu", "ffn_w_down", "conv_w_in", "conv_w_dw", "conv_w_out", "fox_w_in", "fox_b_f",
          "fox_q_gain", "fox_k_gain", "fox_w_out", "ssd_w_in", "ssd_conv_w", "ssd_conv_b", "ssd_dt_bias", "ssd_a_log",
          "ssd_d", "ssd_norm_w", "ssd_w_out"]
_MATRICES = ["ffn_w_gu", "ffn_w_down", "conv_w_in", "conv_w_out", "fox_w_in", "fox_w_out", "ssd_w_in", "ssd_w_out"]
_VECTORS = {"conv_w_dw": 2, "ssd_conv_w": 2, "ssd_conv_b": 1, "ssd_norm_w": 1}
_REPLICATED = ["mix_norm", "ffn_norm", "fox_b_f", "fox_q_gain", "fox_k_gain", "ssd_dt_bias", "ssd_a_log", "ssd_d"]


def _to_rows(flat):
    n = flat.shape[0]
    rows = -(-n // (8 * D_MODEL)) * 8
    return jnp.pad(flat, (0, rows * D_MODEL - n)).reshape(rows, D_MODEL)


def _full_shape(local_shape, axis):
    shp = list(local_shape)
    shp[axis] *= NDEV
    return tuple(shp)


def _cols_from_blocks(g):
    return jnp.moveaxis(g, 0, 1).reshape(g.shape[1], NDEV * g.shape[2])


def _blocks_from_cols(full):
    k, n8 = full.shape
    return jnp.moveaxis(full.reshape(k, NDEV, n8 // NDEV), 1, 0)


def kernel(x, mix_norm, ffn_norm, ffn_w_gu, ffn_w_down, conv_w_in, conv_w_dw, conv_w_out, fox_w_in, fox_b_f, fox_q_gain, fox_k_gain, fox_w_out, ssd_w_in, ssd_conv_w, ssd_conv_b, ssd_dt_bias, ssd_a_log, ssd_d, ssd_norm_w, ssd_w_out, loss_target, m_mix_norm, m_ffn_norm, m_ffn_w_gu, m_ffn_w_down, m_conv_w_in, m_conv_w_dw, m_conv_w_out, m_fox_w_in, m_fox_b_f, m_fox_q_gain, m_fox_k_gain, m_fox_w_out, m_ssd_w_in, m_ssd_conv_w, m_ssd_conv_b, m_ssd_dt_bias, m_ssd_a_log, m_ssd_d, m_ssd_norm_w, m_ssd_w_out, v_mix_norm, v_ffn_norm, v_ffn_w_gu, v_ffn_w_down, v_conv_w_in, v_conv_w_dw, v_conv_w_out, v_fox_w_in, v_fox_b_f, v_fox_q_gain, v_fox_k_gain, v_fox_w_out, v_ssd_w_in, v_ssd_conv_w, v_ssd_conv_b, v_ssd_dt_bias, v_ssd_a_log, v_ssd_d, v_ssd_norm_w, v_ssd_w_out):
    local = dict(mix_norm=mix_norm, ffn_norm=ffn_norm, ffn_w_gu=ffn_w_gu, ffn_w_down=ffn_w_down, conv_w_in=conv_w_in,
                 conv_w_dw=conv_w_dw, conv_w_out=conv_w_out, fox_w_in=fox_w_in, fox_b_f=fox_b_f, fox_q_gain=fox_q_gain,
                 fox_k_gain=fox_k_gain, fox_w_out=fox_w_out, ssd_w_in=ssd_w_in, ssd_conv_w=ssd_conv_w, ssd_conv_b=ssd_conv_b,
                 ssd_dt_bias=ssd_dt_bias, ssd_a_log=ssd_a_log, ssd_d=ssd_d, ssd_norm_w=ssd_norm_w, ssd_w_out=ssd_w_out)
    mom = dict(zip(_NAMES, [m_mix_norm, m_ffn_norm, m_ffn_w_gu, m_ffn_w_down, m_conv_w_in, m_conv_w_dw, m_conv_w_out, m_fox_w_in,
                            m_fox_b_f, m_fox_q_gain, m_fox_k_gain, m_fox_w_out, m_ssd_w_in, m_ssd_conv_w, m_ssd_conv_b,
                            m_ssd_dt_bias, m_ssd_a_log, m_ssd_d, m_ssd_norm_w, m_ssd_w_out]))
    var = dict(zip(_NAMES, [v_mix_norm, v_ffn_norm, v_ffn_w_gu, v_ffn_w_down, v_conv_w_in, v_conv_w_dw, v_conv_w_out, v_fox_w_in,
                            v_fox_b_f, v_fox_q_gain, v_fox_k_gain, v_fox_w_out, v_ssd_w_in, v_ssd_conv_w, v_ssd_conv_b,
                            v_ssd_dt_bias, v_ssd_a_log, v_ssd_d, v_ssd_norm_w, v_ssd_w_out]))

    shard = {k: local[k].astype(BF16) for k in _MATRICES}
    vec_pack = _to_rows(jnp.concatenate([local[k].reshape(-1) for k in _VECTORS]))
    first = _exchange([shard["conv_w_in"][0:1], vec_pack], "gather_first", True)
    gvec = first[1].reshape(NDEV, -1)
    full = {k: local[k] for k in _REPLICATED}
    off = 0
    for k, axis in _VECTORS.items():
        n = local[k].size
        blk = jnp.moveaxis(gvec[:, off:off + n].reshape((NDEV,) + local[k].shape), 0, axis)
        full[k] = blk.reshape(_full_shape(local[k].shape, axis))
        off += n
    full["ssd_conv_w"] = full["ssd_conv_w"][0]
    full["conv_w_in"] = [first[0][:, 0]]
    full["conv_w_out"], full["ffn_w_gu"], full["ffn_w_down"] = [], [], []

    def finish_gu0(w, got):
        return dict(w, ffn_w_gu=[got[0][:, 0]])

    def finish_out0(w, got):
        return dict(w, conv_w_out=[got[0][:, 0].reshape(D_MODEL, D_MODEL)])

    def finish_down0(w, got):
        return dict(w, ffn_w_down=[got[0][:, 0].reshape(4, FF_BLOCK, D_MODEL)], fox_w_out=got[1].reshape(D_MODEL, D_MODEL))

    def finish_fox(w, got):
        return dict(w, fox_w_in=jnp.pad(_cols_from_blocks(got[0][:, 0]), ((0, 0), (0, FOX_IN_PAD - FOX_IN))))

    layer0 = {"conv_in": ([shard["ffn_w_gu"][0:1]], finish_gu0), "conv_gate": ([shard["conv_w_out"][0:1]], finish_out0),
              "ffn_gu": ([shard["ffn_w_down"][0:1], shard["fox_w_out"]], finish_down0),
              "ffn_down": ([shard["fox_w_in"]], finish_fox)}

    rest = [shard["ffn_w_gu"][1:], shard["ffn_w_down"][1:], shard["conv_w_in"][1:], shard["conv_w_out"][1:],
            shard["ssd_w_in"], shard["ssd_w_out"]]

    def finish(w, got):
        w = dict(w)
        w["ffn_w_gu"] = w["ffn_w_gu"] + [got[0][:, i] for i in range(DEPTH - 1)]
        w["ffn_w_down"] = w["ffn_w_down"] + [got[1][:, i].reshape(4, FF_BLOCK, D_MODEL) for i in range(DEPTH - 1)]
        w["conv_w_in"] = w["conv_w_in"] + [got[2][:, 0]]
        w["conv_w_out"] = w["conv_w_out"] + [got[3][:, 0].reshape(D_MODEL, D_MODEL)]
        w["ssd_w_in"] = jnp.pad(_cols_from_blocks(got[4][:, 0]), ((0, 0), (0, SSM_IN_PAD - SSM_IN)))
        w["ssd_w_out"] = got[5].reshape(SSM_INNER, D_MODEL)
        return w

    def early_slabs(g):
        return ([g["ffn_w_gu"][i] for i in range(1, DEPTH)]
                + [g["ffn_w_down"][i].reshape(NDEV, D_FF // NDEV, D_MODEL) for i in range(1, DEPTH)]
                + [g["conv_w_in"][1], g["conv_w_out"][1].reshape(NDEV, D_MODEL // NDEV, D_MODEL),
                   _blocks_from_cols(g["ssd_w_in"]), g["ssd_w_out"].reshape(NDEV, SSM_INNER // NDEV, D_MODEL)])

    def layer0_slabs(stage, g, g_down=None, g_gu=None, g_in=None, g_out=None):
        if stage == "ffn_gdown":
            return [_blocks_from_cols(g["fox_w_in"])]
        if stage == "ffn_dgu":
            return [g_down.reshape(NDEV, D_FF // NDEV, D_MODEL), g["fox_w_out"].reshape(NDEV, D_MODEL // NDEV, D_MODEL)]
        if stage == "ffn_dh":
            return [g_gu]
        if stage == "conv_dh":
            return [g_in, g_out.reshape(NDEV, D_MODEL // NDEV, D_MODEL)]
        return None

    loss_part, dx, grads, early, late = _local_step(x[0], loss_target[0], full, layer0, (rest, finish), early_slabs, layer0_slabs)

    se = [_sum_slabs(r, f"sum_early_{n}") for n, r in enumerate(early)]
    sl = {stage: [_sum_slabs(r, f"sum_{stage}_{n}") for n, r in enumerate(rs)] for stage, rs in late.items()}
    shard_grad = {
        "ffn_w_gu": jnp.stack(sl["ffn_dh"] + se[0:3]), "ffn_w_down": jnp.stack(sl["ffn_dgu"][0:1] + se[3:6]),
        "conv_w_in": jnp.stack([sl["conv_dh"][0], se[6]]), "conv_w_out": jnp.stack([sl["conv_dh"][1], se[7]]),
        "fox_w_in": sl["ffn_gdown"][0][None], "fox_w_out": sl["ffn_dgu"][1][None],
        "ssd_w_in": se[8][None], "ssd_w_out": se[9][None]}

    small_names = _REPLICATED + list(_VECTORS)
    small = [jnp.reshape(loss_part, (1,))] + [grads[k].reshape(-1) for k in small_names]
    total = _all_sum_small(_to_rows(jnp.concatenate(small)), "sum_small").reshape(-1)
    loss = total[0]
    off = 1
    me = _mesh_position()
    for k, part in zip(small_names, small[1:]):
        gk = total[off:off + part.shape[0]]
        off += part.shape[0]
        if k in _VECTORS:
            axis = _VECTORS[k]
            shp = local[k].shape
            gfull = gk.reshape(shp[:axis] + (NDEV, shp[axis]) + shp[axis + 1:])
            shard_grad[k] = lax.dynamic_index_in_dim(gfull, me, axis, keepdims=False)
        else:
            shard_grad[k] = gk.reshape(local[k].shape)

    deltas, new_m, new_v = {}, {}, {}
    for k in _NAMES:
        deltas[k], new_m[k], new_v[k] = _adamw(local[k], shard_grad[k], mom[k], var[k], f"adamw_{k}")
    return (loss, dx[None], *[shard_grad[k] for k in _NAMES], *[deltas[k] for k in _NAMES],
            *[new_m[k] for k in _NAMES], *[new_v[k] for k in _NAMES])
```

```python
import numpy as np

import jax
import jax.numpy as jnp
from jax import lax
from jax.experimental import pallas as pl
from jax.experimental.pallas import tpu as pltpu

F32 = jnp.float32
BF16 = jnp.bfloat16
HI = lax.Precision.HIGHEST

NDEV = 8
D_MODEL = 1024
DEPTH = 4
D_FF = 2816
FF_BLOCK = 2 * D_FF // NDEV
RMS_EPS = 1e-6
HEAD_DIM = 64
ATTN_HEADS = 16
FOX_IN = 3 * D_MODEL + ATTN_HEADS
FOX_IN_PAD = 3200
SSM_INNER = 2048
SSM_HEADS = 32
SSM_GROUPS = 8
SSM_STATE = 128
SSM_CHUNK = 128
SSM_CONV_DIM = 4096
SSM_IN = SSM_INNER + SSM_CONV_DIM + SSM_HEADS
SSM_IN_PAD = 6272
LANES = 128
V7X_VMEM_BYTES = 64 * 1024 * 1024
VMEM_LIMIT_BYTES = (V7X_VMEM_BYTES * 3) // 4
ATTN_BWD_VMEM_BYTES = (V7X_VMEM_BYTES * 7) // 8
LOG2E = 1.4426950408889634
LN2 = 0.6931471805599453
ATTN_TILE = 1024
ATTN_ROWS = 32
CUMSUM_ROWS = 512

ADAM_LR = 0.001
ADAM_B1 = 0.9
ADAM_B2 = 0.999
ADAM_EPS = 1e-08
ADAM_WD = 0.01
ADAM_STEP = 10

_TILE_CANDIDATES = (1024, 1408, 896, 768, 640, 512, 384, 256, 128)


def _pick_tile(n):
    for c in _TILE_CANDIDATES:
        if n % c == 0:
            return c
    raise ValueError(f"no tile for {n}")


def _params(ngrid):
    return pltpu.CompilerParams(dimension_semantics=("arbitrary",) * ngrid, vmem_limit_bytes=VMEM_LIMIT_BYTES)


def _pc(body, name, grid, in_specs, out_specs, out_shape, scratch=(), hosted=None):
    if hosted is None:
        return pl.pallas_call(
            body, name=name, grid=grid, in_specs=in_specs, out_specs=out_specs, out_shape=out_shape,
            scratch_shapes=list(scratch), compiler_params=_params(len(grid)))
    arrays, gather = hosted
    single = not isinstance(out_shape, (list, tuple))
    outs = [out_shape] if single else list(out_shape)
    ospecs = [out_specs] if single else list(out_specs)
    na, n_in, n_out, n_scr = len(arrays), len(in_specs), len(outs), len(scratch)
    pick, xouts, sems = _exchange_parts(arrays, gather)

    def run(*refs):
        ins, srcs = refs[:n_in], refs[n_in:n_in + na]
        res, dsts = refs[n_in + na:n_in + na + n_out], refs[n_in + na + n_out:n_in + 2 * na + n_out]
        scr, xsems = refs[n_in + 2 * na + n_out:n_in + 2 * na + n_out + n_scr], refs[n_in + 2 * na + n_out + n_scr:]
        first = pl.program_id(0) == 0
        last = pl.program_id(0) == grid[0] - 1
        for d in range(1, len(grid)):
            first = jnp.logical_and(first, pl.program_id(d) == 0)
            last = jnp.logical_and(last, pl.program_id(d) == grid[d] - 1)

        @pl.when(first)
        def _():
            _exchange_start(_exchange_copies(pick(srcs), dsts, *xsems))

        body(*ins, *res, *scr)

        @pl.when(last)
        def _():
            _exchange_wait(_exchange_copies(pick(srcs), dsts, *xsems))

    hbm = pl.BlockSpec(memory_space=pl.ANY)
    call = pl.pallas_call(
        run, name=name, grid=grid, in_specs=list(in_specs) + [hbm] * na, out_specs=ospecs + [hbm] * na,
        out_shape=outs + xouts, scratch_shapes=list(scratch) + sems, compiler_params=_params(len(grid)))
    return lambda *args: call(*args, *arrays)


def _dot(a, b, ca, cb, prec=None):
    return lax.dot_general(a, b, (((ca,), (cb,)), ((), ())), preferred_element_type=F32, precision=prec)


def _sds(shape, dtype=F32):
    return jax.ShapeDtypeStruct(shape, dtype)


def _row_tile(s, want=256):
    return want if s % want == 0 else s


def _sigmoid(x):
    return 1.0 / (1.0 + jnp.exp(-x))


def _softplus(x):
    return jnp.maximum(x, 0.0) + jnp.log(1.0 + jnp.exp(-jnp.abs(x)))


def _mm_spec(a, b, name, grid, a_spec, b_spec, o_spec, out, ca, cb, acc_shape, drop=(0, 0, 0), res=None, r_spec=None,
             norm_w=None, hosted=None):
    nk = grid[2]
    da, db, do_ = drop
    has_res = res is not None
    has_norm = norm_w is not None

    def body(*refs):
        refs = list(refs)
        a_ref, b_ref = refs[:2]
        r_ref = refs[2] if has_res else None
        w_ref = refs[2 + has_res] if has_norm else None
        o_ref = refs[2 + has_res + has_norm]
        h_ref = refs[3 + has_res + has_norm] if has_norm else None
        acc_ref = refs[-1]
        k = pl.program_id(2)

        @pl.when(k == 0)
        def _():
            acc_ref[...] = jnp.zeros_like(acc_ref)

        av = a_ref[(0,) * da] if da else a_ref[...]
        bv = b_ref[(0,) * db] if db else b_ref[...]
        acc_ref[...] += _dot(av.astype(BF16), bv.astype(BF16), ca, cb)

        @pl.when(k == nk - 1)
        def _():
            val = acc_ref[...]
            if has_res:
                val = val + r_ref[...]
            if do_:
                o_ref[(0,) * do_] = val.astype(out.dtype)
            else:
                o_ref[...] = val.astype(out.dtype)
            if has_norm:
                r = lax.rsqrt(jnp.mean(val * val, axis=-1, keepdims=True) + RMS_EPS)
                h_ref[...] = ((val * r) * w_ref[...]).astype(BF16)

    in_specs = [a_spec, b_spec] + ([r_spec] if has_res else [])
    args = (a, b) + ((res,) if has_res else ())
    out_specs, outs = o_spec, out
    if has_norm:
        assert acc_shape[1] == norm_w.shape[1] == out.shape[-1]
        in_specs.append(pl.BlockSpec((1, acc_shape[1]), lambda i, j, k: (0, 0)))
        args += (norm_w,)
        out_specs, outs = [o_spec, o_spec], [out, _sds(out.shape, BF16)]
    return _pc(body, name, grid, in_specs, out_specs, outs, [pltpu.VMEM(acc_shape, F32)], hosted=hosted)(*args)


def _mm(a, b, mode, name, out_dtype=F32, res=None, norm_w=None, hosted=None):
    if mode == "tn":
        r, m = a.shape
        n = b.shape[1]
        tm, tn, tk = _pick_tile(m), _pick_tile(n), _pick_tile(r)
        grid = (m // tm, n // tn, r // tk)
        a_spec = pl.BlockSpec((tk, tm), lambda i, j, k: (k, i))
        b_spec = pl.BlockSpec((tk, tn), lambda i, j, k: (k, j))
        ca, cb = 0, 0
    else:
        m, kd = a.shape
        n = b.shape[1] if mode == "nn" else b.shape[0]
        tm, tn, tk = _pick_tile(m), _pick_tile(n), _pick_tile(kd)
        grid = (m // tm, n // tn, kd // tk)
        a_spec = pl.BlockSpec((tm, tk), lambda i, j, k: (i, k))
        if mode == "nn":
            b_spec = pl.BlockSpec((tk, tn), lambda i, j, k: (k, j))
            ca, cb = 1, 0
        else:
            b_spec = pl.BlockSpec((tn, tk), lambda i, j, k: (j, k))
            ca, cb = 1, 1
    o_spec = pl.BlockSpec((tm, tn), lambda i, j, k: (i, j))
    return _mm_spec(a, b, name, grid, a_spec, b_spec, o_spec, _sds((m, n), out_dtype), ca, cb, (tm, tn), res=res, r_spec=o_spec,
                    norm_w=norm_w, hosted=hosted)


def _rms_fwd(x, w, name):
    s, d = x.shape
    ts = _row_tile(s)

    def body(x_ref, w_ref, o_ref):
        xv = x_ref[...]
        r = lax.rsqrt(jnp.mean(xv * xv, axis=-1, keepdims=True) + RMS_EPS)
        o_ref[...] = ((xv * r) * w_ref[...]).astype(BF16)

    row = pl.BlockSpec((ts, d), lambda i: (i, 0))
    return _pc(body, name, (s // ts,), [row, pl.BlockSpec((1, d), lambda i: (0, 0))], row, _sds((s, d), BF16))(x, w)


def _mm_dnorm(a, b, name, nk, a_spec, b_spec, ca, cb, drop, x, w, dres, hosted=None):
    s, d = x.shape
    tm = _pick_tile(s)
    da, db = drop

    def body(a_ref, b_ref, x_ref, w_ref, r_ref, dx_ref, dw_ref, acc_ref):
        i = pl.program_id(0)
        k = pl.program_id(2)

        @pl.when(k == 0)
        def _():
            acc_ref[...] = jnp.zeros_like(acc_ref)

        av = a_ref[(0,) * da] if da else a_ref[...]
        bv = b_ref[(0,) * db] if db else b_ref[...]
        acc_ref[...] += _dot(av.astype(BF16), bv.astype(BF16), ca, cb)

        @pl.when(k == nk - 1)
        def _():
            dhv = acc_ref[...]
            xv = x_ref[...]
            r = lax.rsqrt(jnp.mean(xv * xv, axis=-1, keepdims=True) + RMS_EPS)
            xhat = xv * r
            g = dhv * w_ref[...]
            dx_ref[...] = r_ref[...] + r * (g - xhat * jnp.mean(g * xhat, axis=-1, keepdims=True))
            part = jnp.sum(dhv * xhat, axis=0, keepdims=True)

            @pl.when(i == 0)
            def _():
                dw_ref[...] = part

            @pl.when(i > 0)
            def _():
                dw_ref[...] += part

    row = pl.BlockSpec((tm, d), lambda i, j, k: (i, 0))
    vec = pl.BlockSpec((1, d), lambda i, j, k: (0, 0))
    return list(_pc(body, name, (s // tm, 1, nk), [a_spec, b_spec, row, vec, row], [row, vec], [_sds((s, d)), _sds((1, d))],
                    [pltpu.VMEM((tm, d), F32)], hosted=hosted)(a, b, x, w, dres))


def _mm_dnorm_nt(dproj, w_in, name, x, w, dres, hosted=None):
    tm = _pick_tile(x.shape[0])
    tk = _pick_tile(dproj.shape[1])
    return _mm_dnorm(dproj, w_in, name, dproj.shape[1] // tk, pl.BlockSpec((tm, tk), lambda i, j, k: (i, k)),
                     pl.BlockSpec((D_MODEL, tk), lambda i, j, k: (0, k)), 1, 1, (0, 0), x, w, dres, hosted=hosted)


def _ffn_gate_up(h, w_gu, name, hosted=None):
    s = h.shape[0]
    tm = _pick_tile(s)

    def body(h_ref, wg_ref, wu_ref, gu_ref, a_ref):
        hv = h_ref[...]
        g = _dot(hv, wg_ref[0], 1, 0)
        u = _dot(hv, wu_ref[0], 1, 0)
        gu_ref[0, 0] = g.astype(BF16)
        gu_ref[0, 1] = u.astype(BF16)
        a_ref[0] = (g * _sigmoid(g) * u).astype(BF16)

    wblk = lambda off: pl.BlockSpec((1, D_MODEL, FF_BLOCK), lambda i, k: (k + off, 0, 0))
    return _pc(body, name, (s // tm, 4), [pl.BlockSpec((tm, D_MODEL), lambda i, k: (i, 0)), wblk(0), wblk(4)],
               [pl.BlockSpec((1, 2, tm, FF_BLOCK), lambda i, k: (k, 0, i, 0)), pl.BlockSpec((1, tm, FF_BLOCK), lambda i, k: (k, i, 0))],
               [_sds((4, 2, s, FF_BLOCK), BF16), _sds((4, s, FF_BLOCK), BF16)], hosted=hosted)(h, w_gu, w_gu)


def _stage_gather(stages, stage):
    return (stages[stage][0], True) if stages and stage in stages else None


def _stage_arrived(stages, stage, got, default=None):
    late = stages[stage][1](got) if stages and stage in stages else None
    return default if late is None else late


def _stage_slabs(hosted_fn, stage, **new):
    arrays = hosted_fn(stage, **new) if hosted_fn is not None else None
    return None if arrays is None else (arrays, False)


def _ffn_dgate_up(dy, w_down, gu, name, hosted=None):
    s = dy.shape[0]
    tm = _pick_tile(s)

    def body(dy_ref, w_ref, gu_ref, o_ref):
        dav = _dot(dy_ref[...].astype(BF16), w_ref[0], 1, 1)
        g = gu_ref[0, 0].astype(F32)
        u = gu_ref[0, 1].astype(F32)
        sg = _sigmoid(g)
        o_ref[0, 0] = (dav * u * (sg * (1.0 + g * (1.0 - sg)))).astype(BF16)
        o_ref[0, 1] = (dav * (g * sg)).astype(BF16)

    pair = pl.BlockSpec((1, 2, tm, FF_BLOCK), lambda i, k: (k, 0, i, 0))
    return _pc(body, name, (s // tm, 4),
               [pl.BlockSpec((tm, D_MODEL), lambda i, k: (i, 0)), pl.BlockSpec((1, FF_BLOCK, D_MODEL), lambda i, k: (k, 0, 0)), pair],
               pair, _sds((4, 2, s, FF_BLOCK), BF16), hosted=hosted)(dy, w_down, gu)


def _ffn_fwd(x, h, w_gu, w_down, tag, next_norm=None, stages=None):
    s = x.shape[0]
    tm = _pick_tile(s)
    gu, a, *got = _ffn_gate_up(h, w_gu, f"ffn_gu_{tag}", _stage_gather(stages, "ffn_gu"))
    w_down = _stage_arrived(stages, "ffn_gu", got, w_down)
    xspec = pl.BlockSpec((tm, D_MODEL), lambda i, j, k: (i, 0))
    hosted = _stage_gather(stages, "ffn_down")
    y = _mm_spec(a, w_down, f"ffn_down_{tag}", (s // tm, 1, 4),
                 pl.BlockSpec((1, tm, FF_BLOCK), lambda i, j, k: (k, i, 0)),
                 pl.BlockSpec((1, FF_BLOCK, D_MODEL), lambda i, j, k: (k, 0, 0)),
                 xspec, _sds((s, D_MODEL)), 1, 0, (tm, D_MODEL), drop=(1, 1, 0), res=x, r_spec=xspec, norm_w=next_norm,
                 hosted=hosted)
    n_own = 2 if next_norm is not None else 1
    own = list(y[:n_own]) if (hosted is not None or next_norm is not None) else [y]
    if hosted is not None:
        _stage_arrived(stages, "ffn_down", list(y[n_own:]))
    y, h_next = own if next_norm is not None else (own[0], None)
    return y, h_next, (x, h, gu, a)


def _ffn_bwd(dy, saved, norm_w, w_gu, w_down, tag, hosted_fn=None):
    x, h, gu, a = saved
    s = x.shape[0]
    tm = _pick_tile(s)
    got = {}
    hosted = _stage_slabs(hosted_fn, "ffn_gdown")
    g_down = _mm_spec(a, dy, f"ffn_gdown_{tag}", (4, 1, s // tm),
                      pl.BlockSpec((1, tm, FF_BLOCK), lambda i, j, k: (i, k, 0)),
                      pl.BlockSpec((tm, D_MODEL), lambda i, j, k: (k, 0)),
                      pl.BlockSpec((1, FF_BLOCK, D_MODEL), lambda i, j, k: (i, 0, 0)),
                      _sds((4, FF_BLOCK, D_MODEL), BF16), 0, 0, (FF_BLOCK, D_MODEL), drop=(1, 0, 1), hosted=hosted)
    if hosted is not None:
        g_down, *got["ffn_gdown"] = g_down
    hosted = _stage_slabs(hosted_fn, "ffn_dgu", g_down=g_down)
    dgu = _ffn_dgate_up(dy, w_down, gu, f"ffn_dgu_{tag}", hosted)
    if hosted is not None:
        dgu, *got["ffn_dgu"] = dgu
    g_gu = _mm_spec(h, dgu, f"ffn_ggu_{tag}", (NDEV, 1, s // tm),
                    pl.BlockSpec((tm, D_MODEL), lambda i, j, k: (k, 0)),
                    pl.BlockSpec((1, 1, tm, FF_BLOCK), lambda i, j, k: (i % 4, i // 4, k, 0)),
                    pl.BlockSpec((1, D_MODEL, FF_BLOCK), lambda i, j, k: (i, 0, 0)),
                    _sds((NDEV, D_MODEL, FF_BLOCK), BF16), 0, 0, (D_MODEL, FF_BLOCK), drop=(0, 2, 1))
    hosted = _stage_slabs(hosted_fn, "ffn_dh", g_gu=g_gu)
    dx, g_norm, *arrived = _mm_dnorm(dgu, w_gu, f"ffn_dh_{tag}", NDEV,
                                     pl.BlockSpec((1, 1, tm, FF_BLOCK), lambda i, j, k: (k % 4, k // 4, i, 0)),
                                     pl.BlockSpec((1, D_MODEL, FF_BLOCK), lambda i, j, k: (k, 0, 0)), 1, 1, (2, 1), x, norm_w, dy,
                                     hosted=hosted)
    if hosted is not None:
        got["ffn_dh"] = arrived
    return dx, g_norm, g_gu, g_down, got


def _prev_rows(cur, halo, j, first):
    rid = lax.broadcasted_iota(jnp.int32, cur.shape, 0)
    hid = lax.broadcasted_iota(jnp.int32, halo.shape, 0)
    out = pltpu.roll(cur, j, 0)
    for t in range(j):
        row = jnp.sum(jnp.where(hid == 8 - j + t, halo, 0.0), axis=0, keepdims=True)
        row = jnp.where(first, 0.0, row)
        out = jnp.where(rid == t, row, out)
    return out


def _next_rows(cur, halo, j, last):
    ts = cur.shape[0]
    rid = lax.broadcasted_iota(jnp.int32, cur.shape, 0)
    hid = lax.broadcasted_iota(jnp.int32, halo.shape, 0)
    out = pltpu.roll(cur, ts - j, 0)
    for t in range(j):
        row = jnp.sum(jnp.where(hid == t, halo, 0.0), axis=0, keepdims=True)
        row = jnp.where(last, 0.0, row)
        out = jnp.where(rid == ts - j + t, row, out)
    return out


def _halo_specs(ts, s, width, col):
    per = ts // 8
    nblk = s // 8
    prev = pl.BlockSpec((8, width), lambda i: (jnp.maximum(i * per - 1, 0), col))
    nxt = pl.BlockSpec((8, width), lambda i: (jnp.minimum((i + 1) * per, nblk - 1), col))
    return prev, nxt


def _cgate_fwd(p, w_dw, name, hosted=None):
    s = p.shape[0]
    d = D_MODEL
    ts = _row_tile(s)
    prev, _ = _halo_specs(ts, s, 3 * d, 0)

    def body(p_ref, h_ref, w_ref, z_ref):
        first = pl.program_id(0) == 0
        b = p_ref[:, :d]
        cv = p_ref[:, d:2 * d] * p_ref[:, 2 * d:]
        hcv = h_ref[:, d:2 * d] * h_ref[:, 2 * d:]
        u = w_ref[2:3, :] * cv + w_ref[1:2, :] * _prev_rows(cv, hcv, 1, first) + w_ref[0:1, :] * _prev_rows(cv, hcv, 2, first)
        z_ref[...] = (b * u).astype(BF16)

    return _pc(body, name, (s // ts,),
               [pl.BlockSpec((ts, 3 * d), lambda i: (i, 0)), prev, pl.BlockSpec((3, d), lambda i: (0, 0))],
               pl.BlockSpec((ts, d), lambda i: (i, 0)), _sds((s, d), BF16), hosted=hosted)(p, p, w_dw)


def _cgate_bwd(p, dz, w_dw, name):
    s = p.shape[0]
    d = D_MODEL
    ts = _row_tile(s)
    nt = s // ts
    p_prev, p_next = _halo_specs(ts, s, 3 * d, 0)
    _, dz_next = _halo_specs(ts, s, d, 0)

    def body(p_ref, hp_ref, hn_ref, dz_ref, dzn_ref, w_ref, dp_ref, dw_ref):
        i = pl.program_id(0)
        first = i == 0
        last = i == nt - 1
        b = p_ref[:, :d]
        c = p_ref[:, d:2 * d]
        v = p_ref[:, 2 * d:]
        cv = c * v
        hcv = hp_ref[:, d:2 * d] * hp_ref[:, 2 * d:]
        cv1 = _prev_rows(cv, hcv, 1, first)
        cv2 = _prev_rows(cv, hcv, 2, first)
        w0, w1, w2 = w_ref[0:1, :], w_ref[1:2, :], w_ref[2:3, :]
        u = w2 * cv + w1 * cv1 + w0 * cv2
        dzv = dz_ref[...]
        du = dzv * b
        dun = dzn_ref[...] * hn_ref[:, :d]
        dcv = w2 * du + w1 * _next_rows(du, dun, 1, last) + w0 * _next_rows(du, dun, 2, last)
        dp_ref[:, :d] = (dzv * u).astype(BF16)
        dp_ref[:, d:2 * d] = (dcv * v).astype(BF16)
        dp_ref[:, 2 * d:] = (dcv * c).astype(BF16)

        @pl.when(first)
        def _():
            dw_ref[...] = jnp.zeros_like(dw_ref)

        dw_ref[0:1, :] += jnp.sum(du * cv2, axis=0, keepdims=True)
        dw_ref[1:2, :] += jnp.sum(du * cv1, axis=0, keepdims=True)
        dw_ref[2:3, :] += jnp.sum(du * cv, axis=0, keepdims=True)

    wide = pl.BlockSpec((ts, 3 * d), lambda i: (i, 0))
    wspec = pl.BlockSpec((3, d), lambda i: (0, 0))
    return _pc(body, name, (nt,),
               [wide, p_prev, p_next, pl.BlockSpec((ts, d), lambda i: (i, 0)), dz_next, wspec],
               [wide, wspec], [_sds((s, 3 * d), BF16), _sds((3, d))])(p, p, p, dz, dz, w_dw)


def _conv_fwd(x, h, w_in, w_dw, w_out, tag, next_norm, stages=None):
    wn = _cols_from_blocks(w_in)
    hosted = _stage_gather(stages, "conv_in")
    p = _mm(h, wn, "nn", f"conv_in_{tag}", hosted=hosted)
    if hosted is not None:
        p, *got = p
        _stage_arrived(stages, "conv_in", got)
    hosted = _stage_gather(stages, "conv_gate")
    z = _cgate_fwd(p, w_dw, f"conv_gate_{tag}", hosted)
    if hosted is not None:
        z, *got = z
        w_out = _stage_arrived(stages, "conv_gate", got, w_out)
    y, h_next = _mm(z, w_out, "nn", f"conv_out_{tag}", res=x, norm_w=next_norm)
    return y, h_next, (x, h, p, z, wn)


def _conv_bwd(dy, saved, norm_w, w_in, w_dw, w_out, tag, hosted_fn=None):
    x, h, p, z, wn = saved
    dz = _mm(dy, w_out, "nt", f"conv_dz_{tag}")
    g_out = _mm(z, dy, "tn", f"conv_gout_{tag}", out_dtype=BF16)
    dp, g_dw = _cgate_bwd(p, dz, w_dw, f"conv_dgate_{tag}")
    g_in = _blocks_from_cols(_mm(h, dp, "tn", f"conv_gin_{tag}", out_dtype=BF16))
    hosted = _stage_slabs(hosted_fn, "conv_dh", g_in=g_in, g_out=g_out)
    dx, g_norm, *got = _mm_dnorm_nt(dp, wn, f"conv_dh_{tag}", x, norm_w, dy, hosted=hosted)
    return dx, g_norm, g_in, g_dw, g_out, ({"conv_dh": got} if hosted is not None else {})


def _tri(lower, n=LANES):
    r = lax.broadcasted_iota(jnp.int32, (n, n), 0)
    c = lax.broadcasted_iota(jnp.int32, (n, n), 1)
    return jnp.where((r >= c) if lower else (r <= c), 1.0, 0.0).astype(F32)


def _cumsum_rows(v, name):
    s = v.shape[0]
    rows = _row_tile(s, CUMSUM_ROWS)

    def body(v_ref, o_ref, carry_ref):
        @pl.when(pl.program_id(0) == 0)
        def _():
            carry_ref[...] = jnp.zeros_like(carry_ref)

        blk = v_ref[...]
        o_ref[...] = _dot(_tri(True, rows), blk, 1, 0, HI) + carry_ref[0:1, :]
        carry_ref[...] += jnp.sum(blk, axis=0, keepdims=True)

    spec = pl.BlockSpec((rows, LANES), lambda i: (i, 0))
    return _pc(body, name, (s // rows,), [spec], spec, _sds((s, LANES)), [pltpu.VMEM((8, LANES), F32)])(v)


def _fox_dcum(dck, dcq, name):
    s = dck.shape[0]
    rows = _row_tile(s, CUMSUM_ROWS)
    n = s // rows

    def body(k_ref, q_ref, o_ref, carry_ref):
        @pl.when(pl.program_id(0) == 0)
        def _():
            carry_ref[...] = jnp.zeros_like(carry_ref)

        head = lax.broadcasted_iota(jnp.int32, (ATTN_HEADS, LANES), 0)
        unit = jnp.where(head == lax.broadcasted_iota(jnp.int32, (ATTN_HEADS, LANES), 1), 1.0, 0.0).astype(F32)
        blk = _dot(q_ref[...], unit, 0, 0, HI)
        lane = lax.broadcasted_iota(jnp.int32, (rows, LANES), 1)
        for hd in range(ATTN_HEADS):
            first_lane = hd * HEAD_DIM
            pair = k_ref[:, first_lane // LANES * LANES:(first_lane // LANES + 1) * LANES]
            blk = blk + jnp.where(lane == hd, pltpu.roll(pair, (hd - first_lane) % LANES, axis=1), 0.0)
        o_ref[...] = _dot(_tri(False, rows), blk, 1, 0, HI) + carry_ref[0:1, :]
        carry_ref[...] += jnp.sum(blk, axis=0, keepdims=True)

    return _pc(body, name, (n,),
               [pl.BlockSpec((rows, D_MODEL), lambda i: (n - 1 - i, 0)), pl.BlockSpec((ATTN_HEADS, rows), lambda i: (0, n - 1 - i))],
               pl.BlockSpec((rows, LANES), lambda i: (n - 1 - i, 0)), _sds((s, LANES)), [pltpu.VMEM((8, LANES), F32)])(dck, dcq)


def _lo_mask(shape):
    return lax.broadcasted_iota(jnp.int32, shape, len(shape) - 1) < HEAD_DIM


def _half_sums(v, lo):
    sa = jnp.sum(jnp.where(lo, v, 0.0), axis=-1, keepdims=True)
    sb = jnp.sum(jnp.where(lo, 0.0, v), axis=-1, keepdims=True)
    return jnp.where(lo, sa, sb)


def _fox_prep_fwd(proj, gq, gk, name):
    s = proj.shape[0]
    ts = _row_tile(s)
    qscale = HEAD_DIM ** -0.5 * LOG2E

    def body(q_ref, k_ref, v_ref, gq_ref, gk_ref, qo_ref, ko_ref, vo_ref):
        lo = _lo_mask((ts, LANES))

        def hnorm(xv, g):
            ms = _half_sums(xv * xv, lo) * (1.0 / HEAD_DIM)
            return (xv * lax.rsqrt(ms + RMS_EPS)) * g

        for p in range(8):
            cols = slice(p * LANES, (p + 1) * LANES)
            qo_ref[:, cols] = (hnorm(q_ref[:, cols], gq_ref[...]) * qscale).astype(BF16)
            ko_ref[:, cols] = hnorm(k_ref[:, cols], gk_ref[...]).astype(BF16)
        vo_ref[...] = v_ref[...].astype(BF16)

    def wide(blk):
        return pl.BlockSpec((ts, D_MODEL), lambda i: (i, blk))

    gspec = pl.BlockSpec((1, LANES), lambda i: (0, 0))
    out = _sds((s, D_MODEL), BF16)
    return _pc(body, name, (s // ts,), [wide(0), wide(1), wide(2), gspec, gspec], [wide(0)] * 3, [out] * 3)(
        proj, proj, proj, gq, gk)


def _fox_logf(proj, bf, name):
    s = proj.shape[0]
    ts = _row_tile(s, 512)

    def body(f_ref, b_ref, o_ref):
        z = f_ref[...] + b_ref[...]
        lf = jnp.minimum(z, 0.0) - jnp.log(1.0 + jnp.exp(-jnp.abs(z)))
        real = lax.broadcasted_iota(jnp.int32, (ts, LANES), 1) < ATTN_HEADS
        o_ref[...] = jnp.where(real, lf, 0.0)

    return _pc(body, name, (s // ts,), [pl.BlockSpec((ts, LANES), lambda i: (i, 24)), pl.BlockSpec((1, LANES), lambda i: (0, 0))],
               pl.BlockSpec((ts, LANES), lambda i: (i, 0)), _sds((s, LANES)))(proj, bf)


def _fox_dlogf(proj, bf, dlf, name):
    s = proj.shape[0]
    ts = _row_tile(s, 512)

    def body(f_ref, b_ref, d_ref, o_ref, db_ref):
        z = f_ref[...] + b_ref[...]
        real = lax.broadcasted_iota(jnp.int32, (ts, LANES), 1) < ATTN_HEADS
        g = jnp.where(real, d_ref[...] * _sigmoid(-z), 0.0)
        o_ref[...] = g.astype(BF16)

        @pl.when(pl.program_id(0) == 0)
        def _():
            db_ref[...] = jnp.zeros_like(db_ref)

        db_ref[...] += jnp.sum(g, axis=0, keepdims=True)

    vec = pl.BlockSpec((1, LANES), lambda i: (0, 0))
    row = pl.BlockSpec((ts, LANES), lambda i: (i, 0))
    return _pc(body, name, (s // ts,), [pl.BlockSpec((ts, LANES), lambda i: (i, 24)), vec, row], [row, vec],
               [_sds((s, LANES), BF16), _sds((1, LANES))])(proj, bf, dlf)


def _decay_placement():
    pq = np.zeros((3 * LANES, D_MODEL), np.float32)
    pk = np.zeros((3 * LANES, D_MODEL), np.float32)
    oq = np.zeros((1, D_MODEL), np.float32)
    ok = np.zeros((1, D_MODEL), np.float32)
    for hd in range(ATTN_HEADS):
        base = (hd // 2) * LANES + (0 if hd % 2 else HEAD_DIM)
        for term in range(3):
            pq[term * LANES + hd, base + term] = 1.0
            pk[term * LANES + hd, base + 3 + term] = -1.0
        oq[0, base + 3:base + 6] = 1.0
        ok[0, base:base + 3] = 1.0
    return jnp.asarray(pq, BF16), jnp.asarray(pk, BF16), jnp.asarray(oq), jnp.asarray(ok)


def _decay_terms(cum, name):
    s = cum.shape[0]
    ts = _row_tile(s)

    def body(c_ref, pq_ref, pk_ref, oq_ref, ok_ref, aq_ref, ak_ref):
        c2 = c_ref[...] * LOG2E
        hi = c2.astype(BF16)
        rest = c2 - hi.astype(F32)
        mid = rest.astype(BF16)
        low = (rest - mid.astype(F32)).astype(BF16)
        terms = jnp.concatenate([hi, mid, low], axis=1)
        aq_ref[...] = (_dot(terms, pq_ref[...], 1, 0) + oq_ref[...]).astype(BF16)
        ak_ref[...] = (_dot(terms, pk_ref[...], 1, 0) + ok_ref[...]).astype(BF16)

    mat = pl.BlockSpec((3 * LANES, D_MODEL), lambda i: (0, 0))
    row = pl.BlockSpec((1, D_MODEL), lambda i: (0, 0))
    out = pl.BlockSpec((ts, D_MODEL), lambda i: (i, 0))
    return _pc(body, name, (s // ts,), [pl.BlockSpec((ts, LANES), lambda i: (i, 0)), mat, mat, row, row], [out, out],
               [_sds((s, D_MODEL), BF16), _sds((s, D_MODEL), BF16)])(cum, *_decay_placement())


def _attn_tiles(s):
    t = s
    for cand in (ATTN_TILE, ATTN_TILE // 2):
        if s % cand == 0:
            t = cand
            break
    return t, s // t


def _tri_steps(n, by_key):
    if by_key:
        pairs = [(q, k) for k in range(n) for q in range(k, n)]
    else:
        pairs = [(q, k) for q in range(n) for k in range(q + 1)]
    arr = np.asarray(pairs, np.int32)
    return jnp.asarray(arr[:, 0]), jnp.asarray(arr[:, 1])


def _attn_call(body, name, s, by_key, inputs, in_kinds, out_kinds, out_shapes, scratch, hosted=None, vmem=VMEM_LIMIT_BYTES):
    t, n = _attn_tiles(s)
    qi_arr, ki_arr = _tri_steps(n, by_key)
    nsteps = int(qi_arr.shape[0])
    specs = {
        "q": pl.BlockSpec((t, LANES), lambda p, i, qi, ki: (qi[i], p)),
        "k": pl.BlockSpec((t, LANES), lambda p, i, qi, ki: (ki[i], p)),
        "r": pl.BlockSpec((1, 2, t), lambda p, i, qi, ki: (p, 0, qi[i])),
        "m": pl.BlockSpec((1, t, t), lambda p, i, qi, ki: (jnp.where(qi[i] == ki[i], 1, 0), 0, 0)),
        "Q": pl.BlockSpec((1, LANES, s), lambda p, i, qi, ki: (p, 0, 0)),
        "R": pl.BlockSpec((1, 2, s), lambda p, i, qi, ki: (p, 0, 0)),
    }
    in_specs = [specs[c] for c in in_kinds]
    out_specs = [specs[c] for c in out_kinds]
    out_shapes, scratch, inputs = list(out_shapes), list(scratch), list(inputs)
    run = body
    if hosted is not None:
        arrays, gather = hosted
        na, n_in, n_out, n_scr = len(arrays), len(inputs), len(out_kinds), len(scratch)
        pick, xouts, sems = _exchange_parts(arrays, gather)

        def run(qi_ref, ki_ref, *refs):
            ins, srcs = refs[:n_in], refs[n_in:n_in + na]
            outs, dsts = refs[n_in + na:n_in + na + n_out], refs[n_in + na + n_out:n_in + 2 * na + n_out]
            scr, xsems = refs[n_in + 2 * na + n_out:n_in + 2 * na + n_out + n_scr], refs[n_in + 2 * na + n_out + n_scr:]
            p = pl.program_id(0)
            i = pl.program_id(1)

            @pl.when(jnp.logical_and(p == 0, i == 0))
            def _():
                _exchange_start(_exchange_copies(pick(srcs), dsts, *xsems))

            body(qi_ref, ki_ref, *ins, *outs, *scr)

            @pl.when(jnp.logical_and(p == 7, i == nsteps - 1))
            def _():
                _exchange_wait(_exchange_copies(pick(srcs), dsts, *xsems))

        hbm = pl.BlockSpec(memory_space=pl.ANY)
        in_specs += [hbm] * na
        out_specs += [hbm] * na
        out_shapes += xouts
        scratch += sems
        inputs += list(arrays)
    grid_spec = pltpu.PrefetchScalarGridSpec(
        num_scalar_prefetch=2, grid=(8, nsteps), in_specs=in_specs, out_specs=out_specs, scratch_shapes=scratch)
    params = pltpu.CompilerParams(dimension_semantics=("arbitrary", "arbitrary"), vmem_limit_bytes=vmem)
    return pl.pallas_call(run, name=name, grid_spec=grid_spec, out_shape=out_shapes, compiler_params=params)(
        qi_arr, ki_arr, *inputs)


def _biased_kq(q2, k2, aq, ak, lo):
    sa = _dot(jnp.where(lo, k2, ak), jnp.where(lo, q2, aq), 1, 1)
    sb = _dot(jnp.where(lo, ak, k2), jnp.where(lo, aq, q2), 1, 1)
    return sa, sb


def _causal_bias(s):
    t, _ = _attn_tiles(s)
    kid = lax.broadcasted_iota(jnp.int32, (t, t), 0)
    qid = lax.broadcasted_iota(jnp.int32, (t, t), 1)
    return jnp.stack([jnp.zeros((t, t), BF16), jnp.where(kid > qid, -jnp.inf, 0.0).astype(BF16)])


def _fold8(v, op):
    return op(v.reshape(v.shape[0] // 8, 8, v.shape[1]), axis=0)


def _chunk(ref, mask_ref, hd, r):
    rows = slice(r * ATTN_ROWS, (r + 1) * ATTN_ROWS)
    return rows, ref[hd, rows, :] + mask_ref[0, rows, :].astype(F32)


def _flash_fwd(qs, kn, vb, augq, augk, cmask, name, hosted=None):
    s = qs.shape[0]
    t, n = _attn_tiles(s)
    nch = t // ATTN_ROWS

    def body(qi_ref, ki_ref, q_ref, k_ref, v_ref, aq_ref, ak_ref, mk_ref, o_ref, lse_ref, s_ref, p_ref, m_ref, l_ref, acc_ref):
        i = pl.program_id(1)
        qi = qi_ref[i]
        ki = ki_ref[i]

        @pl.when(ki == 0)
        def _():
            m_ref[...] = jnp.full_like(m_ref, -jnp.inf)
            l_ref[...] = jnp.zeros_like(l_ref)
            acc_ref[...] = jnp.zeros_like(acc_ref)

        lo = _lo_mask((t, LANES))
        rowlo = lax.broadcasted_iota(jnp.int32, (LANES, t), 0) < HEAD_DIM
        v2 = v_ref[...]
        sa, sb = _biased_kq(q_ref[...], k_ref[...], aq_ref[...], ak_ref[...], lo)
        s_ref[0] = sa
        s_ref[1] = sb
        alphas, pvs = [], []
        for hd in range(2):
            mx = jnp.full((8, t), -jnp.inf, F32)
            for r in range(nch):
                _, sc = _chunk(s_ref, mk_ref, hd, r)
                mx = jnp.maximum(mx, _fold8(sc, jnp.max))
            m_prev = m_ref[hd:hd + 1, :]
            m_new = jnp.maximum(m_prev, jnp.max(mx, axis=0, keepdims=True))
            ls = jnp.zeros((8, t), F32)
            for r in range(nch):
                rows, sc = _chunk(s_ref, mk_ref, hd, r)
                pm = jnp.exp2(sc - m_new)
                ls = ls + _fold8(pm, jnp.sum)
                p_ref[hd, rows, :] = pm.astype(BF16)
            alpha = jnp.exp2(m_prev - m_new)
            l_ref[hd:hd + 1, :] = alpha * l_ref[hd:hd + 1, :] + jnp.sum(ls, axis=0, keepdims=True)
            m_ref[hd:hd + 1, :] = m_new
            alphas.append(alpha)
            pvs.append(_dot(v2, p_ref[hd], 0, 0))
        acc_ref[...] = jnp.where(rowlo, alphas[0], alphas[1]) * acc_ref[...] + jnp.where(rowlo, pvs[0], pvs[1])

        @pl.when(ki == qi)
        def _():
            o_ref[...] = (acc_ref[...] / jnp.where(rowlo, l_ref[0:1, :], l_ref[1:2, :])).T
            lse_ref[0] = m_ref[0:2, :] + jnp.log2(l_ref[0:2, :])

    stat = pltpu.VMEM((8, t), F32)
    return _attn_call(body, name, s, False, (qs, kn, vb, augq, augk, cmask), "qkkqkm", "qr",
                      [_sds((s, D_MODEL)), _sds((8, 2, s))],
                      [pltpu.VMEM((2, t, t), F32), pltpu.VMEM((2, t, t), BF16), stat, stat, pltpu.VMEM((LANES, t), F32)],
                      hosted=hosted)


def _fox_delta(do, o, name):
    s = do.shape[0]
    ts = _row_tile(s)

    def body(do_ref, o_ref, d_ref):
        head = lax.broadcasted_iota(jnp.int32, (ATTN_HEADS, D_MODEL), 0)
        col = lax.broadcasted_iota(jnp.int32, (ATTN_HEADS, D_MODEL), 1)
        member = jnp.where(col // HEAD_DIM == head, 1.0, 0.0).astype(F32)
        d_ref[...] = _dot(member, do_ref[...] * o_ref[...], 1, 1, HI)

    spec = pl.BlockSpec((ts, D_MODEL), lambda i: (i, 0))
    return _pc(body, name, (s // ts,), [spec, spec], pl.BlockSpec((ATTN_HEADS, ts), lambda i: (0, i)), _sds((ATTN_HEADS, s)))(do, o)


def _bwd_tile(q_ref, k_ref, v_ref, aq_ref, ak_ref, do_ref, s_ref, dp_ref, lo):
    do2 = do_ref[...].astype(BF16)
    zero = jnp.zeros_like(do2)
    v2 = v_ref[...]
    sa, sb = _biased_kq(q_ref[...], k_ref[...], aq_ref[...], ak_ref[...], lo)
    s_ref[0] = sa
    s_ref[1] = sb
    dp_ref[0] = _dot(v2, jnp.where(lo, do2, zero), 1, 1)
    dp_ref[1] = _dot(v2, jnp.where(lo, zero, do2), 1, 1)
    return do2


def _bwd_chunk(s_ref, dp_ref, mk_ref, lse_ref, dl_ref, hd, r):
    rows, sc = _chunk(s_ref, mk_ref, hd, r)
    pm = jnp.exp2(sc - lse_ref[0, hd:hd + 1, :])
    ds = pm * (dp_ref[hd, rows, :] - dl_ref[0, hd:hd + 1, :])
    return rows, pm, ds


def _flash_bwd(qs, kn, vb, augq, augk, cmask, do, lse, delta, name, hosted=None):
    s = qs.shape[0]
    t, n = _attn_tiles(s)
    nch = t // ATTN_ROWS

    def body(qi_ref, ki_ref, q_ref, k_ref, v_ref, aq_ref, ak_ref, mk_ref, do_ref, lse_ref, dl_ref,
             dk_ref, dv_ref, dc_ref, dq_ref, dcq_ref, s_ref, dp_ref, p_ref, ds_ref, dka_ref, dva_ref, dca_ref):
        i = pl.program_id(1)
        qi = qi_ref[i]
        ki = ki_ref[i]

        @pl.when(i == 0)
        def _():
            dq_ref[...] = jnp.zeros_like(dq_ref)
            dcq_ref[...] = jnp.zeros_like(dcq_ref)

        @pl.when(qi == ki)
        def _():
            dka_ref[...] = jnp.zeros_like(dka_ref)
            dva_ref[...] = jnp.zeros_like(dva_ref)
            dca_ref[...] = jnp.zeros_like(dca_ref)

        lo = _lo_mask((t, LANES))
        rowlo = lax.broadcasted_iota(jnp.int32, (LANES, t), 0) < HEAD_DIM
        do2 = _bwd_tile(q_ref, k_ref, v_ref, aq_ref, ak_ref, do_ref, s_ref, dp_ref, lo)
        q2 = q_ref[...]
        k2 = k_ref[...]
        qcols = pl.ds(pl.multiple_of(qi * t, t), t)
        dvs, dks, dqs = [], [], []
        for hd in range(2):
            rs = jnp.zeros((8, t), F32)
            for r in range(nch):
                rows, pm, ds = _bwd_chunk(s_ref, dp_ref, mk_ref, lse_ref, dl_ref, hd, r)
                rs = rs + _fold8(ds, jnp.sum)
                part = ds[:, 0:LANES]
                for c in range(1, t // LANES):
                    part = part + ds[:, c * LANES:(c + 1) * LANES]
                dca_ref[hd, rows, :] += part
                p_ref[hd, rows, :] = pm.astype(BF16)
                ds_ref[hd, rows, :] = ds.astype(BF16)
            dcq_ref[0, hd:hd + 1, qcols] += jnp.sum(rs, axis=0, keepdims=True)
            dvs.append(_dot(p_ref[hd], do2, 1, 0))
            dks.append(_dot(ds_ref[hd], q2, 1, 0))
            dqs.append(_dot(k2, ds_ref[hd], 0, 0))
        dva_ref[...] += jnp.where(lo, dvs[0], dvs[1])
        dka_ref[...] += jnp.where(lo, dks[0], dks[1])
        dq_ref[0, :, qcols] += jnp.where(rowlo, dqs[0], dqs[1])

        @pl.when(qi == n - 1)
        def _():
            dk_ref[...] = dka_ref[...] * LN2
            dv_ref[...] = dva_ref[...]
            dc_ref[...] = -jnp.where(lo, jnp.sum(dca_ref[0], axis=-1, keepdims=True), jnp.sum(dca_ref[1], axis=-1, keepdims=True))

    out = _sds((s, D_MODEL))
    return _attn_call(body, name, s, True, (qs, kn, vb, augq, augk, cmask, do, lse, delta), "qkkqkmqrr", "kkkQR",
                      [out, out, out, _sds((8, LANES, s)), _sds((8, 2, s))],
                      [pltpu.VMEM((2, t, t), F32), pltpu.VMEM((2, t, t), F32), pltpu.VMEM((2, t, t), BF16),
                       pltpu.VMEM((2, t, t), BF16), pltpu.VMEM((t, LANES), F32), pltpu.VMEM((t, LANES), F32),
                       pltpu.VMEM((2, t, LANES), F32)], hosted=hosted, vmem=ATTN_BWD_VMEM_BYTES)


def _fox_prep_bwd(proj, dqs, dk, dv, gq, gk, name):
    s = proj.shape[0]
    ts = _row_tile(s)
    scale = HEAD_DIM ** -0.5

    def body(q_ref, k_ref, dq_ref, dk_ref, dv_ref, gq_ref, gk_ref, oq_ref, ok_ref, ov_ref, dgq_ref, dgk_ref):
        lo = _lo_mask((ts, LANES))

        @pl.when(pl.program_id(0) == 0)
        def _():
            dgq_ref[...] = jnp.zeros_like(dgq_ref)
            dgk_ref[...] = jnp.zeros_like(dgk_ref)

        def back(xv, dout, g):
            r = lax.rsqrt(_half_sums(xv * xv, lo) * (1.0 / HEAD_DIM) + RMS_EPS)
            y = xv * r
            dy = dout * g
            dx = r * (dy - y * (_half_sums(dy * y, lo) * (1.0 / HEAD_DIM)))
            return dx, jnp.sum(dout * y, axis=0, keepdims=True)

        for p in range(8):
            cols = slice(p * LANES, (p + 1) * LANES)
            dxq, dgq = back(q_ref[:, cols], dq_ref[p].T * scale, gq_ref[...])
            dxk, dgk = back(k_ref[:, cols], dk_ref[:, cols], gk_ref[...])
            oq_ref[:, cols] = dxq.astype(BF16)
            ok_ref[:, cols] = dxk.astype(BF16)
            dgq_ref[...] += dgq
            dgk_ref[...] += dgk
        ov_ref[...] = dv_ref[...].astype(BF16)

    def wide(blk):
        return pl.BlockSpec((ts, D_MODEL), lambda i: (i, blk))

    gspec = pl.BlockSpec((1, LANES), lambda i: (0, 0))
    out = _sds((s, D_MODEL), BF16)
    dqt = pl.BlockSpec((8, LANES, ts), lambda i: (0, 0, i))
    return _pc(body, name, (s // ts,), [wide(0), wide(1), dqt, wide(0), wide(0), gspec, gspec],
               [wide(0)] * 3 + [gspec] * 2, [out] * 3 + [_sds((1, LANES))] * 2)(proj, proj, dqs, dk, dv, gq, gk)


def _fox_fwd(x, h, w_in, b_f, q_gain, k_gain, w_out, next_norm, hosted=None):
    proj = _mm(h, w_in, "nn", "fox_in")
    gq = jnp.tile(q_gain, (1, 2))
    gk = jnp.tile(k_gain, (1, 2))
    bf = jnp.pad(b_f, ((0, 0), (0, LANES - ATTN_HEADS)))
    qs, kn, vb = _fox_prep_fwd(proj, gq, gk, "fox_prep")
    augq, augk = _decay_terms(_cumsum_rows(_fox_logf(proj, bf, "fox_logf"), "fox_cum"), "fox_decay")
    cmask = _causal_bias(x.shape[0])
    o, lse, *got = _flash_fwd(qs, kn, vb, augq, augk, cmask, "fox_attn", hosted=hosted)
    y, h_next = _mm(o, w_out, "nn", "fox_out", res=x, norm_w=next_norm)
    return y, h_next, (x, h, proj, gq, gk, bf, qs, kn, vb, augq, augk, cmask, o, lse), got


def _fox_bwd(dy, saved, norm_w, w_in, w_out, hosted=None):
    x, h, proj, gq, gk, bf, qs, kn, vb, augq, augk, cmask, o, lse = saved
    s = x.shape[0]
    do = _mm(dy, w_out, "nt", "fox_do")
    g_out = _mm(o, dy, "tn", "fox_gout", out_dtype=BF16)
    delta = _fox_delta(do, o, "fox_delta").reshape(8, 2, s)
    dk, dv, dck, dqs, dcq, *got = _flash_bwd(qs, kn, vb, augq, augk, cmask, do, lse, delta, "fox_dattn", hosted=hosted)
    dlf = _fox_dcum(dck, dcq.reshape(ATTN_HEADS, s), "fox_dcum")
    dfl, g_bf = _fox_dlogf(proj, bf, dlf, "fox_dlogf")
    dq_o, dk_o, dv_o, g_gq, g_gk = _fox_prep_bwd(proj, dqs, dk, dv, gq, gk, "fox_dprep")
    dproj = jnp.concatenate([dq_o, dk_o, dv_o, dfl], axis=1)
    g_in = _mm(h, dproj, "tn", "fox_gin", out_dtype=BF16)
    dx, g_norm = _mm_dnorm_nt(dproj, w_in, "fox_dh", x, norm_w, dy)
    g_q = g_gq[:, :HEAD_DIM] + g_gq[:, HEAD_DIM:]
    g_k = g_gk[:, :HEAD_DIM] + g_gk[:, HEAD_DIM:]
    return dx, g_norm, g_in[:, :FOX_IN], g_bf[:, :ATTN_HEADS], g_q, g_k, g_out, got


def _ssd_conv_fwd(proj, cw, cb, name):
    s = proj.shape[0]
    ts = _row_tile(s)
    w = 1024
    per = ts // 8

    def body(p_ref, h_ref, w_ref, b_ref, o_ref):
        first = pl.program_id(0) == 0
        cur = p_ref[...]
        halo = h_ref[...]
        u = w_ref[3:4, :] * cur + b_ref[...]
        for j in range(1, 4):
            u = u + w_ref[3 - j:4 - j, :] * _prev_rows(cur, halo, j, first)
        o_ref[...] = u * _sigmoid(u)

    return _pc(body, name, (s // ts, 4),
               [pl.BlockSpec((ts, w), lambda i, j: (i, 2 + j)),
                pl.BlockSpec((8, w), lambda i, j: (jnp.maximum(i * per - 1, 0), 2 + j)),
                pl.BlockSpec((4, w), lambda i, j: (0, j)), pl.BlockSpec((1, w), lambda i, j: (0, j))],
               pl.BlockSpec((ts, w), lambda i, j: (i, j)), _sds((s, SSM_CONV_DIM)))(proj, proj, cw, cb)


def _ssd_conv_bwd(proj, d, first_col, cw, cb, name):
    s = proj.shape[0]
    ts = _row_tile(s)
    nt = s // ts
    w = 1024
    ncol = d.shape[1] // w
    per = ts // 8
    nblk = s // 8

    def body(p_ref, hp_ref, hn_ref, d_ref, dn_ref, w_ref, b_ref, o_ref, dw_ref, db_ref):
        i = pl.program_id(1)
        first = i == 0
        last = i == nt - 1
        cur = p_ref[...]
        prev = [cur] + [_prev_rows(cur, hp_ref[...], j, first) for j in range(1, 4)]
        nxt = hn_ref[...]
        tail = cur[ts - 8:, :]
        u = b_ref[...]
        un = b_ref[...]
        for j in range(4):
            u = u + w_ref[3 - j:4 - j, :] * prev[j]
            un = un + w_ref[3 - j:4 - j, :] * (nxt if j == 0 else _prev_rows(nxt, tail, j, False))
        sg = _sigmoid(u)
        g = d_ref[...] * (sg * (1.0 + u * (1.0 - sg)))
        sn = _sigmoid(un)
        gn = dn_ref[...] * (sn * (1.0 + un * (1.0 - sn)))

        @pl.when(first)
        def _():
            dw_ref[...] = jnp.zeros_like(dw_ref)
            db_ref[...] = jnp.zeros_like(db_ref)

        dpre = w_ref[3:4, :] * g
        for j in range(1, 4):
            dpre = dpre + w_ref[3 - j:4 - j, :] * _next_rows(g, gn, j, last)
        for j in range(4):
            dw_ref[3 - j:4 - j, :] += jnp.sum(g * prev[j], axis=0, keepdims=True)
        db_ref[...] += jnp.sum(g, axis=0, keepdims=True)
        o_ref[...] = dpre.astype(BF16)

    tile = pl.BlockSpec((ts, w), lambda j, i: (i, j))
    wspec = lambda off: pl.BlockSpec((4, w), lambda j, i: (0, off + j))
    vec = lambda off: pl.BlockSpec((1, w), lambda j, i: (0, off + j))
    nxt_blk = lambda off: pl.BlockSpec((8, w), lambda j, i: (jnp.minimum((i + 1) * per, nblk - 1), off + j))
    in_proj = 2 + first_col
    return _pc(body, name, (ncol, nt),
               [pl.BlockSpec((ts, w), lambda j, i: (i, in_proj + j)),
                pl.BlockSpec((8, w), lambda j, i: (jnp.maximum(i * per - 1, 0), in_proj + j)), nxt_blk(in_proj),
                tile, nxt_blk(0), wspec(first_col), vec(first_col)],
               [tile, wspec(0), vec(0)], [_sds((s, ncol * w), BF16), _sds((4, ncol * w)), _sds((1, ncol * w))])(
                   proj, proj, proj, d, d, cw, cb)


def _ssd_dt_fwd(proj, bias, a_neg, name):
    s = proj.shape[0]
    n = s // SSM_CHUNK

    def body(r_ref, b_ref, a_ref, dt_ref, ac_ref):
        real = lax.broadcasted_iota(jnp.int32, (SSM_CHUNK, LANES), 1) < SSM_HEADS
        dt = jnp.where(real, _softplus(r_ref[...] + b_ref[...]), 0.0)
        dt_ref[...] = dt
        ac_ref[...] = _dot(_tri(True), dt * a_ref[...], 1, 0, HI)

    vec = pl.BlockSpec((1, LANES), lambda c: (0, 0))
    row = pl.BlockSpec((SSM_CHUNK, LANES), lambda c: (c, 0))
    return _pc(body, name, (n,), [pl.BlockSpec((SSM_CHUNK, LANES), lambda c: (c, 48)), vec, vec], [row, row],
               [_sds((s, LANES)), _sds((s, LANES))])(proj, bias, a_neg)


def _ssd_dt_bwd(proj, bias, ddt, name):
    s = proj.shape[0]
    ts = _row_tile(s, 512)

    def body(r_ref, b_ref, d_ref, o_ref, db_ref):
        real = lax.broadcasted_iota(jnp.int32, (ts, LANES), 1) < SSM_HEADS
        g = jnp.where(real, d_ref[...] * _sigmoid(r_ref[...] + b_ref[...]), 0.0)
        o_ref[...] = g.astype(BF16)

        @pl.when(pl.program_id(0) == 0)
        def _():
            db_ref[...] = jnp.zeros_like(db_ref)

        db_ref[...] += jnp.sum(g, axis=0, keepdims=True)

    vec = pl.BlockSpec((1, LANES), lambda i: (0, 0))
    row = pl.BlockSpec((ts, LANES), lambda i: (i, 0))
    return _pc(body, name, (s // ts,), [pl.BlockSpec((ts, LANES), lambda i: (i, 48)), vec, row], [row, vec],
               [_sds((s, LANES), BF16), _sds((1, LANES))])(proj, bias, ddt)


def _pair_cols(cols, k0, lo):
    return jnp.where(lo, cols[:, k0:k0 + 1], cols[:, k0 + 1:k0 + 2])


def _last_lane(row):
    lane = lax.broadcasted_iota(jnp.int32, row.shape, 1)
    return jnp.sum(jnp.where(lane == SSM_CHUNK - 1, row, 0.0), axis=-1, keepdims=True)


SSD_FWD_GROUPS = 2
SSD_BWD_GROUPS = 1


def _ssd_specs(nc, rev, n):
    cc = (lambda c: nc - 1 - c) if rev else (lambda c: c)
    nb = SSM_INNER // (LANES * n)
    return dict(
        x=pl.BlockSpec((SSM_CHUNK, 256 * n), lambda g, c: (cc(c), g)),
        b=pl.BlockSpec((SSM_CHUNK, LANES * n), lambda g, c: (cc(c), nb + g)),
        c=pl.BlockSpec((SSM_CHUNK, LANES * n), lambda g, c: (cc(c), nb + SSM_GROUPS // n + g)),
        col=pl.BlockSpec((n, SSM_CHUNK, 4), lambda g, c: (g, cc(c), 0)),
        row=pl.BlockSpec((n, 4, SSM_CHUNK), lambda g, c: (g, 0, cc(c))),
        grp=pl.BlockSpec((n, 1, 256), lambda g, c: (g, 0, 0)),
        grow=pl.BlockSpec((n, 4, LANES), lambda g, c: (g, 0, 0)),
        hs=pl.BlockSpec((1, n, 256, SSM_STATE), lambda g, c: (cc(c), g, 0, 0)),
        bc=pl.BlockSpec((SSM_CHUNK, LANES * n), lambda g, c: (cc(c), g)),
    )


def _ssd_scan_fwd(xbc, dtc, acol, drow, arow, dskip, name):
    s = xbc.shape[0]
    nc = s // SSM_CHUNK
    n = SSD_FWD_GROUPS
    sp = _ssd_specs(nc, False, n)
    L = SSM_CHUNK

    def body(x_ref, b_ref, c_ref, dtc_ref, ac_ref, dr_ref, ar_ref, dk_ref, y_ref, hs_ref, h_ref):
        @pl.when(pl.program_id(1) == 0)
        def _():
            h_ref[...] = jnp.zeros_like(h_ref)

        for gi in range(n):
            group(gi, x_ref, b_ref, c_ref, dtc_ref, ac_ref, dr_ref, ar_ref, dk_ref, y_ref, hs_ref, h_ref)

    def group(gi, x_ref, b_ref, c_ref, dtc_ref, ac_ref, dr_ref, ar_ref, dk_ref, y_ref, hs_ref, h_ref):
        x0 = gi * 256
        bb = b_ref[:, gi * LANES:(gi + 1) * LANES].astype(BF16)
        cb = c_ref[:, gi * LANES:(gi + 1) * LANES].astype(BF16)
        gm = _dot(cb, bb, 1, 1)
        dtc = dtc_ref[gi]
        ac = ac_ref[gi]
        dr = dr_ref[gi]
        ar = ar_ref[gi]
        dsk = dk_ref[gi]
        hs_ref[0, gi] = h_ref[gi]
        tril = lax.broadcasted_iota(jnp.int32, (L, L), 0) >= lax.broadcasted_iota(jnp.int32, (L, L), 1)
        lo = _lo_mask((L, LANES))
        rowlo = lax.broadcasted_iota(jnp.int32, (L, LANES), 0) < HEAD_DIM
        for pr in range(2):
            k0 = 2 * pr
            xp = x_ref[:, x0 + pr * LANES:x0 + (pr + 1) * LANES]
            xpb = xp.astype(BF16)
            hp = h_ref[gi, pr * LANES:(pr + 1) * LANES, :]
            yd, al = [], []
            for k in (k0, k0 + 1):
                seg = ac[:, k:k + 1] - ar[k:k + 1, :]
                wk = gm * jnp.exp(jnp.where(tril, seg, -jnp.inf)) * dr[k:k + 1, :]
                yd.append(_dot(wk.astype(BF16), xpb, 1, 0))
                al.append(_last_lane(ar[k:k + 1, :]))
            e = jnp.exp(_pair_cols(ac, k0, lo))
            yo = _dot(cb, hp.astype(BF16), 1, 1) * e
            y_ref[:, x0 + pr * LANES:x0 + (pr + 1) * LANES] = (
                jnp.where(lo, yd[0], yd[1]) + yo + dsk[:, pr * LANES:(pr + 1) * LANES] * xp)
            wp = jnp.where(lo, jnp.exp(al[0] - ac[:, k0:k0 + 1]) * dtc[:, k0:k0 + 1],
                           jnp.exp(al[1] - ac[:, k0 + 1:k0 + 2]) * dtc[:, k0 + 1:k0 + 2])
            st = _dot((xp * wp).astype(BF16), bb, 0, 0)
            dec = jnp.where(rowlo, jnp.exp(al[0]), jnp.exp(al[1]))
            h_ref[gi, pr * LANES:(pr + 1) * LANES, :] = dec * hp + st

    return _pc(body, name, (SSM_GROUPS // n, nc),
               [sp["x"], sp["b"], sp["c"], sp["col"], sp["col"], sp["row"], sp["row"], sp["grp"]],
               [sp["x"], sp["hs"]], [_sds((s, SSM_INNER)), _sds((nc, SSM_GROUPS, 256, SSM_STATE))],
               [pltpu.VMEM((n, 256, SSM_STATE), F32)])(xbc, xbc, xbc, dtc, acol, drow, arow, dskip)


def _ssd_scan_bwd(xbc, dtc, acol, drow, arow, dskip, agrp, hs, dy, name):
    s = xbc.shape[0]
    nc = s // SSM_CHUNK
    n = SSD_BWD_GROUPS
    sp = _ssd_specs(nc, True, n)
    L = SSM_CHUNK

    def body(x_ref, b_ref, c_ref, dtc_ref, ac_ref, dr_ref, ar_ref, dk_ref, ag_ref, hs_ref, dy_ref,
             dx_ref, db_ref, dc_ref, ddt_ref, da_ref, dd_ref, dh_ref):
        @pl.when(pl.program_id(1) == 0)
        def _():
            dh_ref[...] = jnp.zeros_like(dh_ref)
            da_ref[...] = jnp.zeros_like(da_ref)
            dd_ref[...] = jnp.zeros_like(dd_ref)

        for gi in range(n):
            group(gi, x_ref, b_ref, c_ref, dtc_ref, ac_ref, dr_ref, ar_ref, dk_ref, ag_ref, hs_ref, dy_ref,
                  dx_ref, db_ref, dc_ref, ddt_ref, da_ref, dd_ref, dh_ref)

    def group(gi, x_ref, b_ref, c_ref, dtc_ref, ac_ref, dr_ref, ar_ref, dk_ref, ag_ref, hs_ref, dy_ref,
              dx_ref, db_ref, dc_ref, ddt_ref, da_ref, dd_ref, dh_ref):
        x0 = gi * 256
        bcols = slice(gi * LANES, (gi + 1) * LANES)
        bb = b_ref[:, bcols].astype(BF16)
        cb = c_ref[:, bcols].astype(BF16)
        gm = _dot(cb, bb, 1, 1)
        dtc = dtc_ref[gi]
        ac = ac_ref[gi]
        dr = dr_ref[gi]
        ar = ar_ref[gi]
        dsk = dk_ref[gi]
        ag = ag_ref[gi]
        tril = lax.broadcasted_iota(jnp.int32, (L, L), 0) >= lax.broadcasted_iota(jnp.int32, (L, L), 1)
        lo = _lo_mask((L, LANES))
        nlo = jnp.logical_not(lo)
        rowlo = lax.broadcasted_iota(jnp.int32, (L, LANES), 0) < HEAD_DIM
        lane = lax.broadcasted_iota(jnp.int32, (L, LANES), 1)
        lane_row = lax.broadcasted_iota(jnp.int32, (1, LANES), 1)
        dgm = jnp.zeros((L, L), F32)
        dcm = jnp.zeros((L, SSM_STATE), F32)
        dbm = jnp.zeros((L, SSM_STATE), F32)
        cols = jnp.zeros((L, LANES), F32)
        rows_ddt, rows_q, al_all, dcd_all = [], [], [], []
        for pr in range(2):
            k0 = 2 * pr
            xcols = slice(x0 + pr * LANES, x0 + (pr + 1) * LANES)
            xp = x_ref[:, xcols]
            xpb = xp.astype(BF16)
            dyp = dy_ref[:, xcols]
            dypb = dyp.astype(BF16)
            zero = jnp.zeros_like(dypb)
            hp = hs_ref[0, gi, pr * LANES:(pr + 1) * LANES, :]
            hpb = hp.astype(BF16)
            dst = dh_ref[gi, pr * LANES:(pr + 1) * LANES, :]
            dstb = dst.astype(BF16)
            dxd, al = [], []
            for k in (k0, k0 + 1):
                sel = lo if k == k0 else nlo
                seg = ac[:, k:k + 1] - ar[k:k + 1, :]
                lam = jnp.exp(jnp.where(tril, seg, -jnp.inf))
                wk = gm * lam * dr[k:k + 1, :]
                dwk = _dot(jnp.where(sel, dypb, zero), xpb, 1, 1)
                mk = dwk * gm * lam
                qk = mk * dr[k:k + 1, :]
                dgm = dgm + dwk * lam * dr[k:k + 1, :]
                rows_ddt.append(jnp.sum(mk, axis=0, keepdims=True))
                rows_q.append(jnp.sum(qk, axis=0, keepdims=True))
                cols = jnp.where(lane == k, jnp.sum(qk, axis=-1, keepdims=True), cols)
                dxd.append(_dot(wk.astype(BF16), dypb, 0, 0))
                al.append(_last_lane(ar[k:k + 1, :]))
            al_all += al
            dxp = jnp.where(lo, dxd[0], dxd[1])
            e = jnp.exp(_pair_cols(ac, k0, lo))
            dye = dyp * e
            dyeb = dye.astype(BF16)
            dcm = dcm + _dot(dyeb, hpb, 1, 0)
            dh_yoff = _dot(dyeb, cb, 0, 0)
            tq = dye * _dot(cb, hpb, 1, 1)
            cols = jnp.where(lane == 4 + k0, jnp.sum(jnp.where(lo, tq, 0.0), axis=-1, keepdims=True), cols)
            cols = jnp.where(lane == 5 + k0, jnp.sum(jnp.where(lo, 0.0, tq), axis=-1, keepdims=True), cols)
            wp = jnp.where(lo, jnp.exp(al[0] - ac[:, k0:k0 + 1]) * dtc[:, k0:k0 + 1],
                           jnp.exp(al[1] - ac[:, k0 + 1:k0 + 2]) * dtc[:, k0 + 1:k0 + 2])
            dxw = _dot(bb, dstb, 1, 1)
            dxp = dxp + dxw * wp
            tw = xp * dxw
            cols = jnp.where(lane == 8 + k0, jnp.sum(jnp.where(lo, tw, 0.0), axis=-1, keepdims=True), cols)
            cols = jnp.where(lane == 9 + k0, jnp.sum(jnp.where(lo, 0.0, tw), axis=-1, keepdims=True), cols)
            dbm = dbm + _dot((xp * wp).astype(BF16), dstb, 1, 0)
            dsl = dsk[:, pr * LANES:(pr + 1) * LANES]
            dx_ref[:, xcols] = dxp + dsl * dyp
            dd_ref[gi, :, pr * LANES:(pr + 1) * LANES] += jnp.sum(dyp * xp, axis=0, keepdims=True)
            prod = dst * hp
            dcd_all.append(jnp.sum(jnp.sum(jnp.where(rowlo, prod, 0.0), axis=-1, keepdims=True), axis=0, keepdims=True))
            dcd_all.append(jnp.sum(jnp.sum(jnp.where(rowlo, 0.0, prod), axis=-1, keepdims=True), axis=0, keepdims=True))
            dec = jnp.where(rowlo, jnp.exp(al[0]), jnp.exp(al[1]))
            dh_ref[gi, pr * LANES:(pr + 1) * LANES, :] = dec * dst + dh_yoff
        dgb = dgm.astype(BF16)
        dc_ref[:, bcols] = dcm + _dot(dgb, bb, 1, 0)
        db_ref[:, bcols] = dbm + _dot(dgb, cb, 0, 0)
        colt = cols.T
        sub8 = lax.broadcasted_iota(jnp.int32, (8, LANES), 0)
        da_rows = jnp.zeros((8, LANES), F32)
        ddt_part = []
        for k in range(4):
            rs = colt[k:k + 1, :]
            uo = colt[4 + k:5 + k, :]
            dwl = colt[8 + k:9 + k, :]
            es = jnp.exp(al_all[k] - ar[k:k + 1, :])
            wrow = es * dr[k:k + 1, :]
            dwl_w = dwl * wrow
            da_k = rs - rows_q[k] + uo - dwl_w
            tail = jnp.sum(dwl_w, axis=-1, keepdims=True) + jnp.exp(al_all[k]) * dcd_all[k]
            da_k = da_k + jnp.where(lane_row == L - 1, tail, 0.0)
            da_rows = jnp.where(sub8 == k, da_k, da_rows)
            ddt_part.append(rows_ddt[k] + dwl * es)
        dda = _dot(da_rows, _tri(True), 1, 0, HI)
        for k in range(4):
            dda_k = dda[k:k + 1, :]
            ddt_ref[gi, k:k + 1, :] = ddt_part[k] + dda_k * ag[k:k + 1, :]
            da_ref[gi, k:k + 1, :] += dda_k * dr[k:k + 1, :] * ag[k:k + 1, :]

    return _pc(body, name, (SSM_GROUPS // n, nc),
               [sp["x"], sp["b"], sp["c"], sp["col"], sp["col"], sp["row"], sp["row"], sp["grp"], sp["grow"], sp["hs"], sp["x"]],
               [sp["x"], sp["bc"], sp["bc"], sp["row"], sp["grow"], sp["grp"]],
               [_sds((s, SSM_INNER)), _sds((s, 1024)), _sds((s, 1024)), _sds((SSM_GROUPS, 4, s)),
                _sds((SSM_GROUPS, 4, LANES)), _sds((SSM_GROUPS, 1, 256))],
               [pltpu.VMEM((n, 256, SSM_STATE), F32)])(xbc, xbc, xbc, dtc, acol, drow, arow, dskip, agrp, hs, dy)


def _gnorm_fwd(y, proj, nw, name):
    s = y.shape[0]
    ts = _row_tile(s)
    gw = SSM_INNER // SSM_GROUPS

    def body(y_ref, z_ref, w_ref, o_ref):
        for g in range(SSM_GROUPS):
            sl = slice(g * gw, (g + 1) * gw)
            z = z_ref[:, sl]
            y2 = y_ref[:, sl] * (z * _sigmoid(z))
            r = lax.rsqrt(jnp.mean(y2 * y2, axis=-1, keepdims=True) + RMS_EPS)
            o_ref[:, sl] = ((y2 * r) * w_ref[:, sl]).astype(BF16)

    row = pl.BlockSpec((ts, SSM_INNER), lambda i: (i, 0))
    return _pc(body, name, (s // ts,), [row, row, pl.BlockSpec((1, SSM_INNER), lambda i: (0, 0))], row,
               _sds((s, SSM_INNER), BF16))(y, proj, nw)


def _gnorm_bwd(y, proj, nw, dyn, name):
    s = y.shape[0]
    ts = _row_tile(s)
    gw = SSM_INNER // SSM_GROUPS

    def body(y_ref, z_ref, w_ref, d_ref, dy_ref, dz_ref, dw_ref):
        @pl.when(pl.program_id(0) == 0)
        def _():
            dw_ref[...] = jnp.zeros_like(dw_ref)

        for g in range(SSM_GROUPS):
            sl = slice(g * gw, (g + 1) * gw)
            z = z_ref[:, sl]
            yv = y_ref[:, sl]
            sg = _sigmoid(z)
            sz = z * sg
            y2 = yv * sz
            r = lax.rsqrt(jnp.mean(y2 * y2, axis=-1, keepdims=True) + RMS_EPS)
            yn = y2 * r
            dout = d_ref[:, sl]
            dyg = dout * w_ref[:, sl]
            dy2 = r * (dyg - yn * jnp.mean(dyg * yn, axis=-1, keepdims=True))
            dy_ref[:, sl] = dy2 * sz
            dz_ref[:, sl] = (dy2 * yv * (sg * (1.0 + z * (1.0 - sg)))).astype(BF16)
            dw_ref[:, sl] += jnp.sum(dout * yn, axis=0, keepdims=True)

    row = pl.BlockSpec((ts, SSM_INNER), lambda i: (i, 0))
    vec = pl.BlockSpec((1, SSM_INNER), lambda i: (0, 0))
    return _pc(body, name, (s // ts,), [row, row, vec, row], [row, row, vec],
               [_sds((s, SSM_INNER)), _sds((s, SSM_INNER), BF16), _sds((1, SSM_INNER))])(y, proj, nw, dyn)


def _head_layouts(v, s):
    return v.reshape(s, SSM_GROUPS, 4).transpose(1, 0, 2), v.T.reshape(SSM_GROUPS, 4, s)


def _ssd_fwd(x, h, w_in, conv_w, conv_b, dt_bias, a_log, d_skip, gnorm_w, w_out, next_norm):
    s = x.shape[0]
    proj = _mm(h, w_in, "nn", "ssd_in")
    xbc = _ssd_conv_fwd(proj, conv_w, conv_b, "ssd_conv")
    pad = ((0, 0), (0, LANES - SSM_HEADS))
    a_neg = -jnp.exp(a_log)
    bias = jnp.pad(dt_bias, pad)
    dt, acum = _ssd_dt_fwd(proj, bias, jnp.pad(a_neg, pad), "ssd_dt")
    dtc, drow = _head_layouts(dt[:, :SSM_HEADS], s)
    acol, arow = _head_layouts(acum[:, :SSM_HEADS], s)
    dskip = jnp.repeat(d_skip.reshape(SSM_GROUPS, 1, 4), HEAD_DIM, axis=2)
    y, hs = _ssd_scan_fwd(xbc, dtc, acol, drow, arow, dskip, "ssd_scan")
    yn = _gnorm_fwd(y, proj, gnorm_w, "ssd_gnorm")
    out, h_next = _mm(yn, w_out, "nn", "ssd_out", res=x, norm_w=next_norm)
    return out, h_next, (x, h, proj, xbc, bias, a_neg, dtc, acol, drow, arow, dskip, y, hs, yn)


def _ssd_bwd(dout, saved, norm_w, w_in, conv_w, conv_b, gnorm_w, w_out):
    x, h, proj, xbc, bias, a_neg, dtc, acol, drow, arow, dskip, y, hs, yn = saved
    s = x.shape[0]
    dyn = _mm(dout, w_out, "nt", "ssd_dyn")
    g_out = _mm(yn, dout, "tn", "ssd_gout", out_dtype=BF16)
    dy, dz, g_gnorm = _gnorm_bwd(y, proj, gnorm_w, dyn, "ssd_dgnorm")
    agrp = jnp.broadcast_to(a_neg.reshape(SSM_GROUPS, 4, 1), (SSM_GROUPS, 4, LANES))
    dxs, db, dc, ddt_row, da_acc, dd_acc = _ssd_scan_bwd(xbc, dtc, acol, drow, arow, dskip, agrp, hs, dy, "ssd_dscan")
    parts = [_ssd_conv_bwd(proj, d, col, conv_w, conv_b, f"ssd_dconv_{tag}") for d, col, tag in ((dxs, 0, "x"), (db, 2, "b"), (dc, 3, "c"))]
    g_cw = jnp.concatenate([p[1] for p in parts], axis=1)
    g_cb = jnp.concatenate([p[2] for p in parts], axis=1)
    ddt = jnp.pad(ddt_row.reshape(SSM_HEADS, s).T, ((0, 0), (0, LANES - SSM_HEADS)))
    ddtraw, g_dtb = _ssd_dt_bwd(proj, bias, ddt, "ssd_ddt")
    dproj = jnp.concatenate([dz] + [p[0] for p in parts] + [ddtraw], axis=1)
    g_in = _mm(h, dproj, "tn", "ssd_gin", out_dtype=BF16)
    dx, g_norm = _mm_dnorm_nt(dproj, w_in, "ssd_dh", x, norm_w, dout)
    g_alog = jnp.sum(da_acc, axis=-1).reshape(1, SSM_HEADS)
    g_d = jnp.sum(dd_acc.reshape(SSM_GROUPS, 4, HEAD_DIM), axis=-1).reshape(1, SSM_HEADS)
    return dx, g_norm, g_in[:, :SSM_IN], g_cw, g_cb, g_dtb[:, :SSM_HEADS], g_alog, g_d, g_gnorm, g_out


def _loss_head(y, target, name):
    s, d = y.shape
    ts = _row_tile(s)

    def body(y_ref, t_ref, dy_ref, l_ref):
        @pl.when(pl.program_id(0) == 0)
        def _():
            l_ref[...] = jnp.zeros_like(l_ref)

        e = y_ref[...] - t_ref[...]
        dy_ref[...] = e * (1.0 / d)
        part = jnp.sum(jnp.sum(e * e, axis=-1, keepdims=True), axis=0, keepdims=True) * (0.5 / d)
        l_ref[...] += jnp.broadcast_to(part, l_ref.shape)

    row = pl.BlockSpec((ts, d), lambda i: (i, 0))
    dy, lacc = _pc(body, name, (s // ts,), [row, row], [row, pl.BlockSpec((8, LANES), lambda i: (0, 0))],
                   [_sds((s, d)), _sds((8, LANES))])(y, target)
    return lacc[0, 0], dy


def _local_step(x, target, w, gather_layer0=None, gather_rest=None, scatter_first=None, scatter_layer0=None):
    saved = []
    received, received_layer0 = None, {}

    def layer0_stages():
        def entry(stage):
            shards, finish = gather_layer0[stage]

            def on_arrival(got):
                nonlocal w
                w = finish(w, got)
                return {"conv_gate": lambda: w["conv_w_out"][0], "ffn_gu": lambda: w["ffn_w_down"][0]}.get(stage, lambda: None)()
            return shards, on_arrival
        return {stage: entry(stage) for stage in gather_layer0}

    at = lambda weights, n: weights[n] if n < len(weights) else None
    h = _rms_fwd(x, w["mix_norm"][0:1], "first_norm")
    for i in range(DEPTH):
        kind, j = i % 3, i // 3
        fn = w["ffn_norm"][i:i + 1]
        stages = layer0_stages() if (i == 0 and gather_layer0 is not None) else None
        if kind == 0:
            x, h, sv = _conv_fwd(x, h, w["conv_w_in"][j], w["conv_w_dw"][j], at(w["conv_w_out"], j), str(i), fn, stages)
        elif kind == 1:
            hosted = None if gather_rest is None else (gather_rest[0], True)
            x, h, sv, got = _fox_fwd(x, h, w["fox_w_in"], w["fox_b_f"], w["fox_q_gain"], w["fox_k_gain"], w["fox_w_out"], fn, hosted)
            if gather_rest is not None:
                w = gather_rest[1](w, got)
        else:
            x, h, sv = _ssd_fwd(x, h, w["ssd_w_in"], w["ssd_conv_w"], w["ssd_conv_b"], w["ssd_dt_bias"],
                                w["ssd_a_log"], w["ssd_d"], w["ssd_norm_w"], w["ssd_w_out"], fn)
        nxt = w["mix_norm"][i + 1:i + 2] if i + 1 < DEPTH else None
        x, h, sf = _ffn_fwd(x, h, w["ffn_w_gu"][i], at(w["ffn_w_down"], i), str(i), nxt, stages)
        saved.append((sv, sf))
    loss, dx = _loss_head(x, target, "loss_head")
    g = {k: [None] * n for k, n in (("mix_norm", DEPTH), ("ffn_norm", DEPTH), ("ffn_w_gu", DEPTH), ("ffn_w_down", DEPTH),
                                    ("conv_w_in", 2), ("conv_w_dw", 2), ("conv_w_out", 2))}
    for i in reversed(range(DEPTH)):
        kind, j = i % 3, i // 3
        sv, sf = saved[i]
        hosted_fn = None
        if i == 0 and scatter_layer0 is not None:
            hosted_fn = lambda stage, **new: scatter_layer0(stage, g, **new)
        dx, g["ffn_norm"][i], g["ffn_w_gu"][i], g["ffn_w_down"][i], got = _ffn_bwd(
            dx, sf, w["ffn_norm"][i:i + 1], w["ffn_w_gu"][i], w["ffn_w_down"][i], str(i), hosted_fn)
        received_layer0.update(got)
        mn = w["mix_norm"][i:i + 1]
        if kind == 0:
            dx, g["mix_norm"][i], g["conv_w_in"][j], g["conv_w_dw"][j], g["conv_w_out"][j], got = _conv_bwd(
                dx, sv, mn, w["conv_w_in"][j], w["conv_w_dw"][j], w["conv_w_out"][j], str(i), hosted_fn)
            received_layer0.update(got)
        elif kind == 1:
            hosted = None if scatter_first is None else (scatter_first(g), False)
            (dx, g["mix_norm"][i], g["fox_w_in"], g["fox_b_f"], g["fox_q_gain"], g["fox_k_gain"],
             g["fox_w_out"], received) = _fox_bwd(dx, sv, mn, w["fox_w_in"], w["fox_w_out"], hosted)
        else:
            (dx, g["mix_norm"][i], g["ssd_w_in"], g["ssd_conv_w"], g["ssd_conv_b"], g["ssd_dt_bias"], g["ssd_a_log"],
             g["ssd_d"], g["ssd_norm_w"], g["ssd_w_out"]) = _ssd_bwd(
                 dx, sv, mn, w["ssd_w_in"], w["ssd_conv_w"], w["ssd_conv_b"], w["ssd_norm_w"], w["ssd_w_out"])
    g["mix_norm"] = jnp.concatenate(g["mix_norm"], axis=0)
    g["ffn_norm"] = jnp.concatenate(g["ffn_norm"], axis=0)
    g["conv_w_dw"] = jnp.stack(g["conv_w_dw"], axis=0)
    g["ssd_conv_w"] = g["ssd_conv_w"][None]
    return loss, dx, g, received, received_layer0


def _mesh_position():
    return lax.axis_index("x") * 4 + lax.axis_index("y") * 2 + lax.axis_index("c")


def _device_of(t):
    return (lax.shift_right_logical(t, 2), lax.bitwise_and(lax.shift_right_logical(t, 1), 1), lax.bitwise_and(t, 1))


def _exchange_copies(srcs_of, out_refs, send_sems, recv_sems, local_sems):
    me = _mesh_position()
    na = len(out_refs)
    locals_ = [pltpu.make_async_copy(srcs_of(a, me), out_refs[a].at[me], local_sems.at[a]) for a in range(na)]
    sends, arrivals = [], []
    for j in range(1, NDEV):
        t = lax.rem(me + j, NDEV)
        frm = lax.rem(me + NDEV - j, NDEV)
        for a in range(na):
            sends.append(pltpu.make_async_remote_copy(
                src_ref=srcs_of(a, t), dst_ref=out_refs[a].at[me], send_sem=send_sems.at[a, j - 1],
                recv_sem=recv_sems.at[a, j - 1], device_id=_device_of(t), device_id_type=pl.DeviceIdType.MESH))
            arrivals.append(pltpu.make_async_remote_copy(
                src_ref=srcs_of(a, me), dst_ref=out_refs[a].at[frm], send_sem=send_sems.at[a, j - 1],
                recv_sem=recv_sems.at[a, j - 1], device_id=_device_of(frm), device_id_type=pl.DeviceIdType.MESH))
    return locals_, sends, arrivals


def _exchange_start(copies):
    locals_, sends, _ = copies
    for cp in locals_ + sends:
        cp.start()


def _exchange_wait(copies):
    locals_, sends, arrivals = copies
    for cp in sends:
        cp.wait_send()
    for cp in arrivals:
        cp.wait_recv()
    for cp in locals_:
        cp.wait()


def _exchange_run(srcs_of, out_refs, send_sems, recv_sems, local_sems):
    copies = _exchange_copies(srcs_of, out_refs, send_sems, recv_sems, local_sems)
    _exchange_start(copies)
    _exchange_wait(copies)


def _exchange_parts(arrays, gather):
    na = len(arrays)
    outs = [_sds(((NDEV,) + a.shape) if gather else a.shape, a.dtype) for a in arrays]
    sems = [pltpu.SemaphoreType.DMA((na, NDEV - 1)), pltpu.SemaphoreType.DMA((na, NDEV - 1)), pltpu.SemaphoreType.DMA((na,))]
    pick = (lambda srcs: (lambda a, t: srcs[a])) if gather else (lambda srcs: (lambda a, t: srcs[a].at[t]))
    return pick, outs, sems


def _exchange(arrays, name, gather):
    na = len(arrays)
    pick, outs, sems = _exchange_parts(arrays, gather)

    def body(*refs):
        _exchange_run(pick(refs[:na]), refs[na:2 * na], *refs[2 * na:])

    hbm = pl.BlockSpec(memory_space=pl.ANY)
    return pl.pallas_call(body, name=name, in_specs=[hbm] * na, out_specs=[hbm] * na, out_shape=outs, scratch_shapes=sems)(*arrays)


def _all_sum_small(pack, name):
    def body(src_ref, out_ref, buf_ref, send_sems, recv_sems, local_sems):
        _exchange_run(lambda a, t: src_ref, [buf_ref], send_sems, recv_sems, local_sems)
        acc = buf_ref[0]
        for d in range(1, NDEV):
            acc = acc + buf_ref[d]
        out_ref[...] = acc

    vmem = pl.BlockSpec(memory_space=pltpu.VMEM)
    return pl.pallas_call(
        body, name=name, in_specs=[vmem], out_specs=vmem, out_shape=_sds(pack.shape, pack.dtype),
        scratch_shapes=[pltpu.VMEM((NDEV,) + pack.shape, pack.dtype), pltpu.SemaphoreType.DMA((1, NDEV - 1)),
                        pltpu.SemaphoreType.DMA((1, NDEV - 1)), pltpu.SemaphoreType.DMA((1,))])(pack)


def _sum_slabs(slabs, name):
    _, r, c = slabs.shape
    tr = r
    for cand in (256, 352):
        if r % cand == 0:
            tr = cand
            break

    def body(s_ref, o_ref):
        acc = s_ref[0].astype(F32)
        for d in range(1, NDEV):
            acc = acc + s_ref[d].astype(F32)
        o_ref[...] = acc

    return _pc(body, name, (r // tr,), [pl.BlockSpec((NDEV, tr, c), lambda i: (0, i, 0))],
               pl.BlockSpec((tr, c), lambda i: (i, 0)), _sds((r, c)))(slabs)


def _adamw(wt, g, m, v, name):
    shape = wt.shape
    w2, g2, m2, v2 = (a.reshape(-1, shape[-1]) for a in (wt, g, m, v))
    r, c = w2.shape
    tr = r
    for cand in (512, 352, 256):
        if r % cand == 0:
            tr = cand
            break
    c1 = 1.0 - ADAM_B1 ** ADAM_STEP
    c2 = 1.0 - ADAM_B2 ** ADAM_STEP

    def body(w_ref, g_ref, m_ref, v_ref, d_ref, mo_ref, vo_ref):
        gv = g_ref[...]
        mn = ADAM_B1 * m_ref[...] + (1.0 - ADAM_B1) * gv
        vn = ADAM_B2 * v_ref[...] + (1.0 - ADAM_B2) * (gv * gv)
        mo_ref[...] = mn
        vo_ref[...] = vn
        d_ref[...] = -ADAM_LR * ((mn / c1) / (jnp.sqrt(vn / c2) + ADAM_EPS) + ADAM_WD * w_ref[...])

    spec = pl.BlockSpec((tr, c), lambda i: (i, 0))
    outs = _pc(body, name, (r // tr,), [spec] * 4, [spec] * 3, [_sds((r, c))] * 3)(w2, g2, m2, v2)
    return tuple(o.reshape(shape) for o in outs)


_NAMES = ["mix_norm", "ffn_norm", "ffn_w_gu", "ffn_w_down", "conv_w_in", "conv_w_dw", "conv_w_out", "fox_w_in", "fox_b_f",
          "fox_q_gain", "fox_k_gain", "fox_w_out", "ssd_w_in", "ssd_conv_w", "ssd_conv_b", "ssd_dt_bias", "ssd_a_log",
          "ssd_d", "ssd_norm_w", "ssd_w_out"]
_MATRICES = ["ffn_w_gu", "ffn_w_down", "conv_w_in", "conv_w_out", "fox_w_in", "fox_w_out", "ssd_w_in", "ssd_w_out"]
_VECTORS = {"conv_w_dw": 2, "ssd_conv_w": 2, "ssd_conv_b": 1, "ssd_norm_w": 1}
_REPLICATED = ["mix_norm", "ffn_norm", "fox_b_f", "fox_q_gain", "fox_k_gain", "ssd_dt_bias", "ssd_a_log", "ssd_d"]


def _to_rows(flat):
    n = flat.shape[0]
    rows = -(-n // (8 * D_MODEL)) * 8
    return jnp.pad(flat, (0, rows * D_MODEL - n)).reshape(rows, D_MODEL)


def _full_shape(local_shape, axis):
    shp = list(local_shape)
    shp[axis] *= NDEV
    return tuple(shp)


def _cols_from_blocks(g):
    return jnp.moveaxis(g, 0, 1).reshape(g.shape[1], NDEV * g.shape[2])


def _blocks_from_cols(full):
    k, n8 = full.shape
    return jnp.moveaxis(full.reshape(k, NDEV, n8 // NDEV), 1, 0)


def kernel(x, mix_norm, ffn_norm, ffn_w_gu, ffn_w_down, conv_w_in, conv_w_dw, conv_w_out, fox_w_in, fox_b_f, fox_q_gain, fox_k_gain, fox_w_out, ssd_w_in, ssd_conv_w, ssd_conv_b, ssd_dt_bias, ssd_a_log, ssd_d, ssd_norm_w, ssd_w_out, loss_target, m_mix_norm, m_ffn_norm, m_ffn_w_gu, m_ffn_w_down, m_conv_w_in, m_conv_w_dw, m_conv_w_out, m_fox_w_in, m_fox_b_f, m_fox_q_gain, m_fox_k_gain, m_fox_w_out, m_ssd_w_in, m_ssd_conv_w, m_ssd_conv_b, m_ssd_dt_bias, m_ssd_a_log, m_ssd_d, m_ssd_norm_w, m_ssd_w_out, v_mix_norm, v_ffn_norm, v_ffn_w_gu, v_ffn_w_down, v_conv_w_in, v_conv_w_dw, v_conv_w_out, v_fox_w_in, v_fox_b_f, v_fox_q_gain, v_fox_k_gain, v_fox_w_out, v_ssd_w_in, v_ssd_conv_w, v_ssd_conv_b, v_ssd_dt_bias, v_ssd_a_log, v_ssd_d, v_ssd_norm_w, v_ssd_w_out):
    local = dict(mix_norm=mix_norm, ffn_norm=ffn_norm, ffn_w_gu=ffn_w_gu, ffn_w_down=ffn_w_down, conv_w_in=conv_w_in,
                 conv_w_dw=conv_w_dw, conv_w_out=conv_w_out, fox_w_in=fox_w_in, fox_b_f=fox_b_f, fox_q_gain=fox_q_gain,
                 fox_k_gain=fox_k_gain, fox_w_out=fox_w_out, ssd_w_in=ssd_w_in, ssd_conv_w=ssd_conv_w, ssd_conv_b=ssd_conv_b,
                 ssd_dt_bias=ssd_dt_bias, ssd_a_log=ssd_a_log, ssd_d=ssd_d, ssd_norm_w=ssd_norm_w, ssd_w_out=ssd_w_out)
    mom = dict(zip(_NAMES, [m_mix_norm, m_ffn_norm, m_ffn_w_gu, m_ffn_w_down, m_conv_w_in, m_conv_w_dw, m_conv_w_out, m_fox_w_in,
                            m_fox_b_f, m_fox_q_gain, m_fox_k_gain, m_fox_w_out, m_ssd_w_in, m_ssd_conv_w, m_ssd_conv_b,
                            m_ssd_dt_bias, m_ssd_a_log, m_ssd_d, m_ssd_norm_w, m_ssd_w_out]))
    var = dict(zip(_NAMES, [v_mix_norm, v_ffn_norm, v_ffn_w_gu, v_ffn_w_down, v_conv_w_in, v_conv_w_dw, v_conv_w_out, v_fox_w_in,
                            v_fox_b_f, v_fox_q_gain, v_fox_k_gain, v_fox_w_out, v_ssd_w_in, v_ssd_conv_w, v_ssd_conv_b,
                            v_ssd_dt_bias, v_ssd_a_log, v_ssd_d, v_ssd_norm_w, v_ssd_w_out]))

    shard = {k: local[k].astype(BF16) for k in _MATRICES}
    vec_pack = _to_rows(jnp.concatenate([local[k].reshape(-1) for k in _VECTORS]))
    first = _exchange([shard["conv_w_in"][0:1], vec_pack], "gather_first", True)
    gvec = first[1].reshape(NDEV, -1)
    full = {k: local[k] for k in _REPLICATED}
    off = 0
    for k, axis in _VECTORS.items():
        n = local[k].size
        blk = jnp.moveaxis(gvec[:, off:off + n].reshape((NDEV,) + local[k].shape), 0, axis)
        full[k] = blk.reshape(_full_shape(local[k].shape, axis))
        off += n
    full["ssd_conv_w"] = full["ssd_conv_w"][0]
    full["conv_w_in"] = [first[0][:, 0]]
    full["conv_w_out"], full["ffn_w_gu"], full["ffn_w_down"] = [], [], []

    def finish_gu0(w, got):
        return dict(w, ffn_w_gu=[got[0][:, 0]])

    def finish_out0(w, got):
        return dict(w, conv_w_out=[got[0][:, 0].reshape(D_MODEL, D_MODEL)])

    def finish_down0(w, got):
        return dict(w, ffn_w_down=[got[0][:, 0].reshape(4, FF_BLOCK, D_MODEL)], fox_w_out=got[1].reshape(D_MODEL, D_MODEL))

    def finish_fox(w, got):
        return dict(w, fox_w_in=jnp.pad(_cols_from_blocks(got[0][:, 0]), ((0, 0), (0, FOX_IN_PAD - FOX_IN))))

    layer0 = {"conv_in": ([shard["ffn_w_gu"][0:1]], finish_gu0), "conv_gate": ([shard["conv_w_out"][0:1]], finish_out0),
              "ffn_gu": ([shard["ffn_w_down"][0:1], shard["fox_w_out"]], finish_down0),
              "ffn_down": ([shard["fox_w_in"]], finish_fox)}

    rest = [shard["ffn_w_gu"][1:], shard["ffn_w_down"][1:], shard["conv_w_in"][1:], shard["conv_w_out"][1:],
            shard["ssd_w_in"], shard["ssd_w_out"]]

    def finish(w, got):
        w = dict(w)
        w["ffn_w_gu"] = w["ffn_w_gu"] + [got[0][:, i] for i in range(DEPTH - 1)]
        w["ffn_w_down"] = w["ffn_w_down"] + [got[1][:, i].reshape(4, FF_BLOCK, D_MODEL) for i in range(DEPTH - 1)]
        w["conv_w_in"] = w["conv_w_in"] + [got[2][:, 0]]
        w["conv_w_out"] = w["conv_w_out"] + [got[3][:, 0].reshape(D_MODEL, D_MODEL)]
        w["ssd_w_in"] = jnp.pad(_cols_from_blocks(got[4][:, 0]), ((0, 0), (0, SSM_IN_PAD - SSM_IN)))
        w["ssd_w_out"] = got[5].reshape(SSM_INNER, D_MODEL)
        return w

    def early_slabs(g):
        return ([g["ffn_w_gu"][i] for i in range(1, DEPTH)]
                + [g["ffn_w_down"][i].reshape(NDEV, D_FF // NDEV, D_MODEL) for i in range(1, DEPTH)]
                + [g["conv_w_in"][1], g["conv_w_out"][1].reshape(NDEV, D_MODEL // NDEV, D_MODEL),
                   _blocks_from_cols(g["ssd_w_in"]), g["ssd_w_out"].reshape(NDEV, SSM_INNER // NDEV, D_MODEL)])

    def layer0_slabs(stage, g, g_down=None, g_gu=None, g_in=None, g_out=None):
        if stage == "ffn_gdown":
            return [_blocks_from_cols(g["fox_w_in"])]
        if stage == "ffn_dgu":
            return [g_down.reshape(NDEV, D_FF // NDEV, D_MODEL), g["fox_w_out"].reshape(NDEV, D_MODEL // NDEV, D_MODEL)]
        if stage == "ffn_dh":
            return [g_gu]
        if stage == "conv_dh":
            return [g_in, g_out.reshape(NDEV, D_MODEL // NDEV, D_MODEL)]
        return None

    loss_part, dx, grads, early, late = _local_step(x[0], loss_target[0], full, layer0, (rest, finish), early_slabs, layer0_slabs)

    se = [_sum_slabs(r, f"sum_early_{n}") for n, r in enumerate(early)]
    sl = {stage: [_sum_slabs(r, f"sum_{stage}_{n}") for n, r in enumerate(rs)] for stage, rs in late.items()}
    shard_grad = {
        "ffn_w_gu": jnp.stack(sl["ffn_dh"] + se[0:3]), "ffn_w_down": jnp.stack(sl["ffn_dgu"][0:1] + se[3:6]),
        "conv_w_in": jnp.stack([sl["conv_dh"][0], se[6]]), "conv_w_out": jnp.stack([sl["conv_dh"][1], se[7]]),
        "fox_w_in": sl["ffn_gdown"][0][None], "fox_w_out": sl["ffn_dgu"][1][None],
        "ssd_w_in": se[8][None], "ssd_w_out": se[9][None]}

    small_names = _REPLICATED + list(_VECTORS)
    small = [jnp.reshape(loss_part, (1,))] + [grads[k].reshape(-1) for k in small_names]
    total = _all_sum_small(_to_rows(jnp.concatenate(small)), "sum_small").reshape(-1)
    loss = total[0]
    off = 1
    me = _mesh_position()
    for k, part in zip(small_names, small[1:]):
        gk = total[off:off + part.shape[0]]
        off += part.shape[0]
        if k in _VECTORS:
            axis = _VECTORS[k]
            shp = local[k].shape
            gfull = gk.reshape(shp[:axis] + (NDEV, shp[axis]) + shp[axis + 1:])
            shard_grad[k] = lax.dynamic_index_in_dim(gfull, me, axis, keepdims=False)
        else:
            shard_grad[k] = gk.reshape(local[k].shape)

    deltas, new_m, new_v = {}, {}, {}
    for k in _NAMES:
        deltas[k], new_m[k], new_v[k] = _adamw(local[k], shard_grad[k], mom[k], var[k], f"adamw_{k}")
    return (loss, dx[None], *[shard_grad[k] for k in _NAMES], *[deltas[k] for k in _NAMES],
            *[new_m[k] for k in _NAMES], *[new_v[k] for k in _NAMES])
```

```python
import numpy as np

import jax
import jax.numpy as jnp
from jax import lax
from jax.experimental import pallas as pl
from jax.experimental.pallas import tpu as pltpu

F32 = jnp.float32
BF16 = jnp.bfloat16
HI = lax.Precision.HIGHEST

NDEV = 8
D_MODEL = 1024
DEPTH = 4
D_FF = 2816
FF_BLOCK = 2 * D_FF // NDEV
RMS_EPS = 1e-6
HEAD_DIM = 64
ATTN_HEADS = 16
FOX_IN = 3 * D_MODEL + ATTN_HEADS
FOX_IN_PAD = 3200
SSM_INNER = 2048
SSM_HEADS = 32
SSM_GROUPS = 8
SSM_STATE = 128
SSM_CHUNK = 128
SSM_CONV_DIM = 4096
SSM_IN = SSM_INNER + SSM_CONV_DIM + SSM_HEADS
SSM_IN_PAD = 6272
LANES = 128
V7X_VMEM_BYTES = 64 * 1024 * 1024
VMEM_LIMIT_BYTES = (V7X_VMEM_BYTES * 3) // 4
ATTN_BWD_VMEM_BYTES = (V7X_VMEM_BYTES * 7) // 8
LOG2E = 1.4426950408889634
LN2 = 0.6931471805599453
ATTN_TILE = 1024
ATTN_ROWS = 32
CUMSUM_ROWS = 512

ADAM_LR = 0.001
ADAM_B1 = 0.9
ADAM_B2 = 0.999
ADAM_EPS = 1e-08
ADAM_WD = 0.01
ADAM_STEP = 10

_TILE_CANDIDATES = (1024, 1408, 896, 768, 640, 512, 384, 256, 128)


def _pick_tile(n):
    for c in _TILE_CANDIDATES:
        if n % c == 0:
            return c
    raise ValueError(f"no tile for {n}")


def _params(ngrid):
    return pltpu.CompilerParams(dimension_semantics=("arbitrary",) * ngrid, vmem_limit_bytes=VMEM_LIMIT_BYTES)


def _pc(body, name, grid, in_specs, out_specs, out_shape, scratch=(), hosted=None):
    if hosted is None:
        return pl.pallas_call(
            body, name=name, grid=grid, in_specs=in_specs, out_specs=out_specs, out_shape=out_shape,
            scratch_shapes=list(scratch), compiler_params=_params(len(grid)))
    arrays, gather = hosted
    single = not isinstance(out_shape, (list, tuple))
    outs = [out_shape] if single else list(out_shape)
    ospecs = [out_specs] if single else list(out_specs)
    na, n_in, n_out, n_scr = len(arrays), len(in_specs), len(outs), len(scratch)
    pick, xouts, sems = _exchange_parts(arrays, gather)

    def run(*refs):
        ins, srcs = refs[:n_in], refs[n_in:n_in + na]
        res, dsts = refs[n_in + na:n_in + na + n_out], refs[n_in + na + n_out:n_in + 2 * na + n_out]
        scr, xsems = refs[n_in + 2 * na + n_out:n_in + 2 * na + n_out + n_scr], refs[n_in + 2 * na + n_out + n_scr:]
        first = pl.program_id(0) == 0
        last = pl.program_id(0) == grid[0] - 1
        for d in range(1, len(grid)):
            first = jnp.logical_and(first, pl.program_id(d) == 0)
            last = jnp.logical_and(last, pl.program_id(d) == grid[d] - 1)

        @pl.when(first)
        def _():
            _exchange_start(_exchange_copies(pick(srcs), dsts, *xsems))

        body(*ins, *res, *scr)

        @pl.when(last)
        def _():
            _exchange_wait(_exchange_copies(pick(srcs), dsts, *xsems))

    hbm = pl.BlockSpec(memory_space=pl.ANY)
    call = pl.pallas_call(
        run, name=name, grid=grid, in_specs=list(in_specs) + [hbm] * na, out_specs=ospecs + [hbm] * na,
        out_shape=outs + xouts, scratch_shapes=list(scratch) + sems, compiler_params=_params(len(grid)))
    return lambda *args: call(*args, *arrays)


def _dot(a, b, ca, cb, prec=None):
    return lax.dot_general(a, b, (((ca,), (cb,)), ((), ())), preferred_element_type=F32, precision=prec)


def _sds(shape, dtype=F32):
    return jax.ShapeDtypeStruct(shape, dtype)


def _row_tile(s, want=256):
    return want if s % want == 0 else s


def _sigmoid(x):
    return 1.0 / (1.0 + jnp.exp(-x))


def _softplus(x):
    return jnp.maximum(x, 0.0) + jnp.log(1.0 + jnp.exp(-jnp.abs(x)))


def _mm_spec(a, b, name, grid, a_spec, b_spec, o_spec, out, ca, cb, acc_shape, drop=(0, 0, 0), res=None, r_spec=None,
             norm_w=None, hosted=None):
    nk = grid[2]
    da, db, do_ = drop
    has_res = res is not None
    has_norm = norm_w is not None

    def body(*refs):
        refs = list(refs)
        a_ref, b_ref = refs[:2]
        r_ref = refs[2] if has_res else None
        w_ref = refs[2 + has_res] if has_norm else None
        o_ref = refs[2 + has_res + has_norm]
        h_ref = refs[3 + has_res + has_norm] if has_norm else None
        acc_ref = refs[-1]
        k = pl.program_id(2)

        @pl.when(k == 0)
        def _():
            acc_ref[...] = jnp.zeros_like(acc_ref)

        av = a_ref[(0,) * da] if da else a_ref[...]
        bv = b_ref[(0,) * db] if db else b_ref[...]
        acc_ref[...] += _dot(av.astype(BF16), bv.astype(BF16), ca, cb)

        @pl.when(k == nk - 1)
        def _():
            val = acc_ref[...]
            if has_res:
                val = val + r_ref[...]
            if do_:
                o_ref[(0,) * do_] = val.astype(out.dtype)
            else:
                o_ref[...] = val.astype(out.dtype)
            if has_norm:
                r = lax.rsqrt(jnp.mean(val * val, axis=-1, keepdims=True) + RMS_EPS)
                h_ref[...] = ((val * r) * w_ref[...]).astype(BF16)

    in_specs = [a_spec, b_spec] + ([r_spec] if has_res else [])
    args = (a, b) + ((res,) if has_res else ())
    out_specs, outs = o_spec, out
    if has_norm:
        assert acc_shape[1] == norm_w.shape[1] == out.shape[-1]
        in_specs.append(pl.BlockSpec((1, acc_shape[1]), lambda i, j, k: (0, 0)))
        args += (norm_w,)
        out_specs, outs = [o_spec, o_spec], [out, _sds(out.shape, BF16)]
    return _pc(body, name, grid, in_specs, out_specs, outs, [pltpu.VMEM(acc_shape, F32)], hosted=hosted)(*args)


def _mm(a, b, mode, name, out_dtype=F32, res=None, norm_w=None, hosted=None):
    if mode == "tn":
        r, m = a.shape
        n = b.shape[1]
        tm, tn, tk = _pick_tile(m), _pick_tile(n), _pick_tile(r)
        grid = (m // tm, n // tn, r // tk)
        a_spec = pl.BlockSpec((tk, tm), lambda i, j, k: (k, i))
        b_spec = pl.BlockSpec((tk, tn), lambda i, j, k: (k, j))
        ca, cb = 0, 0
    else:
        m, kd = a.shape
        n = b.shape[1] if mode == "nn" else b.shape[0]
        tm, tn, tk = _pick_tile(m), _pick_tile(n), _pick_tile(kd)
        grid = (m // tm, n // tn, kd // tk)
        a_spec = pl.BlockSpec((tm, tk), lambda i, j, k: (i, k))
        if mode == "nn":
            b_spec = pl.BlockSpec((tk, tn), lambda i, j, k: (k, j))
            ca, cb = 1, 0
        else:
            b_spec = pl.BlockSpec((tn, tk), lambda i, j, k: (j, k))
            ca, cb = 1, 1
    o_spec = pl.BlockSpec((tm, tn), lambda i, j, k: (i, j))
    return _mm_spec(a, b, name, grid, a_spec, b_spec, o_spec, _sds((m, n), out_dtype), ca, cb, (tm, tn), res=res, r_spec=o_spec,
                    norm_w=norm_w, hosted=hosted)


def _rms_fwd(x, w, name):
    s, d = x.shape
    ts = _row_tile(s)

    def body(x_ref, w_ref, o_ref):
        xv = x_ref[...]
        r = lax.rsqrt(jnp.mean(xv * xv, axis=-1, keepdims=True) + RMS_EPS)
        o_ref[...] = ((xv * r) * w_ref[...]).astype(BF16)

    row = pl.BlockSpec((ts, d), lambda i: (i, 0))
    return _pc(body, name, (s // ts,), [row, pl.BlockSpec((1, d), lambda i: (0, 0))], row, _sds((s, d), BF16))(x, w)


def _mm_dnorm(a, b, name, nk, a_spec, b_spec, ca, cb, drop, x, w, dres, hosted=None):
    s, d = x.shape
    tm = _pick_tile(s)
    da, db = drop

    def body(a_ref, b_ref, x_ref, w_ref, r_ref, dx_ref, dw_ref, acc_ref):
        i = pl.program_id(0)
        k = pl.program_id(2)

        @pl.when(k == 0)
        def _():
            acc_ref[...] = jnp.zeros_like(acc_ref)

        av = a_ref[(0,) * da] if da else a_ref[...]
        bv = b_ref[(0,) * db] if db else b_ref[...]
        acc_ref[...] += _dot(av.astype(BF16), bv.astype(BF16), ca, cb)

        @pl.when(k == nk - 1)
        def _():
            dhv = acc_ref[...]
            xv = x_ref[...]
            r = lax.rsqrt(jnp.mean(xv * xv, axis=-1, keepdims=True) + RMS_EPS)
            xhat = xv * r
            g = dhv * w_ref[...]
            dx_ref[...] = r_ref[...] + r * (g - xhat * jnp.mean(g * xhat, axis=-1, keepdims=True))
            part = jnp.sum(dhv * xhat, axis=0, keepdims=True)

            @pl.when(i == 0)
            def _():
                dw_ref[...] = part

            @pl.when(i > 0)
            def _():
                dw_ref[...] += part

    row = pl.BlockSpec((tm, d), lambda i, j, k: (i, 0))
    vec = pl.BlockSpec((1, d), lambda i, j, k: (0, 0))
    return list(_pc(body, name, (s // tm, 1, nk), [a_spec, b_spec, row, vec, row], [row, vec], [_sds((s, d)), _sds((1, d))],
                    [pltpu.VMEM((tm, d), F32)], hosted=hosted)(a, b, x, w, dres))


def _mm_dnorm_nt(dproj, w_in, name, x, w, dres, hosted=None):
    tm = _pick_tile(x.shape[0])
    tk = _pick_tile(dproj.shape[1])
    return _mm_dnorm(dproj, w_in, name, dproj.shape[1] // tk, pl.BlockSpec((tm, tk), lambda i, j, k: (i, k)),
                     pl.BlockSpec((D_MODEL, tk), lambda i, j, k: (0, k)), 1, 1, (0, 0), x, w, dres, hosted=hosted)


def _ffn_gate_up(h, w_gu, name, hosted=None):
    s = h.shape[0]
    tm = _pick_tile(s)

    def body(h_ref, wg_ref, wu_ref, gu_ref, a_ref):
        hv = h_ref[...]
        g = _dot(hv, wg_ref[0], 1, 0)
        u = _dot(hv, wu_ref[0], 1, 0)
        gu_ref[0, 0] = g.astype(BF16)
        gu_ref[0, 1] = u.astype(BF16)
        a_ref[0] = (g * _sigmoid(g) * u).astype(BF16)

    wblk = lambda off: pl.BlockSpec((1, D_MODEL, FF_BLOCK), lambda i, k: (k + off, 0, 0))
    return _pc(body, name, (s // tm, 4), [pl.BlockSpec((tm, D_MODEL), lambda i, k: (i, 0)), wblk(0), wblk(4)],
               [pl.BlockSpec((1, 2, tm, FF_BLOCK), lambda i, k: (k, 0, i, 0)), pl.BlockSpec((1, tm, FF_BLOCK), lambda i, k: (k, i, 0))],
               [_sds((4, 2, s, FF_BLOCK), BF16), _sds((4, s, FF_BLOCK), BF16)], hosted=hosted)(h, w_gu, w_gu)


def _stage_gather(stages, stage):
    return (stages[stage][0], True) if stages and stage in stages else None


def _stage_arrived(stages, stage, got, default=None):
    late = stages[stage][1](got) if stages and stage in stages else None
    return default if late is None else late


def _stage_slabs(hosted_fn, stage, **new):
    arrays = hosted_fn(stage, **new) if hosted_fn is not None else None
    return None if arrays is None else (arrays, False)


def _ffn_dgate_up(dy, w_down, gu, name, hosted=None):
    s = dy.shape[0]
    tm = _pick_tile(s)

    def body(dy_ref, w_ref, gu_ref, o_ref):
        dav = _dot(dy_ref[...].astype(BF16), w_ref[0], 1, 1)
        g = gu_ref[0, 0].astype(F32)
        u = gu_ref[0, 1].astype(F32)
        sg = _sigmoid(g)
        o_ref[0, 0] = (dav * u * (sg * (1.0 + g * (1.0 - sg)))).astype(BF16)
        o_ref[0, 1] = (dav * (g * sg)).astype(BF16)

    pair = pl.BlockSpec((1, 2, tm, FF_BLOCK), lambda i, k: (k, 0, i, 0))
    return _pc(body, name, (s // tm, 4),
               [pl.BlockSpec((tm, D_MODEL), lambda i, k: (i, 0)), pl.BlockSpec((1, FF_BLOCK, D_MODEL), lambda i, k: (k, 0, 0)), pair],
               pair, _sds((4, 2, s, FF_BLOCK), BF16), hosted=hosted)(dy, w_down, gu)


def _ffn_fwd(x, h, w_gu, w_down, tag, next_norm=None, stages=None):
    s = x.shape[0]
    tm = _pick_tile(s)
    gu, a, *got = _ffn_gate_up(h, w_gu, f"ffn_gu_{tag}", _stage_gather(stages, "ffn_gu"))
    w_down = _stage_arrived(stages, "ffn_gu", got, w_down)
    xspec = pl.BlockSpec((tm, D_MODEL), lambda i, j, k: (i, 0))
    hosted = _stage_gather(stages, "ffn_down")
    y = _mm_spec(a, w_down, f"ffn_down_{tag}", (s // tm, 1, 4),
                 pl.BlockSpec((1, tm, FF_BLOCK), lambda i, j, k: (k, i, 0)),
                 pl.BlockSpec((1, FF_BLOCK, D_MODEL), lambda i, j, k: (k, 0, 0)),
                 xspec, _sds((s, D_MODEL)), 1, 0, (tm, D_MODEL), drop=(1, 1, 0), res=x, r_spec=xspec, norm_w=next_norm,
                 hosted=hosted)
    n_own = 2 if next_norm is not None else 1
    own = list(y[:n_own]) if (hosted is not None or next_norm is not None) else [y]
    if hosted is not None:
        _stage_arrived(stages, "ffn_down", list(y[n_own:]))
    y, h_next = own if next_norm is not None else (own[0], None)
    return y, h_next, (x, h, gu, a)


def _ffn_bwd(dy, saved, norm_w, w_gu, w_down, tag, hosted_fn=None):
    x, h, gu, a = saved
    s = x.shape[0]
    tm = _pick_tile(s)
    got = {}
    hosted = _stage_slabs(hosted_fn, "ffn_gdown")
    g_down = _mm_spec(a, dy, f"ffn_gdown_{tag}", (4, 1, s // tm),
                      pl.BlockSpec((1, tm, FF_BLOCK), lambda i, j, k: (i, k, 0)),
                      pl.BlockSpec((tm, D_MODEL), lambda i, j, k: (k, 0)),
                      pl.BlockSpec((1, FF_BLOCK, D_MODEL), lambda i, j, k: (i, 0, 0)),
                      _sds((4, FF_BLOCK, D_MODEL), BF16), 0, 0, (FF_BLOCK, D_MODEL), drop=(1, 0, 1), hosted=hosted)
    if hosted is not None:
        g_down, *got["ffn_gdown"] = g_down
    hosted = _stage_slabs(hosted_fn, "ffn_dgu", g_down=g_down)
    dgu = _ffn_dgate_up(dy, w_down, gu, f"ffn_dgu_{tag}", hosted)
    if hosted is not None:
        dgu, *got["ffn_dgu"] = dgu
    g_gu = _mm_spec(h, dgu, f"ffn_ggu_{tag}", (NDEV, 1, s // tm),
                    pl.BlockSpec((tm, D_MODEL), lambda i, j, k: (k, 0)),
                    pl.BlockSpec((1, 1, tm, FF_BLOCK), lambda i, j, k: (i % 4, i // 4, k, 0)),
                    pl.BlockSpec((1, D_MODEL, FF_BLOCK), lambda i, j, k: (i, 0, 0)),
                    _sds((NDEV, D_MODEL, FF_BLOCK), BF16), 0, 0, (D_MODEL, FF_BLOCK), drop=(0, 2, 1))
    hosted = _stage_slabs(hosted_fn, "ffn_dh", g_gu=g_gu)
    dx, g_norm, *arrived = _mm_dnorm(dgu, w_gu, f"ffn_dh_{tag}", NDEV,
                                     pl.BlockSpec((1, 1, tm, FF_BLOCK), lambda i, j, k: (k % 4, k // 4, i, 0)),
                                     pl.BlockSpec((1, D_MODEL, FF_BLOCK), lambda i, j, k: (k, 0, 0)), 1, 1, (2, 1), x, norm_w, dy,
                                     hosted=hosted)
    if hosted is not None:
        got["ffn_dh"] = arrived
    return dx, g_norm, g_gu, g_down, got


def _prev_rows(cur, halo, j, first):
    rid = lax.broadcasted_iota(jnp.int32, cur.shape, 0)
    hid = lax.broadcasted_iota(jnp.int32, halo.shape, 0)
    out = pltpu.roll(cur, j, 0)
    for t in range(j):
        row = jnp.sum(jnp.where(hid == 8 - j + t, halo, 0.0), axis=0, keepdims=True)
        row = jnp.where(first, 0.0, row)
        out = jnp.where(rid == t, row, out)
    return out


def _next_rows(cur, halo, j, last):
    ts = cur.shape[0]
    rid = lax.broadcasted_iota(jnp.int32, cur.shape, 0)
    hid = lax.broadcasted_iota(jnp.int32, halo.shape, 0)
    out = pltpu.roll(cur, ts - j, 0)
    for t in range(j):
        row = jnp.sum(jnp.where(hid == t, halo, 0.0), axis=0, keepdims=True)
        row = jnp.where(last, 0.0, row)
        out = jnp.where(rid == ts - j + t, row, out)
    return out


def _halo_specs(ts, s, width, col):
    per = ts // 8
    nblk = s // 8
    prev = pl.BlockSpec((8, width), lambda i: (jnp.maximum(i * per - 1, 0), col))
    nxt = pl.BlockSpec((8, width), lambda i: (jnp.minimum((i + 1) * per, nblk - 1), col))
    return prev, nxt


def _cgate_fwd(p, w_dw, name, hosted=None):
    s = p.shape[0]
    d = D_MODEL
    ts = _row_tile(s)
    prev, _ = _halo_specs(ts, s, 3 * d, 0)

    def body(p_ref, h_ref, w_ref, z_ref):
        first = pl.program_id(0) == 0
        b = p_ref[:, :d]
        cv = p_ref[:, d:2 * d] * p_ref[:, 2 * d:]
        hcv = h_ref[:, d:2 * d] * h_ref[:, 2 * d:]
        u = w_ref[2:3, :] * cv + w_ref[1:2, :] * _prev_rows(cv, hcv, 1, first) + w_ref[0:1, :] * _prev_rows(cv, hcv, 2, first)
        z_ref[...] = (b * u).astype(BF16)

    return _pc(body, name, (s // ts,),
               [pl.BlockSpec((ts, 3 * d), lambda i: (i, 0)), prev, pl.BlockSpec((3, d), lambda i: (0, 0))],
               pl.BlockSpec((ts, d), lambda i: (i, 0)), _sds((s, d), BF16), hosted=hosted)(p, p, w_dw)


def _cgate_bwd(p, dz, w_dw, name):
    s = p.shape[0]
    d = D_MODEL
    ts = _row_tile(s)
    nt = s // ts
    p_prev, p_next = _halo_specs(ts, s, 3 * d, 0)
    _, dz_next = _halo_specs(ts, s, d, 0)

    def body(p_ref, hp_ref, hn_ref, dz_ref, dzn_ref, w_ref, dp_ref, dw_ref):
        i = pl.program_id(0)
        first = i == 0
        last = i == nt - 1
        b = p_ref[:, :d]
        c = p_ref[:, d:2 * d]
        v = p_ref[:, 2 * d:]
        cv = c * v
        hcv = hp_ref[:, d:2 * d] * hp_ref[:, 2 * d:]
        cv1 = _prev_rows(cv, hcv, 1, first)
        cv2 = _prev_rows(cv, hcv, 2, first)
        w0, w1, w2 = w_ref[0:1, :], w_ref[1:2, :], w_ref[2:3, :]
        u = w2 * cv + w1 * cv1 + w0 * cv2
        dzv = dz_ref[...]
        du = dzv * b
        dun = dzn_ref[...] * hn_ref[:, :d]
        dcv = w2 * du + w1 * _next_rows(du, dun, 1, last) + w0 * _next_rows(du, dun, 2, last)
        dp_ref[:, :d] = (dzv * u).astype(BF16)
        dp_ref[:, d:2 * d] = (dcv * v).astype(BF16)
        dp_ref[:, 2 * d:] = (dcv * c).astype(BF16)

        @pl.when(first)
        def _():
            dw_ref[...] = jnp.zeros_like(dw_ref)

        dw_ref[0:1, :] += jnp.sum(du * cv2, axis=0, keepdims=True)
        dw_ref[1:2, :] += jnp.sum(du * cv1, axis=0, keepdims=True)
        dw_ref[2:3, :] += jnp.sum(du * cv, axis=0, keepdims=True)

    wide = pl.BlockSpec((ts, 3 * d), lambda i: (i, 0))
    wspec = pl.BlockSpec((3, d), lambda i: (0, 0))
    return _pc(body, name, (nt,),
               [wide, p_prev, p_next, pl.BlockSpec((ts, d), lambda i: (i, 0)), dz_next, wspec],
               [wide, wspec], [_sds((s, 3 * d), BF16), _sds((3, d))])(p, p, p, dz, dz, w_dw)


def _conv_fwd(x, h, w_in, w_dw, w_out, tag, next_norm, stages=None):
    wn = _cols_from_blocks(w_in)
    hosted = _stage_gather(stages, "conv_in")
    p = _mm(h, wn, "nn", f"conv_in_{tag}", hosted=hosted)
    if hosted is not None:
        p, *got = p
        _stage_arrived(stages, "conv_in", got)
    hosted = _stage_gather(stages, "conv_gate")
    z = _cgate_fwd(p, w_dw, f"conv_gate_{tag}", hosted)
    if hosted is not None:
        z, *got = z
        w_out = _stage_arrived(stages, "conv_gate", got, w_out)
    y, h_next = _mm(z, w_out, "nn", f"conv_out_{tag}", res=x, norm_w=next_norm)
    return y, h_next, (x, h, p, z, wn)


def _conv_bwd(dy, saved, norm_w, w_in, w_dw, w_out, tag, hosted_fn=None):
    x, h, p, z, wn = saved
    dz = _mm(dy, w_out, "nt", f"conv_dz_{tag}")
    g_out = _mm(z, dy, "tn", f"conv_gout_{tag}", out_dtype=BF16)
    dp, g_dw = _cgate_bwd(p, dz, w_dw, f"conv_dgate_{tag}")
    g_in = _blocks_from_cols(_mm(h, dp, "tn", f"conv_gin_{tag}", out_dtype=BF16))
    hosted = _stage_slabs(hosted_fn, "conv_dh", g_in=g_in, g_out=g_out)
    dx, g_norm, *got = _mm_dnorm_nt(dp, wn, f"conv_dh_{tag}", x, norm_w, dy, hosted=hosted)
    return dx, g_norm, g_in, g_dw, g_out, ({"conv_dh": got} if hosted is not None else {})


def _tri(lower, n=LANES):
    r = lax.broadcasted_iota(jnp.int32, (n, n), 0)
    c = lax.broadcasted_iota(jnp.int32, (n, n), 1)
    return jnp.where((r >= c) if lower else (r <= c), 1.0, 0.0).astype(F32)


def _cumsum_rows(v, name):
    s = v.shape[0]
    rows = _row_tile(s, CUMSUM_ROWS)

    def body(v_ref, o_ref, carry_ref):
        @pl.when(pl.program_id(0) == 0)
        def _():
            carry_ref[...] = jnp.zeros_like(carry_ref)

        blk = v_ref[...]
        o_ref[...] = _dot(_tri(True, rows), blk, 1, 0, HI) + carry_ref[0:1, :]
        carry_ref[...] += jnp.sum(blk, axis=0, keepdims=True)

    spec = pl.BlockSpec((rows, LANES), lambda i: (i, 0))
    return _pc(body, name, (s // rows,), [spec], spec, _sds((s, LANES)), [pltpu.VMEM((8, LANES), F32)])(v)


def _fox_dcum(dck, dcq, name):
    s = dck.shape[0]
    rows = _row_tile(s, CUMSUM_ROWS)
    n = s // rows

    def body(k_ref, q_ref, o_ref, carry_ref):
        @pl.when(pl.program_id(0) == 0)
        def _():
            carry_ref[...] = jnp.zeros_like(carry_ref)

        head = lax.broadcasted_iota(jnp.int32, (ATTN_HEADS, LANES), 0)
        unit = jnp.where(head == lax.broadcasted_iota(jnp.int32, (ATTN_HEADS, LANES), 1), 1.0, 0.0).astype(F32)
        blk = _dot(q_ref[...], unit, 0, 0, HI)
        lane = lax.broadcasted_iota(jnp.int32, (rows, LANES), 1)
        for hd in range(ATTN_HEADS):
            first_lane = hd * HEAD_DIM
            pair = k_ref[:, first_lane // LANES * LANES:(first_lane // LANES + 1) * LANES]
            blk = blk + jnp.where(lane == hd, pltpu.roll(pair, (hd - first_lane) % LANES, axis=1), 0.0)
        o_ref[...] = _dot(_tri(False, rows), blk, 1, 0, HI) + carry_ref[0:1, :]
        carry_ref[...] += jnp.sum(blk, axis=0, keepdims=True)

    return _pc(body, name, (n,),
               [pl.BlockSpec((rows, D_MODEL), lambda i: (n - 1 - i, 0)), pl.BlockSpec((ATTN_HEADS, rows), lambda i: (0, n - 1 - i))],
               pl.BlockSpec((rows, LANES), lambda i: (n - 1 - i, 0)), _sds((s, LANES)), [pltpu.VMEM((8, LANES), F32)])(dck, dcq)


def _lo_mask(shape):
    return lax.broadcasted_iota(jnp.int32, shape, len(shape) - 1) < HEAD_DIM


def _half_sums(v, lo):
    sa = jnp.sum(jnp.where(lo, v, 0.0), axis=-1, keepdims=True)
    sb = jnp.sum(jnp.where(lo, 0.0, v), axis=-1, keepdims=True)
    return jnp.where(lo, sa, sb)


def _fox_prep_fwd(proj, gq, gk, name):
    s = proj.shape[0]
    ts = _row_tile(s)
    qscale = HEAD_DIM ** -0.5 * LOG2E

    def body(q_ref, k_ref, v_ref, gq_ref, gk_ref, qo_ref, ko_ref, vo_ref):
        lo = _lo_mask((ts, LANES))

        def hnorm(xv, g):
            ms = _half_sums(xv * xv, lo) * (1.0 / HEAD_DIM)
            return (xv * lax.rsqrt(ms + RMS_EPS)) * g

        for p in range(8):
            cols = slice(p * LANES, (p + 1) * LANES)
            qo_ref[:, cols] = (hnorm(q_ref[:, cols], gq_ref[...]) * qscale).astype(BF16)
            ko_ref[:, cols] = hnorm(k_ref[:, cols], gk_ref[...]).astype(BF16)
        vo_ref[...] = v_ref[...].astype(BF16)

    def wide(blk):
        return pl.BlockSpec((ts, D_MODEL), lambda i: (i, blk))

    gspec = pl.BlockSpec((1, LANES), lambda i: (0, 0))
    out = _sds((s, D_MODEL), BF16)
    return _pc(body, name, (s // ts,), [wide(0), wide(1), wide(2), gspec, gspec], [wide(0)] * 3, [out] * 3)(
        proj, proj, proj, gq, gk)


def _fox_logf(proj, bf, name):
    s = proj.shape[0]
    ts = _row_tile(s, 512)

    def body(f_ref, b_ref, o_ref):
        z = f_ref[...] + b_ref[...]
        lf = jnp.minimum(z, 0.0) - jnp.log(1.0 + jnp.exp(-jnp.abs(z)))
        real = lax.broadcasted_iota(jnp.int32, (ts, LANES), 1) < ATTN_HEADS
        o_ref[...] = jnp.where(real, lf, 0.0)

    return _pc(body, name, (s // ts,), [pl.BlockSpec((ts, LANES), lambda i: (i, 24)), pl.BlockSpec((1, LANES), lambda i: (0, 0))],
               pl.BlockSpec((ts, LANES), lambda i: (i, 0)), _sds((s, LANES)))(proj, bf)


def _fox_dlogf(proj, bf, dlf, name):
    s = proj.shape[0]
    ts = _row_tile(s, 512)

    def body(f_ref, b_ref, d_ref, o_ref, db_ref):
        z = f_ref[...] + b_ref[...]
        real = lax.broadcasted_iota(jnp.int32, (ts, LANES), 1) < ATTN_HEADS
        g = jnp.where(real, d_ref[...] * _sigmoid(-z), 0.0)
        o_ref[...] = g.astype(BF16)

        @pl.when(pl.program_id(0) == 0)
        def _():
            db_ref[...] = jnp.zeros_like(db_ref)

        db_ref[...] += jnp.sum(g, axis=0, keepdims=True)

    vec = pl.BlockSpec((1, LANES), lambda i: (0, 0))
    row = pl.BlockSpec((ts, LANES), lambda i: (i, 0))
    return _pc(body, name, (s // ts,), [pl.BlockSpec((ts, LANES), lambda i: (i, 24)), vec, row], [row, vec],
               [_sds((s, LANES), BF16), _sds((1, LANES))])(proj, bf, dlf)


def _decay_placement():
    pq = np.zeros((3 * LANES, D_MODEL), np.float32)
    pk = np.zeros((3 * LANES, D_MODEL), np.float32)
    oq = np.zeros((1, D_MODEL), np.float32)
    ok = np.zeros((1, D_MODEL), np.float32)
    for hd in range(ATTN_HEADS):
        base = (hd // 2) * LANES + (0 if hd % 2 else HEAD_DIM)
        for term in range(3):
            pq[term * LANES + hd, base + term] = 1.0
            pk[term * LANES + hd, base + 3 + term] = -1.0
        oq[0, base + 3:base + 6] = 1.0
        ok[0, base:base + 3] = 1.0
    return jnp.asarray(pq, BF16), jnp.asarray(pk, BF16), jnp.asarray(oq), jnp.asarray(ok)


def _decay_terms(cum, name):
    s = cum.shape[0]
    ts = _row_tile(s)

    def body(c_ref, pq_ref, pk_ref, oq_ref, ok_ref, aq_ref, ak_ref):
        c2 = c_ref[...] * LOG2E
        hi = c2.astype(BF16)
        rest = c2 - hi.astype(F32)
        mid = rest.astype(BF16)
        low = (rest - mid.astype(F32)).astype(BF16)
        terms = jnp.concatenate([hi, mid, low], axis=1)
        aq_ref[...] = (_dot(terms, pq_ref[...], 1, 0) + oq_ref[...]).astype(BF16)
        ak_ref[...] = (_dot(terms, pk_ref[...], 1, 0) + ok_ref[...]).astype(BF16)

    mat = pl.BlockSpec((3 * LANES, D_MODEL), lambda i: (0, 0))
    row = pl.BlockSpec((1, D_MODEL), lambda i: (0, 0))
    out = pl.BlockSpec((ts, D_MODEL), lambda i: (i, 0))
    return _pc(body, name, (s // ts,), [pl.BlockSpec((ts, LANES), lambda i: (i, 0)), mat, mat, row, row], [out, out],
               [_sds((s, D_MODEL), BF16), _sds((s, D_MODEL), BF16)])(cum, *_decay_placement())


def _attn_tiles(s):
    t = s
    for cand in (ATTN_TILE, ATTN_TILE // 2):
        if s % cand == 0:
            t = cand
            break
    return t, s // t


def _tri_steps(n, by_key):
    if by_key:
        pairs = [(q, k) for k in range(n) for q in range(k, n)]
    else:
        pairs = [(q, k) for q in range(n) for k in range(q + 1)]
    arr = np.asarray(pairs, np.int32)
    return jnp.asarray(arr[:, 0]), jnp.asarray(arr[:, 1])


def _attn_call(body, name, s, by_key, inputs, in_kinds, out_kinds, out_shapes, scratch, hosted=None, vmem=VMEM_LIMIT_BYTES):
    t, n = _attn_tiles(s)
    qi_arr, ki_arr = _tri_steps(n, by_key)
    nsteps = int(qi_arr.shape[0])
    specs = {
        "q": pl.BlockSpec((t, LANES), lambda p, i, qi, ki: (qi[i], p)),
        "k": pl.BlockSpec((t, LANES), lambda p, i, qi, ki: (ki[i], p)),
        "r": pl.BlockSpec((1, 2, t), lambda p, i, qi, ki: (p, 0, qi[i])),
        "m": pl.BlockSpec((1, t, t), lambda p, i, qi, ki: (jnp.where(qi[i] == ki[i], 1, 0), 0, 0)),
        "Q": pl.BlockSpec((1, LANES, s), lambda p, i, qi, ki: (p, 0, 0)),
        "R": pl.BlockSpec((1, 2, s), lambda p, i, qi, ki: (p, 0, 0)),
    }
    in_specs = [specs[c] for c in in_kinds]
    out_specs = [specs[c] for c in out_kinds]
    out_shapes, scratch, inputs = list(out_shapes), list(scratch), list(inputs)
    run = body
    if hosted is not None:
        arrays, gather = hosted
        na, n_in, n_out, n_scr = len(arrays), len(inputs), len(out_kinds), len(scratch)
        pick, xouts, sems = _exchange_parts(arrays, gather)

        def run(qi_ref, ki_ref, *refs):
            ins, srcs = refs[:n_in], refs[n_in:n_in + na]
            outs, dsts = refs[n_in + na:n_in + na + n_out], refs[n_in + na + n_out:n_in + 2 * na + n_out]
            scr, xsems = refs[n_in + 2 * na + n_out:n_in + 2 * na + n_out + n_scr], refs[n_in + 2 * na + n_out + n_scr:]
            p = pl.program_id(0)
            i = pl.program_id(1)

            @pl.when(jnp.logical_and(p == 0, i == 0))
            def _():
                _exchange_start(_exchange_copies(pick(srcs), dsts, *xsems))

            body(qi_ref, ki_ref, *ins, *outs, *scr)

            @pl.when(jnp.logical_and(p == 7, i == nsteps - 1))
            def _():
                _exchange_wait(_exchange_copies(pick(srcs), dsts, *xsems))

        hbm = pl.BlockSpec(memory_space=pl.ANY)
        in_specs += [hbm] * na
        out_specs += [hbm] * na
        out_shapes += xouts
        scratch += sems
        inputs += list(arrays)
    grid_spec = pltpu.PrefetchScalarGridSpec(
        num_scalar_prefetch=2, grid=(8, nsteps), in_specs=in_specs, out_specs=out_specs, scratch_shapes=scratch)
    params = pltpu.CompilerParams(dimension_semantics=("arbitrary", "arbitrary"), vmem_limit_bytes=vmem)
    return pl.pallas_call(run, name=name, grid_spec=grid_spec, out_shape=out_shapes, compiler_params=params)(
        qi_arr, ki_arr, *inputs)


def _biased_kq(q2, k2, aq, ak, lo):
    sa = _dot(jnp.where(lo, k2, ak), jnp.where(lo, q2, aq), 1, 1)
    sb = _dot(jnp.where(lo, ak, k2), jnp.where(lo, aq, q2), 1, 1)
    return sa, sb


def _causal_bias(s):
    t, _ = _attn_tiles(s)
    kid = lax.broadcasted_iota(jnp.int32, (t, t), 0)
    qid = lax.broadcasted_iota(jnp.int32, (t, t), 1)
    return jnp.stack([jnp.zeros((t, t), BF16), jnp.where(kid > qid, -jnp.inf, 0.0).astype(BF16)])


def _fold8(v, op):
    return op(v.reshape(v.shape[0] // 8, 8, v.shape[1]), axis=0)


def _chunk(ref, mask_ref, hd, r):
    rows = slice(r * ATTN_ROWS, (r + 1) * ATTN_ROWS)
    return rows, ref[hd, rows, :] + mask_ref[0, rows, :].astype(F32)


def _flash_fwd(qs, kn, vb, augq, augk, cmask, name, hosted=None):
    s = qs.shape[0]
    t, n = _attn_tiles(s)
    nch = t // ATTN_ROWS

    def body(qi_ref, ki_ref, q_ref, k_ref, v_ref, aq_ref, ak_ref, mk_ref, o_ref, lse_ref, s_ref, p_ref, m_ref, l_ref, acc_ref):
        i = pl.program_id(1)
        qi = qi_ref[i]
        ki = ki_ref[i]

        @pl.when(ki == 0)
        def _():
            m_ref[...] = jnp.full_like(m_ref, -jnp.inf)
            l_ref[...] = jnp.zeros_like(l_ref)
            acc_ref[...] = jnp.zeros_like(acc_ref)

        lo = _lo_mask((t, LANES))
        rowlo = lax.broadcasted_iota(jnp.int32, (LANES, t), 0) < HEAD_DIM
        v2 = v_ref[...]
        sa, sb = _biased_kq(q_ref[...], k_ref[...], aq_ref[...], ak_ref[...], lo)
        s_ref[0] = sa
        s_ref[1] = sb
        alphas, pvs = [], []
        for hd in range(2):
            mx = jnp.full((8, t), -jnp.inf, F32)
            for r in range(nch):
                _, sc = _chunk(s_ref, mk_ref, hd, r)
                mx = jnp.maximum(mx, _fold8(sc, jnp.max))
            m_prev = m_ref[hd:hd + 1, :]
            m_new = jnp.maximum(m_prev, jnp.max(mx, axis=0, keepdims=True))
            ls = jnp.zeros((8, t), F32)
            for r in range(nch):
                rows, sc = _chunk(s_ref, mk_ref, hd, r)
                pm = jnp.exp2(sc - m_new)
                ls = ls + _fold8(pm, jnp.sum)
                p_ref[hd, rows, :] = pm.astype(BF16)
            alpha = jnp.exp2(m_prev - m_new)
            l_ref[hd:hd + 1, :] = alpha * l_ref[hd:hd + 1, :] + jnp.sum(ls, axis=0, keepdims=True)
            m_ref[hd:hd + 1, :] = m_new
            alphas.append(alpha)
            pvs.append(_dot(v2, p_ref[hd], 0, 0))
        acc_ref[...] = jnp.where(rowlo, alphas[0], alphas[1]) * acc_ref[...] + jnp.where(rowlo, pvs[0], pvs[1])

        @pl.when(ki == qi)
        def _():
            o_ref[...] = (acc_ref[...] / jnp.where(rowlo, l_ref[0:1, :], l_ref[1:2, :])).T
            lse_ref[0] = m_ref[0:2, :] + jnp.log2(l_ref[0:2, :])

    stat = pltpu.VMEM((8, t), F32)
    return _attn_call(body, name, s, False, (qs, kn, vb, augq, augk, cmask), "qkkqkm", "qr",
                      [_sds((s, D_MODEL)), _sds((8, 2, s))],
                      [pltpu.VMEM((2, t, t), F32), pltpu.VMEM((2, t, t), BF16), stat, stat, pltpu.VMEM((LANES, t), F32)],
                      hosted=hosted)


def _fox_delta(do, o, name):
    s = do.shape[0]
    ts = _row_tile(s)

    def body(do_ref, o_ref, d_ref):
        head = lax.broadcasted_iota(jnp.int32, (ATTN_HEADS, D_MODEL), 0)
        col = lax.broadcasted_iota(jnp.int32, (ATTN_HEADS, D_MODEL), 1)
        member = jnp.where(col // HEAD_DIM == head, 1.0, 0.0).astype(F32)
        d_ref[...] = _dot(member, do_ref[...] * o_ref[...], 1, 1, HI)

    spec = pl.BlockSpec((ts, D_MODEL), lambda i: (i, 0))
    return _pc(body, name, (s // ts,), [spec, spec], pl.BlockSpec((ATTN_HEADS, ts), lambda i: (0, i)), _sds((ATTN_HEADS, s)))(do, o)


def _bwd_tile(q_ref, k_ref, v_ref, aq_ref, ak_ref, do_ref, s_ref, dp_ref, lo):
    do2 = do_ref[...].astype(BF16)
    zero = jnp.zeros_like(do2)
    v2 = v_ref[...]
    sa, sb = _biased_kq(q_ref[...], k_ref[...], aq_ref[...], ak_ref[...], lo)
    s_ref[0] = sa
    s_ref[1] = sb
    dp_ref[0] = _dot(v2, jnp.where(lo, do2, zero), 1, 1)
    dp_ref[1] = _dot(v2, jnp.where(lo, zero, do2), 1, 1)
    return do2


def _bwd_chunk(s_ref, dp_ref, mk_ref, lse_ref, dl_ref, hd, r):
    rows, sc = _chunk(s_ref, mk_ref, hd, r)
    pm = jnp.exp2(sc - lse_ref[0, hd:hd + 1, :])
    ds = pm * (dp_ref[hd, rows, :] - dl_ref[0, hd:hd + 1, :])
    return rows, pm, ds


def _flash_bwd(qs, kn, vb, augq, augk, cmask, do, lse, delta, name, hosted=None):
    s = qs.shape[0]
    t, n = _attn_tiles(s)
    nch = t // ATTN_ROWS

    def body(qi_ref, ki_ref, q_ref, k_ref, v_ref, aq_ref, ak_ref, mk_ref, do_ref, lse_ref, dl_ref,
             dk_ref, dv_ref, dc_ref, dq_ref, dcq_ref, s_ref, dp_ref, p_ref, ds_ref, dka_ref, dva_ref, dca_ref):
        i = pl.program_id(1)
        qi = qi_ref[i]
        ki = ki_ref[i]

        @pl.when(i == 0)
        def _():
            dq_ref[...] = jnp.zeros_like(dq_ref)
            dcq_ref[...] = jnp.zeros_like(dcq_ref)

        @pl.when(qi == ki)
        def _():
            dka_ref[...] = jnp.zeros_like(dka_ref)
            dva_ref[...] = jnp.zeros_like(dva_ref)
            dca_ref[...] = jnp.zeros_like(dca_ref)

        lo = _lo_mask((t, LANES))
        rowlo = lax.broadcasted_iota(jnp.int32, (LANES, t), 0) < HEAD_DIM
        do2 = _bwd_tile(q_ref, k_ref, v_ref, aq_ref, ak_ref, do_ref, s_ref, dp_ref, lo)
        q2 = q_ref[...]
        k2 = k_ref[...]
        qcols = pl.ds(pl.multiple_of(qi * t, t), t)
        dvs, dks, dqs = [], [], []
        for hd in range(2):
            rs = jnp.zeros((8, t), F32)
            for r in range(nch):
                rows, pm, ds = _bwd_chunk(s_ref, dp_ref, mk_ref, lse_ref, dl_ref, hd, r)
                rs = rs + _fold8(ds, jnp.sum)
                part = ds[:, 0:LANES]
                for c in range(1, t // LANES):
                    part = part + ds[:, c * LANES:(c + 1) * LANES]
                dca_ref[hd, rows, :] += part
                p_ref[hd, rows, :] = pm.astype(BF16)
                ds_ref[hd, rows, :] = ds.astype(BF16)
            dcq_ref[0, hd:hd + 1, qcols] += jnp.sum(rs, axis=0, keepdims=True)
            dvs.append(_dot(p_ref[hd], do2, 1, 0))
            dks.append(_dot(ds_ref[hd], q2, 1, 0))
            dqs.append(_dot(k2, ds_ref[hd], 0, 0))
        dva_ref[...] += jnp.where(lo, dvs[0], dvs[1])
        dka_ref[...] += jnp.where(lo, dks[0], dks[1])
        dq_ref[0, :, qcols] += jnp.where(rowlo, dqs[0], dqs[1])

        @pl.when(qi == n - 1)
        def _():
            dk_ref[...] = dka_ref[...] * LN2
            dv_ref[...] = dva_ref[...]
            dc_ref[...] = -jnp.where(lo, jnp.sum(dca_ref[0], axis=-1, keepdims=True), jnp.sum(dca_ref[1], axis=-1, keepdims=True))

    out = _sds((s, D_MODEL))
    return _attn_call(body, name, s, True, (qs, kn, vb, augq, augk, cmask, do, lse, delta), "qkkqkmqrr", "kkkQR",
                      [out, out, out, _sds((8, LANES, s)), _sds((8, 2, s))],
                      [pltpu.VMEM((2, t, t), F32), pltpu.VMEM((2, t, t), F32), pltpu.VMEM((2, t, t), BF16),
                       pltpu.VMEM((2, t, t), BF16), pltpu.VMEM((t, LANES), F32), pltpu.VMEM((t, LANES), F32),
                       pltpu.VMEM((2, t, LANES), F32)], hosted=hosted, vmem=ATTN_BWD_VMEM_BYTES)


def _fox_prep_bwd(proj, dqs, dk, dv, gq, gk, name):
    s = proj.shape[0]
    ts = _row_tile(s)
    scale = HEAD_DIM ** -0.5

    def body(q_ref, k_ref, dq_ref, dk_ref, dv_ref, gq_ref, gk_ref, oq_ref, ok_ref, ov_ref, dgq_ref, dgk_ref):
        lo = _lo_mask((ts, LANES))

        @pl.when(pl.program_id(0) == 0)
        def _():
            dgq_ref[...] = jnp.zeros_like(dgq_ref)
            dgk_ref[...] = jnp.zeros_like(dgk_ref)

        def back(xv, dout, g):
            r = lax.rsqrt(_half_sums(xv * xv, lo) * (1.0 / HEAD_DIM) + RMS_EPS)
            y = xv * r
            dy = dout * g
            dx = r * (dy - y * (_half_sums(dy * y, lo) * (1.0 / HEAD_DIM)))
            return dx, jnp.sum(dout * y, axis=0, keepdims=True)

        for p in range(8):
            cols = slice(p * LANES, (p + 1) * LANES)
            dxq, dgq = back(q_ref[:, cols], dq_ref[p].T * scale, gq_ref[...])
            dxk, dgk = back(k_ref[:, cols], dk_ref[:, cols], gk_ref[...])
            oq_ref[:, cols] = dxq.astype(BF16)
            ok_ref[:, cols] = dxk.astype(BF16)
            dgq_ref[...] += dgq
            dgk_ref[...] += dgk
        ov_ref[...] = dv_ref[...].astype(BF16)

    def wide(blk):
        return pl.BlockSpec((ts, D_MODEL), lambda i: (i, blk))

    gspec = pl.BlockSpec((1, LANES), lambda i: (0, 0))
    out = _sds((s, D_MODEL), BF16)
    dqt = pl.BlockSpec((8, LANES, ts), lambda i: (0, 0, i))
    return _pc(body, name, (s // ts,), [wide(0), wide(1), dqt, wide(0), wide(0), gspec, gspec],
               [wide(0)] * 3 + [gspec] * 2, [out] * 3 + [_sds((1, LANES))] * 2)(proj, proj, dqs, dk, dv, gq, gk)


def _fox_fwd(x, h, w_in, b_f, q_gain, k_gain, w_out, next_norm, hosted=None):
    proj = _mm(h, w_in, "nn", "fox_in")
    gq = jnp.tile(q_gain, (1, 2))
    gk = jnp.tile(k_gain, (1, 2))
    bf = jnp.pad(b_f, ((0, 0), (0, LANES - ATTN_HEADS)))
    qs, kn, vb = _fox_prep_fwd(proj, gq, gk, "fox_prep")
    augq, augk = _decay_terms(_cumsum_rows(_fox_logf(proj, bf, "fox_logf"), "fox_cum"), "fox_decay")
    cmask = _causal_bias(x.shape[0])
    o, lse, *got = _flash_fwd(qs, kn, vb, augq, augk, cmask, "fox_attn", hosted=hosted)
    y, h_next = _mm(o, w_out, "nn", "fox_out", res=x, norm_w=next_norm)
    return y, h_next, (x, h, proj, gq, gk, bf, qs, kn, vb, augq, augk, cmask, o, lse), got


def _fox_bwd(dy, saved, norm_w, w_in, w_out, hosted=None):
    x, h, proj, gq, gk, bf, qs, kn, vb, augq, augk, cmask, o, lse = saved
    s = x.shape[0]
    do = _mm(dy, w_out, "nt", "fox_do")
    g_out = _mm(o, dy, "tn", "fox_gout", out_dtype=BF16)
    delta = _fox_delta(do, o, "fox_delta").reshape(8, 2, s)
    dk, dv, dck, dqs, dcq, *got = _flash_bwd(qs, kn, vb, augq, augk, cmask, do, lse, delta, "fox_dattn", hosted=hosted)
    dlf = _fox_dcum(dck, dcq.reshape(ATTN_HEADS, s), "fox_dcum")
    dfl, g_bf = _fox_dlogf(proj, bf, dlf, "fox_dlogf")
    dq_o, dk_o, dv_o, g_gq, g_gk = _fox_prep_bwd(proj, dqs, dk, dv, gq, gk, "fox_dprep")
    dproj = jnp.concatenate([dq_o, dk_o, dv_o, dfl], axis=1)
    g_in = _mm(h, dproj, "tn", "fox_gin", out_dtype=BF16)
    dx, g_norm = _mm_dnorm_nt(dproj, w_in, "fox_dh", x, norm_w, dy)
    g_q = g_gq[:, :HEAD_DIM] + g_gq[:, HEAD_DIM:]
    g_k = g_gk[:, :HEAD_DIM] + g_gk[:, HEAD_DIM:]
    return dx, g_norm, g_in[:, :FOX_IN], g_bf[:, :ATTN_HEADS], g_q, g_k, g_out, got


def _ssd_conv_fwd(proj, cw, cb, name):
    s = proj.shape[0]
    ts = _row_tile(s)
    w = 1024
    per = ts // 8

    def body(p_ref, h_ref, w_ref, b_ref, o_ref):
        first = pl.program_id(0) == 0
        cur = p_ref[...]
        halo = h_ref[...]
        u = w_ref[3:4, :] * cur + b_ref[...]
        for j in range(1, 4):
            u = u + w_ref[3 - j:4 - j, :] * _prev_rows(cur, halo, j, first)
        o_ref[...] = u * _sigmoid(u)

    return _pc(body, name, (s // ts, 4),
               [pl.BlockSpec((ts, w), lambda i, j: (i, 2 + j)),
                pl.BlockSpec((8, w), lambda i, j: (jnp.maximum(i * per - 1, 0), 2 + j)),
                pl.BlockSpec((4, w), lambda i, j: (0, j)), pl.BlockSpec((1, w), lambda i, j: (0, j))],
               pl.BlockSpec((ts, w), lambda i, j: (i, j)), _sds((s, SSM_CONV_DIM)))(proj, proj, cw, cb)


def _ssd_conv_bwd(proj, d, first_col, cw, cb, name):
    s = proj.shape[0]
    ts = _row_tile(s)
    nt = s // ts
    w = 1024
    ncol = d.shape[1] // w
    per = ts // 8
    nblk = s // 8

    def body(p_ref, hp_ref, hn_ref, d_ref, dn_ref, w_ref, b_ref, o_ref, dw_ref, db_ref):
        i = pl.program_id(1)
        first = i == 0
        last = i == nt - 1
        cur = p_ref[...]
        prev = [cur] + [_prev_rows(cur, hp_ref[...], j, first) for j in range(1, 4)]
        nxt = hn_ref[...]
        tail = cur[ts - 8:, :]
        u = b_ref[...]
        un = b_ref[...]
        for j in range(4):
            u = u + w_ref[3 - j:4 - j, :] * prev[j]
            un = un + w_ref[3 - j:4 - j, :] * (nxt if j == 0 else _prev_rows(nxt, tail, j, False))
        sg = _sigmoid(u)
        g = d_ref[...] * (sg * (1.0 + u * (1.0 - sg)))
        sn = _sigmoid(un)
        gn = dn_ref[...] * (sn * (1.0 + un * (1.0 - sn)))

        @pl.when(first)
        def _():
            dw_ref[...] = jnp.zeros_like(dw_ref)
            db_ref[...] = jnp.zeros_like(db_ref)

        dpre = w_ref[3:4, :] * g
        for j in range(1, 4):
            dpre = dpre + w_ref[3 - j:4 - j, :] * _next_rows(g, gn, j, last)
        for j in range(4):
            dw_ref[3 - j:4 - j, :] += jnp.sum(g * prev[j], axis=0, keepdims=True)
        db_ref[...] += jnp.sum(g, axis=0, keepdims=True)
        o_ref[...] = dpre.astype(BF16)

    tile = pl.BlockSpec((ts, w), lambda j, i: (i, j))
    wspec = lambda off: pl.BlockSpec((4, w), lambda j, i: (0, off + j))
    vec = lambda off: pl.BlockSpec((1, w), lambda j, i: (0, off + j))
    nxt_blk = lambda off: pl.BlockSpec((8, w), lambda j, i: (jnp.minimum((i + 1) * per, nblk - 1), off + j))
    in_proj = 2 + first_col
    return _pc(body, name, (ncol, nt),
               [pl.BlockSpec((ts, w), lambda j, i: (i, in_proj + j)),
                pl.BlockSpec((8, w), lambda j, i: (jnp.maximum(i * per - 1, 0), in_proj + j)), nxt_blk(in_proj),
                tile, nxt_blk(0), wspec(first_col), vec(first_col)],
               [tile, wspec(0), vec(0)], [_sds((s, ncol * w), BF16), _sds((4, ncol * w)), _sds((1, ncol * w))])(
                   proj, proj, proj, d, d, cw, cb)


def _ssd_dt_fwd(proj, bias, a_neg, name):
    s = proj.shape[0]
    n = s // SSM_CHUNK

    def body(r_ref, b_ref, a_ref, dt_ref, ac_ref):
        real = lax.broadcasted_iota(jnp.int32, (SSM_CHUNK, LANES), 1) < SSM_HEADS
        dt = jnp.where(real, _softplus(r_ref[...] + b_ref[...]), 0.0)
        dt_ref[...] = dt
        ac_ref[...] = _dot(_tri(True), dt * a_ref[...], 1, 0, HI)

    vec = pl.BlockSpec((1, LANES), lambda c: (0, 0))
    row = pl.BlockSpec((SSM_CHUNK, LANES), lambda c: (c, 0))
    return _pc(body, name, (n,), [pl.BlockSpec((SSM_CHUNK, LANES), lambda c: (c, 48)), vec, vec], [row, row],
               [_sds((s, LANES)), _sds((s, LANES))])(proj, bias, a_neg)


def _ssd_dt_bwd(proj, bias, ddt, name):
    s = proj.shape[0]
    ts = _row_tile(s, 512)

    def body(r_ref, b_ref, d_ref, o_ref, db_ref):
        real = lax.broadcasted_iota(jnp.int32, (ts, LANES), 1) < SSM_HEADS
        g = jnp.where(real, d_ref[...] * _sigmoid(r_ref[...] + b_ref[...]), 0.0)
        o_ref[...] = g.astype(BF16)

        @pl.when(pl.program_id(0) == 0)
        def _():
            db_ref[...] = jnp.zeros_like(db_ref)

        db_ref[...] += jnp.sum(g, axis=0, keepdims=True)

    vec = pl.BlockSpec((1, LANES), lambda i: (0, 0))
    row = pl.BlockSpec((ts, LANES), lambda i: (i, 0))
    return _pc(body, name, (s // ts,), [pl.BlockSpec((ts, LANES), lambda i: (i, 48)), vec, row], [row, vec],
               [_sds((s, LANES), BF16), _sds((1, LANES))])(proj, bias, ddt)


def _pair_cols(cols, k0, lo):
    return jnp.where(lo, cols[:, k0:k0 + 1], cols[:, k0 + 1:k0 + 2])


def _last_lane(row):
    lane = lax.broadcasted_iota(jnp.int32, row.shape, 1)
    return jnp.sum(jnp.where(lane == SSM_CHUNK - 1, row, 0.0), axis=-1, keepdims=True)


SSD_FWD_GROUPS = 2
SSD_BWD_GROUPS = 1


def _ssd_specs(nc, rev, n):
    cc = (lambda c: nc - 1 - c) if rev else (lambda c: c)
    nb = SSM_INNER // (LANES * n)
    return dict(
        x=pl.BlockSpec((SSM_CHUNK, 256 * n), lambda g, c: (cc(c), g)),
        b=pl.BlockSpec((SSM_CHUNK, LANES * n), lambda g, c: (cc(c), nb + g)),
        c=pl.BlockSpec((SSM_CHUNK, LANES * n), lambda g, c: (cc(c), nb + SSM_GROUPS // n + g)),
        col=pl.BlockSpec((n, SSM_CHUNK, 4), lambda g, c: (g, cc(c), 0)),
        row=pl.BlockSpec((n, 4, SSM_CHUNK), lambda g, c: (g, 0, cc(c))),
        grp=pl.BlockSpec((n, 1, 256), lambda g, c: (g, 0, 0)),
        grow=pl.BlockSpec((n, 4, LANES), lambda g, c: (g, 0, 0)),
        hs=pl.BlockSpec((1, n, 256, SSM_STATE), lambda g, c: (cc(c), g, 0, 0)),
        bc=pl.BlockSpec((SSM_CHUNK, LANES * n), lambda g, c: (cc(c), g)),
    )


def _ssd_scan_fwd(xbc, dtc, acol, drow, arow, dskip, name):
    s = xbc.shape[0]
    nc = s // SSM_CHUNK
    n = SSD_FWD_GROUPS
    sp = _ssd_specs(nc, False, n)
    L = SSM_CHUNK

    def body(x_ref, b_ref, c_ref, dtc_ref, ac_ref, dr_ref, ar_ref, dk_ref, y_ref, hs_ref, h_ref):
        @pl.when(pl.program_id(1) == 0)
        def _():
            h_ref[...] = jnp.zeros_like(h_ref)

        for gi in range(n):
            group(gi, x_ref, b_ref, c_ref, dtc_ref, ac_ref, dr_ref, ar_ref, dk_ref, y_ref, hs_ref, h_ref)

    def group(gi, x_ref, b_ref, c_ref, dtc_ref, ac_ref, dr_ref, ar_ref, dk_ref, y_ref, hs_ref, h_ref):
        x0 = gi * 256
        bb = b_ref[:, gi * LANES:(gi + 1) * LANES].astype(BF16)
        cb = c_ref[:, gi * LANES:(gi + 1) * LANES].astype(BF16)
        gm = _dot(cb, bb, 1, 1)
        dtc = dtc_ref[gi]
        ac = ac_ref[gi]
        dr = dr_ref[gi]
        ar = ar_ref[gi]
        dsk = dk_ref[gi]
        hs_ref[0, gi] = h_ref[gi]
        tril = lax.broadcasted_iota(jnp.int32, (L, L), 0) >= lax.broadcasted_iota(jnp.int32, (L, L), 1)
        lo = _lo_mask((L, LANES))
        rowlo = lax.broadcasted_iota(jnp.int32, (L, LANES), 0) < HEAD_DIM
        for pr in range(2):
            k0 = 2 * pr
            xp = x_ref[:, x0 + pr * LANES:x0 + (pr + 1) * LANES]
            xpb = xp.astype(BF16)
            hp = h_ref[gi, pr * LANES:(pr + 1) * LANES, :]
            yd, al = [], []
            for k in (k0, k0 + 1):
                seg = ac[:, k:k + 1] - ar[k:k + 1, :]
                wk = gm * jnp.exp(jnp.where(tril, seg, -jnp.inf)) * dr[k:k + 1, :]
                yd.append(_dot(wk.astype(BF16), xpb, 1, 0))
                al.append(_last_lane(ar[k:k + 1, :]))
            e = jnp.exp(_pair_cols(ac, k0, lo))
            yo = _dot(cb, hp.astype(BF16), 1, 1) * e
            y_ref[:, x0 + pr * LANES:x0 + (pr + 1) * LANES] = (
                jnp.where(lo, yd[0], yd[1]) + yo + dsk[:, pr * LANES:(pr + 1) * LANES] * xp)
            wp = jnp.where(lo, jnp.exp(al[0] - ac[:, k0:k0 + 1]) * dtc[:, k0:k0 + 1],
                           jnp.exp(al[1] - ac[:, k0 + 1:k0 + 2]) * dtc[:, k0 + 1:k0 + 2])
            st = _dot((xp * wp).astype(BF16), bb, 0, 0)
            dec = jnp.where(rowlo, jnp.exp(al[0]), jnp.exp(al[1]))
            h_ref[gi, pr * LANES:(pr + 1) * LANES, :] = dec * hp + st

    return _pc(body, name, (SSM_GROUPS // n, nc),
               [sp["x"], sp["b"], sp["c"], sp["col"], sp["col"], sp["row"], sp["row"], sp["grp"]],
               [sp["x"], sp["hs"]], [_sds((s, SSM_INNER)), _sds((nc, SSM_GROUPS, 256, SSM_STATE))],
               [pltpu.VMEM((n, 256, SSM_STATE), F32)])(xbc, xbc, xbc, dtc, acol, drow, arow, dskip)


def _ssd_scan_bwd(xbc, dtc, acol, drow, arow, dskip, agrp, hs, dy, name):
    s = xbc.shape[0]
    nc = s // SSM_CHUNK
    n = SSD_BWD_GROUPS
    sp = _ssd_specs(nc, True, n)
    L = SSM_CHUNK

    def body(x_ref, b_ref, c_ref, dtc_ref, ac_ref, dr_ref, ar_ref, dk_ref, ag_ref, hs_ref, dy_ref,
             dx_ref, db_ref, dc_ref, ddt_ref, da_ref, dd_ref, dh_ref):
        @pl.when(pl.program_id(1) == 0)
        def _():
            dh_ref[...] = jnp.zeros_like(dh_ref)
            da_ref[...] = jnp.zeros_like(da_ref)
            dd_ref[...] = jnp.zeros_like(dd_ref)

        for gi in range(n):
            group(gi, x_ref, b_ref, c_ref, dtc_ref, ac_ref, dr_ref, ar_ref, dk_ref, ag_ref, hs_ref, dy_ref,
                  dx_ref, db_ref, dc_ref, ddt_ref, da_ref, dd_ref, dh_ref)

    def group(gi, x_ref, b_ref, c_ref, dtc_ref, ac_ref, dr_ref, ar_ref, dk_ref, ag_ref, hs_ref, dy_ref,
              dx_ref, db_ref, dc_ref, ddt_ref, da_ref, dd_ref, dh_ref):
        x0 = gi * 256
        bcols = slice(gi * LANES, (gi + 1) * LANES)
        bb = b_ref[:, bcols].astype(BF16)
        cb = c_ref[:, bcols].astype(BF16)
        gm = _dot(cb, bb, 1, 1)
        dtc = dtc_ref[gi]
        ac = ac_ref[gi]
        dr = dr_ref[gi]
        ar = ar_ref[gi]
        dsk = dk_ref[gi]
        ag = ag_ref[gi]
        tril = lax.broadcasted_iota(jnp.int32, (L, L), 0) >= lax.broadcasted_iota(jnp.int32, (L, L), 1)
        lo = _lo_mask((L, LANES))
        nlo = jnp.logical_not(lo)
        rowlo = lax.broadcasted_iota(jnp.int32, (L, LANES), 0) < HEAD_DIM
        lane = lax.broadcasted_iota(jnp.int32, (L, LANES), 1)
        lane_row = lax.broadcasted_iota(jnp.int32, (1, LANES), 1)
        dgm = jnp.zeros((L, L), F32)
        dcm = jnp.zeros((L, SSM_STATE), F32)
        dbm = jnp.zeros((L, SSM_STATE), F32)
        cols = jnp.zeros((L, LANES), F32)
        rows_ddt, rows_q, al_all, dcd_all = [], [], [], []
        for pr in range(2):
            k0 = 2 * pr
            xcols = slice(x0 + pr * LANES, x0 + (pr + 1) * LANES)
            xp = x_ref[:, xcols]
            xpb = xp.astype(BF16)
            dyp = dy_ref[:, xcols]
            dypb = dyp.astype(BF16)
            zero = jnp.zeros_like(dypb)
            hp = hs_ref[0, gi, pr * LANES:(pr + 1) * LANES, :]
            hpb = hp.astype(BF16)
            dst = dh_ref[gi, pr * LANES:(pr + 1) * LANES, :]
            dstb = dst.astype(BF16)
            dxd, al = [], []
            for k in (k0, k0 + 1):
                sel = lo if k == k0 else nlo
                seg = ac[:, k:k + 1] - ar[k:k + 1, :]
                lam = jnp.exp(jnp.where(tril, seg, -jnp.inf))
                wk = gm * lam * dr[k:k + 1, :]
                dwk = _dot(jnp.where(sel, dypb, zero), xpb, 1, 1)
                mk = dwk * gm * lam
                qk = mk * dr[k:k + 1, :]
                dgm = dgm + dwk * lam * dr[k:k + 1, :]
                rows_ddt.append(jnp.sum(mk, axis=0, keepdims=True))
                rows_q.append(jnp.sum(qk, axis=0, keepdims=True))
                cols = jnp.where(lane == k, jnp.sum(qk, axis=-1, keepdims=True), cols)
                dxd.append(_dot(wk.astype(BF16), dypb, 0, 0))
                al.append(_last_lane(ar[k:k + 1, :]))
            al_all += al
            dxp = jnp.where(lo, dxd[0], dxd[1])
            e = jnp.exp(_pair_cols(ac, k0, lo))
            dye = dyp * e
            dyeb = dye.astype(BF16)
            dcm = dcm + _dot(dyeb, hpb, 1, 0)
            dh_yoff = _dot(dyeb, cb, 0, 0)
            tq = dye * _dot(cb, hpb, 1, 1)
            cols = jnp.where(lane == 4 + k0, jnp.sum(jnp.where(lo, tq, 0.0), axis=-1, keepdims=True), cols)
            cols = jnp.where(lane == 5 + k0, jnp.sum(jnp.where(lo, 0.0, tq), axis=-1, keepdims=True), cols)
            wp = jnp.where(lo, jnp.exp(al[0] - ac[:, k0:k0 + 1]) * dtc[:, k0:k0 + 1],
                           jnp.exp(al[1] - ac[:, k0 + 1:k0 + 2]) * dtc[:, k0 + 1:k0 + 2])
            dxw = _dot(bb, dstb, 1, 1)
            dxp = dxp + dxw * wp
            tw = xp * dxw
            cols = jnp.where(lane == 8 + k0, jnp.sum(jnp.where(lo, tw, 0.0), axis=-1, keepdims=True), cols)
            cols = jnp.where(lane == 9 + k0, jnp.sum(jnp.where(lo, 0.0, tw), axis=-1, keepdims=True), cols)
            dbm = dbm + _dot((xp * wp).astype(BF16), dstb, 1, 0)
            dsl = dsk[:, pr * LANES:(pr + 1) * LANES]
            dx_ref[:, xcols] = dxp + dsl * dyp
            dd_ref[gi, :, pr * LANES:(pr + 1) * LANES] += jnp.sum(dyp * xp, axis=0, keepdims=True)
            prod = dst * hp
            dcd_all.append(jnp.sum(jnp.sum(jnp.where(rowlo, prod, 0.0), axis=-1, keepdims=True), axis=0, keepdims=True))
            dcd_all.append(jnp.sum(jnp.sum(jnp.where(rowlo, 0.0, prod), axis=-1, keepdims=True), axis=0, keepdims=True))
            dec = jnp.where(rowlo, jnp.exp(al[0]), jnp.exp(al[1]))
            dh_ref[gi, pr * LANES:(pr + 1) * LANES, :] = dec * dst + dh_yoff
        dgb = dgm.astype(BF16)
        dc_ref[:, bcols] = dcm + _dot(dgb, bb, 1, 0)
        db_ref[:, bcols] = dbm + _dot(dgb, cb, 0, 0)
        colt = cols.T
        sub8 = lax.broadcasted_iota(jnp.int32, (8, LANES), 0)
        da_rows = jnp.zeros((8, LANES), F32)
        ddt_part = []
        for k in range(4):
            rs = colt[k:k + 1, :]
            uo = colt[4 + k:5 + k, :]
            dwl = colt[8 + k:9 + k, :]
            es = jnp.exp(al_all[k] - ar[k:k + 1, :])
            wrow = es * dr[k:k + 1, :]
            dwl_w = dwl * wrow
            da_k = rs - rows_q[k] + uo - dwl_w
            tail = jnp.sum(dwl_w, axis=-1, keepdims=True) + jnp.exp(al_all[k]) * dcd_all[k]
            da_k = da_k + jnp.where(lane_row == L - 1, tail, 0.0)
            da_rows = jnp.where(sub8 == k, da_k, da_rows)
            ddt_part.append(rows_ddt[k] + dwl * es)
        dda = _dot(da_rows, _tri(True), 1, 0, HI)
        for k in range(4):
            dda_k = dda[k:k + 1, :]
            ddt_ref[gi, k:k + 1, :] = ddt_part[k] + dda_k * ag[k:k + 1, :]
            da_ref[gi, k:k + 1, :] += dda_k * dr[k:k + 1, :] * ag[k:k + 1, :]

    return _pc(body, name, (SSM_GROUPS // n, nc),
               [sp["x"], sp["b"], sp["c"], sp["col"], sp["col"], sp["row"], sp["row"], sp["grp"], sp["grow"], sp["hs"], sp["x"]],
               [sp["x"], sp["bc"], sp["bc"], sp["row"], sp["grow"], sp["grp"]],
               [_sds((s, SSM_INNER)), _sds((s, 1024)), _sds((s, 1024)), _sds((SSM_GROUPS, 4, s)),
                _sds((SSM_GROUPS, 4, LANES)), _sds((SSM_GROUPS, 1, 256))],
               [pltpu.VMEM((n, 256, SSM_STATE), F32)])(xbc, xbc, xbc, dtc, acol, drow, arow, dskip, agrp, hs, dy)


def _gnorm_fwd(y, proj, nw, name):
    s = y.shape[0]
    ts = _row_tile(s)
    gw = SSM_INNER // SSM_GROUPS

    def body(y_ref, z_ref, w_ref, o_ref):
        for g in range(SSM_GROUPS):
            sl = slice(g * gw, (g + 1) * gw)
            z = z_ref[:, sl]
            y2 = y_ref[:, sl] * (z * _sigmoid(z))
            r = lax.rsqrt(jnp.mean(y2 * y2, axis=-1, keepdims=True) + RMS_EPS)
            o_ref[:, sl] = ((y2 * r) * w_ref[:, sl]).astype(BF16)

    row = pl.BlockSpec((ts, SSM_INNER), lambda i: (i, 0))
    return _pc(body, name, (s // ts,), [row, row, pl.BlockSpec((1, SSM_INNER), lambda i: (0, 0))], row,
               _sds((s, SSM_INNER), BF16))(y, proj, nw)


def _gnorm_bwd(y, proj, nw, dyn, name):
    s = y.shape[0]
    ts = _row_tile(s)
    gw = SSM_INNER // SSM_GROUPS

    def body(y_ref, z_ref, w_ref, d_ref, dy_ref, dz_ref, dw_ref):
        @pl.when(pl.program_id(0) == 0)
        def _():
            dw_ref[...] = jnp.zeros_like(dw_ref)

        for g in range(SSM_GROUPS):
            sl = slice(g * gw, (g + 1) * gw)
            z = z_ref[:, sl]
            yv = y_ref[:, sl]
            sg = _sigmoid(z)
            sz = z * sg
            y2 = yv * sz
            r = lax.rsqrt(jnp.mean(y2 * y2, axis=-1, keepdims=True) + RMS_EPS)
            yn = y2 * r
            dout = d_ref[:, sl]
            dyg = dout * w_ref[:, sl]
            dy2 = r * (dyg - yn * jnp.mean(dyg * yn, axis=-1, keepdims=True))
            dy_ref[:, sl] = dy2 * sz
            dz_ref[:, sl] = (dy2 * yv * (sg * (1.0 + z * (1.0 - sg)))).astype(BF16)
            dw_ref[:, sl] += jnp.sum(dout * yn, axis=0, keepdims=True)

    row = pl.BlockSpec((ts, SSM_INNER), lambda i: (i, 0))
    vec = pl.BlockSpec((1, SSM_INNER), lambda i: (0, 0))
    return _pc(body, name, (s // ts,), [row, row, vec, row], [row, row, vec],
               [_sds((s, SSM_INNER)), _sds((s, SSM_INNER), BF16), _sds((1, SSM_INNER))])(y, proj, nw, dyn)


def _head_layouts(v, s):
    return v.reshape(s, SSM_GROUPS, 4).transpose(1, 0, 2), v.T.reshape(SSM_GROUPS, 4, s)


def _ssd_fwd(x, h, w_in, conv_w, conv_b, dt_bias, a_log, d_skip, gnorm_w, w_out, next_norm):
    s = x.shape[0]
    proj = _mm(h, w_in, "nn", "ssd_in")
    xbc = _ssd_conv_fwd(proj, conv_w, conv_b, "ssd_conv")
    pad = ((0, 0), (0, LANES - SSM_HEADS))
    a_neg = -jnp.exp(a_log)
    bias = jnp.pad(dt_bias, pad)
    dt, acum = _ssd_dt_fwd(proj, bias, jnp.pad(a_neg, pad), "ssd_dt")
    dtc, drow = _head_layouts(dt[:, :SSM_HEADS], s)
    acol, arow = _head_layouts(acum[:, :SSM_HEADS], s)
    dskip = jnp.repeat(d_skip.reshape(SSM_GROUPS, 1, 4), HEAD_DIM, axis=2)
    y, hs = _ssd_scan_fwd(xbc, dtc, acol, drow, arow, dskip, "ssd_scan")
    yn = _gnorm_fwd(y, proj, gnorm_w, "ssd_gnorm")
    out, h_next = _mm(yn, w_out, "nn", "ssd_out", res=x, norm_w=next_norm)
    return out, h_next, (x, h, proj, xbc, bias, a_neg, dtc, acol, drow, arow, dskip, y, hs, yn)


def _ssd_bwd(dout, saved, norm_w, w_in, conv_w, conv_b, gnorm_w, w_out):
    x, h, proj, xbc, bias, a_neg, dtc, acol, drow, arow, dskip, y, hs, yn = saved
    s = x.shape[0]
    dyn = _mm(dout, w_out, "nt", "ssd_dyn")
    g_out = _mm(yn, dout, "tn", "ssd_gout", out_dtype=BF16)
    dy, dz, g_gnorm = _gnorm_bwd(y, proj, gnorm_w, dyn, "ssd_dgnorm")
    agrp = jnp.broadcast_to(a_neg.reshape(SSM_GROUPS, 4, 1), (SSM_GROUPS, 4, LANES))
    dxs, db, dc, ddt_row, da_acc, dd_acc = _ssd_scan_bwd(xbc, dtc, acol, drow, arow, dskip, agrp, hs, dy, "ssd_dscan")
    parts = [_ssd_conv_bwd(proj, d, col, conv_w, conv_b, f"ssd_dconv_{tag}") for d, col, tag in ((dxs, 0, "x"), (db, 2, "b"), (dc, 3, "c"))]
    g_cw = jnp.concatenate([p[1] for p in parts], axis=1)
    g_cb = jnp.concatenate([p[2] for p in parts], axis=1)
    ddt = jnp.pad(ddt_row.reshape(SSM_HEADS, s).T, ((0, 0), (0, LANES - SSM_HEADS)))
    ddtraw, g_dtb = _ssd_dt_bwd(proj, bias, ddt, "ssd_ddt")
    dproj = jnp.concatenate([dz] + [p[0] for p in parts] + [ddtraw], axis=1)
    g_in = _mm(h, dproj, "tn", "ssd_gin", out_dtype=BF16)
    dx, g_norm = _mm_dnorm_nt(dproj, w_in, "ssd_dh", x, norm_w, dout)
    g_alog = jnp.sum(da_acc, axis=-1).reshape(1, SSM_HEADS)
    g_d = jnp.sum(dd_acc.reshape(SSM_GROUPS, 4, HEAD_DIM), axis=-1).reshape(1, SSM_HEADS)
    return dx, g_norm, g_in[:, :SSM_IN], g_cw, g_cb, g_dtb[:, :SSM_HEADS], g_alog, g_d, g_gnorm, g_out


def _loss_head(y, target, name):
    s, d = y.shape
    ts = _row_tile(s)

    def body(y_ref, t_ref, dy_ref, l_ref):
        @pl.when(pl.program_id(0) == 0)
        def _():
            l_ref[...] = jnp.zeros_like(l_ref)

        e = y_ref[...] - t_ref[...]
        dy_ref[...] = e * (1.0 / d)
        part = jnp.sum(jnp.sum(e * e, axis=-1, keepdims=True), axis=0, keepdims=True) * (0.5 / d)
        l_ref[...] += jnp.broadcast_to(part, l_ref.shape)

    row = pl.BlockSpec((ts, d), lambda i: (i, 0))
    dy, lacc = _pc(body, name, (s // ts,), [row, row], [row, pl.BlockSpec((8, LANES), lambda i: (0, 0))],
                   [_sds((s, d)), _sds((8, LANES))])(y, target)
    return lacc[0, 0], dy


def _local_step(x, target, w, gather_layer0=None, gather_rest=None, scatter_first=None, scatter_layer0=None):
    saved = []
    received, received_layer0 = None, {}

    def layer0_stages():
        def entry(stage):
            shards, finish = gather_layer0[stage]

            def on_arrival(got):
                nonlocal w
                w = finish(w, got)
                return {"conv_gate": lambda: w["conv_w_out"][0], "ffn_gu": lambda: w["ffn_w_down"][0]}.get(stage, lambda: None)()
            return shards, on_arrival
        return {stage: entry(stage) for stage in gather_layer0}

    at = lambda weights, n: weights[n] if n < len(weights) else None
    h = _rms_fwd(x, w["mix_norm"][0:1], "first_norm")
    for i in range(DEPTH):
        kind, j = i % 3, i // 3
        fn = w["ffn_norm"][i:i + 1]
        stages = layer0_stages() if (i == 0 and gather_layer0 is not None) else None
        if kind == 0:
            x, h, sv = _conv_fwd(x, h, w["conv_w_in"][j], w["conv_w_dw"][j], at(w["conv_w_out"], j), str(i), fn, stages)
        elif kind == 1:
            hosted = None if gather_rest is None else (gather_rest[0], True)
            x, h, sv, got = _fox_fwd(x, h, w["fox_w_in"], w["fox_b_f"], w["fox_q_gain"], w["fox_k_gain"], w["fox_w_out"], fn, hosted)
            if gather_rest is not None:
                w = gather_rest[1](w, got)
        else:
            x, h, sv = _ssd_fwd(x, h, w["ssd_w_in"], w["ssd_conv_w"], w["ssd_conv_b"], w["ssd_dt_bias"],
                                w["ssd_a_log"], w["ssd_d"], w["ssd_norm_w"], w["ssd_w_out"], fn)
        nxt = w["mix_norm"][i + 1:i + 2] if i + 1 < DEPTH else None
        x, h, sf = _ffn_fwd(x, h, w["ffn_w_gu"][i], at(w["ffn_w_down"], i), str(i), nxt, stages)
        saved.append((sv, sf))
    loss, dx = _loss_head(x, target, "loss_head")
    g = {k: [None] * n for k, n in (("mix_norm", DEPTH), ("ffn_norm", DEPTH), ("ffn_w_gu", DEPTH), ("ffn_w_down", DEPTH),
                                    ("conv_w_in", 2), ("conv_w_dw", 2), ("conv_w_out", 2))}
    for i in reversed(range(DEPTH)):
        kind, j = i % 3, i // 3
        sv, sf = saved[i]
        hosted_fn = None
        if i == 0 and scatter_layer0 is not None:
            hosted_fn = lambda stage, **new: scatter_layer0(stage, g, **new)
        dx, g["ffn_norm"][i], g["ffn_w_gu"][i], g["ffn_w_down"][i], got = _ffn_bwd(
            dx, sf, w["ffn_norm"][i:i + 1], w["ffn_w_gu"][i], w["ffn_w_down"][i], str(i), hosted_fn)
        received_layer0.update(got)
        mn = w["mix_norm"][i:i + 1]
        if kind == 0:
            dx, g["mix_norm"][i], g["conv_w_in"][j], g["conv_w_dw"][j], g["conv_w_out"][j], got = _conv_bwd(
                dx, sv, mn, w["conv_w_in"][j], w["conv_w_dw"][j], w["conv_w_out"][j], str(i), hosted_fn)
            received_layer0.update(got)
        elif kind == 1:
            hosted = None if scatter_first is None else (scatter_first(g), False)
            (dx, g["mix_norm"][i], g["fox_w_in"], g["fox_b_f"], g["fox_q_gain"], g["fox_k_gain"],
             g["fox_w_out"], received) = _fox_bwd(dx, sv, mn, w["fox_w_in"], w["fox_w_out"], hosted)
        else:
            (dx, g["mix_norm"][i], g["ssd_w_in"], g["ssd_conv_w"], g["ssd_conv_b"], g["ssd_dt_bias"], g["ssd_a_log"],
             g["ssd_d"], g["ssd_norm_w"], g["ssd_w_out"]) = _ssd_bwd(
                 dx, sv, mn, w["ssd_w_in"], w["ssd_conv_w"], w["ssd_conv_b"], w["ssd_norm_w"], w["ssd_w_out"])
    g["mix_norm"] = jnp.concatenate(g["mix_norm"], axis=0)
    g["ffn_norm"] = jnp.concatenate(g["ffn_norm"], axis=0)
    g["conv_w_dw"] = jnp.stack(g["conv_w_dw"], axis=0)
    g["ssd_conv_w"] = g["ssd_conv_w"][None]
    return loss, dx, g, received, received_layer0


def _mesh_position():
    return lax.axis_index("x") * 4 + lax.axis_index("y") * 2 + lax.axis_index("c")


def _device_of(t):
    return (lax.shift_right_logical(t, 2), lax.bitwise_and(lax.shift_right_logical(t, 1), 1), lax.bitwise_and(t, 1))


def _exchange_copies(srcs_of, out_refs, send_sems, recv_sems, local_sems):
    me = _mesh_position()
    na = len(out_refs)
    locals_ = [pltpu.make_async_copy(srcs_of(a, me), out_refs[a].at[me], local_sems.at[a]) for a in range(na)]
    sends, arrivals = [], []
    for j in range(1, NDEV):
        t = lax.rem(me + j, NDEV)
        frm = lax.rem(me + NDEV - j, NDEV)
        for a in range(na):
            sends.append(pltpu.make_async_remote_copy(
                src_ref=srcs_of(a, t), dst_ref=out_refs[a].at[me], send_sem=send_sems.at[a, j - 1],
                recv_sem=recv_sems.at[a, j - 1], device_id=_device_of(t), device_id_type=pl.DeviceIdType.MESH))
            arrivals.append(pltpu.make_async_remote_copy(
                src_ref=srcs_of(a, me), dst_ref=out_refs[a].at[frm], send_sem=send_sems.at[a, j - 1],
                recv_sem=recv_sems.at[a, j - 1], device_id=_device_of(frm), device_id_type=pl.DeviceIdType.MESH))
    return locals_, sends, arrivals


def _exchange_start(copies):
    locals_, sends, _ = copies
    for cp in locals_ + sends:
        cp.start()


def _exchange_wait(copies):
    locals_, sends, arrivals = copies
    for cp in sends:
        cp.wait_send()
    for cp in arrivals:
        cp.wait_recv()
    for cp in locals_:
        cp.wait()


def _exchange_run(srcs_of, out_refs, send_sems, recv_sems, local_sems):
    copies = _exchange_copies(srcs_of, out_refs, send_sems, recv_sems, local_sems)
    _exchange_start(copies)
    _exchange_wait(copies)


def _exchange_parts(arrays, gather):
    na = len(arrays)
    outs = [_sds(((NDEV,) + a.shape) if gather else a.shape, a.dtype) for a in arrays]
    sems = [pltpu.SemaphoreType.DMA((na, NDEV - 1)), pltpu.SemaphoreType.DMA((na, NDEV - 1)), pltpu.SemaphoreType.DMA((na,))]
    pick = (lambda srcs: (lambda a, t: srcs[a])) if gather else (lambda srcs: (lambda a, t: srcs[a].at[t]))
    return pick, outs, sems


def _exchange(arrays, name, gather):
    na = len(arrays)
    pick, outs, sems = _exchange_parts(arrays, gather)

    def body(*refs):
        _exchange_run(pick(refs[:na]), refs[na:2 * na], *refs[2 * na:])

    hbm = pl.BlockSpec(memory_space=pl.ANY)
    return pl.pallas_call(body, name=name, in_specs=[hbm] * na, out_specs=[hbm] * na, out_shape=outs, scratch_shapes=sems)(*arrays)


def _all_sum_small(pack, name):
    def body(src_ref, out_ref, buf_ref, send_sems, recv_sems, local_sems):
        _exchange_run(lambda a, t: src_ref, [buf_ref], send_sems, recv_sems, local_sems)
        acc = buf_ref[0]
        for d in range(1, NDEV):
            acc = acc + buf_ref[d]
        out_ref[...] = acc

    vmem = pl.BlockSpec(memory_space=pltpu.VMEM)
    return pl.pallas_call(
        body, name=name, in_specs=[vmem], out_specs=vmem, out_shape=_sds(pack.shape, pack.dtype),
        scratch_shapes=[pltpu.VMEM((NDEV,) + pack.shape, pack.dtype), pltpu.SemaphoreType.DMA((1, NDEV - 1)),
                        pltpu.SemaphoreType.DMA((1, NDEV - 1)), pltpu.SemaphoreType.DMA((1,))])(pack)


def _sum_slabs(slabs, name):
    _, r, c = slabs.shape
    tr = r
    for cand in (256, 352):
        if r % cand == 0:
            tr = cand
            break

    def body(s_ref, o_ref):
        acc = s_ref[0].astype(F32)
        for d in range(1, NDEV):
            acc = acc + s_ref[d].astype(F32)
        o_ref[...] = acc

    return _pc(body, name, (r // tr,), [pl.BlockSpec((NDEV, tr, c), lambda i: (0, i, 0))],
               pl.BlockSpec((tr, c), lambda i: (i, 0)), _sds((r, c)))(slabs)


def _adamw(wt, g, m, v, name):
    shape = wt.shape
    w2, g2, m2, v2 = (a.reshape(-1, shape[-1]) for a in (wt, g, m, v))
    r, c = w2.shape
    tr = r
    for cand in (512, 352, 256):
        if r % cand == 0:
            tr = cand
            break
    c1 = 1.0 - ADAM_B1 ** ADAM_STEP
    c2 = 1.0 - ADAM_B2 ** ADAM_STEP

    def body(w_ref, g_ref, m_ref, v_ref, d_ref, mo_ref, vo_ref):
        gv = g_ref[...]
        mn = ADAM_B1 * m_ref[...] + (1.0 - ADAM_B1) * gv
        vn = ADAM_B2 * v_ref[...] + (1.0 - ADAM_B2) * (gv * gv)
        mo_ref[...] = mn
        vo_ref[...] = vn
        d_ref[...] = -ADAM_LR * ((mn / c1) / (jnp.sqrt(vn / c2) + ADAM_EPS) + ADAM_WD * w_ref[...])

    spec = pl.BlockSpec((tr, c), lambda i: (i, 0))
    outs = _pc(body, name, (r // tr,), [spec] * 4, [spec] * 3, [_sds((r, c))] * 3)(w2, g2, m2, v2)
    return tuple(o.reshape(shape) for o in outs)


_NAMES = ["mix_norm", "ffn_norm", "ffn_w_gu", "ffn_w_down", "conv_w_in", "conv_w_dw", "conv_w_out", "fox_w_in", "fox_b_f",
          "fox_q_gain", "fox_k_gain", "fox_w_out", "ssd_w_in", "ssd_conv_w", "ssd_conv_b", "ssd_dt_bias", "ssd_a_log",
          "ssd_d", "ssd_norm_w", "ssd_w_out"]
_MATRICES = ["ffn_w_gu", "ffn_w_down", "conv_w_in", "conv_w_out", "fox_w_in", "fox_w_out", "ssd_w_in", "ssd_w_out"]
_VECTORS = {"conv_w_dw": 2, "ssd_conv_w": 2, "ssd_conv_b": 1, "ssd_norm_w": 1}
_REPLICATED = ["mix_norm", "ffn_norm", "fox_b_f", "fox_q_gain", "fox_k_gain", "ssd_dt_bias", "ssd_a_log", "ssd_d"]


def _to_rows(flat):
    n = flat.shape[0]
    rows = -(-n // (8 * D_MODEL)) * 8
    return jnp.pad(flat, (0, rows * D_MODEL - n)).reshape(rows, D_MODEL)


def _full_shape(local_shape, axis):
    shp = list(local_shape)
    shp[axis] *= NDEV
    return tuple(shp)


def _cols_from_blocks(g):
    return jnp.moveaxis(g, 0, 1).reshape(g.shape[1], NDEV * g.shape[2])


def _blocks_from_cols(full):
    k, n8 = full.shape
    return jnp.moveaxis(full.reshape(k, NDEV, n8 // NDEV), 1, 0)


def kernel(x, mix_norm, ffn_norm, ffn_w_gu, ffn_w_down, conv_w_in, conv_w_dw, conv_w_out, fox_w_in, fox_b_f, fox_q_gain, fox_k_gain, fox_w_out, ssd_w_in, ssd_conv_w, ssd_conv_b, ssd_dt_bias, ssd_a_log, ssd_d, ssd_norm_w, ssd_w_out, loss_target, m_mix_norm, m_ffn_norm, m_ffn_w_gu, m_ffn_w_down, m_conv_w_in, m_conv_w_dw, m_conv_w_out, m_fox_w_in, m_fox_b_f, m_fox_q_gain, m_fox_k_gain, m_fox_w_out, m_ssd_w_in, m_ssd_conv_w, m_ssd_conv_b, m_ssd_dt_bias, m_ssd_a_log, m_ssd_d, m_ssd_norm_w, m_ssd_w_out, v_mix_norm, v_ffn_norm, v_ffn_w_gu, v_ffn_w_down, v_conv_w_in, v_conv_w_dw, v_conv_w_out, v_fox_w_in, v_fox_b_f, v_fox_q_gain, v_fox_k_gain, v_fox_w_out, v_ssd_w_in, v_ssd_conv_w, v_ssd_conv_b, v_ssd_dt_bias, v_ssd_a_log, v_ssd_d, v_ssd_norm_w, v_ssd_w_out):
    local = dict(mix_norm=mix_norm, ffn_norm=ffn_norm, ffn_w_gu=ffn_w_gu, ffn_w_down=ffn_w_down, conv_w_in=conv_w_in,
                 conv_w_dw=conv_w_dw, conv_w_out=conv_w_out, fox_w_in=fox_w_in, fox_b_f=fox_b_f, fox_q_gain=fox_q_gain,
                 fox_k_gain=fox_k_gain, fox_w_out=fox_w_out, ssd_w_in=ssd_w_in, ssd_conv_w=ssd_conv_w, ssd_conv_b=ssd_conv_b,
                 ssd_dt_bias=ssd_dt_bias, ssd_a_log=ssd_a_log, ssd_d=ssd_d, ssd_norm_w=ssd_norm_w, ssd_w_out=ssd_w_out)
    mom = dict(zip(_NAMES, [m_mix_norm, m_ffn_norm, m_ffn_w_gu, m_ffn_w_down, m_conv_w_in, m_conv_w_dw, m_conv_w_out, m_fox_w_in,
                            m_fox_b_f, m_fox_q_gain, m_fox_k_gain, m_fox_w_out, m_ssd_w_in, m_ssd_conv_w, m_ssd_conv_b,
                            m_ssd_dt_bias, m_ssd_a_log, m_ssd_d, m_ssd_norm_w, m_ssd_w_out]))
    var = dict(zip(_NAMES, [v_mix_norm, v_ffn_norm, v_ffn_w_gu, v_ffn_w_down, v_conv_w_in, v_conv_w_dw, v_conv_w_out, v_fox_w_in,
                            v_fox_b_f, v_fox_q_gain, v_fox_k_gain, v_fox_w_out, v_ssd_w_in, v_ssd_conv_w, v_ssd_conv_b,
                            v_ssd_dt_bias, v_ssd_a_log, v_ssd_d, v_ssd_norm_w, v_ssd_w_out]))

    shard = {k: local[k].astype(BF16) for k in _MATRICES}
    vec_pack = _to_rows(jnp.concatenate([local[k].reshape(-1) for k in _VECTORS]))
    first = _exchange([shard["conv_w_in"][0:1], vec_pack], "gather_first", True)
    gvec = first[1].reshape(NDEV, -1)
    full = {k: local[k] for k in _REPLICATED}
    off = 0
    for k, axis in _VECTORS.items():
        n = local[k].size
        blk = jnp.moveaxis(gvec[:, off:off + n].reshape((NDEV,) + local[k].shape), 0, axis)
        full[k] = blk.reshape(_full_shape(local[k].shape, axis))
        off += n
    full["ssd_conv_w"] = full["ssd_conv_w"][0]
    full["conv_w_in"] = [first[0][:, 0]]
    full["conv_w_out"], full["ffn_w_gu"], full["ffn_w_down"] = [], [], []

    def finish_gu0(w, got):
        return dict(w, ffn_w_gu=[got[0][:, 0]])

    def finish_out0(w, got):
        return dict(w, conv_w_out=[got[0][:, 0].reshape(D_MODEL, D_MODEL)])

    def finish_down0(w, got):
        return dict(w, ffn_w_down=[got[0][:, 0].reshape(4, FF_BLOCK, D_MODEL)], fox_w_out=got[1].reshape(D_MODEL, D_MODEL))

    def finish_fox(w, got):
        return dict(w, fox_w_in=jnp.pad(_cols_from_blocks(got[0][:, 0]), ((0, 0), (0, FOX_IN_PAD - FOX_IN))))

    layer0 = {"conv_in": ([shard["ffn_w_gu"][0:1]], finish_gu0), "conv_gate": ([shard["conv_w_out"][0:1]], finish_out0),
              "ffn_gu": ([shard["ffn_w_down"][0:1], shard["fox_w_out"]], finish_down0),
              "ffn_down": ([shard["fox_w_in"]], finish_fox)}

    later = range(1, DEPTH)
    rest = ([shard["ffn_w_gu"][i:i + 1] for i in later] + [shard["ffn_w_down"][i:i + 1] for i in later]
            + [shard["conv_w_in"][1:], shard["conv_w_out"][1:], shard["ssd_w_in"], shard["ssd_w_out"]])

    def finish(w, got):
        w = dict(w)
        n = DEPTH - 1
        w["ffn_w_gu"] = w["ffn_w_gu"] + [got[i][:, 0] for i in range(n)]
        w["ffn_w_down"] = w["ffn_w_down"] + [got[n + i][:, 0].reshape(4, FF_BLOCK, D_MODEL) for i in range(n)]
        w["conv_w_in"] = w["conv_w_in"] + [got[2 * n][:, 0]]
        w["conv_w_out"] = w["conv_w_out"] + [got[2 * n + 1][:, 0].reshape(D_MODEL, D_MODEL)]
        w["ssd_w_in"] = jnp.pad(_cols_from_blocks(got[2 * n + 2][:, 0]), ((0, 0), (0, SSM_IN_PAD - SSM_IN)))
        w["ssd_w_out"] = got[2 * n + 3].reshape(SSM_INNER, D_MODEL)
        return w

    def early_slabs(g):
        return ([g["ffn_w_gu"][i] for i in range(1, DEPTH)]
                + [g["ffn_w_down"][i].reshape(NDEV, D_FF // NDEV, D_MODEL) for i in range(1, DEPTH)]
                + [g["conv_w_in"][1], g["conv_w_out"][1].reshape(NDEV, D_MODEL // NDEV, D_MODEL),
                   _blocks_from_cols(g["ssd_w_in"]), g["ssd_w_out"].reshape(NDEV, SSM_INNER // NDEV, D_MODEL)])

    def layer0_slabs(stage, g, g_down=None, g_gu=None, g_in=None, g_out=None):
        if stage == "ffn_gdown":
            return [_blocks_from_cols(g["fox_w_in"])]
        if stage == "ffn_dgu":
            return [g_down.reshape(NDEV, D_FF // NDEV, D_MODEL), g["fox_w_out"].reshape(NDEV, D_MODEL // NDEV, D_MODEL)]
        if stage == "ffn_dh":
            return [g_gu]
        if stage == "conv_dh":
            return [g_in, g_out.reshape(NDEV, D_MODEL // NDEV, D_MODEL)]
        return None

    loss_part, dx, grads, early, late = _local_step(x[0], loss_target[0], full, layer0, (rest, finish), early_slabs, layer0_slabs)

    se = [_sum_slabs(r, f"sum_early_{n}") for n, r in enumerate(early)]
    sl = {stage: [_sum_slabs(r, f"sum_{stage}_{n}") for n, r in enumerate(rs)] for stage, rs in late.items()}
    shard_grad = {
        "ffn_w_gu": jnp.stack(sl["ffn_dh"] + se[0:3]), "ffn_w_down": jnp.stack(sl["ffn_dgu"][0:1] + se[3:6]),
        "conv_w_in": jnp.stack([sl["conv_dh"][0], se[6]]), "conv_w_out": jnp.stack([sl["conv_dh"][1], se[7]]),
        "fox_w_in": sl["ffn_gdown"][0][None], "fox_w_out": sl["ffn_dgu"][1][None],
        "ssd_w_in": se[8][None], "ssd_w_out": se[9][None]}

    small_names = _REPLICATED + list(_VECTORS)
    small = [jnp.reshape(loss_part, (1,))] + [grads[k].reshape(-1) for k in small_names]
    total = _all_sum_small(_to_rows(jnp.concatenate(small)), "sum_small").reshape(-1)
    loss = total[0]
    off = 1
    me = _mesh_position()
    for k, part in zip(small_names, small[1:]):
        gk = total[off:off + part.shape[0]]
        off += part.shape[0]
        if k in _VECTORS:
            axis = _VECTORS[k]
            shp = local[k].shape
            gfull = gk.reshape(shp[:axis] + (NDEV, shp[axis]) + shp[axis + 1:])
            shard_grad[k] = lax.dynamic_index_in_dim(gfull, me, axis, keepdims=False)
        else:
            shard_grad[k] = gk.reshape(local[k].shape)

    deltas, new_m, new_v = {}, {}, {}
    for k in _NAMES:
        deltas[k], new_m[k], new_v[k] = _adamw(local[k], shard_grad[k], mom[k], var[k], f"adamw_{k}")
    return (loss, dx[None], *[shard_grad[k] for k in _NAMES], *[deltas[k] for k in _NAMES],
            *[new_m[k] for k in _NAMES], *[new_v[k] for k in _NAMES])
```

```python
import numpy as np

import jax
import jax.numpy as jnp
from jax import lax
from jax.experimental import pallas as pl
from jax.experimental.pallas import tpu as pltpu

F32 = jnp.float32
BF16 = jnp.bfloat16
HI = lax.Precision.HIGHEST

NDEV = 8
D_MODEL = 1024
DEPTH = 4
D_FF = 2816
FF_BLOCK = 2 * D_FF // NDEV
RMS_EPS = 1e-6
HEAD_DIM = 64
ATTN_HEADS = 16
FOX_IN = 3 * D_MODEL + ATTN_HEADS
FOX_IN_PAD = 3200
SSM_INNER = 2048
SSM_HEADS = 32
SSM_GROUPS = 8
SSM_STATE = 128
SSM_CHUNK = 128
SSM_CONV_DIM = 4096
SSM_IN = SSM_INNER + SSM_CONV_DIM + SSM_HEADS
SSM_IN_PAD = 6272
LANES = 128
V7X_VMEM_BYTES = 64 * 1024 * 1024
VMEM_LIMIT_BYTES = (V7X_VMEM_BYTES * 3) // 4
ATTN_BWD_VMEM_BYTES = (V7X_VMEM_BYTES * 7) // 8
LOG2E = 1.4426950408889634
LN2 = 0.6931471805599453
ATTN_TILE = 1024
ATTN_ROWS = 32
SSD_DCONV_COLS = 512
CUMSUM_ROWS = 512

ADAM_LR = 0.001
ADAM_B1 = 0.9
ADAM_B2 = 0.999
ADAM_EPS = 1e-08
ADAM_WD = 0.01
ADAM_STEP = 10

_TILE_CANDIDATES = (1024, 1408, 896, 768, 640, 512, 384, 256, 128)


def _pick_tile(n):
    for c in _TILE_CANDIDATES:
        if n % c == 0:
            return c
    raise ValueError(f"no tile for {n}")


def _params(ngrid):
    return pltpu.CompilerParams(dimension_semantics=("arbitrary",) * ngrid, vmem_limit_bytes=VMEM_LIMIT_BYTES)


def _pc(body, name, grid, in_specs, out_specs, out_shape, scratch=(), hosted=None):
    if hosted is None:
        return pl.pallas_call(
            body, name=name, grid=grid, in_specs=in_specs, out_specs=out_specs, out_shape=out_shape,
            scratch_shapes=list(scratch), compiler_params=_params(len(grid)))
    arrays, gather = hosted
    single = not isinstance(out_shape, (list, tuple))
    outs = [out_shape] if single else list(out_shape)
    ospecs = [out_specs] if single else list(out_specs)
    na, n_in, n_out, n_scr = len(arrays), len(in_specs), len(outs), len(scratch)
    pick, xouts, sems = _exchange_parts(arrays, gather)

    def run(*refs):
        ins, srcs = refs[:n_in], refs[n_in:n_in + na]
        res, dsts = refs[n_in + na:n_in + na + n_out], refs[n_in + na + n_out:n_in + 2 * na + n_out]
        scr, xsems = refs[n_in + 2 * na + n_out:n_in + 2 * na + n_out + n_scr], refs[n_in + 2 * na + n_out + n_scr:]
        first = pl.program_id(0) == 0
        last = pl.program_id(0) == grid[0] - 1
        for d in range(1, len(grid)):
            first = jnp.logical_and(first, pl.program_id(d) == 0)
            last = jnp.logical_and(last, pl.program_id(d) == grid[d] - 1)

        @pl.when(first)
        def _():
            _exchange_start(_exchange_copies(pick(srcs), dsts, *xsems))

        body(*ins, *res, *scr)

        @pl.when(last)
        def _():
            _exchange_wait(_exchange_copies(pick(srcs), dsts, *xsems))

    hbm = pl.BlockSpec(memory_space=pl.ANY)
    call = pl.pallas_call(
        run, name=name, grid=grid, in_specs=list(in_specs) + [hbm] * na, out_specs=ospecs + [hbm] * na,
        out_shape=outs + xouts, scratch_shapes=list(scratch) + sems, compiler_params=_params(len(grid)))
    return lambda *args: call(*args, *arrays)


def _dot(a, b, ca, cb, prec=None):
    return lax.dot_general(a, b, (((ca,), (cb,)), ((), ())), preferred_element_type=F32, precision=prec)


def _sds(shape, dtype=F32):
    return jax.ShapeDtypeStruct(shape, dtype)


def _row_tile(s, want=256):
    return want if s % want == 0 else s


def _sigmoid(x):
    return 1.0 / (1.0 + jnp.exp(-x))


def _softplus(x):
    return jnp.maximum(x, 0.0) + jnp.log(1.0 + jnp.exp(-jnp.abs(x)))


def _mm_spec(a, b, name, grid, a_spec, b_spec, o_spec, out, ca, cb, acc_shape, drop=(0, 0, 0), res=None, r_spec=None,
             norm_w=None, hosted=None):
    nk = grid[2]
    da, db, do_ = drop
    has_res = res is not None
    has_norm = norm_w is not None

    def body(*refs):
        refs = list(refs)
        a_ref, b_ref = refs[:2]
        r_ref = refs[2] if has_res else None
        w_ref = refs[2 + has_res] if has_norm else None
        o_ref = refs[2 + has_res + has_norm]
        h_ref = refs[3 + has_res + has_norm] if has_norm else None
        acc_ref = refs[-1]
        k = pl.program_id(2)

        @pl.when(k == 0)
        def _():
            acc_ref[...] = jnp.zeros_like(acc_ref)

        av = a_ref[(0,) * da] if da else a_ref[...]
        bv = b_ref[(0,) * db] if db else b_ref[...]
        acc_ref[...] += _dot(av.astype(BF16), bv.astype(BF16), ca, cb)

        @pl.when(k == nk - 1)
        def _():
            val = acc_ref[...]
            if has_res:
                val = val + r_ref[...]
            if do_:
                o_ref[(0,) * do_] = val.astype(out.dtype)
            else:
                o_ref[...] = val.astype(out.dtype)
            if has_norm:
                r = lax.rsqrt(jnp.mean(val * val, axis=-1, keepdims=True) + RMS_EPS)
                h_ref[...] = ((val * r) * w_ref[...]).astype(BF16)

    in_specs = [a_spec, b_spec] + ([r_spec] if has_res else [])
    args = (a, b) + ((res,) if has_res else ())
    out_specs, outs = o_spec, out
    if has_norm:
        assert acc_shape[1] == norm_w.shape[1] == out.shape[-1]
        in_specs.append(pl.BlockSpec((1, acc_shape[1]), lambda i, j, k: (0, 0)))
        args += (norm_w,)
        out_specs, outs = [o_spec, o_spec], [out, _sds(out.shape, BF16)]
    return _pc(body, name, grid, in_specs, out_specs, outs, [pltpu.VMEM(acc_shape, F32)], hosted=hosted)(*args)


def _mm(a, b, mode, name, out_dtype=F32, res=None, norm_w=None, hosted=None):
    if mode == "tn":
        r, m = a.shape
        n = b.shape[1]
        tm, tn, tk = _pick_tile(m), _pick_tile(n), _pick_tile(r)
        grid = (m // tm, n // tn, r // tk)
        a_spec = pl.BlockSpec((tk, tm), lambda i, j, k: (k, i))
        b_spec = pl.BlockSpec((tk, tn), lambda i, j, k: (k, j))
        ca, cb = 0, 0
    else:
        m, kd = a.shape
        n = b.shape[1] if mode == "nn" else b.shape[0]
        tm, tn, tk = _pick_tile(m), _pick_tile(n), _pick_tile(kd)
        grid = (m // tm, n // tn, kd // tk)
        a_spec = pl.BlockSpec((tm, tk), lambda i, j, k: (i, k))
        if mode == "nn":
            b_spec = pl.BlockSpec((tk, tn), lambda i, j, k: (k, j))
            ca, cb = 1, 0
        else:
            b_spec = pl.BlockSpec((tn, tk), lambda i, j, k: (j, k))
            ca, cb = 1, 1
    o_spec = pl.BlockSpec((tm, tn), lambda i, j, k: (i, j))
    return _mm_spec(a, b, name, grid, a_spec, b_spec, o_spec, _sds((m, n), out_dtype), ca, cb, (tm, tn), res=res, r_spec=o_spec,
                    norm_w=norm_w, hosted=hosted)


def _rms_fwd(x, w, name):
    s, d = x.shape
    ts = _row_tile(s)

    def body(x_ref, w_ref, o_ref):
        xv = x_ref[...]
        r = lax.rsqrt(jnp.mean(xv * xv, axis=-1, keepdims=True) + RMS_EPS)
        o_ref[...] = ((xv * r) * w_ref[...]).astype(BF16)

    row = pl.BlockSpec((ts, d), lambda i: (i, 0))
    return _pc(body, name, (s // ts,), [row, pl.BlockSpec((1, d), lambda i: (0, 0))], row, _sds((s, d), BF16))(x, w)


def _mm_dnorm(a, b, name, nk, a_spec, b_spec, ca, cb, drop, x, w, dres, hosted=None):
    s, d = x.shape
    tm = _pick_tile(s)
    da, db = drop

    def body(a_ref, b_ref, x_ref, w_ref, r_ref, dx_ref, dw_ref, acc_ref):
        i = pl.program_id(0)
        k = pl.program_id(2)

        @pl.when(k == 0)
        def _():
            acc_ref[...] = jnp.zeros_like(acc_ref)

        av = a_ref[(0,) * da] if da else a_ref[...]
        bv = b_ref[(0,) * db] if db else b_ref[...]
        acc_ref[...] += _dot(av.astype(BF16), bv.astype(BF16), ca, cb)

        @pl.when(k == nk - 1)
        def _():
            dhv = acc_ref[...]
            xv = x_ref[...]
            r = lax.rsqrt(jnp.mean(xv * xv, axis=-1, keepdims=True) + RMS_EPS)
            xhat = xv * r
            g = dhv * w_ref[...]
            dx_ref[...] = r_ref[...] + r * (g - xhat * jnp.mean(g * xhat, axis=-1, keepdims=True))
            part = jnp.sum(dhv * xhat, axis=0, keepdims=True)

            @pl.when(i == 0)
            def _():
                dw_ref[...] = part

            @pl.when(i > 0)
            def _():
                dw_ref[...] += part

    row = pl.BlockSpec((tm, d), lambda i, j, k: (i, 0))
    vec = pl.BlockSpec((1, d), lambda i, j, k: (0, 0))
    return list(_pc(body, name, (s // tm, 1, nk), [a_spec, b_spec, row, vec, row], [row, vec], [_sds((s, d)), _sds((1, d))],
                    [pltpu.VMEM((tm, d), F32)], hosted=hosted)(a, b, x, w, dres))


def _mm_dnorm_nt(dproj, w_in, name, x, w, dres, hosted=None):
    tm = _pick_tile(x.shape[0])
    tk = _pick_tile(dproj.shape[1])
    return _mm_dnorm(dproj, w_in, name, dproj.shape[1] // tk, pl.BlockSpec((tm, tk), lambda i, j, k: (i, k)),
                     pl.BlockSpec((D_MODEL, tk), lambda i, j, k: (0, k)), 1, 1, (0, 0), x, w, dres, hosted=hosted)


def _ffn_gate_up(h, w_gu, name, hosted=None):
    s = h.shape[0]
    tm = _pick_tile(s)

    def body(h_ref, wg_ref, wu_ref, gu_ref, a_ref):
        hv = h_ref[...]
        g = _dot(hv, wg_ref[0], 1, 0)
        u = _dot(hv, wu_ref[0], 1, 0)
        gu_ref[0, 0] = g.astype(BF16)
        gu_ref[0, 1] = u.astype(BF16)
        a_ref[0] = (g * _sigmoid(g) * u).astype(BF16)

    wblk = lambda off: pl.BlockSpec((1, D_MODEL, FF_BLOCK), lambda i, k: (k + off, 0, 0))
    return _pc(body, name, (s // tm, 4), [pl.BlockSpec((tm, D_MODEL), lambda i, k: (i, 0)), wblk(0), wblk(4)],
               [pl.BlockSpec((1, 2, tm, FF_BLOCK), lambda i, k: (k, 0, i, 0)), pl.BlockSpec((1, tm, FF_BLOCK), lambda i, k: (k, i, 0))],
               [_sds((4, 2, s, FF_BLOCK), BF16), _sds((4, s, FF_BLOCK), BF16)], hosted=hosted)(h, w_gu, w_gu)


def _stage_gather(stages, stage):
    return (stages[stage][0], True) if stages and stage in stages else None


def _stage_arrived(stages, stage, got, default=None):
    late = stages[stage][1](got) if stages and stage in stages else None
    return default if late is None else late


def _stage_slabs(hosted_fn, stage, **new):
    arrays = hosted_fn(stage, **new) if hosted_fn is not None else None
    return None if arrays is None else (arrays, False)


def _ffn_dgate_up(dy, w_down, gu, name, hosted=None):
    s = dy.shape[0]
    tm = _pick_tile(s)

    def body(dy_ref, w_ref, gu_ref, o_ref):
        dav = _dot(dy_ref[...].astype(BF16), w_ref[0], 1, 1)
        g = gu_ref[0, 0].astype(F32)
        u = gu_ref[0, 1].astype(F32)
        sg = _sigmoid(g)
        o_ref[0, 0] = (dav * u * (sg * (1.0 + g * (1.0 - sg)))).astype(BF16)
        o_ref[0, 1] = (dav * (g * sg)).astype(BF16)

    pair = pl.BlockSpec((1, 2, tm, FF_BLOCK), lambda i, k: (k, 0, i, 0))
    return _pc(body, name, (s // tm, 4),
               [pl.BlockSpec((tm, D_MODEL), lambda i, k: (i, 0)), pl.BlockSpec((1, FF_BLOCK, D_MODEL), lambda i, k: (k, 0, 0)), pair],
               pair, _sds((4, 2, s, FF_BLOCK), BF16), hosted=hosted)(dy, w_down, gu)


def _ffn_fwd(x, h, w_gu, w_down, tag, next_norm=None, stages=None):
    s = x.shape[0]
    tm = _pick_tile(s)
    gu, a, *got = _ffn_gate_up(h, w_gu, f"ffn_gu_{tag}", _stage_gather(stages, "ffn_gu"))
    w_down = _stage_arrived(stages, "ffn_gu", got, w_down)
    xspec = pl.BlockSpec((tm, D_MODEL), lambda i, j, k: (i, 0))
    hosted = _stage_gather(stages, "ffn_down")
    y = _mm_spec(a, w_down, f"ffn_down_{tag}", (s // tm, 1, 4),
                 pl.BlockSpec((1, tm, FF_BLOCK), lambda i, j, k: (k, i, 0)),
                 pl.BlockSpec((1, FF_BLOCK, D_MODEL), lambda i, j, k: (k, 0, 0)),
                 xspec, _sds((s, D_MODEL)), 1, 0, (tm, D_MODEL), drop=(1, 1, 0), res=x, r_spec=xspec, norm_w=next_norm,
                 hosted=hosted)
    n_own = 2 if next_norm is not None else 1
    own = list(y[:n_own]) if (hosted is not None or next_norm is not None) else [y]
    if hosted is not None:
        _stage_arrived(stages, "ffn_down", list(y[n_own:]))
    y, h_next = own if next_norm is not None else (own[0], None)
    return y, h_next, (x, h, gu, a)


def _ffn_bwd(dy, saved, norm_w, w_gu, w_down, tag, hosted_fn=None):
    x, h, gu, a = saved
    s = x.shape[0]
    tm = _pick_tile(s)
    got = {}
    hosted = _stage_slabs(hosted_fn, "ffn_gdown")
    g_down = _mm_spec(a, dy, f"ffn_gdown_{tag}", (4, 1, s // tm),
                      pl.BlockSpec((1, tm, FF_BLOCK), lambda i, j, k: (i, k, 0)),
                      pl.BlockSpec((tm, D_MODEL), lambda i, j, k: (k, 0)),
                      pl.BlockSpec((1, FF_BLOCK, D_MODEL), lambda i, j, k: (i, 0, 0)),
                      _sds((4, FF_BLOCK, D_MODEL), BF16), 0, 0, (FF_BLOCK, D_MODEL), drop=(1, 0, 1), hosted=hosted)
    if hosted is not None:
        g_down, *got["ffn_gdown"] = g_down
    hosted = _stage_slabs(hosted_fn, "ffn_dgu", g_down=g_down)
    dgu = _ffn_dgate_up(dy, w_down, gu, f"ffn_dgu_{tag}", hosted)
    if hosted is not None:
        dgu, *got["ffn_dgu"] = dgu
    g_gu = _mm_spec(h, dgu, f"ffn_ggu_{tag}", (NDEV, 1, s // tm),
                    pl.BlockSpec((tm, D_MODEL), lambda i, j, k: (k, 0)),
                    pl.BlockSpec((1, 1, tm, FF_BLOCK), lambda i, j, k: (i % 4, i // 4, k, 0)),
                    pl.BlockSpec((1, D_MODEL, FF_BLOCK), lambda i, j, k: (i, 0, 0)),
                    _sds((NDEV, D_MODEL, FF_BLOCK), BF16), 0, 0, (D_MODEL, FF_BLOCK), drop=(0, 2, 1))
    hosted = _stage_slabs(hosted_fn, "ffn_dh", g_gu=g_gu)
    dx, g_norm, *arrived = _mm_dnorm(dgu, w_gu, f"ffn_dh_{tag}", NDEV,
                                     pl.BlockSpec((1, 1, tm, FF_BLOCK), lambda i, j, k: (k % 4, k // 4, i, 0)),
                                     pl.BlockSpec((1, D_MODEL, FF_BLOCK), lambda i, j, k: (k, 0, 0)), 1, 1, (2, 1), x, norm_w, dy,
                                     hosted=hosted)
    if hosted is not None:
        got["ffn_dh"] = arrived
    return dx, g_norm, g_gu, g_down, got


def _prev_rows(cur, halo, j, first):
    rid = lax.broadcasted_iota(jnp.int32, cur.shape, 0)
    hid = lax.broadcasted_iota(jnp.int32, halo.shape, 0)
    out = pltpu.roll(cur, j, 0)
    for t in range(j):
        row = jnp.sum(jnp.where(hid == 8 - j + t, halo, 0.0), axis=0, keepdims=True)
        row = jnp.where(first, 0.0, row)
        out = jnp.where(rid == t, row, out)
    return out


def _next_rows(cur, halo, j, last):
    ts = cur.shape[0]
    rid = lax.broadcasted_iota(jnp.int32, cur.shape, 0)
    hid = lax.broadcasted_iota(jnp.int32, halo.shape, 0)
    out = pltpu.roll(cur, ts - j, 0)
    for t in range(j):
        row = jnp.sum(jnp.where(hid == t, halo, 0.0), axis=0, keepdims=True)
        row = jnp.where(last, 0.0, row)
        out = jnp.where(rid == ts - j + t, row, out)
    return out


def _halo_specs(ts, s, width, col):
    per = ts // 8
    nblk = s // 8
    prev = pl.BlockSpec((8, width), lambda i: (jnp.maximum(i * per - 1, 0), col))
    nxt = pl.BlockSpec((8, width), lambda i: (jnp.minimum((i + 1) * per, nblk - 1), col))
    return prev, nxt


def _cgate_fwd(p, w_dw, name, hosted=None):
    s = p.shape[0]
    d = D_MODEL
    ts = _row_tile(s)
    prev, _ = _halo_specs(ts, s, 3 * d, 0)

    def body(p_ref, h_ref, w_ref, z_ref):
        first = pl.program_id(0) == 0
        b = p_ref[:, :d]
        cv = p_ref[:, d:2 * d] * p_ref[:, 2 * d:]
        hcv = h_ref[:, d:2 * d] * h_ref[:, 2 * d:]
        u = w_ref[2:3, :] * cv + w_ref[1:2, :] * _prev_rows(cv, hcv, 1, first) + w_ref[0:1, :] * _prev_rows(cv, hcv, 2, first)
        z_ref[...] = (b * u).astype(BF16)

    return _pc(body, name, (s // ts,),
               [pl.BlockSpec((ts, 3 * d), lambda i: (i, 0)), prev, pl.BlockSpec((3, d), lambda i: (0, 0))],
               pl.BlockSpec((ts, d), lambda i: (i, 0)), _sds((s, d), BF16), hosted=hosted)(p, p, w_dw)


def _cgate_bwd(p, dz, w_dw, name):
    s = p.shape[0]
    d = D_MODEL
    ts = _row_tile(s)
    nt = s // ts
    p_prev, p_next = _halo_specs(ts, s, 3 * d, 0)
    _, dz_next = _halo_specs(ts, s, d, 0)

    def body(p_ref, hp_ref, hn_ref, dz_ref, dzn_ref, w_ref, dp_ref, dw_ref):
        i = pl.program_id(0)
        first = i == 0
        last = i == nt - 1
        b = p_ref[:, :d]
        c = p_ref[:, d:2 * d]
        v = p_ref[:, 2 * d:]
        cv = c * v
        hcv = hp_ref[:, d:2 * d] * hp_ref[:, 2 * d:]
        cv1 = _prev_rows(cv, hcv, 1, first)
        cv2 = _prev_rows(cv, hcv, 2, first)
        w0, w1, w2 = w_ref[0:1, :], w_ref[1:2, :], w_ref[2:3, :]
        u = w2 * cv + w1 * cv1 + w0 * cv2
        dzv = dz_ref[...]
        du = dzv * b
        dun = dzn_ref[...] * hn_ref[:, :d]
        dcv = w2 * du + w1 * _next_rows(du, dun, 1, last) + w0 * _next_rows(du, dun, 2, last)
        dp_ref[:, :d] = (dzv * u).astype(BF16)
        dp_ref[:, d:2 * d] = (dcv * v).astype(BF16)
        dp_ref[:, 2 * d:] = (dcv * c).astype(BF16)

        @pl.when(first)
        def _():
            dw_ref[...] = jnp.zeros_like(dw_ref)

        dw_ref[0:1, :] += jnp.sum(du * cv2, axis=0, keepdims=True)
        dw_ref[1:2, :] += jnp.sum(du * cv1, axis=0, keepdims=True)
        dw_ref[2:3, :] += jnp.sum(du * cv, axis=0, keepdims=True)

    wide = pl.BlockSpec((ts, 3 * d), lambda i: (i, 0))
    wspec = pl.BlockSpec((3, d), lambda i: (0, 0))
    return _pc(body, name, (nt,),
               [wide, p_prev, p_next, pl.BlockSpec((ts, d), lambda i: (i, 0)), dz_next, wspec],
               [wide, wspec], [_sds((s, 3 * d), BF16), _sds((3, d))])(p, p, p, dz, dz, w_dw)


def _conv_fwd(x, h, w_in, w_dw, w_out, tag, next_norm, stages=None):
    wn = _cols_from_blocks(w_in)
    hosted = _stage_gather(stages, "conv_in")
    p = _mm(h, wn, "nn", f"conv_in_{tag}", hosted=hosted)
    if hosted is not None:
        p, *got = p
        _stage_arrived(stages, "conv_in", got)
    hosted = _stage_gather(stages, "conv_gate")
    z = _cgate_fwd(p, w_dw, f"conv_gate_{tag}", hosted)
    if hosted is not None:
        z, *got = z
        w_out = _stage_arrived(stages, "conv_gate", got, w_out)
    y, h_next = _mm(z, w_out, "nn", f"conv_out_{tag}", res=x, norm_w=next_norm)
    return y, h_next, (x, h, p, z, wn)


def _conv_bwd(dy, saved, norm_w, w_in, w_dw, w_out, tag, hosted_fn=None):
    x, h, p, z, wn = saved
    dz = _mm(dy, w_out, "nt", f"conv_dz_{tag}")
    g_out = _mm(z, dy, "tn", f"conv_gout_{tag}", out_dtype=BF16)
    dp, g_dw = _cgate_bwd(p, dz, w_dw, f"conv_dgate_{tag}")
    g_in = _blocks_from_cols(_mm(h, dp, "tn", f"conv_gin_{tag}", out_dtype=BF16))
    hosted = _stage_slabs(hosted_fn, "conv_dh", g_in=g_in, g_out=g_out)
    dx, g_norm, *got = _mm_dnorm_nt(dp, wn, f"conv_dh_{tag}", x, norm_w, dy, hosted=hosted)
    return dx, g_norm, g_in, g_dw, g_out, ({"conv_dh": got} if hosted is not None else {})


def _tri(lower, n=LANES):
    r = lax.broadcasted_iota(jnp.int32, (n, n), 0)
    c = lax.broadcasted_iota(jnp.int32, (n, n), 1)
    return jnp.where((r >= c) if lower else (r <= c), 1.0, 0.0).astype(F32)


def _cumsum_rows(v, name):
    s = v.shape[0]
    rows = _row_tile(s, CUMSUM_ROWS)

    def body(v_ref, o_ref, carry_ref):
        @pl.when(pl.program_id(0) == 0)
        def _():
            carry_ref[...] = jnp.zeros_like(carry_ref)

        blk = v_ref[...]
        o_ref[...] = _dot(_tri(True, rows), blk, 1, 0, HI) + carry_ref[0:1, :]
        carry_ref[...] += jnp.sum(blk, axis=0, keepdims=True)

    spec = pl.BlockSpec((rows, LANES), lambda i: (i, 0))
    return _pc(body, name, (s // rows,), [spec], spec, _sds((s, LANES)), [pltpu.VMEM((8, LANES), F32)])(v)


def _fox_dcum(dck, dcq, name):
    s = dck.shape[0]
    rows = _row_tile(s, CUMSUM_ROWS)
    n = s // rows

    def body(k_ref, q_ref, o_ref, carry_ref):
        @pl.when(pl.program_id(0) == 0)
        def _():
            carry_ref[...] = jnp.zeros_like(carry_ref)

        head = lax.broadcasted_iota(jnp.int32, (ATTN_HEADS, LANES), 0)
        unit = jnp.where(head == lax.broadcasted_iota(jnp.int32, (ATTN_HEADS, LANES), 1), 1.0, 0.0).astype(F32)
        blk = _dot(q_ref[...], unit, 0, 0, HI)
        lane = lax.broadcasted_iota(jnp.int32, (rows, LANES), 1)
        for hd in range(ATTN_HEADS):
            first_lane = hd * HEAD_DIM
            pair = k_ref[:, first_lane // LANES * LANES:(first_lane // LANES + 1) * LANES]
            blk = blk + jnp.where(lane == hd, pltpu.roll(pair, (hd - first_lane) % LANES, axis=1), 0.0)
        o_ref[...] = _dot(_tri(False, rows), blk, 1, 0, HI) + carry_ref[0:1, :]
        carry_ref[...] += jnp.sum(blk, axis=0, keepdims=True)

    return _pc(body, name, (n,),
               [pl.BlockSpec((rows, D_MODEL), lambda i: (n - 1 - i, 0)), pl.BlockSpec((ATTN_HEADS, rows), lambda i: (0, n - 1 - i))],
               pl.BlockSpec((rows, LANES), lambda i: (n - 1 - i, 0)), _sds((s, LANES)), [pltpu.VMEM((8, LANES), F32)])(dck, dcq)


def _lo_mask(shape):
    return lax.broadcasted_iota(jnp.int32, shape, len(shape) - 1) < HEAD_DIM


def _half_sums(v, lo):
    sa = jnp.sum(jnp.where(lo, v, 0.0), axis=-1, keepdims=True)
    sb = jnp.sum(jnp.where(lo, 0.0, v), axis=-1, keepdims=True)
    return jnp.where(lo, sa, sb)


def _fox_prep_fwd(proj, gq, gk, name):
    s = proj.shape[0]
    ts = _row_tile(s)
    qscale = HEAD_DIM ** -0.5 * LOG2E

    def body(q_ref, k_ref, v_ref, gq_ref, gk_ref, qo_ref, ko_ref, vo_ref):
        lo = _lo_mask((ts, LANES))

        def hnorm(xv, g):
            ms = _half_sums(xv * xv, lo) * (1.0 / HEAD_DIM)
            return (xv * lax.rsqrt(ms + RMS_EPS)) * g

        for p in range(8):
            cols = slice(p * LANES, (p + 1) * LANES)
            qo_ref[:, cols] = (hnorm(q_ref[:, cols], gq_ref[...]) * qscale).astype(BF16)
            ko_ref[:, cols] = hnorm(k_ref[:, cols], gk_ref[...]).astype(BF16)
        vo_ref[...] = v_ref[...].astype(BF16)

    def wide(blk):
        return pl.BlockSpec((ts, D_MODEL), lambda i: (i, blk))

    gspec = pl.BlockSpec((1, LANES), lambda i: (0, 0))
    out = _sds((s, D_MODEL), BF16)
    return _pc(body, name, (s // ts,), [wide(0), wide(1), wide(2), gspec, gspec], [wide(0)] * 3, [out] * 3)(
        proj, proj, proj, gq, gk)


def _fox_logf(proj, bf, name):
    s = proj.shape[0]
    ts = _row_tile(s, 512)

    def body(f_ref, b_ref, o_ref):
        z = f_ref[...] + b_ref[...]
        lf = jnp.minimum(z, 0.0) - jnp.log(1.0 + jnp.exp(-jnp.abs(z)))
        real = lax.broadcasted_iota(jnp.int32, (ts, LANES), 1) < ATTN_HEADS
        o_ref[...] = jnp.where(real, lf, 0.0)

    return _pc(body, name, (s // ts,), [pl.BlockSpec((ts, LANES), lambda i: (i, 24)), pl.BlockSpec((1, LANES), lambda i: (0, 0))],
               pl.BlockSpec((ts, LANES), lambda i: (i, 0)), _sds((s, LANES)))(proj, bf)


def _fox_dlogf(proj, bf, dlf, name):
    s = proj.shape[0]
    ts = _row_tile(s, 512)

    def body(f_ref, b_ref, d_ref, o_ref, db_ref):
        z = f_ref[...] + b_ref[...]
        real = lax.broadcasted_iota(jnp.int32, (ts, LANES), 1) < ATTN_HEADS
        g = jnp.where(real, d_ref[...] * _sigmoid(-z), 0.0)
        o_ref[...] = g.astype(BF16)

        @pl.when(pl.program_id(0) == 0)
        def _():
            db_ref[...] = jnp.zeros_like(db_ref)

        db_ref[...] += jnp.sum(g, axis=0, keepdims=True)

    vec = pl.BlockSpec((1, LANES), lambda i: (0, 0))
    row = pl.BlockSpec((ts, LANES), lambda i: (i, 0))
    return _pc(body, name, (s // ts,), [pl.BlockSpec((ts, LANES), lambda i: (i, 24)), vec, row], [row, vec],
               [_sds((s, LANES), BF16), _sds((1, LANES))])(proj, bf, dlf)


def _decay_placement():
    pq = np.zeros((3 * LANES, D_MODEL), np.float32)
    pk = np.zeros((3 * LANES, D_MODEL), np.float32)
    oq = np.zeros((1, D_MODEL), np.float32)
    ok = np.zeros((1, D_MODEL), np.float32)
    for hd in range(ATTN_HEADS):
        base = (hd // 2) * LANES + (0 if hd % 2 else HEAD_DIM)
        for term in range(3):
            pq[term * LANES + hd, base + term] = 1.0
            pk[term * LANES + hd, base + 3 + term] = -1.0
        oq[0, base + 3:base + 6] = 1.0
        ok[0, base:base + 3] = 1.0
    return jnp.asarray(pq, BF16), jnp.asarray(pk, BF16), jnp.asarray(oq), jnp.asarray(ok)


def _decay_terms(cum, name):
    s = cum.shape[0]
    ts = _row_tile(s)

    def body(c_ref, pq_ref, pk_ref, oq_ref, ok_ref, aq_ref, ak_ref):
        c2 = c_ref[...] * LOG2E
        hi = c2.astype(BF16)
        rest = c2 - hi.astype(F32)
        mid = rest.astype(BF16)
        low = (rest - mid.astype(F32)).astype(BF16)
        terms = jnp.concatenate([hi, mid, low], axis=1)
        aq_ref[...] = (_dot(terms, pq_ref[...], 1, 0) + oq_ref[...]).astype(BF16)
        ak_ref[...] = (_dot(terms, pk_ref[...], 1, 0) + ok_ref[...]).astype(BF16)

    mat = pl.BlockSpec((3 * LANES, D_MODEL), lambda i: (0, 0))
    row = pl.BlockSpec((1, D_MODEL), lambda i: (0, 0))
    out = pl.BlockSpec((ts, D_MODEL), lambda i: (i, 0))
    return _pc(body, name, (s // ts,), [pl.BlockSpec((ts, LANES), lambda i: (i, 0)), mat, mat, row, row], [out, out],
               [_sds((s, D_MODEL), BF16), _sds((s, D_MODEL), BF16)])(cum, *_decay_placement())


def _attn_tiles(s):
    t = s
    for cand in (ATTN_TILE, ATTN_TILE // 2):
        if s % cand == 0:
            t = cand
            break
    return t, s // t


def _tri_steps(n, by_key):
    if by_key:
        pairs = [(q, k) for k in range(n) for q in range(k, n)]
    else:
        pairs = [(q, k) for q in range(n) for k in range(q + 1)]
    arr = np.asarray(pairs, np.int32)
    return jnp.asarray(arr[:, 0]), jnp.asarray(arr[:, 1])


def _attn_call(body, name, s, by_key, inputs, in_kinds, out_kinds, out_shapes, scratch, hosted=None, vmem=VMEM_LIMIT_BYTES):
    t, n = _attn_tiles(s)
    qi_arr, ki_arr = _tri_steps(n, by_key)
    nsteps = int(qi_arr.shape[0])
    specs = {
        "q": pl.BlockSpec((t, LANES), lambda p, i, qi, ki: (qi[i], p)),
        "k": pl.BlockSpec((t, LANES), lambda p, i, qi, ki: (ki[i], p)),
        "r": pl.BlockSpec((1, 2, t), lambda p, i, qi, ki: (p, 0, qi[i])),
        "m": pl.BlockSpec((1, t, t), lambda p, i, qi, ki: (jnp.where(qi[i] == ki[i], 1, 0), 0, 0)),
        "Q": pl.BlockSpec((1, LANES, s), lambda p, i, qi, ki: (p, 0, 0)),
        "R": pl.BlockSpec((1, 2, s), lambda p, i, qi, ki: (p, 0, 0)),
    }
    in_specs = [specs[c] for c in in_kinds]
    out_specs = [specs[c] for c in out_kinds]
    out_shapes, scratch, inputs = list(out_shapes), list(scratch), list(inputs)
    run = body
    if hosted is not None:
        arrays, gather = hosted
        na, n_in, n_out, n_scr = len(arrays), len(inputs), len(out_kinds), len(scratch)
        pick, xouts, sems = _exchange_parts(arrays, gather)

        def run(qi_ref, ki_ref, *refs):
            ins, srcs = refs[:n_in], refs[n_in:n_in + na]
            outs, dsts = refs[n_in + na:n_in + na + n_out], refs[n_in + na + n_out:n_in + 2 * na + n_out]
            scr, xsems = refs[n_in + 2 * na + n_out:n_in + 2 * na + n_out + n_scr], refs[n_in + 2 * na + n_out + n_scr:]
            p = pl.program_id(0)
            i = pl.program_id(1)

            @pl.when(jnp.logical_and(p == 0, i == 0))
            def _():
                _exchange_start(_exchange_copies(pick(srcs), dsts, *xsems))

            body(qi_ref, ki_ref, *ins, *outs, *scr)

            @pl.when(jnp.logical_and(p == 7, i == nsteps - 1))
            def _():
                _exchange_wait(_exchange_copies(pick(srcs), dsts, *xsems))

        hbm = pl.BlockSpec(memory_space=pl.ANY)
        in_specs += [hbm] * na
        out_specs += [hbm] * na
        out_shapes += xouts
        scratch += sems
        inputs += list(arrays)
    grid_spec = pltpu.PrefetchScalarGridSpec(
        num_scalar_prefetch=2, grid=(8, nsteps), in_specs=in_specs, out_specs=out_specs, scratch_shapes=scratch)
    params = pltpu.CompilerParams(dimension_semantics=("arbitrary", "arbitrary"), vmem_limit_bytes=vmem)
    return pl.pallas_call(run, name=name, grid_spec=grid_spec, out_shape=out_shapes, compiler_params=params)(
        qi_arr, ki_arr, *inputs)


def _biased_kq(q2, k2, aq, ak, lo):
    sa = _dot(jnp.where(lo, k2, ak), jnp.where(lo, q2, aq), 1, 1)
    sb = _dot(jnp.where(lo, ak, k2), jnp.where(lo, aq, q2), 1, 1)
    return sa, sb


def _causal_bias(s):
    t, _ = _attn_tiles(s)
    kid = lax.broadcasted_iota(jnp.int32, (t, t), 0)
    qid = lax.broadcasted_iota(jnp.int32, (t, t), 1)
    return jnp.stack([jnp.zeros((t, t), BF16), jnp.where(kid > qid, -jnp.inf, 0.0).astype(BF16)])


def _fold8(v, op):
    return op(v.reshape(v.shape[0] // 8, 8, v.shape[1]), axis=0)


def _chunk(ref, mask_ref, hd, r):
    rows = slice(r * ATTN_ROWS, (r + 1) * ATTN_ROWS)
    return rows, ref[hd, rows, :] + mask_ref[0, rows, :].astype(F32)


def _flash_fwd(qs, kn, vb, augq, augk, cmask, name, hosted=None):
    s = qs.shape[0]
    t, n = _attn_tiles(s)
    nch = t // ATTN_ROWS

    def body(qi_ref, ki_ref, q_ref, k_ref, v_ref, aq_ref, ak_ref, mk_ref, o_ref, lse_ref, s_ref, p_ref, m_ref, l_ref, acc_ref):
        i = pl.program_id(1)
        qi = qi_ref[i]
        ki = ki_ref[i]

        @pl.when(ki == 0)
        def _():
            m_ref[...] = jnp.full_like(m_ref, -jnp.inf)
            l_ref[...] = jnp.zeros_like(l_ref)
            acc_ref[...] = jnp.zeros_like(acc_ref)

        lo = _lo_mask((t, LANES))
        rowlo = lax.broadcasted_iota(jnp.int32, (LANES, t), 0) < HEAD_DIM
        v2 = v_ref[...]
        sa, sb = _biased_kq(q_ref[...], k_ref[...], aq_ref[...], ak_ref[...], lo)
        s_ref[0] = sa
        s_ref[1] = sb
        alphas, pvs = [], []
        for hd in range(2):
            mx = jnp.full((8, t), -jnp.inf, F32)
            for r in range(nch):
                _, sc = _chunk(s_ref, mk_ref, hd, r)
                mx = jnp.maximum(mx, _fold8(sc, jnp.max))
            m_prev = m_ref[hd:hd + 1, :]
            m_new = jnp.maximum(m_prev, jnp.max(mx, axis=0, keepdims=True))
            ls = jnp.zeros((8, t), F32)
            for r in range(nch):
                rows, sc = _chunk(s_ref, mk_ref, hd, r)
                pm = jnp.exp2(sc - m_new)
                ls = ls + _fold8(pm, jnp.sum)
                p_ref[hd, rows, :] = pm.astype(BF16)
            alpha = jnp.exp2(m_prev - m_new)
            l_ref[hd:hd + 1, :] = alpha * l_ref[hd:hd + 1, :] + jnp.sum(ls, axis=0, keepdims=True)
            m_ref[hd:hd + 1, :] = m_new
            alphas.append(alpha)
            pvs.append(_dot(v2, p_ref[hd], 0, 0))
        acc_ref[...] = jnp.where(rowlo, alphas[0], alphas[1]) * acc_ref[...] + jnp.where(rowlo, pvs[0], pvs[1])

        @pl.when(ki == qi)
        def _():
            o_ref[...] = (acc_ref[...] / jnp.where(rowlo, l_ref[0:1, :], l_ref[1:2, :])).T
            lse_ref[0] = m_ref[0:2, :] + jnp.log2(l_ref[0:2, :])

    stat = pltpu.VMEM((8, t), F32)
    return _attn_call(body, name, s, False, (qs, kn, vb, augq, augk, cmask), "qkkqkm", "qr",
                      [_sds((s, D_MODEL)), _sds((8, 2, s))],
                      [pltpu.VMEM((2, t, t), F32), pltpu.VMEM((2, t, t), BF16), stat, stat, pltpu.VMEM((LANES, t), F32)],
                      hosted=hosted)


def _fox_delta(do, o, name):
    s = do.shape[0]
    ts = _row_tile(s)

    def body(do_ref, o_ref, d_ref):
        head = lax.broadcasted_iota(jnp.int32, (ATTN_HEADS, D_MODEL), 0)
        col = lax.broadcasted_iota(jnp.int32, (ATTN_HEADS, D_MODEL), 1)
        member = jnp.where(col // HEAD_DIM == head, 1.0, 0.0).astype(F32)
        d_ref[...] = _dot(member, do_ref[...] * o_ref[...], 1, 1, HI)

    spec = pl.BlockSpec((ts, D_MODEL), lambda i: (i, 0))
    return _pc(body, name, (s // ts,), [spec, spec], pl.BlockSpec((ATTN_HEADS, ts), lambda i: (0, i)), _sds((ATTN_HEADS, s)))(do, o)


def _bwd_tile(q_ref, k_ref, v_ref, aq_ref, ak_ref, do_ref, s_ref, dp_ref, lo):
    do2 = do_ref[...].astype(BF16)
    zero = jnp.zeros_like(do2)
    v2 = v_ref[...]
    sa, sb = _biased_kq(q_ref[...], k_ref[...], aq_ref[...], ak_ref[...], lo)
    s_ref[0] = sa
    s_ref[1] = sb
    dp_ref[0] = _dot(v2, jnp.where(lo, do2, zero), 1, 1)
    dp_ref[1] = _dot(v2, jnp.where(lo, zero, do2), 1, 1)
    return do2


def _bwd_chunk(s_ref, dp_ref, mk_ref, lse_ref, dl_ref, hd, r):
    rows, sc = _chunk(s_ref, mk_ref, hd, r)
    pm = jnp.exp2(sc - lse_ref[0, hd:hd + 1, :])
    ds = pm * (dp_ref[hd, rows, :] - dl_ref[0, hd:hd + 1, :])
    return rows, pm, ds


def _flash_bwd(qs, kn, vb, augq, augk, cmask, do, lse, delta, name, hosted=None):
    s = qs.shape[0]
    t, n = _attn_tiles(s)
    nch = t // ATTN_ROWS

    def body(qi_ref, ki_ref, q_ref, k_ref, v_ref, aq_ref, ak_ref, mk_ref, do_ref, lse_ref, dl_ref,
             dk_ref, dv_ref, dc_ref, dq_ref, dcq_ref, s_ref, dp_ref, p_ref, ds_ref, dka_ref, dva_ref, dca_ref):
        i = pl.program_id(1)
        qi = qi_ref[i]
        ki = ki_ref[i]

        @pl.when(i == 0)
        def _():
            dq_ref[...] = jnp.zeros_like(dq_ref)
            dcq_ref[...] = jnp.zeros_like(dcq_ref)

        @pl.when(qi == ki)
        def _():
            dka_ref[...] = jnp.zeros_like(dka_ref)
            dva_ref[...] = jnp.zeros_like(dva_ref)
            dca_ref[...] = jnp.zeros_like(dca_ref)

        lo = _lo_mask((t, LANES))
        rowlo = lax.broadcasted_iota(jnp.int32, (LANES, t), 0) < HEAD_DIM
        do2 = _bwd_tile(q_ref, k_ref, v_ref, aq_ref, ak_ref, do_ref, s_ref, dp_ref, lo)
        q2 = q_ref[...]
        k2 = k_ref[...]
        qcols = pl.ds(pl.multiple_of(qi * t, t), t)
        dvs, dks, dqs = [], [], []
        for hd in range(2):
            rs = jnp.zeros((8, t), F32)
            for r in range(nch):
                rows, pm, ds = _bwd_chunk(s_ref, dp_ref, mk_ref, lse_ref, dl_ref, hd, r)
                rs = rs + _fold8(ds, jnp.sum)
                part = ds[:, 0:LANES]
                for c in range(1, t // LANES):
                    part = part + ds[:, c * LANES:(c + 1) * LANES]
                dca_ref[hd, rows, :] += part
                p_ref[hd, rows, :] = pm.astype(BF16)
                ds_ref[hd, rows, :] = ds.astype(BF16)
            dcq_ref[0, hd:hd + 1, qcols] += jnp.sum(rs, axis=0, keepdims=True)
            dvs.append(_dot(p_ref[hd], do2, 1, 0))
            dks.append(_dot(ds_ref[hd], q2, 1, 0))
            dqs.append(_dot(k2, ds_ref[hd], 0, 0))
        dva_ref[...] += jnp.where(lo, dvs[0], dvs[1])
        dka_ref[...] += jnp.where(lo, dks[0], dks[1])
        dq_ref[0, :, qcols] += jnp.where(rowlo, dqs[0], dqs[1])

        @pl.when(qi == n - 1)
        def _():
            dk_ref[...] = dka_ref[...] * LN2
            dv_ref[...] = dva_ref[...]
            dc_ref[...] = -jnp.where(lo, jnp.sum(dca_ref[0], axis=-1, keepdims=True), jnp.sum(dca_ref[1], axis=-1, keepdims=True))

    out = _sds((s, D_MODEL))
    return _attn_call(body, name, s, True, (qs, kn, vb, augq, augk, cmask, do, lse, delta), "qkkqkmqrr", "kkkQR",
                      [out, out, out, _sds((8, LANES, s)), _sds((8, 2, s))],
                      [pltpu.VMEM((2, t, t), F32), pltpu.VMEM((2, t, t), F32), pltpu.VMEM((2, t, t), BF16),
                       pltpu.VMEM((2, t, t), BF16), pltpu.VMEM((t, LANES), F32), pltpu.VMEM((t, LANES), F32),
                       pltpu.VMEM((2, t, LANES), F32)], hosted=hosted, vmem=ATTN_BWD_VMEM_BYTES)


def _fox_prep_bwd(proj, dqs, dk, dv, gq, gk, name):
    s = proj.shape[0]
    ts = _row_tile(s)
    scale = HEAD_DIM ** -0.5

    def body(q_ref, k_ref, dq_ref, dk_ref, dv_ref, gq_ref, gk_ref, oq_ref, ok_ref, ov_ref, dgq_ref, dgk_ref):
        lo = _lo_mask((ts, LANES))

        @pl.when(pl.program_id(0) == 0)
        def _():
            dgq_ref[...] = jnp.zeros_like(dgq_ref)
            dgk_ref[...] = jnp.zeros_like(dgk_ref)

        def back(xv, dout, g):
            r = lax.rsqrt(_half_sums(xv * xv, lo) * (1.0 / HEAD_DIM) + RMS_EPS)
            y = xv * r
            dy = dout * g
            dx = r * (dy - y * (_half_sums(dy * y, lo) * (1.0 / HEAD_DIM)))
            return dx, jnp.sum(dout * y, axis=0, keepdims=True)

        for p in range(8):
            cols = slice(p * LANES, (p + 1) * LANES)
            dxq, dgq = back(q_ref[:, cols], dq_ref[p].T * scale, gq_ref[...])
            dxk, dgk = back(k_ref[:, cols], dk_ref[:, cols], gk_ref[...])
            oq_ref[:, cols] = dxq.astype(BF16)
            ok_ref[:, cols] = dxk.astype(BF16)
            dgq_ref[...] += dgq
            dgk_ref[...] += dgk
        ov_ref[...] = dv_ref[...].astype(BF16)

    def wide(blk):
        return pl.BlockSpec((ts, D_MODEL), lambda i: (i, blk))

    gspec = pl.BlockSpec((1, LANES), lambda i: (0, 0))
    out = _sds((s, D_MODEL), BF16)
    dqt = pl.BlockSpec((8, LANES, ts), lambda i: (0, 0, i))
    return _pc(body, name, (s // ts,), [wide(0), wide(1), dqt, wide(0), wide(0), gspec, gspec],
               [wide(0)] * 3 + [gspec] * 2, [out] * 3 + [_sds((1, LANES))] * 2)(proj, proj, dqs, dk, dv, gq, gk)


def _fox_fwd(x, h, w_in, b_f, q_gain, k_gain, w_out, next_norm, hosted=None):
    proj = _mm(h, w_in, "nn", "fox_in")
    gq = jnp.tile(q_gain, (1, 2))
    gk = jnp.tile(k_gain, (1, 2))
    bf = jnp.pad(b_f, ((0, 0), (0, LANES - ATTN_HEADS)))
    qs, kn, vb = _fox_prep_fwd(proj, gq, gk, "fox_prep")
    augq, augk = _decay_terms(_cumsum_rows(_fox_logf(proj, bf, "fox_logf"), "fox_cum"), "fox_decay")
    cmask = _causal_bias(x.shape[0])
    o, lse, *got = _flash_fwd(qs, kn, vb, augq, augk, cmask, "fox_attn", hosted=hosted)
    y, h_next = _mm(o, w_out, "nn", "fox_out", res=x, norm_w=next_norm)
    return y, h_next, (x, h, proj, gq, gk, bf, qs, kn, vb, augq, augk, cmask, o, lse), got


def _fox_bwd(dy, saved, norm_w, w_in, w_out, hosted=None):
    x, h, proj, gq, gk, bf, qs, kn, vb, augq, augk, cmask, o, lse = saved
    s = x.shape[0]
    do = _mm(dy, w_out, "nt", "fox_do")
    g_out = _mm(o, dy, "tn", "fox_gout", out_dtype=BF16)
    delta = _fox_delta(do, o, "fox_delta").reshape(8, 2, s)
    dk, dv, dck, dqs, dcq, *got = _flash_bwd(qs, kn, vb, augq, augk, cmask, do, lse, delta, "fox_dattn", hosted=hosted)
    dlf = _fox_dcum(dck, dcq.reshape(ATTN_HEADS, s), "fox_dcum")
    dfl, g_bf = _fox_dlogf(proj, bf, dlf, "fox_dlogf")
    dq_o, dk_o, dv_o, g_gq, g_gk = _fox_prep_bwd(proj, dqs, dk, dv, gq, gk, "fox_dprep")
    dproj = jnp.concatenate([dq_o, dk_o, dv_o, dfl], axis=1)
    g_in = _mm(h, dproj, "tn", "fox_gin", out_dtype=BF16)
    dx, g_norm = _mm_dnorm_nt(dproj, w_in, "fox_dh", x, norm_w, dy)
    g_q = g_gq[:, :HEAD_DIM] + g_gq[:, HEAD_DIM:]
    g_k = g_gk[:, :HEAD_DIM] + g_gk[:, HEAD_DIM:]
    return dx, g_norm, g_in[:, :FOX_IN], g_bf[:, :ATTN_HEADS], g_q, g_k, g_out, got


def _ssd_conv_fwd(proj, cw, cb, name):
    s = proj.shape[0]
    ts = _row_tile(s)
    w = 1024
    per = ts // 8

    def body(p_ref, h_ref, w_ref, b_ref, o_ref):
        first = pl.program_id(0) == 0
        cur = p_ref[...]
        halo = h_ref[...]
        u = w_ref[3:4, :] * cur + b_ref[...]
        for j in range(1, 4):
            u = u + w_ref[3 - j:4 - j, :] * _prev_rows(cur, halo, j, first)
        o_ref[...] = u * _sigmoid(u)

    return _pc(body, name, (s // ts, 4),
               [pl.BlockSpec((ts, w), lambda i, j: (i, 2 + j)),
                pl.BlockSpec((8, w), lambda i, j: (jnp.maximum(i * per - 1, 0), 2 + j)),
                pl.BlockSpec((4, w), lambda i, j: (0, j)), pl.BlockSpec((1, w), lambda i, j: (0, j))],
               pl.BlockSpec((ts, w), lambda i, j: (i, j)), _sds((s, SSM_CONV_DIM)))(proj, proj, cw, cb)


def _ssd_conv_bwd(proj, d, first_col, cw, cb, name):
    s = proj.shape[0]
    ts = _row_tile(s)
    nt = s // ts
    w = SSD_DCONV_COLS
    ncol = d.shape[1] // w
    first_col = first_col * (1024 // w)
    proj_col = 2 * (1024 // w)
    per = ts // 8
    nblk = s // 8

    def body(p_ref, hp_ref, hn_ref, d_ref, dn_ref, w_ref, b_ref, o_ref, dw_ref, db_ref):
        i = pl.program_id(1)
        first = i == 0
        last = i == nt - 1
        cur = p_ref[...]
        prev = [cur] + [_prev_rows(cur, hp_ref[...], j, first) for j in range(1, 4)]
        nxt = hn_ref[...]
        tail = cur[ts - 8:, :]
        u = b_ref[...]
        un = b_ref[...]
        for j in range(4):
            u = u + w_ref[3 - j:4 - j, :] * prev[j]
            un = un + w_ref[3 - j:4 - j, :] * (nxt if j == 0 else _prev_rows(nxt, tail, j, False))
        sg = _sigmoid(u)
        g = d_ref[...] * (sg * (1.0 + u * (1.0 - sg)))
        sn = _sigmoid(un)
        gn = dn_ref[...] * (sn * (1.0 + un * (1.0 - sn)))

        @pl.when(first)
        def _():
            dw_ref[...] = jnp.zeros_like(dw_ref)
            db_ref[...] = jnp.zeros_like(db_ref)

        dpre = w_ref[3:4, :] * g
        for j in range(1, 4):
            dpre = dpre + w_ref[3 - j:4 - j, :] * _next_rows(g, gn, j, last)
        for j in range(4):
            dw_ref[3 - j:4 - j, :] += jnp.sum(g * prev[j], axis=0, keepdims=True)
        db_ref[...] += jnp.sum(g, axis=0, keepdims=True)
        o_ref[...] = dpre.astype(BF16)

    tile = pl.BlockSpec((ts, w), lambda j, i: (i, j))
    wspec = lambda off: pl.BlockSpec((4, w), lambda j, i: (0, off + j))
    vec = lambda off: pl.BlockSpec((1, w), lambda j, i: (0, off + j))
    nxt_blk = lambda off: pl.BlockSpec((8, w), lambda j, i: (jnp.minimum((i + 1) * per, nblk - 1), off + j))
    in_proj = proj_col + first_col
    return _pc(body, name, (ncol, nt),
               [pl.BlockSpec((ts, w), lambda j, i: (i, in_proj + j)),
                pl.BlockSpec((8, w), lambda j, i: (jnp.maximum(i * per - 1, 0), in_proj + j)), nxt_blk(in_proj),
                tile, nxt_blk(0), wspec(first_col), vec(first_col)],
               [tile, wspec(0), vec(0)], [_sds((s, ncol * w), BF16), _sds((4, ncol * w)), _sds((1, ncol * w))])(
                   proj, proj, proj, d, d, cw, cb)


def _ssd_dt_fwd(proj, bias, a_neg, name):
    s = proj.shape[0]
    n = s // SSM_CHUNK

    def body(r_ref, b_ref, a_ref, dt_ref, ac_ref):
        real = lax.broadcasted_iota(jnp.int32, (SSM_CHUNK, LANES), 1) < SSM_HEADS
        dt = jnp.where(real, _softplus(r_ref[...] + b_ref[...]), 0.0)
        dt_ref[...] = dt
        ac_ref[...] = _dot(_tri(True), dt * a_ref[...], 1, 0, HI)

    vec = pl.BlockSpec((1, LANES), lambda c: (0, 0))
    row = pl.BlockSpec((SSM_CHUNK, LANES), lambda c: (c, 0))
    return _pc(body, name, (n,), [pl.BlockSpec((SSM_CHUNK, LANES), lambda c: (c, 48)), vec, vec], [row, row],
               [_sds((s, LANES)), _sds((s, LANES))])(proj, bias, a_neg)


def _ssd_dt_bwd(proj, bias, ddt, name):
    s = proj.shape[0]
    ts = _row_tile(s, 512)

    def body(r_ref, b_ref, d_ref, o_ref, db_ref):
        real = lax.broadcasted_iota(jnp.int32, (ts, LANES), 1) < SSM_HEADS
        g = jnp.where(real, d_ref[...] * _sigmoid(r_ref[...] + b_ref[...]), 0.0)
        o_ref[...] = g.astype(BF16)

        @pl.when(pl.program_id(0) == 0)
        def _():
            db_ref[...] = jnp.zeros_like(db_ref)

        db_ref[...] += jnp.sum(g, axis=0, keepdims=True)

    vec = pl.BlockSpec((1, LANES), lambda i: (0, 0))
    row = pl.BlockSpec((ts, LANES), lambda i: (i, 0))
    return _pc(body, name, (s // ts,), [pl.BlockSpec((ts, LANES), lambda i: (i, 48)), vec, row], [row, vec],
               [_sds((s, LANES), BF16), _sds((1, LANES))])(proj, bias, ddt)


def _pair_cols(cols, k0, lo):
    return jnp.where(lo, cols[:, k0:k0 + 1], cols[:, k0 + 1:k0 + 2])


def _last_lane(row):
    lane = lax.broadcasted_iota(jnp.int32, row.shape, 1)
    return jnp.sum(jnp.where(lane == SSM_CHUNK - 1, row, 0.0), axis=-1, keepdims=True)


SSD_FWD_GROUPS = 2
SSD_BWD_GROUPS = 1


def _ssd_specs(nc, rev, n):
    cc = (lambda c: nc - 1 - c) if rev else (lambda c: c)
    nb = SSM_INNER // (LANES * n)
    return dict(
        x=pl.BlockSpec((SSM_CHUNK, 256 * n), lambda g, c: (cc(c), g)),
        b=pl.BlockSpec((SSM_CHUNK, LANES * n), lambda g, c: (cc(c), nb + g)),
        c=pl.BlockSpec((SSM_CHUNK, LANES * n), lambda g, c: (cc(c), nb + SSM_GROUPS // n + g)),
        col=pl.BlockSpec((n, SSM_CHUNK, 4), lambda g, c: (g, cc(c), 0)),
        row=pl.BlockSpec((n, 4, SSM_CHUNK), lambda g, c: (g, 0, cc(c))),
        grp=pl.BlockSpec((n, 1, 256), lambda g, c: (g, 0, 0)),
        grow=pl.BlockSpec((n, 4, LANES), lambda g, c: (g, 0, 0)),
        hs=pl.BlockSpec((1, n, 256, SSM_STATE), lambda g, c: (cc(c), g, 0, 0)),
        bc=pl.BlockSpec((SSM_CHUNK, LANES * n), lambda g, c: (cc(c), g)),
    )


def _ssd_scan_fwd(xbc, dtc, acol, drow, arow, dskip, name):
    s = xbc.shape[0]
    nc = s // SSM_CHUNK
    n = SSD_FWD_GROUPS
    sp = _ssd_specs(nc, False, n)
    L = SSM_CHUNK

    def body(x_ref, b_ref, c_ref, dtc_ref, ac_ref, dr_ref, ar_ref, dk_ref, y_ref, hs_ref, h_ref):
        @pl.when(pl.program_id(1) == 0)
        def _():
            h_ref[...] = jnp.zeros_like(h_ref)

        for gi in range(n):
            group(gi, x_ref, b_ref, c_ref, dtc_ref, ac_ref, dr_ref, ar_ref, dk_ref, y_ref, hs_ref, h_ref)

    def group(gi, x_ref, b_ref, c_ref, dtc_ref, ac_ref, dr_ref, ar_ref, dk_ref, y_ref, hs_ref, h_ref):
        x0 = gi * 256
        bb = b_ref[:, gi * LANES:(gi + 1) * LANES].astype(BF16)
        cb = c_ref[:, gi * LANES:(gi + 1) * LANES].astype(BF16)
        gm = _dot(cb, bb, 1, 1)
        dtc = dtc_ref[gi]
        ac = ac_ref[gi]
        dr = dr_ref[gi]
        ar = ar_ref[gi]
        dsk = dk_ref[gi]
        hs_ref[0, gi] = h_ref[gi]
        tril = lax.broadcasted_iota(jnp.int32, (L, L), 0) >= lax.broadcasted_iota(jnp.int32, (L, L), 1)
        lo = _lo_mask((L, LANES))
        rowlo = lax.broadcasted_iota(jnp.int32, (L, LANES), 0) < HEAD_DIM
        for pr in range(2):
            k0 = 2 * pr
            xp = x_ref[:, x0 + pr * LANES:x0 + (pr + 1) * LANES]
            xpb = xp.astype(BF16)
            hp = h_ref[gi, pr * LANES:(pr + 1) * LANES, :]
            yd, al = [], []
            for k in (k0, k0 + 1):
                seg = ac[:, k:k + 1] - ar[k:k + 1, :]
                wk = gm * jnp.exp(jnp.where(tril, seg, -jnp.inf)) * dr[k:k + 1, :]
                yd.append(_dot(wk.astype(BF16), xpb, 1, 0))
                al.append(_last_lane(ar[k:k + 1, :]))
            e = jnp.exp(_pair_cols(ac, k0, lo))
            yo = _dot(cb, hp.astype(BF16), 1, 1) * e
            y_ref[:, x0 + pr * LANES:x0 + (pr + 1) * LANES] = (
                jnp.where(lo, yd[0], yd[1]) + yo + dsk[:, pr * LANES:(pr + 1) * LANES] * xp)
            wp = jnp.where(lo, jnp.exp(al[0] - ac[:, k0:k0 + 1]) * dtc[:, k0:k0 + 1],
                           jnp.exp(al[1] - ac[:, k0 + 1:k0 + 2]) * dtc[:, k0 + 1:k0 + 2])
            st = _dot((xp * wp).astype(BF16), bb, 0, 0)
            dec = jnp.where(rowlo, jnp.exp(al[0]), jnp.exp(al[1]))
            h_ref[gi, pr * LANES:(pr + 1) * LANES, :] = dec * hp + st

    return _pc(body, name, (SSM_GROUPS // n, nc),
               [sp["x"], sp["b"], sp["c"], sp["col"], sp["col"], sp["row"], sp["row"], sp["grp"]],
               [sp["x"], sp["hs"]], [_sds((s, SSM_INNER)), _sds((nc, SSM_GROUPS, 256, SSM_STATE))],
               [pltpu.VMEM((n, 256, SSM_STATE), F32)])(xbc, xbc, xbc, dtc, acol, drow, arow, dskip)


def _ssd_scan_bwd(xbc, dtc, acol, drow, arow, dskip, agrp, hs, dy, name):
    s = xbc.shape[0]
    nc = s // SSM_CHUNK
    n = SSD_BWD_GROUPS
    sp = _ssd_specs(nc, True, n)
    L = SSM_CHUNK

    def body(x_ref, b_ref, c_ref, dtc_ref, ac_ref, dr_ref, ar_ref, dk_ref, ag_ref, hs_ref, dy_ref,
             dx_ref, db_ref, dc_ref, ddt_ref, da_ref, dd_ref, dh_ref):
        @pl.when(pl.program_id(1) == 0)
        def _():
            dh_ref[...] = jnp.zeros_like(dh_ref)
            da_ref[...] = jnp.zeros_like(da_ref)
            dd_ref[...] = jnp.zeros_like(dd_ref)

        for gi in range(n):
            group(gi, x_ref, b_ref, c_ref, dtc_ref, ac_ref, dr_ref, ar_ref, dk_ref, ag_ref, hs_ref, dy_ref,
                  dx_ref, db_ref, dc_ref, ddt_ref, da_ref, dd_ref, dh_ref)

    def group(gi, x_ref, b_ref, c_ref, dtc_ref, ac_ref, dr_ref, ar_ref, dk_ref, ag_ref, hs_ref, dy_ref,
              dx_ref, db_ref, dc_ref, ddt_ref, da_ref, dd_ref, dh_ref):
        x0 = gi * 256
        bcols = slice(gi * LANES, (gi + 1) * LANES)
        bb = b_ref[:, bcols].astype(BF16)
        cb = c_ref[:, bcols].astype(BF16)
        gm = _dot(cb, bb, 1, 1)
        dtc = dtc_ref[gi]
        ac = ac_ref[gi]
        dr = dr_ref[gi]
        ar = ar_ref[gi]
        dsk = dk_ref[gi]
        ag = ag_ref[gi]
        tril = lax.broadcasted_iota(jnp.int32, (L, L), 0) >= lax.broadcasted_iota(jnp.int32, (L, L), 1)
        lo = _lo_mask((L, LANES))
        nlo = jnp.logical_not(lo)
        rowlo = lax.broadcasted_iota(jnp.int32, (L, LANES), 0) < HEAD_DIM
        lane = lax.broadcasted_iota(jnp.int32, (L, LANES), 1)
        lane_row = lax.broadcasted_iota(jnp.int32, (1, LANES), 1)
        dgm = jnp.zeros((L, L), F32)
        dcm = jnp.zeros((L, SSM_STATE), F32)
        dbm = jnp.zeros((L, SSM_STATE), F32)
        cols = jnp.zeros((L, LANES), F32)
        rows_ddt, rows_q, al_all, dcd_all = [], [], [], []
        for pr in range(2):
            k0 = 2 * pr
            xcols = slice(x0 + pr * LANES, x0 + (pr + 1) * LANES)
            xp = x_ref[:, xcols]
            xpb = xp.astype(BF16)
            dyp = dy_ref[:, xcols]
            dypb = dyp.astype(BF16)
            zero = jnp.zeros_like(dypb)
            hp = hs_ref[0, gi, pr * LANES:(pr + 1) * LANES, :]
            hpb = hp.astype(BF16)
            dst = dh_ref[gi, pr * LANES:(pr + 1) * LANES, :]
            dstb = dst.astype(BF16)
            dxd, al = [], []
            for k in (k0, k0 + 1):
                sel = lo if k == k0 else nlo
                seg = ac[:, k:k + 1] - ar[k:k + 1, :]
                lam = jnp.exp(jnp.where(tril, seg, -jnp.inf))
                wk = gm * lam * dr[k:k + 1, :]
                dwk = _dot(jnp.where(sel, dypb, zero), xpb, 1, 1)
                mk = dwk * gm * lam
                qk = mk * dr[k:k + 1, :]
                dgm = dgm + dwk * lam * dr[k:k + 1, :]
                rows_ddt.append(jnp.sum(mk, axis=0, keepdims=True))
                rows_q.append(jnp.sum(qk, axis=0, keepdims=True))
                cols = jnp.where(lane == k, jnp.sum(qk, axis=-1, keepdims=True), cols)
                dxd.append(_dot(wk.astype(BF16), dypb, 0, 0))
                al.append(_last_lane(ar[k:k + 1, :]))
            al_all += al
            dxp = jnp.where(lo, dxd[0], dxd[1])
            e = jnp.exp(_pair_cols(ac, k0, lo))
            dye = dyp * e
            dyeb = dye.astype(BF16)
            dcm = dcm + _dot(dyeb, hpb, 1, 0)
            dh_yoff = _dot(dyeb, cb, 0, 0)
            tq = dye * _dot(cb, hpb, 1, 1)
            cols = jnp.where(lane == 4 + k0, jnp.sum(jnp.where(lo, tq, 0.0), axis=-1, keepdims=True), cols)
            cols = jnp.where(lane == 5 + k0, jnp.sum(jnp.where(lo, 0.0, tq), axis=-1, keepdims=True), cols)
            wp = jnp.where(lo, jnp.exp(al[0] - ac[:, k0:k0 + 1]) * dtc[:, k0:k0 + 1],
                           jnp.exp(al[1] - ac[:, k0 + 1:k0 + 2]) * dtc[:, k0 + 1:k0 + 2])
            dxw = _dot(bb, dstb, 1, 1)
            dxp = dxp + dxw * wp
            tw = xp * dxw
            cols = jnp.where(lane == 8 + k0, jnp.sum(jnp.where(lo, tw, 0.0), axis=-1, keepdims=True), cols)
            cols = jnp.where(lane == 9 + k0, jnp.sum(jnp.where(lo, 0.0, tw), axis=-1, keepdims=True), cols)
            dbm = dbm + _dot((xp * wp).astype(BF16), dstb, 1, 0)
            dsl = dsk[:, pr * LANES:(pr + 1) * LANES]
            dx_ref[:, xcols] = dxp + dsl * dyp
            dd_ref[gi, :, pr * LANES:(pr + 1) * LANES] += jnp.sum(dyp * xp, axis=0, keepdims=True)
            prod = dst * hp
            dcd_all.append(jnp.sum(jnp.sum(jnp.where(rowlo, prod, 0.0), axis=-1, keepdims=True), axis=0, keepdims=True))
            dcd_all.append(jnp.sum(jnp.sum(jnp.where(rowlo, 0.0, prod), axis=-1, keepdims=True), axis=0, keepdims=True))
            dec = jnp.where(rowlo, jnp.exp(al[0]), jnp.exp(al[1]))
            dh_ref[gi, pr * LANES:(pr + 1) * LANES, :] = dec * dst + dh_yoff
        dgb = dgm.astype(BF16)
        dc_ref[:, bcols] = dcm + _dot(dgb, bb, 1, 0)
        db_ref[:, bcols] = dbm + _dot(dgb, cb, 0, 0)
        colt = cols.T
        sub8 = lax.broadcasted_iota(jnp.int32, (8, LANES), 0)
        da_rows = jnp.zeros((8, LANES), F32)
        ddt_part = []
        for k in range(4):
            rs = colt[k:k + 1, :]
            uo = colt[4 + k:5 + k, :]
            dwl = colt[8 + k:9 + k, :]
            es = jnp.exp(al_all[k] - ar[k:k + 1, :])
            wrow = es * dr[k:k + 1, :]
            dwl_w = dwl * wrow
            da_k = rs - rows_q[k] + uo - dwl_w
            tail = jnp.sum(dwl_w, axis=-1, keepdims=True) + jnp.exp(al_all[k]) * dcd_all[k]
            da_k = da_k + jnp.where(lane_row == L - 1, tail, 0.0)
            da_rows = jnp.where(sub8 == k, da_k, da_rows)
            ddt_part.append(rows_ddt[k] + dwl * es)
        dda = _dot(da_rows, _tri(True), 1, 0, HI)
        for k in range(4):
            dda_k = dda[k:k + 1, :]
            ddt_ref[gi, k:k + 1, :] = ddt_part[k] + dda_k * ag[k:k + 1, :]
            da_ref[gi, k:k + 1, :] += dda_k * dr[k:k + 1, :] * ag[k:k + 1, :]

    return _pc(body, name, (SSM_GROUPS // n, nc),
               [sp["x"], sp["b"], sp["c"], sp["col"], sp["col"], sp["row"], sp["row"], sp["grp"], sp["grow"], sp["hs"], sp["x"]],
               [sp["x"], sp["bc"], sp["bc"], sp["row"], sp["grow"], sp["grp"]],
               [_sds((s, SSM_INNER)), _sds((s, 1024)), _sds((s, 1024)), _sds((SSM_GROUPS, 4, s)),
                _sds((SSM_GROUPS, 4, LANES)), _sds((SSM_GROUPS, 1, 256))],
               [pltpu.VMEM((n, 256, SSM_STATE), F32)])(xbc, xbc, xbc, dtc, acol, drow, arow, dskip, agrp, hs, dy)


def _gnorm_fwd(y, proj, nw, name):
    s = y.shape[0]
    ts = _row_tile(s)
    gw = SSM_INNER // SSM_GROUPS

    def body(y_ref, z_ref, w_ref, o_ref):
        for g in range(SSM_GROUPS):
            sl = slice(g * gw, (g + 1) * gw)
            z = z_ref[:, sl]
            y2 = y_ref[:, sl] * (z * _sigmoid(z))
            r = lax.rsqrt(jnp.mean(y2 * y2, axis=-1, keepdims=True) + RMS_EPS)
            o_ref[:, sl] = ((y2 * r) * w_ref[:, sl]).astype(BF16)

    row = pl.BlockSpec((ts, SSM_INNER), lambda i: (i, 0))
    return _pc(body, name, (s // ts,), [row, row, pl.BlockSpec((1, SSM_INNER), lambda i: (0, 0))], row,
               _sds((s, SSM_INNER), BF16))(y, proj, nw)


def _gnorm_bwd(y, proj, nw, dyn, name):
    s = y.shape[0]
    ts = _row_tile(s)
    gw = SSM_INNER // SSM_GROUPS

    def body(y_ref, z_ref, w_ref, d_ref, dy_ref, dz_ref, dw_ref):
        @pl.when(pl.program_id(0) == 0)
        def _():
            dw_ref[...] = jnp.zeros_like(dw_ref)

        for g in range(SSM_GROUPS):
            sl = slice(g * gw, (g + 1) * gw)
            z = z_ref[:, sl]
            yv = y_ref[:, sl]
            sg = _sigmoid(z)
            sz = z * sg
            y2 = yv * sz
            r = lax.rsqrt(jnp.mean(y2 * y2, axis=-1, keepdims=True) + RMS_EPS)
            yn = y2 * r
            dout = d_ref[:, sl]
            dyg = dout * w_ref[:, sl]
            dy2 = r * (dyg - yn * jnp.mean(dyg * yn, axis=-1, keepdims=True))
            dy_ref[:, sl] = dy2 * sz
            dz_ref[:, sl] = (dy2 * yv * (sg * (1.0 + z * (1.0 - sg)))).astype(BF16)
            dw_ref[:, sl] += jnp.sum(dout * yn, axis=0, keepdims=True)

    row = pl.BlockSpec((ts, SSM_INNER), lambda i: (i, 0))
    vec = pl.BlockSpec((1, SSM_INNER), lambda i: (0, 0))
    return _pc(body, name, (s // ts,), [row, row, vec, row], [row, row, vec],
               [_sds((s, SSM_INNER)), _sds((s, SSM_INNER), BF16), _sds((1, SSM_INNER))])(y, proj, nw, dyn)


def _head_layouts(v, s):
    return v.reshape(s, SSM_GROUPS, 4).transpose(1, 0, 2), v.T.reshape(SSM_GROUPS, 4, s)


def _ssd_fwd(x, h, w_in, conv_w, conv_b, dt_bias, a_log, d_skip, gnorm_w, w_out, next_norm):
    s = x.shape[0]
    proj = _mm(h, w_in, "nn", "ssd_in")
    xbc = _ssd_conv_fwd(proj, conv_w, conv_b, "ssd_conv")
    pad = ((0, 0), (0, LANES - SSM_HEADS))
    a_neg = -jnp.exp(a_log)
    bias = jnp.pad(dt_bias, pad)
    dt, acum = _ssd_dt_fwd(proj, bias, jnp.pad(a_neg, pad), "ssd_dt")
    dtc, drow = _head_layouts(dt[:, :SSM_HEADS], s)
    acol, arow = _head_layouts(acum[:, :SSM_HEADS], s)
    dskip = jnp.repeat(d_skip.reshape(SSM_GROUPS, 1, 4), HEAD_DIM, axis=2)
    y, hs = _ssd_scan_fwd(xbc, dtc, acol, drow, arow, dskip, "ssd_scan")
    yn = _gnorm_fwd(y, proj, gnorm_w, "ssd_gnorm")
    out, h_next = _mm(yn, w_out, "nn", "ssd_out", res=x, norm_w=next_norm)
    return out, h_next, (x, h, proj, xbc, bias, a_neg, dtc, acol, drow, arow, dskip, y, hs, yn)


def _ssd_bwd(dout, saved, norm_w, w_in, conv_w, conv_b, gnorm_w, w_out):
    x, h, proj, xbc, bias, a_neg, dtc, acol, drow, arow, dskip, y, hs, yn = saved
    s = x.shape[0]
    dyn = _mm(dout, w_out, "nt", "ssd_dyn")
    g_out = _mm(yn, dout, "tn", "ssd_gout", out_dtype=BF16)
    dy, dz, g_gnorm = _gnorm_bwd(y, proj, gnorm_w, dyn, "ssd_dgnorm")
    agrp = jnp.broadcast_to(a_neg.reshape(SSM_GROUPS, 4, 1), (SSM_GROUPS, 4, LANES))
    dxs, db, dc, ddt_row, da_acc, dd_acc = _ssd_scan_bwd(xbc, dtc, acol, drow, arow, dskip, agrp, hs, dy, "ssd_dscan")
    parts = [_ssd_conv_bwd(proj, d, col, conv_w, conv_b, f"ssd_dconv_{tag}") for d, col, tag in ((dxs, 0, "x"), (db, 2, "b"), (dc, 3, "c"))]
    g_cw = jnp.concatenate([p[1] for p in parts], axis=1)
    g_cb = jnp.concatenate([p[2] for p in parts], axis=1)
    ddt = jnp.pad(ddt_row.reshape(SSM_HEADS, s).T, ((0, 0), (0, LANES - SSM_HEADS)))
    ddtraw, g_dtb = _ssd_dt_bwd(proj, bias, ddt, "ssd_ddt")
    dproj = jnp.concatenate([dz] + [p[0] for p in parts] + [ddtraw], axis=1)
    g_in = _mm(h, dproj, "tn", "ssd_gin", out_dtype=BF16)
    dx, g_norm = _mm_dnorm_nt(dproj, w_in, "ssd_dh", x, norm_w, dout)
    g_alog = jnp.sum(da_acc, axis=-1).reshape(1, SSM_HEADS)
    g_d = jnp.sum(dd_acc.reshape(SSM_GROUPS, 4, HEAD_DIM), axis=-1).reshape(1, SSM_HEADS)
    return dx, g_norm, g_in[:, :SSM_IN], g_cw, g_cb, g_dtb[:, :SSM_HEADS], g_alog, g_d, g_gnorm, g_out


def _loss_head(y, target, name):
    s, d = y.shape
    ts = _row_tile(s)

    def body(y_ref, t_ref, dy_ref, l_ref):
        @pl.when(pl.program_id(0) == 0)
        def _():
            l_ref[...] = jnp.zeros_like(l_ref)

        e = y_ref[...] - t_ref[...]
        dy_ref[...] = e * (1.0 / d)
        part = jnp.sum(jnp.sum(e * e, axis=-1, keepdims=True), axis=0, keepdims=True) * (0.5 / d)
        l_ref[...] += jnp.broadcast_to(part, l_ref.shape)

    row = pl.BlockSpec((ts, d), lambda i: (i, 0))
    dy, lacc = _pc(body, name, (s // ts,), [row, row], [row, pl.BlockSpec((8, LANES), lambda i: (0, 0))],
                   [_sds((s, d)), _sds((8, LANES))])(y, target)
    return lacc[0, 0], dy


def _local_step(x, target, w, gather_layer0=None, gather_rest=None, scatter_first=None, scatter_layer0=None):
    saved = []
    received, received_layer0 = None, {}

    def layer0_stages():
        def entry(stage):
            shards, finish = gather_layer0[stage]

            def on_arrival(got):
                nonlocal w
                w = finish(w, got)
                return {"conv_gate": lambda: w["conv_w_out"][0], "ffn_gu": lambda: w["ffn_w_down"][0]}.get(stage, lambda: None)()
            return shards, on_arrival
        return {stage: entry(stage) for stage in gather_layer0}

    at = lambda weights, n: weights[n] if n < len(weights) else None
    h = _rms_fwd(x, w["mix_norm"][0:1], "first_norm")
    for i in range(DEPTH):
        kind, j = i % 3, i // 3
        fn = w["ffn_norm"][i:i + 1]
        stages = layer0_stages() if (i == 0 and gather_layer0 is not None) else None
        if kind == 0:
            x, h, sv = _conv_fwd(x, h, w["conv_w_in"][j], w["conv_w_dw"][j], at(w["conv_w_out"], j), str(i), fn, stages)
        elif kind == 1:
            hosted = None if gather_rest is None else (gather_rest[0], True)
            x, h, sv, got = _fox_fwd(x, h, w["fox_w_in"], w["fox_b_f"], w["fox_q_gain"], w["fox_k_gain"], w["fox_w_out"], fn, hosted)
            if gather_rest is not None:
                w = gather_rest[1](w, got)
        else:
            x, h, sv = _ssd_fwd(x, h, w["ssd_w_in"], w["ssd_conv_w"], w["ssd_conv_b"], w["ssd_dt_bias"],
                                w["ssd_a_log"], w["ssd_d"], w["ssd_norm_w"], w["ssd_w_out"], fn)
        nxt = w["mix_norm"][i + 1:i + 2] if i + 1 < DEPTH else None
        x, h, sf = _ffn_fwd(x, h, w["ffn_w_gu"][i], at(w["ffn_w_down"], i), str(i), nxt, stages)
        saved.append((sv, sf))
    loss, dx = _loss_head(x, target, "loss_head")
    g = {k: [None] * n for k, n in (("mix_norm", DEPTH), ("ffn_norm", DEPTH), ("ffn_w_gu", DEPTH), ("ffn_w_down", DEPTH),
                                    ("conv_w_in", 2), ("conv_w_dw", 2), ("conv_w_out", 2))}
    for i in reversed(range(DEPTH)):
        kind, j = i % 3, i // 3
        sv, sf = saved[i]
        hosted_fn = None
        if i == 0 and scatter_layer0 is not None:
            hosted_fn = lambda stage, **new: scatter_layer0(stage, g, **new)
        dx, g["ffn_norm"][i], g["ffn_w_gu"][i], g["ffn_w_down"][i], got = _ffn_bwd(
            dx, sf, w["ffn_norm"][i:i + 1], w["ffn_w_gu"][i], w["ffn_w_down"][i], str(i), hosted_fn)
        received_layer0.update(got)
        mn = w["mix_norm"][i:i + 1]
        if kind == 0:
            dx, g["mix_norm"][i], g["conv_w_in"][j], g["conv_w_dw"][j], g["conv_w_out"][j], got = _conv_bwd(
                dx, sv, mn, w["conv_w_in"][j], w["conv_w_dw"][j], w["conv_w_out"][j], str(i), hosted_fn)
            received_layer0.update(got)
        elif kind == 1:
            hosted = None if scatter_first is None else (scatter_first(g), False)
            (dx, g["mix_norm"][i], g["fox_w_in"], g["fox_b_f"], g["fox_q_gain"], g["fox_k_gain"],
             g["fox_w_out"], received) = _fox_bwd(dx, sv, mn, w["fox_w_in"], w["fox_w_out"], hosted)
        else:
            (dx, g["mix_norm"][i], g["ssd_w_in"], g["ssd_conv_w"], g["ssd_conv_b"], g["ssd_dt_bias"], g["ssd_a_log"],
             g["ssd_d"], g["ssd_norm_w"], g["ssd_w_out"]) = _ssd_bwd(
                 dx, sv, mn, w["ssd_w_in"], w["ssd_conv_w"], w["ssd_conv_b"], w["ssd_norm_w"], w["ssd_w_out"])
    g["mix_norm"] = jnp.concatenate(g["mix_norm"], axis=0)
    g["ffn_norm"] = jnp.concatenate(g["ffn_norm"], axis=0)
    g["conv_w_dw"] = jnp.stack(g["conv_w_dw"], axis=0)
    g["ssd_conv_w"] = g["ssd_conv_w"][None]
    return loss, dx, g, received, received_layer0


def _mesh_position():
    return lax.axis_index("x") * 4 + lax.axis_index("y") * 2 + lax.axis_index("c")


def _device_of(t):
    return (lax.shift_right_logical(t, 2), lax.bitwise_and(lax.shift_right_logical(t, 1), 1), lax.bitwise_and(t, 1))


def _exchange_copies(srcs_of, out_refs, send_sems, recv_sems, local_sems):
    me = _mesh_position()
    na = len(out_refs)
    locals_ = [pltpu.make_async_copy(srcs_of(a, me), out_refs[a].at[me], local_sems.at[a]) for a in range(na)]
    sends, arrivals = [], []
    for j in range(1, NDEV):
        t = lax.rem(me + j, NDEV)
        frm = lax.rem(me + NDEV - j, NDEV)
        for a in range(na):
            sends.append(pltpu.make_async_remote_copy(
                src_ref=srcs_of(a, t), dst_ref=out_refs[a].at[me], send_sem=send_sems.at[a, j - 1],
                recv_sem=recv_sems.at[a, j - 1], device_id=_device_of(t), device_id_type=pl.DeviceIdType.MESH))
            arrivals.append(pltpu.make_async_remote_copy(
                src_ref=srcs_of(a, me), dst_ref=out_refs[a].at[frm], send_sem=send_sems.at[a, j - 1],
                recv_sem=recv_sems.at[a, j - 1], device_id=_device_of(frm), device_id_type=pl.DeviceIdType.MESH))
    return locals_, sends, arrivals


def _exchange_start(copies):
    locals_, sends, _ = copies
    for cp in locals_ + sends:
        cp.start()


def _exchange_wait(copies):
    locals_, sends, arrivals = copies
    for cp in sends:
        cp.wait_send()
    for cp in arrivals:
        cp.wait_recv()
    for cp in locals_:
        cp.wait()


def _exchange_run(srcs_of, out_refs, send_sems, recv_sems, local_sems):
    copies = _exchange_copies(srcs_of, out_refs, send_sems, recv_sems, local_sems)
    _exchange_start(copies)
    _exchange_wait(copies)


def _exchange_parts(arrays, gather):
    na = len(arrays)
    outs = [_sds(((NDEV,) + a.shape) if gather else a.shape, a.dtype) for a in arrays]
    sems = [pltpu.SemaphoreType.DMA((na, NDEV - 1)), pltpu.SemaphoreType.DMA((na, NDEV - 1)), pltpu.SemaphoreType.DMA((na,))]
    pick = (lambda srcs: (lambda a, t: srcs[a])) if gather else (lambda srcs: (lambda a, t: srcs[a].at[t]))
    return pick, outs, sems


def _exchange(arrays, name, gather):
    na = len(arrays)
    pick, outs, sems = _exchange_parts(arrays, gather)

    def body(*refs):
        _exchange_run(pick(refs[:na]), refs[na:2 * na], *refs[2 * na:])

    hbm = pl.BlockSpec(memory_space=pl.ANY)
    return pl.pallas_call(body, name=name, in_specs=[hbm] * na, out_specs=[hbm] * na, out_shape=outs, scratch_shapes=sems)(*arrays)


def _all_sum_small(pack, name):
    def body(src_ref, out_ref, buf_ref, send_sems, recv_sems, local_sems):
        _exchange_run(lambda a, t: src_ref, [buf_ref], send_sems, recv_sems, local_sems)
        acc = buf_ref[0]
        for d in range(1, NDEV):
            acc = acc + buf_ref[d]
        out_ref[...] = acc

    vmem = pl.BlockSpec(memory_space=pltpu.VMEM)
    return pl.pallas_call(
        body, name=name, in_specs=[vmem], out_specs=vmem, out_shape=_sds(pack.shape, pack.dtype),
        scratch_shapes=[pltpu.VMEM((NDEV,) + pack.shape, pack.dtype), pltpu.SemaphoreType.DMA((1, NDEV - 1)),
                        pltpu.SemaphoreType.DMA((1, NDEV - 1)), pltpu.SemaphoreType.DMA((1,))])(pack)


def _sum_slabs(slabs, name):
    _, r, c = slabs.shape
    tr = r
    for cand in (256, 352):
        if r % cand == 0:
            tr = cand
            break

    def body(s_ref, o_ref):
        acc = s_ref[0].astype(F32)
        for d in range(1, NDEV):
            acc = acc + s_ref[d].astype(F32)
        o_ref[...] = acc

    return _pc(body, name, (r // tr,), [pl.BlockSpec((NDEV, tr, c), lambda i: (0, i, 0))],
               pl.BlockSpec((tr, c), lambda i: (i, 0)), _sds((r, c)))(slabs)


def _adamw(wt, g, m, v, name):
    shape = wt.shape
    w2, g2, m2, v2 = (a.reshape(-1, shape[-1]) for a in (wt, g, m, v))
    r, c = w2.shape
    tr = r
    for cand in (512, 352, 256):
        if r % cand == 0:
            tr = cand
            break
    c1 = 1.0 - ADAM_B1 ** ADAM_STEP
    c2 = 1.0 - ADAM_B2 ** ADAM_STEP

    def body(w_ref, g_ref, m_ref, v_ref, d_ref, mo_ref, vo_ref):
        gv = g_ref[...]
        mn = ADAM_B1 * m_ref[...] + (1.0 - ADAM_B1) * gv
        vn = ADAM_B2 * v_ref[...] + (1.0 - ADAM_B2) * (gv * gv)
        mo_ref[...] = mn
        vo_ref[...] = vn
        d_ref[...] = -ADAM_LR * ((mn / c1) / (jnp.sqrt(vn / c2) + ADAM_EPS) + ADAM_WD * w_ref[...])

    spec = pl.BlockSpec((tr, c), lambda i: (i, 0))
    outs = _pc(body, name, (r // tr,), [spec] * 4, [spec] * 3, [_sds((r, c))] * 3)(w2, g2, m2, v2)
    return tuple(o.reshape(shape) for o in outs)


_NAMES = ["mix_norm", "ffn_norm", "ffn_w_gu", "ffn_w_down", "conv_w_in", "conv_w_dw", "conv_w_out", "fox_w_in", "fox_b_f",
          "fox_q_gain", "fox_k_gain", "fox_w_out", "ssd_w_in", "ssd_conv_w", "ssd_conv_b", "ssd_dt_bias", "ssd_a_log",
          "ssd_d", "ssd_norm_w", "ssd_w_out"]
_MATRICES = ["ffn_w_gu", "ffn_w_down", "conv_w_in", "conv_w_out", "fox_w_in", "fox_w_out", "ssd_w_in", "ssd_w_out"]
_VECTORS = {"conv_w_dw": 2, "ssd_conv_w": 2, "ssd_conv_b": 1, "ssd_norm_w": 1}
_REPLICATED = ["mix_norm", "ffn_norm", "fox_b_f", "fox_q_gain", "fox_k_gain", "ssd_dt_bias", "ssd_a_log", "ssd_d"]


def _to_rows(flat):
    n = flat.shape[0]
    rows = -(-n // (8 * D_MODEL)) * 8
    return jnp.pad(flat, (0, rows * D_MODEL - n)).reshape(rows, D_MODEL)


def _full_shape(local_shape, axis):
    shp = list(local_shape)
    shp[axis] *= NDEV
    return tuple(shp)


def _cols_from_blocks(g):
    return jnp.moveaxis(g, 0, 1).reshape(g.shape[1], NDEV * g.shape[2])


def _blocks_from_cols(full):
    k, n8 = full.shape
    return jnp.moveaxis(full.reshape(k, NDEV, n8 // NDEV), 1, 0)


def kernel(x, mix_norm, ffn_norm, ffn_w_gu, ffn_w_down, conv_w_in, conv_w_dw, conv_w_out, fox_w_in, fox_b_f, fox_q_gain, fox_k_gain, fox_w_out, ssd_w_in, ssd_conv_w, ssd_conv_b, ssd_dt_bias, ssd_a_log, ssd_d, ssd_norm_w, ssd_w_out, loss_target, m_mix_norm, m_ffn_norm, m_ffn_w_gu, m_ffn_w_down, m_conv_w_in, m_conv_w_dw, m_conv_w_out, m_fox_w_in, m_fox_b_f, m_fox_q_gain, m_fox_k_gain, m_fox_w_out, m_ssd_w_in, m_ssd_conv_w, m_ssd_conv_b, m_ssd_dt_bias, m_ssd_a_log, m_ssd_d, m_ssd_norm_w, m_ssd_w_out, v_mix_norm, v_ffn_norm, v_ffn_w_gu, v_ffn_w_down, v_conv_w_in, v_conv_w_dw, v_conv_w_out, v_fox_w_in, v_fox_b_f, v_fox_q_gain, v_fox_k_gain, v_fox_w_out, v_ssd_w_in, v_ssd_conv_w, v_ssd_conv_b, v_ssd_dt_bias, v_ssd_a_log, v_ssd_d, v_ssd_norm_w, v_ssd_w_out):
    local = dict(mix_norm=mix_norm, ffn_norm=ffn_norm, ffn_w_gu=ffn_w_gu, ffn_w_down=ffn_w_down, conv_w_in=conv_w_in,
                 conv_w_dw=conv_w_dw, conv_w_out=conv_w_out, fox_w_in=fox_w_in, fox_b_f=fox_b_f, fox_q_gain=fox_q_gain,
                 fox_k_gain=fox_k_gain, fox_w_out=fox_w_out, ssd_w_in=ssd_w_in, ssd_conv_w=ssd_conv_w, ssd_conv_b=ssd_conv_b,
                 ssd_dt_bias=ssd_dt_bias, ssd_a_log=ssd_a_log, ssd_d=ssd_d, ssd_norm_w=ssd_norm_w, ssd_w_out=ssd_w_out)
    mom = dict(zip(_NAMES, [m_mix_norm, m_ffn_norm, m_ffn_w_gu, m_ffn_w_down, m_conv_w_in, m_conv_w_dw, m_conv_w_out, m_fox_w_in,
                            m_fox_b_f, m_fox_q_gain, m_fox_k_gain, m_fox_w_out, m_ssd_w_in, m_ssd_conv_w, m_ssd_conv_b,
                            m_ssd_dt_bias, m_ssd_a_log, m_ssd_d, m_ssd_norm_w, m_ssd_w_out]))
    var = dict(zip(_NAMES, [v_mix_norm, v_ffn_norm, v_ffn_w_gu, v_ffn_w_down, v_conv_w_in, v_conv_w_dw, v_conv_w_out, v_fox_w_in,
                            v_fox_b_f, v_fox_q_gain, v_fox_k_gain, v_fox_w_out, v_ssd_w_in, v_ssd_conv_w, v_ssd_conv_b,
                            v_ssd_dt_bias, v_ssd_a_log, v_ssd_d, v_ssd_norm_w, v_ssd_w_out]))

    shard = {k: local[k].astype(BF16) for k in _MATRICES}
    vec_pack = _to_rows(jnp.concatenate([local[k].reshape(-1) for k in _VECTORS]))
    first = _exchange([shard["conv_w_in"][0:1], vec_pack], "gather_first", True)
    gvec = first[1].reshape(NDEV, -1)
    full = {k: local[k] for k in _REPLICATED}
    off = 0
    for k, axis in _VECTORS.items():
        n = local[k].size
        blk = jnp.moveaxis(gvec[:, off:off + n].reshape((NDEV,) + local[k].shape), 0, axis)
        full[k] = blk.reshape(_full_shape(local[k].shape, axis))
        off += n
    full["ssd_conv_w"] = full["ssd_conv_w"][0]
    full["conv_w_in"] = [first[0][:, 0]]
    full["conv_w_out"], full["ffn_w_gu"], full["ffn_w_down"] = [], [], []

    def finish_gu0(w, got):
        return dict(w, ffn_w_gu=[got[0][:, 0]])

    def finish_out0(w, got):
        return dict(w, conv_w_out=[got[0][:, 0].reshape(D_MODEL, D_MODEL)])

    def finish_down0(w, got):
        return dict(w, ffn_w_down=[got[0][:, 0].reshape(4, FF_BLOCK, D_MODEL)], fox_w_out=got[1].reshape(D_MODEL, D_MODEL))

    def finish_fox(w, got):
        return dict(w, fox_w_in=jnp.pad(_cols_from_blocks(got[0][:, 0]), ((0, 0), (0, FOX_IN_PAD - FOX_IN))))

    layer0 = {"conv_in": ([shard["ffn_w_gu"][0:1]], finish_gu0), "conv_gate": ([shard["conv_w_out"][0:1]], finish_out0),
              "ffn_gu": ([shard["ffn_w_down"][0:1], shard["fox_w_out"]], finish_down0),
              "ffn_down": ([shard["fox_w_in"]], finish_fox)}

    later = range(1, DEPTH)
    rest = ([shard["ffn_w_gu"][i:i + 1] for i in later] + [shard["ffn_w_down"][i:i + 1] for i in later]
            + [shard["conv_w_in"][1:], shard["conv_w_out"][1:], shard["ssd_w_in"], shard["ssd_w_out"]])

    def finish(w, got):
        w = dict(w)
        n = DEPTH - 1
        w["ffn_w_gu"] = w["ffn_w_gu"] + [got[i][:, 0] for i in range(n)]
        w["ffn_w_down"] = w["ffn_w_down"] + [got[n + i][:, 0].reshape(4, FF_BLOCK, D_MODEL) for i in range(n)]
        w["conv_w_in"] = w["conv_w_in"] + [got[2 * n][:, 0]]
        w["conv_w_out"] = w["conv_w_out"] + [got[2 * n + 1][:, 0].reshape(D_MODEL, D_MODEL)]
        w["ssd_w_in"] = jnp.pad(_cols_from_blocks(got[2 * n + 2][:, 0]), ((0, 0), (0, SSM_IN_PAD - SSM_IN)))
        w["ssd_w_out"] = got[2 * n + 3].reshape(SSM_INNER, D_MODEL)
        return w

    def early_slabs(g):
        return ([g["ffn_w_gu"][i] for i in range(1, DEPTH)]
                + [g["ffn_w_down"][i].reshape(NDEV, D_FF // NDEV, D_MODEL) for i in range(1, DEPTH)]
                + [g["conv_w_in"][1], g["conv_w_out"][1].reshape(NDEV, D_MODEL // NDEV, D_MODEL),
                   _blocks_from_cols(g["ssd_w_in"]), g["ssd_w_out"].reshape(NDEV, SSM_INNER // NDEV, D_MODEL)])

    def layer0_slabs(stage, g, g_down=None, g_gu=None, g_in=None, g_out=None):
        if stage == "ffn_gdown":
            return [_blocks_from_cols(g["fox_w_in"])]
        if stage == "ffn_dgu":
            return [g_down.reshape(NDEV, D_FF // NDEV, D_MODEL), g["fox_w_out"].reshape(NDEV, D_MODEL // NDEV, D_MODEL)]
        if stage == "ffn_dh":
            return [g_gu]
        if stage == "conv_dh":
            return [g_in, g_out.reshape(NDEV, D_MODEL // NDEV, D_MODEL)]
        return None

    loss_part, dx, grads, early, late = _local_step(x[0], loss_target[0], full, layer0, (rest, finish), early_slabs, layer0_slabs)

    se = [_sum_slabs(r, f"sum_early_{n}") for n, r in enumerate(early)]
    sl = {stage: [_sum_slabs(r, f"sum_{stage}_{n}") for n, r in enumerate(rs)] for stage, rs in late.items()}
    shard_grad = {
        "ffn_w_gu": jnp.stack(sl["ffn_dh"] + se[0:3]), "ffn_w_down": jnp.stack(sl["ffn_dgu"][0:1] + se[3:6]),
        "conv_w_in": jnp.stack([sl["conv_dh"][0], se[6]]), "conv_w_out": jnp.stack([sl["conv_dh"][1], se[7]]),
        "fox_w_in": sl["ffn_gdown"][0][None], "fox_w_out": sl["ffn_dgu"][1][None],
        "ssd_w_in": se[8][None], "ssd_w_out": se[9][None]}

    small_names = _REPLICATED + list(_VECTORS)
    small = [jnp.reshape(loss_part, (1,))] + [grads[k].reshape(-1) for k in small_names]
    total = _all_sum_small(_to_rows(jnp.concatenate(small)), "sum_small").reshape(-1)
    loss = total[0]
    off = 1
    me = _mesh_position()
    for k, part in zip(small_names, small[1:]):
        gk = total[off:off + part.shape[0]]
        off += part.shape[0]
        if k in _VECTORS:
            axis = _VECTORS[k]
            shp = local[k].shape
            gfull = gk.reshape(shp[:axis] + (NDEV, shp[axis]) + shp[axis + 1:])
            shard_grad[k] = lax.dynamic_index_in_dim(gfull, me, axis, keepdims=False)
        else:
            shard_grad[k] = gk.reshape(local[k].shape)

    deltas, new_m, new_v = {}, {}, {}
    for k in _NAMES:
        deltas[k], new_m[k], new_v[k] = _adamw(local[k], shard_grad[k], mom[k], var[k], f"adamw_{k}")
    return (loss, dx[None], *[shard_grad[k] for k in _NAMES], *[deltas[k] for k in _NAMES],
            *[new_m[k] for k in _NAMES], *[new_v[k] for k in _NAMES])
```
